```python
import math
import jax
import jax.numpy as jnp
from jax import lax
import numpy as np

D_MODEL = 1024
BATCH = 32
SEQ = 256
DEPTH = 2
DEC_BATCH = 2
DEC_SEQ = 1024
PAST_LEN = 512

GRID_W = 64
N_EVEN = (DEPTH + 1) // 2
N_ODD = DEPTH // 2
EPS = 1e-6
A_WIDTH = D_MODEL // 2
A_HEADS = 4
A_DK = A_WIDTH // A_HEADS
A_DV = A_WIDTH // A_HEADS
CHUNK = 64
B_HEAD_DIM = 64
B_HEADS = (D_MODEL // 2) // B_HEAD_DIM
B_KV_HEADS = 2
Q_BLOCK = 128
ROPE_THETA = 10000.0
ROPE_PAIRS = B_HEAD_DIM // 4
AB_IN = 5 * A_WIDTH + (B_HEADS + 2 * B_KV_HEADS) * B_HEAD_DIM
AB_OUT = A_WIDTH + B_HEADS * B_HEAD_DIM
HY_BANDS = 16
HY_EMB = 1 + 2 * HY_BANDS
HY_FFN = 64
HY_SHORT = 3
HY_DECAY_TARGET = 1e-2
HY_FAST_PCT = 0.3
HY_SLOW_PCT = 1.5
N_EXPERTS = 16
N_GROUPS = 4
EXPERTS_PER_GROUP = N_EXPERTS // N_GROUPS
TOP_K = 2
D_EXPERT = 512

kernel_name = 'hybrid_diffusion_hgrn2_gqa_hyena_moe_step'


def _rms(x, g):
    xf = x.astype(jnp.float32)
    y = xf * lax.rsqrt(jnp.mean(xf * xf, axis=-1, keepdims=True) + EPS)
    return (y * g.astype(jnp.float32)).astype(x.dtype)


def _heads(a, n):
    b_, l_, _ = a.shape
    return a.reshape(b_, l_, n, -1).transpose(0, 2, 1, 3)


def hgrn_scan(q, log_f, inp, s0):
    bn, hn, L, dk = q.shape
    dv = inp.shape[-1]
    n = L // CHUNK
    k = -jnp.expm1(log_f)
    ch = lambda a: a.reshape(bn, hn, n, CHUNK, a.shape[-1])
    qc, kc, vc = ch(q), ch(k), ch(inp)
    b = jnp.cumsum(ch(log_f), axis=3)
    b_mid = b[:, :, :, CHUNK // 2:CHUNK // 2 + 1]
    att = jnp.einsum('bhnik,bhnjk->bhnij', qc * jnp.exp(b - b_mid), kc * jnp.exp(b_mid - b))
    causal = jnp.tril(jnp.ones((CHUNK, CHUNK), dtype=bool))
    o_intra = jnp.einsum('bhnij,bhnjv->bhniv', jnp.where(causal, att, 0.0), vc)
    b_last = b[:, :, :, -1:]
    chunk_state = jnp.einsum('bhnjk,bhnjv->bhnkv', kc * jnp.exp(b_last - b), vc)
    chunk_decay = jnp.exp(b_last[:, :, :, 0])

    def step(s, xs):
        dec, upd = xs
        return dec[..., None] * s + upd, s

    s_fin, s_prev = lax.scan(step, s0, (jnp.moveaxis(chunk_decay, 2, 0), jnp.moveaxis(chunk_state, 2, 0)))
    o_inter = jnp.einsum('bhnik,bhnkv->bhniv', qc * jnp.exp(b), jnp.moveaxis(s_prev, 0, 2))
    return (o_intra + o_inter).reshape(bn, hn, L, dv), s_fin


def rope_2d(x, row, col):
    inv = ROPE_THETA ** (-jnp.arange(ROPE_PAIRS, dtype=jnp.float32) / ROPE_PAIRS)

    def rot(xa, p):
        ang = p.astype(jnp.float32)[:, None] * inv
        cos = jnp.cos(ang)[None, :, None, :]
        sin = jnp.sin(ang)[None, :, None, :]
        x1, x2 = jnp.split(xa.astype(jnp.float32), 2, axis=-1)
        return jnp.concatenate([x1 * cos - x2 * sin, x2 * cos + x1 * sin], axis=-1)

    xr, xc = jnp.split(x, 2, axis=-1)
    return jnp.concatenate([rot(xr, row), rot(xc, col)], axis=-1).astype(x.dtype)


def attend(q, k, v):
    bn, hq, s, hd = q.shape
    hkv = k.shape[1]
    g = hq // hkv
    nb = s // Q_BLOCK
    qb = jnp.moveaxis(q.reshape(bn, hkv, g, nb, Q_BLOCK, hd), 3, 0)
    scale = hd ** -0.5

    def block(qi):
        sc = jnp.einsum('bhgqd,bhkd->bhgqk', qi, k).astype(jnp.float32) * scale
        p = jax.nn.softmax(sc, axis=-1).astype(v.dtype)
        return jnp.einsum('bhgqk,bhkd->bhgqd', p, v)

    o = lax.map(block, qb)
    return jnp.moveaxis(o, 0, 3).reshape(bn, hq, s, hd)


def ab_mixer(h, w_in, lb, onorm_g, qn_g, kn_g, w_out, s0, ctx_k, ctx_v, pos):
    f32 = jnp.float32
    bn, L, _ = h.shape
    sizes = [A_WIDTH] * 5 + [B_HEADS * B_HEAD_DIM, B_KV_HEADS * B_HEAD_DIM, B_KV_HEADS * B_HEAD_DIM]
    z = h @ w_in
    qa, zf, zb, ia, ga, qb, kb, vb = jnp.split(z, np.cumsum(sizes)[:-1].tolist(), axis=-1)
    q_h = _heads(qa.astype(f32), A_HEADS)
    i_h = _heads(ia.astype(f32), A_HEADS)
    lf_f = _heads(jnp.log(lb[0] + (1.0 - lb[0]) * jax.nn.sigmoid(zf.astype(f32))), A_HEADS)
    lf_b = _heads(jnp.log(lb[1] + (1.0 - lb[1]) * jax.nn.sigmoid(zb.astype(f32))), A_HEADS)
    s0 = s0.astype(f32)
    o_f, s_f = hgrn_scan(q_h, lf_f, i_h, s0[:, 0])
    o_r, s_b = hgrn_scan(q_h[:, :, ::-1], lf_b[:, :, ::-1], i_h[:, :, ::-1], s0[:, 1])
    o_a = (o_f + o_r[:, :, ::-1]).transpose(0, 2, 1, 3)
    o_a = o_a * lax.rsqrt(jnp.mean(o_a * o_a, axis=-1, keepdims=True) + EPS)
    o_a = (o_a * onorm_g.astype(f32).reshape(A_HEADS, A_DV)).reshape(bn, L, A_WIDTH)
    o_a = o_a.astype(h.dtype) * jax.nn.silu(ga)
    q = _rms(qb.reshape(bn, L, B_HEADS, B_HEAD_DIM), qn_g)
    k = _rms(kb.reshape(bn, L, B_KV_HEADS, B_HEAD_DIM), kn_g)
    v = vb.reshape(bn, L, B_KV_HEADS, B_HEAD_DIM)
    if pos is not None:
        q = rope_2d(q, pos[0], pos[1])
        k = rope_2d(k, pos[0], pos[1])
    q = q.transpose(0, 2, 1, 3)
    k = k.transpose(0, 2, 1, 3)
    v = v.transpose(0, 2, 1, 3)
    if ctx_k is None:
        keys, vals = k, v
    else:
        keys = jnp.concatenate([k, ctx_k.astype(k.dtype)], axis=2)
        vals = jnp.concatenate([v, ctx_v.astype(v.dtype)], axis=2)
    o_attn = attend(q, keys, vals).transpose(0, 2, 1, 3).reshape(bn, L, B_HEADS * B_HEAD_DIM)
    out = jnp.concatenate([o_a, o_attn], axis=-1) @ w_out
    return out, jnp.stack([s_f, s_b], axis=1), k, v


def hyena_filter(L, w1, b1, w2, b2, w3, freq):
    f32 = jnp.float32
    t = jnp.linspace(0.0, 1.0, L, dtype=f32)[:, None]
    w = 2.0 * np.pi * jnp.arange(L, dtype=f32)[:, None] / L
    fb = jnp.linspace(1e-4, HY_BANDS - 1, HY_BANDS, dtype=f32)[None, :]
    z = jnp.concatenate([t, jnp.cos(fb * w), -jnp.sin(fb * w)], axis=-1)
    fr = freq.astype(f32)
    hd = jnp.sin(fr * (z @ w1.astype(f32) + b1.astype(f32)))
    hd = jnp.sin(fr * (hd @ w2.astype(f32) + b2.astype(f32)))
    filt = (hd @ w3.astype(f32)).reshape(L, 2, D_MODEL)
    lo = math.log(HY_DECAY_TARGET) / HY_SLOW_PCT
    hi = math.log(HY_DECAY_TARGET) / HY_FAST_PCT
    deltas = jnp.abs(jnp.linspace(lo, hi, D_MODEL, dtype=f32))
    decay = jnp.exp(-t * deltas)
    return filt * decay[:, None, :]


def bidir_long_conv(u, filt):
    L = u.shape[1]
    k2 = jnp.concatenate([filt[:, 0], jnp.zeros((1, D_MODEL), jnp.float32), filt[:0:-1, 1]], axis=0)
    uf = jnp.fft.rfft(u.astype(jnp.float32), n=2 * L, axis=1)
    kf = jnp.fft.rfft(k2, n=2 * L, axis=0)
    return jnp.fft.irfft(uf * kf[None], n=2 * L, axis=1)[:, :L]


def hyena(h, in_w, in_b, conv_w, conv_b, w1, b1, w2, b2, w3, freq, dskip, out_w):
    bn, L, _ = h.shape
    u = h @ in_w + in_b
    up = jnp.pad(u, ((0, 0), (1, 1), (0, 0)))
    u = up[:, :-2] * conv_w[0] + up[:, 1:-1] * conv_w[1] + up[:, 2:] * conv_w[2] + conv_b
    x0, x1, v = jnp.split(u, 3, axis=-1)
    zz = v * x1
    filt = hyena_filter(L, w1, b1, w2, b2, w3, freq)
    y = bidir_long_conv(zz, filt).astype(h.dtype) + zz * dskip
    return (x0 * y) @ out_w


def moe(h, router_w, router_b, w_gate, w_up, w_down):
    f32 = jnp.float32
    bn, L, d = h.shape
    t = h.reshape(bn * L, d)
    scores = jax.nn.softmax((t @ router_w).astype(f32), axis=-1)
    biased = (scores + router_b.astype(f32)).reshape(-1, N_GROUPS, EXPERTS_PER_GROUP)
    group_score = lax.top_k(biased, TOP_K)[0].sum(-1)
    grp = jnp.argmax(group_score, axis=-1)
    in_grp = jnp.take_along_axis(biased, grp[:, None, None], axis=1)[:, 0]
    _, local = lax.top_k(in_grp, TOP_K)
    idx = grp[:, None] * EXPERTS_PER_GROUP + local
    wsel = jnp.take_along_axis(scores, idx, axis=-1)
    wsel = wsel / jnp.sum(wsel, axis=-1, keepdims=True)
    gates = jnp.sum(jax.nn.one_hot(idx, N_EXPERTS, dtype=f32) * wsel[..., None], axis=1)
    a = jnp.einsum('td,edf->etf', t, w_gate)
    b = jnp.einsum('td,edf->etf', t, w_up)
    hid = jax.nn.silu(a) * b * gates.T[:, :, None].astype(t.dtype)
    out = jnp.einsum('etf,efd->td', hid, w_down)
    return out.reshape(bn, L, d)


def setup_inputs(seed: int = 0) -> dict:
    key = jax.random.key(seed)
    keys = iter(jax.random.split(key, 48))
    nrm = lambda shape, scale: jax.random.normal(next(keys), shape, jnp.float32) * scale
    D = D_MODEL
    return {
        'x_prompt': nrm((BATCH, SEQ, D), 1.0),
        'x_sample': nrm((DEC_BATCH, DEC_SEQ, D), 1.0),
        'cache_k': nrm((DEC_BATCH, N_EVEN, B_KV_HEADS, PAST_LEN, B_HEAD_DIM), 1.0),
        'cache_v': nrm((DEC_BATCH, N_EVEN, B_KV_HEADS, PAST_LEN, B_HEAD_DIM), 1.0),
        'state_hgrn': nrm((DEC_BATCH, N_EVEN, 2, A_HEADS, A_DK, A_DV), 0.5),
        'c': nrm((DEC_BATCH, D), 1.0),
        'c_ctx': nrm((D,), 1.0),
        'norm_g': 1.0 + nrm((DEPTH, 2, D), 0.02),
        'mod_w': nrm((DEPTH, D, 6 * D), 0.5 * D ** -0.5),
        'mod_b': nrm((DEPTH, 6 * D), 0.02),
        'ab_in_w': nrm((N_EVEN, D, AB_IN), D ** -0.5),
        'hgrn_lb': nrm((N_EVEN + 1, 2, A_WIDTH), 0.1),
        'hgrn_onorm_g': 1.0 + nrm((N_EVEN, A_WIDTH), 0.02),
        'attn_qnorm_g': 1.0 + nrm((N_EVEN, B_HEAD_DIM), 0.02),
        'attn_knorm_g': 1.0 + nrm((N_EVEN, B_HEAD_DIM), 0.02),
        'ab_out_w': nrm((N_EVEN, AB_OUT, D), AB_OUT ** -0.5),
        'hy_in_w': nrm((N_ODD, D, 3 * D), D ** -0.5),
        'hy_in_b': nrm((N_ODD, 3 * D), 0.02),
        'hy_conv_w': nrm((N_ODD, HY_SHORT, 3 * D), HY_SHORT ** -0.5),
        'hy_conv_b': nrm((N_ODD, 3 * D), 0.02),
        'hy_f_w1': nrm((N_ODD, HY_EMB, HY_FFN), HY_EMB ** -0.5),
        'hy_f_b1': nrm((N_ODD, HY_FFN), 0.02),
        'hy_f_w2': nrm((N_ODD, HY_FFN, HY_FFN), HY_FFN ** -0.5),
        'hy_f_b2': nrm((N_ODD, HY_FFN), 0.02),
        'hy_f_w3': nrm((N_ODD, HY_FFN, 2 * D), 0.01),
        'hy_f_freq': 1.0 + nrm((N_ODD, HY_FFN), 0.02),
        'hy_dskip': nrm((N_ODD, D), 0.5),
        'hy_out_w': nrm((N_ODD, D, D), D ** -0.5),
        'router_w': nrm((D, N_EXPERTS), D ** -0.5),
        'router_b': nrm((N_EXPERTS,), 0.01),
        'moe_w_gate': nrm((DEPTH, N_EXPERTS, D, D_EXPERT), D ** -0.5),
        'moe_w_up': nrm((DEPTH, N_EXPERTS, D, D_EXPERT), D ** -0.5),
        'moe_w_down': nrm((DEPTH, N_EXPERTS, D_EXPERT, D), D_EXPERT ** -0.5),
    }


def reference(x_prompt, x_sample, cache_k, cache_v, state_hgrn, c, c_ctx, norm_g, mod_w, mod_b,
              ab_in_w, hgrn_lb, hgrn_onorm_g, attn_qnorm_g, attn_knorm_g, ab_out_w,
              hy_in_w, hy_in_b, hy_conv_w, hy_conv_b, hy_f_w1, hy_f_b1, hy_f_w2, hy_f_b2,
              hy_f_w3, hy_f_freq, hy_dskip, hy_out_w, router_w, router_b,
              moe_w_gate, moe_w_up, moe_w_down):
    f32 = jnp.float32
    lb_all = jnp.cumsum(jax.nn.softmax(hgrn_lb.astype(f32), axis=0), axis=0)

    def trunk(x, cond, latent):
        bn, L, _ = x.shape
        if latent:
            rows = L // GRID_W
            pos = (jnp.repeat(jnp.arange(rows), GRID_W), jnp.tile(jnp.arange(GRID_W), rows))
        else:
            pos = None
        ks, vs, ss = [], [], []
        for l in range(DEPTH):
            j = l // 2
            mod = (jax.nn.silu(cond) @ mod_w[l] + mod_b[l])[:, None, :]
            sh1, sc1, g1, sh2, sc2, g2 = jnp.split(mod, 6, axis=-1)
            h = _rms(x, norm_g[l, 0]) * (1.0 + sc1) + sh1
            if l % 2 == 0:
                if latent:
                    s0, ck, cv = state_hgrn[:, j], cache_k[:, j], cache_v[:, j]
                else:
                    s0, ck, cv = jnp.zeros((bn, 2, A_HEADS, A_DK, A_DV), f32), None, None
                out, s_fin, k_c, v_c = ab_mixer(h, ab_in_w[j], lb_all[j], hgrn_onorm_g[j],
                                                attn_qnorm_g[j], attn_knorm_g[j], ab_out_w[j],
                                                s0, ck, cv, pos)
                if not latent:
                    ks.append(k_c)
                    vs.append(v_c)
                    ss.append(s_fin)
            else:
                out = hyena(h, hy_in_w[j], hy_in_b[j], hy_conv_w[j], hy_conv_b[j], hy_f_w1[j],
                            hy_f_b1[j], hy_f_w2[j], hy_f_b2[j], hy_f_w3[j], hy_f_freq[j],
                            hy_dskip[j], hy_out_w[j])
            x = x + g1 * out
            h = _rms(x, norm_g[l, 1]) * (1.0 + sc2) + sh2
            x = x + g2 * moe(h, router_w, router_b, moe_w_gate[l], moe_w_up[l], moe_w_down[l])
        return x, ks, vs, ss

    y_prompt, ks, vs, ss = trunk(x_prompt, c_ctx[None, :], False)
    y_sample, _, _, _ = trunk(x_sample, c, True)
    new_cache_k = jnp.stack(ks, axis=1)
    new_cache_v = jnp.stack(vs, axis=1)
    new_state_hgrn = jnp.stack(ss, axis=1).astype(x_prompt.dtype)
    return (y_prompt, y_sample, new_cache_k, new_cache_v, new_state_hgrn)
```

```python
import functools
import math

import numpy as np
import jax
import jax.numpy as jnp
from jax import lax
from jax.experimental import pallas as pl
from jax.experimental.pallas import tpu as pltpu

F32 = jnp.float32
BF16 = jnp.bfloat16
HIGHEST = lax.Precision.HIGHEST

D_MODEL = 1024
N_PROMPT_SEQ = 32
PROMPT_LEN = 256
N_LATENT_SEQ = 2
LATENT_LEN = 1024
PAST_LEN = 512
GRID_W = 64
N_PROMPT_TOK = N_PROMPT_SEQ * PROMPT_LEN
N_LATENT_TOK = N_LATENT_SEQ * LATENT_LEN
N_TOK = N_PROMPT_TOK + N_LATENT_TOK
N_COND = 8
EPS = 1e-6

A_WIDTH = 512
A_HEADS = 4
A_DK = 128
CHUNK = 64
HEAD_DIM = 64
Q_HEADS = 8
KV_HEADS = 2
Q_PER_KV = Q_HEADS // KV_HEADS
Q_BLOCK = 256
ROPE_THETA = 10000.0
ROPE_PAIRS = HEAD_DIM // 4
AB_IN = 5 * A_WIDTH + (Q_HEADS + 2 * KV_HEADS) * HEAD_DIM

HY_BANDS = 16
HY_FFN = 64
HY_DECAY_TARGET = 1e-2
HY_FAST_PCT = 0.3
HY_SLOW_PCT = 1.5

N_EXPERTS = 16
N_GROUPS = 4
EXPERTS_PER_GROUP = 4
D_EXPERT = 512

VMEM_LIMIT = 56 * 1024 * 1024


def _params(*sem):
    return pltpu.CompilerParams(dimension_semantics=sem, vmem_limit_bytes=VMEM_LIMIT)


def _cond_of_token_block(i, block_rows):
    start = i * block_rows
    return jnp.where(start < N_PROMPT_TOK, 0, 1 + (start - N_PROMPT_TOK) // LATENT_LEN)


def _mod_kernel(cond_ref, w_ref, b_ref, o_ref):
    cnd = cond_ref[...]
    s = cnd * jax.nn.sigmoid(cnd)
    o_ref[...] = jnp.dot(s, w_ref[...], precision=HIGHEST, preferred_element_type=F32) + b_ref[...]


def _modulation(cond, mod_w, mod_b):
    depth = mod_w.shape[0]
    n_chunk = 6
    out = pl.pallas_call(
        _mod_kernel,
        grid=(depth, n_chunk),
        in_specs=[
            pl.BlockSpec((N_COND, D_MODEL), lambda l, j: (0, 0)),
            pl.BlockSpec((None, D_MODEL, D_MODEL), lambda l, j: (l, 0, j)),
            pl.BlockSpec((None, 1, D_MODEL), lambda l, j: (l, 0, j)),
        ],
        out_specs=pl.BlockSpec((None, N_COND, D_MODEL), lambda l, j: (l, 0, j)),
        out_shape=jax.ShapeDtypeStruct((depth, N_COND, n_chunk * D_MODEL), F32),
        compiler_params=_params("arbitrary", "arbitrary"),
        name="modulation",
    )(cond, mod_w, mod_b.reshape(depth, 1, n_chunk * D_MODEL))
    return out.reshape(depth, N_COND, n_chunk, D_MODEL)


def _modulated_norm(x, g, mod, shift_row, scale_row):
    ms = jnp.mean(x * x, axis=-1, keepdims=True)
    y = x * lax.rsqrt(ms + EPS) * g
    return y * (1.0 + mod[scale_row:scale_row + 1, :]) + mod[shift_row:shift_row + 1, :]


def _in_proj_kernel(x_ref, g_ref, mod_ref, w_ref, b_ref, o_ref):
    h = _modulated_norm(x_ref[...], g_ref[...], mod_ref[...], 0, 1)
    o_ref[...] = jnp.dot(h.astype(BF16), w_ref[...], preferred_element_type=F32) + b_ref[...]


def _in_proj(x, g, mod_l, w_bf16, bias, block_rows=256):
    n = w_bf16.shape[1]
    return pl.pallas_call(
        _in_proj_kernel,
        grid=(N_TOK // block_rows,),
        in_specs=[
            pl.BlockSpec((block_rows, D_MODEL), lambda i: (i, 0)),
            pl.BlockSpec((1, D_MODEL), lambda i: (0, 0)),
            pl.BlockSpec((None, 6, D_MODEL), lambda i: (_cond_of_token_block(i, block_rows), 0, 0)),
            pl.BlockSpec((D_MODEL, n), lambda i: (0, 0)),
            pl.BlockSpec((1, n), lambda i: (0, 0)),
        ],
        out_specs=pl.BlockSpec((block_rows, n), lambda i: (i, 0)),
        out_shape=jax.ShapeDtypeStruct((N_TOK, n), F32),
        compiler_params=_params("arbitrary"),
        name="in_proj",
    )(x, g.reshape(1, D_MODEL), mod_l, w_bf16, bias.reshape(1, n))


def _hgrn_kernel(*refs, seq_len, with_state):
    if with_state:
        (q_ref, zf_ref, zb_ref, i_ref, ga_ref, lb_ref, og_ref, s0_ref, o_ref, of_ref, ob_ref) = refs
    else:
        (q_ref, zf_ref, zb_ref, i_ref, ga_ref, lb_ref, og_ref, o_ref, s_ref, of_ref, ob_ref) = refs
    n_chunks = seq_len // CHUNK

    lbr = lb_ref[...]
    mx = jnp.maximum(lbr[0], lbr[1])
    e0 = jnp.exp(lbr[0] - mx)
    e1 = jnp.exp(lbr[1] - mx)
    lb = e0 / (e0 + e1)

    row = lax.broadcasted_iota(jnp.int32, (CHUNK, CHUNK), 0)
    col = lax.broadcasted_iota(jnp.int32, (CHUNK, CHUNK), 1)
    lower = col <= row
    tri_lo = jnp.where(lower, 1.0, 0.0).astype(F32)
    tri_up = jnp.where(col >= row, 1.0, 0.0).astype(F32)

    def direction(c, st, z_ref, lbd, tri, keep, mid, last, out_ref):
        r0 = pl.multiple_of(c * CHUNK, CHUNK)
        rows = pl.ds(r0, CHUNK)
        f = lbd + (1.0 - lbd) * jax.nn.sigmoid(z_ref[rows, :])
        lf = jnp.log(f)
        k = 1.0 - f
        q = q_ref[rows, :]
        v = i_ref[rows, :]
        b = jnp.dot(tri, lf, precision=HIGHEST, preferred_element_type=F32)
        bm = b[mid:mid + 1, :]
        bl = b[last:last + 1, :]
        qe = (q * jnp.exp(b - bm)).astype(BF16)
        ke = (k * jnp.exp(bm - b)).astype(BF16)
        att = lax.dot_general(qe, ke, (((1,), (1,)), ((), ())), preferred_element_type=F32)
        att = jnp.where(keep, att, 0.0)
        vb = v.astype(BF16)
        o = jnp.dot(att.astype(BF16), vb, preferred_element_type=F32)
        qb = (q * jnp.exp(b)).astype(BF16)
        o = o + lax.dot_general(qb, st.astype(BF16), (((1,), (1,)), ((), ())), preferred_element_type=F32)
        out_ref[rows, :] = o
        ks = (k * jnp.exp(bl - b)).astype(BF16)
        upd = lax.dot_general(vb, ks, (((0,), (0,)), ((), ())), preferred_element_type=F32)
        return st * jnp.exp(bl) + upd

    def body(c, carry):
        st_f, st_b = carry
        st_f = direction(c, st_f, zf_ref, lb[0:1, :], tri_lo, lower, CHUNK // 2, CHUNK - 1, of_ref)
        st_b = direction(n_chunks - 1 - c, st_b, zb_ref, lb[1:2, :], tri_up, col >= row,
                         CHUNK - 1 - CHUNK // 2, 0, ob_ref)
        return st_f, st_b

    if with_state:
        init = (s0_ref[0].T, s0_ref[1].T)
    else:
        init = (jnp.zeros((A_DK, A_DK), F32), jnp.zeros((A_DK, A_DK), F32))
    st_f, st_b = lax.fori_loop(0, n_chunks, body, init)
    if not with_state:
        s_ref[0] = st_f.T
        s_ref[1] = st_b.T

    o = of_ref[...] + ob_ref[...]
    o = o * lax.rsqrt(jnp.mean(o * o, axis=-1, keepdims=True) + EPS) * og_ref[...]
    ga = ga_ref[...]
    o_ref[...] = (o * (ga * jax.nn.sigmoid(ga))).astype(o_ref.dtype)


def _hgrn(z, hgrn_lb, onorm_g, state, *, latent):
    seq_len = LATENT_LEN if latent else PROMPT_LEN
    n_seq = N_LATENT_SEQ if latent else N_PROMPT_SEQ
    row0 = (N_PROMPT_TOK // seq_len) if latent else 0

    def zspec(col0):
        return pl.BlockSpec((seq_len, A_DK), lambda s, h: (row0 + s, col0 + h))

    in_specs = [zspec(0), zspec(4), zspec(8), zspec(12), zspec(16),
                pl.BlockSpec((2, 2, A_DK), lambda s, h: (0, 0, h)),
                pl.BlockSpec((1, A_DK), lambda s, h: (0, h))]
    args = [z, z, z, z, z, hgrn_lb, onorm_g.reshape(1, A_WIDTH)]
    state_spec = pl.BlockSpec((None, None, 2, None, A_DK, A_DK), lambda s, h: (s, 0, 0, h, 0, 0))
    o_shape = jax.ShapeDtypeStruct((n_seq * seq_len, A_WIDTH), BF16)
    o_spec = pl.BlockSpec((seq_len, A_DK), lambda s, h: (s, h))
    if latent:
        in_specs.append(state_spec)
        args.append(state)
        out_shape, out_specs = o_shape, o_spec
    else:
        out_shape = (o_shape, jax.ShapeDtypeStruct((n_seq, 1, 2, A_HEADS, A_DK, A_DK), F32))
        out_specs = (o_spec, state_spec)
    return pl.pallas_call(
        functools.partial(_hgrn_kernel, seq_len=seq_len, with_state=latent),
        grid=(n_seq, A_HEADS),
        in_specs=in_specs,
        out_specs=out_specs,
        out_shape=out_shape,
        scratch_shapes=[pltpu.VMEM((seq_len, A_DK), F32), pltpu.VMEM((seq_len, A_DK), F32)],
        compiler_params=_params("arbitrary", "arbitrary"),
        name="hgrn_latent" if latent else "hgrn_prompt",
    )(*args)


def _rope_tables():
    pos = np.arange(LATENT_LEN)
    row, colp = pos // GRID_W, pos % GRID_W
    inv = ROPE_THETA ** (-np.arange(ROPE_PAIRS, dtype=np.float32) / ROPE_PAIRS)
    inv = inv.astype(np.float32)
    ang_r = (row.astype(np.float32)[:, None] * inv).astype(np.float32)
    ang_c = (colp.astype(np.float32)[:, None] * inv).astype(np.float32)
    cos = np.concatenate([np.cos(ang_r), np.cos(ang_r), np.cos(ang_c), np.cos(ang_c)], axis=1)
    sin = np.concatenate([-np.sin(ang_r), np.sin(ang_r), -np.sin(ang_c), np.sin(ang_c)], axis=1)
    perm = np.zeros((HEAD_DIM, HEAD_DIM), np.float32)
    for d in range(HEAD_DIM):
        partner = d + ROPE_PAIRS if (d // ROPE_PAIRS) % 2 == 0 else d - ROPE_PAIRS
        perm[partner, d] = 1.0
    return cos.astype(np.float32), sin.astype(np.float32), perm


def _attn_kernel(*refs, latent):
    if latent:
        (q_ref, k_ref, v_ref, qg_ref, kg_ref, cosq_ref, sinq_ref, cosk_ref, sink_ref, perm_ref,
         ck_ref, cv_ref, o_ref) = refs
    else:
        (q_ref, k_ref, v_ref, qg_ref, kg_ref, o_ref, kout_ref) = refs

    def head(x_ref, h, g, cos, sin):
        xh = x_ref[:, h * HEAD_DIM:(h + 1) * HEAD_DIM]
        xh = xh * lax.rsqrt(jnp.mean(xh * xh, axis=-1, keepdims=True) + EPS) * g
        if latent:
            swapped = jnp.dot(xh, perm_ref[...], precision=HIGHEST, preferred_element_type=F32)
            xh = xh * cos + swapped * sin
        return xh

    qg = qg_ref[...]
    kg = kg_ref[...]
    cq = sq = ck = sk = None
    if latent:
        cq, sq, ck, sk = cosq_ref[...], sinq_ref[...], cosk_ref[...], sink_ref[...]
    scale = HEAD_DIM ** -0.5
    n_q = q_ref.shape[0]
    for j in range(KV_HEADS):
        kh = head(k_ref, j, kg, ck, sk)
        if not latent:
            kout_ref[:, j * HEAD_DIM:(j + 1) * HEAD_DIM] = kh
        vh = v_ref[:, j * HEAD_DIM:(j + 1) * HEAD_DIM]
        qs = jnp.concatenate(
            [head(q_ref, j * Q_PER_KV + t, qg, cq, sq) * scale for t in range(Q_PER_KV)], axis=0)
        qs = qs.astype(BF16)
        nt = (((1,), (1,)), ((), ()))
        s_new = lax.dot_general(qs, kh.astype(BF16), nt, preferred_element_type=F32)
        m = jnp.max(s_new, axis=-1, keepdims=True)
        if latent:
            s_old = lax.dot_general(qs, ck_ref[j].astype(BF16), nt, preferred_element_type=F32)
            m = jnp.maximum(m, jnp.max(s_old, axis=-1, keepdims=True))
        p_new = jnp.exp(s_new - m)
        den = jnp.sum(p_new, axis=-1, keepdims=True)
        acc = jnp.dot(p_new.astype(BF16), vh.astype(BF16), preferred_element_type=F32)
        if latent:
            p_old = jnp.exp(s_old - m)
            den = den + jnp.sum(p_old, axis=-1, keepdims=True)
            acc = acc + jnp.dot(p_old.astype(BF16), cv_ref[j].astype(BF16), preferred_element_type=F32)
        out = acc / den
        for t in range(Q_PER_KV):
            hq = j * Q_PER_KV + t
            o_ref[:, hq * HEAD_DIM:(hq + 1) * HEAD_DIM] = out[t * n_q:(t + 1) * n_q, :].astype(o_ref.dtype)


def _attention_prompt(z, qn_g, kn_g):
    L = PROMPT_LEN
    q_col = (5 * A_WIDTH) // (Q_HEADS * HEAD_DIM)
    k_col = (5 * A_WIDTH + Q_HEADS * HEAD_DIM) // (KV_HEADS * HEAD_DIM)
    kv_w = KV_HEADS * HEAD_DIM
    return pl.pallas_call(
        functools.partial(_attn_kernel, latent=False),
        grid=(N_PROMPT_SEQ,),
        in_specs=[
            pl.BlockSpec((L, Q_HEADS * HEAD_DIM), lambda s: (s, q_col)),
            pl.BlockSpec((L, kv_w), lambda s: (s, k_col)),
            pl.BlockSpec((L, kv_w), lambda s: (s, k_col + 1)),
            pl.BlockSpec((1, HEAD_DIM), lambda s: (0, 0)),
            pl.BlockSpec((1, HEAD_DIM), lambda s: (0, 0)),
        ],
        out_specs=(pl.BlockSpec((L, Q_HEADS * HEAD_DIM), lambda s: (s, 0)),
                   pl.BlockSpec((L, kv_w), lambda s: (s, 0))),
        out_shape=(jax.ShapeDtypeStruct((N_PROMPT_TOK, Q_HEADS * HEAD_DIM), BF16),
                   jax.ShapeDtypeStruct((N_PROMPT_TOK, kv_w), F32)),
        compiler_params=_params("arbitrary"),
        name="attn_prompt",
    )(z, z, z, qn_g.reshape(1, HEAD_DIM), kn_g.reshape(1, HEAD_DIM))


def _attention_latent(z, qn_g, kn_g, cache_k, cache_v):
    L = LATENT_LEN
    nqb = L // Q_BLOCK
    q_col = (5 * A_WIDTH) // (Q_HEADS * HEAD_DIM)
    k_col = (5 * A_WIDTH + Q_HEADS * HEAD_DIM) // (KV_HEADS * HEAD_DIM)
    kv_w = KV_HEADS * HEAD_DIM
    qrow0 = N_PROMPT_TOK // Q_BLOCK
    krow0 = N_PROMPT_TOK // L
    cos, sin, perm = _rope_tables()
    cache_spec = pl.BlockSpec((None, None, KV_HEADS, PAST_LEN, HEAD_DIM), lambda s, b: (s, 0, 0, 0, 0))
    return pl.pallas_call(
        functools.partial(_attn_kernel, latent=True),
        grid=(N_LATENT_SEQ, nqb),
        in_specs=[
            pl.BlockSpec((Q_BLOCK, Q_HEADS * HEAD_DIM), lambda s, b: (qrow0 + s * nqb + b, q_col)),
            pl.BlockSpec((L, kv_w), lambda s, b: (krow0 + s, k_col)),
            pl.BlockSpec((L, kv_w), lambda s, b: (krow0 + s, k_col + 1)),
            pl.BlockSpec((1, HEAD_DIM), lambda s, b: (0, 0)),
            pl.BlockSpec((1, HEAD_DIM), lambda s, b: (0, 0)),
            pl.BlockSpec((Q_BLOCK, HEAD_DIM), lambda s, b: (b, 0)),
            pl.BlockSpec((Q_BLOCK, HEAD_DIM), lambda s, b: (b, 0)),
            pl.BlockSpec((L, HEAD_DIM), lambda s, b: (0, 0)),
            pl.BlockSpec((L, HEAD_DIM), lambda s, b: (0, 0)),
            pl.BlockSpec((HEAD_DIM, HEAD_DIM), lambda s, b: (0, 0)),
            cache_spec, cache_spec,
        ],
        out_specs=pl.BlockSpec((Q_BLOCK, Q_HEADS * HEAD_DIM), lambda s, b: (s * nqb + b, 0)),
        out_shape=jax.ShapeDtypeStruct((N_LATENT_TOK, Q_HEADS * HEAD_DIM), BF16),
        compiler_params=_params("arbitrary", "arbitrary"),
        name="attn_latent",
    )(z, z, z, qn_g.reshape(1, HEAD_DIM), kn_g.reshape(1, HEAD_DIM),
      jnp.asarray(cos), jnp.asarray(sin), jnp.asarray(cos), jnp.asarray(sin), jnp.asarray(perm),
      cache_k, cache_v)


def _out_proj_kernel(*refs, n_in):
    a_refs = refs[:n_in]
    w_refs = refs[n_in:2 * n_in]
    x_ref, g_ref, mod_ref, rw_ref, xo_ref, h_ref, lg_ref = refs[2 * n_in:]
    acc = jnp.dot(a_refs[0][...], w_refs[0][...], preferred_element_type=F32)
    for a_ref, w_ref in zip(a_refs[1:], w_refs[1:]):
        acc = acc + jnp.dot(a_ref[...], w_ref[...], preferred_element_type=F32)
    mod = mod_ref[...]
    x = x_ref[...] + mod[2:3, :] * acc
    xo_ref[...] = x
    h = _modulated_norm(x, g_ref[...], mod, 3, 4)
    h_ref[...] = h.astype(BF16)
    lg_ref[...] = lax.dot_general(rw_ref[...], h, (((1,), (1,)), ((), ())), precision=HIGHEST,
                                  preferred_element_type=F32)


def _out_proj(acts, weights, x, g, mod_l, router_wt, block_rows=256):
    n_in = len(acts)
    in_specs = [pl.BlockSpec((block_rows, a.shape[1]), lambda i: (i, 0)) for a in acts]
    in_specs += [pl.BlockSpec(w.shape, lambda i: (0, 0)) for w in weights]
    in_specs += [
        pl.BlockSpec((block_rows, D_MODEL), lambda i: (i, 0)),
        pl.BlockSpec((1, D_MODEL), lambda i: (0, 0)),
        pl.BlockSpec((None, 6, D_MODEL), lambda i: (_cond_of_token_block(i, block_rows), 0, 0)),
        pl.BlockSpec((N_EXPERTS, D_MODEL), lambda i: (0, 0)),
    ]
    return pl.pallas_call(
        functools.partial(_out_proj_kernel, n_in=n_in),
        grid=(N_TOK // block_rows,),
        in_specs=in_specs,
        out_specs=(pl.BlockSpec((block_rows, D_MODEL), lambda i: (i, 0)),
                   pl.BlockSpec((block_rows, D_MODEL), lambda i: (i, 0)),
                   pl.BlockSpec((N_EXPERTS, block_rows), lambda i: (0, i))),
        out_shape=(jax.ShapeDtypeStruct((N_TOK, D_MODEL), F32),
                   jax.ShapeDtypeStruct((N_TOK, D_MODEL), BF16),
                   jax.ShapeDtypeStruct((N_EXPERTS, N_TOK), F32)),
        compiler_params=_params("arbitrary"),
        name="out_proj",
    )(*acts, *weights, x, g.reshape(1, D_MODEL), mod_l, router_wt)


def _router_kernel(lg_ref, rb_ref, o_ref):
    lg = lg_ref[...]
    ex = jnp.exp(lg - jnp.max(lg, axis=0, keepdims=True))
    scores = ex / jnp.sum(ex, axis=0, keepdims=True)
    biased = scores + rb_ref[...]
    rows = [biased[e:e + 1, :] for e in range(N_EXPERTS)]
    selected = []
    group_score = []
    for gi in range(N_GROUPS):
        r = rows[gi * EXPERTS_PER_GROUP:(gi + 1) * EXPERTS_PER_GROUP]
        total = None
        for i in range(EXPERTS_PER_GROUP):
            rank = None
            for j in range(EXPERTS_PER_GROUP):
                if j == i:
                    continue
                ahead = (r[j] > r[i]) if j > i else (r[j] >= r[i])
                ahead = jnp.where(ahead, 1.0, 0.0)
                rank = ahead if rank is None else rank + ahead
            sel = rank < 1.5
            selected.append(sel)
            contrib = jnp.where(sel, r[i], 0.0)
            total = contrib if total is None else total + contrib
        group_score.append(total)
    best = group_score[0]
    best_group = jnp.zeros_like(best)
    for gi in range(1, N_GROUPS):
        better = group_score[gi] > best
        best_group = jnp.where(better, float(gi), best_group)
        best = jnp.where(better, group_score[gi], best)
    picked = []
    den = None
    for e in range(N_EXPERTS):
        in_group = best_group == float(e // EXPERTS_PER_GROUP)
        w = jnp.where(selected[e], jnp.where(in_group, scores[e:e + 1, :], 0.0), 0.0)
        picked.append(w)
        den = w if den is None else den + w
    for e in range(N_EXPERTS):
        o_ref[e:e + 1, :] = picked[e] / den


def _router(logits_t, router_b):
    return pl.pallas_call(
        _router_kernel,
        grid=(1,),
        in_specs=[pl.BlockSpec((N_EXPERTS, N_TOK), lambda i: (0, 0)),
                  pl.BlockSpec((N_EXPERTS, 1), lambda i: (0, 0))],
        out_specs=pl.BlockSpec((N_EXPERTS, N_TOK), lambda i: (0, 0)),
        out_shape=jax.ShapeDtypeStruct((N_EXPERTS, N_TOK), F32),
        compiler_params=_params("arbitrary"),
        name="router",
    )(logits_t, router_b.reshape(N_EXPERTS, 1))


def _moe_kernel(h_ref, gates_ref, wg_ref, wu_ref, wd_ref, x_ref, mod_ref, o_ref, acc_ref):
    e = pl.program_id(1)

    @pl.when(e == 0)
    def _():
        acc_ref[...] = jnp.zeros_like(acc_ref)

    h = h_ref[...]
    a = jnp.dot(h, wg_ref[...].astype(BF16), preferred_element_type=F32)
    b = jnp.dot(h, wu_ref[...].astype(BF16), preferred_element_type=F32)
    gates = gates_ref[...]
    lane = lax.broadcasted_iota(jnp.int32, gates.shape, 1)
    gate = jnp.sum(jnp.where(lane == e, gates, 0.0), axis=-1, keepdims=True)
    hid = (a * jax.nn.sigmoid(a)) * b * gate
    acc_ref[...] += jnp.dot(hid.astype(BF16), wd_ref[...].astype(BF16), preferred_element_type=F32)

    @pl.when(e == N_EXPERTS - 1)
    def _():
        o_ref[...] = x_ref[...] + mod_ref[5:6, :] * acc_ref[...]


def _moe(h, gates, w_gate, w_up, w_down, layer, x, mod_l, block_rows=512):
    return pl.pallas_call(
        _moe_kernel,
        grid=(N_TOK // block_rows, N_EXPERTS),
        in_specs=[
            pl.BlockSpec((block_rows, D_MODEL), lambda i, e: (i, 0)),
            pl.BlockSpec((block_rows, N_EXPERTS), lambda i, e: (i, 0)),
            pl.BlockSpec((None, None, D_MODEL, D_EXPERT), lambda i, e: (layer, e, 0, 0)),
            pl.BlockSpec((None, None, D_MODEL, D_EXPERT), lambda i, e: (layer, e, 0, 0)),
            pl.BlockSpec((None, None, D_EXPERT, D_MODEL), lambda i, e: (layer, e, 0, 0)),
            pl.BlockSpec((block_rows, D_MODEL), lambda i, e: (i, 0)),
            pl.BlockSpec((None, 6, D_MODEL), lambda i, e: (_cond_of_token_block(i, block_rows), 0, 0)),
        ],
        out_specs=pl.BlockSpec((block_rows, D_MODEL), lambda i, e: (i, 0)),
        out_shape=jax.ShapeDtypeStruct((N_TOK, D_MODEL), F32),
        scratch_shapes=[pltpu.VMEM((block_rows, D_MODEL), F32)],
        compiler_params=_params("arbitrary", "arbitrary"),
        name="moe",
    )(h, gates, w_gate, w_up, w_down, x, mod_l)


def _dft_tables(L):
    k = np.arange(L)[:, None]
    m = np.arange(L)[None, :]
    r = (k * m) % (2 * L)
    ang = np.pi * r.astype(np.float64) / L
    fc = np.cos(ang)
    fs = np.sin(ang)
    fs[0, :] = np.where(np.arange(L) % 2 == 0, 1.0, -1.0)
    wk = np.full((L, 1), 1.0 / L)
    wk[0, 0] = 0.5 / L
    gc = (fc * wk).T
    gs = (fs * wk).T
    return [jnp.asarray(t.astype(np.float32)).astype(BF16) for t in (fc, fs, gc, gs)]


def _filter_consts(L):
    t = np.linspace(0.0, 1.0, L, dtype=np.float32)[:, None]
    w = (np.float32(2.0 * np.pi) * np.arange(L, dtype=np.float32)[:, None] / np.float32(L)).astype(np.float32)
    fb = np.linspace(1e-4, HY_BANDS - 1, HY_BANDS, dtype=np.float32)[None, :]
    emb = np.concatenate([t, np.cos(fb * w), -np.sin(fb * w)], axis=-1).astype(np.float32)
    lo = math.log(HY_DECAY_TARGET) / HY_SLOW_PCT
    hi = math.log(HY_DECAY_TARGET) / HY_FAST_PCT
    deltas = np.abs(np.linspace(lo, hi, D_MODEL, dtype=np.float32))
    decay = np.exp(-t * deltas).astype(np.float32)
    return jnp.asarray(emb), jnp.asarray(decay)


def _filter_kernel(emb_ref, w1_ref, b1_ref, w2_ref, b2_ref, fr_ref, w3f_ref, w3b_ref, dec_ref,
                   fc_ref, fs_ref, kr_ref, q_ref, krn_ref):
    fr = fr_ref[...]
    hd = jnp.sin(fr * (jnp.dot(emb_ref[...], w1_ref[...], precision=HIGHEST,
                               preferred_element_type=F32) + b1_ref[...]))
    hd = jnp.sin(fr * (jnp.dot(hd, w2_ref[...], precision=HIGHEST,
                               preferred_element_type=F32) + b2_ref[...]))
    dec = dec_ref[...]
    f = jnp.dot(hd, w3f_ref[...], precision=HIGHEST, preferred_element_type=F32) * dec
    g = jnp.dot(hd, w3b_ref[...], precision=HIGHEST, preferred_element_type=F32) * dec
    row = lax.broadcasted_iota(jnp.int32, f.shape, 0)
    g = jnp.where(row == 0, 0.0, g)
    s = f + g
    d = f - g
    kr = jnp.dot(fc_ref[...], s.astype(BF16), preferred_element_type=F32)
    qq = jnp.dot(fs_ref[...], d.astype(BF16), preferred_element_type=F32)
    alt = jnp.where(row % 2 == 0, 1.0, -1.0)
    nyq = jnp.sum(alt * s, axis=0, keepdims=True)
    kr_ref[...] = kr
    q_ref[...] = jnp.where(row == 0, 0.0, qq)
    krn_ref[...] = jnp.where(row == 0, nyq, kr)


def _hyena_filter_spectrum(L, w1, b1, w2, b2, w3, freq, fc, fs, cblk=256):
    emb, decay = _filter_consts(L)
    ncb = D_MODEL // cblk
    n_emb = 128
    emb = jnp.pad(emb, ((0, 0), (0, n_emb - emb.shape[1])))
    w1 = jnp.pad(w1, ((0, n_emb - w1.shape[0]), (0, 0)))
    full = lambda shape: pl.BlockSpec(shape, lambda j: tuple(0 for _ in shape))
    out_sds = jax.ShapeDtypeStruct((L, D_MODEL), F32)
    out_spec = pl.BlockSpec((L, cblk), lambda j: (0, j))
    return pl.pallas_call(
        _filter_kernel,
        grid=(ncb,),
        in_specs=[
            full((L, n_emb)), full((n_emb, HY_FFN)), full((1, HY_FFN)), full((HY_FFN, HY_FFN)),
            full((1, HY_FFN)), full((1, HY_FFN)),
            pl.BlockSpec((HY_FFN, cblk), lambda j: (0, j)),
            pl.BlockSpec((HY_FFN, cblk), lambda j: (0, ncb + j)),
            pl.BlockSpec((L, cblk), lambda j: (0, j)),
            full((L, L)), full((L, L)),
        ],
        out_specs=(out_spec, out_spec, out_spec),
        out_shape=(out_sds, out_sds, out_sds),
        compiler_params=_params("arbitrary"),
        name=f"hyena_filter_{L}",
    )(emb, w1, b1.reshape(1, HY_FFN), w2, b2.reshape(1, HY_FFN), freq.reshape(1, HY_FFN), w3, w3, decay, fc, fs)


def _hyena_conv_kernel(x0_ref, x1_ref, v_ref, cw0_ref, cw1_ref, cwv_ref, cb0_ref, cb1_ref, cbv_ref,
                       kr_ref, q_ref, krn_ref, ds_ref, fc_ref, fs_ref, gc_ref, gs_ref, o_ref):
    L = x0_ref.shape[0]
    row = lax.broadcasted_iota(jnp.int32, x0_ref.shape, 0)

    def short_conv(u_ref, w_ref, b_ref):
        u = u_ref[...]
        w = w_ref[...]
        prev = jnp.where(row == 0, 0.0, pltpu.roll(u, 1, axis=0))
        nxt = jnp.where(row == L - 1, 0.0, pltpu.roll(u, L - 1, axis=0))
        return prev * w[0:1, :] + u * w[1:2, :] + nxt * w[2:3, :] + b_ref[...]

    x0 = short_conv(x0_ref, cw0_ref, cb0_ref)
    x1 = short_conv(x1_ref, cw1_ref, cb1_ref)
    v = short_conv(v_ref, cwv_ref, cbv_ref)
    zz = v * x1
    zb = zz.astype(BF16)
    ur = jnp.dot(fc_ref[...], zb, preferred_element_type=F32)
    p = jnp.dot(fs_ref[...], zb, preferred_element_type=F32)
    qq = q_ref[...]
    yr = ur * kr_ref[...] - p * qq
    yw = ur * qq + p * krn_ref[...]
    y = jnp.dot(gc_ref[...], yr.astype(BF16), preferred_element_type=F32)
    y = y + jnp.dot(gs_ref[...], yw.astype(BF16), preferred_element_type=F32)
    o_ref[...] = (x0 * (y + zz * ds_ref[...])).astype(o_ref.dtype)


def _hyena_conv(u, conv_w, conv_b, dskip, spectrum, tables, *, latent):
    L = LATENT_LEN if latent else PROMPT_LEN
    n_seq = N_LATENT_SEQ if latent else N_PROMPT_SEQ
    cblk = 256 if latent else 512
    ncb = D_MODEL // cblk
    row0 = (N_PROMPT_TOK // L) if latent else 0
    kr, qq, krn = spectrum
    fc, fs, gc, gs = tables

    def part(p, rows):
        return pl.BlockSpec((rows, cblk), lambda j, s: (0 if rows != L else row0 + s, p * ncb + j))

    def const_cols(rows):
        return pl.BlockSpec((rows, cblk), lambda j, s: (0, j))

    mat = pl.BlockSpec((L, L), lambda j, s: (0, 0))
    conv_b2 = conv_b.reshape(1, 3 * D_MODEL)
    return pl.pallas_call(
        _hyena_conv_kernel,
        grid=(ncb, n_seq),
        in_specs=[part(0, L), part(1, L), part(2, L),
                  part(0, 3), part(1, 3), part(2, 3),
                  part(0, 1), part(1, 1), part(2, 1),
                  const_cols(L), const_cols(L), const_cols(L), const_cols(1),
                  mat, mat, mat, mat],
        out_specs=pl.BlockSpec((L, cblk), lambda j, s: (s, j)),
        out_shape=jax.ShapeDtypeStruct((n_seq * L, D_MODEL), BF16),
        compiler_params=_params("arbitrary", "arbitrary"),
        name="hyena_conv_latent" if latent else "hyena_conv_prompt",
    )(u, u, u, conv_w, conv_w, conv_w, conv_b2, conv_b2, conv_b2,
      kr, qq, krn, dskip.reshape(1, D_MODEL), fc, fs, gc, gs)


def kernel(x_prompt, x_sample, cache_k, cache_v, state_hgrn, c, c_ctx, norm_g, mod_w, mod_b, ab_in_w, hgrn_lb, hgrn_onorm_g, attn_qnorm_g, attn_knorm_g, ab_out_w, hy_in_w, hy_in_b, hy_conv_w, hy_conv_b, hy_f_w1, hy_f_b1, hy_f_w2, hy_f_b2, hy_f_w3, hy_f_freq, hy_dskip, hy_out_w, router_w, router_b, moe_w_gate, moe_w_up, moe_w_down):
    x = jnp.concatenate([x_prompt.reshape(N_PROMPT_TOK, D_MODEL), x_sample.reshape(N_LATENT_TOK, D_MODEL)], axis=0)
    cond = jnp.concatenate([c_ctx[None, :], c, jnp.zeros((N_COND - 1 - N_LATENT_SEQ, D_MODEL), F32)], axis=0)
    mod = _modulation(cond, mod_w, mod_b)
    router_wt = router_w.T

    z = _in_proj(x, norm_g[0, 0], mod[0], ab_in_w[0].astype(BF16), jnp.zeros((AB_IN,), F32))
    oa_p, new_state = _hgrn(z, hgrn_lb, hgrn_onorm_g[0], None, latent=False)
    oa_l = _hgrn(z, hgrn_lb, hgrn_onorm_g[0], state_hgrn, latent=True)
    ob_p, k_prompt = _attention_prompt(z, attn_qnorm_g[0], attn_knorm_g[0])
    ob_l = _attention_latent(z, attn_qnorm_g[0], attn_knorm_g[0], cache_k, cache_v)
    o_a = jnp.concatenate([oa_p, oa_l], axis=0)
    o_b = jnp.concatenate([ob_p, ob_l], axis=0)
    w_out = ab_out_w[0].astype(BF16)
    x, h, logits_t = _out_proj([o_a, o_b], [w_out[:A_WIDTH], w_out[A_WIDTH:]], x, norm_g[0, 1], mod[0], router_wt)
    gates = _router(logits_t, router_b).T
    x = _moe(h, gates, moe_w_gate, moe_w_up, moe_w_down, 0, x, mod[0])

    u = _in_proj(x, norm_g[1, 0], mod[1], hy_in_w[0].astype(BF16), hy_in_b[0])
    pre = []
    for latent in (False, True):
        L = LATENT_LEN if latent else PROMPT_LEN
        tables = _dft_tables(L)
        spectrum = _hyena_filter_spectrum(L, hy_f_w1[0], hy_f_b1[0], hy_f_w2[0], hy_f_b2[0], hy_f_w3[0],
                                          hy_f_freq[0], tables[0], tables[1])
        pre.append(_hyena_conv(u, hy_conv_w[0], hy_conv_b[0], hy_dskip[0], spectrum, tables, latent=latent))
    pre = jnp.concatenate(pre, axis=0)
    x, h, logits_t = _out_proj([pre], [hy_out_w[0].astype(BF16)], x, norm_g[1, 1], mod[1], router_wt)
    gates = _router(logits_t, router_b).T
    x = _moe(h, gates, moe_w_gate, moe_w_up, moe_w_down, 1, x, mod[1])

    y_prompt = x[:N_PROMPT_TOK].reshape(N_PROMPT_SEQ, PROMPT_LEN, D_MODEL)
    y_sample = x[N_PROMPT_TOK:].reshape(N_LATENT_SEQ, LATENT_LEN, D_MODEL)
    kv_shape = (N_PROMPT_SEQ, PROMPT_LEN, KV_HEADS, HEAD_DIM)
    new_k = k_prompt.reshape(kv_shape).transpose(0, 2, 1, 3)[:, None]
    v_col = 5 * A_WIDTH + (Q_HEADS + KV_HEADS) * HEAD_DIM
    new_v = z[:N_PROMPT_TOK, v_col:].reshape(kv_shape).transpose(0, 2, 1, 3)[:, None]
    return (y_prompt, y_sample, new_k, new_v, new_state)
```

```python
import functools
import math

import numpy as np
import jax
import jax.numpy as jnp
from jax import lax
from jax.experimental import pallas as pl
from jax.experimental.pallas import tpu as pltpu
from jax.experimental.pallas import tpu_sc as plsc

F32 = jnp.float32
BF16 = jnp.bfloat16
HIGHEST = lax.Precision.HIGHEST

D_MODEL = 1024
N_PROMPT_SEQ = 32
PROMPT_LEN = 256
N_LATENT_SEQ = 2
LATENT_LEN = 1024
PAST_LEN = 512
GRID_W = 64
N_PROMPT_TOK = N_PROMPT_SEQ * PROMPT_LEN
N_LATENT_TOK = N_LATENT_SEQ * LATENT_LEN
N_TOK = N_PROMPT_TOK + N_LATENT_TOK
N_COND = 8
EPS = 1e-6

A_WIDTH = 512
A_HEADS = 4
A_DK = 128
CHUNK = 64
HEAD_DIM = 64
Q_HEADS = 8
KV_HEADS = 2
Q_PER_KV = Q_HEADS // KV_HEADS
Q_BLOCK = 256
ROPE_THETA = 10000.0
ROPE_PAIRS = HEAD_DIM // 4
AB_IN = 5 * A_WIDTH + (Q_HEADS + 2 * KV_HEADS) * HEAD_DIM

HY_BANDS = 16
HY_FFN = 64
HY_DECAY_TARGET = 1e-2
HY_FAST_PCT = 0.3
HY_SLOW_PCT = 1.5

N_EXPERTS = 16
N_GROUPS = 4
EXPERTS_PER_GROUP = 4
TOP_K = 2
D_EXPERT = 512
MOE_TILE = 256
MOE_ROWS = N_TOK * TOP_K + N_EXPERTS * MOE_TILE

SC_CORES = 2
SC_WORKERS = 32
SC_CHUNK = 64

VMEM_LIMIT = 56 * 1024 * 1024


def _params(*sem):
    return pltpu.CompilerParams(dimension_semantics=sem, vmem_limit_bytes=VMEM_LIMIT)


def _cond_of_token_block(i, block_rows):
    start = i * block_rows
    return jnp.where(start < N_PROMPT_TOK, 0, 1 + (start - N_PROMPT_TOK) // LATENT_LEN)


def _mod_kernel(cond_ref, w_ref, b_ref, o_ref):
    cnd = cond_ref[...]
    s = cnd * jax.nn.sigmoid(cnd)
    o_ref[...] = jnp.dot(s, w_ref[...], precision=HIGHEST, preferred_element_type=F32) + b_ref[...]


def _modulation(cond, mod_w, mod_b):
    depth = mod_w.shape[0]
    n_chunk = 6
    out = pl.pallas_call(
        _mod_kernel,
        grid=(depth, n_chunk),
        in_specs=[
            pl.BlockSpec((N_COND, D_MODEL), lambda l, j: (0, 0)),
            pl.BlockSpec((None, D_MODEL, D_MODEL), lambda l, j: (l, 0, j)),
            pl.BlockSpec((None, 1, D_MODEL), lambda l, j: (l, 0, j)),
        ],
        out_specs=pl.BlockSpec((None, N_COND, D_MODEL), lambda l, j: (l, 0, j)),
        out_shape=jax.ShapeDtypeStruct((depth, N_COND, n_chunk * D_MODEL), F32),
        compiler_params=_params("arbitrary", "arbitrary"),
        name="modulation",
    )(cond, mod_w, mod_b.reshape(depth, 1, n_chunk * D_MODEL))
    return out.reshape(depth, N_COND, n_chunk, D_MODEL)


def _modulated_norm(x, g, mod, shift_row, scale_row):
    ms = jnp.mean(x * x, axis=-1, keepdims=True)
    y = x * lax.rsqrt(ms + EPS) * g
    return y * (1.0 + mod[scale_row:scale_row + 1, :]) + mod[shift_row:shift_row + 1, :]


def _in_proj_kernel(x_ref, g_ref, mod_ref, w_ref, b_ref, o_ref):
    h = _modulated_norm(x_ref[...], g_ref[...], mod_ref[...], 0, 1)
    o_ref[...] = jnp.dot(h.astype(BF16), w_ref[...], preferred_element_type=F32) + b_ref[...]


def _in_proj(x, g, mod_l, w_bf16, bias, block_rows=256):
    n = w_bf16.shape[1]
    return pl.pallas_call(
        _in_proj_kernel,
        grid=(N_TOK // block_rows,),
        in_specs=[
            pl.BlockSpec((block_rows, D_MODEL), lambda i: (i, 0)),
            pl.BlockSpec((1, D_MODEL), lambda i: (0, 0)),
            pl.BlockSpec((None, 6, D_MODEL), lambda i: (_cond_of_token_block(i, block_rows), 0, 0)),
            pl.BlockSpec((D_MODEL, n), lambda i: (0, 0)),
            pl.BlockSpec((1, n), lambda i: (0, 0)),
        ],
        out_specs=pl.BlockSpec((block_rows, n), lambda i: (i, 0)),
        out_shape=jax.ShapeDtypeStruct((N_TOK, n), F32),
        compiler_params=_params("arbitrary"),
        name="in_proj",
    )(x, g.reshape(1, D_MODEL), mod_l, w_bf16, bias.reshape(1, n))


def _hgrn_kernel(*refs, seq_len, with_state):
    if with_state:
        (q_ref, zf_ref, zb_ref, i_ref, ga_ref, lb_ref, og_ref, s0_ref, o_ref, of_ref, ob_ref) = refs
    else:
        (q_ref, zf_ref, zb_ref, i_ref, ga_ref, lb_ref, og_ref, o_ref, s_ref, of_ref, ob_ref) = refs
    n_chunks = seq_len // CHUNK

    lbr = lb_ref[...]
    mx = jnp.maximum(lbr[0], lbr[1])
    e0 = jnp.exp(lbr[0] - mx)
    e1 = jnp.exp(lbr[1] - mx)
    lb = e0 / (e0 + e1)

    row = lax.broadcasted_iota(jnp.int32, (CHUNK, CHUNK), 0)
    col = lax.broadcasted_iota(jnp.int32, (CHUNK, CHUNK), 1)
    lower = col <= row
    tri_lo = jnp.where(lower, 1.0, 0.0).astype(F32)
    tri_up = jnp.where(col >= row, 1.0, 0.0).astype(F32)

    def direction(c, st, z_ref, lbd, tri, keep, mid, last, out_ref):
        r0 = pl.multiple_of(c * CHUNK, CHUNK)
        rows = pl.ds(r0, CHUNK)
        f = lbd + (1.0 - lbd) * jax.nn.sigmoid(z_ref[rows, :])
        lf = jnp.log(f)
        k = 1.0 - f
        q = q_ref[rows, :]
        v = i_ref[rows, :]
        b = jnp.dot(tri, lf, precision=HIGHEST, preferred_element_type=F32)
        bm = b[mid:mid + 1, :]
        bl = b[last:last + 1, :]
        qe = (q * jnp.exp(b - bm)).astype(BF16)
        ke = (k * jnp.exp(bm - b)).astype(BF16)
        att = lax.dot_general(qe, ke, (((1,), (1,)), ((), ())), preferred_element_type=F32)
        att = jnp.where(keep, att, 0.0)
        vb = v.astype(BF16)
        o = jnp.dot(att.astype(BF16), vb, preferred_element_type=F32)
        qb = (q * jnp.exp(b)).astype(BF16)
        o = o + lax.dot_general(qb, st.astype(BF16), (((1,), (1,)), ((), ())), preferred_element_type=F32)
        out_ref[rows, :] = o
        ks = (k * jnp.exp(bl - b)).astype(BF16)
        upd = lax.dot_general(vb, ks, (((0,), (0,)), ((), ())), preferred_element_type=F32)
        return st * jnp.exp(bl) + upd

    def body(c, carry):
        st_f, st_b = carry
        st_f = direction(c, st_f, zf_ref, lb[0:1, :], tri_lo, lower, CHUNK // 2, CHUNK - 1, of_ref)
        st_b = direction(n_chunks - 1 - c, st_b, zb_ref, lb[1:2, :], tri_up, col >= row,
                         CHUNK - 1 - CHUNK // 2, 0, ob_ref)
        return st_f, st_b

    if with_state:
        init = (s0_ref[0].T, s0_ref[1].T)
    else:
        init = (jnp.zeros((A_DK, A_DK), F32), jnp.zeros((A_DK, A_DK), F32))
    st_f, st_b = lax.fori_loop(0, n_chunks, body, init)
    if not with_state:
        s_ref[0] = st_f.T
        s_ref[1] = st_b.T

    o = of_ref[...] + ob_ref[...]
    o = o * lax.rsqrt(jnp.mean(o * o, axis=-1, keepdims=True) + EPS) * og_ref[...]
    ga = ga_ref[...]
    o_ref[...] = (o * (ga * jax.nn.sigmoid(ga))).astype(o_ref.dtype)


def _hgrn(z, hgrn_lb, onorm_g, state, *, latent):
    seq_len = LATENT_LEN if latent else PROMPT_LEN
    n_seq = N_LATENT_SEQ if latent else N_PROMPT_SEQ
    row0 = (N_PROMPT_TOK // seq_len) if latent else 0

    def zspec(col0):
        return pl.BlockSpec((seq_len, A_DK), lambda s, h: (row0 + s, col0 + h))

    in_specs = [zspec(0), zspec(4), zspec(8), zspec(12), zspec(16),
                pl.BlockSpec((2, 2, A_DK), lambda s, h: (0, 0, h)),
                pl.BlockSpec((1, A_DK), lambda s, h: (0, h))]
    args = [z, z, z, z, z, hgrn_lb, onorm_g.reshape(1, A_WIDTH)]
    state_spec = pl.BlockSpec((None, None, 2, None, A_DK, A_DK), lambda s, h: (s, 0, 0, h, 0, 0))
    o_shape = jax.ShapeDtypeStruct((n_seq * seq_len, A_WIDTH), BF16)
    o_spec = pl.BlockSpec((seq_len, A_DK), lambda s, h: (s, h))
    if latent:
        in_specs.append(state_spec)
        args.append(state)
        out_shape, out_specs = o_shape, o_spec
    else:
        out_shape = (o_shape, jax.ShapeDtypeStruct((n_seq, 1, 2, A_HEADS, A_DK, A_DK), F32))
        out_specs = (o_spec, state_spec)
    return pl.pallas_call(
        functools.partial(_hgrn_kernel, seq_len=seq_len, with_state=latent),
        grid=(n_seq, A_HEADS),
        in_specs=in_specs,
        out_specs=out_specs,
        out_shape=out_shape,
        scratch_shapes=[pltpu.VMEM((seq_len, A_DK), F32), pltpu.VMEM((seq_len, A_DK), F32)],
        compiler_params=_params("arbitrary", "arbitrary"),
        name="hgrn_latent" if latent else "hgrn_prompt",
    )(*args)


def _rope_tables():
    pos = np.arange(LATENT_LEN)
    row, colp = pos // GRID_W, pos % GRID_W
    inv = ROPE_THETA ** (-np.arange(ROPE_PAIRS, dtype=np.float32) / ROPE_PAIRS)
    inv = inv.astype(np.float32)
    ang_r = (row.astype(np.float32)[:, None] * inv).astype(np.float32)
    ang_c = (colp.astype(np.float32)[:, None] * inv).astype(np.float32)
    cos = np.concatenate([np.cos(ang_r), np.cos(ang_r), np.cos(ang_c), np.cos(ang_c)], axis=1)
    sin = np.concatenate([-np.sin(ang_r), np.sin(ang_r), -np.sin(ang_c), np.sin(ang_c)], axis=1)
    perm = np.zeros((HEAD_DIM, HEAD_DIM), np.float32)
    for d in range(HEAD_DIM):
        partner = d + ROPE_PAIRS if (d // ROPE_PAIRS) % 2 == 0 else d - ROPE_PAIRS
        perm[partner, d] = 1.0
    return cos.astype(np.float32), sin.astype(np.float32), perm


def _attn_kernel(*refs, latent):
    if latent:
        (q_ref, k_ref, v_ref, qg_ref, kg_ref, cosq_ref, sinq_ref, cosk_ref, sink_ref, perm_ref,
         ck_ref, cv_ref, o_ref) = refs
    else:
        (q_ref, k_ref, v_ref, qg_ref, kg_ref, o_ref, kout_ref) = refs

    def head(x_ref, h, g, cos, sin):
        xh = x_ref[:, h * HEAD_DIM:(h + 1) * HEAD_DIM]
        xh = xh * lax.rsqrt(jnp.mean(xh * xh, axis=-1, keepdims=True) + EPS) * g
        if latent:
            swapped = jnp.dot(xh, perm_ref[...], precision=HIGHEST, preferred_element_type=F32)
            xh = xh * cos + swapped * sin
        return xh

    qg = qg_ref[...]
    kg = kg_ref[...]
    cq = sq = ck = sk = None
    if latent:
        cq, sq, ck, sk = cosq_ref[...], sinq_ref[...], cosk_ref[...], sink_ref[...]
    scale = HEAD_DIM ** -0.5
    n_q = q_ref.shape[0]
    for j in range(KV_HEADS):
        kh = head(k_ref, j, kg, ck, sk)
        if not latent:
            kout_ref[:, j * HEAD_DIM:(j + 1) * HEAD_DIM] = kh
        vh = v_ref[:, j * HEAD_DIM:(j + 1) * HEAD_DIM]
        qs = jnp.concatenate(
            [head(q_ref, j * Q_PER_KV + t, qg, cq, sq) * scale for t in range(Q_PER_KV)], axis=0)
        qs = qs.astype(BF16)
        nt = (((1,), (1,)), ((), ()))
        s_new = lax.dot_general(qs, kh.astype(BF16), nt, preferred_element_type=F32)
        m = jnp.max(s_new, axis=-1, keepdims=True)
        if latent:
            s_old = lax.dot_general(qs, ck_ref[j].astype(BF16), nt, preferred_element_type=F32)
            m = jnp.maximum(m, jnp.max(s_old, axis=-1, keepdims=True))
        p_new = jnp.exp(s_new - m)
        den = jnp.sum(p_new, axis=-1, keepdims=True)
        acc = jnp.dot(p_new.astype(BF16), vh.astype(BF16), preferred_element_type=F32)
        if latent:
            p_old = jnp.exp(s_old - m)
            den = den + jnp.sum(p_old, axis=-1, keepdims=True)
            acc = acc + jnp.dot(p_old.astype(BF16), cv_ref[j].astype(BF16), preferred_element_type=F32)
        out = acc / den
        for t in range(Q_PER_KV):
            hq = j * Q_PER_KV + t
            o_ref[:, hq * HEAD_DIM:(hq + 1) * HEAD_DIM] = out[t * n_q:(t + 1) * n_q, :].astype(o_ref.dtype)


def _attention_prompt(z, qn_g, kn_g):
    L = PROMPT_LEN
    q_col = (5 * A_WIDTH) // (Q_HEADS * HEAD_DIM)
    k_col = (5 * A_WIDTH + Q_HEADS * HEAD_DIM) // (KV_HEADS * HEAD_DIM)
    kv_w = KV_HEADS * HEAD_DIM
    return pl.pallas_call(
        functools.partial(_attn_kernel, latent=False),
        grid=(N_PROMPT_SEQ,),
        in_specs=[
            pl.BlockSpec((L, Q_HEADS * HEAD_DIM), lambda s: (s, q_col)),
            pl.BlockSpec((L, kv_w), lambda s: (s, k_col)),
            pl.BlockSpec((L, kv_w), lambda s: (s, k_col + 1)),
            pl.BlockSpec((1, HEAD_DIM), lambda s: (0, 0)),
            pl.BlockSpec((1, HEAD_DIM), lambda s: (0, 0)),
        ],
        out_specs=(pl.BlockSpec((L, Q_HEADS * HEAD_DIM), lambda s: (s, 0)),
                   pl.BlockSpec((L, kv_w), lambda s: (s, 0))),
        out_shape=(jax.ShapeDtypeStruct((N_PROMPT_TOK, Q_HEADS * HEAD_DIM), BF16),
                   jax.ShapeDtypeStruct((N_PROMPT_TOK, kv_w), F32)),
        compiler_params=_params("arbitrary"),
        name="attn_prompt",
    )(z, z, z, qn_g.reshape(1, HEAD_DIM), kn_g.reshape(1, HEAD_DIM))


def _attention_latent(z, qn_g, kn_g, cache_k, cache_v):
    L = LATENT_LEN
    nqb = L // Q_BLOCK
    q_col = (5 * A_WIDTH) // (Q_HEADS * HEAD_DIM)
    k_col = (5 * A_WIDTH + Q_HEADS * HEAD_DIM) // (KV_HEADS * HEAD_DIM)
    kv_w = KV_HEADS * HEAD_DIM
    qrow0 = N_PROMPT_TOK // Q_BLOCK
    krow0 = N_PROMPT_TOK // L
    cos, sin, perm = _rope_tables()
    cache_spec = pl.BlockSpec((None, None, KV_HEADS, PAST_LEN, HEAD_DIM), lambda s, b: (s, 0, 0, 0, 0))
    return pl.pallas_call(
        functools.partial(_attn_kernel, latent=True),
        grid=(N_LATENT_SEQ, nqb),
        in_specs=[
            pl.BlockSpec((Q_BLOCK, Q_HEADS * HEAD_DIM), lambda s, b: (qrow0 + s * nqb + b, q_col)),
            pl.BlockSpec((L, kv_w), lambda s, b: (krow0 + s, k_col)),
            pl.BlockSpec((L, kv_w), lambda s, b: (krow0 + s, k_col + 1)),
            pl.BlockSpec((1, HEAD_DIM), lambda s, b: (0, 0)),
            pl.BlockSpec((1, HEAD_DIM), lambda s, b: (0, 0)),
            pl.BlockSpec((Q_BLOCK, HEAD_DIM), lambda s, b: (b, 0)),
            pl.BlockSpec((Q_BLOCK, HEAD_DIM), lambda s, b: (b, 0)),
            pl.BlockSpec((L, HEAD_DIM), lambda s, b: (0, 0)),
            pl.BlockSpec((L, HEAD_DIM), lambda s, b: (0, 0)),
            pl.BlockSpec((HEAD_DIM, HEAD_DIM), lambda s, b: (0, 0)),
            cache_spec, cache_spec,
        ],
        out_specs=pl.BlockSpec((Q_BLOCK, Q_HEADS * HEAD_DIM), lambda s, b: (s * nqb + b, 0)),
        out_shape=jax.ShapeDtypeStruct((N_LATENT_TOK, Q_HEADS * HEAD_DIM), BF16),
        compiler_params=_params("arbitrary", "arbitrary"),
        name="attn_latent",
    )(z, z, z, qn_g.reshape(1, HEAD_DIM), kn_g.reshape(1, HEAD_DIM),
      jnp.asarray(cos), jnp.asarray(sin), jnp.asarray(cos), jnp.asarray(sin), jnp.asarray(perm),
      cache_k, cache_v)


def _out_proj_kernel(*refs, n_in):
    a_refs = refs[:n_in]
    w_refs = refs[n_in:2 * n_in]
    x_ref, g_ref, mod_ref, rw_ref, xo_ref, h_ref, lg_ref = refs[2 * n_in:]
    acc = jnp.dot(a_refs[0][...], w_refs[0][...], preferred_element_type=F32)
    for a_ref, w_ref in zip(a_refs[1:], w_refs[1:]):
        acc = acc + jnp.dot(a_ref[...], w_ref[...], preferred_element_type=F32)
    mod = mod_ref[...]
    x = x_ref[...] + mod[2:3, :] * acc
    xo_ref[...] = x
    h = _modulated_norm(x, g_ref[...], mod, 3, 4)
    h_ref[...] = h
    lg_ref[...] = lax.dot_general(rw_ref[...], h, (((1,), (1,)), ((), ())), precision=HIGHEST,
                                  preferred_element_type=F32)


def _out_proj(acts, weights, x, g, mod_l, router_wt, block_rows=256):
    n_in = len(acts)
    in_specs = [pl.BlockSpec((block_rows, a.shape[1]), lambda i: (i, 0)) for a in acts]
    in_specs += [pl.BlockSpec(w.shape, lambda i: (0, 0)) for w in weights]
    in_specs += [
        pl.BlockSpec((block_rows, D_MODEL), lambda i: (i, 0)),
        pl.BlockSpec((1, D_MODEL), lambda i: (0, 0)),
        pl.BlockSpec((None, 6, D_MODEL), lambda i: (_cond_of_token_block(i, block_rows), 0, 0)),
        pl.BlockSpec((N_EXPERTS, D_MODEL), lambda i: (0, 0)),
    ]
    return pl.pallas_call(
        functools.partial(_out_proj_kernel, n_in=n_in),
        grid=(N_TOK // block_rows,),
        in_specs=in_specs,
        out_specs=(pl.BlockSpec((block_rows, D_MODEL), lambda i: (i, 0)),
                   pl.BlockSpec((block_rows, D_MODEL), lambda i: (i, 0)),
                   pl.BlockSpec((N_EXPERTS, block_rows), lambda i: (0, i))),
        out_shape=(jax.ShapeDtypeStruct((N_TOK, D_MODEL), F32),
                   jax.ShapeDtypeStruct((N_TOK, D_MODEL), F32),
                   jax.ShapeDtypeStruct((N_EXPERTS, N_TOK), F32)),
        compiler_params=_params("arbitrary"),
        name="out_proj",
    )(*acts, *weights, x, g.reshape(1, D_MODEL), mod_l, router_wt)


def _router_kernel(lg_ref, rb_ref, pos_ref, w_ref, plan_ref, rank_ref):
    lg = lg_ref[...]
    ex = jnp.exp(lg - jnp.max(lg, axis=0, keepdims=True))
    scores = ex / jnp.sum(ex, axis=0, keepdims=True)
    biased = scores + rb_ref[...]
    rows = [biased[e:e + 1, :] for e in range(N_EXPERTS)]
    selected = []
    group_score = []
    for gi in range(N_GROUPS):
        r = rows[gi * EXPERTS_PER_GROUP:(gi + 1) * EXPERTS_PER_GROUP]
        total = None
        for i in range(EXPERTS_PER_GROUP):
            rank = None
            for j in range(EXPERTS_PER_GROUP):
                if j == i:
                    continue
                ahead = (r[j] > r[i]) if j > i else (r[j] >= r[i])
                ahead = jnp.where(ahead, 1.0, 0.0)
                rank = ahead if rank is None else rank + ahead
            sel = rank < 1.5
            selected.append(sel)
            contrib = jnp.where(sel, r[i], 0.0)
            total = contrib if total is None else total + contrib
        group_score.append(total)
    best = group_score[0]
    best_group = jnp.zeros_like(best)
    for gi in range(1, N_GROUPS):
        better = group_score[gi] > best
        best_group = jnp.where(better, float(gi), best_group)
        best = jnp.where(better, group_score[gi], best)
    picked = []
    chosen = []
    den = None
    for e in range(N_EXPERTS):
        in_group = best_group == float(e // EXPERTS_PER_GROUP)
        use = jnp.where(selected[e], jnp.where(in_group, 1.0, 0.0), 0.0)
        w = use * scores[e:e + 1, :]
        chosen.append(use)
        picked.append(w)
        den = w if den is None else den + w
    lanes = 128
    n_blk = N_TOK // lanes
    li = lax.broadcasted_iota(jnp.int32, (lanes, lanes), 0)
    lj = lax.broadcasted_iota(jnp.int32, (lanes, lanes), 1)
    prefix = jnp.where(li <= lj, 1.0, 0.0).astype(BF16)
    carry = jnp.zeros((N_EXPERTS, 1), F32)
    for blk in range(n_blk):
        cols = slice(blk * lanes, (blk + 1) * lanes)
        m = jnp.concatenate([chosen[e][:, cols] for e in range(N_EXPERTS)], axis=0)
        incl = jnp.dot(m.astype(BF16), prefix, preferred_element_type=F32)
        rank_ref[:, cols] = incl - m + carry
        carry = carry + incl[:, lanes - 1:lanes]
    count = carry
    padded = jnp.floor((count + float(MOE_TILE - 1)) * (1.0 / MOE_TILE)) * float(MOE_TILE)
    erow = lax.broadcasted_iota(jnp.int32, (N_EXPERTS, 1), 0)
    offset = jnp.zeros((N_EXPERTS, 1), F32)
    for e in range(N_EXPERTS - 1):
        offset = offset + jnp.where(erow > e, padded[e:e + 1, :], 0.0)
    seen = jnp.zeros_like(den)
    pos_a = jnp.zeros_like(den)
    pos_b = jnp.zeros_like(den)
    w_a = jnp.zeros_like(den)
    w_b = jnp.zeros_like(den)
    for e in range(N_EXPERTS):
        pos_e = rank_ref[e:e + 1, :] + offset[e:e + 1, :]
        gate_e = picked[e] / den
        first = jnp.where(seen < 0.5, chosen[e], 0.0) > 0.5
        second = jnp.where(seen > 0.5, chosen[e], 0.0) > 0.5
        pos_a = jnp.where(first, pos_e, pos_a)
        w_a = jnp.where(first, gate_e, w_a)
        pos_b = jnp.where(second, pos_e, pos_b)
        w_b = jnp.where(second, gate_e, w_b)
        seen = seen + chosen[e]
    pos_ref[0:1, :] = pos_a.astype(jnp.int32)
    pos_ref[1:2, :] = pos_b.astype(jnp.int32)
    w_ref[0:1, :] = w_a
    w_ref[1:2, :] = w_b
    start = (lax.broadcasted_iota(jnp.int32, (N_EXPERTS, lanes), 1) * MOE_TILE).astype(F32)
    end = offset + padded
    tile_expert = jnp.sum(jnp.where(end <= start, 1.0, 0.0), axis=0, keepdims=True)
    inside = (offset <= start) & (start < end)
    real = jnp.clip(count - (start - offset), 0.0, float(MOE_TILE))
    tile_rows = jnp.sum(jnp.where(inside, real, 0.0), axis=0, keepdims=True)
    plan_ref[0:1, :] = jnp.minimum(tile_expert, float(N_EXPERTS - 1)).astype(jnp.int32)
    plan_ref[1:2, :] = tile_rows.astype(jnp.int32)


def _router(logits_t, router_b):
    whole = lambda shape: pl.BlockSpec(shape, lambda i: (0, 0))
    return pl.pallas_call(
        _router_kernel,
        grid=(1,),
        in_specs=[whole((N_EXPERTS, N_TOK)), whole((N_EXPERTS, 1))],
        out_specs=(whole((2, N_TOK)), whole((2, N_TOK)), whole((2, 128))),
        out_shape=(jax.ShapeDtypeStruct((2, N_TOK), jnp.int32),
                   jax.ShapeDtypeStruct((2, N_TOK), F32),
                   jax.ShapeDtypeStruct((2, 128), jnp.int32)),
        scratch_shapes=[pltpu.VMEM((N_EXPERTS, N_TOK), F32)],
        compiler_params=_params("arbitrary"),
        name="router",
    )(logits_t, router_b.reshape(N_EXPERTS, 1))


def _sc_mesh():
    return plsc.VectorSubcoreMesh(core_axis_name="c", subcore_axis_name="s")


def _sc_worker_base():
    return (lax.axis_index("s") * SC_CORES + lax.axis_index("c")) * (N_TOK // SC_WORKERS)


def _moe_dispatch(h, pos_a, pos_b):
    n_chunks = N_TOK // SC_WORKERS // SC_CHUNK

    @functools.partial(
        pl.kernel, mesh=_sc_mesh(),
        out_type=jax.ShapeDtypeStruct((MOE_ROWS, D_MODEL), F32),
        scratch_types=[pltpu.VMEM((SC_CHUNK,), jnp.int32), pltpu.VMEM((SC_CHUNK,), jnp.int32),
                       pltpu.VMEM((SC_CHUNK, D_MODEL), F32)],
        name="moe_dispatch",
    )
    def run(h_hbm, pa_hbm, pb_hbm, xs_hbm, ia_v, ib_v, rows_v):
        base = _sc_worker_base()

        @pl.loop(0, n_chunks)
        def _(ci):
            tok = pl.ds(pl.multiple_of(base + ci * SC_CHUNK, SC_CHUNK), SC_CHUNK)
            pltpu.sync_copy(pa_hbm.at[tok], ia_v)
            pltpu.sync_copy(pb_hbm.at[tok], ib_v)
            pltpu.sync_copy(h_hbm.at[tok], rows_v)
            pltpu.sync_copy(rows_v, xs_hbm.at[ia_v])
            pltpu.sync_copy(rows_v, xs_hbm.at[ib_v])

    return run(h, pos_a, pos_b)


def _moe_collect(ys, pos_a, pos_b):
    n_chunks = N_TOK // SC_WORKERS // SC_CHUNK
    out = jax.ShapeDtypeStruct((N_TOK, D_MODEL), F32)

    @functools.partial(
        pl.kernel, mesh=_sc_mesh(), out_type=(out, out),
        scratch_types=[pltpu.VMEM((SC_CHUNK,), jnp.int32), pltpu.VMEM((SC_CHUNK,), jnp.int32),
                       pltpu.VMEM((SC_CHUNK, D_MODEL), F32)],
        name="moe_collect",
    )
    def run(ys_hbm, pa_hbm, pb_hbm, ya_hbm, yb_hbm, ia_v, ib_v, rows_v):
        base = _sc_worker_base()

        @pl.loop(0, n_chunks)
        def _(ci):
            tok = pl.ds(pl.multiple_of(base + ci * SC_CHUNK, SC_CHUNK), SC_CHUNK)
            pltpu.sync_copy(pa_hbm.at[tok], ia_v)
            pltpu.sync_copy(pb_hbm.at[tok], ib_v)
            pltpu.sync_copy(ys_hbm.at[ia_v], rows_v)
            pltpu.sync_copy(rows_v, ya_hbm.at[tok])
            pltpu.sync_copy(ys_hbm.at[ib_v], rows_v)
            pltpu.sync_copy(rows_v, yb_hbm.at[tok])

    return run(ys, pos_a, pos_b)


def _experts_kernel(plan_ref, xs_ref, wg_ref, wu_ref, wd_ref, y_ref, wgb_ref, wub_ref, wdb_ref):
    j = pl.program_id(0)
    expert = plan_ref[j]
    n_real = plan_ref[128 + j]
    fresh = jnp.logical_or(j == 0, expert != plan_ref[jnp.maximum(j - 1, 0)])

    @pl.when(jnp.logical_and(n_real > 0, fresh))
    def _():
        wgb_ref[...] = wg_ref[...].astype(BF16)
        wub_ref[...] = wu_ref[...].astype(BF16)
        wdb_ref[...] = wd_ref[...].astype(BF16)

    @pl.when(n_real > 0)
    def _():
        row = lax.broadcasted_iota(jnp.int32, xs_ref.shape, 0)
        x = jnp.where(row < n_real, xs_ref[...], 0.0).astype(BF16)
        a = jnp.dot(x, wgb_ref[...], preferred_element_type=F32)
        b = jnp.dot(x, wub_ref[...], preferred_element_type=F32)
        hid = (a * jax.nn.sigmoid(a)) * b
        y_ref[...] = jnp.dot(hid.astype(BF16), wdb_ref[...], preferred_element_type=F32)


def _experts(plan, xs, w_gate, w_up, w_down, layer):
    wspec = lambda r, c: pl.BlockSpec((None, None, r, c), lambda j, plan: (layer, plan[j], 0, 0))
    return pl.pallas_call(
        _experts_kernel,
        grid_spec=pltpu.PrefetchScalarGridSpec(
            num_scalar_prefetch=1,
            grid=(MOE_ROWS // MOE_TILE,),
            in_specs=[pl.BlockSpec((MOE_TILE, D_MODEL), lambda j, plan: (j, 0)),
                      wspec(D_MODEL, D_EXPERT), wspec(D_MODEL, D_EXPERT), wspec(D_EXPERT, D_MODEL)],
            out_specs=pl.BlockSpec((MOE_TILE, D_MODEL), lambda j, plan: (j, 0)),
            scratch_shapes=[pltpu.VMEM((D_MODEL, D_EXPERT), BF16), pltpu.VMEM((D_MODEL, D_EXPERT), BF16),
                            pltpu.VMEM((D_EXPERT, D_MODEL), BF16)],
        ),
        out_shape=jax.ShapeDtypeStruct((MOE_ROWS, D_MODEL), F32),
        compiler_params=_params("arbitrary"),
        name="experts",
    )(plan, xs, w_gate, w_up, w_down)


def _combine_kernel(x_ref, ya_ref, yb_ref, w_ref, mod_ref, o_ref):
    w = w_ref[...]
    mix = w[:, 0:1] * ya_ref[...] + w[:, 1:2] * yb_ref[...]
    o_ref[...] = x_ref[...] + mod_ref[5:6, :] * mix


def _combine(x, ya, yb, w_tok, mod_l, block_rows=512):
    tok = pl.BlockSpec((block_rows, D_MODEL), lambda i: (i, 0))
    return pl.pallas_call(
        _combine_kernel,
        grid=(N_TOK // block_rows,),
        in_specs=[tok, tok, tok,
                  pl.BlockSpec((block_rows, 2), lambda i: (i, 0)),
                  pl.BlockSpec((None, 6, D_MODEL), lambda i: (_cond_of_token_block(i, block_rows), 0, 0))],
        out_specs=tok,
        out_shape=jax.ShapeDtypeStruct((N_TOK, D_MODEL), F32),
        compiler_params=_params("arbitrary"),
        name="combine",
    )(x, ya, yb, w_tok, mod_l)


def _moe(h, logits_t, router_b, w_gate, w_up, w_down, layer, x, mod_l):
    pos, w, plan = _router(logits_t, router_b)
    xs = _moe_dispatch(h, pos[0], pos[1])
    ys = _experts(plan.reshape(-1), xs, w_gate, w_up, w_down, layer)
    ya, yb = _moe_collect(ys, pos[0], pos[1])
    return _combine(x, ya, yb, w.T, mod_l)


def _dft_tables(L):
    k = np.arange(L)[:, None]
    m = np.arange(L)[None, :]
    r = (k * m) % (2 * L)
    ang = np.pi * r.astype(np.float64) / L
    fc = np.cos(ang)
    fs = np.sin(ang)
    fs[0, :] = np.where(np.arange(L) % 2 == 0, 1.0, -1.0)
    wk = np.full((L, 1), 1.0 / L)
    wk[0, 0] = 0.5 / L
    gc = (fc * wk).T
    gs = (fs * wk).T
    return [jnp.asarray(t.astype(np.float32)).astype(BF16) for t in (fc, fs, gc, gs)]


def _filter_consts(L):
    t = np.linspace(0.0, 1.0, L, dtype=np.float32)[:, None]
    w = (np.float32(2.0 * np.pi) * np.arange(L, dtype=np.float32)[:, None] / np.float32(L)).astype(np.float32)
    fb = np.linspace(1e-4, HY_BANDS - 1, HY_BANDS, dtype=np.float32)[None, :]
    emb = np.concatenate([t, np.cos(fb * w), -np.sin(fb * w)], axis=-1).astype(np.float32)
    lo = math.log(HY_DECAY_TARGET) / HY_SLOW_PCT
    hi = math.log(HY_DECAY_TARGET) / HY_FAST_PCT
    deltas = np.abs(np.linspace(lo, hi, D_MODEL, dtype=np.float32))
    decay = np.exp(-t * deltas).astype(np.float32)
    return jnp.asarray(emb), jnp.asarray(decay)


def _filter_kernel(emb_ref, w1_ref, b1_ref, w2_ref, b2_ref, fr_ref, w3f_ref, w3b_ref, dec_ref,
                   fc_ref, fs_ref, kr_ref, q_ref, krn_ref):
    fr = fr_ref[...]
    hd = jnp.sin(fr * (jnp.dot(emb_ref[...], w1_ref[...], precision=HIGHEST,
                               preferred_element_type=F32) + b1_ref[...]))
    hd = jnp.sin(fr * (jnp.dot(hd, w2_ref[...], precision=HIGHEST,
                               preferred_element_type=F32) + b2_ref[...]))
    dec = dec_ref[...]
    f = jnp.dot(hd, w3f_ref[...], precision=HIGHEST, preferred_element_type=F32) * dec
    g = jnp.dot(hd, w3b_ref[...], precision=HIGHEST, preferred_element_type=F32) * dec
    row = lax.broadcasted_iota(jnp.int32, f.shape, 0)
    g = jnp.where(row == 0, 0.0, g)
    s = f + g
    d = f - g
    kr = jnp.dot(fc_ref[...], s.astype(BF16), preferred_element_type=F32)
    qq = jnp.dot(fs_ref[...], d.astype(BF16), preferred_element_type=F32)
    alt = jnp.where(row % 2 == 0, 1.0, -1.0)
    nyq = jnp.sum(alt * s, axis=0, keepdims=True)
    kr_ref[...] = kr
    q_ref[...] = jnp.where(row == 0, 0.0, qq)
    krn_ref[...] = jnp.where(row == 0, nyq, kr)


def _hyena_filter_spectrum(L, w1, b1, w2, b2, w3, freq, fc, fs, cblk=256):
    emb, decay = _filter_consts(L)
    ncb = D_MODEL // cblk
    n_emb = 128
    emb = jnp.pad(emb, ((0, 0), (0, n_emb - emb.shape[1])))
    w1 = jnp.pad(w1, ((0, n_emb - w1.shape[0]), (0, 0)))
    full = lambda shape: pl.BlockSpec(shape, lambda j: tuple(0 for _ in shape))
    out_sds = jax.ShapeDtypeStruct((L, D_MODEL), F32)
    out_spec = pl.BlockSpec((L, cblk), lambda j: (0, j))
    return pl.pallas_call(
        _filter_kernel,
        grid=(ncb,),
        in_specs=[
            full((L, n_emb)), full((n_emb, HY_FFN)), full((1, HY_FFN)), full((HY_FFN, HY_FFN)),
            full((1, HY_FFN)), full((1, HY_FFN)),
            pl.BlockSpec((HY_FFN, cblk), lambda j: (0, j)),
            pl.BlockSpec((HY_FFN, cblk), lambda j: (0, ncb + j)),
            pl.BlockSpec((L, cblk), lambda j: (0, j)),
            full((L, L)), full((L, L)),
        ],
        out_specs=(out_spec, out_spec, out_spec),
        out_shape=(out_sds, out_sds, out_sds),
        compiler_params=_params("arbitrary"),
        name=f"hyena_filter_{L}",
    )(emb, w1, b1.reshape(1, HY_FFN), w2, b2.reshape(1, HY_FFN), freq.reshape(1, HY_FFN), w3, w3, decay, fc, fs)


def _hyena_conv_kernel(x0_ref, x1_ref, v_ref, cw0_ref, cw1_ref, cwv_ref, cb0_ref, cb1_ref, cbv_ref,
                       kr_ref, q_ref, krn_ref, ds_ref, fc_ref, fs_ref, gc_ref, gs_ref, o_ref):
    L = x0_ref.shape[0]
    row = lax.broadcasted_iota(jnp.int32, x0_ref.shape, 0)

    def short_conv(u_ref, w_ref, b_ref):
        u = u_ref[...]
        w = w_ref[...]
        prev = jnp.where(row == 0, 0.0, pltpu.roll(u, 1, axis=0))
        nxt = jnp.where(row == L - 1, 0.0, pltpu.roll(u, L - 1, axis=0))
        return prev * w[0:1, :] + u * w[1:2, :] + nxt * w[2:3, :] + b_ref[...]

    x0 = short_conv(x0_ref, cw0_ref, cb0_ref)
    x1 = short_conv(x1_ref, cw1_ref, cb1_ref)
    v = short_conv(v_ref, cwv_ref, cbv_ref)
    zz = v * x1
    zb = zz.astype(BF16)
    ur = jnp.dot(fc_ref[...], zb, preferred_element_type=F32)
    p = jnp.dot(fs_ref[...], zb, preferred_element_type=F32)
    qq = q_ref[...]
    yr = ur * kr_ref[...] - p * qq
    yw = ur * qq + p * krn_ref[...]
    y = jnp.dot(gc_ref[...], yr.astype(BF16), preferred_element_type=F32)
    y = y + jnp.dot(gs_ref[...], yw.astype(BF16), preferred_element_type=F32)
    o_ref[...] = (x0 * (y + zz * ds_ref[...])).astype(o_ref.dtype)


def _hyena_conv(u, conv_w, conv_b, dskip, spectrum, tables, *, latent):
    L = LATENT_LEN if latent else PROMPT_LEN
    n_seq = N_LATENT_SEQ if latent else N_PROMPT_SEQ
    cblk = 256 if latent else 512
    ncb = D_MODEL // cblk
    row0 = (N_PROMPT_TOK // L) if latent else 0
    kr, qq, krn = spectrum
    fc, fs, gc, gs = tables

    def part(p, rows):
        return pl.BlockSpec((rows, cblk), lambda j, s: (0 if rows != L else row0 + s, p * ncb + j))

    def const_cols(rows):
        return pl.BlockSpec((rows, cblk), lambda j, s: (0, j))

    mat = pl.BlockSpec((L, L), lambda j, s: (0, 0))
    conv_b2 = conv_b.reshape(1, 3 * D_MODEL)
    return pl.pallas_call(
        _hyena_conv_kernel,
        grid=(ncb, n_seq),
        in_specs=[part(0, L), part(1, L), part(2, L),
                  part(0, 3), part(1, 3), part(2, 3),
                  part(0, 1), part(1, 1), part(2, 1),
                  const_cols(L), const_cols(L), const_cols(L), const_cols(1),
                  mat, mat, mat, mat],
        out_specs=pl.BlockSpec((L, cblk), lambda j, s: (s, j)),
        out_shape=jax.ShapeDtypeStruct((n_seq * L, D_MODEL), BF16),
        compiler_params=_params("arbitrary", "arbitrary"),
        name="hyena_conv_latent" if latent else "hyena_conv_prompt",
    )(u, u, u, conv_w, conv_w, conv_w, conv_b2, conv_b2, conv_b2,
      kr, qq, krn, dskip.reshape(1, D_MODEL), fc, fs, gc, gs)


def kernel(x_prompt, x_sample, cache_k, cache_v, state_hgrn, c, c_ctx, norm_g, mod_w, mod_b, ab_in_w, hgrn_lb, hgrn_onorm_g, attn_qnorm_g, attn_knorm_g, ab_out_w, hy_in_w, hy_in_b, hy_conv_w, hy_conv_b, hy_f_w1, hy_f_b1, hy_f_w2, hy_f_b2, hy_f_w3, hy_f_freq, hy_dskip, hy_out_w, router_w, router_b, moe_w_gate, moe_w_up, moe_w_down):
    x = jnp.concatenate([x_prompt.reshape(N_PROMPT_TOK, D_MODEL), x_sample.reshape(N_LATENT_TOK, D_MODEL)], axis=0)
    cond = jnp.concatenate([c_ctx[None, :], c, jnp.zeros((N_COND - 1 - N_LATENT_SEQ, D_MODEL), F32)], axis=0)
    mod = _modulation(cond, mod_w, mod_b)
    router_wt = router_w.T

    z = _in_proj(x, norm_g[0, 0], mod[0], ab_in_w[0].astype(BF16), jnp.zeros((AB_IN,), F32))
    oa_p, new_state = _hgrn(z, hgrn_lb, hgrn_onorm_g[0], None, latent=False)
    oa_l = _hgrn(z, hgrn_lb, hgrn_onorm_g[0], state_hgrn, latent=True)
    ob_p, k_prompt = _attention_prompt(z, attn_qnorm_g[0], attn_knorm_g[0])
    ob_l = _attention_latent(z, attn_qnorm_g[0], attn_knorm_g[0], cache_k, cache_v)
    o_a = jnp.concatenate([oa_p, oa_l], axis=0)
    o_b = jnp.concatenate([ob_p, ob_l], axis=0)
    w_out = ab_out_w[0].astype(BF16)
    x, h, logits_t = _out_proj([o_a, o_b], [w_out[:A_WIDTH], w_out[A_WIDTH:]], x, norm_g[0, 1], mod[0], router_wt)
    x = _moe(h, logits_t, router_b, moe_w_gate, moe_w_up, moe_w_down, 0, x, mod[0])

    u = _in_proj(x, norm_g[1, 0], mod[1], hy_in_w[0].astype(BF16), hy_in_b[0])
    pre = []
    for latent in (False, True):
        L = LATENT_LEN if latent else PROMPT_LEN
        tables = _dft_tables(L)
        spectrum = _hyena_filter_spectrum(L, hy_f_w1[0], hy_f_b1[0], hy_f_w2[0], hy_f_b2[0], hy_f_w3[0],
                                          hy_f_freq[0], tables[0], tables[1])
        pre.append(_hyena_conv(u, hy_conv_w[0], hy_conv_b[0], hy_dskip[0], spectrum, tables, latent=latent))
    pre = jnp.concatenate(pre, axis=0)
    x, h, logits_t = _out_proj([pre], [hy_out_w[0].astype(BF16)], x, norm_g[1, 1], mod[1], router_wt)
    x = _moe(h, logits_t, router_b, moe_w_gate, moe_w_up, moe_w_down, 1, x, mod[1])

    y_prompt = x[:N_PROMPT_TOK].reshape(N_PROMPT_SEQ, PROMPT_LEN, D_MODEL)
    y_sample = x[N_PROMPT_TOK:].reshape(N_LATENT_SEQ, LATENT_LEN, D_MODEL)
    kv_shape = (N_PROMPT_SEQ, PROMPT_LEN, KV_HEADS, HEAD_DIM)
    new_k = k_prompt.reshape(kv_shape).transpose(0, 2, 1, 3)[:, None]
    v_col = 5 * A_WIDTH + (Q_HEADS + KV_HEADS) * HEAD_DIM
    new_v = z[:N_PROMPT_TOK, v_col:].reshape(kv_shape).transpose(0, 2, 1, 3)[:, None]
    return (y_prompt, y_sample, new_k, new_v, new_state)
```

```python
import functools
import math

import numpy as np
import jax
import jax.numpy as jnp
from jax import lax
from jax.experimental import pallas as pl
from jax.experimental.pallas import tpu as pltpu
from jax.experimental.pallas import tpu_sc as plsc

F32 = jnp.float32
BF16 = jnp.bfloat16
HIGHEST = lax.Precision.HIGHEST

D_MODEL = 1024
N_PROMPT_SEQ = 32
PROMPT_LEN = 256
N_LATENT_SEQ = 2
LATENT_LEN = 1024
PAST_LEN = 512
GRID_W = 64
N_PROMPT_TOK = N_PROMPT_SEQ * PROMPT_LEN
N_LATENT_TOK = N_LATENT_SEQ * LATENT_LEN
N_TOK = N_PROMPT_TOK + N_LATENT_TOK
N_COND = 8
EPS = 1e-6

A_WIDTH = 512
A_HEADS = 4
A_DK = 128
CHUNK = 64
HGRN_BLOCK = 256
HEAD_DIM = 64
Q_HEADS = 8
KV_HEADS = 2
Q_PER_KV = Q_HEADS // KV_HEADS
Q_BLOCK = 256
ROPE_THETA = 10000.0
ROPE_PAIRS = HEAD_DIM // 4
AB_IN = 5 * A_WIDTH + (Q_HEADS + 2 * KV_HEADS) * HEAD_DIM

HY_BANDS = 16
HY_FFN = 64
HY_DECAY_TARGET = 1e-2
HY_FAST_PCT = 0.3
HY_SLOW_PCT = 1.5

N_EXPERTS = 16
N_GROUPS = 4
EXPERTS_PER_GROUP = 4
TOP_K = 2
D_EXPERT = 512
MOE_TILE = 256
MOE_ROWS = N_TOK * TOP_K + N_EXPERTS * MOE_TILE

SC_CORES = 2
SC_WORKERS = 32
SC_CHUNK = 80
ROW_WORDS = D_MODEL // 2

VMEM_LIMIT = 56 * 1024 * 1024


def _params(*sem):
    return pltpu.CompilerParams(dimension_semantics=sem, vmem_limit_bytes=VMEM_LIMIT)


def _pack_rows(x):
    n = x.shape[1] // 2
    bits = pltpu.bitcast(x.astype(BF16).astype(F32), jnp.uint32)
    return pltpu.bitcast(bits[:, :n] | (bits[:, n:] >> 16), jnp.int32)


def _unpack_rows(p):
    bits = pltpu.bitcast(p, jnp.uint32)
    hi = pltpu.bitcast(bits & jnp.uint32(0xFFFF0000), F32)
    lo = pltpu.bitcast(bits << 16, F32)
    return jnp.concatenate([hi, lo], axis=1)


def _cond_of_token_block(i, block_rows):
    start = i * block_rows
    return jnp.where(start < N_PROMPT_TOK, 0, 1 + (start - N_PROMPT_TOK) // LATENT_LEN)


def _mod_kernel(cond_ref, w_ref, b_ref, o_ref):
    cnd = cond_ref[...]
    s = cnd * jax.nn.sigmoid(cnd)
    o_ref[...] = jnp.dot(s, w_ref[...], precision=HIGHEST, preferred_element_type=F32) + b_ref[...]


def _modulation(cond, mod_w, mod_b):
    depth = mod_w.shape[0]
    n_chunk = 6
    out = pl.pallas_call(
        _mod_kernel,
        grid=(depth, n_chunk),
        in_specs=[
            pl.BlockSpec((N_COND, D_MODEL), lambda l, j: (0, 0)),
            pl.BlockSpec((None, D_MODEL, D_MODEL), lambda l, j: (l, 0, j)),
            pl.BlockSpec((None, 1, D_MODEL), lambda l, j: (l, 0, j)),
        ],
        out_specs=pl.BlockSpec((None, N_COND, D_MODEL), lambda l, j: (l, 0, j)),
        out_shape=jax.ShapeDtypeStruct((depth, N_COND, n_chunk * D_MODEL), F32),
        compiler_params=_params("arbitrary", "arbitrary"),
        name="modulation",
    )(cond, mod_w, mod_b.reshape(depth, 1, n_chunk * D_MODEL))
    return out.reshape(depth, N_COND, n_chunk, D_MODEL)


def _modulated_norm(x, g, mod, shift_row, scale_row):
    ms = jnp.mean(x * x, axis=-1, keepdims=True)
    y = x * lax.rsqrt(ms + EPS) * g
    return y * (1.0 + mod[scale_row:scale_row + 1, :]) + mod[shift_row:shift_row + 1, :]


def _in_proj_kernel(x_ref, g_ref, mod_ref, w_ref, b_ref, o_ref):
    h = _modulated_norm(x_ref[...], g_ref[...], mod_ref[...], 0, 1)
    o_ref[...] = jnp.dot(h.astype(BF16), w_ref[...], preferred_element_type=F32) + b_ref[...]


def _in_proj(x, g, mod_l, w_bf16, bias, block_rows=256):
    n = w_bf16.shape[1]
    return pl.pallas_call(
        _in_proj_kernel,
        grid=(N_TOK // block_rows,),
        in_specs=[
            pl.BlockSpec((block_rows, D_MODEL), lambda i: (i, 0)),
            pl.BlockSpec((1, D_MODEL), lambda i: (0, 0)),
            pl.BlockSpec((None, 6, D_MODEL), lambda i: (_cond_of_token_block(i, block_rows), 0, 0)),
            pl.BlockSpec((D_MODEL, n), lambda i: (0, 0)),
            pl.BlockSpec((1, n), lambda i: (0, 0)),
        ],
        out_specs=pl.BlockSpec((block_rows, n), lambda i: (i, 0)),
        out_shape=jax.ShapeDtypeStruct((N_TOK, n), F32),
        compiler_params=_params("arbitrary"),
        name="in_proj",
    )(x, g.reshape(1, D_MODEL), mod_l, w_bf16, bias.reshape(1, n))


def _hgrn_kernel(*refs, seq_len, with_state):
    if with_state:
        (q_ref, zf_ref, zb_ref, i_ref, ga_ref, lb_ref, og_ref, s0_ref, o_ref, of_ref, ob_ref) = refs
    else:
        (q_ref, zf_ref, zb_ref, i_ref, ga_ref, lb_ref, og_ref, o_ref, s_ref, of_ref, ob_ref) = refs
    n_blocks = seq_len // HGRN_BLOCK
    chunks_per_block = HGRN_BLOCK // CHUNK

    lbr = lb_ref[...]
    mx = jnp.maximum(lbr[0], lbr[1])
    e0 = jnp.exp(lbr[0] - mx)
    e1 = jnp.exp(lbr[1] - mx)
    lb = e0 / (e0 + e1)

    row = lax.broadcasted_iota(jnp.int32, (HGRN_BLOCK, HGRN_BLOCK), 0)
    col = lax.broadcasted_iota(jnp.int32, (HGRN_BLOCK, HGRN_BLOCK), 1)
    same_chunk = (row // CHUNK) == (col // CHUNK)
    nt = (((1,), (1,)), ((), ()))
    tn = (((0,), (0,)), ((), ()))

    def per_chunk_row(x, idx):
        return jnp.concatenate(
            [jnp.broadcast_to(x[n * CHUNK + idx:n * CHUNK + idx + 1, :], (CHUNK, x.shape[1]))
             for n in range(chunks_per_block)], axis=0)

    def in_chunk_cumsum(tri, x):
        hi = x.astype(BF16)
        r1 = x - hi.astype(F32)
        mid = r1.astype(BF16)
        lo = (r1 - mid.astype(F32)).astype(BF16)
        out = jnp.dot(tri, hi, preferred_element_type=F32)
        out = out + jnp.dot(tri, mid, preferred_element_type=F32)
        return out + jnp.dot(tri, lo, preferred_element_type=F32)

    def block(blk, st, z_ref, lbd, forward, out_ref):
        rows = slice(blk * HGRN_BLOCK, (blk + 1) * HGRN_BLOCK)
        keep = (same_chunk & (col <= row)) if forward else (same_chunk & (col >= row))
        tri = jnp.where(keep, 1.0, 0.0).astype(BF16)
        mid = CHUNK // 2 if forward else CHUNK - 1 - CHUNK // 2
        last = CHUNK - 1 if forward else 0
        f = lbd + (1.0 - lbd) * jax.nn.sigmoid(z_ref[rows, :])
        lf = jnp.log(f)
        k = 1.0 - f
        q = q_ref[rows, :]
        vb = i_ref[rows, :].astype(BF16)
        b = in_chunk_cumsum(tri, lf)
        bm = per_chunk_row(b, mid)
        bl = per_chunk_row(b, last)
        qe = (q * jnp.exp(b - bm)).astype(BF16)
        ke = (k * jnp.exp(bm - b)).astype(BF16)
        att = lax.dot_general(qe, ke, nt, preferred_element_type=F32)
        att = jnp.where(keep, att, 0.0)
        o_intra = jnp.dot(att.astype(BF16), vb, preferred_element_type=F32)
        qb = (q * jnp.exp(b)).astype(BF16)
        ks = (k * jnp.exp(bl - b)).astype(BF16)
        decay = jnp.exp(bl)
        order = range(chunks_per_block) if forward else range(chunks_per_block - 1, -1, -1)
        o_inter = [None] * chunks_per_block
        for n in order:
            cr = slice(n * CHUNK, (n + 1) * CHUNK)
            o_inter[n] = lax.dot_general(qb[cr], st.astype(BF16), nt, preferred_element_type=F32)
            upd = lax.dot_general(vb[cr], ks[cr], tn, preferred_element_type=F32)
            st = st * decay[n * CHUNK:n * CHUNK + 1, :] + upd
        out_ref[rows, :] = o_intra + jnp.concatenate(o_inter, axis=0)
        return st

    if with_state:
        st_f, st_b = s0_ref[0].T, s0_ref[1].T
    else:
        st_f, st_b = jnp.zeros((A_DK, A_DK), F32), jnp.zeros((A_DK, A_DK), F32)
    for step in range(n_blocks):
        st_f = block(step, st_f, zf_ref, lb[0:1, :], True, of_ref)
        st_b = block(n_blocks - 1 - step, st_b, zb_ref, lb[1:2, :], False, ob_ref)
    if not with_state:
        s_ref[0] = st_f.T
        s_ref[1] = st_b.T

    o = of_ref[...] + ob_ref[...]
    o = o * lax.rsqrt(jnp.mean(o * o, axis=-1, keepdims=True) + EPS) * og_ref[...]
    ga = ga_ref[...]
    o_ref[...] = (o * (ga * jax.nn.sigmoid(ga))).astype(o_ref.dtype)


def _hgrn(z, hgrn_lb, onorm_g, state, *, latent):
    seq_len = LATENT_LEN if latent else PROMPT_LEN
    n_seq = N_LATENT_SEQ if latent else N_PROMPT_SEQ
    row0 = (N_PROMPT_TOK // seq_len) if latent else 0

    def zspec(col0):
        return pl.BlockSpec((seq_len, A_DK), lambda s, h: (row0 + s, col0 + h))

    in_specs = [zspec(0), zspec(4), zspec(8), zspec(12), zspec(16),
                pl.BlockSpec((2, 2, A_DK), lambda s, h: (0, 0, h)),
                pl.BlockSpec((1, A_DK), lambda s, h: (0, h))]
    args = [z, z, z, z, z, hgrn_lb, onorm_g.reshape(1, A_WIDTH)]
    state_spec = pl.BlockSpec((None, None, 2, None, A_DK, A_DK), lambda s, h: (s, 0, 0, h, 0, 0))
    o_shape = jax.ShapeDtypeStruct((n_seq * seq_len, A_WIDTH), BF16)
    o_spec = pl.BlockSpec((seq_len, A_DK), lambda s, h: (s, h))
    if latent:
        in_specs.append(state_spec)
        args.append(state)
        out_shape, out_specs = o_shape, o_spec
    else:
        out_shape = (o_shape, jax.ShapeDtypeStruct((n_seq, 1, 2, A_HEADS, A_DK, A_DK), F32))
        out_specs = (o_spec, state_spec)
    return pl.pallas_call(
        functools.partial(_hgrn_kernel, seq_len=seq_len, with_state=latent),
        grid=(n_seq, A_HEADS),
        in_specs=in_specs,
        out_specs=out_specs,
        out_shape=out_shape,
        scratch_shapes=[pltpu.VMEM((seq_len, A_DK), F32), pltpu.VMEM((seq_len, A_DK), F32)],
        compiler_params=_params("arbitrary", "arbitrary"),
        name="hgrn_latent" if latent else "hgrn_prompt",
    )(*args)


def _rope_tables():
    pos = np.arange(LATENT_LEN)
    row, colp = pos // GRID_W, pos % GRID_W
    inv = ROPE_THETA ** (-np.arange(ROPE_PAIRS, dtype=np.float32) / ROPE_PAIRS)
    inv = inv.astype(np.float32)
    ang_r = (row.astype(np.float32)[:, None] * inv).astype(np.float32)
    ang_c = (colp.astype(np.float32)[:, None] * inv).astype(np.float32)
    cos = np.concatenate([np.cos(ang_r), np.cos(ang_r), np.cos(ang_c), np.cos(ang_c)], axis=1)
    sin = np.concatenate([-np.sin(ang_r), np.sin(ang_r), -np.sin(ang_c), np.sin(ang_c)], axis=1)
    perm = np.zeros((HEAD_DIM, HEAD_DIM), np.float32)
    for d in range(HEAD_DIM):
        partner = d + ROPE_PAIRS if (d // ROPE_PAIRS) % 2 == 0 else d - ROPE_PAIRS
        perm[partner, d] = 1.0
    return cos.astype(np.float32), sin.astype(np.float32), perm


def _attn_kernel(*refs, latent):
    if latent:
        (q_ref, k_ref, v_ref, qg_ref, kg_ref, cosq_ref, sinq_ref, cosk_ref, sink_ref, perm_ref,
         ck_ref, cv_ref, o_ref) = refs
    else:
        (q_ref, k_ref, v_ref, qg_ref, kg_ref, o_ref, kout_ref) = refs

    def head(x_ref, h, g, cos, sin):
        xh = x_ref[:, h * HEAD_DIM:(h + 1) * HEAD_DIM]
        xh = xh * lax.rsqrt(jnp.mean(xh * xh, axis=-1, keepdims=True) + EPS) * g
        if latent:
            swapped = jnp.dot(xh, perm_ref[...], precision=HIGHEST, preferred_element_type=F32)
            xh = xh * cos + swapped * sin
        return xh

    qg = qg_ref[...]
    kg = kg_ref[...]
    cq = sq = ck = sk = None
    if latent:
        cq, sq, ck, sk = cosq_ref[...], sinq_ref[...], cosk_ref[...], sink_ref[...]
    scale = HEAD_DIM ** -0.5
    n_q = q_ref.shape[0]
    for j in range(KV_HEADS):
        kh = head(k_ref, j, kg, ck, sk)
        if not latent:
            kout_ref[:, j * HEAD_DIM:(j + 1) * HEAD_DIM] = kh
        vh = v_ref[:, j * HEAD_DIM:(j + 1) * HEAD_DIM]
        qs = jnp.concatenate(
            [head(q_ref, j * Q_PER_KV + t, qg, cq, sq) * scale for t in range(Q_PER_KV)], axis=0)
        qs = qs.astype(BF16)
        nt = (((1,), (1,)), ((), ()))
        s_new = lax.dot_general(qs, kh.astype(BF16), nt, preferred_element_type=F32)
        m = jnp.max(s_new, axis=-1, keepdims=True)
        if latent:
            s_old = lax.dot_general(qs, ck_ref[j].astype(BF16), nt, preferred_element_type=F32)
            m = jnp.maximum(m, jnp.max(s_old, axis=-1, keepdims=True))
        p_new = jnp.exp(s_new - m)
        den = jnp.sum(p_new, axis=-1, keepdims=True)
        acc = jnp.dot(p_new.astype(BF16), vh.astype(BF16), preferred_element_type=F32)
        if latent:
            p_old = jnp.exp(s_old - m)
            den = den + jnp.sum(p_old, axis=-1, keepdims=True)
            acc = acc + jnp.dot(p_old.astype(BF16), cv_ref[j].astype(BF16), preferred_element_type=F32)
        out = acc / den
        for t in range(Q_PER_KV):
            hq = j * Q_PER_KV + t
            o_ref[:, hq * HEAD_DIM:(hq + 1) * HEAD_DIM] = out[t * n_q:(t + 1) * n_q, :].astype(o_ref.dtype)


def _attention_prompt(z, qn_g, kn_g):
    L = PROMPT_LEN
    q_col = (5 * A_WIDTH) // (Q_HEADS * HEAD_DIM)
    k_col = (5 * A_WIDTH + Q_HEADS * HEAD_DIM) // (KV_HEADS * HEAD_DIM)
    kv_w = KV_HEADS * HEAD_DIM
    return pl.pallas_call(
        functools.partial(_attn_kernel, latent=False),
        grid=(N_PROMPT_SEQ,),
        in_specs=[
            pl.BlockSpec((L, Q_HEADS * HEAD_DIM), lambda s: (s, q_col)),
            pl.BlockSpec((L, kv_w), lambda s: (s, k_col)),
            pl.BlockSpec((L, kv_w), lambda s: (s, k_col + 1)),
            pl.BlockSpec((1, HEAD_DIM), lambda s: (0, 0)),
            pl.BlockSpec((1, HEAD_DIM), lambda s: (0, 0)),
        ],
        out_specs=(pl.BlockSpec((L, Q_HEADS * HEAD_DIM), lambda s: (s, 0)),
                   pl.BlockSpec((L, kv_w), lambda s: (s, 0))),
        out_shape=(jax.ShapeDtypeStruct((N_PROMPT_TOK, Q_HEADS * HEAD_DIM), BF16),
                   jax.ShapeDtypeStruct((N_PROMPT_TOK, kv_w), F32)),
        compiler_params=_params("arbitrary"),
        name="attn_prompt",
    )(z, z, z, qn_g.reshape(1, HEAD_DIM), kn_g.reshape(1, HEAD_DIM))


def _attention_latent(z, qn_g, kn_g, cache_k, cache_v):
    L = LATENT_LEN
    nqb = L // Q_BLOCK
    q_col = (5 * A_WIDTH) // (Q_HEADS * HEAD_DIM)
    k_col = (5 * A_WIDTH + Q_HEADS * HEAD_DIM) // (KV_HEADS * HEAD_DIM)
    kv_w = KV_HEADS * HEAD_DIM
    qrow0 = N_PROMPT_TOK // Q_BLOCK
    krow0 = N_PROMPT_TOK // L
    cos, sin, perm = _rope_tables()
    cache_spec = pl.BlockSpec((None, None, KV_HEADS, PAST_LEN, HEAD_DIM), lambda s, b: (s, 0, 0, 0, 0))
    return pl.pallas_call(
        functools.partial(_attn_kernel, latent=True),
        grid=(N_LATENT_SEQ, nqb),
        in_specs=[
            pl.BlockSpec((Q_BLOCK, Q_HEADS * HEAD_DIM), lambda s, b: (qrow0 + s * nqb + b, q_col)),
            pl.BlockSpec((L, kv_w), lambda s, b: (krow0 + s, k_col)),
            pl.BlockSpec((L, kv_w), lambda s, b: (krow0 + s, k_col + 1)),
            pl.BlockSpec((1, HEAD_DIM), lambda s, b: (0, 0)),
            pl.BlockSpec((1, HEAD_DIM), lambda s, b: (0, 0)),
            pl.BlockSpec((Q_BLOCK, HEAD_DIM), lambda s, b: (b, 0)),
            pl.BlockSpec((Q_BLOCK, HEAD_DIM), lambda s, b: (b, 0)),
            pl.BlockSpec((L, HEAD_DIM), lambda s, b: (0, 0)),
            pl.BlockSpec((L, HEAD_DIM), lambda s, b: (0, 0)),
            pl.BlockSpec((HEAD_DIM, HEAD_DIM), lambda s, b: (0, 0)),
            cache_spec, cache_spec,
        ],
        out_specs=pl.BlockSpec((Q_BLOCK, Q_HEADS * HEAD_DIM), lambda s, b: (s * nqb + b, 0)),
        out_shape=jax.ShapeDtypeStruct((N_LATENT_TOK, Q_HEADS * HEAD_DIM), BF16),
        compiler_params=_params("arbitrary", "arbitrary"),
        name="attn_latent",
    )(z, z, z, qn_g.reshape(1, HEAD_DIM), kn_g.reshape(1, HEAD_DIM),
      jnp.asarray(cos), jnp.asarray(sin), jnp.asarray(cos), jnp.asarray(sin), jnp.asarray(perm),
      cache_k, cache_v)


def _out_proj_kernel(*refs, n_in):
    a_refs = refs[:n_in]
    w_refs = refs[n_in:2 * n_in]
    x_ref, g_ref, mod_ref, rw_ref, xo_ref, h_ref, lg_ref = refs[2 * n_in:]
    acc = jnp.dot(a_refs[0][...], w_refs[0][...], preferred_element_type=F32)
    for a_ref, w_ref in zip(a_refs[1:], w_refs[1:]):
        acc = acc + jnp.dot(a_ref[...], w_ref[...], preferred_element_type=F32)
    mod = mod_ref[...]
    x = x_ref[...] + mod[2:3, :] * acc
    xo_ref[...] = x
    h = _modulated_norm(x, g_ref[...], mod, 3, 4)
    h_ref[...] = _pack_rows(h)
    lg_ref[...] = lax.dot_general(rw_ref[...], h, (((1,), (1,)), ((), ())), precision=HIGHEST,
                                  preferred_element_type=F32)


def _out_proj(acts, weights, x, g, mod_l, router_wt, block_rows=256):
    n_in = len(acts)
    in_specs = [pl.BlockSpec((block_rows, a.shape[1]), lambda i: (i, 0)) for a in acts]
    in_specs += [pl.BlockSpec(w.shape, lambda i: (0, 0)) for w in weights]
    in_specs += [
        pl.BlockSpec((block_rows, D_MODEL), lambda i: (i, 0)),
        pl.BlockSpec((1, D_MODEL), lambda i: (0, 0)),
        pl.BlockSpec((None, 6, D_MODEL), lambda i: (_cond_of_token_block(i, block_rows), 0, 0)),
        pl.BlockSpec((N_EXPERTS, D_MODEL), lambda i: (0, 0)),
    ]
    return pl.pallas_call(
        functools.partial(_out_proj_kernel, n_in=n_in),
        grid=(N_TOK // block_rows,),
        in_specs=in_specs,
        out_specs=(pl.BlockSpec((block_rows, D_MODEL), lambda i: (i, 0)),
                   pl.BlockSpec((block_rows, ROW_WORDS), lambda i: (i, 0)),
                   pl.BlockSpec((N_EXPERTS, block_rows), lambda i: (0, i))),
        out_shape=(jax.ShapeDtypeStruct((N_TOK, D_MODEL), F32),
                   jax.ShapeDtypeStruct((N_TOK, ROW_WORDS), jnp.int32),
                   jax.ShapeDtypeStruct((N_EXPERTS, N_TOK), F32)),
        compiler_params=_params("arbitrary"),
        name="out_proj",
    )(*acts, *weights, x, g.reshape(1, D_MODEL), mod_l, router_wt)


def _router_kernel(lg_ref, rb_ref, pos_ref, w_ref, plan_ref, rank_ref):
    lg = lg_ref[...]
    ex = jnp.exp(lg - jnp.max(lg, axis=0, keepdims=True))
    scores = ex / jnp.sum(ex, axis=0, keepdims=True)
    biased = scores + rb_ref[...]
    rows = [biased[e:e + 1, :] for e in range(N_EXPERTS)]
    selected = []
    group_score = []
    for gi in range(N_GROUPS):
        r = rows[gi * EXPERTS_PER_GROUP:(gi + 1) * EXPERTS_PER_GROUP]
        total = None
        for i in range(EXPERTS_PER_GROUP):
            rank = None
            for j in range(EXPERTS_PER_GROUP):
                if j == i:
                    continue
                ahead = (r[j] > r[i]) if j > i else (r[j] >= r[i])
                ahead = jnp.where(ahead, 1.0, 0.0)
                rank = ahead if rank is None else rank + ahead
            sel = rank < 1.5
            selected.append(sel)
            contrib = jnp.where(sel, r[i], 0.0)
            total = contrib if total is None else total + contrib
        group_score.append(total)
    best = group_score[0]
    best_group = jnp.zeros_like(best)
    for gi in range(1, N_GROUPS):
        better = group_score[gi] > best
        best_group = jnp.where(better, float(gi), best_group)
        best = jnp.where(better, group_score[gi], best)
    picked = []
    chosen = []
    den = None
    for e in range(N_EXPERTS):
        in_group = best_group == float(e // EXPERTS_PER_GROUP)
        use = jnp.where(selected[e], jnp.where(in_group, 1.0, 0.0), 0.0)
        w = use * scores[e:e + 1, :]
        chosen.append(use)
        picked.append(w)
        den = w if den is None else den + w
    lanes = 128
    n_blk = N_TOK // lanes
    li = lax.broadcasted_iota(jnp.int32, (lanes, lanes), 0)
    lj = lax.broadcasted_iota(jnp.int32, (lanes, lanes), 1)
    prefix = jnp.where(li <= lj, 1.0, 0.0).astype(BF16)
    carry = jnp.zeros((N_EXPERTS, 1), F32)
    for blk in range(n_blk):
        cols = slice(blk * lanes, (blk + 1) * lanes)
        m = jnp.concatenate([chosen[e][:, cols] for e in range(N_EXPERTS)], axis=0)
        incl = jnp.dot(m.astype(BF16), prefix, preferred_element_type=F32)
        rank_ref[:, cols] = incl - m + carry
        carry = carry + incl[:, lanes - 1:lanes]
    count = carry
    padded = jnp.floor((count + float(MOE_TILE - 1)) * (1.0 / MOE_TILE)) * float(MOE_TILE)
    erow = lax.broadcasted_iota(jnp.int32, (N_EXPERTS, 1), 0)
    offset = jnp.zeros((N_EXPERTS, 1), F32)
    for e in range(N_EXPERTS - 1):
        offset = offset + jnp.where(erow > e, padded[e:e + 1, :], 0.0)
    seen = jnp.zeros_like(den)
    pos_a = jnp.zeros_like(den)
    pos_b = jnp.zeros_like(den)
    w_a = jnp.zeros_like(den)
    w_b = jnp.zeros_like(den)
    for e in range(N_EXPERTS):
        pos_e = rank_ref[e:e + 1, :] + offset[e:e + 1, :]
        gate_e = picked[e] / den
        first = jnp.where(seen < 0.5, chosen[e], 0.0) > 0.5
        second = jnp.where(seen > 0.5, chosen[e], 0.0) > 0.5
        pos_a = jnp.where(first, pos_e, pos_a)
        w_a = jnp.where(first, gate_e, w_a)
        pos_b = jnp.where(second, pos_e, pos_b)
        w_b = jnp.where(second, gate_e, w_b)
        seen = seen + chosen[e]
    pos_ref[0:1, :] = pos_a.astype(jnp.int32)
    pos_ref[1:2, :] = pos_b.astype(jnp.int32)
    w_ref[0:1, :] = w_a
    w_ref[1:2, :] = w_b
    start = (lax.broadcasted_iota(jnp.int32, (N_EXPERTS, lanes), 1) * MOE_TILE).astype(F32)
    end = offset + padded
    tile_expert = jnp.sum(jnp.where(end <= start, 1.0, 0.0), axis=0, keepdims=True)
    inside = (offset <= start) & (start < end)
    real = jnp.clip(count - (start - offset), 0.0, float(MOE_TILE))
    tile_rows = jnp.sum(jnp.where(inside, real, 0.0), axis=0, keepdims=True)
    plan_ref[0:1, :] = jnp.minimum(tile_expert, float(N_EXPERTS - 1)).astype(jnp.int32)
    plan_ref[1:2, :] = tile_rows.astype(jnp.int32)


def _router(logits_t, router_b):
    whole = lambda shape: pl.BlockSpec(shape, lambda i: (0, 0))
    return pl.pallas_call(
        _router_kernel,
        grid=(1,),
        in_specs=[whole((N_EXPERTS, N_TOK)), whole((N_EXPERTS, 1))],
        out_specs=(whole((2, N_TOK)), whole((2, N_TOK)), whole((2, 128))),
        out_shape=(jax.ShapeDtypeStruct((2, N_TOK), jnp.int32),
                   jax.ShapeDtypeStruct((2, N_TOK), F32),
                   jax.ShapeDtypeStruct((2, 128), jnp.int32)),
        scratch_shapes=[pltpu.VMEM((N_EXPERTS, N_TOK), F32)],
        compiler_params=_params("arbitrary"),
        name="router",
    )(logits_t, router_b.reshape(N_EXPERTS, 1))


def _sc_mesh():
    return plsc.VectorSubcoreMesh(core_axis_name="c", subcore_axis_name="s")


def _sc_worker_base():
    return (lax.axis_index("s") * SC_CORES + lax.axis_index("c")) * (N_TOK // SC_WORKERS)


def _moe_dispatch(h, pos_a, pos_b):
    n_chunks = N_TOK // SC_WORKERS // SC_CHUNK

    @functools.partial(
        pl.kernel, mesh=_sc_mesh(),
        out_type=jax.ShapeDtypeStruct((MOE_ROWS, ROW_WORDS), jnp.int32),
        scratch_types=[pltpu.VMEM((SC_CHUNK,), jnp.int32), pltpu.VMEM((SC_CHUNK,), jnp.int32),
                       pltpu.VMEM((SC_CHUNK, ROW_WORDS), jnp.int32)],
        name="moe_dispatch",
    )
    def run(h_hbm, pa_hbm, pb_hbm, xs_hbm, ia_v, ib_v, rows_v):
        base = _sc_worker_base()

        @pl.loop(0, n_chunks)
        def _(ci):
            tok = pl.ds(pl.multiple_of(base + ci * SC_CHUNK, SC_CHUNK), SC_CHUNK)
            pltpu.sync_copy(pa_hbm.at[tok], ia_v)
            pltpu.sync_copy(pb_hbm.at[tok], ib_v)
            pltpu.sync_copy(h_hbm.at[tok], rows_v)
            pltpu.sync_copy(rows_v, xs_hbm.at[ia_v])
            pltpu.sync_copy(rows_v, xs_hbm.at[ib_v])

    return run(h, pos_a, pos_b)


def _moe_collect(ys, pos_a, pos_b):
    n_chunks = N_TOK // SC_WORKERS // SC_CHUNK
    out = jax.ShapeDtypeStruct((N_TOK, ROW_WORDS), jnp.int32)

    @functools.partial(
        pl.kernel, mesh=_sc_mesh(), out_type=(out, out),
        scratch_types=[pltpu.VMEM((SC_CHUNK,), jnp.int32), pltpu.VMEM((SC_CHUNK,), jnp.int32),
                       pltpu.VMEM((SC_CHUNK, ROW_WORDS), jnp.int32)],
        name="moe_collect",
    )
    def run(ys_hbm, pa_hbm, pb_hbm, ya_hbm, yb_hbm, ia_v, ib_v, rows_v):
        base = _sc_worker_base()

        @pl.loop(0, n_chunks)
        def _(ci):
            tok = pl.ds(pl.multiple_of(base + ci * SC_CHUNK, SC_CHUNK), SC_CHUNK)
            pltpu.sync_copy(pa_hbm.at[tok], ia_v)
            pltpu.sync_copy(pb_hbm.at[tok], ib_v)
            pltpu.sync_copy(ys_hbm.at[ia_v], rows_v)
            pltpu.sync_copy(rows_v, ya_hbm.at[tok])
            pltpu.sync_copy(ys_hbm.at[ib_v], rows_v)
            pltpu.sync_copy(rows_v, yb_hbm.at[tok])

    return run(ys, pos_a, pos_b)


def _experts_kernel(plan_ref, xs_ref, wg_ref, wu_ref, wd_ref, y_ref, wgb_ref, wub_ref, wdb_ref):
    j = pl.program_id(0)
    expert = plan_ref[j]
    n_real = plan_ref[128 + j]
    fresh = jnp.logical_or(j == 0, expert != plan_ref[jnp.maximum(j - 1, 0)])

    @pl.when(jnp.logical_and(n_real > 0, fresh))
    def _():
        wgb_ref[...] = wg_ref[...].astype(BF16)
        wub_ref[...] = wu_ref[...].astype(BF16)
        wdb_ref[...] = wd_ref[...].astype(BF16)

    @pl.when(n_real > 0)
    def _():
        row = lax.broadcasted_iota(jnp.int32, xs_ref.shape, 0)
        words = jnp.where(row < n_real, xs_ref[...], 0)
        x = _unpack_rows(words).astype(BF16)
        a = jnp.dot(x, wgb_ref[...], preferred_element_type=F32)
        b = jnp.dot(x, wub_ref[...], preferred_element_type=F32)
        hid = (a * jax.nn.sigmoid(a)) * b
        y_ref[...] = _pack_rows(jnp.dot(hid.astype(BF16), wdb_ref[...], preferred_element_type=F32))


def _experts(plan, xs, w_gate, w_up, w_down, layer):
    wspec = lambda r, c: pl.BlockSpec((None, None, r, c), lambda j, plan: (layer, plan[j], 0, 0))
    return pl.pallas_call(
        _experts_kernel,
        grid_spec=pltpu.PrefetchScalarGridSpec(
            num_scalar_prefetch=1,
            grid=(MOE_ROWS // MOE_TILE,),
            in_specs=[pl.BlockSpec((MOE_TILE, ROW_WORDS), lambda j, plan: (j, 0)),
                      wspec(D_MODEL, D_EXPERT), wspec(D_MODEL, D_EXPERT), wspec(D_EXPERT, D_MODEL)],
            out_specs=pl.BlockSpec((MOE_TILE, ROW_WORDS), lambda j, plan: (j, 0)),
            scratch_shapes=[pltpu.VMEM((D_MODEL, D_EXPERT), BF16), pltpu.VMEM((D_MODEL, D_EXPERT), BF16),
                            pltpu.VMEM((D_EXPERT, D_MODEL), BF16)],
        ),
        out_shape=jax.ShapeDtypeStruct((MOE_ROWS, ROW_WORDS), jnp.int32),
        compiler_params=_params("arbitrary"),
        name="experts",
    )(plan, xs, w_gate, w_up, w_down)


def _combine_kernel(x_ref, ya_ref, yb_ref, w_ref, mod_ref, o_ref):
    w = w_ref[...]
    mix = w[:, 0:1] * _unpack_rows(ya_ref[...]) + w[:, 1:2] * _unpack_rows(yb_ref[...])
    o_ref[...] = x_ref[...] + mod_ref[5:6, :] * mix


def _combine(x, ya, yb, w_tok, mod_l, block_rows=512):
    tok = pl.BlockSpec((block_rows, D_MODEL), lambda i: (i, 0))
    packed = pl.BlockSpec((block_rows, ROW_WORDS), lambda i: (i, 0))
    return pl.pallas_call(
        _combine_kernel,
        grid=(N_TOK // block_rows,),
        in_specs=[tok, packed, packed,
                  pl.BlockSpec((block_rows, 2), lambda i: (i, 0)),
                  pl.BlockSpec((None, 6, D_MODEL), lambda i: (_cond_of_token_block(i, block_rows), 0, 0))],
        out_specs=tok,
        out_shape=jax.ShapeDtypeStruct((N_TOK, D_MODEL), F32),
        compiler_params=_params("arbitrary"),
        name="combine",
    )(x, ya, yb, w_tok, mod_l)


def _moe(h, logits_t, router_b, w_gate, w_up, w_down, layer, x, mod_l):
    pos, w, plan = _router(logits_t, router_b)
    xs = _moe_dispatch(h, pos[0], pos[1])
    ys = _experts(plan.reshape(-1), xs, w_gate, w_up, w_down, layer)
    ya, yb = _moe_collect(ys, pos[0], pos[1])
    return _combine(x, ya, yb, w.T, mod_l)


def _dft_tables(L):
    k = np.arange(L)[:, None]
    m = np.arange(L)[None, :]
    r = (k * m) % (2 * L)
    ang = np.pi * r.astype(np.float64) / L
    fc = np.cos(ang)
    fs = np.sin(ang)
    fs[0, :] = np.where(np.arange(L) % 2 == 0, 1.0, -1.0)
    wk = np.full((L, 1), 1.0 / L)
    wk[0, 0] = 0.5 / L
    gc = (fc * wk).T
    gs = (fs * wk).T
    return [jnp.asarray(t.astype(np.float32)).astype(BF16) for t in (fc, fs, gc, gs)]


def _filter_consts(L):
    t = np.linspace(0.0, 1.0, L, dtype=np.float32)[:, None]
    w = (np.float32(2.0 * np.pi) * np.arange(L, dtype=np.float32)[:, None] / np.float32(L)).astype(np.float32)
    fb = np.linspace(1e-4, HY_BANDS - 1, HY_BANDS, dtype=np.float32)[None, :]
    emb = np.concatenate([t, np.cos(fb * w), -np.sin(fb * w)], axis=-1).astype(np.float32)
    lo = math.log(HY_DECAY_TARGET) / HY_SLOW_PCT
    hi = math.log(HY_DECAY_TARGET) / HY_FAST_PCT
    deltas = np.abs(np.linspace(lo, hi, D_MODEL, dtype=np.float32))
    decay = np.exp(-t * deltas).astype(np.float32)
    return jnp.asarray(emb), jnp.asarray(decay)


def _filter_kernel(emb_ref, w1_ref, b1_ref, w2_ref, b2_ref, fr_ref, w3f_ref, w3b_ref, dec_ref,
                   fc_ref, fs_ref, kr_ref, q_ref, krn_ref):
    fr = fr_ref[...]
    hd = jnp.sin(fr * (jnp.dot(emb_ref[...], w1_ref[...], precision=HIGHEST,
                               preferred_element_type=F32) + b1_ref[...]))
    hd = jnp.sin(fr * (jnp.dot(hd, w2_ref[...], precision=HIGHEST,
                               preferred_element_type=F32) + b2_ref[...]))
    dec = dec_ref[...]
    f = jnp.dot(hd, w3f_ref[...], precision=HIGHEST, preferred_element_type=F32) * dec
    g = jnp.dot(hd, w3b_ref[...], precision=HIGHEST, preferred_element_type=F32) * dec
    row = lax.broadcasted_iota(jnp.int32, f.shape, 0)
    g = jnp.where(row == 0, 0.0, g)
    s = f + g
    d = f - g
    kr = jnp.dot(fc_ref[...], s.astype(BF16), preferred_element_type=F32)
    qq = jnp.dot(fs_ref[...], d.astype(BF16), preferred_element_type=F32)
    alt = jnp.where(row % 2 == 0, 1.0, -1.0)
    nyq = jnp.sum(alt * s, axis=0, keepdims=True)
    kr_ref[...] = kr
    q_ref[...] = jnp.where(row == 0, 0.0, qq)
    krn_ref[...] = jnp.where(row == 0, nyq, kr)


def _hyena_filter_spectrum(L, w1, b1, w2, b2, w3, freq, fc, fs, cblk=256):
    emb, decay = _filter_consts(L)
    ncb = D_MODEL // cblk
    n_emb = 128
    emb = jnp.pad(emb, ((0, 0), (0, n_emb - emb.shape[1])))
    w1 = jnp.pad(w1, ((0, n_emb - w1.shape[0]), (0, 0)))
    full = lambda shape: pl.BlockSpec(shape, lambda j: tuple(0 for _ in shape))
    out_sds = jax.ShapeDtypeStruct((L, D_MODEL), F32)
    out_spec = pl.BlockSpec((L, cblk), lambda j: (0, j))
    return pl.pallas_call(
        _filter_kernel,
        grid=(ncb,),
        in_specs=[
            full((L, n_emb)), full((n_emb, HY_FFN)), full((1, HY_FFN)), full((HY_FFN, HY_FFN)),
            full((1, HY_FFN)), full((1, HY_FFN)),
            pl.BlockSpec((HY_FFN, cblk), lambda j: (0, j)),
            pl.BlockSpec((HY_FFN, cblk), lambda j: (0, ncb + j)),
            pl.BlockSpec((L, cblk), lambda j: (0, j)),
            full((L, L)), full((L, L)),
        ],
        out_specs=(out_spec, out_spec, out_spec),
        out_shape=(out_sds, out_sds, out_sds),
        compiler_params=_params("arbitrary"),
        name=f"hyena_filter_{L}",
    )(emb, w1, b1.reshape(1, HY_FFN), w2, b2.reshape(1, HY_FFN), freq.reshape(1, HY_FFN), w3, w3, decay, fc, fs)


def _hyena_conv_kernel(x0_ref, x1_ref, v_ref, cw0_ref, cw1_ref, cwv_ref, cb0_ref, cb1_ref, cbv_ref,
                       kr_ref, q_ref, krn_ref, ds_ref, fc_ref, fs_ref, gc_ref, gs_ref, o_ref):
    L = x0_ref.shape[0]
    row = lax.broadcasted_iota(jnp.int32, x0_ref.shape, 0)

    def short_conv(u_ref, w_ref, b_ref):
        u = u_ref[...]
        w = w_ref[...]
        prev = jnp.where(row == 0, 0.0, pltpu.roll(u, 1, axis=0))
        nxt = jnp.where(row == L - 1, 0.0, pltpu.roll(u, L - 1, axis=0))
        return prev * w[0:1, :] + u * w[1:2, :] + nxt * w[2:3, :] + b_ref[...]

    x0 = short_conv(x0_ref, cw0_ref, cb0_ref)
    x1 = short_conv(x1_ref, cw1_ref, cb1_ref)
    v = short_conv(v_ref, cwv_ref, cbv_ref)
    zz = v * x1
    zb = zz.astype(BF16)
    ur = jnp.dot(fc_ref[...], zb, preferred_element_type=F32)
    p = jnp.dot(fs_ref[...], zb, preferred_element_type=F32)
    qq = q_ref[...]
    yr = ur * kr_ref[...] - p * qq
    yw = ur * qq + p * krn_ref[...]
    y = jnp.dot(gc_ref[...], yr.astype(BF16), preferred_element_type=F32)
    y = y + jnp.dot(gs_ref[...], yw.astype(BF16), preferred_element_type=F32)
    o_ref[...] = (x0 * (y + zz * ds_ref[...])).astype(o_ref.dtype)


def _hyena_conv(u, conv_w, conv_b, dskip, spectrum, tables, *, latent):
    L = LATENT_LEN if latent else PROMPT_LEN
    n_seq = N_LATENT_SEQ if latent else N_PROMPT_SEQ
    cblk = 256 if latent else 512
    ncb = D_MODEL // cblk
    row0 = (N_PROMPT_TOK // L) if latent else 0
    kr, qq, krn = spectrum
    fc, fs, gc, gs = tables

    def part(p, rows):
        return pl.BlockSpec((rows, cblk), lambda j, s: (0 if rows != L else row0 + s, p * ncb + j))

    def const_cols(rows):
        return pl.BlockSpec((rows, cblk), lambda j, s: (0, j))

    mat = pl.BlockSpec((L, L), lambda j, s: (0, 0))
    conv_b2 = conv_b.reshape(1, 3 * D_MODEL)
    return pl.pallas_call(
        _hyena_conv_kernel,
        grid=(ncb, n_seq),
        in_specs=[part(0, L), part(1, L), part(2, L),
                  part(0, 3), part(1, 3), part(2, 3),
                  part(0, 1), part(1, 1), part(2, 1),
                  const_cols(L), const_cols(L), const_cols(L), const_cols(1),
                  mat, mat, mat, mat],
        out_specs=pl.BlockSpec((L, cblk), lambda j, s: (s, j)),
        out_shape=jax.ShapeDtypeStruct((n_seq * L, D_MODEL), BF16),
        compiler_params=_params("arbitrary", "arbitrary"),
        name="hyena_conv_latent" if latent else "hyena_conv_prompt",
    )(u, u, u, conv_w, conv_w, conv_w, conv_b2, conv_b2, conv_b2,
      kr, qq, krn, dskip.reshape(1, D_MODEL), fc, fs, gc, gs)


def kernel(x_prompt, x_sample, cache_k, cache_v, state_hgrn, c, c_ctx, norm_g, mod_w, mod_b, ab_in_w, hgrn_lb, hgrn_onorm_g, attn_qnorm_g, attn_knorm_g, ab_out_w, hy_in_w, hy_in_b, hy_conv_w, hy_conv_b, hy_f_w1, hy_f_b1, hy_f_w2, hy_f_b2, hy_f_w3, hy_f_freq, hy_dskip, hy_out_w, router_w, router_b, moe_w_gate, moe_w_up, moe_w_down):
    x = jnp.concatenate([x_prompt.reshape(N_PROMPT_TOK, D_MODEL), x_sample.reshape(N_LATENT_TOK, D_MODEL)], axis=0)
    cond = jnp.concatenate([c_ctx[None, :], c, jnp.zeros((N_COND - 1 - N_LATENT_SEQ, D_MODEL), F32)], axis=0)
    mod = _modulation(cond, mod_w, mod_b)
    router_wt = router_w.T

    z = _in_proj(x, norm_g[0, 0], mod[0], ab_in_w[0].astype(BF16), jnp.zeros((AB_IN,), F32))
    oa_p, new_state = _hgrn(z, hgrn_lb, hgrn_onorm_g[0], None, latent=False)
    oa_l = _hgrn(z, hgrn_lb, hgrn_onorm_g[0], state_hgrn, latent=True)
    ob_p, k_prompt = _attention_prompt(z, attn_qnorm_g[0], attn_knorm_g[0])
    ob_l = _attention_latent(z, attn_qnorm_g[0], attn_knorm_g[0], cache_k, cache_v)
    o_a = jnp.concatenate([oa_p, oa_l], axis=0)
    o_b = jnp.concatenate([ob_p, ob_l], axis=0)
    w_out = ab_out_w[0].astype(BF16)
    x, h, logits_t = _out_proj([o_a, o_b], [w_out[:A_WIDTH], w_out[A_WIDTH:]], x, norm_g[0, 1], mod[0], router_wt)
    x = _moe(h, logits_t, router_b, moe_w_gate, moe_w_up, moe_w_down, 0, x, mod[0])

    u = _in_proj(x, norm_g[1, 0], mod[1], hy_in_w[0].astype(BF16), hy_in_b[0])
    pre = []
    for latent in (False, True):
        L = LATENT_LEN if latent else PROMPT_LEN
        tables = _dft_tables(L)
        spectrum = _hyena_filter_spectrum(L, hy_f_w1[0], hy_f_b1[0], hy_f_w2[0], hy_f_b2[0], hy_f_w3[0],
                                          hy_f_freq[0], tables[0], tables[1])
        pre.append(_hyena_conv(u, hy_conv_w[0], hy_conv_b[0], hy_dskip[0], spectrum, tables, latent=latent))
    pre = jnp.concatenate(pre, axis=0)
    x, h, logits_t = _out_proj([pre], [hy_out_w[0].astype(BF16)], x, norm_g[1, 1], mod[1], router_wt)
    x = _moe(h, logits_t, router_b, moe_w_gate, moe_w_up, moe_w_down, 1, x, mod[1])

    y_prompt = x[:N_PROMPT_TOK].reshape(N_PROMPT_SEQ, PROMPT_LEN, D_MODEL)
    y_sample = x[N_PROMPT_TOK:].reshape(N_LATENT_SEQ, LATENT_LEN, D_MODEL)
    kv_shape = (N_PROMPT_SEQ, PROMPT_LEN, KV_HEADS, HEAD_DIM)
    new_k = k_prompt.reshape(kv_shape).transpose(0, 2, 1, 3)[:, None]
    v_col = 5 * A_WIDTH + (Q_HEADS + KV_HEADS) * HEAD_DIM
    new_v = z[:N_PROMPT_TOK, v_col:].reshape(kv_shape).transpose(0, 2, 1, 3)[:, None]
    return (y_prompt, y_sample, new_k, new_v, new_state)
```

```python
import functools
import math

import numpy as np
import jax
import jax.numpy as jnp
from jax import lax
from jax.experimental import pallas as pl
from jax.experimental.pallas import tpu as pltpu
from jax.experimental.pallas import tpu_sc as plsc

F32 = jnp.float32
BF16 = jnp.bfloat16
HIGHEST = lax.Precision.HIGHEST

D_MODEL = 1024
N_PROMPT_SEQ = 32
PROMPT_LEN = 256
N_LATENT_SEQ = 2
LATENT_LEN = 1024
PAST_LEN = 512
GRID_W = 64
N_PROMPT_TOK = N_PROMPT_SEQ * PROMPT_LEN
N_LATENT_TOK = N_LATENT_SEQ * LATENT_LEN
N_TOK = N_PROMPT_TOK + N_LATENT_TOK
N_COND = 8
EPS = 1e-6

A_WIDTH = 512
A_HEADS = 4
A_DK = 128
CHUNK = 64
HGRN_BLOCK = 256
HEAD_DIM = 64
Q_HEADS = 8
KV_HEADS = 2
Q_PER_KV = Q_HEADS // KV_HEADS
Q_BLOCK = 256
ROPE_THETA = 10000.0
ROPE_PAIRS = HEAD_DIM // 4
AB_IN = 5 * A_WIDTH + (Q_HEADS + 2 * KV_HEADS) * HEAD_DIM

HY_BANDS = 16
HY_FFN = 64
HY_DECAY_TARGET = 1e-2
HY_FAST_PCT = 0.3
HY_SLOW_PCT = 1.5

N_EXPERTS = 16
N_GROUPS = 4
EXPERTS_PER_GROUP = 4
TOP_K = 2
D_EXPERT = 512
MOE_TILE = 256
MOE_ROWS = N_TOK * TOP_K + N_EXPERTS * MOE_TILE

SC_CORES = 2
SC_WORKERS = 32
SC_CHUNK = 80
ROW_WORDS = D_MODEL // 2

VMEM_LIMIT = 56 * 1024 * 1024


def _params(*sem):
    return pltpu.CompilerParams(dimension_semantics=sem, vmem_limit_bytes=VMEM_LIMIT)


def _pack_rows(x):
    n = x.shape[1] // 2
    bits = pltpu.bitcast(x.astype(BF16).astype(F32), jnp.uint32)
    return pltpu.bitcast(bits[:, :n] | (bits[:, n:] >> 16), jnp.int32)


def _unpack_rows(p):
    bits = pltpu.bitcast(p, jnp.uint32)
    hi = pltpu.bitcast(bits & jnp.uint32(0xFFFF0000), F32)
    lo = pltpu.bitcast(bits << 16, F32)
    return jnp.concatenate([hi, lo], axis=1)


def _cond_of_token_block(i, block_rows):
    start = i * block_rows
    return jnp.where(start < N_PROMPT_TOK, 0, 1 + (start - N_PROMPT_TOK) // LATENT_LEN)


def _mod_kernel(cond_ref, w_ref, b_ref, o_ref):
    cnd = cond_ref[...]
    s = cnd * jax.nn.sigmoid(cnd)
    o_ref[...] = jnp.dot(s, w_ref[...], precision=HIGHEST, preferred_element_type=F32) + b_ref[...]


def _modulation(cond, mod_w, mod_b):
    depth = mod_w.shape[0]
    n_chunk = 6
    out = pl.pallas_call(
        _mod_kernel,
        grid=(depth, n_chunk),
        in_specs=[
            pl.BlockSpec((N_COND, D_MODEL), lambda l, j: (0, 0)),
            pl.BlockSpec((None, D_MODEL, D_MODEL), lambda l, j: (l, 0, j)),
            pl.BlockSpec((None, 1, D_MODEL), lambda l, j: (l, 0, j)),
        ],
        out_specs=pl.BlockSpec((None, N_COND, D_MODEL), lambda l, j: (l, 0, j)),
        out_shape=jax.ShapeDtypeStruct((depth, N_COND, n_chunk * D_MODEL), F32),
        compiler_params=_params("arbitrary", "arbitrary"),
        name="modulation",
    )(cond, mod_w, mod_b.reshape(depth, 1, n_chunk * D_MODEL))
    return out.reshape(depth, N_COND, n_chunk, D_MODEL)


def _modulated_norm(x, g, mod, shift_row, scale_row):
    ms = jnp.mean(x * x, axis=-1, keepdims=True)
    y = x * lax.rsqrt(ms + EPS) * g
    return y * (1.0 + mod[scale_row:scale_row + 1, :]) + mod[shift_row:shift_row + 1, :]


def _trunk_specs(block_rows, width):
    n_prompt_blocks = N_PROMPT_TOK // block_rows
    return (pl.BlockSpec((block_rows, width), lambda i: (jnp.minimum(i, n_prompt_blocks - 1), 0)),
            pl.BlockSpec((block_rows, width), lambda i: (jnp.maximum(i - n_prompt_blocks, 0), 0)))


def _select_trunk(p_ref, l_ref):
    block_rows = p_ref.shape[0]
    return jnp.where(pl.program_id(0) < N_PROMPT_TOK // block_rows, p_ref[...], l_ref[...])


def _cast_once(w_ref, wb_ref):
    @pl.when(pl.program_id(0) == 0)
    def _():
        wb_ref[...] = w_ref[...].astype(BF16)


def _resident(shape):
    return pl.BlockSpec(shape, lambda i: tuple(0 for _ in shape), pipeline_mode=pl.Buffered(1))


def _mod_spec(block_rows):
    return pl.BlockSpec((None, 6, D_MODEL), lambda i: (_cond_of_token_block(i, block_rows), 0, 0))


def _in_proj0_kernel(xp_ref, xl_ref, g_ref, mod_ref, w_ref, o_ref, wb_ref):
    _cast_once(w_ref, wb_ref)
    h = _modulated_norm(_select_trunk(xp_ref, xl_ref), g_ref[...], mod_ref[...], 0, 1)
    o_ref[...] = jnp.dot(h.astype(BF16), wb_ref[...], preferred_element_type=F32)


def _in_proj0(x_prompt, x_latent, g, mod_l, w, block_rows=256):
    n = w.shape[1]
    return pl.pallas_call(
        _in_proj0_kernel,
        grid=(N_TOK // block_rows,),
        in_specs=[*_trunk_specs(block_rows, D_MODEL), _resident((1, D_MODEL)), _mod_spec(block_rows),
                  _resident((D_MODEL, n))],
        out_specs=pl.BlockSpec((block_rows, n), lambda i: (i, 0)),
        out_shape=jax.ShapeDtypeStruct((N_TOK, n), F32),
        scratch_shapes=[pltpu.VMEM((D_MODEL, n), BF16)],
        compiler_params=_params("arbitrary"),
        name="in_proj0",
    )(x_prompt, x_latent, g.reshape(1, D_MODEL), mod_l, w)


def _moe_mix(x_ref, ya_ref, yb_ref, wt_ref, mod_ref):
    wt = wt_ref[...]
    mix = wt[:, 0:1] * _unpack_rows(ya_ref[...]) + wt[:, 1:2] * _unpack_rows(yb_ref[...])
    return x_ref[...] + mod_ref[5:6, :] * mix


def _in_proj1_kernel(x_ref, ya_ref, yb_ref, wt_ref, modp_ref, g_ref, mod_ref, w_ref, b_ref, xo_ref, o_ref, wb_ref):
    _cast_once(w_ref, wb_ref)
    x = _moe_mix(x_ref, ya_ref, yb_ref, wt_ref, modp_ref)
    xo_ref[...] = x
    h = _modulated_norm(x, g_ref[...], mod_ref[...], 0, 1)
    o_ref[...] = jnp.dot(h.astype(BF16), wb_ref[...], preferred_element_type=F32) + b_ref[...]


def _in_proj1(x, moe_out, mod_prev, g, mod_l, w, bias, block_rows=256):
    ya, yb, w_tok = moe_out
    n = w.shape[1]
    tok = pl.BlockSpec((block_rows, D_MODEL), lambda i: (i, 0))
    packed = pl.BlockSpec((block_rows, ROW_WORDS), lambda i: (i, 0))
    return pl.pallas_call(
        _in_proj1_kernel,
        grid=(N_TOK // block_rows,),
        in_specs=[tok, packed, packed, pl.BlockSpec((block_rows, TOP_K), lambda i: (i, 0)), _mod_spec(block_rows),
                  _resident((1, D_MODEL)), _mod_spec(block_rows), _resident((D_MODEL, n)), _resident((1, n))],
        out_specs=(tok, pl.BlockSpec((block_rows, n), lambda i: (i, 0))),
        out_shape=(jax.ShapeDtypeStruct((N_TOK, D_MODEL), F32), jax.ShapeDtypeStruct((N_TOK, n), F32)),
        scratch_shapes=[pltpu.VMEM((D_MODEL, n), BF16)],
        compiler_params=_params("arbitrary"),
        name="in_proj1",
    )(x, ya, yb, w_tok, mod_prev, g.reshape(1, D_MODEL), mod_l, w, bias.reshape(1, n))


def _hgrn_kernel(*refs, seq_len, with_state):
    if with_state:
        (q_ref, zf_ref, zb_ref, i_ref, ga_ref, lb_ref, og_ref, s0_ref, shared_ref, o_ref, of_ref, ob_ref) = refs
    else:
        (q_ref, zf_ref, zb_ref, i_ref, ga_ref, lb_ref, og_ref, o_ref, s_ref, of_ref, ob_ref) = refs
    n_blocks = seq_len // HGRN_BLOCK
    chunks_per_block = HGRN_BLOCK // CHUNK

    lbr = lb_ref[...]
    mx = jnp.maximum(lbr[0], lbr[1])
    e0 = jnp.exp(lbr[0] - mx)
    e1 = jnp.exp(lbr[1] - mx)
    lb = e0 / (e0 + e1)

    row = lax.broadcasted_iota(jnp.int32, (HGRN_BLOCK, HGRN_BLOCK), 0)
    col = lax.broadcasted_iota(jnp.int32, (HGRN_BLOCK, HGRN_BLOCK), 1)
    same_chunk = (row // CHUNK) == (col // CHUNK)
    nt = (((1,), (1,)), ((), ()))
    tn = (((0,), (0,)), ((), ()))

    def per_chunk_row(x, idx):
        return jnp.concatenate(
            [jnp.broadcast_to(x[n * CHUNK + idx:n * CHUNK + idx + 1, :], (CHUNK, x.shape[1]))
             for n in range(chunks_per_block)], axis=0)

    def in_chunk_cumsum(tri, x):
        hi = x.astype(BF16)
        r1 = x - hi.astype(F32)
        mid = r1.astype(BF16)
        lo = (r1 - mid.astype(F32)).astype(BF16)
        out = jnp.dot(tri, hi, preferred_element_type=F32)
        out = out + jnp.dot(tri, mid, preferred_element_type=F32)
        return out + jnp.dot(tri, lo, preferred_element_type=F32)

    def block(blk, st, z_ref, lbd, forward, out_ref):
        rows = slice(blk * HGRN_BLOCK, (blk + 1) * HGRN_BLOCK)
        keep = (same_chunk & (col <= row)) if forward else (same_chunk & (col >= row))
        tri = jnp.where(keep, 1.0, 0.0).astype(BF16)
        mid = CHUNK // 2 if forward else CHUNK - 1 - CHUNK // 2
        last = CHUNK - 1 if forward else 0
        f = lbd + (1.0 - lbd) * jax.nn.sigmoid(z_ref[rows, :])
        lf = jnp.log(f)
        k = 1.0 - f
        q = q_ref[rows, :]
        vb = i_ref[rows, :].astype(BF16)
        b = in_chunk_cumsum(tri, lf)
        bm = per_chunk_row(b, mid)
        bl = per_chunk_row(b, last)
        qe = (q * jnp.exp(b - bm)).astype(BF16)
        ke = (k * jnp.exp(bm - b)).astype(BF16)
        att = lax.dot_general(qe, ke, nt, preferred_element_type=F32)
        att = jnp.where(keep, att, 0.0)
        o_intra = jnp.dot(att.astype(BF16), vb, preferred_element_type=F32)
        qb = (q * jnp.exp(b)).astype(BF16)
        ks = (k * jnp.exp(bl - b)).astype(BF16)
        decay = jnp.exp(bl)
        order = range(chunks_per_block) if forward else range(chunks_per_block - 1, -1, -1)
        o_inter = [None] * chunks_per_block
        for n in order:
            cr = slice(n * CHUNK, (n + 1) * CHUNK)
            o_inter[n] = lax.dot_general(qb[cr], st.astype(BF16), nt, preferred_element_type=F32)
            upd = lax.dot_general(vb[cr], ks[cr], tn, preferred_element_type=F32)
            st = st * decay[n * CHUNK:n * CHUNK + 1, :] + upd
        out_ref[rows, :] = o_intra + jnp.concatenate(o_inter, axis=0)
        return st

    if with_state:
        st_f, st_b = s0_ref[0].T, s0_ref[1].T
    else:
        st_f, st_b = jnp.zeros((A_DK, A_DK), F32), jnp.zeros((A_DK, A_DK), F32)
    for step in range(n_blocks):
        st_f = block(step, st_f, zf_ref, lb[0:1, :], True, of_ref)
        st_b = block(n_blocks - 1 - step, st_b, zb_ref, lb[1:2, :], False, ob_ref)
    if not with_state:
        s_ref[0] = st_f.T
        s_ref[1] = st_b.T

    o = of_ref[...] + ob_ref[...]
    o = o * lax.rsqrt(jnp.mean(o * o, axis=-1, keepdims=True) + EPS) * og_ref[...]
    ga = ga_ref[...]
    o_ref[...] = (o * (ga * jax.nn.sigmoid(ga))).astype(o_ref.dtype)


def _hgrn(z, hgrn_lb, onorm_g, state, shared, *, latent):
    seq_len = LATENT_LEN if latent else PROMPT_LEN
    n_seq = N_LATENT_SEQ if latent else N_PROMPT_SEQ
    row0 = (N_PROMPT_TOK // seq_len) if latent else 0

    def zspec(col0):
        return pl.BlockSpec((seq_len, A_DK), lambda s, h: (row0 + s, col0 + h))

    in_specs = [zspec(0), zspec(4), zspec(8), zspec(12), zspec(16),
                pl.BlockSpec((2, 2, A_DK), lambda s, h: (0, 0, h)),
                pl.BlockSpec((1, A_DK), lambda s, h: (0, h))]
    args = [z, z, z, z, z, hgrn_lb, onorm_g.reshape(1, A_WIDTH)]
    state_spec = pl.BlockSpec((None, None, 2, None, A_DK, A_DK), lambda s, h: (s, 0, 0, h, 0, 0))
    o_shape = jax.ShapeDtypeStruct((N_TOK, A_WIDTH), BF16)
    o_spec = pl.BlockSpec((seq_len, A_DK), lambda s, h: (row0 + s, h))
    aliases = {}
    if latent:
        in_specs += [state_spec, pl.BlockSpec(memory_space=pl.ANY)]
        args += [state, shared]
        aliases = {len(args) - 1: 0}
        out_shape, out_specs = o_shape, o_spec
    else:
        out_shape = (o_shape, jax.ShapeDtypeStruct((n_seq, 1, 2, A_HEADS, A_DK, A_DK), F32))
        out_specs = (o_spec, state_spec)
    return pl.pallas_call(
        functools.partial(_hgrn_kernel, seq_len=seq_len, with_state=latent),
        grid=(n_seq, A_HEADS),
        in_specs=in_specs,
        out_specs=out_specs,
        out_shape=out_shape,
        input_output_aliases=aliases,
        scratch_shapes=[pltpu.VMEM((seq_len, A_DK), F32), pltpu.VMEM((seq_len, A_DK), F32)],
        compiler_params=_params("arbitrary", "arbitrary"),
        name="hgrn_latent" if latent else "hgrn_prompt",
    )(*args)


def _rope_tables():
    pos = np.arange(LATENT_LEN)
    row, colp = pos // GRID_W, pos % GRID_W
    inv = ROPE_THETA ** (-np.arange(ROPE_PAIRS, dtype=np.float32) / ROPE_PAIRS)
    inv = inv.astype(np.float32)
    ang_r = (row.astype(np.float32)[:, None] * inv).astype(np.float32)
    ang_c = (colp.astype(np.float32)[:, None] * inv).astype(np.float32)
    cos = np.concatenate([np.cos(ang_r), np.cos(ang_r), np.cos(ang_c), np.cos(ang_c)], axis=1)
    sin = np.concatenate([-np.sin(ang_r), np.sin(ang_r), -np.sin(ang_c), np.sin(ang_c)], axis=1)
    perm = np.zeros((HEAD_DIM, HEAD_DIM), np.float32)
    for d in range(HEAD_DIM):
        partner = d + ROPE_PAIRS if (d // ROPE_PAIRS) % 2 == 0 else d - ROPE_PAIRS
        perm[partner, d] = 1.0
    return cos.astype(np.float32), sin.astype(np.float32), perm


def _attn_kernel(*refs, latent):
    if latent:
        (q_ref, k_ref, v_ref, qg_ref, kg_ref, cosq_ref, sinq_ref, cosk_ref, sink_ref, perm_ref,
         ck_ref, cv_ref, shared_ref, o_ref) = refs
    else:
        (q_ref, k_ref, v_ref, qg_ref, kg_ref, o_ref, kout_ref) = refs

    def head(x_ref, h, g, cos, sin):
        xh = x_ref[:, h * HEAD_DIM:(h + 1) * HEAD_DIM]
        xh = xh * lax.rsqrt(jnp.mean(xh * xh, axis=-1, keepdims=True) + EPS) * g
        if latent:
            swapped = jnp.dot(xh, perm_ref[...], precision=HIGHEST, preferred_element_type=F32)
            xh = xh * cos + swapped * sin
        return xh

    qg = qg_ref[...]
    kg = kg_ref[...]
    cq = sq = ck = sk = None
    if latent:
        cq, sq, ck, sk = cosq_ref[...], sinq_ref[...], cosk_ref[...], sink_ref[...]
    scale = HEAD_DIM ** -0.5
    n_q = q_ref.shape[0]
    for j in range(KV_HEADS):
        kh = head(k_ref, j, kg, ck, sk)
        if not latent:
            kout_ref[:, j * HEAD_DIM:(j + 1) * HEAD_DIM] = kh
        vh = v_ref[:, j * HEAD_DIM:(j + 1) * HEAD_DIM]
        qs = jnp.concatenate(
            [head(q_ref, j * Q_PER_KV + t, qg, cq, sq) * scale for t in range(Q_PER_KV)], axis=0)
        qs = qs.astype(BF16)
        nt = (((1,), (1,)), ((), ()))
        s_new = lax.dot_general(qs, kh.astype(BF16), nt, preferred_element_type=F32)
        m = jnp.max(s_new, axis=-1, keepdims=True)
        if latent:
            s_old = lax.dot_general(qs, ck_ref[j].astype(BF16), nt, preferred_element_type=F32)
            m = jnp.maximum(m, jnp.max(s_old, axis=-1, keepdims=True))
        p_new = jnp.exp(s_new - m)
        den = jnp.sum(p_new, axis=-1, keepdims=True)
        acc = jnp.dot(p_new.astype(BF16), vh.astype(BF16), preferred_element_type=F32)
        if latent:
            p_old = jnp.exp(s_old - m)
            den = den + jnp.sum(p_old, axis=-1, keepdims=True)
            acc = acc + jnp.dot(p_old.astype(BF16), cv_ref[j].astype(BF16), preferred_element_type=F32)
        out = acc / den
        for t in range(Q_PER_KV):
            hq = j * Q_PER_KV + t
            o_ref[:, hq * HEAD_DIM:(hq + 1) * HEAD_DIM] = out[t * n_q:(t + 1) * n_q, :].astype(o_ref.dtype)


def _attention_prompt(z, qn_g, kn_g):
    L = PROMPT_LEN
    q_col = (5 * A_WIDTH) // (Q_HEADS * HEAD_DIM)
    k_col = (5 * A_WIDTH + Q_HEADS * HEAD_DIM) // (KV_HEADS * HEAD_DIM)
    kv_w = KV_HEADS * HEAD_DIM
    return pl.pallas_call(
        functools.partial(_attn_kernel, latent=False),
        grid=(N_PROMPT_SEQ,),
        in_specs=[
            pl.BlockSpec((L, Q_HEADS * HEAD_DIM), lambda s: (s, q_col)),
            pl.BlockSpec((L, kv_w), lambda s: (s, k_col)),
            pl.BlockSpec((L, kv_w), lambda s: (s, k_col + 1)),
            pl.BlockSpec((1, HEAD_DIM), lambda s: (0, 0)),
            pl.BlockSpec((1, HEAD_DIM), lambda s: (0, 0)),
        ],
        out_specs=(pl.BlockSpec((L, Q_HEADS * HEAD_DIM), lambda s: (s, 0)),
                   pl.BlockSpec((L, kv_w), lambda s: (s, 0))),
        out_shape=(jax.ShapeDtypeStruct((N_TOK, Q_HEADS * HEAD_DIM), BF16),
                   jax.ShapeDtypeStruct((N_PROMPT_TOK, kv_w), F32)),
        compiler_params=_params("arbitrary"),
        name="attn_prompt",
    )(z, z, z, qn_g.reshape(1, HEAD_DIM), kn_g.reshape(1, HEAD_DIM))


def _attention_latent(z, qn_g, kn_g, cache_k, cache_v, shared):
    L = LATENT_LEN
    nqb = L // Q_BLOCK
    q_col = (5 * A_WIDTH) // (Q_HEADS * HEAD_DIM)
    k_col = (5 * A_WIDTH + Q_HEADS * HEAD_DIM) // (KV_HEADS * HEAD_DIM)
    kv_w = KV_HEADS * HEAD_DIM
    qrow0 = N_PROMPT_TOK // Q_BLOCK
    krow0 = N_PROMPT_TOK // L
    cos, sin, perm = _rope_tables()
    cache_spec = pl.BlockSpec((None, None, KV_HEADS, PAST_LEN, HEAD_DIM), lambda s, b: (s, 0, 0, 0, 0))
    return pl.pallas_call(
        functools.partial(_attn_kernel, latent=True),
        grid=(N_LATENT_SEQ, nqb),
        in_specs=[
            pl.BlockSpec((Q_BLOCK, Q_HEADS * HEAD_DIM), lambda s, b: (qrow0 + s * nqb + b, q_col)),
            pl.BlockSpec((L, kv_w), lambda s, b: (krow0 + s, k_col)),
            pl.BlockSpec((L, kv_w), lambda s, b: (krow0 + s, k_col + 1)),
            pl.BlockSpec((1, HEAD_DIM), lambda s, b: (0, 0)),
            pl.BlockSpec((1, HEAD_DIM), lambda s, b: (0, 0)),
            pl.BlockSpec((Q_BLOCK, HEAD_DIM), lambda s, b: (b, 0)),
            pl.BlockSpec((Q_BLOCK, HEAD_DIM), lambda s, b: (b, 0)),
            pl.BlockSpec((L, HEAD_DIM), lambda s, b: (0, 0)),
            pl.BlockSpec((L, HEAD_DIM), lambda s, b: (0, 0)),
            pl.BlockSpec((HEAD_DIM, HEAD_DIM), lambda s, b: (0, 0)),
            cache_spec, cache_spec, pl.BlockSpec(memory_space=pl.ANY),
        ],
        out_specs=pl.BlockSpec((Q_BLOCK, Q_HEADS * HEAD_DIM), lambda s, b: (qrow0 + s * nqb + b, 0)),
        out_shape=jax.ShapeDtypeStruct((N_TOK, Q_HEADS * HEAD_DIM), BF16),
        input_output_aliases={12: 0},
        compiler_params=_params("arbitrary", "arbitrary"),
        name="attn_latent",
    )(z, z, z, qn_g.reshape(1, HEAD_DIM), kn_g.reshape(1, HEAD_DIM),
      jnp.asarray(cos), jnp.asarray(sin), jnp.asarray(cos), jnp.asarray(sin), jnp.asarray(perm),
      cache_k, cache_v, shared)


def _out_proj_kernel(*refs, n_act, n_x):
    a_refs = refs[:n_act]
    x_refs = refs[n_act:n_act + n_x]
    g_ref, mod_ref, rw_ref, w_ref, xo_ref, h_ref, lg_ref, wb_ref = refs[n_act + n_x:]
    _cast_once(w_ref, wb_ref)
    acc = None
    k0 = 0
    for a_ref in a_refs:
        k1 = k0 + a_ref.shape[1]
        part = jnp.dot(a_ref[...], wb_ref[k0:k1, :], preferred_element_type=F32)
        acc = part if acc is None else acc + part
        k0 = k1
    mod = mod_ref[...]
    x_in = x_refs[0][...] if n_x == 1 else _select_trunk(*x_refs)
    x = x_in + mod[2:3, :] * acc
    xo_ref[...] = x
    h = _modulated_norm(x, g_ref[...], mod, 3, 4)
    h_ref[...] = _pack_rows(h)
    lg_ref[...] = lax.dot_general(rw_ref[...], h, (((1,), (1,)), ((), ())), precision=HIGHEST,
                                  preferred_element_type=F32)


def _out_proj(acts, w, xs, g, mod_l, router_wt, block_rows=256):
    tok = lambda width: pl.BlockSpec((block_rows, width), lambda i: (i, 0))
    in_specs = [tok(a.shape[1]) for a in acts]
    in_specs += [tok(D_MODEL)] if len(xs) == 1 else list(_trunk_specs(block_rows, D_MODEL))
    in_specs += [_resident((1, D_MODEL)), _mod_spec(block_rows), _resident((N_EXPERTS, D_MODEL)),
                 _resident(w.shape)]
    return pl.pallas_call(
        functools.partial(_out_proj_kernel, n_act=len(acts), n_x=len(xs)),
        grid=(N_TOK // block_rows,),
        in_specs=in_specs,
        out_specs=(tok(D_MODEL), tok(ROW_WORDS), pl.BlockSpec((N_EXPERTS, block_rows), lambda i: (0, i))),
        out_shape=(jax.ShapeDtypeStruct((N_TOK, D_MODEL), F32),
                   jax.ShapeDtypeStruct((N_TOK, ROW_WORDS), jnp.int32),
                   jax.ShapeDtypeStruct((N_EXPERTS, N_TOK), F32)),
        scratch_shapes=[pltpu.VMEM(w.shape, BF16)],
        compiler_params=_params("arbitrary"),
        name="out_proj",
    )(*acts, *xs, g.reshape(1, D_MODEL), mod_l, router_wt, w)


def _router_kernel(lg_ref, rb_ref, pos_ref, w_ref, plan_ref, rank_ref):
    lg = lg_ref[...]
    ex = jnp.exp(lg - jnp.max(lg, axis=0, keepdims=True))
    scores = ex / jnp.sum(ex, axis=0, keepdims=True)
    biased = scores + rb_ref[...]
    rows = [biased[e:e + 1, :] for e in range(N_EXPERTS)]
    selected = []
    group_score = []
    for gi in range(N_GROUPS):
        r = rows[gi * EXPERTS_PER_GROUP:(gi + 1) * EXPERTS_PER_GROUP]
        total = None
        for i in range(EXPERTS_PER_GROUP):
            rank = None
            for j in range(EXPERTS_PER_GROUP):
                if j == i:
                    continue
                ahead = (r[j] > r[i]) if j > i else (r[j] >= r[i])
                ahead = jnp.where(ahead, 1.0, 0.0)
                rank = ahead if rank is None else rank + ahead
            sel = rank < 1.5
            selected.append(sel)
            contrib = jnp.where(sel, r[i], 0.0)
            total = contrib if total is None else total + contrib
        group_score.append(total)
    best = group_score[0]
    best_group = jnp.zeros_like(best)
    for gi in range(1, N_GROUPS):
        better = group_score[gi] > best
        best_group = jnp.where(better, float(gi), best_group)
        best = jnp.where(better, group_score[gi], best)
    picked = []
    chosen = []
    den = None
    for e in range(N_EXPERTS):
        in_group = best_group == float(e // EXPERTS_PER_GROUP)
        use = jnp.where(selected[e], jnp.where(in_group, 1.0, 0.0), 0.0)
        w = use * scores[e:e + 1, :]
        chosen.append(use)
        picked.append(w)
        den = w if den is None else den + w
    lanes = 128
    n_blk = N_TOK // lanes
    li = lax.broadcasted_iota(jnp.int32, (lanes, lanes), 0)
    lj = lax.broadcasted_iota(jnp.int32, (lanes, lanes), 1)
    prefix = jnp.where(li <= lj, 1.0, 0.0).astype(BF16)
    carry = jnp.zeros((N_EXPERTS, 1), F32)
    for blk in range(n_blk):
        cols = slice(blk * lanes, (blk + 1) * lanes)
        m = jnp.concatenate([chosen[e][:, cols] for e in range(N_EXPERTS)], axis=0)
        incl = jnp.dot(m.astype(BF16), prefix, preferred_element_type=F32)
        rank_ref[:, cols] = incl - m + carry
        carry = carry + incl[:, lanes - 1:lanes]
    count = carry
    padded = jnp.floor((count + float(MOE_TILE - 1)) * (1.0 / MOE_TILE)) * float(MOE_TILE)
    erow = lax.broadcasted_iota(jnp.int32, (N_EXPERTS, 1), 0)
    offset = jnp.zeros((N_EXPERTS, 1), F32)
    for e in range(N_EXPERTS - 1):
        offset = offset + jnp.where(erow > e, padded[e:e + 1, :], 0.0)
    seen = jnp.zeros_like(den)
    pos_a = jnp.zeros_like(den)
    pos_b = jnp.zeros_like(den)
    w_a = jnp.zeros_like(den)
    w_b = jnp.zeros_like(den)
    for e in range(N_EXPERTS):
        pos_e = rank_ref[e:e + 1, :] + offset[e:e + 1, :]
        gate_e = picked[e] / den
        first = jnp.where(seen < 0.5, chosen[e], 0.0) > 0.5
        second = jnp.where(seen > 0.5, chosen[e], 0.0) > 0.5
        pos_a = jnp.where(first, pos_e, pos_a)
        w_a = jnp.where(first, gate_e, w_a)
        pos_b = jnp.where(second, pos_e, pos_b)
        w_b = jnp.where(second, gate_e, w_b)
        seen = seen + chosen[e]
    pos_ref[0:1, :] = pos_a.astype(jnp.int32)
    pos_ref[1:2, :] = pos_b.astype(jnp.int32)
    w_ref[0:1, :] = w_a
    w_ref[1:2, :] = w_b
    start = (lax.broadcasted_iota(jnp.int32, (N_EXPERTS, lanes), 1) * MOE_TILE).astype(F32)
    end = offset + padded
    tile_expert = jnp.sum(jnp.where(end <= start, 1.0, 0.0), axis=0, keepdims=True)
    inside = (offset <= start) & (start < end)
    real = jnp.clip(count - (start - offset), 0.0, float(MOE_TILE))
    tile_rows = jnp.sum(jnp.where(inside, real, 0.0), axis=0, keepdims=True)
    plan_ref[0:1, :] = jnp.minimum(tile_expert, float(N_EXPERTS - 1)).astype(jnp.int32)
    plan_ref[1:2, :] = tile_rows.astype(jnp.int32)


def _router(logits_t, router_b):
    whole = lambda shape: pl.BlockSpec(shape, lambda i: (0, 0))
    return pl.pallas_call(
        _router_kernel,
        grid=(1,),
        in_specs=[whole((N_EXPERTS, N_TOK)), whole((N_EXPERTS, 1))],
        out_specs=(whole((2, N_TOK)), whole((2, N_TOK)), whole((2, 128))),
        out_shape=(jax.ShapeDtypeStruct((2, N_TOK), jnp.int32),
                   jax.ShapeDtypeStruct((2, N_TOK), F32),
                   jax.ShapeDtypeStruct((2, 128), jnp.int32)),
        scratch_shapes=[pltpu.VMEM((N_EXPERTS, N_TOK), F32)],
        compiler_params=_params("arbitrary"),
        name="router",
    )(logits_t, router_b.reshape(N_EXPERTS, 1))


def _sc_mesh():
    return plsc.VectorSubcoreMesh(core_axis_name="c", subcore_axis_name="s")


def _sc_worker_base():
    return (lax.axis_index("s") * SC_CORES + lax.axis_index("c")) * (N_TOK // SC_WORKERS)


def _moe_dispatch(h, pos_a, pos_b):
    n_chunks = N_TOK // SC_WORKERS // SC_CHUNK

    @functools.partial(
        pl.kernel, mesh=_sc_mesh(),
        out_type=jax.ShapeDtypeStruct((MOE_ROWS, ROW_WORDS), jnp.int32),
        scratch_types=[pltpu.VMEM((SC_CHUNK,), jnp.int32), pltpu.VMEM((SC_CHUNK,), jnp.int32),
                       pltpu.VMEM((SC_CHUNK, ROW_WORDS), jnp.int32)],
        name="moe_dispatch",
    )
    def run(h_hbm, pa_hbm, pb_hbm, xs_hbm, ia_v, ib_v, rows_v):
        base = _sc_worker_base()

        @pl.loop(0, n_chunks)
        def _(ci):
            tok = pl.ds(pl.multiple_of(base + ci * SC_CHUNK, SC_CHUNK), SC_CHUNK)
            pltpu.sync_copy(pa_hbm.at[tok], ia_v)
            pltpu.sync_copy(pb_hbm.at[tok], ib_v)
            pltpu.sync_copy(h_hbm.at[tok], rows_v)
            pltpu.sync_copy(rows_v, xs_hbm.at[ia_v])
            pltpu.sync_copy(rows_v, xs_hbm.at[ib_v])

    return run(h, pos_a, pos_b)


def _moe_collect(ys, pos_a, pos_b):
    n_chunks = N_TOK // SC_WORKERS // SC_CHUNK
    out = jax.ShapeDtypeStruct((N_TOK, ROW_WORDS), jnp.int32)

    @functools.partial(
        pl.kernel, mesh=_sc_mesh(), out_type=(out, out),
        scratch_types=[pltpu.VMEM((SC_CHUNK,), jnp.int32), pltpu.VMEM((SC_CHUNK,), jnp.int32),
                       pltpu.VMEM((SC_CHUNK, ROW_WORDS), jnp.int32)],
        name="moe_collect",
    )
    def run(ys_hbm, pa_hbm, pb_hbm, ya_hbm, yb_hbm, ia_v, ib_v, rows_v):
        base = _sc_worker_base()

        @pl.loop(0, n_chunks)
        def _(ci):
            tok = pl.ds(pl.multiple_of(base + ci * SC_CHUNK, SC_CHUNK), SC_CHUNK)
            pltpu.sync_copy(pa_hbm.at[tok], ia_v)
            pltpu.sync_copy(pb_hbm.at[tok], ib_v)
            pltpu.sync_copy(ys_hbm.at[ia_v], rows_v)
            pltpu.sync_copy(rows_v, ya_hbm.at[tok])
            pltpu.sync_copy(ys_hbm.at[ib_v], rows_v)
            pltpu.sync_copy(rows_v, yb_hbm.at[tok])

    return run(ys, pos_a, pos_b)


def _experts_kernel(plan_ref, xs_ref, wg_ref, wu_ref, wd_ref, y_ref, wgb_ref, wub_ref, wdb_ref):
    j = pl.program_id(0)
    expert = plan_ref[j]
    n_real = plan_ref[128 + j]
    fresh = jnp.logical_or(j == 0, expert != plan_ref[jnp.maximum(j - 1, 0)])

    @pl.when(jnp.logical_and(n_real > 0, fresh))
    def _():
        wgb_ref[...] = wg_ref[...].astype(BF16)
        wub_ref[...] = wu_ref[...].astype(BF16)
        wdb_ref[...] = wd_ref[...].astype(BF16)

    @pl.when(n_real > 0)
    def _():
        row = lax.broadcasted_iota(jnp.int32, xs_ref.shape, 0)
        words = jnp.where(row < n_real, xs_ref[...], 0)
        x = _unpack_rows(words).astype(BF16)
        a = jnp.dot(x, wgb_ref[...], preferred_element_type=F32)
        b = jnp.dot(x, wub_ref[...], preferred_element_type=F32)
        hid = (a * jax.nn.sigmoid(a)) * b
        y_ref[...] = _pack_rows(jnp.dot(hid.astype(BF16), wdb_ref[...], preferred_element_type=F32))


def _experts(plan, xs, w_gate, w_up, w_down, layer):
    wspec = lambda r, c: pl.BlockSpec((None, None, r, c), lambda j, plan: (layer, plan[j], 0, 0))
    return pl.pallas_call(
        _experts_kernel,
        grid_spec=pltpu.PrefetchScalarGridSpec(
            num_scalar_prefetch=1,
            grid=(MOE_ROWS // MOE_TILE,),
            in_specs=[pl.BlockSpec((MOE_TILE, ROW_WORDS), lambda j, plan: (j, 0)),
                      wspec(D_MODEL, D_EXPERT), wspec(D_MODEL, D_EXPERT), wspec(D_EXPERT, D_MODEL)],
            out_specs=pl.BlockSpec((MOE_TILE, ROW_WORDS), lambda j, plan: (j, 0)),
            scratch_shapes=[pltpu.VMEM((D_MODEL, D_EXPERT), BF16), pltpu.VMEM((D_MODEL, D_EXPERT), BF16),
                            pltpu.VMEM((D_EXPERT, D_MODEL), BF16)],
        ),
        out_shape=jax.ShapeDtypeStruct((MOE_ROWS, ROW_WORDS), jnp.int32),
        compiler_params=_params("arbitrary"),
        name="experts",
    )(plan, xs, w_gate, w_up, w_down)


def _combine_kernel(x_ref, ya_ref, yb_ref, wt_ref, mod_ref, o_ref):
    o_ref[...] = _moe_mix(x_ref, ya_ref, yb_ref, wt_ref, mod_ref)


def _combine(x, moe_out, mod_l, tok0, n_tok, block_rows=512):
    ya, yb, w_tok = moe_out
    b0 = tok0 // block_rows
    rows = lambda width: pl.BlockSpec((block_rows, width), lambda i: (b0 + i, 0))
    return pl.pallas_call(
        _combine_kernel,
        grid=(n_tok // block_rows,),
        in_specs=[rows(D_MODEL), rows(ROW_WORDS), rows(ROW_WORDS), rows(TOP_K),
                  pl.BlockSpec((None, 6, D_MODEL), lambda i: (_cond_of_token_block(b0 + i, block_rows), 0, 0))],
        out_specs=pl.BlockSpec((block_rows, D_MODEL), lambda i: (i, 0)),
        out_shape=jax.ShapeDtypeStruct((n_tok, D_MODEL), F32),
        compiler_params=_params("arbitrary"),
        name="combine",
    )(x, ya, yb, w_tok, mod_l)


def _moe(h, logits_t, router_b, w_gate, w_up, w_down, layer):
    pos, w, plan = _router(logits_t, router_b)
    xs = _moe_dispatch(h, pos[0], pos[1])
    ys = _experts(plan.reshape(-1), xs, w_gate, w_up, w_down, layer)
    ya, yb = _moe_collect(ys, pos[0], pos[1])
    return ya, yb, w.T


def _dft_tables(L):
    k = np.arange(L)[:, None]
    m = np.arange(L)[None, :]
    r = (k * m) % (2 * L)
    ang = np.pi * r.astype(np.float64) / L
    fc = np.cos(ang)
    fs = np.sin(ang)
    fs[0, :] = np.where(np.arange(L) % 2 == 0, 1.0, -1.0)
    wk = np.full((L, 1), 1.0 / L)
    wk[0, 0] = 0.5 / L
    gc = (fc * wk).T
    gs = (fs * wk).T
    return [jnp.asarray(t.astype(np.float32)).astype(BF16) for t in (fc, fs, gc, gs)]


def _filter_consts(L):
    t = np.linspace(0.0, 1.0, L, dtype=np.float32)[:, None]
    w = (np.float32(2.0 * np.pi) * np.arange(L, dtype=np.float32)[:, None] / np.float32(L)).astype(np.float32)
    fb = np.linspace(1e-4, HY_BANDS - 1, HY_BANDS, dtype=np.float32)[None, :]
    emb = np.concatenate([t, np.cos(fb * w), -np.sin(fb * w)], axis=-1).astype(np.float32)
    lo = math.log(HY_DECAY_TARGET) / HY_SLOW_PCT
    hi = math.log(HY_DECAY_TARGET) / HY_FAST_PCT
    deltas = np.abs(np.linspace(lo, hi, D_MODEL, dtype=np.float32))
    decay = np.exp(-t * deltas).astype(np.float32)
    return jnp.asarray(emb), jnp.asarray(decay)


def _filter_kernel(emb_ref, w1_ref, b1_ref, w2_ref, b2_ref, fr_ref, w3f_ref, w3b_ref, dec_ref,
                   fc_ref, fs_ref, kr_ref, q_ref, krn_ref):
    fr = fr_ref[...]
    hd = jnp.sin(fr * (jnp.dot(emb_ref[...], w1_ref[...], precision=HIGHEST,
                               preferred_element_type=F32) + b1_ref[...]))
    hd = jnp.sin(fr * (jnp.dot(hd, w2_ref[...], precision=HIGHEST,
                               preferred_element_type=F32) + b2_ref[...]))
    dec = dec_ref[...]
    f = jnp.dot(hd, w3f_ref[...], precision=HIGHEST, preferred_element_type=F32) * dec
    g = jnp.dot(hd, w3b_ref[...], precision=HIGHEST, preferred_element_type=F32) * dec
    row = lax.broadcasted_iota(jnp.int32, f.shape, 0)
    g = jnp.where(row == 0, 0.0, g)
    s = f + g
    d = f - g
    kr = jnp.dot(fc_ref[...], s.astype(BF16), preferred_element_type=F32)
    qq = jnp.dot(fs_ref[...], d.astype(BF16), preferred_element_type=F32)
    alt = jnp.where(row % 2 == 0, 1.0, -1.0)
    nyq = jnp.sum(alt * s, axis=0, keepdims=True)
    kr_ref[...] = kr
    q_ref[...] = jnp.where(row == 0, 0.0, qq)
    krn_ref[...] = jnp.where(row == 0, nyq, kr)


def _hyena_filter_spectrum(L, w1, b1, w2, b2, w3, freq, fc, fs, cblk=256):
    emb, decay = _filter_consts(L)
    ncb = D_MODEL // cblk
    n_emb = 128
    emb = jnp.pad(emb, ((0, 0), (0, n_emb - emb.shape[1])))
    w1 = jnp.pad(w1, ((0, n_emb - w1.shape[0]), (0, 0)))
    full = lambda shape: pl.BlockSpec(shape, lambda j: tuple(0 for _ in shape))
    out_sds = jax.ShapeDtypeStruct((L, D_MODEL), F32)
    out_spec = pl.BlockSpec((L, cblk), lambda j: (0, j))
    return pl.pallas_call(
        _filter_kernel,
        grid=(ncb,),
        in_specs=[
            full((L, n_emb)), full((n_emb, HY_FFN)), full((1, HY_FFN)), full((HY_FFN, HY_FFN)),
            full((1, HY_FFN)), full((1, HY_FFN)),
            pl.BlockSpec((HY_FFN, cblk), lambda j: (0, j)),
            pl.BlockSpec((HY_FFN, cblk), lambda j: (0, ncb + j)),
            pl.BlockSpec((L, cblk), lambda j: (0, j)),
            full((L, L)), full((L, L)),
        ],
        out_specs=(out_spec, out_spec, out_spec),
        out_shape=(out_sds, out_sds, out_sds),
        compiler_params=_params("arbitrary"),
        name=f"hyena_filter_{L}",
    )(emb, w1, b1.reshape(1, HY_FFN), w2, b2.reshape(1, HY_FFN), freq.reshape(1, HY_FFN), w3, w3, decay, fc, fs)


def _hyena_conv_kernel(x0_ref, x1_ref, v_ref, cw0_ref, cw1_ref, cwv_ref, cb0_ref, cb1_ref, cbv_ref,
                       kr_ref, q_ref, krn_ref, ds_ref, fc_ref, fs_ref, gc_ref, gs_ref, *rest):
    o_ref = rest[-1]
    L = x0_ref.shape[0]
    row = lax.broadcasted_iota(jnp.int32, x0_ref.shape, 0)

    def short_conv(u_ref, w_ref, b_ref):
        u = u_ref[...]
        w = w_ref[...]
        prev = jnp.where(row == 0, 0.0, pltpu.roll(u, 1, axis=0))
        nxt = jnp.where(row == L - 1, 0.0, pltpu.roll(u, L - 1, axis=0))
        return prev * w[0:1, :] + u * w[1:2, :] + nxt * w[2:3, :] + b_ref[...]

    x0 = short_conv(x0_ref, cw0_ref, cb0_ref)
    x1 = short_conv(x1_ref, cw1_ref, cb1_ref)
    v = short_conv(v_ref, cwv_ref, cbv_ref)
    zz = v * x1
    zb = zz.astype(BF16)
    ur = jnp.dot(fc_ref[...], zb, preferred_element_type=F32)
    p = jnp.dot(fs_ref[...], zb, preferred_element_type=F32)
    qq = q_ref[...]
    yr = ur * kr_ref[...] - p * qq
    yw = ur * qq + p * krn_ref[...]
    y = jnp.dot(gc_ref[...], yr.astype(BF16), preferred_element_type=F32)
    y = y + jnp.dot(gs_ref[...], yw.astype(BF16), preferred_element_type=F32)
    o_ref[...] = (x0 * (y + zz * ds_ref[...])).astype(o_ref.dtype)


def _hyena_conv(u, conv_w, conv_b, dskip, spectrum, tables, shared, *, latent):
    L = LATENT_LEN if latent else PROMPT_LEN
    n_seq = N_LATENT_SEQ if latent else N_PROMPT_SEQ
    cblk = 256 if latent else 512
    ncb = D_MODEL // cblk
    row0 = (N_PROMPT_TOK // L) if latent else 0
    kr, qq, krn = spectrum
    fc, fs, gc, gs = tables

    def part(p, rows):
        return pl.BlockSpec((rows, cblk), lambda j, s: (0 if rows != L else row0 + s, p * ncb + j))

    def const_cols(rows):
        return pl.BlockSpec((rows, cblk), lambda j, s: (0, j))

    mat = pl.BlockSpec((L, L), lambda j, s: (0, 0))
    conv_b2 = conv_b.reshape(1, 3 * D_MODEL)
    in_specs = [part(0, L), part(1, L), part(2, L),
                part(0, 3), part(1, 3), part(2, 3),
                part(0, 1), part(1, 1), part(2, 1),
                const_cols(L), const_cols(L), const_cols(L), const_cols(1),
                mat, mat, mat, mat]
    args = [u, u, u, conv_w, conv_w, conv_w, conv_b2, conv_b2, conv_b2,
            kr, qq, krn, dskip.reshape(1, D_MODEL), fc, fs, gc, gs]
    aliases = {}
    if latent:
        in_specs.append(pl.BlockSpec(memory_space=pl.ANY))
        args.append(shared)
        aliases = {len(args) - 1: 0}
    return pl.pallas_call(
        _hyena_conv_kernel,
        grid=(ncb, n_seq),
        in_specs=in_specs,
        out_specs=pl.BlockSpec((L, cblk), lambda j, s: (row0 + s, j)),
        out_shape=jax.ShapeDtypeStruct((N_TOK, D_MODEL), BF16),
        input_output_aliases=aliases,
        compiler_params=_params("arbitrary", "arbitrary"),
        name="hyena_conv_latent" if latent else "hyena_conv_prompt",
    )(*args)


def kernel(x_prompt, x_sample, cache_k, cache_v, state_hgrn, c, c_ctx, norm_g, mod_w, mod_b, ab_in_w, hgrn_lb, hgrn_onorm_g, attn_qnorm_g, attn_knorm_g, ab_out_w, hy_in_w, hy_in_b, hy_conv_w, hy_conv_b, hy_f_w1, hy_f_b1, hy_f_w2, hy_f_b2, hy_f_w3, hy_f_freq, hy_dskip, hy_out_w, router_w, router_b, moe_w_gate, moe_w_up, moe_w_down):
    xp = x_prompt.reshape(N_PROMPT_TOK, D_MODEL)
    xl = x_sample.reshape(N_LATENT_TOK, D_MODEL)
    cond = jnp.concatenate([c_ctx[None, :], c, jnp.zeros((N_COND - 1 - N_LATENT_SEQ, D_MODEL), F32)], axis=0)
    mod = _modulation(cond, mod_w, mod_b)
    router_wt = router_w.T

    z = _in_proj0(xp, xl, norm_g[0, 0], mod[0], ab_in_w[0])
    o_a, new_state = _hgrn(z, hgrn_lb, hgrn_onorm_g[0], None, None, latent=False)
    o_a = _hgrn(z, hgrn_lb, hgrn_onorm_g[0], state_hgrn, o_a, latent=True)
    o_b, k_prompt = _attention_prompt(z, attn_qnorm_g[0], attn_knorm_g[0])
    o_b = _attention_latent(z, attn_qnorm_g[0], attn_knorm_g[0], cache_k, cache_v, o_b)
    x, h, logits_t = _out_proj([o_a, o_b], ab_out_w[0], (xp, xl), norm_g[0, 1], mod[0], router_wt)
    moe_out = _moe(h, logits_t, router_b, moe_w_gate, moe_w_up, moe_w_down, 0)

    x, u = _in_proj1(x, moe_out, mod[0], norm_g[1, 0], mod[1], hy_in_w[0], hy_in_b[0])
    pre = None
    for latent in (False, True):
        L = LATENT_LEN if latent else PROMPT_LEN
        tables = _dft_tables(L)
        spectrum = _hyena_filter_spectrum(L, hy_f_w1[0], hy_f_b1[0], hy_f_w2[0], hy_f_b2[0], hy_f_w3[0],
                                          hy_f_freq[0], tables[0], tables[1])
        pre = _hyena_conv(u, hy_conv_w[0], hy_conv_b[0], hy_dskip[0], spectrum, tables, pre, latent=latent)
    x, h, logits_t = _out_proj([pre], hy_out_w[0], (x,), norm_g[1, 1], mod[1], router_wt)
    moe_out = _moe(h, logits_t, router_b, moe_w_gate, moe_w_up, moe_w_down, 1)

    y_prompt = _combine(x, moe_out, mod[1], 0, N_PROMPT_TOK).reshape(N_PROMPT_SEQ, PROMPT_LEN, D_MODEL)
    y_sample = _combine(x, moe_out, mod[1], N_PROMPT_TOK, N_LATENT_TOK).reshape(N_LATENT_SEQ, LATENT_LEN, D_MODEL)
    kv_shape = (N_PROMPT_SEQ, PROMPT_LEN, KV_HEADS, HEAD_DIM)
    new_k = k_prompt.reshape(kv_shape).transpose(0, 2, 1, 3)[:, None]
    v_col = 5 * A_WIDTH + (Q_HEADS + KV_HEADS) * HEAD_DIM
    new_v = z[:N_PROMPT_TOK, v_col:].reshape(kv_shape).transpose(0, 2, 1, 3)[:, None]
    return (y_prompt, y_sample, new_k, new_v, new_state)
```

```python
import functools
import math

import numpy as np
import jax
import jax.numpy as jnp
from jax import lax
from jax.experimental import pallas as pl
from jax.experimental.pallas import tpu as pltpu
from jax.experimental.pallas import tpu_sc as plsc

F32 = jnp.float32
BF16 = jnp.bfloat16
HIGHEST = lax.Precision.HIGHEST

D_MODEL = 1024
N_PROMPT_SEQ = 32
PROMPT_LEN = 256
N_LATENT_SEQ = 2
LATENT_LEN = 1024
PAST_LEN = 512
GRID_W = 64
N_PROMPT_TOK = N_PROMPT_SEQ * PROMPT_LEN
N_LATENT_TOK = N_LATENT_SEQ * LATENT_LEN
N_TOK = N_PROMPT_TOK + N_LATENT_TOK
N_COND = 8
EPS = 1e-6

A_WIDTH = 512
A_HEADS = 4
A_DK = 128
CHUNK = 64
HGRN_BLOCK = 256
HGRN_HEADS_PER_STEP = 4
HEAD_DIM = 64
Q_HEADS = 8
KV_HEADS = 2
Q_PER_KV = Q_HEADS // KV_HEADS
Q_BLOCK = 256
ROPE_THETA = 10000.0
ROPE_PAIRS = HEAD_DIM // 4
AB_IN = 5 * A_WIDTH + (Q_HEADS + 2 * KV_HEADS) * HEAD_DIM

HY_BANDS = 16
HY_FFN = 64
HY_DECAY_TARGET = 1e-2
HY_FAST_PCT = 0.3
HY_SLOW_PCT = 1.5

N_EXPERTS = 16
N_GROUPS = 4
EXPERTS_PER_GROUP = 4
TOP_K = 2
D_EXPERT = 512
ROUTER_LANES = 128
MOE_TILE = 256
MOE_ROWS = N_TOK * TOP_K + N_EXPERTS * MOE_TILE

SC_CORES = 2
SC_WORKERS = 32
SC_CHUNK = 80
ROW_WORDS = D_MODEL // 2

VMEM_LIMIT = 56 * 1024 * 1024


def _params(*sem):
    return pltpu.CompilerParams(dimension_semantics=sem, vmem_limit_bytes=VMEM_LIMIT)


def _pack_rows(x):
    n = x.shape[1] // 2
    bits = pltpu.bitcast(x.astype(BF16).astype(F32), jnp.uint32)
    return pltpu.bitcast(bits[:, :n] | (bits[:, n:] >> 16), jnp.int32)


def _unpack_rows(p):
    bits = pltpu.bitcast(p, jnp.uint32)
    hi = pltpu.bitcast(bits & jnp.uint32(0xFFFF0000), F32)
    lo = pltpu.bitcast(bits << 16, F32)
    return jnp.concatenate([hi, lo], axis=1)


def _cond_of_token_block(i, block_rows):
    start = i * block_rows
    return jnp.where(start < N_PROMPT_TOK, 0, 1 + (start - N_PROMPT_TOK) // LATENT_LEN)


def _mod_kernel(cond_ref, w_ref, b_ref, o_ref):
    cnd = cond_ref[...]
    s = cnd * jax.nn.sigmoid(cnd)
    o_ref[...] = jnp.dot(s, w_ref[...], precision=HIGHEST, preferred_element_type=F32) + b_ref[...]


def _modulation(cond, mod_w, mod_b):
    depth = mod_w.shape[0]
    n_chunk = 6
    out = pl.pallas_call(
        _mod_kernel,
        grid=(depth, n_chunk),
        in_specs=[
            pl.BlockSpec((N_COND, D_MODEL), lambda l, j: (0, 0)),
            pl.BlockSpec((None, D_MODEL, D_MODEL), lambda l, j: (l, 0, j)),
            pl.BlockSpec((None, 1, D_MODEL), lambda l, j: (l, 0, j)),
        ],
        out_specs=pl.BlockSpec((None, N_COND, D_MODEL), lambda l, j: (l, 0, j)),
        out_shape=jax.ShapeDtypeStruct((depth, N_COND, n_chunk * D_MODEL), F32),
        compiler_params=_params("arbitrary", "arbitrary"),
        name="modulation",
    )(cond, mod_w, mod_b.reshape(depth, 1, n_chunk * D_MODEL))
    return out.reshape(depth, N_COND, n_chunk, D_MODEL)


def _modulated_norm(x, g, mod, shift_row, scale_row):
    ms = jnp.mean(x * x, axis=-1, keepdims=True)
    y = x * lax.rsqrt(ms + EPS) * g
    return y * (1.0 + mod[scale_row:scale_row + 1, :]) + mod[shift_row:shift_row + 1, :]


def _trunk_specs(block_rows, width):
    n_prompt_blocks = N_PROMPT_TOK // block_rows
    return (pl.BlockSpec((block_rows, width), lambda i: (jnp.minimum(i, n_prompt_blocks - 1), 0)),
            pl.BlockSpec((block_rows, width), lambda i: (jnp.maximum(i - n_prompt_blocks, 0), 0)))


def _select_trunk(p_ref, l_ref):
    block_rows = p_ref.shape[0]
    return jnp.where(pl.program_id(0) < N_PROMPT_TOK // block_rows, p_ref[...], l_ref[...])


def _cast_once(w_ref, wb_ref):
    @pl.when(pl.program_id(0) == 0)
    def _():
        wb_ref[...] = w_ref[...].astype(BF16)


def _resident(shape):
    return pl.BlockSpec(shape, lambda i: tuple(0 for _ in shape), pipeline_mode=pl.Buffered(1))


def _mod_spec(block_rows):
    return pl.BlockSpec((None, 6, D_MODEL), lambda i: (_cond_of_token_block(i, block_rows), 0, 0))


def _in_proj0_kernel(xp_ref, xl_ref, g_ref, mod_ref, w_ref, o_ref, wb_ref):
    _cast_once(w_ref, wb_ref)
    h = _modulated_norm(_select_trunk(xp_ref, xl_ref), g_ref[...], mod_ref[...], 0, 1)
    o_ref[...] = jnp.dot(h.astype(BF16), wb_ref[...], preferred_element_type=F32).astype(o_ref.dtype)


def _in_proj0(x_prompt, x_latent, g, mod_l, w, block_rows=256):
    n = w.shape[1]
    return pl.pallas_call(
        _in_proj0_kernel,
        grid=(N_TOK // block_rows,),
        in_specs=[*_trunk_specs(block_rows, D_MODEL), _resident((1, D_MODEL)), _mod_spec(block_rows),
                  _resident((D_MODEL, n))],
        out_specs=pl.BlockSpec((block_rows, n), lambda i: (i, 0)),
        out_shape=jax.ShapeDtypeStruct((N_TOK, n), BF16),
        scratch_shapes=[pltpu.VMEM((D_MODEL, n), BF16)],
        compiler_params=_params("arbitrary"),
        name="in_proj0",
    )(x_prompt, x_latent, g.reshape(1, D_MODEL), mod_l, w)


def _moe_mix(x_ref, ya_ref, yb_ref, wt_ref, mod_ref):
    wt = wt_ref[...]
    mix = wt[:, 0:1] * _unpack_rows(ya_ref[...]) + wt[:, 1:2] * _unpack_rows(yb_ref[...])
    return x_ref[...] + mod_ref[5:6, :] * mix


def _in_proj1_kernel(x_ref, ya_ref, yb_ref, wt_ref, modp_ref, g_ref, mod_ref, w_ref, b_ref, xo_ref, o_ref, wb_ref):
    _cast_once(w_ref, wb_ref)
    x = _moe_mix(x_ref, ya_ref, yb_ref, wt_ref, modp_ref)
    xo_ref[...] = x
    h = _modulated_norm(x, g_ref[...], mod_ref[...], 0, 1)
    u = jnp.dot(h.astype(BF16), wb_ref[...], preferred_element_type=F32) + b_ref[...]
    o_ref[...] = u.astype(o_ref.dtype)


def _in_proj1(x, moe_out, mod_prev, g, mod_l, w, bias, block_rows=256):
    ya, yb, w_tok = moe_out
    n = w.shape[1]
    tok = pl.BlockSpec((block_rows, D_MODEL), lambda i: (i, 0))
    packed = pl.BlockSpec((block_rows, ROW_WORDS), lambda i: (i, 0))
    return pl.pallas_call(
        _in_proj1_kernel,
        grid=(N_TOK // block_rows,),
        in_specs=[tok, packed, packed, pl.BlockSpec((block_rows, TOP_K), lambda i: (i, 0)), _mod_spec(block_rows),
                  _resident((1, D_MODEL)), _mod_spec(block_rows), _resident((D_MODEL, n)), _resident((1, n))],
        out_specs=(tok, pl.BlockSpec((block_rows, n), lambda i: (i, 0))),
        out_shape=(jax.ShapeDtypeStruct((N_TOK, D_MODEL), F32), jax.ShapeDtypeStruct((N_TOK, n), BF16)),
        scratch_shapes=[pltpu.VMEM((D_MODEL, n), BF16)],
        compiler_params=_params("arbitrary"),
        name="in_proj1",
    )(x, ya, yb, w_tok, mod_prev, g.reshape(1, D_MODEL), mod_l, w, bias.reshape(1, n))


def _hgrn_kernel(*refs, seq_len, with_state):
    if with_state:
        (q_ref, zf_ref, zb_ref, i_ref, ga_ref, lb_ref, og_ref, s0_ref, shared_ref, o_ref, of_ref, ob_ref) = refs
    else:
        (q_ref, zf_ref, zb_ref, i_ref, ga_ref, lb_ref, og_ref, o_ref, s_ref, of_ref, ob_ref) = refs
    n_blocks = seq_len // HGRN_BLOCK
    chunks_per_block = HGRN_BLOCK // CHUNK

    lbr = lb_ref[...]
    mx = jnp.maximum(lbr[0], lbr[1])
    e0 = jnp.exp(lbr[0] - mx)
    e1 = jnp.exp(lbr[1] - mx)
    lb = e0 / (e0 + e1)

    row = lax.broadcasted_iota(jnp.int32, (HGRN_BLOCK, HGRN_BLOCK), 0)
    col = lax.broadcasted_iota(jnp.int32, (HGRN_BLOCK, HGRN_BLOCK), 1)
    same_chunk = (row // CHUNK) == (col // CHUNK)
    nt = (((1,), (1,)), ((), ()))
    tn = (((0,), (0,)), ((), ()))

    def per_chunk_row(x, idx):
        return jnp.concatenate(
            [jnp.broadcast_to(x[n * CHUNK + idx:n * CHUNK + idx + 1, :], (CHUNK, x.shape[1]))
             for n in range(chunks_per_block)], axis=0)

    def in_chunk_cumsum(tri, x):
        hi = x.astype(BF16)
        lo = (x - hi.astype(F32)).astype(BF16)
        return jnp.dot(tri, hi, preferred_element_type=F32) + jnp.dot(tri, lo, preferred_element_type=F32)

    def block(blk, cols, st, z_ref, lbd, forward, out_ref):
        rows = slice(blk * HGRN_BLOCK, (blk + 1) * HGRN_BLOCK)
        keep = (same_chunk & (col <= row)) if forward else (same_chunk & (col >= row))
        tri = jnp.where(keep, 1.0, 0.0).astype(BF16)
        mid = CHUNK // 2 if forward else CHUNK - 1 - CHUNK // 2
        last = CHUNK - 1 if forward else 0
        f = lbd + (1.0 - lbd) * jax.nn.sigmoid(z_ref[rows, cols].astype(F32))
        lf = jnp.log(f)
        k = 1.0 - f
        q = q_ref[rows, cols].astype(F32)
        vb = i_ref[rows, cols].astype(BF16)
        b = in_chunk_cumsum(tri, lf)
        bm = per_chunk_row(b, mid)
        bl = per_chunk_row(b, last)
        qe = (q * jnp.exp(b - bm)).astype(BF16)
        ke = (k * jnp.exp(bm - b)).astype(BF16)
        att = lax.dot_general(qe, ke, nt, preferred_element_type=F32)
        att = jnp.where(keep, att, 0.0)
        o_intra = jnp.dot(att.astype(BF16), vb, preferred_element_type=F32)
        qb = (q * jnp.exp(b)).astype(BF16)
        ks = (k * jnp.exp(bl - b)).astype(BF16)
        decay = jnp.exp(bl)
        order = range(chunks_per_block) if forward else range(chunks_per_block - 1, -1, -1)
        o_inter = [None] * chunks_per_block
        for n in order:
            cr = slice(n * CHUNK, (n + 1) * CHUNK)
            o_inter[n] = lax.dot_general(qb[cr], st.astype(BF16), nt, preferred_element_type=F32)
            upd = lax.dot_general(vb[cr], ks[cr], tn, preferred_element_type=F32)
            st = st * decay[n * CHUNK:n * CHUNK + 1, :] + upd
        out_ref[rows, cols] = o_intra + jnp.concatenate(o_inter, axis=0)
        return st

    for hd in range(q_ref.shape[1] // A_DK):
        cols = slice(hd * A_DK, (hd + 1) * A_DK)
        if with_state:
            st_f, st_b = s0_ref[0, hd].T, s0_ref[1, hd].T
        else:
            st_f, st_b = jnp.zeros((A_DK, A_DK), F32), jnp.zeros((A_DK, A_DK), F32)
        for step in range(n_blocks):
            st_f = block(step, cols, st_f, zf_ref, lb[0:1, cols], True, of_ref)
            st_b = block(n_blocks - 1 - step, cols, st_b, zb_ref, lb[1:2, cols], False, ob_ref)
        if not with_state:
            s_ref[0, hd] = st_f.T
            s_ref[1, hd] = st_b.T
        o = of_ref[:, cols] + ob_ref[:, cols]
        o = o * lax.rsqrt(jnp.mean(o * o, axis=-1, keepdims=True) + EPS) * og_ref[:, cols]
        ga = ga_ref[:, cols].astype(F32)
        o_ref[:, cols] = (o * (ga * jax.nn.sigmoid(ga))).astype(o_ref.dtype)


def _hgrn(z, hgrn_lb, onorm_g, state, shared, *, latent):
    seq_len = LATENT_LEN if latent else PROMPT_LEN
    n_seq = N_LATENT_SEQ if latent else N_PROMPT_SEQ
    row0 = (N_PROMPT_TOK // seq_len) if latent else 0

    hw = HGRN_HEADS_PER_STEP * A_DK
    n_hg = A_HEADS // HGRN_HEADS_PER_STEP

    def zspec(part):
        return pl.BlockSpec((seq_len, hw), lambda s, h: (row0 + s, part * n_hg + h))

    in_specs = [zspec(0), zspec(1), zspec(2), zspec(3), zspec(4),
                pl.BlockSpec((2, 2, hw), lambda s, h: (0, 0, h)),
                pl.BlockSpec((1, hw), lambda s, h: (0, h))]
    args = [z, z, z, z, z, hgrn_lb, onorm_g.reshape(1, A_WIDTH)]
    state_spec = pl.BlockSpec((None, None, 2, HGRN_HEADS_PER_STEP, A_DK, A_DK), lambda s, h: (s, 0, 0, h, 0, 0))
    o_shape = jax.ShapeDtypeStruct((N_TOK, A_WIDTH), BF16)
    o_spec = pl.BlockSpec((seq_len, hw), lambda s, h: (row0 + s, h))
    aliases = {}
    if latent:
        in_specs += [state_spec, pl.BlockSpec(memory_space=pl.ANY)]
        args += [state, shared]
        aliases = {len(args) - 1: 0}
        out_shape, out_specs = o_shape, o_spec
    else:
        out_shape = (o_shape, jax.ShapeDtypeStruct((n_seq, 1, 2, A_HEADS, A_DK, A_DK), F32))
        out_specs = (o_spec, state_spec)
    return pl.pallas_call(
        functools.partial(_hgrn_kernel, seq_len=seq_len, with_state=latent),
        grid=(n_seq, n_hg),
        in_specs=in_specs,
        out_specs=out_specs,
        out_shape=out_shape,
        input_output_aliases=aliases,
        scratch_shapes=[pltpu.VMEM((seq_len, hw), F32), pltpu.VMEM((seq_len, hw), F32)],
        compiler_params=_params("arbitrary", "arbitrary"),
        name="hgrn_latent" if latent else "hgrn_prompt",
    )(*args)


def _rope_tables():
    pos = np.arange(LATENT_LEN)
    row, colp = pos // GRID_W, pos % GRID_W
    inv = ROPE_THETA ** (-np.arange(ROPE_PAIRS, dtype=np.float32) / ROPE_PAIRS)
    inv = inv.astype(np.float32)
    ang_r = (row.astype(np.float32)[:, None] * inv).astype(np.float32)
    ang_c = (colp.astype(np.float32)[:, None] * inv).astype(np.float32)
    cos = np.concatenate([np.cos(ang_r), np.cos(ang_r), np.cos(ang_c), np.cos(ang_c)], axis=1)
    sin = np.concatenate([-np.sin(ang_r), np.sin(ang_r), -np.sin(ang_c), np.sin(ang_c)], axis=1)
    perm = np.zeros((HEAD_DIM, HEAD_DIM), np.float32)
    for d in range(HEAD_DIM):
        partner = d + ROPE_PAIRS if (d // ROPE_PAIRS) % 2 == 0 else d - ROPE_PAIRS
        perm[partner, d] = 1.0
    return cos.astype(np.float32), sin.astype(np.float32), perm


def _attn_kernel(*refs, latent):
    if latent:
        (q_ref, k_ref, v_ref, qg_ref, kg_ref, cosq_ref, sinq_ref, cosk_ref, sink_ref, perm_ref,
         ck_ref, cv_ref, shared_ref, o_ref) = refs
    else:
        (q_ref, k_ref, v_ref, qg_ref, kg_ref, o_ref, kout_ref) = refs

    def head(x_ref, h, g, cos, sin):
        xh = x_ref[:, h * HEAD_DIM:(h + 1) * HEAD_DIM].astype(F32)
        xh = xh * lax.rsqrt(jnp.mean(xh * xh, axis=-1, keepdims=True) + EPS) * g
        if latent:
            swapped = jnp.dot(xh, perm_ref[...], precision=HIGHEST, preferred_element_type=F32)
            xh = xh * cos + swapped * sin
        return xh

    qg = qg_ref[...]
    kg = kg_ref[...]
    cq = sq = ck = sk = None
    if latent:
        cq, sq, ck, sk = cosq_ref[...], sinq_ref[...], cosk_ref[...], sink_ref[...]
    scale = HEAD_DIM ** -0.5
    n_q = q_ref.shape[0]
    for j in range(KV_HEADS):
        kh = head(k_ref, j, kg, ck, sk)
        if not latent:
            kout_ref[:, j * HEAD_DIM:(j + 1) * HEAD_DIM] = kh
        vh = v_ref[:, j * HEAD_DIM:(j + 1) * HEAD_DIM]
        qs = jnp.concatenate(
            [head(q_ref, j * Q_PER_KV + t, qg, cq, sq) * scale for t in range(Q_PER_KV)], axis=0)
        qs = qs.astype(BF16)
        nt = (((1,), (1,)), ((), ()))
        s_new = lax.dot_general(qs, kh.astype(BF16), nt, preferred_element_type=F32)
        m = jnp.max(s_new, axis=-1, keepdims=True)
        if latent:
            s_old = lax.dot_general(qs, ck_ref[j].astype(BF16), nt, preferred_element_type=F32)
            m = jnp.maximum(m, jnp.max(s_old, axis=-1, keepdims=True))
        p_new = jnp.exp(s_new - m)
        den = jnp.sum(p_new, axis=-1, keepdims=True)
        acc = jnp.dot(p_new.astype(BF16), vh.astype(BF16), preferred_element_type=F32)
        if latent:
            p_old = jnp.exp(s_old - m)
            den = den + jnp.sum(p_old, axis=-1, keepdims=True)
            acc = acc + jnp.dot(p_old.astype(BF16), cv_ref[j].astype(BF16), preferred_element_type=F32)
        out = acc / den
        for t in range(Q_PER_KV):
            hq = j * Q_PER_KV + t
            o_ref[:, hq * HEAD_DIM:(hq + 1) * HEAD_DIM] = out[t * n_q:(t + 1) * n_q, :].astype(o_ref.dtype)


def _attention_prompt(z, qn_g, kn_g):
    L = PROMPT_LEN
    q_col = (5 * A_WIDTH) // (Q_HEADS * HEAD_DIM)
    k_col = (5 * A_WIDTH + Q_HEADS * HEAD_DIM) // (KV_HEADS * HEAD_DIM)
    kv_w = KV_HEADS * HEAD_DIM
    return pl.pallas_call(
        functools.partial(_attn_kernel, latent=False),
        grid=(N_PROMPT_SEQ,),
        in_specs=[
            pl.BlockSpec((L, Q_HEADS * HEAD_DIM), lambda s: (s, q_col)),
            pl.BlockSpec((L, kv_w), lambda s: (s, k_col)),
            pl.BlockSpec((L, kv_w), lambda s: (s, k_col + 1)),
            pl.BlockSpec((1, HEAD_DIM), lambda s: (0, 0)),
            pl.BlockSpec((1, HEAD_DIM), lambda s: (0, 0)),
        ],
        out_specs=(pl.BlockSpec((L, Q_HEADS * HEAD_DIM), lambda s: (s, 0)),
                   pl.BlockSpec((L, kv_w), lambda s: (s, 0))),
        out_shape=(jax.ShapeDtypeStruct((N_TOK, Q_HEADS * HEAD_DIM), BF16),
                   jax.ShapeDtypeStruct((N_PROMPT_TOK, kv_w), F32)),
        compiler_params=_params("arbitrary"),
        name="attn_prompt",
    )(z, z, z, qn_g.reshape(1, HEAD_DIM), kn_g.reshape(1, HEAD_DIM))


def _attention_latent(z, qn_g, kn_g, cache_k, cache_v, shared):
    L = LATENT_LEN
    nqb = L // Q_BLOCK
    q_col = (5 * A_WIDTH) // (Q_HEADS * HEAD_DIM)
    k_col = (5 * A_WIDTH + Q_HEADS * HEAD_DIM) // (KV_HEADS * HEAD_DIM)
    kv_w = KV_HEADS * HEAD_DIM
    qrow0 = N_PROMPT_TOK // Q_BLOCK
    krow0 = N_PROMPT_TOK // L
    cos, sin, perm = _rope_tables()
    cache_spec = pl.BlockSpec((None, None, KV_HEADS, PAST_LEN, HEAD_DIM), lambda s, b: (s, 0, 0, 0, 0))
    return pl.pallas_call(
        functools.partial(_attn_kernel, latent=True),
        grid=(N_LATENT_SEQ, nqb),
        in_specs=[
            pl.BlockSpec((Q_BLOCK, Q_HEADS * HEAD_DIM), lambda s, b: (qrow0 + s * nqb + b, q_col)),
            pl.BlockSpec((L, kv_w), lambda s, b: (krow0 + s, k_col)),
            pl.BlockSpec((L, kv_w), lambda s, b: (krow0 + s, k_col + 1)),
            pl.BlockSpec((1, HEAD_DIM), lambda s, b: (0, 0)),
            pl.BlockSpec((1, HEAD_DIM), lambda s, b: (0, 0)),
            pl.BlockSpec((Q_BLOCK, HEAD_DIM), lambda s, b: (b, 0)),
            pl.BlockSpec((Q_BLOCK, HEAD_DIM), lambda s, b: (b, 0)),
            pl.BlockSpec((L, HEAD_DIM), lambda s, b: (0, 0)),
            pl.BlockSpec((L, HEAD_DIM), lambda s, b: (0, 0)),
            pl.BlockSpec((HEAD_DIM, HEAD_DIM), lambda s, b: (0, 0)),
            cache_spec, cache_spec, pl.BlockSpec(memory_space=pl.ANY),
        ],
        out_specs=pl.BlockSpec((Q_BLOCK, Q_HEADS * HEAD_DIM), lambda s, b: (qrow0 + s * nqb + b, 0)),
        out_shape=jax.ShapeDtypeStruct((N_TOK, Q_HEADS * HEAD_DIM), BF16),
        input_output_aliases={12: 0},
        compiler_params=_params("arbitrary", "arbitrary"),
        name="attn_latent",
    )(z, z, z, qn_g.reshape(1, HEAD_DIM), kn_g.reshape(1, HEAD_DIM),
      jnp.asarray(cos), jnp.asarray(sin), jnp.asarray(cos), jnp.asarray(sin), jnp.asarray(perm),
      cache_k, cache_v, shared)


def _out_proj_kernel(*refs, n_act, n_x):
    a_refs = refs[:n_act]
    x_refs = refs[n_act:n_act + n_x]
    g_ref, mod_ref, rw_ref, w_ref, xo_ref, h_ref, lg_ref, wb_ref, rwh_ref, rwl_ref = refs[n_act + n_x:]
    _cast_once(w_ref, wb_ref)

    @pl.when(pl.program_id(0) == 0)
    def _():
        rw = rw_ref[...]
        hi = rw.astype(BF16)
        rwh_ref[...] = hi
        rwl_ref[...] = (rw - hi.astype(F32)).astype(BF16)

    acc = None
    k0 = 0
    for a_ref in a_refs:
        k1 = k0 + a_ref.shape[1]
        part = jnp.dot(a_ref[...], wb_ref[k0:k1, :], preferred_element_type=F32)
        acc = part if acc is None else acc + part
        k0 = k1
    mod = mod_ref[...]
    x_in = x_refs[0][...] if n_x == 1 else _select_trunk(*x_refs)
    x = x_in + mod[2:3, :] * acc
    xo_ref[...] = x
    h = _modulated_norm(x, g_ref[...], mod, 3, 4)
    h_ref[...] = _pack_rows(h)
    h_hi = h.astype(BF16)
    h_lo = (h - h_hi.astype(F32)).astype(BF16)
    lg = jnp.dot(h_hi, rwh_ref[...], preferred_element_type=F32)
    lg = lg + jnp.dot(h_lo, rwh_ref[...], preferred_element_type=F32)
    lg = lg + jnp.dot(h_hi, rwl_ref[...], preferred_element_type=F32)
    lg_ref[...] = lg.T[:N_EXPERTS, :]


def _out_proj(acts, w, xs, g, mod_l, router_wp, block_rows=256):
    tok = lambda width: pl.BlockSpec((block_rows, width), lambda i: (i, 0))
    in_specs = [tok(a.shape[1]) for a in acts]
    in_specs += [tok(D_MODEL)] if len(xs) == 1 else list(_trunk_specs(block_rows, D_MODEL))
    in_specs += [_resident((1, D_MODEL)), _mod_spec(block_rows), _resident((D_MODEL, ROUTER_LANES)),
                 _resident(w.shape)]
    return pl.pallas_call(
        functools.partial(_out_proj_kernel, n_act=len(acts), n_x=len(xs)),
        grid=(N_TOK // block_rows,),
        in_specs=in_specs,
        out_specs=(tok(D_MODEL), tok(ROW_WORDS), pl.BlockSpec((N_EXPERTS, block_rows), lambda i: (0, i))),
        out_shape=(jax.ShapeDtypeStruct((N_TOK, D_MODEL), F32),
                   jax.ShapeDtypeStruct((N_TOK, ROW_WORDS), jnp.int32),
                   jax.ShapeDtypeStruct((N_EXPERTS, N_TOK), F32)),
        scratch_shapes=[pltpu.VMEM(w.shape, BF16), pltpu.VMEM((D_MODEL, ROUTER_LANES), BF16),
                        pltpu.VMEM((D_MODEL, ROUTER_LANES), BF16)],
        compiler_params=_params("arbitrary"),
        name="out_proj",
    )(*acts, *xs, g.reshape(1, D_MODEL), mod_l, router_wp, w)


def _router_kernel(lg_ref, rb_ref, pos_ref, w_ref, plan_ref, rank_ref):
    lg = lg_ref[...]
    ex = jnp.exp(lg - jnp.max(lg, axis=0, keepdims=True))
    scores = ex / jnp.sum(ex, axis=0, keepdims=True)
    biased = scores + rb_ref[...]
    rows = [biased[e:e + 1, :] for e in range(N_EXPERTS)]
    selected = []
    group_score = []
    for gi in range(N_GROUPS):
        r = rows[gi * EXPERTS_PER_GROUP:(gi + 1) * EXPERTS_PER_GROUP]
        total = None
        for i in range(EXPERTS_PER_GROUP):
            rank = None
            for j in range(EXPERTS_PER_GROUP):
                if j == i:
                    continue
                ahead = (r[j] > r[i]) if j > i else (r[j] >= r[i])
                ahead = jnp.where(ahead, 1.0, 0.0)
                rank = ahead if rank is None else rank + ahead
            sel = rank < 1.5
            selected.append(sel)
            contrib = jnp.where(sel, r[i], 0.0)
            total = contrib if total is None else total + contrib
        group_score.append(total)
    best = group_score[0]
    best_group = jnp.zeros_like(best)
    for gi in range(1, N_GROUPS):
        better = group_score[gi] > best
        best_group = jnp.where(better, float(gi), best_group)
        best = jnp.where(better, group_score[gi], best)
    picked = []
    chosen = []
    den = None
    for e in range(N_EXPERTS):
        in_group = best_group == float(e // EXPERTS_PER_GROUP)
        use = jnp.where(selected[e], jnp.where(in_group, 1.0, 0.0), 0.0)
        w = use * scores[e:e + 1, :]
        chosen.append(use)
        picked.append(w)
        den = w if den is None else den + w
    lanes = 128
    n_blk = N_TOK // lanes
    li = lax.broadcasted_iota(jnp.int32, (lanes, lanes), 0)
    lj = lax.broadcasted_iota(jnp.int32, (lanes, lanes), 1)
    prefix = jnp.where(li <= lj, 1.0, 0.0).astype(BF16)
    carry = jnp.zeros((N_EXPERTS, 1), F32)
    for blk in range(n_blk):
        cols = slice(blk * lanes, (blk + 1) * lanes)
        m = jnp.concatenate([chosen[e][:, cols] for e in range(N_EXPERTS)], axis=0)
        incl = jnp.dot(m.astype(BF16), prefix, preferred_element_type=F32)
        rank_ref[:, cols] = incl - m + carry
        carry = carry + incl[:, lanes - 1:lanes]
    count = carry
    padded = jnp.floor((count + float(MOE_TILE - 1)) * (1.0 / MOE_TILE)) * float(MOE_TILE)
    erow = lax.broadcasted_iota(jnp.int32, (N_EXPERTS, 1), 0)
    offset = jnp.zeros((N_EXPERTS, 1), F32)
    for e in range(N_EXPERTS - 1):
        offset = offset + jnp.where(erow > e, padded[e:e + 1, :], 0.0)
    seen = jnp.zeros_like(den)
    pos_a = jnp.zeros_like(den)
    pos_b = jnp.zeros_like(den)
    w_a = jnp.zeros_like(den)
    w_b = jnp.zeros_like(den)
    for e in range(N_EXPERTS):
        pos_e = rank_ref[e:e + 1, :] + offset[e:e + 1, :]
        gate_e = picked[e] / den
        first = jnp.where(seen < 0.5, chosen[e], 0.0) > 0.5
        second = jnp.where(seen > 0.5, chosen[e], 0.0) > 0.5
        pos_a = jnp.where(first, pos_e, pos_a)
        w_a = jnp.where(first, gate_e, w_a)
        pos_b = jnp.where(second, pos_e, pos_b)
        w_b = jnp.where(second, gate_e, w_b)
        seen = seen + chosen[e]
    pos_ref[0:1, :] = pos_a.astype(jnp.int32)
    pos_ref[1:2, :] = pos_b.astype(jnp.int32)
    w_ref[0:1, :] = w_a
    w_ref[1:2, :] = w_b
    start = (lax.broadcasted_iota(jnp.int32, (N_EXPERTS, lanes), 1) * MOE_TILE).astype(F32)
    end = offset + padded
    tile_expert = jnp.sum(jnp.where(end <= start, 1.0, 0.0), axis=0, keepdims=True)
    inside = (offset <= start) & (start < end)
    real = jnp.clip(count - (start - offset), 0.0, float(MOE_TILE))
    tile_rows = jnp.sum(jnp.where(inside, real, 0.0), axis=0, keepdims=True)
    plan_ref[0:1, :] = jnp.minimum(tile_expert, float(N_EXPERTS - 1)).astype(jnp.int32)
    plan_ref[1:2, :] = tile_rows.astype(jnp.int32)


def _router(logits_t, router_b):
    whole = lambda shape: pl.BlockSpec(shape, lambda i: (0, 0))
    return pl.pallas_call(
        _router_kernel,
        grid=(1,),
        in_specs=[whole((N_EXPERTS, N_TOK)), whole((N_EXPERTS, 1))],
        out_specs=(whole((2, N_TOK)), whole((2, N_TOK)), whole((2, 128))),
        out_shape=(jax.ShapeDtypeStruct((2, N_TOK), jnp.int32),
                   jax.ShapeDtypeStruct((2, N_TOK), F32),
                   jax.ShapeDtypeStruct((2, 128), jnp.int32)),
        scratch_shapes=[pltpu.VMEM((N_EXPERTS, N_TOK), F32)],
        compiler_params=_params("arbitrary"),
        name="router",
    )(logits_t, router_b.reshape(N_EXPERTS, 1))


def _sc_mesh():
    return plsc.VectorSubcoreMesh(core_axis_name="c", subcore_axis_name="s")


def _sc_worker_base():
    return (lax.axis_index("s") * SC_CORES + lax.axis_index("c")) * (N_TOK // SC_WORKERS)


def _moe_dispatch(h, pos_a, pos_b):
    n_chunks = N_TOK // SC_WORKERS // SC_CHUNK

    @functools.partial(
        pl.kernel, mesh=_sc_mesh(),
        out_type=jax.ShapeDtypeStruct((MOE_ROWS, ROW_WORDS), jnp.int32),
        scratch_types=[pltpu.VMEM((SC_CHUNK,), jnp.int32), pltpu.VMEM((SC_CHUNK,), jnp.int32),
                       pltpu.VMEM((SC_CHUNK, ROW_WORDS), jnp.int32)],
        name="moe_dispatch",
    )
    def run(h_hbm, pa_hbm, pb_hbm, xs_hbm, ia_v, ib_v, rows_v):
        base = _sc_worker_base()

        @pl.loop(0, n_chunks)
        def _(ci):
            tok = pl.ds(pl.multiple_of(base + ci * SC_CHUNK, SC_CHUNK), SC_CHUNK)
            pltpu.sync_copy(pa_hbm.at[tok], ia_v)
            pltpu.sync_copy(pb_hbm.at[tok], ib_v)
            pltpu.sync_copy(h_hbm.at[tok], rows_v)
            pltpu.sync_copy(rows_v, xs_hbm.at[ia_v])
            pltpu.sync_copy(rows_v, xs_hbm.at[ib_v])

    return run(h, pos_a, pos_b)


def _moe_collect(ys, pos_a, pos_b):
    n_chunks = N_TOK // SC_WORKERS // SC_CHUNK
    out = jax.ShapeDtypeStruct((N_TOK, ROW_WORDS), jnp.int32)

    @functools.partial(
        pl.kernel, mesh=_sc_mesh(), out_type=(out, out),
        scratch_types=[pltpu.VMEM((SC_CHUNK,), jnp.int32), pltpu.VMEM((SC_CHUNK,), jnp.int32),
                       pltpu.VMEM((SC_CHUNK, ROW_WORDS), jnp.int32)],
        name="moe_collect",
    )
    def run(ys_hbm, pa_hbm, pb_hbm, ya_hbm, yb_hbm, ia_v, ib_v, rows_v):
        base = _sc_worker_base()

        @pl.loop(0, n_chunks)
        def _(ci):
            tok = pl.ds(pl.multiple_of(base + ci * SC_CHUNK, SC_CHUNK), SC_CHUNK)
            pltpu.sync_copy(pa_hbm.at[tok], ia_v)
            pltpu.sync_copy(pb_hbm.at[tok], ib_v)
            pltpu.sync_copy(ys_hbm.at[ia_v], rows_v)
            pltpu.sync_copy(rows_v, ya_hbm.at[tok])
            pltpu.sync_copy(ys_hbm.at[ib_v], rows_v)
            pltpu.sync_copy(rows_v, yb_hbm.at[tok])

    return run(ys, pos_a, pos_b)


def _experts_kernel(plan_ref, xs_ref, wg_ref, wu_ref, wd_ref, y_ref, wgb_ref, wub_ref, wdb_ref):
    j = pl.program_id(0)
    expert = plan_ref[j]
    n_real = plan_ref[128 + j]
    fresh = jnp.logical_or(j == 0, expert != plan_ref[jnp.maximum(j - 1, 0)])

    @pl.when(jnp.logical_and(n_real > 0, fresh))
    def _():
        wgb_ref[...] = wg_ref[...].astype(BF16)
        wub_ref[...] = wu_ref[...].astype(BF16)
        wdb_ref[...] = wd_ref[...].astype(BF16)

    @pl.when(n_real > 0)
    def _():
        row = lax.broadcasted_iota(jnp.int32, xs_ref.shape, 0)
        words = jnp.where(row < n_real, xs_ref[...], 0)
        x = _unpack_rows(words).astype(BF16)
        a = jnp.dot(x, wgb_ref[...], preferred_element_type=F32)
        b = jnp.dot(x, wub_ref[...], preferred_element_type=F32)
        hid = (a * jax.nn.sigmoid(a)) * b
        y_ref[...] = _pack_rows(jnp.dot(hid.astype(BF16), wdb_ref[...], preferred_element_type=F32))


def _experts(plan, xs, w_gate, w_up, w_down, layer):
    wspec = lambda r, c: pl.BlockSpec((None, None, r, c), lambda j, plan: (layer, plan[j], 0, 0))
    return pl.pallas_call(
        _experts_kernel,
        grid_spec=pltpu.PrefetchScalarGridSpec(
            num_scalar_prefetch=1,
            grid=(MOE_ROWS // MOE_TILE,),
            in_specs=[pl.BlockSpec((MOE_TILE, ROW_WORDS), lambda j, plan: (j, 0)),
                      wspec(D_MODEL, D_EXPERT), wspec(D_MODEL, D_EXPERT), wspec(D_EXPERT, D_MODEL)],
            out_specs=pl.BlockSpec((MOE_TILE, ROW_WORDS), lambda j, plan: (j, 0)),
            scratch_shapes=[pltpu.VMEM((D_MODEL, D_EXPERT), BF16), pltpu.VMEM((D_MODEL, D_EXPERT), BF16),
                            pltpu.VMEM((D_EXPERT, D_MODEL), BF16)],
        ),
        out_shape=jax.ShapeDtypeStruct((MOE_ROWS, ROW_WORDS), jnp.int32),
        compiler_params=_params("arbitrary"),
        name="experts",
    )(plan, xs, w_gate, w_up, w_down)


def _combine_kernel(x_ref, ya_ref, yb_ref, wt_ref, mod_ref, o_ref):
    o_ref[...] = _moe_mix(x_ref, ya_ref, yb_ref, wt_ref, mod_ref)


def _combine(x, moe_out, mod_l, tok0, n_tok, block_rows=512):
    ya, yb, w_tok = moe_out
    b0 = tok0 // block_rows
    rows = lambda width: pl.BlockSpec((block_rows, width), lambda i: (b0 + i, 0))
    return pl.pallas_call(
        _combine_kernel,
        grid=(n_tok // block_rows,),
        in_specs=[rows(D_MODEL), rows(ROW_WORDS), rows(ROW_WORDS), rows(TOP_K),
                  pl.BlockSpec((None, 6, D_MODEL), lambda i: (_cond_of_token_block(b0 + i, block_rows), 0, 0))],
        out_specs=pl.BlockSpec((block_rows, D_MODEL), lambda i: (i, 0)),
        out_shape=jax.ShapeDtypeStruct((n_tok, D_MODEL), F32),
        compiler_params=_params("arbitrary"),
        name="combine",
    )(x, ya, yb, w_tok, mod_l)


def _moe(h, logits_t, router_b, w_gate, w_up, w_down, layer):
    pos, w, plan = _router(logits_t, router_b)
    xs = _moe_dispatch(h, pos[0], pos[1])
    ys = _experts(plan.reshape(-1), xs, w_gate, w_up, w_down, layer)
    ya, yb = _moe_collect(ys, pos[0], pos[1])
    return ya, yb, w.T


def _dft_tables(L):
    k = np.arange(L)[:, None]
    m = np.arange(L)[None, :]
    r = (k * m) % (2 * L)
    ang = np.pi * r.astype(np.float64) / L
    fc = np.cos(ang)
    fs = np.sin(ang)
    fs[0, :] = np.where(np.arange(L) % 2 == 0, 1.0, -1.0)
    wk = np.full((L, 1), 1.0 / L)
    wk[0, 0] = 0.5 / L
    gc = (fc * wk).T
    gs = (fs * wk).T
    return [jnp.asarray(t.astype(np.float32)).astype(BF16) for t in (fc, fs, gc, gs)]


def _filter_consts(L):
    t = np.linspace(0.0, 1.0, L, dtype=np.float32)[:, None]
    w = (np.float32(2.0 * np.pi) * np.arange(L, dtype=np.float32)[:, None] / np.float32(L)).astype(np.float32)
    fb = np.linspace(1e-4, HY_BANDS - 1, HY_BANDS, dtype=np.float32)[None, :]
    emb = np.concatenate([t, np.cos(fb * w), -np.sin(fb * w)], axis=-1).astype(np.float32)
    lo = math.log(HY_DECAY_TARGET) / HY_SLOW_PCT
    hi = math.log(HY_DECAY_TARGET) / HY_FAST_PCT
    deltas = np.abs(np.linspace(lo, hi, D_MODEL, dtype=np.float32))
    decay = np.exp(-t * deltas).astype(np.float32)
    return jnp.asarray(emb), jnp.asarray(decay)


def _filter_kernel(emb_ref, w1_ref, b1_ref, w2_ref, b2_ref, fr_ref, w3f_ref, w3b_ref, dec_ref,
                   fc_ref, fs_ref, kr_ref, q_ref, krn_ref):
    fr = fr_ref[...]
    hd = jnp.sin(fr * (jnp.dot(emb_ref[...], w1_ref[...], precision=HIGHEST,
                               preferred_element_type=F32) + b1_ref[...]))
    hd = jnp.sin(fr * (jnp.dot(hd, w2_ref[...], precision=HIGHEST,
                               preferred_element_type=F32) + b2_ref[...]))
    dec = dec_ref[...]
    f = jnp.dot(hd, w3f_ref[...], precision=HIGHEST, preferred_element_type=F32) * dec
    g = jnp.dot(hd, w3b_ref[...], precision=HIGHEST, preferred_element_type=F32) * dec
    row = lax.broadcasted_iota(jnp.int32, f.shape, 0)
    g = jnp.where(row == 0, 0.0, g)
    s = f + g
    d = f - g
    kr = jnp.dot(fc_ref[...], s.astype(BF16), preferred_element_type=F32)
    qq = jnp.dot(fs_ref[...], d.astype(BF16), preferred_element_type=F32)
    alt = jnp.where(row % 2 == 0, 1.0, -1.0)
    nyq = jnp.sum(alt * s, axis=0, keepdims=True)
    kr_ref[...] = kr
    q_ref[...] = jnp.where(row == 0, 0.0, qq)
    krn_ref[...] = jnp.where(row == 0, nyq, kr)


def _hyena_filter_spectrum(L, w1, b1, w2, b2, w3, freq, fc, fs, cblk=256):
    emb, decay = _filter_consts(L)
    ncb = D_MODEL // cblk
    n_emb = 128
    emb = jnp.pad(emb, ((0, 0), (0, n_emb - emb.shape[1])))
    w1 = jnp.pad(w1, ((0, n_emb - w1.shape[0]), (0, 0)))
    full = lambda shape: pl.BlockSpec(shape, lambda j: tuple(0 for _ in shape))
    out_sds = jax.ShapeDtypeStruct((L, D_MODEL), F32)
    out_spec = pl.BlockSpec((L, cblk), lambda j: (0, j))
    return pl.pallas_call(
        _filter_kernel,
        grid=(ncb,),
        in_specs=[
            full((L, n_emb)), full((n_emb, HY_FFN)), full((1, HY_FFN)), full((HY_FFN, HY_FFN)),
            full((1, HY_FFN)), full((1, HY_FFN)),
            pl.BlockSpec((HY_FFN, cblk), lambda j: (0, j)),
            pl.BlockSpec((HY_FFN, cblk), lambda j: (0, ncb + j)),
            pl.BlockSpec((L, cblk), lambda j: (0, j)),
            full((L, L)), full((L, L)),
        ],
        out_specs=(out_spec, out_spec, out_spec),
        out_shape=(out_sds, out_sds, out_sds),
        compiler_params=_params("arbitrary"),
        name=f"hyena_filter_{L}",
    )(emb, w1, b1.reshape(1, HY_FFN), w2, b2.reshape(1, HY_FFN), freq.reshape(1, HY_FFN), w3, w3, decay, fc, fs)


def _hyena_conv_kernel(x0_ref, x1_ref, v_ref, cw0_ref, cw1_ref, cwv_ref, cb0_ref, cb1_ref, cbv_ref,
                       kr_ref, q_ref, krn_ref, ds_ref, fc_ref, fs_ref, gc_ref, gs_ref, *rest):
    o_ref = rest[-1]
    L = x0_ref.shape[0]
    row = lax.broadcasted_iota(jnp.int32, x0_ref.shape, 0)

    def short_conv(u_ref, w_ref, b_ref):
        u = u_ref[...].astype(F32)
        w = w_ref[...]
        prev = jnp.where(row == 0, 0.0, pltpu.roll(u, 1, axis=0))
        nxt = jnp.where(row == L - 1, 0.0, pltpu.roll(u, L - 1, axis=0))
        return prev * w[0:1, :] + u * w[1:2, :] + nxt * w[2:3, :] + b_ref[...]

    x0 = short_conv(x0_ref, cw0_ref, cb0_ref)
    x1 = short_conv(x1_ref, cw1_ref, cb1_ref)
    v = short_conv(v_ref, cwv_ref, cbv_ref)
    zz = v * x1
    zb = zz.astype(BF16)
    ur = jnp.dot(fc_ref[...], zb, preferred_element_type=F32)
    p = jnp.dot(fs_ref[...], zb, preferred_element_type=F32)
    qq = q_ref[...]
    yr = ur * kr_ref[...] - p * qq
    yw = ur * qq + p * krn_ref[...]
    y = jnp.dot(gc_ref[...], yr.astype(BF16), preferred_element_type=F32)
    y = y + jnp.dot(gs_ref[...], yw.astype(BF16), preferred_element_type=F32)
    o_ref[...] = (x0 * (y + zz * ds_ref[...])).astype(o_ref.dtype)


def _hyena_conv(u, conv_w, conv_b, dskip, spectrum, tables, shared, *, latent):
    L = LATENT_LEN if latent else PROMPT_LEN
    n_seq = N_LATENT_SEQ if latent else N_PROMPT_SEQ
    cblk = 256 if latent else 512
    ncb = D_MODEL // cblk
    row0 = (N_PROMPT_TOK // L) if latent else 0
    kr, qq, krn = spectrum
    fc, fs, gc, gs = tables

    def part(p, rows):
        return pl.BlockSpec((rows, cblk), lambda j, s: (0 if rows != L else row0 + s, p * ncb + j))

    def const_cols(rows):
        return pl.BlockSpec((rows, cblk), lambda j, s: (0, j))

    mat = pl.BlockSpec((L, L), lambda j, s: (0, 0))
    conv_b2 = conv_b.reshape(1, 3 * D_MODEL)
    in_specs = [part(0, L), part(1, L), part(2, L),
                part(0, 3), part(1, 3), part(2, 3),
                part(0, 1), part(1, 1), part(2, 1),
                const_cols(L), const_cols(L), const_cols(L), const_cols(1),
                mat, mat, mat, mat]
    args = [u, u, u, conv_w, conv_w, conv_w, conv_b2, conv_b2, conv_b2,
            kr, qq, krn, dskip.reshape(1, D_MODEL), fc, fs, gc, gs]
    aliases = {}
    if latent:
        in_specs.append(pl.BlockSpec(memory_space=pl.ANY))
        args.append(shared)
        aliases = {len(args) - 1: 0}
    return pl.pallas_call(
        _hyena_conv_kernel,
        grid=(ncb, n_seq),
        in_specs=in_specs,
        out_specs=pl.BlockSpec((L, cblk), lambda j, s: (row0 + s, j)),
        out_shape=jax.ShapeDtypeStruct((N_TOK, D_MODEL), BF16),
        input_output_aliases=aliases,
        compiler_params=_params("arbitrary", "arbitrary"),
        name="hyena_conv_latent" if latent else "hyena_conv_prompt",
    )(*args)


def kernel(x_prompt, x_sample, cache_k, cache_v, state_hgrn, c, c_ctx, norm_g, mod_w, mod_b, ab_in_w, hgrn_lb, hgrn_onorm_g, attn_qnorm_g, attn_knorm_g, ab_out_w, hy_in_w, hy_in_b, hy_conv_w, hy_conv_b, hy_f_w1, hy_f_b1, hy_f_w2, hy_f_b2, hy_f_w3, hy_f_freq, hy_dskip, hy_out_w, router_w, router_b, moe_w_gate, moe_w_up, moe_w_down):
    xp = x_prompt.reshape(N_PROMPT_TOK, D_MODEL)
    xl = x_sample.reshape(N_LATENT_TOK, D_MODEL)
    cond = jnp.concatenate([c_ctx[None, :], c, jnp.zeros((N_COND - 1 - N_LATENT_SEQ, D_MODEL), F32)], axis=0)
    mod = _modulation(cond, mod_w, mod_b)
    router_wp = jnp.pad(router_w, ((0, 0), (0, ROUTER_LANES - N_EXPERTS)))

    z = _in_proj0(xp, xl, norm_g[0, 0], mod[0], ab_in_w[0])
    o_a, new_state = _hgrn(z, hgrn_lb, hgrn_onorm_g[0], None, None, latent=False)
    o_a = _hgrn(z, hgrn_lb, hgrn_onorm_g[0], state_hgrn, o_a, latent=True)
    o_b, k_prompt = _attention_prompt(z, attn_qnorm_g[0], attn_knorm_g[0])
    o_b = _attention_latent(z, attn_qnorm_g[0], attn_knorm_g[0], cache_k, cache_v, o_b)
    x, h, logits_t = _out_proj([o_a, o_b], ab_out_w[0], (xp, xl), norm_g[0, 1], mod[0], router_wp)
    moe_out = _moe(h, logits_t, router_b, moe_w_gate, moe_w_up, moe_w_down, 0)

    x, u = _in_proj1(x, moe_out, mod[0], norm_g[1, 0], mod[1], hy_in_w[0], hy_in_b[0])
    pre = None
    for latent in (False, True):
        L = LATENT_LEN if latent else PROMPT_LEN
        tables = _dft_tables(L)
        spectrum = _hyena_filter_spectrum(L, hy_f_w1[0], hy_f_b1[0], hy_f_w2[0], hy_f_b2[0], hy_f_w3[0],
                                          hy_f_freq[0], tables[0], tables[1])
        pre = _hyena_conv(u, hy_conv_w[0], hy_conv_b[0], hy_dskip[0], spectrum, tables, pre, latent=latent)
    x, h, logits_t = _out_proj([pre], hy_out_w[0], (x,), norm_g[1, 1], mod[1], router_wp)
    moe_out = _moe(h, logits_t, router_b, moe_w_gate, moe_w_up, moe_w_down, 1)

    y_prompt = _combine(x, moe_out, mod[1], 0, N_PROMPT_TOK).reshape(N_PROMPT_SEQ, PROMPT_LEN, D_MODEL)
    y_sample = _combine(x, moe_out, mod[1], N_PROMPT_TOK, N_LATENT_TOK).reshape(N_LATENT_SEQ, LATENT_LEN, D_MODEL)
    kv_shape = (N_PROMPT_SEQ, PROMPT_LEN, KV_HEADS, HEAD_DIM)
    new_k = k_prompt.reshape(kv_shape).transpose(0, 2, 1, 3)[:, None]
    v_col = 5 * A_WIDTH + (Q_HEADS + KV_HEADS) * HEAD_DIM
    new_v = z[:N_PROMPT_TOK, v_col:].astype(F32).reshape(kv_shape).transpose(0, 2, 1, 3)[:, None]
    return (y_prompt, y_sample, new_k, new_v, new_state)
```

```python
import functools
import math

import numpy as np
import jax
import jax.numpy as jnp
from jax import lax
from jax.experimental import pallas as pl
from jax.experimental.pallas import tpu as pltpu
from jax.experimental.pallas import tpu_sc as plsc

F32 = jnp.float32
BF16 = jnp.bfloat16
HIGHEST = lax.Precision.HIGHEST

D_MODEL = 1024
N_PROMPT_SEQ = 32
PROMPT_LEN = 256
N_LATENT_SEQ = 2
LATENT_LEN = 1024
PAST_LEN = 512
GRID_W = 64
N_PROMPT_TOK = N_PROMPT_SEQ * PROMPT_LEN
N_LATENT_TOK = N_LATENT_SEQ * LATENT_LEN
N_TOK = N_PROMPT_TOK + N_LATENT_TOK
N_COND = 8
EPS = 1e-6

A_WIDTH = 512
A_HEADS = 4
A_DK = 128
CHUNK = 64
HGRN_BLOCK = 256
HGRN_HEADS_PER_STEP = 4
HEAD_DIM = 64
Q_HEADS = 8
KV_HEADS = 2
Q_PER_KV = Q_HEADS // KV_HEADS
Q_BLOCK = 256
ROPE_THETA = 10000.0
ROPE_PAIRS = HEAD_DIM // 4
AB_IN = 5 * A_WIDTH + (Q_HEADS + 2 * KV_HEADS) * HEAD_DIM

HY_BANDS = 16
HY_FFN = 64
HY_DECAY_TARGET = 1e-2
HY_FAST_PCT = 0.3
HY_SLOW_PCT = 1.5

N_EXPERTS = 16
N_GROUPS = 4
EXPERTS_PER_GROUP = 4
TOP_K = 2
D_EXPERT = 512
ROUTER_LANES = 128
MOE_TILE = 512
MOE_ROWS = N_TOK * TOP_K + N_EXPERTS * MOE_TILE

SC_CORES = 2
SC_WORKERS = 32
SC_CHUNK = 80
ROW_WORDS = D_MODEL // 2

VMEM_LIMIT = 56 * 1024 * 1024


def _params(*sem):
    return pltpu.CompilerParams(dimension_semantics=sem, vmem_limit_bytes=VMEM_LIMIT)


def _pack_rows(x):
    n = x.shape[1] // 2
    bits = pltpu.bitcast(x.astype(BF16).astype(F32), jnp.uint32)
    return pltpu.bitcast(bits[:, :n] | (bits[:, n:] >> 16), jnp.int32)


def _unpack_rows(p):
    bits = pltpu.bitcast(p, jnp.uint32)
    hi = pltpu.bitcast(bits & jnp.uint32(0xFFFF0000), F32)
    lo = pltpu.bitcast(bits << 16, F32)
    return jnp.concatenate([hi, lo], axis=1)


def _cond_of_token_block(i, block_rows):
    start = i * block_rows
    return jnp.where(start < N_PROMPT_TOK, 0, 1 + (start - N_PROMPT_TOK) // LATENT_LEN)


def _mod_kernel(cond_ref, w_ref, b_ref, o_ref):
    cnd = cond_ref[...]
    s = cnd * jax.nn.sigmoid(cnd)
    o_ref[...] = jnp.dot(s, w_ref[...], precision=HIGHEST, preferred_element_type=F32) + b_ref[...]


def _modulation(cond, mod_w, mod_b):
    depth = mod_w.shape[0]
    n_chunk = 6
    out = pl.pallas_call(
        _mod_kernel,
        grid=(depth, n_chunk),
        in_specs=[
            pl.BlockSpec((N_COND, D_MODEL), lambda l, j: (0, 0)),
            pl.BlockSpec((None, D_MODEL, D_MODEL), lambda l, j: (l, 0, j)),
            pl.BlockSpec((None, 1, D_MODEL), lambda l, j: (l, 0, j)),
        ],
        out_specs=pl.BlockSpec((None, N_COND, D_MODEL), lambda l, j: (l, 0, j)),
        out_shape=jax.ShapeDtypeStruct((depth, N_COND, n_chunk * D_MODEL), F32),
        compiler_params=_params("arbitrary", "arbitrary"),
        name="modulation",
    )(cond, mod_w, mod_b.reshape(depth, 1, n_chunk * D_MODEL))
    return out.reshape(depth, N_COND, n_chunk, D_MODEL)


def _modulated_norm(x, g, mod, shift_row, scale_row):
    ms = jnp.mean(x * x, axis=-1, keepdims=True)
    y = x * lax.rsqrt(ms + EPS) * g
    return y * (1.0 + mod[scale_row:scale_row + 1, :]) + mod[shift_row:shift_row + 1, :]


def _trunk_specs(block_rows, width):
    n_prompt_blocks = N_PROMPT_TOK // block_rows
    return (pl.BlockSpec((block_rows, width), lambda i: (jnp.minimum(i, n_prompt_blocks - 1), 0)),
            pl.BlockSpec((block_rows, width), lambda i: (jnp.maximum(i - n_prompt_blocks, 0), 0)))


def _select_trunk(p_ref, l_ref):
    block_rows = p_ref.shape[0]
    return jnp.where(pl.program_id(0) < N_PROMPT_TOK // block_rows, p_ref[...], l_ref[...])


def _cast_once(w_ref, wb_ref):
    @pl.when(pl.program_id(0) == 0)
    def _():
        wb_ref[...] = w_ref[...].astype(BF16)


def _resident(shape):
    return pl.BlockSpec(shape, lambda i: tuple(0 for _ in shape), pipeline_mode=pl.Buffered(1))


def _mod_spec(block_rows):
    return pl.BlockSpec((None, 6, D_MODEL), lambda i: (_cond_of_token_block(i, block_rows), 0, 0))


def _in_proj0_kernel(xp_ref, xl_ref, g_ref, mod_ref, w_ref, o_ref, wb_ref):
    _cast_once(w_ref, wb_ref)
    h = _modulated_norm(_select_trunk(xp_ref, xl_ref), g_ref[...], mod_ref[...], 0, 1)
    o_ref[...] = jnp.dot(h.astype(BF16), wb_ref[...], preferred_element_type=F32).astype(o_ref.dtype)


def _in_proj0(x_prompt, x_latent, g, mod_l, w, block_rows=512):
    n = w.shape[1]
    return pl.pallas_call(
        _in_proj0_kernel,
        grid=(N_TOK // block_rows,),
        in_specs=[*_trunk_specs(block_rows, D_MODEL), _resident((1, D_MODEL)), _mod_spec(block_rows),
                  _resident((D_MODEL, n))],
        out_specs=pl.BlockSpec((block_rows, n), lambda i: (i, 0)),
        out_shape=jax.ShapeDtypeStruct((N_TOK, n), BF16),
        scratch_shapes=[pltpu.VMEM((D_MODEL, n), BF16)],
        compiler_params=_params("arbitrary"),
        name="in_proj0",
    )(x_prompt, x_latent, g.reshape(1, D_MODEL), mod_l, w)


def _moe_mix(x_ref, ya_ref, yb_ref, wt_ref, mod_ref):
    wt = wt_ref[...]
    mix = wt[:, 0:1] * _unpack_rows(ya_ref[...]) + wt[:, 1:2] * _unpack_rows(yb_ref[...])
    return x_ref[...] + mod_ref[5:6, :] * mix


def _in_proj1_kernel(x_ref, ya_ref, yb_ref, wt_ref, modp_ref, g_ref, mod_ref, w_ref, b_ref, xo_ref, o_ref, wb_ref):
    _cast_once(w_ref, wb_ref)
    x = _moe_mix(x_ref, ya_ref, yb_ref, wt_ref, modp_ref)
    xo_ref[...] = x
    h = _modulated_norm(x, g_ref[...], mod_ref[...], 0, 1)
    u = jnp.dot(h.astype(BF16), wb_ref[...], preferred_element_type=F32) + b_ref[...]
    o_ref[...] = u.astype(o_ref.dtype)


def _in_proj1(x, moe_out, mod_prev, g, mod_l, w, bias, block_rows=512):
    ya, yb, w_tok = moe_out
    n = w.shape[1]
    tok = pl.BlockSpec((block_rows, D_MODEL), lambda i: (i, 0))
    packed = pl.BlockSpec((block_rows, ROW_WORDS), lambda i: (i, 0))
    return pl.pallas_call(
        _in_proj1_kernel,
        grid=(N_TOK // block_rows,),
        in_specs=[tok, packed, packed, pl.BlockSpec((block_rows, TOP_K), lambda i: (i, 0)), _mod_spec(block_rows),
                  _resident((1, D_MODEL)), _mod_spec(block_rows), _resident((D_MODEL, n)), _resident((1, n))],
        out_specs=(tok, pl.BlockSpec((block_rows, n), lambda i: (i, 0))),
        out_shape=(jax.ShapeDtypeStruct((N_TOK, D_MODEL), F32), jax.ShapeDtypeStruct((N_TOK, n), BF16)),
        scratch_shapes=[pltpu.VMEM((D_MODEL, n), BF16)],
        compiler_params=_params("arbitrary"),
        name="in_proj1",
    )(x, ya, yb, w_tok, mod_prev, g.reshape(1, D_MODEL), mod_l, w, bias.reshape(1, n))


def _hgrn_kernel(*refs, seq_len, with_state):
    if with_state:
        (q_ref, zf_ref, zb_ref, i_ref, ga_ref, lb_ref, og_ref, s0_ref, shared_ref, o_ref, of_ref, ob_ref) = refs
    else:
        (q_ref, zf_ref, zb_ref, i_ref, ga_ref, lb_ref, og_ref, o_ref, s_ref, of_ref, ob_ref) = refs
    n_blocks = seq_len // HGRN_BLOCK
    chunks_per_block = HGRN_BLOCK // CHUNK

    lbr = lb_ref[...]
    mx = jnp.maximum(lbr[0], lbr[1])
    e0 = jnp.exp(lbr[0] - mx)
    e1 = jnp.exp(lbr[1] - mx)
    lb = e0 / (e0 + e1)

    row = lax.broadcasted_iota(jnp.int32, (HGRN_BLOCK, HGRN_BLOCK), 0)
    col = lax.broadcasted_iota(jnp.int32, (HGRN_BLOCK, HGRN_BLOCK), 1)
    same_chunk = (row // CHUNK) == (col // CHUNK)
    nt = (((1,), (1,)), ((), ()))
    tn = (((0,), (0,)), ((), ()))

    def per_chunk_row(x, idx):
        return jnp.concatenate(
            [jnp.broadcast_to(x[n * CHUNK + idx:n * CHUNK + idx + 1, :], (CHUNK, x.shape[1]))
             for n in range(chunks_per_block)], axis=0)

    def in_chunk_cumsum(tri, x):
        hi = x.astype(BF16)
        lo = (x - hi.astype(F32)).astype(BF16)
        return jnp.dot(tri, hi, preferred_element_type=F32) + jnp.dot(tri, lo, preferred_element_type=F32)

    def block(blk, cols, st, z_ref, lbd, forward, out_ref):
        rows = slice(blk * HGRN_BLOCK, (blk + 1) * HGRN_BLOCK)
        keep = (same_chunk & (col <= row)) if forward else (same_chunk & (col >= row))
        tri = jnp.where(keep, 1.0, 0.0).astype(BF16)
        mid = CHUNK // 2 if forward else CHUNK - 1 - CHUNK // 2
        last = CHUNK - 1 if forward else 0
        f = lbd + (1.0 - lbd) * jax.nn.sigmoid(z_ref[rows, cols].astype(F32))
        lf = jnp.log(f)
        k = 1.0 - f
        q = q_ref[rows, cols].astype(F32)
        vb = i_ref[rows, cols].astype(BF16)
        b = in_chunk_cumsum(tri, lf)
        bm = per_chunk_row(b, mid)
        bl = per_chunk_row(b, last)
        qe = (q * jnp.exp(b - bm)).astype(BF16)
        ke = (k * jnp.exp(bm - b)).astype(BF16)
        att = lax.dot_general(qe, ke, nt, preferred_element_type=F32)
        att = jnp.where(keep, att, 0.0)
        o_intra = jnp.dot(att.astype(BF16), vb, preferred_element_type=F32)
        qb = (q * jnp.exp(b)).astype(BF16)
        ks = (k * jnp.exp(bl - b)).astype(BF16)
        decay = jnp.exp(bl)
        order = range(chunks_per_block) if forward else range(chunks_per_block - 1, -1, -1)
        o_inter = [None] * chunks_per_block
        for n in order:
            cr = slice(n * CHUNK, (n + 1) * CHUNK)
            o_inter[n] = lax.dot_general(qb[cr], st.astype(BF16), nt, preferred_element_type=F32)
            upd = lax.dot_general(vb[cr], ks[cr], tn, preferred_element_type=F32)
            st = st * decay[n * CHUNK:n * CHUNK + 1, :] + upd
        out_ref[rows, cols] = o_intra + jnp.concatenate(o_inter, axis=0)
        return st

    for hd in range(q_ref.shape[1] // A_DK):
        cols = slice(hd * A_DK, (hd + 1) * A_DK)
        if with_state:
            st_f, st_b = s0_ref[0, hd].T, s0_ref[1, hd].T
        else:
            st_f, st_b = jnp.zeros((A_DK, A_DK), F32), jnp.zeros((A_DK, A_DK), F32)
        for step in range(n_blocks):
            st_f = block(step, cols, st_f, zf_ref, lb[0:1, cols], True, of_ref)
            st_b = block(n_blocks - 1 - step, cols, st_b, zb_ref, lb[1:2, cols], False, ob_ref)
        if not with_state:
            s_ref[0, hd] = st_f.T
            s_ref[1, hd] = st_b.T
        o = of_ref[:, cols] + ob_ref[:, cols]
        o = o * lax.rsqrt(jnp.mean(o * o, axis=-1, keepdims=True) + EPS) * og_ref[:, cols]
        ga = ga_ref[:, cols].astype(F32)
        o_ref[:, cols] = (o * (ga * jax.nn.sigmoid(ga))).astype(o_ref.dtype)


def _hgrn(z, hgrn_lb, onorm_g, state, shared, *, latent):
    seq_len = LATENT_LEN if latent else PROMPT_LEN
    n_seq = N_LATENT_SEQ if latent else N_PROMPT_SEQ
    row0 = (N_PROMPT_TOK // seq_len) if latent else 0

    hw = HGRN_HEADS_PER_STEP * A_DK
    n_hg = A_HEADS // HGRN_HEADS_PER_STEP

    def zspec(part):
        return pl.BlockSpec((seq_len, hw), lambda s, h: (row0 + s, part * n_hg + h))

    in_specs = [zspec(0), zspec(1), zspec(2), zspec(3), zspec(4),
                pl.BlockSpec((2, 2, hw), lambda s, h: (0, 0, h)),
                pl.BlockSpec((1, hw), lambda s, h: (0, h))]
    args = [z, z, z, z, z, hgrn_lb, onorm_g.reshape(1, A_WIDTH)]
    state_spec = pl.BlockSpec((None, None, 2, HGRN_HEADS_PER_STEP, A_DK, A_DK), lambda s, h: (s, 0, 0, h, 0, 0))
    o_shape = jax.ShapeDtypeStruct((N_TOK, A_WIDTH), BF16)
    o_spec = pl.BlockSpec((seq_len, hw), lambda s, h: (row0 + s, h))
    aliases = {}
    if latent:
        in_specs += [state_spec, pl.BlockSpec(memory_space=pl.ANY)]
        args += [state, shared]
        aliases = {len(args) - 1: 0}
        out_shape, out_specs = o_shape, o_spec
    else:
        out_shape = (o_shape, jax.ShapeDtypeStruct((n_seq, 1, 2, A_HEADS, A_DK, A_DK), F32))
        out_specs = (o_spec, state_spec)
    return pl.pallas_call(
        functools.partial(_hgrn_kernel, seq_len=seq_len, with_state=latent),
        grid=(n_seq, n_hg),
        in_specs=in_specs,
        out_specs=out_specs,
        out_shape=out_shape,
        input_output_aliases=aliases,
        scratch_shapes=[pltpu.VMEM((seq_len, hw), F32), pltpu.VMEM((seq_len, hw), F32)],
        compiler_params=_params("arbitrary", "arbitrary"),
        name="hgrn_latent" if latent else "hgrn_prompt",
    )(*args)


def _rope_tables():
    pos = np.arange(LATENT_LEN)
    row, colp = pos // GRID_W, pos % GRID_W
    inv = ROPE_THETA ** (-np.arange(ROPE_PAIRS, dtype=np.float32) / ROPE_PAIRS)
    inv = inv.astype(np.float32)
    ang_r = (row.astype(np.float32)[:, None] * inv).astype(np.float32)
    ang_c = (colp.astype(np.float32)[:, None] * inv).astype(np.float32)
    cos = np.concatenate([np.cos(ang_r), np.cos(ang_r), np.cos(ang_c), np.cos(ang_c)], axis=1)
    sin = np.concatenate([-np.sin(ang_r), np.sin(ang_r), -np.sin(ang_c), np.sin(ang_c)], axis=1)
    perm = np.zeros((HEAD_DIM, HEAD_DIM), np.float32)
    for d in range(HEAD_DIM):
        partner = d + ROPE_PAIRS if (d // ROPE_PAIRS) % 2 == 0 else d - ROPE_PAIRS
        perm[partner, d] = 1.0
    return cos.astype(np.float32), sin.astype(np.float32), perm


def _attn_kernel(*refs, latent):
    if latent:
        (q_ref, k_ref, v_ref, qg_ref, kg_ref, cosq_ref, sinq_ref, cosk_ref, sink_ref, perm_ref,
         ck_ref, cv_ref, shared_ref, o_ref) = refs
    else:
        (q_ref, k_ref, v_ref, qg_ref, kg_ref, o_ref, kout_ref) = refs

    def head(x_ref, h, g, cos, sin):
        xh = x_ref[:, h * HEAD_DIM:(h + 1) * HEAD_DIM].astype(F32)
        xh = xh * lax.rsqrt(jnp.mean(xh * xh, axis=-1, keepdims=True) + EPS) * g
        if latent:
            swapped = jnp.dot(xh, perm_ref[...], precision=HIGHEST, preferred_element_type=F32)
            xh = xh * cos + swapped * sin
        return xh

    qg = qg_ref[...]
    kg = kg_ref[...]
    cq = sq = ck = sk = None
    if latent:
        cq, sq, ck, sk = cosq_ref[...], sinq_ref[...], cosk_ref[...], sink_ref[...]
    scale = HEAD_DIM ** -0.5
    n_q = q_ref.shape[0]
    for j in range(KV_HEADS):
        kh = head(k_ref, j, kg, ck, sk)
        if not latent:
            kout_ref[:, j * HEAD_DIM:(j + 1) * HEAD_DIM] = kh
        vh = v_ref[:, j * HEAD_DIM:(j + 1) * HEAD_DIM]
        qs = jnp.concatenate(
            [head(q_ref, j * Q_PER_KV + t, qg, cq, sq) * scale for t in range(Q_PER_KV)], axis=0)
        qs = qs.astype(BF16)
        nt = (((1,), (1,)), ((), ()))
        s_new = lax.dot_general(qs, kh.astype(BF16), nt, preferred_element_type=F32)
        m = jnp.max(s_new, axis=-1, keepdims=True)
        if latent:
            s_old = lax.dot_general(qs, ck_ref[j].astype(BF16), nt, preferred_element_type=F32)
            m = jnp.maximum(m, jnp.max(s_old, axis=-1, keepdims=True))
        p_new = jnp.exp(s_new - m)
        den = jnp.sum(p_new, axis=-1, keepdims=True)
        acc = jnp.dot(p_new.astype(BF16), vh.astype(BF16), preferred_element_type=F32)
        if latent:
            p_old = jnp.exp(s_old - m)
            den = den + jnp.sum(p_old, axis=-1, keepdims=True)
            acc = acc + jnp.dot(p_old.astype(BF16), cv_ref[j].astype(BF16), preferred_element_type=F32)
        out = acc / den
        for t in range(Q_PER_KV):
            hq = j * Q_PER_KV + t
            o_ref[:, hq * HEAD_DIM:(hq + 1) * HEAD_DIM] = out[t * n_q:(t + 1) * n_q, :].astype(o_ref.dtype)


def _attention_prompt(z, qn_g, kn_g):
    L = PROMPT_LEN
    q_col = (5 * A_WIDTH) // (Q_HEADS * HEAD_DIM)
    k_col = (5 * A_WIDTH + Q_HEADS * HEAD_DIM) // (KV_HEADS * HEAD_DIM)
    kv_w = KV_HEADS * HEAD_DIM
    return pl.pallas_call(
        functools.partial(_attn_kernel, latent=False),
        grid=(N_PROMPT_SEQ,),
        in_specs=[
            pl.BlockSpec((L, Q_HEADS * HEAD_DIM), lambda s: (s, q_col)),
            pl.BlockSpec((L, kv_w), lambda s: (s, k_col)),
            pl.BlockSpec((L, kv_w), lambda s: (s, k_col + 1)),
            pl.BlockSpec((1, HEAD_DIM), lambda s: (0, 0)),
            pl.BlockSpec((1, HEAD_DIM), lambda s: (0, 0)),
        ],
        out_specs=(pl.BlockSpec((L, Q_HEADS * HEAD_DIM), lambda s: (s, 0)),
                   pl.BlockSpec((L, kv_w), lambda s: (s, 0))),
        out_shape=(jax.ShapeDtypeStruct((N_TOK, Q_HEADS * HEAD_DIM), BF16),
                   jax.ShapeDtypeStruct((N_PROMPT_TOK, kv_w), F32)),
        compiler_params=_params("arbitrary"),
        name="attn_prompt",
    )(z, z, z, qn_g.reshape(1, HEAD_DIM), kn_g.reshape(1, HEAD_DIM))


def _attention_latent(z, qn_g, kn_g, cache_k, cache_v, shared):
    L = LATENT_LEN
    nqb = L // Q_BLOCK
    q_col = (5 * A_WIDTH) // (Q_HEADS * HEAD_DIM)
    k_col = (5 * A_WIDTH + Q_HEADS * HEAD_DIM) // (KV_HEADS * HEAD_DIM)
    kv_w = KV_HEADS * HEAD_DIM
    qrow0 = N_PROMPT_TOK // Q_BLOCK
    krow0 = N_PROMPT_TOK // L
    cos, sin, perm = _rope_tables()
    cache_spec = pl.BlockSpec((None, None, KV_HEADS, PAST_LEN, HEAD_DIM), lambda s, b: (s, 0, 0, 0, 0))
    return pl.pallas_call(
        functools.partial(_attn_kernel, latent=True),
        grid=(N_LATENT_SEQ, nqb),
        in_specs=[
            pl.BlockSpec((Q_BLOCK, Q_HEADS * HEAD_DIM), lambda s, b: (qrow0 + s * nqb + b, q_col)),
            pl.BlockSpec((L, kv_w), lambda s, b: (krow0 + s, k_col)),
            pl.BlockSpec((L, kv_w), lambda s, b: (krow0 + s, k_col + 1)),
            pl.BlockSpec((1, HEAD_DIM), lambda s, b: (0, 0)),
            pl.BlockSpec((1, HEAD_DIM), lambda s, b: (0, 0)),
            pl.BlockSpec((Q_BLOCK, HEAD_DIM), lambda s, b: (b, 0)),
            pl.BlockSpec((Q_BLOCK, HEAD_DIM), lambda s, b: (b, 0)),
            pl.BlockSpec((L, HEAD_DIM), lambda s, b: (0, 0)),
            pl.BlockSpec((L, HEAD_DIM), lambda s, b: (0, 0)),
            pl.BlockSpec((HEAD_DIM, HEAD_DIM), lambda s, b: (0, 0)),
            cache_spec, cache_spec, pl.BlockSpec(memory_space=pl.ANY),
        ],
        out_specs=pl.BlockSpec((Q_BLOCK, Q_HEADS * HEAD_DIM), lambda s, b: (qrow0 + s * nqb + b, 0)),
        out_shape=jax.ShapeDtypeStruct((N_TOK, Q_HEADS * HEAD_DIM), BF16),
        input_output_aliases={12: 0},
        compiler_params=_params("arbitrary", "arbitrary"),
        name="attn_latent",
    )(z, z, z, qn_g.reshape(1, HEAD_DIM), kn_g.reshape(1, HEAD_DIM),
      jnp.asarray(cos), jnp.asarray(sin), jnp.asarray(cos), jnp.asarray(sin), jnp.asarray(perm),
      cache_k, cache_v, shared)


def _out_proj_kernel(*refs, n_act, n_x):
    a_refs = refs[:n_act]
    x_refs = refs[n_act:n_act + n_x]
    g_ref, mod_ref, rw_ref, w_ref, xo_ref, h_ref, lg_ref, wb_ref, rwh_ref, rwl_ref = refs[n_act + n_x:]
    _cast_once(w_ref, wb_ref)

    @pl.when(pl.program_id(0) == 0)
    def _():
        rw = rw_ref[...]
        hi = rw.astype(BF16)
        rwh_ref[...] = hi
        rwl_ref[...] = (rw - hi.astype(F32)).astype(BF16)

    acc = None
    k0 = 0
    for a_ref in a_refs:
        k1 = k0 + a_ref.shape[1]
        part = jnp.dot(a_ref[...], wb_ref[k0:k1, :], preferred_element_type=F32)
        acc = part if acc is None else acc + part
        k0 = k1
    mod = mod_ref[...]
    x_in = x_refs[0][...] if n_x == 1 else _select_trunk(*x_refs)
    x = x_in + mod[2:3, :] * acc
    xo_ref[...] = x
    h = _modulated_norm(x, g_ref[...], mod, 3, 4)
    h_ref[...] = _pack_rows(h)
    h_hi = h.astype(BF16)
    h_lo = (h - h_hi.astype(F32)).astype(BF16)
    lg = jnp.dot(h_hi, rwh_ref[...], preferred_element_type=F32)
    lg = lg + jnp.dot(h_lo, rwh_ref[...], preferred_element_type=F32)
    lg = lg + jnp.dot(h_hi, rwl_ref[...], preferred_element_type=F32)
    lg_ref[...] = lg.T[:N_EXPERTS, :]


def _out_proj(acts, w, xs, g, mod_l, router_wp, block_rows=256):
    tok = lambda width: pl.BlockSpec((block_rows, width), lambda i: (i, 0))
    in_specs = [tok(a.shape[1]) for a in acts]
    in_specs += [tok(D_MODEL)] if len(xs) == 1 else list(_trunk_specs(block_rows, D_MODEL))
    in_specs += [_resident((1, D_MODEL)), _mod_spec(block_rows), _resident((D_MODEL, ROUTER_LANES)),
                 _resident(w.shape)]
    return pl.pallas_call(
        functools.partial(_out_proj_kernel, n_act=len(acts), n_x=len(xs)),
        grid=(N_TOK // block_rows,),
        in_specs=in_specs,
        out_specs=(tok(D_MODEL), tok(ROW_WORDS), pl.BlockSpec((N_EXPERTS, block_rows), lambda i: (0, i))),
        out_shape=(jax.ShapeDtypeStruct((N_TOK, D_MODEL), F32),
                   jax.ShapeDtypeStruct((N_TOK, ROW_WORDS), jnp.int32),
                   jax.ShapeDtypeStruct((N_EXPERTS, N_TOK), F32)),
        scratch_shapes=[pltpu.VMEM(w.shape, BF16), pltpu.VMEM((D_MODEL, ROUTER_LANES), BF16),
                        pltpu.VMEM((D_MODEL, ROUTER_LANES), BF16)],
        compiler_params=_params("arbitrary"),
        name="out_proj",
    )(*acts, *xs, g.reshape(1, D_MODEL), mod_l, router_wp, w)


def _router_kernel(lg_ref, rb_ref, pos_ref, w_ref, plan_ref, rank_ref):
    lg = lg_ref[...]
    ex = jnp.exp(lg - jnp.max(lg, axis=0, keepdims=True))
    scores = ex / jnp.sum(ex, axis=0, keepdims=True)
    biased = scores + rb_ref[...]
    rows = [biased[e:e + 1, :] for e in range(N_EXPERTS)]
    selected = []
    group_score = []
    for gi in range(N_GROUPS):
        r = rows[gi * EXPERTS_PER_GROUP:(gi + 1) * EXPERTS_PER_GROUP]
        total = None
        for i in range(EXPERTS_PER_GROUP):
            rank = None
            for j in range(EXPERTS_PER_GROUP):
                if j == i:
                    continue
                ahead = (r[j] > r[i]) if j > i else (r[j] >= r[i])
                ahead = jnp.where(ahead, 1.0, 0.0)
                rank = ahead if rank is None else rank + ahead
            sel = rank < 1.5
            selected.append(sel)
            contrib = jnp.where(sel, r[i], 0.0)
            total = contrib if total is None else total + contrib
        group_score.append(total)
    best = group_score[0]
    best_group = jnp.zeros_like(best)
    for gi in range(1, N_GROUPS):
        better = group_score[gi] > best
        best_group = jnp.where(better, float(gi), best_group)
        best = jnp.where(better, group_score[gi], best)
    picked = []
    chosen = []
    den = None
    for e in range(N_EXPERTS):
        in_group = best_group == float(e // EXPERTS_PER_GROUP)
        use = jnp.where(selected[e], jnp.where(in_group, 1.0, 0.0), 0.0)
        w = use * scores[e:e + 1, :]
        chosen.append(use)
        picked.append(w)
        den = w if den is None else den + w
    lanes = 128
    n_blk = N_TOK // lanes
    li = lax.broadcasted_iota(jnp.int32, (lanes, lanes), 0)
    lj = lax.broadcasted_iota(jnp.int32, (lanes, lanes), 1)
    prefix = jnp.where(li <= lj, 1.0, 0.0).astype(BF16)
    carry = jnp.zeros((N_EXPERTS, 1), F32)
    for blk in range(n_blk):
        cols = slice(blk * lanes, (blk + 1) * lanes)
        m = jnp.concatenate([chosen[e][:, cols] for e in range(N_EXPERTS)], axis=0)
        incl = jnp.dot(m.astype(BF16), prefix, preferred_element_type=F32)
        rank_ref[:, cols] = incl - m + carry
        carry = carry + incl[:, lanes - 1:lanes]
    count = carry
    padded = jnp.floor((count + float(MOE_TILE - 1)) * (1.0 / MOE_TILE)) * float(MOE_TILE)
    erow = lax.broadcasted_iota(jnp.int32, (N_EXPERTS, 1), 0)
    offset = jnp.zeros((N_EXPERTS, 1), F32)
    for e in range(N_EXPERTS - 1):
        offset = offset + jnp.where(erow > e, padded[e:e + 1, :], 0.0)
    seen = jnp.zeros_like(den)
    pos_a = jnp.zeros_like(den)
    pos_b = jnp.zeros_like(den)
    w_a = jnp.zeros_like(den)
    w_b = jnp.zeros_like(den)
    for e in range(N_EXPERTS):
        pos_e = rank_ref[e:e + 1, :] + offset[e:e + 1, :]
        gate_e = picked[e] / den
        first = jnp.where(seen < 0.5, chosen[e], 0.0) > 0.5
        second = jnp.where(seen > 0.5, chosen[e], 0.0) > 0.5
        pos_a = jnp.where(first, pos_e, pos_a)
        w_a = jnp.where(first, gate_e, w_a)
        pos_b = jnp.where(second, pos_e, pos_b)
        w_b = jnp.where(second, gate_e, w_b)
        seen = seen + chosen[e]
    pos_ref[0:1, :] = pos_a.astype(jnp.int32)
    pos_ref[1:2, :] = pos_b.astype(jnp.int32)
    w_ref[0:1, :] = w_a
    w_ref[1:2, :] = w_b
    start = (lax.broadcasted_iota(jnp.int32, (N_EXPERTS, lanes), 1) * MOE_TILE).astype(F32)
    end = offset + padded
    tile_expert = jnp.sum(jnp.where(end <= start, 1.0, 0.0), axis=0, keepdims=True)
    inside = (offset <= start) & (start < end)
    real = jnp.clip(count - (start - offset), 0.0, float(MOE_TILE))
    tile_rows = jnp.sum(jnp.where(inside, real, 0.0), axis=0, keepdims=True)
    plan_ref[0:1, :] = jnp.minimum(tile_expert, float(N_EXPERTS - 1)).astype(jnp.int32)
    plan_ref[1:2, :] = tile_rows.astype(jnp.int32)


def _router(logits_t, router_b):
    whole = lambda shape: pl.BlockSpec(shape, lambda i: (0, 0))
    return pl.pallas_call(
        _router_kernel,
        grid=(1,),
        in_specs=[whole((N_EXPERTS, N_TOK)), whole((N_EXPERTS, 1))],
        out_specs=(whole((2, N_TOK)), whole((2, N_TOK)), whole((2, 128))),
        out_shape=(jax.ShapeDtypeStruct((2, N_TOK), jnp.int32),
                   jax.ShapeDtypeStruct((2, N_TOK), F32),
                   jax.ShapeDtypeStruct((2, 128), jnp.int32)),
        scratch_shapes=[pltpu.VMEM((N_EXPERTS, N_TOK), F32)],
        compiler_params=_params("arbitrary"),
        name="router",
    )(logits_t, router_b.reshape(N_EXPERTS, 1))


def _sc_mesh():
    return plsc.VectorSubcoreMesh(core_axis_name="c", subcore_axis_name="s")


def _sc_worker_base():
    return (lax.axis_index("s") * SC_CORES + lax.axis_index("c")) * (N_TOK // SC_WORKERS)


def _moe_dispatch(h, pos_a, pos_b):
    n_chunks = N_TOK // SC_WORKERS // SC_CHUNK

    @functools.partial(
        pl.kernel, mesh=_sc_mesh(),
        out_type=jax.ShapeDtypeStruct((MOE_ROWS, ROW_WORDS), jnp.int32),
        scratch_types=[pltpu.VMEM((SC_CHUNK,), jnp.int32), pltpu.VMEM((SC_CHUNK,), jnp.int32),
                       pltpu.VMEM((SC_CHUNK, ROW_WORDS), jnp.int32)],
        name="moe_dispatch",
    )
    def run(h_hbm, pa_hbm, pb_hbm, xs_hbm, ia_v, ib_v, rows_v):
        base = _sc_worker_base()

        @pl.loop(0, n_chunks)
        def _(ci):
            tok = pl.ds(pl.multiple_of(base + ci * SC_CHUNK, SC_CHUNK), SC_CHUNK)
            pltpu.sync_copy(pa_hbm.at[tok], ia_v)
            pltpu.sync_copy(pb_hbm.at[tok], ib_v)
            pltpu.sync_copy(h_hbm.at[tok], rows_v)
            pltpu.sync_copy(rows_v, xs_hbm.at[ia_v])
            pltpu.sync_copy(rows_v, xs_hbm.at[ib_v])

    return run(h, pos_a, pos_b)


def _moe_collect(ys, pos_a, pos_b):
    n_chunks = N_TOK // SC_WORKERS // SC_CHUNK
    out = jax.ShapeDtypeStruct((N_TOK, ROW_WORDS), jnp.int32)

    @functools.partial(
        pl.kernel, mesh=_sc_mesh(), out_type=(out, out),
        scratch_types=[pltpu.VMEM((SC_CHUNK,), jnp.int32), pltpu.VMEM((SC_CHUNK,), jnp.int32),
                       pltpu.VMEM((SC_CHUNK, ROW_WORDS), jnp.int32)],
        name="moe_collect",
    )
    def run(ys_hbm, pa_hbm, pb_hbm, ya_hbm, yb_hbm, ia_v, ib_v, rows_v):
        base = _sc_worker_base()

        @pl.loop(0, n_chunks)
        def _(ci):
            tok = pl.ds(pl.multiple_of(base + ci * SC_CHUNK, SC_CHUNK), SC_CHUNK)
            pltpu.sync_copy(pa_hbm.at[tok], ia_v)
            pltpu.sync_copy(pb_hbm.at[tok], ib_v)
            pltpu.sync_copy(ys_hbm.at[ia_v], rows_v)
            pltpu.sync_copy(rows_v, ya_hbm.at[tok])
            pltpu.sync_copy(ys_hbm.at[ib_v], rows_v)
            pltpu.sync_copy(rows_v, yb_hbm.at[tok])

    return run(ys, pos_a, pos_b)


def _experts_kernel(plan_ref, xs_ref, wg_ref, wu_ref, wd_ref, y_ref, wgb_ref, wub_ref, wdb_ref):
    j = pl.program_id(0)
    expert = plan_ref[j]
    n_real = plan_ref[128 + j]
    fresh = jnp.logical_or(j == 0, expert != plan_ref[jnp.maximum(j - 1, 0)])

    @pl.when(jnp.logical_and(n_real > 0, fresh))
    def _():
        wgb_ref[...] = wg_ref[...].astype(BF16)
        wub_ref[...] = wu_ref[...].astype(BF16)
        wdb_ref[...] = wd_ref[...].astype(BF16)

    @pl.when(n_real > 0)
    def _():
        row = lax.broadcasted_iota(jnp.int32, xs_ref.shape, 0)
        words = jnp.where(row < n_real, xs_ref[...], 0)
        x = _unpack_rows(words).astype(BF16)
        a = jnp.dot(x, wgb_ref[...], preferred_element_type=F32)
        b = jnp.dot(x, wub_ref[...], preferred_element_type=F32)
        hid = (a * jax.nn.sigmoid(a)) * b
        y_ref[...] = _pack_rows(jnp.dot(hid.astype(BF16), wdb_ref[...], preferred_element_type=F32))


def _experts(plan, xs, w_gate, w_up, w_down, layer):
    wspec = lambda r, c: pl.BlockSpec((None, None, r, c), lambda j, plan: (layer, plan[j], 0, 0))
    return pl.pallas_call(
        _experts_kernel,
        grid_spec=pltpu.PrefetchScalarGridSpec(
            num_scalar_prefetch=1,
            grid=(MOE_ROWS // MOE_TILE,),
            in_specs=[pl.BlockSpec((MOE_TILE, ROW_WORDS), lambda j, plan: (j, 0)),
                      wspec(D_MODEL, D_EXPERT), wspec(D_MODEL, D_EXPERT), wspec(D_EXPERT, D_MODEL)],
            out_specs=pl.BlockSpec((MOE_TILE, ROW_WORDS), lambda j, plan: (j, 0)),
            scratch_shapes=[pltpu.VMEM((D_MODEL, D_EXPERT), BF16), pltpu.VMEM((D_MODEL, D_EXPERT), BF16),
                            pltpu.VMEM((D_EXPERT, D_MODEL), BF16)],
        ),
        out_shape=jax.ShapeDtypeStruct((MOE_ROWS, ROW_WORDS), jnp.int32),
        compiler_params=_params("arbitrary"),
        name="experts",
    )(plan, xs, w_gate, w_up, w_down)


def _combine_kernel(x_ref, ya_ref, yb_ref, wt_ref, mod_ref, o_ref):
    o_ref[...] = _moe_mix(x_ref, ya_ref, yb_ref, wt_ref, mod_ref)


def _combine(x, moe_out, mod_l, tok0, n_tok, block_rows=512):
    ya, yb, w_tok = moe_out
    b0 = tok0 // block_rows
    rows = lambda width: pl.BlockSpec((block_rows, width), lambda i: (b0 + i, 0))
    return pl.pallas_call(
        _combine_kernel,
        grid=(n_tok // block_rows,),
        in_specs=[rows(D_MODEL), rows(ROW_WORDS), rows(ROW_WORDS), rows(TOP_K),
                  pl.BlockSpec((None, 6, D_MODEL), lambda i: (_cond_of_token_block(b0 + i, block_rows), 0, 0))],
        out_specs=pl.BlockSpec((block_rows, D_MODEL), lambda i: (i, 0)),
        out_shape=jax.ShapeDtypeStruct((n_tok, D_MODEL), F32),
        compiler_params=_params("arbitrary"),
        name="combine",
    )(x, ya, yb, w_tok, mod_l)


def _moe(h, logits_t, router_b, w_gate, w_up, w_down, layer):
    pos, w, plan = _router(logits_t, router_b)
    xs = _moe_dispatch(h, pos[0], pos[1])
    ys = _experts(plan.reshape(-1), xs, w_gate, w_up, w_down, layer)
    ya, yb = _moe_collect(ys, pos[0], pos[1])
    return ya, yb, w.T


def _dft_tables(L):
    k = np.arange(L)[:, None]
    m = np.arange(L)[None, :]
    r = (k * m) % (2 * L)
    ang = np.pi * r.astype(np.float64) / L
    fc = np.cos(ang)
    fs = np.sin(ang)
    fs[0, :] = np.where(np.arange(L) % 2 == 0, 1.0, -1.0)
    wk = np.full((L, 1), 1.0 / L)
    wk[0, 0] = 0.5 / L
    gc = (fc * wk).T
    gs = (fs * wk).T
    return [jnp.asarray(t.astype(np.float32)).astype(BF16) for t in (fc, fs, gc, gs)]


def _filter_consts(L):
    t = np.linspace(0.0, 1.0, L, dtype=np.float32)[:, None]
    w = (np.float32(2.0 * np.pi) * np.arange(L, dtype=np.float32)[:, None] / np.float32(L)).astype(np.float32)
    fb = np.linspace(1e-4, HY_BANDS - 1, HY_BANDS, dtype=np.float32)[None, :]
    emb = np.concatenate([t, np.cos(fb * w), -np.sin(fb * w)], axis=-1).astype(np.float32)
    lo = math.log(HY_DECAY_TARGET) / HY_SLOW_PCT
    hi = math.log(HY_DECAY_TARGET) / HY_FAST_PCT
    deltas = np.abs(np.linspace(lo, hi, D_MODEL, dtype=np.float32))
    decay = np.exp(-t * deltas).astype(np.float32)
    return jnp.asarray(emb), jnp.asarray(decay)


def _filter_kernel(emb_ref, w1_ref, b1_ref, w2_ref, b2_ref, fr_ref, w3f_ref, w3b_ref, dec_ref,
                   fc_ref, fs_ref, kr_ref, q_ref, krn_ref):
    fr = fr_ref[...]
    hd = jnp.sin(fr * (jnp.dot(emb_ref[...], w1_ref[...], precision=HIGHEST,
                               preferred_element_type=F32) + b1_ref[...]))
    hd = jnp.sin(fr * (jnp.dot(hd, w2_ref[...], precision=HIGHEST,
                               preferred_element_type=F32) + b2_ref[...]))
    dec = dec_ref[...]
    f = jnp.dot(hd, w3f_ref[...], precision=HIGHEST, preferred_element_type=F32) * dec
    g = jnp.dot(hd, w3b_ref[...], precision=HIGHEST, preferred_element_type=F32) * dec
    row = lax.broadcasted_iota(jnp.int32, f.shape, 0)
    g = jnp.where(row == 0, 0.0, g)
    s = f + g
    d = f - g
    kr = jnp.dot(fc_ref[...], s.astype(BF16), preferred_element_type=F32)
    qq = jnp.dot(fs_ref[...], d.astype(BF16), preferred_element_type=F32)
    alt = jnp.where(row % 2 == 0, 1.0, -1.0)
    nyq = jnp.sum(alt * s, axis=0, keepdims=True)
    kr_ref[...] = kr
    q_ref[...] = jnp.where(row == 0, 0.0, qq)
    krn_ref[...] = jnp.where(row == 0, nyq, kr)


def _hyena_filter_spectrum(L, w1, b1, w2, b2, w3, freq, fc, fs, cblk=256):
    emb, decay = _filter_consts(L)
    ncb = D_MODEL // cblk
    n_emb = 128
    emb = jnp.pad(emb, ((0, 0), (0, n_emb - emb.shape[1])))
    w1 = jnp.pad(w1, ((0, n_emb - w1.shape[0]), (0, 0)))
    full = lambda shape: pl.BlockSpec(shape, lambda j: tuple(0 for _ in shape))
    out_sds = jax.ShapeDtypeStruct((L, D_MODEL), F32)
    out_spec = pl.BlockSpec((L, cblk), lambda j: (0, j))
    return pl.pallas_call(
        _filter_kernel,
        grid=(ncb,),
        in_specs=[
            full((L, n_emb)), full((n_emb, HY_FFN)), full((1, HY_FFN)), full((HY_FFN, HY_FFN)),
            full((1, HY_FFN)), full((1, HY_FFN)),
            pl.BlockSpec((HY_FFN, cblk), lambda j: (0, j)),
            pl.BlockSpec((HY_FFN, cblk), lambda j: (0, ncb + j)),
            pl.BlockSpec((L, cblk), lambda j: (0, j)),
            full((L, L)), full((L, L)),
        ],
        out_specs=(out_spec, out_spec, out_spec),
        out_shape=(out_sds, out_sds, out_sds),
        compiler_params=_params("arbitrary"),
        name=f"hyena_filter_{L}",
    )(emb, w1, b1.reshape(1, HY_FFN), w2, b2.reshape(1, HY_FFN), freq.reshape(1, HY_FFN), w3, w3, decay, fc, fs)


def _hyena_conv_kernel(x0_ref, x1_ref, v_ref, cw0_ref, cw1_ref, cwv_ref, cb0_ref, cb1_ref, cbv_ref,
                       kr_ref, q_ref, krn_ref, ds_ref, fc_ref, fs_ref, gc_ref, gs_ref, *rest):
    o_ref = rest[-1]
    L = x0_ref.shape[0]
    row = lax.broadcasted_iota(jnp.int32, x0_ref.shape, 0)

    def short_conv(u_ref, w_ref, b_ref):
        u = u_ref[...].astype(F32)
        w = w_ref[...]
        prev = jnp.where(row == 0, 0.0, pltpu.roll(u, 1, axis=0))
        nxt = jnp.where(row == L - 1, 0.0, pltpu.roll(u, L - 1, axis=0))
        return prev * w[0:1, :] + u * w[1:2, :] + nxt * w[2:3, :] + b_ref[...]

    x0 = short_conv(x0_ref, cw0_ref, cb0_ref)
    x1 = short_conv(x1_ref, cw1_ref, cb1_ref)
    v = short_conv(v_ref, cwv_ref, cbv_ref)
    zz = v * x1
    zb = zz.astype(BF16)
    ur = jnp.dot(fc_ref[...], zb, preferred_element_type=F32)
    p = jnp.dot(fs_ref[...], zb, preferred_element_type=F32)
    qq = q_ref[...]
    yr = ur * kr_ref[...] - p * qq
    yw = ur * qq + p * krn_ref[...]
    y = jnp.dot(gc_ref[...], yr.astype(BF16), preferred_element_type=F32)
    y = y + jnp.dot(gs_ref[...], yw.astype(BF16), preferred_element_type=F32)
    o_ref[...] = (x0 * (y + zz * ds_ref[...])).astype(o_ref.dtype)


def _hyena_conv(u, conv_w, conv_b, dskip, spectrum, tables, shared, *, latent):
    L = LATENT_LEN if latent else PROMPT_LEN
    n_seq = N_LATENT_SEQ if latent else N_PROMPT_SEQ
    cblk = 256 if latent else 512
    ncb = D_MODEL // cblk
    row0 = (N_PROMPT_TOK // L) if latent else 0
    kr, qq, krn = spectrum
    fc, fs, gc, gs = tables

    def part(p, rows):
        return pl.BlockSpec((rows, cblk), lambda j, s: (0 if rows != L else row0 + s, p * ncb + j))

    def const_cols(rows):
        return pl.BlockSpec((rows, cblk), lambda j, s: (0, j))

    mat = pl.BlockSpec((L, L), lambda j, s: (0, 0))
    conv_b2 = conv_b.reshape(1, 3 * D_MODEL)
    in_specs = [part(0, L), part(1, L), part(2, L),
                part(0, 3), part(1, 3), part(2, 3),
                part(0, 1), part(1, 1), part(2, 1),
                const_cols(L), const_cols(L), const_cols(L), const_cols(1),
                mat, mat, mat, mat]
    args = [u, u, u, conv_w, conv_w, conv_w, conv_b2, conv_b2, conv_b2,
            kr, qq, krn, dskip.reshape(1, D_MODEL), fc, fs, gc, gs]
    aliases = {}
    if latent:
        in_specs.append(pl.BlockSpec(memory_space=pl.ANY))
        args.append(shared)
        aliases = {len(args) - 1: 0}
    return pl.pallas_call(
        _hyena_conv_kernel,
        grid=(ncb, n_seq),
        in_specs=in_specs,
        out_specs=pl.BlockSpec((L, cblk), lambda j, s: (row0 + s, j)),
        out_shape=jax.ShapeDtypeStruct((N_TOK, D_MODEL), BF16),
        input_output_aliases=aliases,
        compiler_params=_params("arbitrary", "arbitrary"),
        name="hyena_conv_latent" if latent else "hyena_conv_prompt",
    )(*args)


def kernel(x_prompt, x_sample, cache_k, cache_v, state_hgrn, c, c_ctx, norm_g, mod_w, mod_b, ab_in_w, hgrn_lb, hgrn_onorm_g, attn_qnorm_g, attn_knorm_g, ab_out_w, hy_in_w, hy_in_b, hy_conv_w, hy_conv_b, hy_f_w1, hy_f_b1, hy_f_w2, hy_f_b2, hy_f_w3, hy_f_freq, hy_dskip, hy_out_w, router_w, router_b, moe_w_gate, moe_w_up, moe_w_down):
    xp = x_prompt.reshape(N_PROMPT_TOK, D_MODEL)
    xl = x_sample.reshape(N_LATENT_TOK, D_MODEL)
    cond = jnp.concatenate([c_ctx[None, :], c, jnp.zeros((N_COND - 1 - N_LATENT_SEQ, D_MODEL), F32)], axis=0)
    mod = _modulation(cond, mod_w, mod_b)
    router_wp = jnp.pad(router_w, ((0, 0), (0, ROUTER_LANES - N_EXPERTS)))

    z = _in_proj0(xp, xl, norm_g[0, 0], mod[0], ab_in_w[0])
    o_a, new_state = _hgrn(z, hgrn_lb, hgrn_onorm_g[0], None, None, latent=False)
    o_a = _hgrn(z, hgrn_lb, hgrn_onorm_g[0], state_hgrn, o_a, latent=True)
    o_b, k_prompt = _attention_prompt(z, attn_qnorm_g[0], attn_knorm_g[0])
    o_b = _attention_latent(z, attn_qnorm_g[0], attn_knorm_g[0], cache_k, cache_v, o_b)
    x, h, logits_t = _out_proj([o_a, o_b], ab_out_w[0], (xp, xl), norm_g[0, 1], mod[0], router_wp)
    moe_out = _moe(h, logits_t, router_b, moe_w_gate, moe_w_up, moe_w_down, 0)

    x, u = _in_proj1(x, moe_out, mod[0], norm_g[1, 0], mod[1], hy_in_w[0], hy_in_b[0])
    pre = None
    for latent in (False, True):
        L = LATENT_LEN if latent else PROMPT_LEN
        tables = _dft_tables(L)
        spectrum = _hyena_filter_spectrum(L, hy_f_w1[0], hy_f_b1[0], hy_f_w2[0], hy_f_b2[0], hy_f_w3[0],
                                          hy_f_freq[0], tables[0], tables[1])
        pre = _hyena_conv(u, hy_conv_w[0], hy_conv_b[0], hy_dskip[0], spectrum, tables, pre, latent=latent)
    x, h, logits_t = _out_proj([pre], hy_out_w[0], (x,), norm_g[1, 1], mod[1], router_wp)
    moe_out = _moe(h, logits_t, router_b, moe_w_gate, moe_w_up, moe_w_down, 1)

    y_prompt = _combine(x, moe_out, mod[1], 0, N_PROMPT_TOK).reshape(N_PROMPT_SEQ, PROMPT_LEN, D_MODEL)
    y_sample = _combine(x, moe_out, mod[1], N_PROMPT_TOK, N_LATENT_TOK).reshape(N_LATENT_SEQ, LATENT_LEN, D_MODEL)
    kv_shape = (N_PROMPT_SEQ, PROMPT_LEN, KV_HEADS, HEAD_DIM)
    new_k = k_prompt.reshape(kv_shape).transpose(0, 2, 1, 3)[:, None]
    v_col = 5 * A_WIDTH + (Q_HEADS + KV_HEADS) * HEAD_DIM
    new_v = z[:N_PROMPT_TOK, v_col:].astype(F32).reshape(kv_shape).transpose(0, 2, 1, 3)[:, None]
    return (y_prompt, y_sample, new_k, new_v, new_state)
```

```python
import functools
import math

import numpy as np
import jax
import jax.numpy as jnp
from jax import lax
from jax.experimental import pallas as pl
from jax.experimental.pallas import tpu as pltpu
from jax.experimental.pallas import tpu_sc as plsc

F32 = jnp.float32
BF16 = jnp.bfloat16
HIGHEST = lax.Precision.HIGHEST

D_MODEL = 1024
N_PROMPT_SEQ = 32
PROMPT_LEN = 256
N_LATENT_SEQ = 2
LATENT_LEN = 1024
PAST_LEN = 512
GRID_W = 64
N_PROMPT_TOK = N_PROMPT_SEQ * PROMPT_LEN
N_LATENT_TOK = N_LATENT_SEQ * LATENT_LEN
N_TOK = N_PROMPT_TOK + N_LATENT_TOK
N_COND = 8
EPS = 1e-6

A_WIDTH = 512
A_HEADS = 4
A_DK = 128
CHUNK = 64
HGRN_BLOCK = 256
HGRN_HEADS_PER_STEP = 4
HEAD_DIM = 64
Q_HEADS = 8
KV_HEADS = 2
Q_PER_KV = Q_HEADS // KV_HEADS
Q_BLOCK = 256
ROPE_THETA = 10000.0
ROPE_PAIRS = HEAD_DIM // 4
AB_IN = 5 * A_WIDTH + (Q_HEADS + 2 * KV_HEADS) * HEAD_DIM

HY_BANDS = 16
HY_FFN = 64
HY_DECAY_TARGET = 1e-2
HY_FAST_PCT = 0.3
HY_SLOW_PCT = 1.5

N_EXPERTS = 16
N_GROUPS = 4
EXPERTS_PER_GROUP = 4
TOP_K = 2
D_EXPERT = 512
ROUTER_LANES = 128
MOE_TILE = 512
MOE_ROWS = N_TOK * TOP_K + N_EXPERTS * MOE_TILE

SC_CORES = 2
SC_WORKERS = 32
SC_CHUNK = 80
ROW_WORDS = D_MODEL // 2

VMEM_LIMIT = 56 * 1024 * 1024


def _params(*sem):
    return pltpu.CompilerParams(dimension_semantics=sem, vmem_limit_bytes=VMEM_LIMIT)


def _pack_rows(x):
    n = x.shape[1] // 2
    bits = pltpu.bitcast(x.astype(BF16).astype(F32), jnp.uint32)
    return pltpu.bitcast(bits[:, :n] | (bits[:, n:] >> 16), jnp.int32)


def _unpack_rows(p):
    bits = pltpu.bitcast(p, jnp.uint32)
    hi = pltpu.bitcast(bits & jnp.uint32(0xFFFF0000), F32)
    lo = pltpu.bitcast(bits << 16, F32)
    return jnp.concatenate([hi, lo], axis=1)


def _cond_of_token_block(i, block_rows):
    start = i * block_rows
    return jnp.where(start < N_PROMPT_TOK, 0, 1 + (start - N_PROMPT_TOK) // LATENT_LEN)


def _mod_kernel(cond_ref, w_ref, b_ref, o_ref):
    cnd = cond_ref[...]
    s = cnd * jax.nn.sigmoid(cnd)
    s_hi = s.astype(BF16)
    s_lo = (s - s_hi.astype(F32)).astype(BF16)
    w = w_ref[...]
    w_hi = w.astype(BF16)
    w_lo = (w - w_hi.astype(F32)).astype(BF16)
    acc = jnp.dot(s_hi, w_hi, preferred_element_type=F32)
    acc = acc + jnp.dot(s_lo, w_hi, preferred_element_type=F32)
    acc = acc + jnp.dot(s_hi, w_lo, preferred_element_type=F32)
    o_ref[...] = acc + b_ref[...]


def _modulation(cond, mod_w, mod_b):
    depth = mod_w.shape[0]
    n_chunk = 6
    out = pl.pallas_call(
        _mod_kernel,
        grid=(depth, n_chunk),
        in_specs=[
            pl.BlockSpec((N_COND, D_MODEL), lambda l, j: (0, 0)),
            pl.BlockSpec((None, D_MODEL, D_MODEL), lambda l, j: (l, 0, j)),
            pl.BlockSpec((None, 1, D_MODEL), lambda l, j: (l, 0, j)),
        ],
        out_specs=pl.BlockSpec((None, N_COND, D_MODEL), lambda l, j: (l, 0, j)),
        out_shape=jax.ShapeDtypeStruct((depth, N_COND, n_chunk * D_MODEL), F32),
        compiler_params=_params("arbitrary", "arbitrary"),
        name="modulation",
    )(cond, mod_w, mod_b.reshape(depth, 1, n_chunk * D_MODEL))
    return out.reshape(depth, N_COND, n_chunk, D_MODEL)


def _modulated_norm(x, g, mod, shift_row, scale_row):
    ms = jnp.mean(x * x, axis=-1, keepdims=True)
    y = x * lax.rsqrt(ms + EPS) * g
    return y * (1.0 + mod[scale_row:scale_row + 1, :]) + mod[shift_row:shift_row + 1, :]


def _trunk_specs(block_rows, width):
    n_prompt_blocks = N_PROMPT_TOK // block_rows
    return (pl.BlockSpec((block_rows, width), lambda i: (jnp.minimum(i, n_prompt_blocks - 1), 0)),
            pl.BlockSpec((block_rows, width), lambda i: (jnp.maximum(i - n_prompt_blocks, 0), 0)))


def _select_trunk(p_ref, l_ref):
    block_rows = p_ref.shape[0]
    return jnp.where(pl.program_id(0) < N_PROMPT_TOK // block_rows, p_ref[...], l_ref[...])


def _cast_once(w_ref, wb_ref):
    @pl.when(pl.program_id(0) == 0)
    def _():
        wb_ref[...] = w_ref[...].astype(BF16)


def _resident(shape):
    return pl.BlockSpec(shape, lambda i: tuple(0 for _ in shape), pipeline_mode=pl.Buffered(1))


def _mod_spec(block_rows):
    return pl.BlockSpec((None, 6, D_MODEL), lambda i: (_cond_of_token_block(i, block_rows), 0, 0))


def _in_proj0_kernel(xp_ref, xl_ref, g_ref, mod_ref, w_ref, o_ref, wb_ref):
    _cast_once(w_ref, wb_ref)
    h = _modulated_norm(_select_trunk(xp_ref, xl_ref), g_ref[...], mod_ref[...], 0, 1)
    o_ref[...] = jnp.dot(h.astype(BF16), wb_ref[...], preferred_element_type=F32).astype(o_ref.dtype)


def _in_proj0(x_prompt, x_latent, g, mod_l, w, block_rows=512):
    n = w.shape[1]
    return pl.pallas_call(
        _in_proj0_kernel,
        grid=(N_TOK // block_rows,),
        in_specs=[*_trunk_specs(block_rows, D_MODEL), _resident((1, D_MODEL)), _mod_spec(block_rows),
                  _resident((D_MODEL, n))],
        out_specs=pl.BlockSpec((block_rows, n), lambda i: (i, 0)),
        out_shape=jax.ShapeDtypeStruct((N_TOK, n), BF16),
        scratch_shapes=[pltpu.VMEM((D_MODEL, n), BF16)],
        compiler_params=_params("arbitrary"),
        name="in_proj0",
    )(x_prompt, x_latent, g.reshape(1, D_MODEL), mod_l, w)


def _moe_mix(x_ref, ya_ref, yb_ref, wt_ref, mod_ref):
    wt = wt_ref[...]
    mix = wt[:, 0:1] * _unpack_rows(ya_ref[...]) + wt[:, 1:2] * _unpack_rows(yb_ref[...])
    return x_ref[...] + mod_ref[5:6, :] * mix


def _in_proj1_kernel(x_ref, ya_ref, yb_ref, wt_ref, modp_ref, g_ref, mod_ref, w_ref, b_ref, xo_ref, o_ref, wb_ref):
    _cast_once(w_ref, wb_ref)
    x = _moe_mix(x_ref, ya_ref, yb_ref, wt_ref, modp_ref)
    xo_ref[...] = x
    h = _modulated_norm(x, g_ref[...], mod_ref[...], 0, 1)
    u = jnp.dot(h.astype(BF16), wb_ref[...], preferred_element_type=F32) + b_ref[...]
    o_ref[...] = u.astype(o_ref.dtype)


def _in_proj1(x, moe_out, mod_prev, g, mod_l, w, bias, block_rows=512):
    ya, yb, w_tok = moe_out
    n = w.shape[1]
    tok = pl.BlockSpec((block_rows, D_MODEL), lambda i: (i, 0))
    packed = pl.BlockSpec((block_rows, ROW_WORDS), lambda i: (i, 0))
    return pl.pallas_call(
        _in_proj1_kernel,
        grid=(N_TOK // block_rows,),
        in_specs=[tok, packed, packed, pl.BlockSpec((block_rows, TOP_K), lambda i: (i, 0)), _mod_spec(block_rows),
                  _resident((1, D_MODEL)), _mod_spec(block_rows), _resident((D_MODEL, n)), _resident((1, n))],
        out_specs=(tok, pl.BlockSpec((block_rows, n), lambda i: (i, 0))),
        out_shape=(jax.ShapeDtypeStruct((N_TOK, D_MODEL), F32), jax.ShapeDtypeStruct((N_TOK, n), BF16)),
        scratch_shapes=[pltpu.VMEM((D_MODEL, n), BF16)],
        compiler_params=_params("arbitrary"),
        name="in_proj1",
    )(x, ya, yb, w_tok, mod_prev, g.reshape(1, D_MODEL), mod_l, w, bias.reshape(1, n))


def _hgrn_kernel(*refs, seq_len, with_state):
    if with_state:
        (q_ref, zf_ref, zb_ref, i_ref, ga_ref, lb_ref, og_ref, s0_ref, shared_ref, o_ref, of_ref, ob_ref) = refs
    else:
        (q_ref, zf_ref, zb_ref, i_ref, ga_ref, lb_ref, og_ref, o_ref, s_ref, of_ref, ob_ref) = refs
    n_blocks = seq_len // HGRN_BLOCK
    chunks_per_block = HGRN_BLOCK // CHUNK

    lbr = lb_ref[...]
    mx = jnp.maximum(lbr[0], lbr[1])
    e0 = jnp.exp(lbr[0] - mx)
    e1 = jnp.exp(lbr[1] - mx)
    lb = e0 / (e0 + e1)

    row = lax.broadcasted_iota(jnp.int32, (HGRN_BLOCK, HGRN_BLOCK), 0)
    col = lax.broadcasted_iota(jnp.int32, (HGRN_BLOCK, HGRN_BLOCK), 1)
    same_chunk = (row // CHUNK) == (col // CHUNK)
    nt = (((1,), (1,)), ((), ()))
    tn = (((0,), (0,)), ((), ()))

    def per_chunk_row(x, idx):
        return jnp.concatenate(
            [jnp.broadcast_to(x[n * CHUNK + idx:n * CHUNK + idx + 1, :], (CHUNK, x.shape[1]))
             for n in range(chunks_per_block)], axis=0)

    def in_chunk_cumsum(tri, x):
        hi = x.astype(BF16)
        lo = (x - hi.astype(F32)).astype(BF16)
        return jnp.dot(tri, hi, preferred_element_type=F32) + jnp.dot(tri, lo, preferred_element_type=F32)

    def block(blk, cols, st, z_ref, lbd, forward, out_ref):
        rows = slice(blk * HGRN_BLOCK, (blk + 1) * HGRN_BLOCK)
        keep = (same_chunk & (col <= row)) if forward else (same_chunk & (col >= row))
        tri = jnp.where(keep, 1.0, 0.0).astype(BF16)
        mid = CHUNK // 2 if forward else CHUNK - 1 - CHUNK // 2
        last = CHUNK - 1 if forward else 0
        f = lbd + (1.0 - lbd) * jax.nn.sigmoid(z_ref[rows, cols].astype(F32))
        lf = jnp.log(f)
        k = 1.0 - f
        q = q_ref[rows, cols].astype(F32)
        vb = i_ref[rows, cols].astype(BF16)
        b = in_chunk_cumsum(tri, lf)
        bm = per_chunk_row(b, mid)
        bl = per_chunk_row(b, last)
        qe = (q * jnp.exp(b - bm)).astype(BF16)
        ke = (k * jnp.exp(bm - b)).astype(BF16)
        att = lax.dot_general(qe, ke, nt, preferred_element_type=F32)
        att = jnp.where(keep, att, 0.0)
        o_intra = jnp.dot(att.astype(BF16), vb, preferred_element_type=F32)
        qb = (q * jnp.exp(b)).astype(BF16)
        ks = (k * jnp.exp(bl - b)).astype(BF16)
        decay = jnp.exp(bl)
        order = range(chunks_per_block) if forward else range(chunks_per_block - 1, -1, -1)
        o_inter = [None] * chunks_per_block
        for n in order:
            cr = slice(n * CHUNK, (n + 1) * CHUNK)
            o_inter[n] = lax.dot_general(qb[cr], st.astype(BF16), nt, preferred_element_type=F32)
            upd = lax.dot_general(vb[cr], ks[cr], tn, preferred_element_type=F32)
            st = st * decay[n * CHUNK:n * CHUNK + 1, :] + upd
        out_ref[rows, cols] = o_intra + jnp.concatenate(o_inter, axis=0)
        return st

    for hd in range(q_ref.shape[1] // A_DK):
        cols = slice(hd * A_DK, (hd + 1) * A_DK)
        if with_state:
            st_f, st_b = s0_ref[0, hd].T, s0_ref[1, hd].T
        else:
            st_f, st_b = jnp.zeros((A_DK, A_DK), F32), jnp.zeros((A_DK, A_DK), F32)
        for step in range(n_blocks):
            st_f = block(step, cols, st_f, zf_ref, lb[0:1, cols], True, of_ref)
            st_b = block(n_blocks - 1 - step, cols, st_b, zb_ref, lb[1:2, cols], False, ob_ref)
        if not with_state:
            s_ref[0, hd] = st_f.T
            s_ref[1, hd] = st_b.T
        o = of_ref[:, cols] + ob_ref[:, cols]
        o = o * lax.rsqrt(jnp.mean(o * o, axis=-1, keepdims=True) + EPS) * og_ref[:, cols]
        ga = ga_ref[:, cols].astype(F32)
        o_ref[:, cols] = (o * (ga * jax.nn.sigmoid(ga))).astype(o_ref.dtype)


def _hgrn(z, hgrn_lb, onorm_g, state, shared, *, latent):
    seq_len = LATENT_LEN if latent else PROMPT_LEN
    n_seq = N_LATENT_SEQ if latent else N_PROMPT_SEQ
    row0 = (N_PROMPT_TOK // seq_len) if latent else 0

    hw = HGRN_HEADS_PER_STEP * A_DK
    n_hg = A_HEADS // HGRN_HEADS_PER_STEP

    def zspec(part):
        return pl.BlockSpec((seq_len, hw), lambda s, h: (row0 + s, part * n_hg + h))

    in_specs = [zspec(0), zspec(1), zspec(2), zspec(3), zspec(4),
                pl.BlockSpec((2, 2, hw), lambda s, h: (0, 0, h)),
                pl.BlockSpec((1, hw), lambda s, h: (0, h))]
    args = [z, z, z, z, z, hgrn_lb, onorm_g.reshape(1, A_WIDTH)]
    state_spec = pl.BlockSpec((None, None, 2, HGRN_HEADS_PER_STEP, A_DK, A_DK), lambda s, h: (s, 0, 0, h, 0, 0))
    o_shape = jax.ShapeDtypeStruct((N_TOK, A_WIDTH), BF16)
    o_spec = pl.BlockSpec((seq_len, hw), lambda s, h: (row0 + s, h))
    aliases = {}
    if latent:
        in_specs += [state_spec, pl.BlockSpec(memory_space=pl.ANY)]
        args += [state, shared]
        aliases = {len(args) - 1: 0}
        out_shape, out_specs = o_shape, o_spec
    else:
        out_shape = (o_shape, jax.ShapeDtypeStruct((n_seq, 1, 2, A_HEADS, A_DK, A_DK), F32))
        out_specs = (o_spec, state_spec)
    return pl.pallas_call(
        functools.partial(_hgrn_kernel, seq_len=seq_len, with_state=latent),
        grid=(n_seq, n_hg),
        in_specs=in_specs,
        out_specs=out_specs,
        out_shape=out_shape,
        input_output_aliases=aliases,
        scratch_shapes=[pltpu.VMEM((seq_len, hw), F32), pltpu.VMEM((seq_len, hw), F32)],
        compiler_params=_params("arbitrary", "arbitrary"),
        name="hgrn_latent" if latent else "hgrn_prompt",
    )(*args)


def _rope_tables():
    pos = np.arange(LATENT_LEN)
    row, colp = pos // GRID_W, pos % GRID_W
    inv = ROPE_THETA ** (-np.arange(ROPE_PAIRS, dtype=np.float32) / ROPE_PAIRS)
    inv = inv.astype(np.float32)
    ang_r = (row.astype(np.float32)[:, None] * inv).astype(np.float32)
    ang_c = (colp.astype(np.float32)[:, None] * inv).astype(np.float32)
    cos = np.concatenate([np.cos(ang_r), np.cos(ang_r), np.cos(ang_c), np.cos(ang_c)], axis=1)
    sin = np.concatenate([-np.sin(ang_r), np.sin(ang_r), -np.sin(ang_c), np.sin(ang_c)], axis=1)
    return cos.astype(np.float32), sin.astype(np.float32)


def _head_mean_matrix(width):
    idx = np.arange(width) // HEAD_DIM
    return jnp.asarray((idx[:, None] == idx[None, :]).astype(np.float32) / HEAD_DIM).astype(BF16)


def _attn_kernel(*refs, latent):
    if latent:
        (q_ref, k_ref, v_ref, qg_ref, kg_ref, gq_ref, gk_ref, cosq_ref, sinq_ref, cosk_ref, sink_ref,
         ck_ref, cv_ref, shared_ref, o_ref) = refs
    else:
        (q_ref, k_ref, v_ref, qg_ref, kg_ref, gq_ref, gk_ref, o_ref, kout_ref) = refs
    pair_w = 2 * HEAD_DIM

    def head_norm(x, mean_ref, gain):
        sq = x * x
        hi = sq.astype(BF16)
        lo = (sq - hi.astype(F32)).astype(BF16)
        ms = jnp.dot(hi, mean_ref[...], preferred_element_type=F32)
        ms = ms + jnp.dot(lo, mean_ref[...], preferred_element_type=F32)
        return x * lax.rsqrt(ms + EPS) * gain

    def rope(x, cos, sin):
        n = x.shape[1]
        lane = lax.broadcasted_iota(jnp.int32, x.shape, 1)
        first_of_pair = (lane // ROPE_PAIRS) % 2 == 0
        swapped = jnp.where(first_of_pair, pltpu.roll(x, n - ROPE_PAIRS, axis=1), pltpu.roll(x, ROPE_PAIRS, axis=1))
        return x * cos + swapped * sin

    q = head_norm(q_ref[...].astype(F32), gq_ref, qg_ref[...])
    k = head_norm(k_ref[...].astype(F32), gk_ref, kg_ref[...])
    if latent:
        q = rope(q, cosq_ref[...], sinq_ref[...])
        k = rope(k, cosk_ref[...], sink_ref[...])
    else:
        kout_ref[...] = k
    q = q * (HEAD_DIM ** -0.5)
    v = v_ref[...].astype(F32)
    n_q = q.shape[0]
    low_kv = lax.broadcasted_iota(jnp.int32, k.shape, 1) < HEAD_DIM
    low_q = lax.broadcasted_iota(jnp.int32, (n_q, pair_w), 1) < HEAD_DIM
    k_swapped = pltpu.roll(k, HEAD_DIM, axis=1)
    v_swapped = pltpu.roll(v, HEAD_DIM, axis=1)
    nt = (((1,), (1,)), ((), ()))
    for j in range(KV_HEADS):
        kd = (jnp.where(low_kv, k, k_swapped) if j == 0 else jnp.where(low_kv, k_swapped, k)).astype(BF16)
        vd = (jnp.where(low_kv, v, v_swapped) if j == 0 else jnp.where(low_kv, v_swapped, v)).astype(BF16)
        tiles = range(j * Q_PER_KV // 2, (j + 1) * Q_PER_KV // 2)
        parts = []
        for t in tiles:
            qt = q[:, t * pair_w:(t + 1) * pair_w]
            parts += [jnp.where(low_q, qt, 0.0), jnp.where(low_q, 0.0, qt)]
        qs = jnp.concatenate(parts, axis=0).astype(BF16)
        s_new = lax.dot_general(qs, kd, nt, preferred_element_type=F32)
        m = jnp.max(s_new, axis=-1, keepdims=True)
        if latent:
            ckd = jnp.concatenate([ck_ref[j], ck_ref[j]], axis=1).astype(BF16)
            cvd = jnp.concatenate([cv_ref[j], cv_ref[j]], axis=1).astype(BF16)
            s_old = lax.dot_general(qs, ckd, nt, preferred_element_type=F32)
            m = jnp.maximum(m, jnp.max(s_old, axis=-1, keepdims=True))
        p_new = jnp.exp(s_new - m)
        den = jnp.sum(p_new, axis=-1, keepdims=True)
        acc = jnp.dot(p_new.astype(BF16), vd, preferred_element_type=F32)
        if latent:
            p_old = jnp.exp(s_old - m)
            den = den + jnp.sum(p_old, axis=-1, keepdims=True)
            acc = acc + jnp.dot(p_old.astype(BF16), cvd, preferred_element_type=F32)
        out = acc / den
        for i, t in enumerate(tiles):
            lo_head = out[(2 * i) * n_q:(2 * i + 1) * n_q, :]
            hi_head = out[(2 * i + 1) * n_q:(2 * i + 2) * n_q, :]
            o_ref[:, t * pair_w:(t + 1) * pair_w] = jnp.where(low_q, lo_head, hi_head).astype(o_ref.dtype)


def _attn_common_args(qn_g, kn_g):
    q_w, kv_w = Q_HEADS * HEAD_DIM, KV_HEADS * HEAD_DIM
    return (jnp.tile(qn_g, Q_HEADS).reshape(1, q_w), jnp.tile(kn_g, KV_HEADS).reshape(1, kv_w),
            _head_mean_matrix(q_w), _head_mean_matrix(kv_w))


def _attention_prompt(z, qn_g, kn_g):
    L = PROMPT_LEN
    q_w, kv_w = Q_HEADS * HEAD_DIM, KV_HEADS * HEAD_DIM
    q_col = (5 * A_WIDTH) // q_w
    k_col = (5 * A_WIDTH + q_w) // kv_w
    const = lambda r, c: pl.BlockSpec((r, c), lambda s: (0, 0))
    return pl.pallas_call(
        functools.partial(_attn_kernel, latent=False),
        grid=(N_PROMPT_SEQ,),
        in_specs=[
            pl.BlockSpec((L, q_w), lambda s: (s, q_col)),
            pl.BlockSpec((L, kv_w), lambda s: (s, k_col)),
            pl.BlockSpec((L, kv_w), lambda s: (s, k_col + 1)),
            const(1, q_w), const(1, kv_w), const(q_w, q_w), const(kv_w, kv_w),
        ],
        out_specs=(pl.BlockSpec((L, q_w), lambda s: (s, 0)),
                   pl.BlockSpec((L, kv_w), lambda s: (s, 0))),
        out_shape=(jax.ShapeDtypeStruct((N_TOK, q_w), BF16),
                   jax.ShapeDtypeStruct((N_PROMPT_TOK, kv_w), F32)),
        compiler_params=_params("arbitrary"),
        name="attn_prompt",
    )(z, z, z, *_attn_common_args(qn_g, kn_g))


def _attention_latent(z, qn_g, kn_g, cache_k, cache_v, shared):
    L = LATENT_LEN
    nqb = L // Q_BLOCK
    q_w, kv_w = Q_HEADS * HEAD_DIM, KV_HEADS * HEAD_DIM
    q_col = (5 * A_WIDTH) // q_w
    k_col = (5 * A_WIDTH + q_w) // kv_w
    qrow0 = N_PROMPT_TOK // Q_BLOCK
    krow0 = N_PROMPT_TOK // L
    cos, sin = _rope_tables()
    cos_q, sin_q = jnp.asarray(np.tile(cos, (1, Q_HEADS))), jnp.asarray(np.tile(sin, (1, Q_HEADS)))
    cos_k, sin_k = jnp.asarray(np.tile(cos, (1, KV_HEADS))), jnp.asarray(np.tile(sin, (1, KV_HEADS)))
    const = lambda r, c: pl.BlockSpec((r, c), lambda s, b: (0, 0))
    cache_spec = pl.BlockSpec((None, None, KV_HEADS, PAST_LEN, HEAD_DIM), lambda s, b: (s, 0, 0, 0, 0))
    return pl.pallas_call(
        functools.partial(_attn_kernel, latent=True),
        grid=(N_LATENT_SEQ, nqb),
        in_specs=[
            pl.BlockSpec((Q_BLOCK, q_w), lambda s, b: (qrow0 + s * nqb + b, q_col)),
            pl.BlockSpec((L, kv_w), lambda s, b: (krow0 + s, k_col)),
            pl.BlockSpec((L, kv_w), lambda s, b: (krow0 + s, k_col + 1)),
            const(1, q_w), const(1, kv_w), const(q_w, q_w), const(kv_w, kv_w),
            pl.BlockSpec((Q_BLOCK, q_w), lambda s, b: (b, 0)),
            pl.BlockSpec((Q_BLOCK, q_w), lambda s, b: (b, 0)),
            const(L, kv_w), const(L, kv_w),
            cache_spec, cache_spec, pl.BlockSpec(memory_space=pl.ANY),
        ],
        out_specs=pl.BlockSpec((Q_BLOCK, q_w), lambda s, b: (qrow0 + s * nqb + b, 0)),
        out_shape=jax.ShapeDtypeStruct((N_TOK, q_w), BF16),
        input_output_aliases={13: 0},
        compiler_params=_params("arbitrary", "arbitrary"),
        name="attn_latent",
    )(z, z, z, *_attn_common_args(qn_g, kn_g), cos_q, sin_q, cos_k, sin_k, cache_k, cache_v, shared)


def _out_proj_kernel(*refs, n_act, n_x):
    a_refs = refs[:n_act]
    x_refs = refs[n_act:n_act + n_x]
    g_ref, mod_ref, rw_ref, w_ref, xo_ref, h_ref, lg_ref, wb_ref, rwh_ref, rwl_ref = refs[n_act + n_x:]
    _cast_once(w_ref, wb_ref)

    @pl.when(pl.program_id(0) == 0)
    def _():
        rw = rw_ref[...]
        hi = rw.astype(BF16)
        rwh_ref[...] = hi
        rwl_ref[...] = (rw - hi.astype(F32)).astype(BF16)

    acc = None
    k0 = 0
    for a_ref in a_refs:
        k1 = k0 + a_ref.shape[1]
        part = jnp.dot(a_ref[...], wb_ref[k0:k1, :], preferred_element_type=F32)
        acc = part if acc is None else acc + part
        k0 = k1
    mod = mod_ref[...]
    x_in = x_refs[0][...] if n_x == 1 else _select_trunk(*x_refs)
    x = x_in + mod[2:3, :] * acc
    xo_ref[...] = x
    h = _modulated_norm(x, g_ref[...], mod, 3, 4)
    h_ref[...] = _pack_rows(h)
    h_hi = h.astype(BF16)
    h_lo = (h - h_hi.astype(F32)).astype(BF16)
    lg = jnp.dot(h_hi, rwh_ref[...], preferred_element_type=F32)
    lg = lg + jnp.dot(h_lo, rwh_ref[...], preferred_element_type=F32)
    lg = lg + jnp.dot(h_hi, rwl_ref[...], preferred_element_type=F32)
    lg_ref[...] = lg.T[:N_EXPERTS, :]


def _out_proj(acts, w, xs, g, mod_l, router_wp, block_rows=256):
    tok = lambda width: pl.BlockSpec((block_rows, width), lambda i: (i, 0))
    in_specs = [tok(a.shape[1]) for a in acts]
    in_specs += [tok(D_MODEL)] if len(xs) == 1 else list(_trunk_specs(block_rows, D_MODEL))
    in_specs += [_resident((1, D_MODEL)), _mod_spec(block_rows), _resident((D_MODEL, ROUTER_LANES)),
                 _resident(w.shape)]
    return pl.pallas_call(
        functools.partial(_out_proj_kernel, n_act=len(acts), n_x=len(xs)),
        grid=(N_TOK // block_rows,),
        in_specs=in_specs,
        out_specs=(tok(D_MODEL), tok(ROW_WORDS), pl.BlockSpec((N_EXPERTS, block_rows), lambda i: (0, i))),
        out_shape=(jax.ShapeDtypeStruct((N_TOK, D_MODEL), F32),
                   jax.ShapeDtypeStruct((N_TOK, ROW_WORDS), jnp.int32),
                   jax.ShapeDtypeStruct((N_EXPERTS, N_TOK), F32)),
        scratch_shapes=[pltpu.VMEM(w.shape, BF16), pltpu.VMEM((D_MODEL, ROUTER_LANES), BF16),
                        pltpu.VMEM((D_MODEL, ROUTER_LANES), BF16)],
        compiler_params=_params("arbitrary"),
        name="out_proj",
    )(*acts, *xs, g.reshape(1, D_MODEL), mod_l, router_wp, w)


def _router_kernel(lg_ref, rb_ref, pos_ref, w_ref, plan_ref, rank_ref):
    lg = lg_ref[...]
    ex = jnp.exp(lg - jnp.max(lg, axis=0, keepdims=True))
    scores = ex / jnp.sum(ex, axis=0, keepdims=True)
    biased = scores + rb_ref[...]
    rows = [biased[e:e + 1, :] for e in range(N_EXPERTS)]
    selected = []
    group_score = []
    for gi in range(N_GROUPS):
        r = rows[gi * EXPERTS_PER_GROUP:(gi + 1) * EXPERTS_PER_GROUP]
        total = None
        for i in range(EXPERTS_PER_GROUP):
            rank = None
            for j in range(EXPERTS_PER_GROUP):
                if j == i:
                    continue
                ahead = (r[j] > r[i]) if j > i else (r[j] >= r[i])
                ahead = jnp.where(ahead, 1.0, 0.0)
                rank = ahead if rank is None else rank + ahead
            sel = rank < 1.5
            selected.append(sel)
            contrib = jnp.where(sel, r[i], 0.0)
            total = contrib if total is None else total + contrib
        group_score.append(total)
    best = group_score[0]
    best_group = jnp.zeros_like(best)
    for gi in range(1, N_GROUPS):
        better = group_score[gi] > best
        best_group = jnp.where(better, float(gi), best_group)
        best = jnp.where(better, group_score[gi], best)
    picked = []
    chosen = []
    den = None
    for e in range(N_EXPERTS):
        in_group = best_group == float(e // EXPERTS_PER_GROUP)
        use = jnp.where(selected[e], jnp.where(in_group, 1.0, 0.0), 0.0)
        w = use * scores[e:e + 1, :]
        chosen.append(use)
        picked.append(w)
        den = w if den is None else den + w
    lanes = 128
    n_blk = N_TOK // lanes
    li = lax.broadcasted_iota(jnp.int32, (lanes, lanes), 0)
    lj = lax.broadcasted_iota(jnp.int32, (lanes, lanes), 1)
    prefix = jnp.where(li <= lj, 1.0, 0.0).astype(BF16)
    carry = jnp.zeros((N_EXPERTS, 1), F32)
    for blk in range(n_blk):
        cols = slice(blk * lanes, (blk + 1) * lanes)
        m = jnp.concatenate([chosen[e][:, cols] for e in range(N_EXPERTS)], axis=0)
        incl = jnp.dot(m.astype(BF16), prefix, preferred_element_type=F32)
        rank_ref[:, cols] = incl - m + carry
        carry = carry + incl[:, lanes - 1:lanes]
    count = carry
    padded = jnp.floor((count + float(MOE_TILE - 1)) * (1.0 / MOE_TILE)) * float(MOE_TILE)
    erow = lax.broadcasted_iota(jnp.int32, (N_EXPERTS, 1), 0)
    offset = jnp.zeros((N_EXPERTS, 1), F32)
    for e in range(N_EXPERTS - 1):
        offset = offset + jnp.where(erow > e, padded[e:e + 1, :], 0.0)
    seen = jnp.zeros_like(den)
    pos_a = jnp.zeros_like(den)
    pos_b = jnp.zeros_like(den)
    w_a = jnp.zeros_like(den)
    w_b = jnp.zeros_like(den)
    for e in range(N_EXPERTS):
        pos_e = rank_ref[e:e + 1, :] + offset[e:e + 1, :]
        gate_e = picked[e] / den
        first = jnp.where(seen < 0.5, chosen[e], 0.0) > 0.5
        second = jnp.where(seen > 0.5, chosen[e], 0.0) > 0.5
        pos_a = jnp.where(first, pos_e, pos_a)
        w_a = jnp.where(first, gate_e, w_a)
        pos_b = jnp.where(second, pos_e, pos_b)
        w_b = jnp.where(second, gate_e, w_b)
        seen = seen + chosen[e]
    pos_ref[0:1, :] = pos_a.astype(jnp.int32)
    pos_ref[1:2, :] = pos_b.astype(jnp.int32)
    w_ref[0:1, :] = w_a
    w_ref[1:2, :] = w_b
    start = (lax.broadcasted_iota(jnp.int32, (N_EXPERTS, lanes), 1) * MOE_TILE).astype(F32)
    end = offset + padded
    tile_expert = jnp.sum(jnp.where(end <= start, 1.0, 0.0), axis=0, keepdims=True)
    inside = (offset <= start) & (start < end)
    real = jnp.clip(count - (start - offset), 0.0, float(MOE_TILE))
    tile_rows = jnp.sum(jnp.where(inside, real, 0.0), axis=0, keepdims=True)
    plan_ref[0:1, :] = jnp.minimum(tile_expert, float(N_EXPERTS - 1)).astype(jnp.int32)
    plan_ref[1:2, :] = tile_rows.astype(jnp.int32)


def _router(logits_t, router_b):
    whole = lambda shape: pl.BlockSpec(shape, lambda i: (0, 0))
    return pl.pallas_call(
        _router_kernel,
        grid=(1,),
        in_specs=[whole((N_EXPERTS, N_TOK)), whole((N_EXPERTS, 1))],
        out_specs=(whole((2, N_TOK)), whole((2, N_TOK)), whole((2, 128))),
        out_shape=(jax.ShapeDtypeStruct((2, N_TOK), jnp.int32),
                   jax.ShapeDtypeStruct((2, N_TOK), F32),
                   jax.ShapeDtypeStruct((2, 128), jnp.int32)),
        scratch_shapes=[pltpu.VMEM((N_EXPERTS, N_TOK), F32)],
        compiler_params=_params("arbitrary"),
        name="router",
    )(logits_t, router_b.reshape(N_EXPERTS, 1))


def _sc_mesh():
    return plsc.VectorSubcoreMesh(core_axis_name="c", subcore_axis_name="s")


def _sc_worker_base():
    return (lax.axis_index("s") * SC_CORES + lax.axis_index("c")) * (N_TOK // SC_WORKERS)


def _moe_dispatch(h, pos_a, pos_b):
    n_chunks = N_TOK // SC_WORKERS // SC_CHUNK

    @functools.partial(
        pl.kernel, mesh=_sc_mesh(),
        out_type=jax.ShapeDtypeStruct((MOE_ROWS, ROW_WORDS), jnp.int32),
        scratch_types=[pltpu.VMEM((SC_CHUNK,), jnp.int32), pltpu.VMEM((SC_CHUNK,), jnp.int32),
                       pltpu.VMEM((SC_CHUNK, ROW_WORDS), jnp.int32)],
        name="moe_dispatch",
    )
    def run(h_hbm, pa_hbm, pb_hbm, xs_hbm, ia_v, ib_v, rows_v):
        base = _sc_worker_base()

        @pl.loop(0, n_chunks)
        def _(ci):
            tok = pl.ds(pl.multiple_of(base + ci * SC_CHUNK, SC_CHUNK), SC_CHUNK)
            pltpu.sync_copy(pa_hbm.at[tok], ia_v)
            pltpu.sync_copy(pb_hbm.at[tok], ib_v)
            pltpu.sync_copy(h_hbm.at[tok], rows_v)
            pltpu.sync_copy(rows_v, xs_hbm.at[ia_v])
            pltpu.sync_copy(rows_v, xs_hbm.at[ib_v])

    return run(h, pos_a, pos_b)


def _moe_collect(ys, pos_a, pos_b):
    n_chunks = N_TOK // SC_WORKERS // SC_CHUNK
    out = jax.ShapeDtypeStruct((N_TOK, ROW_WORDS), jnp.int32)

    @functools.partial(
        pl.kernel, mesh=_sc_mesh(), out_type=(out, out),
        scratch_types=[pltpu.VMEM((SC_CHUNK,), jnp.int32), pltpu.VMEM((SC_CHUNK,), jnp.int32),
                       pltpu.VMEM((SC_CHUNK, ROW_WORDS), jnp.int32)],
        name="moe_collect",
    )
    def run(ys_hbm, pa_hbm, pb_hbm, ya_hbm, yb_hbm, ia_v, ib_v, rows_v):
        base = _sc_worker_base()

        @pl.loop(0, n_chunks)
        def _(ci):
            tok = pl.ds(pl.multiple_of(base + ci * SC_CHUNK, SC_CHUNK), SC_CHUNK)
            pltpu.sync_copy(pa_hbm.at[tok], ia_v)
            pltpu.sync_copy(pb_hbm.at[tok], ib_v)
            pltpu.sync_copy(ys_hbm.at[ia_v], rows_v)
            pltpu.sync_copy(rows_v, ya_hbm.at[tok])
            pltpu.sync_copy(ys_hbm.at[ib_v], rows_v)
            pltpu.sync_copy(rows_v, yb_hbm.at[tok])

    return run(ys, pos_a, pos_b)


def _experts_kernel(plan_ref, xs_ref, wg_ref, wu_ref, wd_ref, y_ref, wgb_ref, wub_ref, wdb_ref):
    j = pl.program_id(0)
    expert = plan_ref[j]
    n_real = plan_ref[128 + j]
    fresh = jnp.logical_or(j == 0, expert != plan_ref[jnp.maximum(j - 1, 0)])

    @pl.when(jnp.logical_and(n_real > 0, fresh))
    def _():
        wgb_ref[...] = wg_ref[...].astype(BF16)
        wub_ref[...] = wu_ref[...].astype(BF16)
        wdb_ref[...] = wd_ref[...].astype(BF16)

    @pl.when(n_real > 0)
    def _():
        row = lax.broadcasted_iota(jnp.int32, xs_ref.shape, 0)
        words = jnp.where(row < n_real, xs_ref[...], 0)
        x = _unpack_rows(words).astype(BF16)
        a = jnp.dot(x, wgb_ref[...], preferred_element_type=F32)
        b = jnp.dot(x, wub_ref[...], preferred_element_type=F32)
        hid = (a * jax.nn.sigmoid(a)) * b
        y_ref[...] = _pack_rows(jnp.dot(hid.astype(BF16), wdb_ref[...], preferred_element_type=F32))


def _experts(plan, xs, w_gate, w_up, w_down, layer):
    wspec = lambda r, c: pl.BlockSpec((None, None, r, c), lambda j, plan: (layer, plan[j], 0, 0))
    return pl.pallas_call(
        _experts_kernel,
        grid_spec=pltpu.PrefetchScalarGridSpec(
            num_scalar_prefetch=1,
            grid=(MOE_ROWS // MOE_TILE,),
            in_specs=[pl.BlockSpec((MOE_TILE, ROW_WORDS), lambda j, plan: (j, 0)),
                      wspec(D_MODEL, D_EXPERT), wspec(D_MODEL, D_EXPERT), wspec(D_EXPERT, D_MODEL)],
            out_specs=pl.BlockSpec((MOE_TILE, ROW_WORDS), lambda j, plan: (j, 0)),
            scratch_shapes=[pltpu.VMEM((D_MODEL, D_EXPERT), BF16), pltpu.VMEM((D_MODEL, D_EXPERT), BF16),
                            pltpu.VMEM((D_EXPERT, D_MODEL), BF16)],
        ),
        out_shape=jax.ShapeDtypeStruct((MOE_ROWS, ROW_WORDS), jnp.int32),
        compiler_params=_params("arbitrary"),
        name="experts",
    )(plan, xs, w_gate, w_up, w_down)


def _combine_kernel(x_ref, ya_ref, yb_ref, wt_ref, mod_ref, o_ref):
    o_ref[...] = _moe_mix(x_ref, ya_ref, yb_ref, wt_ref, mod_ref)


def _combine(x, moe_out, mod_l, tok0, n_tok, block_rows=512):
    ya, yb, w_tok = moe_out
    b0 = tok0 // block_rows
    rows = lambda width: pl.BlockSpec((block_rows, width), lambda i: (b0 + i, 0))
    return pl.pallas_call(
        _combine_kernel,
        grid=(n_tok // block_rows,),
        in_specs=[rows(D_MODEL), rows(ROW_WORDS), rows(ROW_WORDS), rows(TOP_K),
                  pl.BlockSpec((None, 6, D_MODEL), lambda i: (_cond_of_token_block(b0 + i, block_rows), 0, 0))],
        out_specs=pl.BlockSpec((block_rows, D_MODEL), lambda i: (i, 0)),
        out_shape=jax.ShapeDtypeStruct((n_tok, D_MODEL), F32),
        compiler_params=_params("arbitrary"),
        name="combine",
    )(x, ya, yb, w_tok, mod_l)


def _moe(h, logits_t, router_b, w_gate, w_up, w_down, layer):
    pos, w, plan = _router(logits_t, router_b)
    xs = _moe_dispatch(h, pos[0], pos[1])
    ys = _experts(plan.reshape(-1), xs, w_gate, w_up, w_down, layer)
    ya, yb = _moe_collect(ys, pos[0], pos[1])
    return ya, yb, w.T


def _dft_tables(L):
    k = np.arange(L)[:, None]
    m = np.arange(L)[None, :]
    r = (k * m) % (2 * L)
    ang = np.pi * r.astype(np.float64) / L
    fc = np.cos(ang)
    fs = np.sin(ang)
    fs[0, :] = np.where(np.arange(L) % 2 == 0, 1.0, -1.0)
    wk = np.full((L, 1), 1.0 / L)
    wk[0, 0] = 0.5 / L
    gc = (fc * wk).T
    gs = (fs * wk).T
    return [jnp.asarray(t.astype(np.float32)).astype(BF16) for t in (fc, fs, gc, gs)]


def _filter_consts(L):
    t = np.linspace(0.0, 1.0, L, dtype=np.float32)[:, None]
    w = (np.float32(2.0 * np.pi) * np.arange(L, dtype=np.float32)[:, None] / np.float32(L)).astype(np.float32)
    fb = np.linspace(1e-4, HY_BANDS - 1, HY_BANDS, dtype=np.float32)[None, :]
    emb = np.concatenate([t, np.cos(fb * w), -np.sin(fb * w)], axis=-1).astype(np.float32)
    lo = math.log(HY_DECAY_TARGET) / HY_SLOW_PCT
    hi = math.log(HY_DECAY_TARGET) / HY_FAST_PCT
    deltas = np.abs(np.linspace(lo, hi, D_MODEL, dtype=np.float32))
    decay = np.exp(-t * deltas).astype(np.float32)
    return jnp.asarray(emb), jnp.asarray(decay)


def _filter_kernel(emb_ref, w1_ref, b1_ref, w2_ref, b2_ref, fr_ref, w3f_ref, w3b_ref, dec_ref,
                   fc_ref, fs_ref, kr_ref, q_ref, krn_ref, hd_ref):
    @pl.when(pl.program_id(0) == 0)
    def _():
        fr = fr_ref[...]
        h1 = jnp.sin(fr * (jnp.dot(emb_ref[...], w1_ref[...], precision=HIGHEST,
                                   preferred_element_type=F32) + b1_ref[...]))
        hd_ref[...] = jnp.sin(fr * (jnp.dot(h1, w2_ref[...], precision=HIGHEST,
                                            preferred_element_type=F32) + b2_ref[...]))

    hd = hd_ref[...]
    dec = dec_ref[...]
    f = jnp.dot(hd, w3f_ref[...], precision=HIGHEST, preferred_element_type=F32) * dec
    g = jnp.dot(hd, w3b_ref[...], precision=HIGHEST, preferred_element_type=F32) * dec
    row = lax.broadcasted_iota(jnp.int32, f.shape, 0)
    g = jnp.where(row == 0, 0.0, g)
    s = f + g
    d = f - g
    kr = jnp.dot(fc_ref[...], s.astype(BF16), preferred_element_type=F32)
    qq = jnp.dot(fs_ref[...], d.astype(BF16), preferred_element_type=F32)
    alt = jnp.where(row % 2 == 0, 1.0, -1.0)
    nyq = jnp.sum(alt * s, axis=0, keepdims=True)
    kr_ref[...] = kr
    q_ref[...] = jnp.where(row == 0, 0.0, qq)
    krn_ref[...] = jnp.where(row == 0, nyq, kr)


def _hyena_filter_spectrum(L, w1, b1, w2, b2, w3, freq, fc, fs, cblk=256):
    emb, decay = _filter_consts(L)
    ncb = D_MODEL // cblk
    n_emb = 128
    emb = jnp.pad(emb, ((0, 0), (0, n_emb - emb.shape[1])))
    w1 = jnp.pad(w1, ((0, n_emb - w1.shape[0]), (0, 0)))
    full = lambda shape: pl.BlockSpec(shape, lambda j: tuple(0 for _ in shape))
    out_sds = jax.ShapeDtypeStruct((L, D_MODEL), F32)
    out_spec = pl.BlockSpec((L, cblk), lambda j: (0, j))
    return pl.pallas_call(
        _filter_kernel,
        grid=(ncb,),
        in_specs=[
            full((L, n_emb)), full((n_emb, HY_FFN)), full((1, HY_FFN)), full((HY_FFN, HY_FFN)),
            full((1, HY_FFN)), full((1, HY_FFN)),
            pl.BlockSpec((HY_FFN, cblk), lambda j: (0, j)),
            pl.BlockSpec((HY_FFN, cblk), lambda j: (0, ncb + j)),
            pl.BlockSpec((L, cblk), lambda j: (0, j)),
            full((L, L)), full((L, L)),
        ],
        out_specs=(out_spec, out_spec, out_spec),
        out_shape=(out_sds, out_sds, out_sds),
        scratch_shapes=[pltpu.VMEM((L, HY_FFN), F32)],
        compiler_params=_params("arbitrary"),
        name=f"hyena_filter_{L}",
    )(emb, w1, b1.reshape(1, HY_FFN), w2, b2.reshape(1, HY_FFN), freq.reshape(1, HY_FFN), w3, w3, decay, fc, fs)


def _hyena_conv_kernel(x0_ref, x1_ref, v_ref, cw0_ref, cw1_ref, cwv_ref, cb0_ref, cb1_ref, cbv_ref,
                       kr_ref, q_ref, krn_ref, ds_ref, fc_ref, fs_ref, gc_ref, gs_ref, *rest):
    o_ref = rest[-1]
    L = x0_ref.shape[0]
    row = lax.broadcasted_iota(jnp.int32, x0_ref.shape, 0)

    def short_conv(u_ref, w_ref, b_ref):
        u = u_ref[...].astype(F32)
        w = w_ref[...]
        prev = jnp.where(row == 0, 0.0, pltpu.roll(u, 1, axis=0))
        nxt = jnp.where(row == L - 1, 0.0, pltpu.roll(u, L - 1, axis=0))
        return prev * w[0:1, :] + u * w[1:2, :] + nxt * w[2:3, :] + b_ref[...]

    x0 = short_conv(x0_ref, cw0_ref, cb0_ref)
    x1 = short_conv(x1_ref, cw1_ref, cb1_ref)
    v = short_conv(v_ref, cwv_ref, cbv_ref)
    zz = v * x1
    zb = zz.astype(BF16)
    ur = jnp.dot(fc_ref[...], zb, preferred_element_type=F32)
    p = jnp.dot(fs_ref[...], zb, preferred_element_type=F32)
    qq = q_ref[...]
    yr = ur * kr_ref[...] - p * qq
    yw = ur * qq + p * krn_ref[...]
    y = jnp.dot(gc_ref[...], yr.astype(BF16), preferred_element_type=F32)
    y = y + jnp.dot(gs_ref[...], yw.astype(BF16), preferred_element_type=F32)
    o_ref[...] = (x0 * (y + zz * ds_ref[...])).astype(o_ref.dtype)


def _hyena_conv(u, conv_w, conv_b, dskip, spectrum, tables, shared, *, latent):
    L = LATENT_LEN if latent else PROMPT_LEN
    n_seq = N_LATENT_SEQ if latent else N_PROMPT_SEQ
    cblk = 256 if latent else 512
    ncb = D_MODEL // cblk
    row0 = (N_PROMPT_TOK // L) if latent else 0
    kr, qq, krn = spectrum
    fc, fs, gc, gs = tables

    def part(p, rows):
        return pl.BlockSpec((rows, cblk), lambda j, s: (0 if rows != L else row0 + s, p * ncb + j))

    def const_cols(rows):
        return pl.BlockSpec((rows, cblk), lambda j, s: (0, j))

    mat = pl.BlockSpec((L, L), lambda j, s: (0, 0))
    conv_b2 = conv_b.reshape(1, 3 * D_MODEL)
    in_specs = [part(0, L), part(1, L), part(2, L),
                part(0, 3), part(1, 3), part(2, 3),
                part(0, 1), part(1, 1), part(2, 1),
                const_cols(L), const_cols(L), const_cols(L), const_cols(1),
                mat, mat, mat, mat]
    args = [u, u, u, conv_w, conv_w, conv_w, conv_b2, conv_b2, conv_b2,
            kr, qq, krn, dskip.reshape(1, D_MODEL), fc, fs, gc, gs]
    aliases = {}
    if latent:
        in_specs.append(pl.BlockSpec(memory_space=pl.ANY))
        args.append(shared)
        aliases = {len(args) - 1: 0}
    return pl.pallas_call(
        _hyena_conv_kernel,
        grid=(ncb, n_seq),
        in_specs=in_specs,
        out_specs=pl.BlockSpec((L, cblk), lambda j, s: (row0 + s, j)),
        out_shape=jax.ShapeDtypeStruct((N_TOK, D_MODEL), BF16),
        input_output_aliases=aliases,
        compiler_params=_params("arbitrary", "arbitrary"),
        name="hyena_conv_latent" if latent else "hyena_conv_prompt",
    )(*args)


def kernel(x_prompt, x_sample, cache_k, cache_v, state_hgrn, c, c_ctx, norm_g, mod_w, mod_b, ab_in_w, hgrn_lb, hgrn_onorm_g, attn_qnorm_g, attn_knorm_g, ab_out_w, hy_in_w, hy_in_b, hy_conv_w, hy_conv_b, hy_f_w1, hy_f_b1, hy_f_w2, hy_f_b2, hy_f_w3, hy_f_freq, hy_dskip, hy_out_w, router_w, router_b, moe_w_gate, moe_w_up, moe_w_down):
    xp = x_prompt.reshape(N_PROMPT_TOK, D_MODEL)
    xl = x_sample.reshape(N_LATENT_TOK, D_MODEL)
    cond = jnp.concatenate([c_ctx[None, :], c, jnp.zeros((N_COND - 1 - N_LATENT_SEQ, D_MODEL), F32)], axis=0)
    mod = _modulation(cond, mod_w, mod_b)
    router_wp = jnp.pad(router_w, ((0, 0), (0, ROUTER_LANES - N_EXPERTS)))

    z = _in_proj0(xp, xl, norm_g[0, 0], mod[0], ab_in_w[0])
    o_a, new_state = _hgrn(z, hgrn_lb, hgrn_onorm_g[0], None, None, latent=False)
    o_a = _hgrn(z, hgrn_lb, hgrn_onorm_g[0], state_hgrn, o_a, latent=True)
    o_b, k_prompt = _attention_prompt(z, attn_qnorm_g[0], attn_knorm_g[0])
    o_b = _attention_latent(z, attn_qnorm_g[0], attn_knorm_g[0], cache_k, cache_v, o_b)
    x, h, logits_t = _out_proj([o_a, o_b], ab_out_w[0], (xp, xl), norm_g[0, 1], mod[0], router_wp)
    moe_out = _moe(h, logits_t, router_b, moe_w_gate, moe_w_up, moe_w_down, 0)

    x, u = _in_proj1(x, moe_out, mod[0], norm_g[1, 0], mod[1], hy_in_w[0], hy_in_b[0])
    pre = None
    for latent in (False, True):
        L = LATENT_LEN if latent else PROMPT_LEN
        tables = _dft_tables(L)
        spectrum = _hyena_filter_spectrum(L, hy_f_w1[0], hy_f_b1[0], hy_f_w2[0], hy_f_b2[0], hy_f_w3[0],
                                          hy_f_freq[0], tables[0], tables[1])
        pre = _hyena_conv(u, hy_conv_w[0], hy_conv_b[0], hy_dskip[0], spectrum, tables, pre, latent=latent)
    x, h, logits_t = _out_proj([pre], hy_out_w[0], (x,), norm_g[1, 1], mod[1], router_wp)
    moe_out = _moe(h, logits_t, router_b, moe_w_gate, moe_w_up, moe_w_down, 1)

    y_prompt = _combine(x, moe_out, mod[1], 0, N_PROMPT_TOK).reshape(N_PROMPT_SEQ, PROMPT_LEN, D_MODEL)
    y_sample = _combine(x, moe_out, mod[1], N_PROMPT_TOK, N_LATENT_TOK).reshape(N_LATENT_SEQ, LATENT_LEN, D_MODEL)
    kv_shape = (N_PROMPT_SEQ, PROMPT_LEN, KV_HEADS, HEAD_DIM)
    new_k = k_prompt.reshape(kv_shape).transpose(0, 2, 1, 3)[:, None]
    v_col = 5 * A_WIDTH + (Q_HEADS + KV_HEADS) * HEAD_DIM
    new_v = z[:N_PROMPT_TOK, v_col:].astype(F32).reshape(kv_shape).transpose(0, 2, 1, 3)[:, None]
    return (y_prompt, y_sample, new_k, new_v, new_state)
```

```python
import functools
import math

import numpy as np
import jax
import jax.numpy as jnp
from jax import lax
from jax.experimental import pallas as pl
from jax.experimental.pallas import tpu as pltpu
from jax.experimental.pallas import tpu_sc as plsc

F32 = jnp.float32
BF16 = jnp.bfloat16
HIGHEST = lax.Precision.HIGHEST

D_MODEL = 1024
N_PROMPT_SEQ = 32
PROMPT_LEN = 256
N_LATENT_SEQ = 2
LATENT_LEN = 1024
PAST_LEN = 512
GRID_W = 64
N_PROMPT_TOK = N_PROMPT_SEQ * PROMPT_LEN
N_LATENT_TOK = N_LATENT_SEQ * LATENT_LEN
N_TOK = N_PROMPT_TOK + N_LATENT_TOK
N_COND = 8
EPS = 1e-6

A_WIDTH = 512
A_HEADS = 4
A_DK = 128
CHUNK = 64
HGRN_BLOCK = 256
HGRN_HEADS_PER_STEP = 4
HEAD_DIM = 64
Q_HEADS = 8
KV_HEADS = 2
Q_PER_KV = Q_HEADS // KV_HEADS
Q_BLOCK = 256
ROPE_THETA = 10000.0
ROPE_PAIRS = HEAD_DIM // 4
AB_IN = 5 * A_WIDTH + (Q_HEADS + 2 * KV_HEADS) * HEAD_DIM

HY_BANDS = 16
HY_FFN = 64
HY_DECAY_TARGET = 1e-2
HY_FAST_PCT = 0.3
HY_SLOW_PCT = 1.5

N_EXPERTS = 16
N_GROUPS = 4
EXPERTS_PER_GROUP = 4
TOP_K = 2
D_EXPERT = 512
ROUTER_LANES = 128
MOE_TILE = 512
MOE_ROWS = N_TOK * TOP_K + N_EXPERTS * MOE_TILE
PLAN_LANES = 128

SC_CORES = 2
SC_WORKERS = 32
SC_CHUNK = 80
ROW_WORDS = D_MODEL // 2

VMEM_LIMIT = 56 * 1024 * 1024


def _params(*sem):
    return pltpu.CompilerParams(dimension_semantics=sem, vmem_limit_bytes=VMEM_LIMIT)


def _pack_rows(x):
    n = x.shape[1] // 2
    bits = pltpu.bitcast(x.astype(BF16).astype(F32), jnp.uint32)
    return pltpu.bitcast(bits[:, :n] | (bits[:, n:] >> 16), jnp.int32)


def _unpack_rows(p):
    bits = pltpu.bitcast(p, jnp.uint32)
    hi = pltpu.bitcast(bits & jnp.uint32(0xFFFF0000), F32)
    lo = pltpu.bitcast(bits << 16, F32)
    return jnp.concatenate([hi, lo], axis=1)


def _cond_of_token_block(i, block_rows):
    start = i * block_rows
    return jnp.where(start < N_PROMPT_TOK, 0, 1 + (start - N_PROMPT_TOK) // LATENT_LEN)


def _mod_kernel(cond_ref, w_ref, b_ref, o_ref):
    cnd = cond_ref[...]
    s = cnd * jax.nn.sigmoid(cnd)
    s_hi = s.astype(BF16)
    s_lo = (s - s_hi.astype(F32)).astype(BF16)
    w = w_ref[...]
    w_hi = w.astype(BF16)
    w_lo = (w - w_hi.astype(F32)).astype(BF16)
    acc = jnp.dot(s_hi, w_hi, preferred_element_type=F32)
    acc = acc + jnp.dot(s_lo, w_hi, preferred_element_type=F32)
    acc = acc + jnp.dot(s_hi, w_lo, preferred_element_type=F32)
    o_ref[...] = acc + b_ref[...]


def _modulation(cond, mod_w, mod_b):
    depth = mod_w.shape[0]
    n_chunk = 6
    out = pl.pallas_call(
        _mod_kernel,
        grid=(depth, n_chunk),
        in_specs=[
            pl.BlockSpec((N_COND, D_MODEL), lambda l, j: (0, 0)),
            pl.BlockSpec((None, D_MODEL, D_MODEL), lambda l, j: (l, 0, j)),
            pl.BlockSpec((None, 1, D_MODEL), lambda l, j: (l, 0, j)),
        ],
        out_specs=pl.BlockSpec((None, N_COND, D_MODEL), lambda l, j: (l, 0, j)),
        out_shape=jax.ShapeDtypeStruct((depth, N_COND, n_chunk * D_MODEL), F32),
        compiler_params=_params("arbitrary", "arbitrary"),
        name="modulation",
    )(cond, mod_w, mod_b.reshape(depth, 1, n_chunk * D_MODEL))
    return out.reshape(depth, N_COND, n_chunk, D_MODEL)


def _modulated_norm(x, g, mod, shift_row, scale_row):
    ms = jnp.mean(x * x, axis=-1, keepdims=True)
    y = x * lax.rsqrt(ms + EPS) * g
    return y * (1.0 + mod[scale_row:scale_row + 1, :]) + mod[shift_row:shift_row + 1, :]


def _trunk_specs(block_rows, width):
    n_prompt_blocks = N_PROMPT_TOK // block_rows
    return (pl.BlockSpec((block_rows, width), lambda i: (jnp.minimum(i, n_prompt_blocks - 1), 0)),
            pl.BlockSpec((block_rows, width), lambda i: (jnp.maximum(i - n_prompt_blocks, 0), 0)))


def _select_trunk(p_ref, l_ref):
    block_rows = p_ref.shape[0]
    return jnp.where(pl.program_id(0) < N_PROMPT_TOK // block_rows, p_ref[...], l_ref[...])


def _cast_once(w_ref, wb_ref):
    @pl.when(pl.program_id(0) == 0)
    def _():
        wb_ref[...] = w_ref[...].astype(BF16)


def _resident(shape):
    return pl.BlockSpec(shape, lambda i: tuple(0 for _ in shape), pipeline_mode=pl.Buffered(1))


def _mod_spec(block_rows):
    return pl.BlockSpec((None, 6, D_MODEL), lambda i: (_cond_of_token_block(i, block_rows), 0, 0))


def _in_proj0_kernel(xp_ref, xl_ref, g_ref, mod_ref, w_ref, o_ref, wb_ref):
    _cast_once(w_ref, wb_ref)
    h = _modulated_norm(_select_trunk(xp_ref, xl_ref), g_ref[...], mod_ref[...], 0, 1)
    o_ref[...] = jnp.dot(h.astype(BF16), wb_ref[...], preferred_element_type=F32).astype(o_ref.dtype)


def _in_proj0(x_prompt, x_latent, g, mod_l, w, block_rows=512):
    n = w.shape[1]
    return pl.pallas_call(
        _in_proj0_kernel,
        grid=(N_TOK // block_rows,),
        in_specs=[*_trunk_specs(block_rows, D_MODEL), _resident((1, D_MODEL)), _mod_spec(block_rows),
                  _resident((D_MODEL, n))],
        out_specs=pl.BlockSpec((block_rows, n), lambda i: (i, 0)),
        out_shape=jax.ShapeDtypeStruct((N_TOK, n), BF16),
        scratch_shapes=[pltpu.VMEM((D_MODEL, n), BF16)],
        compiler_params=_params("arbitrary"),
        name="in_proj0",
    )(x_prompt, x_latent, g.reshape(1, D_MODEL), mod_l, w)


def _moe_mix(x_ref, ya_ref, yb_ref, wt_ref, mod_ref):
    wt = wt_ref[...]
    mix = wt[:, 0:1] * _unpack_rows(ya_ref[...]) + wt[:, 1:2] * _unpack_rows(yb_ref[...])
    return x_ref[...] + mod_ref[5:6, :] * mix


def _in_proj1_kernel(x_ref, ya_ref, yb_ref, wt_ref, modp_ref, g_ref, mod_ref, w_ref, b_ref, xo_ref, o_ref, wb_ref):
    _cast_once(w_ref, wb_ref)
    x = _moe_mix(x_ref, ya_ref, yb_ref, wt_ref, modp_ref)
    xo_ref[...] = x
    h = _modulated_norm(x, g_ref[...], mod_ref[...], 0, 1)
    u = jnp.dot(h.astype(BF16), wb_ref[...], preferred_element_type=F32) + b_ref[...]
    o_ref[...] = u.astype(o_ref.dtype)


def _in_proj1(x, moe_out, mod_prev, g, mod_l, w, bias, block_rows=512):
    ya, yb, w_tok = moe_out
    n = w.shape[1]
    tok = pl.BlockSpec((block_rows, D_MODEL), lambda i: (i, 0))
    packed = pl.BlockSpec((block_rows, ROW_WORDS), lambda i: (i, 0))
    return pl.pallas_call(
        _in_proj1_kernel,
        grid=(N_TOK // block_rows,),
        in_specs=[tok, packed, packed, pl.BlockSpec((block_rows, TOP_K), lambda i: (i, 0)), _mod_spec(block_rows),
                  _resident((1, D_MODEL)), _mod_spec(block_rows), _resident((D_MODEL, n)), _resident((1, n))],
        out_specs=(tok, pl.BlockSpec((block_rows, n), lambda i: (i, 0))),
        out_shape=(jax.ShapeDtypeStruct((N_TOK, D_MODEL), F32), jax.ShapeDtypeStruct((N_TOK, n), BF16)),
        scratch_shapes=[pltpu.VMEM((D_MODEL, n), BF16)],
        compiler_params=_params("arbitrary"),
        name="in_proj1",
    )(x, ya, yb, w_tok, mod_prev, g.reshape(1, D_MODEL), mod_l, w, bias.reshape(1, n))


def _hgrn_kernel(*refs, seq_len, with_state):
    if with_state:
        (q_ref, zf_ref, zb_ref, i_ref, ga_ref, lb_ref, og_ref, s0_ref, o_ref, of_ref, ob_ref) = refs
    else:
        (q_ref, zf_ref, zb_ref, i_ref, ga_ref, lb_ref, og_ref, o_ref, s_ref, of_ref, ob_ref) = refs
    n_blocks = seq_len // HGRN_BLOCK
    chunks_per_block = HGRN_BLOCK // CHUNK

    lbr = lb_ref[...]
    mx = jnp.maximum(lbr[0], lbr[1])
    e0 = jnp.exp(lbr[0] - mx)
    e1 = jnp.exp(lbr[1] - mx)
    lb = e0 / (e0 + e1)

    row = lax.broadcasted_iota(jnp.int32, (HGRN_BLOCK, HGRN_BLOCK), 0)
    col = lax.broadcasted_iota(jnp.int32, (HGRN_BLOCK, HGRN_BLOCK), 1)
    same_chunk = (row // CHUNK) == (col // CHUNK)
    nt = (((1,), (1,)), ((), ()))
    tn = (((0,), (0,)), ((), ()))

    def per_chunk_row(x, idx):
        return jnp.concatenate(
            [jnp.broadcast_to(x[n * CHUNK + idx:n * CHUNK + idx + 1, :], (CHUNK, x.shape[1]))
             for n in range(chunks_per_block)], axis=0)

    def in_chunk_cumsum(tri, x):
        hi = x.astype(BF16)
        lo = (x - hi.astype(F32)).astype(BF16)
        return jnp.dot(tri, hi, preferred_element_type=F32) + jnp.dot(tri, lo, preferred_element_type=F32)

    def block(blk, cols, st, z_ref, lbd, forward, out_ref):
        rows = slice(blk * HGRN_BLOCK, (blk + 1) * HGRN_BLOCK)
        keep = (same_chunk & (col <= row)) if forward else (same_chunk & (col >= row))
        tri = jnp.where(keep, 1.0, 0.0).astype(BF16)
        mid = CHUNK // 2 if forward else CHUNK - 1 - CHUNK // 2
        last = CHUNK - 1 if forward else 0
        f = lbd + (1.0 - lbd) * jax.nn.sigmoid(z_ref[rows, cols].astype(F32))
        lf = jnp.log(f)
        k = 1.0 - f
        q = q_ref[rows, cols].astype(F32)
        vb = i_ref[rows, cols].astype(BF16)
        b = in_chunk_cumsum(tri, lf)
        bm = per_chunk_row(b, mid)
        bl = per_chunk_row(b, last)
        qe = (q * jnp.exp(b - bm)).astype(BF16)
        ke = (k * jnp.exp(bm - b)).astype(BF16)
        att = lax.dot_general(qe, ke, nt, preferred_element_type=F32)
        att = jnp.where(keep, att, 0.0)
        o_intra = jnp.dot(att.astype(BF16), vb, preferred_element_type=F32)
        qb = (q * jnp.exp(b)).astype(BF16)
        ks = (k * jnp.exp(bl - b)).astype(BF16)
        decay = jnp.exp(bl)
        order = range(chunks_per_block) if forward else range(chunks_per_block - 1, -1, -1)
        o_inter = [None] * chunks_per_block
        for n in order:
            cr = slice(n * CHUNK, (n + 1) * CHUNK)
            o_inter[n] = lax.dot_general(qb[cr], st.astype(BF16), nt, preferred_element_type=F32)
            upd = lax.dot_general(vb[cr], ks[cr], tn, preferred_element_type=F32)
            st = st * decay[n * CHUNK:n * CHUNK + 1, :] + upd
        out_ref[rows, cols] = o_intra + jnp.concatenate(o_inter, axis=0)
        return st

    for hd in range(q_ref.shape[1] // A_DK):
        cols = slice(hd * A_DK, (hd + 1) * A_DK)
        if with_state:
            st_f, st_b = s0_ref[0, hd].T, s0_ref[1, hd].T
        else:
            st_f, st_b = jnp.zeros((A_DK, A_DK), F32), jnp.zeros((A_DK, A_DK), F32)
        for step in range(n_blocks):
            st_f = block(step, cols, st_f, zf_ref, lb[0:1, cols], True, of_ref)
            st_b = block(n_blocks - 1 - step, cols, st_b, zb_ref, lb[1:2, cols], False, ob_ref)
        if not with_state:
            s_ref[0, hd] = st_f.T
            s_ref[1, hd] = st_b.T
        o = of_ref[:, cols] + ob_ref[:, cols]
        o = o * lax.rsqrt(jnp.mean(o * o, axis=-1, keepdims=True) + EPS) * og_ref[:, cols]
        ga = ga_ref[:, cols].astype(F32)
        o_ref[:, cols] = (o * (ga * jax.nn.sigmoid(ga))).astype(o_ref.dtype)


def _hgrn(z, hgrn_lb, onorm_g, state, *, latent):
    seq_len = LATENT_LEN if latent else PROMPT_LEN
    n_seq = N_LATENT_SEQ if latent else N_PROMPT_SEQ
    row0 = (N_PROMPT_TOK // seq_len) if latent else 0

    hw = HGRN_HEADS_PER_STEP * A_DK
    n_hg = A_HEADS // HGRN_HEADS_PER_STEP

    def zspec(part):
        return pl.BlockSpec((seq_len, hw), lambda s, h: (row0 + s, part * n_hg + h))

    in_specs = [zspec(0), zspec(1), zspec(2), zspec(3), zspec(4),
                pl.BlockSpec((2, 2, hw), lambda s, h: (0, 0, h)),
                pl.BlockSpec((1, hw), lambda s, h: (0, h))]
    args = [z, z, z, z, z, hgrn_lb, onorm_g.reshape(1, A_WIDTH)]
    state_spec = pl.BlockSpec((None, None, 2, HGRN_HEADS_PER_STEP, A_DK, A_DK), lambda s, h: (s, 0, 0, h, 0, 0))
    o_shape = jax.ShapeDtypeStruct((n_seq * seq_len, A_WIDTH), BF16)
    o_spec = pl.BlockSpec((seq_len, hw), lambda s, h: (s, h))
    if latent:
        in_specs.append(state_spec)
        args.append(state)
        out_shape, out_specs = o_shape, o_spec
    else:
        out_shape = (o_shape, jax.ShapeDtypeStruct((n_seq, 1, 2, A_HEADS, A_DK, A_DK), F32))
        out_specs = (o_spec, state_spec)
    return pl.pallas_call(
        functools.partial(_hgrn_kernel, seq_len=seq_len, with_state=latent),
        grid=(n_seq, n_hg),
        in_specs=in_specs,
        out_specs=out_specs,
        out_shape=out_shape,
        scratch_shapes=[pltpu.VMEM((seq_len, hw), F32), pltpu.VMEM((seq_len, hw), F32)],
        compiler_params=_params("arbitrary", "arbitrary"),
        name="hgrn_latent" if latent else "hgrn_prompt",
    )(*args)


def _rope_tables():
    pos = np.arange(LATENT_LEN)
    row, colp = pos // GRID_W, pos % GRID_W
    inv = ROPE_THETA ** (-np.arange(ROPE_PAIRS, dtype=np.float32) / ROPE_PAIRS)
    inv = inv.astype(np.float32)
    ang_r = (row.astype(np.float32)[:, None] * inv).astype(np.float32)
    ang_c = (colp.astype(np.float32)[:, None] * inv).astype(np.float32)
    cos = np.concatenate([np.cos(ang_r), np.cos(ang_r), np.cos(ang_c), np.cos(ang_c)], axis=1)
    sin = np.concatenate([-np.sin(ang_r), np.sin(ang_r), -np.sin(ang_c), np.sin(ang_c)], axis=1)
    return cos.astype(np.float32), sin.astype(np.float32)


def _head_mean_matrix(width):
    idx = np.arange(width) // HEAD_DIM
    return jnp.asarray((idx[:, None] == idx[None, :]).astype(np.float32) / HEAD_DIM).astype(BF16)


def _attn_kernel(*refs, latent):
    if latent:
        (q_ref, k_ref, v_ref, qg_ref, kg_ref, gq_ref, gk_ref, cosq_ref, sinq_ref, cosk_ref, sink_ref,
         ck_ref, cv_ref, o_ref) = refs
    else:
        (q_ref, k_ref, v_ref, qg_ref, kg_ref, gq_ref, gk_ref, o_ref, kout_ref) = refs
    pair_w = 2 * HEAD_DIM

    def head_norm(x, mean_ref, gain):
        sq = x * x
        hi = sq.astype(BF16)
        lo = (sq - hi.astype(F32)).astype(BF16)
        ms = jnp.dot(hi, mean_ref[...], preferred_element_type=F32)
        ms = ms + jnp.dot(lo, mean_ref[...], preferred_element_type=F32)
        return x * lax.rsqrt(ms + EPS) * gain

    def rope(x, cos, sin):
        n = x.shape[1]
        lane = lax.broadcasted_iota(jnp.int32, x.shape, 1)
        first_of_pair = (lane // ROPE_PAIRS) % 2 == 0
        swapped = jnp.where(first_of_pair, pltpu.roll(x, n - ROPE_PAIRS, axis=1), pltpu.roll(x, ROPE_PAIRS, axis=1))
        return x * cos + swapped * sin

    q = head_norm(q_ref[...].astype(F32), gq_ref, qg_ref[...])
    k = head_norm(k_ref[...].astype(F32), gk_ref, kg_ref[...])
    if latent:
        q = rope(q, cosq_ref[...], sinq_ref[...])
        k = rope(k, cosk_ref[...], sink_ref[...])
    else:
        kout_ref[...] = k
    q = q * (HEAD_DIM ** -0.5)
    v = v_ref[...].astype(F32)
    n_q = q.shape[0]
    low_kv = lax.broadcasted_iota(jnp.int32, k.shape, 1) < HEAD_DIM
    low_q = lax.broadcasted_iota(jnp.int32, (n_q, pair_w), 1) < HEAD_DIM
    k_swapped = pltpu.roll(k, HEAD_DIM, axis=1)
    v_swapped = pltpu.roll(v, HEAD_DIM, axis=1)
    nt = (((1,), (1,)), ((), ()))
    for j in range(KV_HEADS):
        kd = (jnp.where(low_kv, k, k_swapped) if j == 0 else jnp.where(low_kv, k_swapped, k)).astype(BF16)
        vd = (jnp.where(low_kv, v, v_swapped) if j == 0 else jnp.where(low_kv, v_swapped, v)).astype(BF16)
        tiles = range(j * Q_PER_KV // 2, (j + 1) * Q_PER_KV // 2)
        parts = []
        for t in tiles:
            qt = q[:, t * pair_w:(t + 1) * pair_w]
            parts += [jnp.where(low_q, qt, 0.0), jnp.where(low_q, 0.0, qt)]
        qs = jnp.concatenate(parts, axis=0).astype(BF16)
        s_new = lax.dot_general(qs, kd, nt, preferred_element_type=F32)
        m = jnp.max(s_new, axis=-1, keepdims=True)
        if latent:
            ckd = jnp.concatenate([ck_ref[j], ck_ref[j]], axis=1).astype(BF16)
            cvd = jnp.concatenate([cv_ref[j], cv_ref[j]], axis=1).astype(BF16)
            s_old = lax.dot_general(qs, ckd, nt, preferred_element_type=F32)
            m = jnp.maximum(m, jnp.max(s_old, axis=-1, keepdims=True))
        p_new = jnp.exp(s_new - m)
        den = jnp.sum(p_new, axis=-1, keepdims=True)
        acc = jnp.dot(p_new.astype(BF16), vd, preferred_element_type=F32)
        if latent:
            p_old = jnp.exp(s_old - m)
            den = den + jnp.sum(p_old, axis=-1, keepdims=True)
            acc = acc + jnp.dot(p_old.astype(BF16), cvd, preferred_element_type=F32)
        out = acc / den
        for i, t in enumerate(tiles):
            lo_head = out[(2 * i) * n_q:(2 * i + 1) * n_q, :]
            hi_head = out[(2 * i + 1) * n_q:(2 * i + 2) * n_q, :]
            o_ref[:, t * pair_w:(t + 1) * pair_w] = jnp.where(low_q, lo_head, hi_head).astype(o_ref.dtype)


def _attn_common_args(qn_g, kn_g):
    q_w, kv_w = Q_HEADS * HEAD_DIM, KV_HEADS * HEAD_DIM
    return (jnp.tile(qn_g, Q_HEADS).reshape(1, q_w), jnp.tile(kn_g, KV_HEADS).reshape(1, kv_w),
            _head_mean_matrix(q_w), _head_mean_matrix(kv_w))


def _attention_prompt(z, qn_g, kn_g):
    L = PROMPT_LEN
    q_w, kv_w = Q_HEADS * HEAD_DIM, KV_HEADS * HEAD_DIM
    q_col = (5 * A_WIDTH) // q_w
    k_col = (5 * A_WIDTH + q_w) // kv_w
    const = lambda r, c: pl.BlockSpec((r, c), lambda s: (0, 0))
    return pl.pallas_call(
        functools.partial(_attn_kernel, latent=False),
        grid=(N_PROMPT_SEQ,),
        in_specs=[
            pl.BlockSpec((L, q_w), lambda s: (s, q_col)),
            pl.BlockSpec((L, kv_w), lambda s: (s, k_col)),
            pl.BlockSpec((L, kv_w), lambda s: (s, k_col + 1)),
            const(1, q_w), const(1, kv_w), const(q_w, q_w), const(kv_w, kv_w),
        ],
        out_specs=(pl.BlockSpec((L, q_w), lambda s: (s, 0)),
                   pl.BlockSpec((L, kv_w), lambda s: (s, 0))),
        out_shape=(jax.ShapeDtypeStruct((N_PROMPT_TOK, q_w), BF16),
                   jax.ShapeDtypeStruct((N_PROMPT_TOK, kv_w), F32)),
        compiler_params=_params("arbitrary"),
        name="attn_prompt",
    )(z, z, z, *_attn_common_args(qn_g, kn_g))


def _attention_latent(z, qn_g, kn_g, cache_k, cache_v):
    L = LATENT_LEN
    nqb = L // Q_BLOCK
    q_w, kv_w = Q_HEADS * HEAD_DIM, KV_HEADS * HEAD_DIM
    q_col = (5 * A_WIDTH) // q_w
    k_col = (5 * A_WIDTH + q_w) // kv_w
    qrow0 = N_PROMPT_TOK // Q_BLOCK
    krow0 = N_PROMPT_TOK // L
    cos, sin = _rope_tables()
    cos_q, sin_q = jnp.asarray(np.tile(cos, (1, Q_HEADS))), jnp.asarray(np.tile(sin, (1, Q_HEADS)))
    cos_k, sin_k = jnp.asarray(np.tile(cos, (1, KV_HEADS))), jnp.asarray(np.tile(sin, (1, KV_HEADS)))
    const = lambda r, c: pl.BlockSpec((r, c), lambda s, b: (0, 0))
    cache_spec = pl.BlockSpec((None, None, KV_HEADS, PAST_LEN, HEAD_DIM), lambda s, b: (s, 0, 0, 0, 0))
    return pl.pallas_call(
        functools.partial(_attn_kernel, latent=True),
        grid=(N_LATENT_SEQ, nqb),
        in_specs=[
            pl.BlockSpec((Q_BLOCK, q_w), lambda s, b: (qrow0 + s * nqb + b, q_col)),
            pl.BlockSpec((L, kv_w), lambda s, b: (krow0 + s, k_col)),
            pl.BlockSpec((L, kv_w), lambda s, b: (krow0 + s, k_col + 1)),
            const(1, q_w), const(1, kv_w), const(q_w, q_w), const(kv_w, kv_w),
            pl.BlockSpec((Q_BLOCK, q_w), lambda s, b: (b, 0)),
            pl.BlockSpec((Q_BLOCK, q_w), lambda s, b: (b, 0)),
            const(L, kv_w), const(L, kv_w),
            cache_spec, cache_spec,
        ],
        out_specs=pl.BlockSpec((Q_BLOCK, q_w), lambda s, b: (s * nqb + b, 0)),
        out_shape=jax.ShapeDtypeStruct((N_LATENT_TOK, q_w), BF16),
        compiler_params=_params("arbitrary", "arbitrary"),
        name="attn_latent",
    )(z, z, z, *_attn_common_args(qn_g, kn_g), cos_q, sin_q, cos_k, sin_k, cache_k, cache_v)


def _out_proj_kernel(*refs, n_act, n_x):
    a_refs = refs[:2 * n_act]
    x_refs = refs[2 * n_act:2 * n_act + n_x]
    g_ref, mod_ref, rw_ref, w_ref, xo_ref, h_ref, lg_ref, wb_ref, rws_ref = refs[2 * n_act + n_x:]
    _cast_once(w_ref, wb_ref)

    @pl.when(pl.program_id(0) == 0)
    def _():
        rw = rw_ref[...]
        hi = rw.astype(BF16).astype(F32)
        lo = (rw - hi).astype(BF16).astype(F32)
        rws_ref[...] = (hi + pltpu.roll(lo, N_EXPERTS, axis=1)).astype(BF16)

    acc = None
    k0 = 0
    for ap_ref, al_ref in zip(a_refs[0::2], a_refs[1::2]):
        k1 = k0 + ap_ref.shape[1]
        part = jnp.dot(_select_trunk(ap_ref, al_ref), wb_ref[k0:k1, :], preferred_element_type=F32)
        acc = part if acc is None else acc + part
        k0 = k1
    mod = mod_ref[...]
    x_in = x_refs[0][...] if n_x == 1 else _select_trunk(*x_refs)
    x = x_in + mod[2:3, :] * acc
    xo_ref[...] = x
    h = _modulated_norm(x, g_ref[...], mod, 3, 4)
    h_ref[...] = _pack_rows(h)
    h_hi = h.astype(BF16)
    h_lo = (h - h_hi.astype(F32)).astype(BF16)
    n = h.shape[0]
    both = jnp.dot(jnp.concatenate([h_hi, h_lo], axis=0), rws_ref[...], preferred_element_type=F32)
    from_hi, from_lo = both[:n], both[n:]
    lg = from_hi + pltpu.roll(from_hi, ROUTER_LANES - N_EXPERTS, axis=1) + from_lo
    lg_ref[...] = lg.T[:N_EXPERTS, :]


def _out_proj(acts, w, xs, g, mod_l, router_wp, block_rows=256):
    tok = lambda width: pl.BlockSpec((block_rows, width), lambda i: (i, 0))
    in_specs = [spec for ap, _ in acts for spec in _trunk_specs(block_rows, ap.shape[1])]
    in_specs += [tok(D_MODEL)] if len(xs) == 1 else list(_trunk_specs(block_rows, D_MODEL))
    in_specs += [_resident((1, D_MODEL)), _mod_spec(block_rows), _resident((D_MODEL, ROUTER_LANES)),
                 _resident(w.shape)]
    return pl.pallas_call(
        functools.partial(_out_proj_kernel, n_act=len(acts), n_x=len(xs)),
        grid=(N_TOK // block_rows,),
        in_specs=in_specs,
        out_specs=(tok(D_MODEL), tok(ROW_WORDS), pl.BlockSpec((N_EXPERTS, block_rows), lambda i: (0, i))),
        out_shape=(jax.ShapeDtypeStruct((N_TOK, D_MODEL), F32),
                   jax.ShapeDtypeStruct((N_TOK, ROW_WORDS), jnp.int32),
                   jax.ShapeDtypeStruct((N_EXPERTS, N_TOK), F32)),
        scratch_shapes=[pltpu.VMEM(w.shape, BF16), pltpu.VMEM((D_MODEL, ROUTER_LANES), BF16)],
        compiler_params=_params("arbitrary"),
        name="out_proj",
    )(*[a for pair in acts for a in pair], *xs, g.reshape(1, D_MODEL), mod_l, router_wp, w)


def _router_kernel(lg_ref, rb_ref, pos_ref, w_ref, plan_ref, rank_ref):
    lg = lg_ref[...]
    ex = jnp.exp(lg - jnp.max(lg, axis=0, keepdims=True))
    scores = ex / jnp.sum(ex, axis=0, keepdims=True)
    biased = scores + rb_ref[...]
    rows = [biased[e:e + 1, :] for e in range(N_EXPERTS)]
    selected = []
    group_score = []
    for gi in range(N_GROUPS):
        r = rows[gi * EXPERTS_PER_GROUP:(gi + 1) * EXPERTS_PER_GROUP]
        total = None
        for i in range(EXPERTS_PER_GROUP):
            rank = None
            for j in range(EXPERTS_PER_GROUP):
                if j == i:
                    continue
                ahead = (r[j] > r[i]) if j > i else (r[j] >= r[i])
                ahead = jnp.where(ahead, 1.0, 0.0)
                rank = ahead if rank is None else rank + ahead
            sel = rank < 1.5
            selected.append(sel)
            contrib = jnp.where(sel, r[i], 0.0)
            total = contrib if total is None else total + contrib
        group_score.append(total)
    best = group_score[0]
    best_group = jnp.zeros_like(best)
    for gi in range(1, N_GROUPS):
        better = group_score[gi] > best
        best_group = jnp.where(better, float(gi), best_group)
        best = jnp.where(better, group_score[gi], best)
    picked = []
    chosen = []
    den = None
    for e in range(N_EXPERTS):
        in_group = best_group == float(e // EXPERTS_PER_GROUP)
        use = jnp.where(selected[e], jnp.where(in_group, 1.0, 0.0), 0.0)
        w = use * scores[e:e + 1, :]
        chosen.append(use)
        picked.append(w)
        den = w if den is None else den + w
    lanes = 128
    n_blk = N_TOK // lanes
    li = lax.broadcasted_iota(jnp.int32, (lanes, lanes), 0)
    lj = lax.broadcasted_iota(jnp.int32, (lanes, lanes), 1)
    prefix = jnp.where(li <= lj, 1.0, 0.0).astype(BF16)
    carry = jnp.zeros((N_EXPERTS, 1), F32)
    for blk in range(n_blk):
        cols = slice(blk * lanes, (blk + 1) * lanes)
        m = jnp.concatenate([chosen[e][:, cols] for e in range(N_EXPERTS)], axis=0)
        incl = jnp.dot(m.astype(BF16), prefix, preferred_element_type=F32)
        rank_ref[:, cols] = incl - m + carry
        carry = carry + incl[:, lanes - 1:lanes]
    count = carry
    padded = jnp.floor((count + float(MOE_TILE - 1)) * (1.0 / MOE_TILE)) * float(MOE_TILE)
    erow = lax.broadcasted_iota(jnp.int32, (N_EXPERTS, 1), 0)
    offset = jnp.zeros((N_EXPERTS, 1), F32)
    for e in range(N_EXPERTS - 1):
        offset = offset + jnp.where(erow > e, padded[e:e + 1, :], 0.0)
    seen = jnp.zeros_like(den)
    pos_a = jnp.zeros_like(den)
    pos_b = jnp.zeros_like(den)
    w_a = jnp.zeros_like(den)
    w_b = jnp.zeros_like(den)
    for e in range(N_EXPERTS):
        pos_e = rank_ref[e:e + 1, :] + offset[e:e + 1, :]
        gate_e = picked[e] / den
        first = jnp.where(seen < 0.5, chosen[e], 0.0) > 0.5
        second = jnp.where(seen > 0.5, chosen[e], 0.0) > 0.5
        pos_a = jnp.where(first, pos_e, pos_a)
        w_a = jnp.where(first, gate_e, w_a)
        pos_b = jnp.where(second, pos_e, pos_b)
        w_b = jnp.where(second, gate_e, w_b)
        seen = seen + chosen[e]
    pos_ref[0:1, :] = pos_a.astype(jnp.int32)
    pos_ref[1:2, :] = pos_b.astype(jnp.int32)
    w_ref[0:1, :] = w_a
    w_ref[1:2, :] = w_b
    start = (lax.broadcasted_iota(jnp.int32, (N_EXPERTS, lanes), 1) * MOE_TILE).astype(F32)
    end = offset + padded
    tile_expert = jnp.sum(jnp.where(end <= start, 1.0, 0.0), axis=0, keepdims=True)
    inside = (offset <= start) & (start < end)
    real = jnp.clip(count - (start - offset), 0.0, float(MOE_TILE))
    tile_rows = jnp.sum(jnp.where(inside, real, 0.0), axis=0, keepdims=True)
    plan_ref[0:1, :] = jnp.minimum(tile_expert, float(N_EXPERTS - 1)).astype(jnp.int32)
    plan_ref[1:2, :] = tile_rows.astype(jnp.int32)


def _router(logits_t, router_b):
    whole = lambda shape: pl.BlockSpec(shape, lambda i: (0, 0))
    return pl.pallas_call(
        _router_kernel,
        grid=(1,),
        in_specs=[whole((N_EXPERTS, N_TOK)), whole((N_EXPERTS, 1))],
        out_specs=(whole((2, N_TOK)), whole((2, N_TOK)), whole((2, 128))),
        out_shape=(jax.ShapeDtypeStruct((2, N_TOK), jnp.int32),
                   jax.ShapeDtypeStruct((2, N_TOK), F32),
                   jax.ShapeDtypeStruct((2, 128), jnp.int32)),
        scratch_shapes=[pltpu.VMEM((N_EXPERTS, N_TOK), F32)],
        compiler_params=_params("arbitrary"),
        name="router",
    )(logits_t, router_b.reshape(N_EXPERTS, 1))


def _sc_mesh():
    return plsc.VectorSubcoreMesh(core_axis_name="c", subcore_axis_name="s")


def _sc_worker_base():
    return (lax.axis_index("s") * SC_CORES + lax.axis_index("c")) * (N_TOK // SC_WORKERS)


def _moe_dispatch(h, pos_a, pos_b):
    n_chunks = N_TOK // SC_WORKERS // SC_CHUNK

    @functools.partial(
        pl.kernel, mesh=_sc_mesh(),
        out_type=jax.ShapeDtypeStruct((MOE_ROWS, ROW_WORDS), jnp.int32),
        scratch_types=[pltpu.VMEM((SC_CHUNK,), jnp.int32), pltpu.VMEM((SC_CHUNK,), jnp.int32),
                       pltpu.VMEM((SC_CHUNK, ROW_WORDS), jnp.int32)],
        name="moe_dispatch",
    )
    def run(h_hbm, pa_hbm, pb_hbm, xs_hbm, ia_v, ib_v, rows_v):
        base = _sc_worker_base()

        @pl.loop(0, n_chunks)
        def _(ci):
            tok = pl.ds(pl.multiple_of(base + ci * SC_CHUNK, SC_CHUNK), SC_CHUNK)
            pltpu.sync_copy(pa_hbm.at[tok], ia_v)
            pltpu.sync_copy(pb_hbm.at[tok], ib_v)
            pltpu.sync_copy(h_hbm.at[tok], rows_v)
            pltpu.sync_copy(rows_v, xs_hbm.at[ia_v])
            pltpu.sync_copy(rows_v, xs_hbm.at[ib_v])

    return run(h, pos_a, pos_b)


def _moe_collect(ys, pos_a, pos_b):
    n_chunks = N_TOK // SC_WORKERS // SC_CHUNK
    out = jax.ShapeDtypeStruct((N_TOK, ROW_WORDS), jnp.int32)

    @functools.partial(
        pl.kernel, mesh=_sc_mesh(), out_type=(out, out),
        scratch_types=[pltpu.VMEM((SC_CHUNK,), jnp.int32), pltpu.VMEM((SC_CHUNK,), jnp.int32),
                       pltpu.VMEM((SC_CHUNK, ROW_WORDS), jnp.int32)],
        name="moe_collect",
    )
    def run(ys_hbm, pa_hbm, pb_hbm, ya_hbm, yb_hbm, ia_v, ib_v, rows_v):
        base = _sc_worker_base()

        @pl.loop(0, n_chunks)
        def _(ci):
            tok = pl.ds(pl.multiple_of(base + ci * SC_CHUNK, SC_CHUNK), SC_CHUNK)
            pltpu.sync_copy(pa_hbm.at[tok], ia_v)
            pltpu.sync_copy(pb_hbm.at[tok], ib_v)
            pltpu.sync_copy(ys_hbm.at[ia_v], rows_v)
            pltpu.sync_copy(rows_v, ya_hbm.at[tok])
            pltpu.sync_copy(ys_hbm.at[ib_v], rows_v)
            pltpu.sync_copy(rows_v, yb_hbm.at[tok])

    return run(ys, pos_a, pos_b)


def _experts_kernel(plan_ref, xs_ref, wg_hbm, wu_hbm, wd_hbm, y_ref,
                    sg_ref, su_ref, sd_ref, wgb_ref, wub_ref, wdb_ref, sems, seg_ref, *, layer):
    j = pl.program_id(0)
    n_tiles = pl.num_programs(0)
    expert = plan_ref[j]
    n_real = plan_ref[PLAN_LANES + j]
    fresh = jnp.logical_or(j == 0, expert != plan_ref[jnp.maximum(j - 1, 0)])

    def weight_copies(e, slot):
        return (pltpu.make_async_copy(wg_hbm.at[layer, e], sg_ref.at[slot], sems.at[slot, 0]),
                pltpu.make_async_copy(wu_hbm.at[layer, e], su_ref.at[slot], sems.at[slot, 1]),
                pltpu.make_async_copy(wd_hbm.at[layer, e], sd_ref.at[slot], sems.at[slot, 2]))

    @pl.when(j == 0)
    def _():
        seg_ref[0] = 0

        @pl.when(n_real > 0)
        def _():
            for cp in weight_copies(expert, 0):
                cp.start()

    @pl.when(jnp.logical_and(n_real > 0, fresh))
    def _():
        slot = seg_ref[0] % 2
        for cp in weight_copies(expert, slot):
            cp.wait()
        wgb_ref[...] = sg_ref[slot].astype(BF16)
        wub_ref[...] = su_ref[slot].astype(BF16)
        wdb_ref[...] = sd_ref[slot].astype(BF16)
        nxt = lax.while_loop(lambda t: jnp.logical_and(t < n_tiles, plan_ref[jnp.minimum(t, n_tiles - 1)] == expert),
                             lambda t: t + 1, j + 1)
        nxt_c = jnp.minimum(nxt, n_tiles - 1)

        @pl.when(jnp.logical_and(nxt < n_tiles, plan_ref[PLAN_LANES + nxt_c] > 0))
        def _():
            for cp in weight_copies(plan_ref[nxt_c], 1 - slot):
                cp.start()

        seg_ref[0] = seg_ref[0] + 1

    @pl.when(n_real > 0)
    def _():
        row = lax.broadcasted_iota(jnp.int32, xs_ref.shape, 0)
        words = jnp.where(row < n_real, xs_ref[...], 0)
        x = _unpack_rows(words).astype(BF16)
        a = jnp.dot(x, wgb_ref[...], preferred_element_type=F32)
        b = jnp.dot(x, wub_ref[...], preferred_element_type=F32)
        hid = (a * jax.nn.sigmoid(a)) * b
        y_ref[...] = _pack_rows(jnp.dot(hid.astype(BF16), wdb_ref[...], preferred_element_type=F32))


def _experts(plan, xs, w_gate, w_up, w_down, layer):
    hbm = pl.BlockSpec(memory_space=pl.ANY)
    return pl.pallas_call(
        functools.partial(_experts_kernel, layer=layer),
        grid_spec=pltpu.PrefetchScalarGridSpec(
            num_scalar_prefetch=1,
            grid=(MOE_ROWS // MOE_TILE,),
            in_specs=[pl.BlockSpec((MOE_TILE, ROW_WORDS), lambda j, plan: (j, 0)), hbm, hbm, hbm],
            out_specs=pl.BlockSpec((MOE_TILE, ROW_WORDS), lambda j, plan: (j, 0)),
            scratch_shapes=[pltpu.VMEM((2, D_MODEL, D_EXPERT), F32), pltpu.VMEM((2, D_MODEL, D_EXPERT), F32),
                            pltpu.VMEM((2, D_EXPERT, D_MODEL), F32),
                            pltpu.VMEM((D_MODEL, D_EXPERT), BF16), pltpu.VMEM((D_MODEL, D_EXPERT), BF16),
                            pltpu.VMEM((D_EXPERT, D_MODEL), BF16),
                            pltpu.SemaphoreType.DMA((2, 3)), pltpu.SMEM((1,), jnp.int32)],
        ),
        out_shape=jax.ShapeDtypeStruct((MOE_ROWS, ROW_WORDS), jnp.int32),
        compiler_params=_params("arbitrary"),
        name="experts",
    )(plan, xs, w_gate, w_up, w_down)


def _combine_kernel(x_ref, ya_ref, yb_ref, wt_ref, mod_ref, o_ref):
    o_ref[...] = _moe_mix(x_ref, ya_ref, yb_ref, wt_ref, mod_ref)


def _combine(x, moe_out, mod_l, tok0, n_tok, block_rows=512):
    ya, yb, w_tok = moe_out
    b0 = tok0 // block_rows
    rows = lambda width: pl.BlockSpec((block_rows, width), lambda i: (b0 + i, 0))
    return pl.pallas_call(
        _combine_kernel,
        grid=(n_tok // block_rows,),
        in_specs=[rows(D_MODEL), rows(ROW_WORDS), rows(ROW_WORDS), rows(TOP_K),
                  pl.BlockSpec((None, 6, D_MODEL), lambda i: (_cond_of_token_block(b0 + i, block_rows), 0, 0))],
        out_specs=pl.BlockSpec((block_rows, D_MODEL), lambda i: (i, 0)),
        out_shape=jax.ShapeDtypeStruct((n_tok, D_MODEL), F32),
        compiler_params=_params("arbitrary"),
        name="combine",
    )(x, ya, yb, w_tok, mod_l)


def _moe(h, logits_t, router_b, w_gate, w_up, w_down, layer):
    pos, w, plan = _router(logits_t, router_b)
    xs = _moe_dispatch(h, pos[0], pos[1])
    ys = _experts(plan.reshape(-1), xs, w_gate, w_up, w_down, layer)
    ya, yb = _moe_collect(ys, pos[0], pos[1])
    return ya, yb, w.T


def _dft_tables(L):
    k = np.arange(L)[:, None]
    m = np.arange(L)[None, :]
    r = (k * m) % (2 * L)
    ang = np.pi * r.astype(np.float64) / L
    fc = np.cos(ang)
    fs = np.sin(ang)
    fs[0, :] = np.where(np.arange(L) % 2 == 0, 1.0, -1.0)
    wk = np.full((L, 1), 1.0 / L)
    wk[0, 0] = 0.5 / L
    gc = (fc * wk).T
    gs = (fs * wk).T
    return [jnp.asarray(t.astype(np.float32)).astype(BF16) for t in (fc, fs, gc, gs)]


def _filter_consts(L):
    t = np.linspace(0.0, 1.0, L, dtype=np.float32)[:, None]
    w = (np.float32(2.0 * np.pi) * np.arange(L, dtype=np.float32)[:, None] / np.float32(L)).astype(np.float32)
    fb = np.linspace(1e-4, HY_BANDS - 1, HY_BANDS, dtype=np.float32)[None, :]
    emb = np.concatenate([t, np.cos(fb * w), -np.sin(fb * w)], axis=-1).astype(np.float32)
    lo = math.log(HY_DECAY_TARGET) / HY_SLOW_PCT
    hi = math.log(HY_DECAY_TARGET) / HY_FAST_PCT
    deltas = np.abs(np.linspace(lo, hi, D_MODEL, dtype=np.float32))
    decay = np.exp(-t * deltas).astype(np.float32)
    return jnp.asarray(emb), jnp.asarray(decay)


def _filter_kernel(emb_ref, w1_ref, b1_ref, w2_ref, b2_ref, fr_ref, w3f_ref, w3b_ref, dec_ref,
                   fc_ref, fs_ref, kr_ref, q_ref, krn_ref, hd_ref):
    @pl.when(pl.program_id(0) == 0)
    def _():
        fr = fr_ref[...]
        h1 = jnp.sin(fr * (jnp.dot(emb_ref[...], w1_ref[...], precision=HIGHEST,
                                   preferred_element_type=F32) + b1_ref[...]))
        hd_ref[...] = jnp.sin(fr * (jnp.dot(h1, w2_ref[...], precision=HIGHEST,
                                            preferred_element_type=F32) + b2_ref[...]))

    hd = hd_ref[...]
    dec = dec_ref[...]
    f = jnp.dot(hd, w3f_ref[...], precision=HIGHEST, preferred_element_type=F32) * dec
    g = jnp.dot(hd, w3b_ref[...], precision=HIGHEST, preferred_element_type=F32) * dec
    row = lax.broadcasted_iota(jnp.int32, f.shape, 0)
    g = jnp.where(row == 0, 0.0, g)
    s = f + g
    d = f - g
    kr = jnp.dot(fc_ref[...], s.astype(BF16), preferred_element_type=F32)
    qq = jnp.dot(fs_ref[...], d.astype(BF16), preferred_element_type=F32)
    alt = jnp.where(row % 2 == 0, 1.0, -1.0)
    nyq = jnp.sum(alt * s, axis=0, keepdims=True)
    kr_ref[...] = kr
    q_ref[...] = jnp.where(row == 0, 0.0, qq)
    krn_ref[...] = jnp.where(row == 0, nyq, kr)


def _hyena_filter_spectrum(L, w1, b1, w2, b2, w3, freq, fc, fs, cblk=256):
    emb, decay = _filter_consts(L)
    ncb = D_MODEL // cblk
    n_emb = 128
    emb = jnp.pad(emb, ((0, 0), (0, n_emb - emb.shape[1])))
    w1 = jnp.pad(w1, ((0, n_emb - w1.shape[0]), (0, 0)))
    full = lambda shape: pl.BlockSpec(shape, lambda j: tuple(0 for _ in shape))
    out_sds = jax.ShapeDtypeStruct((L, D_MODEL), F32)
    out_spec = pl.BlockSpec((L, cblk), lambda j: (0, j))
    return pl.pallas_call(
        _filter_kernel,
        grid=(ncb,),
        in_specs=[
            full((L, n_emb)), full((n_emb, HY_FFN)), full((1, HY_FFN)), full((HY_FFN, HY_FFN)),
            full((1, HY_FFN)), full((1, HY_FFN)),
            pl.BlockSpec((HY_FFN, cblk), lambda j: (0, j)),
            pl.BlockSpec((HY_FFN, cblk), lambda j: (0, ncb + j)),
            pl.BlockSpec((L, cblk), lambda j: (0, j)),
            full((L, L)), full((L, L)),
        ],
        out_specs=(out_spec, out_spec, out_spec),
        out_shape=(out_sds, out_sds, out_sds),
        scratch_shapes=[pltpu.VMEM((L, HY_FFN), F32)],
        compiler_params=_params("arbitrary"),
        name=f"hyena_filter_{L}",
    )(emb, w1, b1.reshape(1, HY_FFN), w2, b2.reshape(1, HY_FFN), freq.reshape(1, HY_FFN), w3, w3, decay, fc, fs)


def _hyena_conv_kernel(x0_ref, x1_ref, v_ref, cw0_ref, cw1_ref, cwv_ref, cb0_ref, cb1_ref, cbv_ref,
                       kr_ref, q_ref, krn_ref, ds_ref, fc_ref, fs_ref, gc_ref, gs_ref, o_ref):
    L = x0_ref.shape[0]
    row = lax.broadcasted_iota(jnp.int32, x0_ref.shape, 0)

    def short_conv(u_ref, w_ref, b_ref):
        u = u_ref[...].astype(F32)
        w = w_ref[...]
        prev = jnp.where(row == 0, 0.0, pltpu.roll(u, 1, axis=0))
        nxt = jnp.where(row == L - 1, 0.0, pltpu.roll(u, L - 1, axis=0))
        return prev * w[0:1, :] + u * w[1:2, :] + nxt * w[2:3, :] + b_ref[...]

    x0 = short_conv(x0_ref, cw0_ref, cb0_ref)
    x1 = short_conv(x1_ref, cw1_ref, cb1_ref)
    v = short_conv(v_ref, cwv_ref, cbv_ref)
    zz = v * x1
    zb = zz.astype(BF16)
    ur = jnp.dot(fc_ref[...], zb, preferred_element_type=F32)
    p = jnp.dot(fs_ref[...], zb, preferred_element_type=F32)
    qq = q_ref[...]
    yr = ur * kr_ref[...] - p * qq
    yw = ur * qq + p * krn_ref[...]
    y = jnp.dot(gc_ref[...], yr.astype(BF16), preferred_element_type=F32)
    y = y + jnp.dot(gs_ref[...], yw.astype(BF16), preferred_element_type=F32)
    o_ref[...] = (x0 * (y + zz * ds_ref[...])).astype(o_ref.dtype)


def _hyena_conv(u, conv_w, conv_b, dskip, spectrum, tables, *, latent):
    L = LATENT_LEN if latent else PROMPT_LEN
    n_seq = N_LATENT_SEQ if latent else N_PROMPT_SEQ
    cblk = 256 if latent else 512
    ncb = D_MODEL // cblk
    row0 = (N_PROMPT_TOK // L) if latent else 0
    kr, qq, krn = spectrum
    fc, fs, gc, gs = tables

    def part(p, rows):
        return pl.BlockSpec((rows, cblk), lambda j, s: (0 if rows != L else row0 + s, p * ncb + j))

    def const_cols(rows):
        return pl.BlockSpec((rows, cblk), lambda j, s: (0, j))

    mat = pl.BlockSpec((L, L), lambda j, s: (0, 0))
    conv_b2 = conv_b.reshape(1, 3 * D_MODEL)
    in_specs = [part(0, L), part(1, L), part(2, L),
                part(0, 3), part(1, 3), part(2, 3),
                part(0, 1), part(1, 1), part(2, 1),
                const_cols(L), const_cols(L), const_cols(L), const_cols(1),
                mat, mat, mat, mat]
    args = [u, u, u, conv_w, conv_w, conv_w, conv_b2, conv_b2, conv_b2,
            kr, qq, krn, dskip.reshape(1, D_MODEL), fc, fs, gc, gs]
    return pl.pallas_call(
        _hyena_conv_kernel,
        grid=(ncb, n_seq),
        in_specs=in_specs,
        out_specs=pl.BlockSpec((L, cblk), lambda j, s: (s, j)),
        out_shape=jax.ShapeDtypeStruct((n_seq * L, D_MODEL), BF16),
        compiler_params=_params("arbitrary", "arbitrary"),
        name="hyena_conv_latent" if latent else "hyena_conv_prompt",
    )(*args)


def kernel(x_prompt, x_sample, cache_k, cache_v, state_hgrn, c, c_ctx, norm_g, mod_w, mod_b, ab_in_w, hgrn_lb, hgrn_onorm_g, attn_qnorm_g, attn_knorm_g, ab_out_w, hy_in_w, hy_in_b, hy_conv_w, hy_conv_b, hy_f_w1, hy_f_b1, hy_f_w2, hy_f_b2, hy_f_w3, hy_f_freq, hy_dskip, hy_out_w, router_w, router_b, moe_w_gate, moe_w_up, moe_w_down):
    xp = x_prompt.reshape(N_PROMPT_TOK, D_MODEL)
    xl = x_sample.reshape(N_LATENT_TOK, D_MODEL)
    cond = jnp.concatenate([c_ctx[None, :], c, jnp.zeros((N_COND - 1 - N_LATENT_SEQ, D_MODEL), F32)], axis=0)
    mod = _modulation(cond, mod_w, mod_b)
    router_wp = jnp.pad(router_w, ((0, 0), (0, ROUTER_LANES - N_EXPERTS)))

    z = _in_proj0(xp, xl, norm_g[0, 0], mod[0], ab_in_w[0])
    oa_p, new_state = _hgrn(z, hgrn_lb, hgrn_onorm_g[0], None, latent=False)
    oa_l = _hgrn(z, hgrn_lb, hgrn_onorm_g[0], state_hgrn, latent=True)
    ob_p, k_prompt = _attention_prompt(z, attn_qnorm_g[0], attn_knorm_g[0])
    ob_l = _attention_latent(z, attn_qnorm_g[0], attn_knorm_g[0], cache_k, cache_v)
    x, h, logits_t = _out_proj([(oa_p, oa_l), (ob_p, ob_l)], ab_out_w[0], (xp, xl), norm_g[0, 1], mod[0],
                               router_wp)
    moe_out = _moe(h, logits_t, router_b, moe_w_gate, moe_w_up, moe_w_down, 0)

    x, u = _in_proj1(x, moe_out, mod[0], norm_g[1, 0], mod[1], hy_in_w[0], hy_in_b[0])
    pre = []
    for latent in (False, True):
        L = LATENT_LEN if latent else PROMPT_LEN
        tables = _dft_tables(L)
        spectrum = _hyena_filter_spectrum(L, hy_f_w1[0], hy_f_b1[0], hy_f_w2[0], hy_f_b2[0], hy_f_w3[0],
                                          hy_f_freq[0], tables[0], tables[1])
        pre.append(_hyena_conv(u, hy_conv_w[0], hy_conv_b[0], hy_dskip[0], spectrum, tables, latent=latent))
    x, h, logits_t = _out_proj([tuple(pre)], hy_out_w[0], (x,), norm_g[1, 1], mod[1], router_wp)
    moe_out = _moe(h, logits_t, router_b, moe_w_gate, moe_w_up, moe_w_down, 1)

    y_prompt = _combine(x, moe_out, mod[1], 0, N_PROMPT_TOK).reshape(N_PROMPT_SEQ, PROMPT_LEN, D_MODEL)
    y_sample = _combine(x, moe_out, mod[1], N_PROMPT_TOK, N_LATENT_TOK).reshape(N_LATENT_SEQ, LATENT_LEN, D_MODEL)
    kv_shape = (N_PROMPT_SEQ, PROMPT_LEN, KV_HEADS, HEAD_DIM)
    new_k = k_prompt.reshape(kv_shape).transpose(0, 2, 1, 3)[:, None]
    v_col = 5 * A_WIDTH + (Q_HEADS + KV_HEADS) * HEAD_DIM
    new_v = z[:N_PROMPT_TOK, v_col:].astype(F32).reshape(kv_shape).transpose(0, 2, 1, 3)[:, None]
    return (y_prompt, y_sample, new_k, new_v, new_state)
```

```python
import functools
import math

import numpy as np
import jax
import jax.numpy as jnp
from jax import lax
from jax.experimental import pallas as pl
from jax.experimental.pallas import tpu as pltpu
from jax.experimental.pallas import tpu_sc as plsc

F32 = jnp.float32
BF16 = jnp.bfloat16
HIGHEST = lax.Precision.HIGHEST

D_MODEL = 1024
N_PROMPT_SEQ = 32
PROMPT_LEN = 256
N_LATENT_SEQ = 2
LATENT_LEN = 1024
PAST_LEN = 512
GRID_W = 64
N_PROMPT_TOK = N_PROMPT_SEQ * PROMPT_LEN
N_LATENT_TOK = N_LATENT_SEQ * LATENT_LEN
N_TOK = N_PROMPT_TOK + N_LATENT_TOK
N_COND = 8
EPS = 1e-6

A_WIDTH = 512
A_HEADS = 4
A_DK = 128
CHUNK = 64
HGRN_BLOCK = 256
HGRN_HEADS_PER_STEP = 4
HEAD_DIM = 64
Q_HEADS = 8
KV_HEADS = 2
Q_PER_KV = Q_HEADS // KV_HEADS
Q_BLOCK = 256
ROPE_THETA = 10000.0
ROPE_PAIRS = HEAD_DIM // 4
AB_IN = 5 * A_WIDTH + (Q_HEADS + 2 * KV_HEADS) * HEAD_DIM

HY_BANDS = 16
HY_FFN = 64
HY_DECAY_TARGET = 1e-2
HY_FAST_PCT = 0.3
HY_SLOW_PCT = 1.5

N_EXPERTS = 16
N_GROUPS = 4
EXPERTS_PER_GROUP = 4
TOP_K = 2
D_EXPERT = 512
ROUTER_LANES = 128
OUT_PROJ_SUB_ROWS = 256
MOE_TILE = 512
MOE_ROWS = N_TOK * TOP_K + N_EXPERTS * MOE_TILE
PLAN_LANES = 128

SC_CORES = 2
SC_WORKERS = 32
SC_CHUNK = 80
ROW_WORDS = D_MODEL // 2

VMEM_LIMIT = 56 * 1024 * 1024


def _params(*sem):
    return pltpu.CompilerParams(dimension_semantics=sem, vmem_limit_bytes=VMEM_LIMIT)


def _pack_rows(x):
    n = x.shape[1] // 2
    bits = pltpu.bitcast(x.astype(BF16).astype(F32), jnp.uint32)
    return pltpu.bitcast(bits[:, :n] | (bits[:, n:] >> 16), jnp.int32)


def _unpack_rows(p):
    bits = pltpu.bitcast(p, jnp.uint32)
    hi = pltpu.bitcast(bits & jnp.uint32(0xFFFF0000), F32)
    lo = pltpu.bitcast(bits << 16, F32)
    return jnp.concatenate([hi, lo], axis=1)


def _cond_of_token_block(i, block_rows):
    start = i * block_rows
    return jnp.where(start < N_PROMPT_TOK, 0, 1 + (start - N_PROMPT_TOK) // LATENT_LEN)


def _mod_kernel(cond_ref, w_ref, b_ref, o_ref):
    cnd = cond_ref[...]
    s = cnd * jax.nn.sigmoid(cnd)
    s_hi = s.astype(BF16)
    s_lo = (s - s_hi.astype(F32)).astype(BF16)
    w = w_ref[...]
    w_hi = w.astype(BF16)
    w_lo = (w - w_hi.astype(F32)).astype(BF16)
    acc = jnp.dot(s_hi, w_hi, preferred_element_type=F32)
    acc = acc + jnp.dot(s_lo, w_hi, preferred_element_type=F32)
    acc = acc + jnp.dot(s_hi, w_lo, preferred_element_type=F32)
    o_ref[...] = acc + b_ref[...]


def _modulation(cond, mod_w, mod_b):
    depth = mod_w.shape[0]
    n_chunk = 6
    out = pl.pallas_call(
        _mod_kernel,
        grid=(depth, n_chunk),
        in_specs=[
            pl.BlockSpec((N_COND, D_MODEL), lambda l, j: (0, 0)),
            pl.BlockSpec((None, D_MODEL, D_MODEL), lambda l, j: (l, 0, j)),
            pl.BlockSpec((None, 1, D_MODEL), lambda l, j: (l, 0, j)),
        ],
        out_specs=pl.BlockSpec((None, N_COND, D_MODEL), lambda l, j: (l, 0, j)),
        out_shape=jax.ShapeDtypeStruct((depth, N_COND, n_chunk * D_MODEL), F32),
        compiler_params=_params("arbitrary", "arbitrary"),
        name="modulation",
    )(cond, mod_w, mod_b.reshape(depth, 1, n_chunk * D_MODEL))
    return out.reshape(depth, N_COND, n_chunk, D_MODEL)


def _modulated_norm(x, g, mod, shift_row, scale_row):
    ms = jnp.mean(x * x, axis=-1, keepdims=True)
    y = x * lax.rsqrt(ms + EPS) * g
    return y * (1.0 + mod[scale_row:scale_row + 1, :]) + mod[shift_row:shift_row + 1, :]


def _trunk_specs(block_rows, width):
    n_prompt_blocks = N_PROMPT_TOK // block_rows
    return (pl.BlockSpec((block_rows, width), lambda i: (jnp.minimum(i, n_prompt_blocks - 1), 0)),
            pl.BlockSpec((block_rows, width), lambda i: (jnp.maximum(i - n_prompt_blocks, 0), 0)))


def _select_trunk(p_ref, l_ref, rows=slice(None)):
    block_rows = p_ref.shape[0]
    return jnp.where(pl.program_id(0) < N_PROMPT_TOK // block_rows, p_ref[rows, :], l_ref[rows, :])


def _cast_once(w_ref, wb_ref):
    @pl.when(pl.program_id(0) == 0)
    def _():
        wb_ref[...] = w_ref[...].astype(BF16)


def _resident(shape):
    return pl.BlockSpec(shape, lambda i: tuple(0 for _ in shape), pipeline_mode=pl.Buffered(1))


def _mod_spec(block_rows):
    return pl.BlockSpec((None, 6, D_MODEL), lambda i: (_cond_of_token_block(i, block_rows), 0, 0))


def _in_proj0_kernel(xp_ref, xl_ref, g_ref, mod_ref, w_ref, o_ref, wb_ref):
    _cast_once(w_ref, wb_ref)
    h = _modulated_norm(_select_trunk(xp_ref, xl_ref), g_ref[...], mod_ref[...], 0, 1)
    o_ref[...] = jnp.dot(h.astype(BF16), wb_ref[...], preferred_element_type=F32).astype(o_ref.dtype)


def _in_proj0(x_prompt, x_latent, g, mod_l, w, block_rows=512):
    n = w.shape[1]
    return pl.pallas_call(
        _in_proj0_kernel,
        grid=(N_TOK // block_rows,),
        in_specs=[*_trunk_specs(block_rows, D_MODEL), _resident((1, D_MODEL)), _mod_spec(block_rows),
                  _resident((D_MODEL, n))],
        out_specs=pl.BlockSpec((block_rows, n), lambda i: (i, 0)),
        out_shape=jax.ShapeDtypeStruct((N_TOK, n), BF16),
        scratch_shapes=[pltpu.VMEM((D_MODEL, n), BF16)],
        compiler_params=_params("arbitrary"),
        name="in_proj0",
    )(x_prompt, x_latent, g.reshape(1, D_MODEL), mod_l, w)


def _moe_mix(x_ref, ya_ref, yb_ref, wt_ref, mod_ref):
    wt = wt_ref[...]
    mix = wt[:, 0:1] * _unpack_rows(ya_ref[...]) + wt[:, 1:2] * _unpack_rows(yb_ref[...])
    return x_ref[...] + mod_ref[5:6, :] * mix


def _in_proj1_kernel(x_ref, ya_ref, yb_ref, wt_ref, modp_ref, g_ref, mod_ref, w_ref, b_ref, xo_ref, o_ref, wb_ref):
    _cast_once(w_ref, wb_ref)
    x = _moe_mix(x_ref, ya_ref, yb_ref, wt_ref, modp_ref)
    xo_ref[...] = x
    h = _modulated_norm(x, g_ref[...], mod_ref[...], 0, 1)
    u = jnp.dot(h.astype(BF16), wb_ref[...], preferred_element_type=F32) + b_ref[...]
    o_ref[...] = u.astype(o_ref.dtype)


def _in_proj1(x, moe_out, mod_prev, g, mod_l, w, bias, block_rows=512):
    ya, yb, w_tok = moe_out
    n = w.shape[1]
    tok = pl.BlockSpec((block_rows, D_MODEL), lambda i: (i, 0))
    packed = pl.BlockSpec((block_rows, ROW_WORDS), lambda i: (i, 0))
    return pl.pallas_call(
        _in_proj1_kernel,
        grid=(N_TOK // block_rows,),
        in_specs=[tok, packed, packed, pl.BlockSpec((block_rows, TOP_K), lambda i: (i, 0)), _mod_spec(block_rows),
                  _resident((1, D_MODEL)), _mod_spec(block_rows), _resident((D_MODEL, n)), _resident((1, n))],
        out_specs=(tok, pl.BlockSpec((block_rows, n), lambda i: (i, 0))),
        out_shape=(jax.ShapeDtypeStruct((N_TOK, D_MODEL), F32), jax.ShapeDtypeStruct((N_TOK, n), BF16)),
        scratch_shapes=[pltpu.VMEM((D_MODEL, n), BF16)],
        compiler_params=_params("arbitrary"),
        name="in_proj1",
    )(x, ya, yb, w_tok, mod_prev, g.reshape(1, D_MODEL), mod_l, w, bias.reshape(1, n))


def _hgrn_kernel(*refs, seq_len, with_state):
    if with_state:
        (q_ref, zf_ref, zb_ref, i_ref, ga_ref, lb_ref, og_ref, s0_ref, o_ref, of_ref, ob_ref) = refs
    else:
        (q_ref, zf_ref, zb_ref, i_ref, ga_ref, lb_ref, og_ref, o_ref, s_ref, of_ref, ob_ref) = refs
    n_blocks = seq_len // HGRN_BLOCK
    chunks_per_block = HGRN_BLOCK // CHUNK

    lbr = lb_ref[...]
    mx = jnp.maximum(lbr[0], lbr[1])
    e0 = jnp.exp(lbr[0] - mx)
    e1 = jnp.exp(lbr[1] - mx)
    lb = e0 / (e0 + e1)

    row = lax.broadcasted_iota(jnp.int32, (HGRN_BLOCK, HGRN_BLOCK), 0)
    col = lax.broadcasted_iota(jnp.int32, (HGRN_BLOCK, HGRN_BLOCK), 1)
    same_chunk = (row // CHUNK) == (col // CHUNK)
    nt = (((1,), (1,)), ((), ()))
    tn = (((0,), (0,)), ((), ()))

    def per_chunk_row(x, idx):
        return jnp.concatenate(
            [jnp.broadcast_to(x[n * CHUNK + idx:n * CHUNK + idx + 1, :], (CHUNK, x.shape[1]))
             for n in range(chunks_per_block)], axis=0)

    def in_chunk_cumsum(tri, x):
        hi = x.astype(BF16)
        lo = (x - hi.astype(F32)).astype(BF16)
        return jnp.dot(tri, hi, preferred_element_type=F32) + jnp.dot(tri, lo, preferred_element_type=F32)

    def block(blk, cols, st, z_ref, lbd, forward, out_ref):
        rows = slice(blk * HGRN_BLOCK, (blk + 1) * HGRN_BLOCK)
        keep = (same_chunk & (col <= row)) if forward else (same_chunk & (col >= row))
        tri = jnp.where(keep, 1.0, 0.0).astype(BF16)
        mid = CHUNK // 2 if forward else CHUNK - 1 - CHUNK // 2
        last = CHUNK - 1 if forward else 0
        f = lbd + (1.0 - lbd) * jax.nn.sigmoid(z_ref[rows, cols].astype(F32))
        lf = jnp.log(f)
        k = 1.0 - f
        q = q_ref[rows, cols].astype(F32)
        vb = i_ref[rows, cols].astype(BF16)
        b = in_chunk_cumsum(tri, lf)
        bm = per_chunk_row(b, mid)
        bl = per_chunk_row(b, last)
        qe = (q * jnp.exp(b - bm)).astype(BF16)
        ke = (k * jnp.exp(bm - b)).astype(BF16)
        att = lax.dot_general(qe, ke, nt, preferred_element_type=F32)
        att = jnp.where(keep, att, 0.0)
        o_intra = jnp.dot(att.astype(BF16), vb, preferred_element_type=F32)
        qb = (q * jnp.exp(b)).astype(BF16)
        ks = (k * jnp.exp(bl - b)).astype(BF16)
        decay = jnp.exp(bl)
        order = range(chunks_per_block) if forward else range(chunks_per_block - 1, -1, -1)
        o_inter = [None] * chunks_per_block
        for n in order:
            cr = slice(n * CHUNK, (n + 1) * CHUNK)
            o_inter[n] = lax.dot_general(qb[cr], st.astype(BF16), nt, preferred_element_type=F32)
            upd = lax.dot_general(vb[cr], ks[cr], tn, preferred_element_type=F32)
            st = st * decay[n * CHUNK:n * CHUNK + 1, :] + upd
        out_ref[rows, cols] = o_intra + jnp.concatenate(o_inter, axis=0)
        return st

    for hd in range(q_ref.shape[1] // A_DK):
        cols = slice(hd * A_DK, (hd + 1) * A_DK)
        if with_state:
            st_f, st_b = s0_ref[0, hd].T, s0_ref[1, hd].T
        else:
            st_f, st_b = jnp.zeros((A_DK, A_DK), F32), jnp.zeros((A_DK, A_DK), F32)
        for step in range(n_blocks):
            st_f = block(step, cols, st_f, zf_ref, lb[0:1, cols], True, of_ref)
            st_b = block(n_blocks - 1 - step, cols, st_b, zb_ref, lb[1:2, cols], False, ob_ref)
        if not with_state:
            s_ref[0, hd] = st_f.T
            s_ref[1, hd] = st_b.T
        o = of_ref[:, cols] + ob_ref[:, cols]
        o = o * lax.rsqrt(jnp.mean(o * o, axis=-1, keepdims=True) + EPS) * og_ref[:, cols]
        ga = ga_ref[:, cols].astype(F32)
        o_ref[:, cols] = (o * (ga * jax.nn.sigmoid(ga))).astype(o_ref.dtype)


def _hgrn(z, hgrn_lb, onorm_g, state, *, latent):
    seq_len = LATENT_LEN if latent else PROMPT_LEN
    n_seq = N_LATENT_SEQ if latent else N_PROMPT_SEQ
    row0 = (N_PROMPT_TOK // seq_len) if latent else 0

    hw = HGRN_HEADS_PER_STEP * A_DK
    n_hg = A_HEADS // HGRN_HEADS_PER_STEP

    def zspec(part):
        return pl.BlockSpec((seq_len, hw), lambda s, h: (row0 + s, part * n_hg + h))

    in_specs = [zspec(0), zspec(1), zspec(2), zspec(3), zspec(4),
                pl.BlockSpec((2, 2, hw), lambda s, h: (0, 0, h)),
                pl.BlockSpec((1, hw), lambda s, h: (0, h))]
    args = [z, z, z, z, z, hgrn_lb, onorm_g.reshape(1, A_WIDTH)]
    state_spec = pl.BlockSpec((None, None, 2, HGRN_HEADS_PER_STEP, A_DK, A_DK), lambda s, h: (s, 0, 0, h, 0, 0))
    o_shape = jax.ShapeDtypeStruct((n_seq * seq_len, A_WIDTH), BF16)
    o_spec = pl.BlockSpec((seq_len, hw), lambda s, h: (s, h))
    if latent:
        in_specs.append(state_spec)
        args.append(state)
        out_shape, out_specs = o_shape, o_spec
    else:
        out_shape = (o_shape, jax.ShapeDtypeStruct((n_seq, 1, 2, A_HEADS, A_DK, A_DK), F32))
        out_specs = (o_spec, state_spec)
    return pl.pallas_call(
        functools.partial(_hgrn_kernel, seq_len=seq_len, with_state=latent),
        grid=(n_seq, n_hg),
        in_specs=in_specs,
        out_specs=out_specs,
        out_shape=out_shape,
        scratch_shapes=[pltpu.VMEM((seq_len, hw), F32), pltpu.VMEM((seq_len, hw), F32)],
        compiler_params=_params("arbitrary", "arbitrary"),
        name="hgrn_latent" if latent else "hgrn_prompt",
    )(*args)


def _rope_tables():
    pos = np.arange(LATENT_LEN)
    row, colp = pos // GRID_W, pos % GRID_W
    inv = ROPE_THETA ** (-np.arange(ROPE_PAIRS, dtype=np.float32) / ROPE_PAIRS)
    inv = inv.astype(np.float32)
    ang_r = (row.astype(np.float32)[:, None] * inv).astype(np.float32)
    ang_c = (colp.astype(np.float32)[:, None] * inv).astype(np.float32)
    cos = np.concatenate([np.cos(ang_r), np.cos(ang_r), np.cos(ang_c), np.cos(ang_c)], axis=1)
    sin = np.concatenate([-np.sin(ang_r), np.sin(ang_r), -np.sin(ang_c), np.sin(ang_c)], axis=1)
    return cos.astype(np.float32), sin.astype(np.float32)


def _head_mean_matrix(width):
    idx = np.arange(width) // HEAD_DIM
    return jnp.asarray((idx[:, None] == idx[None, :]).astype(np.float32) / HEAD_DIM).astype(BF16)


def _attn_kernel(*refs, latent):
    if latent:
        (q_ref, k_ref, v_ref, qg_ref, kg_ref, gq_ref, gk_ref, cosq_ref, sinq_ref, cosk_ref, sink_ref,
         ck_ref, cv_ref, o_ref) = refs
    else:
        (q_ref, k_ref, v_ref, qg_ref, kg_ref, gq_ref, gk_ref, o_ref, kout_ref) = refs
    pair_w = 2 * HEAD_DIM

    def head_norm(x, mean_ref, gain):
        sq = x * x
        hi = sq.astype(BF16)
        lo = (sq - hi.astype(F32)).astype(BF16)
        ms = jnp.dot(hi, mean_ref[...], preferred_element_type=F32)
        ms = ms + jnp.dot(lo, mean_ref[...], preferred_element_type=F32)
        return x * lax.rsqrt(ms + EPS) * gain

    def rope(x, cos, sin):
        n = x.shape[1]
        lane = lax.broadcasted_iota(jnp.int32, x.shape, 1)
        first_of_pair = (lane // ROPE_PAIRS) % 2 == 0
        swapped = jnp.where(first_of_pair, pltpu.roll(x, n - ROPE_PAIRS, axis=1), pltpu.roll(x, ROPE_PAIRS, axis=1))
        return x * cos + swapped * sin

    def attend(rows):
        q = head_norm(q_ref[rows, :].astype(F32), gq_ref, qg_ref[...])
        k = head_norm(k_ref[rows, :].astype(F32), gk_ref, kg_ref[...])
        if latent:
            q = rope(q, cosq_ref[...], sinq_ref[...])
            k = rope(k, cosk_ref[...], sink_ref[...])
        else:
            kout_ref[rows, :] = k
        q = q * (HEAD_DIM ** -0.5)
        v = v_ref[rows, :].astype(F32)
        n_q = q.shape[0]
        low_kv = lax.broadcasted_iota(jnp.int32, k.shape, 1) < HEAD_DIM
        low_q = lax.broadcasted_iota(jnp.int32, (n_q, pair_w), 1) < HEAD_DIM
        k_swapped = pltpu.roll(k, HEAD_DIM, axis=1)
        v_swapped = pltpu.roll(v, HEAD_DIM, axis=1)
        nt = (((1,), (1,)), ((), ()))
        for j in range(KV_HEADS):
            kd = (jnp.where(low_kv, k, k_swapped) if j == 0 else jnp.where(low_kv, k_swapped, k)).astype(BF16)
            vd = (jnp.where(low_kv, v, v_swapped) if j == 0 else jnp.where(low_kv, v_swapped, v)).astype(BF16)
            tiles = range(j * Q_PER_KV // 2, (j + 1) * Q_PER_KV // 2)
            parts = []
            for t in tiles:
                qt = q[:, t * pair_w:(t + 1) * pair_w]
                parts += [jnp.where(low_q, qt, 0.0), jnp.where(low_q, 0.0, qt)]
            qs = jnp.concatenate(parts, axis=0).astype(BF16)
            s_new = lax.dot_general(qs, kd, nt, preferred_element_type=F32)
            m = jnp.max(s_new, axis=-1, keepdims=True)
            if latent:
                ckd = jnp.concatenate([ck_ref[j], ck_ref[j]], axis=1).astype(BF16)
                cvd = jnp.concatenate([cv_ref[j], cv_ref[j]], axis=1).astype(BF16)
                s_old = lax.dot_general(qs, ckd, nt, preferred_element_type=F32)
                m = jnp.maximum(m, jnp.max(s_old, axis=-1, keepdims=True))
            p_new = jnp.exp(s_new - m)
            den = jnp.sum(p_new, axis=-1, keepdims=True)
            acc = jnp.dot(p_new.astype(BF16), vd, preferred_element_type=F32)
            if latent:
                p_old = jnp.exp(s_old - m)
                den = den + jnp.sum(p_old, axis=-1, keepdims=True)
                acc = acc + jnp.dot(p_old.astype(BF16), cvd, preferred_element_type=F32)
            out = acc / den
            for i, t in enumerate(tiles):
                lo_head = out[(2 * i) * n_q:(2 * i + 1) * n_q, :]
                hi_head = out[(2 * i + 1) * n_q:(2 * i + 2) * n_q, :]
                o_ref[rows, t * pair_w:(t + 1) * pair_w] = jnp.where(low_q, lo_head, hi_head).astype(o_ref.dtype)

    if latent:
        attend(slice(None))
    else:
        seq = PROMPT_LEN

        def one_sequence(s, carry):
            attend(pl.ds(pl.multiple_of(s * seq, seq), seq))
            return carry

        lax.fori_loop(0, q_ref.shape[0] // seq, one_sequence, 0)


def _attn_common_args(qn_g, kn_g):
    q_w, kv_w = Q_HEADS * HEAD_DIM, KV_HEADS * HEAD_DIM
    return (jnp.tile(qn_g, Q_HEADS).reshape(1, q_w), jnp.tile(kn_g, KV_HEADS).reshape(1, kv_w),
            _head_mean_matrix(q_w), _head_mean_matrix(kv_w))


def _attention_prompt(z, qn_g, kn_g):
    L = 4 * PROMPT_LEN
    q_w, kv_w = Q_HEADS * HEAD_DIM, KV_HEADS * HEAD_DIM
    q_col = (5 * A_WIDTH) // q_w
    k_col = (5 * A_WIDTH + q_w) // kv_w
    const = lambda r, c: pl.BlockSpec((r, c), lambda s: (0, 0))
    return pl.pallas_call(
        functools.partial(_attn_kernel, latent=False),
        grid=(N_PROMPT_TOK // L,),
        in_specs=[
            pl.BlockSpec((L, q_w), lambda s: (s, q_col)),
            pl.BlockSpec((L, kv_w), lambda s: (s, k_col)),
            pl.BlockSpec((L, kv_w), lambda s: (s, k_col + 1)),
            const(1, q_w), const(1, kv_w), const(q_w, q_w), const(kv_w, kv_w),
        ],
        out_specs=(pl.BlockSpec((L, q_w), lambda s: (s, 0)),
                   pl.BlockSpec((L, kv_w), lambda s: (s, 0))),
        out_shape=(jax.ShapeDtypeStruct((N_PROMPT_TOK, q_w), BF16),
                   jax.ShapeDtypeStruct((N_PROMPT_TOK, kv_w), F32)),
        compiler_params=_params("arbitrary"),
        name="attn_prompt",
    )(z, z, z, *_attn_common_args(qn_g, kn_g))


def _attention_latent(z, qn_g, kn_g, cache_k, cache_v):
    L = LATENT_LEN
    nqb = L // Q_BLOCK
    q_w, kv_w = Q_HEADS * HEAD_DIM, KV_HEADS * HEAD_DIM
    q_col = (5 * A_WIDTH) // q_w
    k_col = (5 * A_WIDTH + q_w) // kv_w
    qrow0 = N_PROMPT_TOK // Q_BLOCK
    krow0 = N_PROMPT_TOK // L
    cos, sin = _rope_tables()
    cos_q, sin_q = jnp.asarray(np.tile(cos, (1, Q_HEADS))), jnp.asarray(np.tile(sin, (1, Q_HEADS)))
    cos_k, sin_k = jnp.asarray(np.tile(cos, (1, KV_HEADS))), jnp.asarray(np.tile(sin, (1, KV_HEADS)))
    const = lambda r, c: pl.BlockSpec((r, c), lambda s, b: (0, 0))
    cache_spec = pl.BlockSpec((None, None, KV_HEADS, PAST_LEN, HEAD_DIM), lambda s, b: (s, 0, 0, 0, 0))
    return pl.pallas_call(
        functools.partial(_attn_kernel, latent=True),
        grid=(N_LATENT_SEQ, nqb),
        in_specs=[
            pl.BlockSpec((Q_BLOCK, q_w), lambda s, b: (qrow0 + s * nqb + b, q_col)),
            pl.BlockSpec((L, kv_w), lambda s, b: (krow0 + s, k_col)),
            pl.BlockSpec((L, kv_w), lambda s, b: (krow0 + s, k_col + 1)),
            const(1, q_w), const(1, kv_w), const(q_w, q_w), const(kv_w, kv_w),
            pl.BlockSpec((Q_BLOCK, q_w), lambda s, b: (b, 0)),
            pl.BlockSpec((Q_BLOCK, q_w), lambda s, b: (b, 0)),
            const(L, kv_w), const(L, kv_w),
            cache_spec, cache_spec,
        ],
        out_specs=pl.BlockSpec((Q_BLOCK, q_w), lambda s, b: (s * nqb + b, 0)),
        out_shape=jax.ShapeDtypeStruct((N_LATENT_TOK, q_w), BF16),
        compiler_params=_params("arbitrary", "arbitrary"),
        name="attn_latent",
    )(z, z, z, *_attn_common_args(qn_g, kn_g), cos_q, sin_q, cos_k, sin_k, cache_k, cache_v)


def _out_proj_kernel(*refs, n_act, n_x):
    a_refs = refs[:2 * n_act]
    x_refs = refs[2 * n_act:2 * n_act + n_x]
    g_ref, mod_ref, rw_ref, w_ref, xo_ref, h_ref, lg_ref, wb_ref, rws_ref = refs[2 * n_act + n_x:]
    _cast_once(w_ref, wb_ref)

    @pl.when(pl.program_id(0) == 0)
    def _():
        rw = rw_ref[...]
        hi = rw.astype(BF16).astype(F32)
        lo = (rw - hi).astype(BF16).astype(F32)
        rws_ref[...] = (hi + pltpu.roll(lo, N_EXPERTS, axis=1)).astype(BF16)

    mod = mod_ref[...]
    n = OUT_PROJ_SUB_ROWS

    def sub_block(r, carry):
        rows = pl.ds(pl.multiple_of(r * n, n), n)
        acc = None
        k0 = 0
        for ap_ref, al_ref in zip(a_refs[0::2], a_refs[1::2]):
            k1 = k0 + ap_ref.shape[1]
            part = jnp.dot(_select_trunk(ap_ref, al_ref, rows), wb_ref[k0:k1, :], preferred_element_type=F32)
            acc = part if acc is None else acc + part
            k0 = k1
        x_in = x_refs[0][rows, :] if n_x == 1 else _select_trunk(*x_refs, rows)
        x = x_in + mod[2:3, :] * acc
        xo_ref[rows, :] = x
        h = _modulated_norm(x, g_ref[...], mod, 3, 4)
        h_ref[rows, :] = _pack_rows(h)
        h_hi = h.astype(BF16)
        h_lo = (h - h_hi.astype(F32)).astype(BF16)
        both = jnp.dot(jnp.concatenate([h_hi, h_lo], axis=0), rws_ref[...], preferred_element_type=F32)
        from_hi, from_lo = both[:n], both[n:]
        lg = from_hi + pltpu.roll(from_hi, ROUTER_LANES - N_EXPERTS, axis=1) + from_lo
        lg_ref[:, rows] = lg.T[:N_EXPERTS, :]
        return carry

    lax.fori_loop(0, xo_ref.shape[0] // n, sub_block, 0)


def _out_proj(acts, w, xs, g, mod_l, router_wp, block_rows=1024):
    tok = lambda width: pl.BlockSpec((block_rows, width), lambda i: (i, 0))
    in_specs = [spec for ap, _ in acts for spec in _trunk_specs(block_rows, ap.shape[1])]
    in_specs += [tok(D_MODEL)] if len(xs) == 1 else list(_trunk_specs(block_rows, D_MODEL))
    in_specs += [_resident((1, D_MODEL)), _mod_spec(block_rows), _resident((D_MODEL, ROUTER_LANES)),
                 _resident(w.shape)]
    return pl.pallas_call(
        functools.partial(_out_proj_kernel, n_act=len(acts), n_x=len(xs)),
        grid=(N_TOK // block_rows,),
        in_specs=in_specs,
        out_specs=(tok(D_MODEL), tok(ROW_WORDS), pl.BlockSpec((N_EXPERTS, block_rows), lambda i: (0, i))),
        out_shape=(jax.ShapeDtypeStruct((N_TOK, D_MODEL), F32),
                   jax.ShapeDtypeStruct((N_TOK, ROW_WORDS), jnp.int32),
                   jax.ShapeDtypeStruct((N_EXPERTS, N_TOK), F32)),
        scratch_shapes=[pltpu.VMEM(w.shape, BF16), pltpu.VMEM((D_MODEL, ROUTER_LANES), BF16)],
        compiler_params=_params("arbitrary"),
        name="out_proj",
    )(*[a for pair in acts for a in pair], *xs, g.reshape(1, D_MODEL), mod_l, router_wp, w)


def _router_kernel(lg_ref, rb_ref, pos_ref, w_ref, plan_ref, rank_ref):
    lg = lg_ref[...]
    ex = jnp.exp(lg - jnp.max(lg, axis=0, keepdims=True))
    scores = ex / jnp.sum(ex, axis=0, keepdims=True)
    biased = scores + rb_ref[...]
    rows = [biased[e:e + 1, :] for e in range(N_EXPERTS)]
    selected = []
    group_score = []
    for gi in range(N_GROUPS):
        r = rows[gi * EXPERTS_PER_GROUP:(gi + 1) * EXPERTS_PER_GROUP]
        total = None
        for i in range(EXPERTS_PER_GROUP):
            rank = None
            for j in range(EXPERTS_PER_GROUP):
                if j == i:
                    continue
                ahead = (r[j] > r[i]) if j > i else (r[j] >= r[i])
                ahead = jnp.where(ahead, 1.0, 0.0)
                rank = ahead if rank is None else rank + ahead
            sel = rank < 1.5
            selected.append(sel)
            contrib = jnp.where(sel, r[i], 0.0)
            total = contrib if total is None else total + contrib
        group_score.append(total)
    best = group_score[0]
    best_group = jnp.zeros_like(best)
    for gi in range(1, N_GROUPS):
        better = group_score[gi] > best
        best_group = jnp.where(better, float(gi), best_group)
        best = jnp.where(better, group_score[gi], best)
    picked = []
    chosen = []
    den = None
    for e in range(N_EXPERTS):
        in_group = best_group == float(e // EXPERTS_PER_GROUP)
        use = jnp.where(selected[e], jnp.where(in_group, 1.0, 0.0), 0.0)
        w = use * scores[e:e + 1, :]
        chosen.append(use)
        picked.append(w)
        den = w if den is None else den + w
    lanes = 128
    n_blk = N_TOK // lanes
    li = lax.broadcasted_iota(jnp.int32, (lanes, lanes), 0)
    lj = lax.broadcasted_iota(jnp.int32, (lanes, lanes), 1)
    prefix = jnp.where(li <= lj, 1.0, 0.0).astype(BF16)
    carry = jnp.zeros((N_EXPERTS, 1), F32)
    for blk in range(n_blk):
        cols = slice(blk * lanes, (blk + 1) * lanes)
        m = jnp.concatenate([chosen[e][:, cols] for e in range(N_EXPERTS)], axis=0)
        incl = jnp.dot(m.astype(BF16), prefix, preferred_element_type=F32)
        rank_ref[:, cols] = incl - m + carry
        carry = carry + incl[:, lanes - 1:lanes]
    count = carry
    padded = jnp.floor((count + float(MOE_TILE - 1)) * (1.0 / MOE_TILE)) * float(MOE_TILE)
    erow = lax.broadcasted_iota(jnp.int32, (N_EXPERTS, 1), 0)
    offset = jnp.zeros((N_EXPERTS, 1), F32)
    for e in range(N_EXPERTS - 1):
        offset = offset + jnp.where(erow > e, padded[e:e + 1, :], 0.0)
    seen = jnp.zeros_like(den)
    pos_a = jnp.zeros_like(den)
    pos_b = jnp.zeros_like(den)
    w_a = jnp.zeros_like(den)
    w_b = jnp.zeros_like(den)
    for e in range(N_EXPERTS):
        pos_e = rank_ref[e:e + 1, :] + offset[e:e + 1, :]
        gate_e = picked[e] / den
        first = jnp.where(seen < 0.5, chosen[e], 0.0) > 0.5
        second = jnp.where(seen > 0.5, chosen[e], 0.0) > 0.5
        pos_a = jnp.where(first, pos_e, pos_a)
        w_a = jnp.where(first, gate_e, w_a)
        pos_b = jnp.where(second, pos_e, pos_b)
        w_b = jnp.where(second, gate_e, w_b)
        seen = seen + chosen[e]
    pos_ref[0:1, :] = pos_a.astype(jnp.int32)
    pos_ref[1:2, :] = pos_b.astype(jnp.int32)
    w_ref[0:1, :] = w_a
    w_ref[1:2, :] = w_b
    start = (lax.broadcasted_iota(jnp.int32, (N_EXPERTS, lanes), 1) * MOE_TILE).astype(F32)
    end = offset + padded
    tile_expert = jnp.sum(jnp.where(end <= start, 1.0, 0.0), axis=0, keepdims=True)
    inside = (offset <= start) & (start < end)
    real = jnp.clip(count - (start - offset), 0.0, float(MOE_TILE))
    tile_rows = jnp.sum(jnp.where(inside, real, 0.0), axis=0, keepdims=True)
    plan_ref[0:1, :] = jnp.minimum(tile_expert, float(N_EXPERTS - 1)).astype(jnp.int32)
    plan_ref[1:2, :] = tile_rows.astype(jnp.int32)


def _router(logits_t, router_b):
    whole = lambda shape: pl.BlockSpec(shape, lambda i: (0, 0))
    return pl.pallas_call(
        _router_kernel,
        grid=(1,),
        in_specs=[whole((N_EXPERTS, N_TOK)), whole((N_EXPERTS, 1))],
        out_specs=(whole((2, N_TOK)), whole((2, N_TOK)), whole((2, 128))),
        out_shape=(jax.ShapeDtypeStruct((2, N_TOK), jnp.int32),
                   jax.ShapeDtypeStruct((2, N_TOK), F32),
                   jax.ShapeDtypeStruct((2, 128), jnp.int32)),
        scratch_shapes=[pltpu.VMEM((N_EXPERTS, N_TOK), F32)],
        compiler_params=_params("arbitrary"),
        name="router",
    )(logits_t, router_b.reshape(N_EXPERTS, 1))


def _sc_mesh():
    return plsc.VectorSubcoreMesh(core_axis_name="c", subcore_axis_name="s")


def _sc_worker_base():
    return (lax.axis_index("s") * SC_CORES + lax.axis_index("c")) * (N_TOK // SC_WORKERS)


def _moe_dispatch(h, pos_a, pos_b):
    n_chunks = N_TOK // SC_WORKERS // SC_CHUNK

    @functools.partial(
        pl.kernel, mesh=_sc_mesh(),
        out_type=jax.ShapeDtypeStruct((MOE_ROWS, ROW_WORDS), jnp.int32),
        scratch_types=[pltpu.VMEM((SC_CHUNK,), jnp.int32), pltpu.VMEM((SC_CHUNK,), jnp.int32),
                       pltpu.VMEM((SC_CHUNK, ROW_WORDS), jnp.int32)],
        name="moe_dispatch",
    )
    def run(h_hbm, pa_hbm, pb_hbm, xs_hbm, ia_v, ib_v, rows_v):
        base = _sc_worker_base()

        @pl.loop(0, n_chunks)
        def _(ci):
            tok = pl.ds(pl.multiple_of(base + ci * SC_CHUNK, SC_CHUNK), SC_CHUNK)
            pltpu.sync_copy(pa_hbm.at[tok], ia_v)
            pltpu.sync_copy(pb_hbm.at[tok], ib_v)
            pltpu.sync_copy(h_hbm.at[tok], rows_v)
            pltpu.sync_copy(rows_v, xs_hbm.at[ia_v])
            pltpu.sync_copy(rows_v, xs_hbm.at[ib_v])

    return run(h, pos_a, pos_b)


def _moe_collect(ys, pos_a, pos_b):
    n_chunks = N_TOK // SC_WORKERS // SC_CHUNK
    out = jax.ShapeDtypeStruct((N_TOK, ROW_WORDS), jnp.int32)

    @functools.partial(
        pl.kernel, mesh=_sc_mesh(), out_type=(out, out),
        scratch_types=[pltpu.VMEM((SC_CHUNK,), jnp.int32), pltpu.VMEM((SC_CHUNK,), jnp.int32),
                       pltpu.VMEM((SC_CHUNK, ROW_WORDS), jnp.int32)],
        name="moe_collect",
    )
    def run(ys_hbm, pa_hbm, pb_hbm, ya_hbm, yb_hbm, ia_v, ib_v, rows_v):
        base = _sc_worker_base()

        @pl.loop(0, n_chunks)
        def _(ci):
            tok = pl.ds(pl.multiple_of(base + ci * SC_CHUNK, SC_CHUNK), SC_CHUNK)
            pltpu.sync_copy(pa_hbm.at[tok], ia_v)
            pltpu.sync_copy(pb_hbm.at[tok], ib_v)
            pltpu.sync_copy(ys_hbm.at[ia_v], rows_v)
            pltpu.sync_copy(rows_v, ya_hbm.at[tok])
            pltpu.sync_copy(ys_hbm.at[ib_v], rows_v)
            pltpu.sync_copy(rows_v, yb_hbm.at[tok])

    return run(ys, pos_a, pos_b)


def _experts_kernel(plan_ref, xs_ref, wg_hbm, wu_hbm, wd_hbm, y_ref,
                    sg_ref, su_ref, sd_ref, wgb_ref, wub_ref, wdb_ref, sems, seg_ref, *, layer):
    j = pl.program_id(0)
    n_tiles = pl.num_programs(0)
    expert = plan_ref[j]
    n_real = plan_ref[PLAN_LANES + j]
    fresh = jnp.logical_or(j == 0, expert != plan_ref[jnp.maximum(j - 1, 0)])

    def weight_copies(e, slot):
        return (pltpu.make_async_copy(wg_hbm.at[layer, e], sg_ref.at[slot], sems.at[slot, 0]),
                pltpu.make_async_copy(wu_hbm.at[layer, e], su_ref.at[slot], sems.at[slot, 1]),
                pltpu.make_async_copy(wd_hbm.at[layer, e], sd_ref.at[slot], sems.at[slot, 2]))

    @pl.when(j == 0)
    def _():
        seg_ref[0] = 0

        @pl.when(n_real > 0)
        def _():
            for cp in weight_copies(expert, 0):
                cp.start()

    @pl.when(jnp.logical_and(n_real > 0, fresh))
    def _():
        slot = seg_ref[0] % 2
        for cp in weight_copies(expert, slot):
            cp.wait()
        wgb_ref[...] = sg_ref[slot].astype(BF16)
        wub_ref[...] = su_ref[slot].astype(BF16)
        wdb_ref[...] = sd_ref[slot].astype(BF16)
        nxt = lax.while_loop(lambda t: jnp.logical_and(t < n_tiles, plan_ref[jnp.minimum(t, n_tiles - 1)] == expert),
                             lambda t: t + 1, j + 1)
        nxt_c = jnp.minimum(nxt, n_tiles - 1)

        @pl.when(jnp.logical_and(nxt < n_tiles, plan_ref[PLAN_LANES + nxt_c] > 0))
        def _():
            for cp in weight_copies(plan_ref[nxt_c], 1 - slot):
                cp.start()

        seg_ref[0] = seg_ref[0] + 1

    @pl.when(n_real > 0)
    def _():
        row = lax.broadcasted_iota(jnp.int32, xs_ref.shape, 0)
        words = jnp.where(row < n_real, xs_ref[...], 0)
        x = _unpack_rows(words).astype(BF16)
        a = jnp.dot(x, wgb_ref[...], preferred_element_type=F32)
        b = jnp.dot(x, wub_ref[...], preferred_element_type=F32)
        hid = (a * jax.nn.sigmoid(a)) * b
        y_ref[...] = _pack_rows(jnp.dot(hid.astype(BF16), wdb_ref[...], preferred_element_type=F32))


def _experts(plan, xs, w_gate, w_up, w_down, layer):
    hbm = pl.BlockSpec(memory_space=pl.ANY)
    return pl.pallas_call(
        functools.partial(_experts_kernel, layer=layer),
        grid_spec=pltpu.PrefetchScalarGridSpec(
            num_scalar_prefetch=1,
            grid=(MOE_ROWS // MOE_TILE,),
            in_specs=[pl.BlockSpec((MOE_TILE, ROW_WORDS), lambda j, plan: (j, 0)), hbm, hbm, hbm],
            out_specs=pl.BlockSpec((MOE_TILE, ROW_WORDS), lambda j, plan: (j, 0)),
            scratch_shapes=[pltpu.VMEM((2, D_MODEL, D_EXPERT), F32), pltpu.VMEM((2, D_MODEL, D_EXPERT), F32),
                            pltpu.VMEM((2, D_EXPERT, D_MODEL), F32),
                            pltpu.VMEM((D_MODEL, D_EXPERT), BF16), pltpu.VMEM((D_MODEL, D_EXPERT), BF16),
                            pltpu.VMEM((D_EXPERT, D_MODEL), BF16),
                            pltpu.SemaphoreType.DMA((2, 3)), pltpu.SMEM((1,), jnp.int32)],
        ),
        out_shape=jax.ShapeDtypeStruct((MOE_ROWS, ROW_WORDS), jnp.int32),
        compiler_params=_params("arbitrary"),
        name="experts",
    )(plan, xs, w_gate, w_up, w_down)


def _combine_kernel(x_ref, ya_ref, yb_ref, wt_ref, mod_ref, o_ref):
    o_ref[...] = _moe_mix(x_ref, ya_ref, yb_ref, wt_ref, mod_ref)


def _combine(x, moe_out, mod_l, tok0, n_tok, block_rows=512):
    ya, yb, w_tok = moe_out
    b0 = tok0 // block_rows
    rows = lambda width: pl.BlockSpec((block_rows, width), lambda i: (b0 + i, 0))
    return pl.pallas_call(
        _combine_kernel,
        grid=(n_tok // block_rows,),
        in_specs=[rows(D_MODEL), rows(ROW_WORDS), rows(ROW_WORDS), rows(TOP_K),
                  pl.BlockSpec((None, 6, D_MODEL), lambda i: (_cond_of_token_block(b0 + i, block_rows), 0, 0))],
        out_specs=pl.BlockSpec((block_rows, D_MODEL), lambda i: (i, 0)),
        out_shape=jax.ShapeDtypeStruct((n_tok, D_MODEL), F32),
        compiler_params=_params("arbitrary"),
        name="combine",
    )(x, ya, yb, w_tok, mod_l)


def _moe(h, logits_t, router_b, w_gate, w_up, w_down, layer):
    pos, w, plan = _router(logits_t, router_b)
    xs = _moe_dispatch(h, pos[0], pos[1])
    ys = _experts(plan.reshape(-1), xs, w_gate, w_up, w_down, layer)
    ya, yb = _moe_collect(ys, pos[0], pos[1])
    return ya, yb, w.T


def _dft_tables(L):
    k = np.arange(L)[:, None]
    m = np.arange(L)[None, :]
    r = (k * m) % (2 * L)
    ang = np.pi * r.astype(np.float64) / L
    fc = np.cos(ang)
    fs = np.sin(ang)
    fs[0, :] = np.where(np.arange(L) % 2 == 0, 1.0, -1.0)
    wk = np.full((L, 1), 1.0 / L)
    wk[0, 0] = 0.5 / L
    gc = (fc * wk).T
    gs = (fs * wk).T
    return [jnp.asarray(t.astype(np.float32)).astype(BF16) for t in (fc, fs, gc, gs)]


def _filter_consts(L):
    t = np.linspace(0.0, 1.0, L, dtype=np.float32)[:, None]
    w = (np.float32(2.0 * np.pi) * np.arange(L, dtype=np.float32)[:, None] / np.float32(L)).astype(np.float32)
    fb = np.linspace(1e-4, HY_BANDS - 1, HY_BANDS, dtype=np.float32)[None, :]
    emb = np.concatenate([t, np.cos(fb * w), -np.sin(fb * w)], axis=-1).astype(np.float32)
    lo = math.log(HY_DECAY_TARGET) / HY_SLOW_PCT
    hi = math.log(HY_DECAY_TARGET) / HY_FAST_PCT
    deltas = np.abs(np.linspace(lo, hi, D_MODEL, dtype=np.float32))
    decay = np.exp(-t * deltas).astype(np.float32)
    return jnp.asarray(emb), jnp.asarray(decay)


def _filter_kernel(emb_ref, w1_ref, b1_ref, w2_ref, b2_ref, fr_ref, w3f_ref, w3b_ref, dec_ref,
                   fc_ref, fs_ref, kr_ref, q_ref, krn_ref, hd_ref):
    @pl.when(pl.program_id(0) == 0)
    def _():
        fr = fr_ref[...]
        h1 = jnp.sin(fr * (jnp.dot(emb_ref[...], w1_ref[...], precision=HIGHEST,
                                   preferred_element_type=F32) + b1_ref[...]))
        hd_ref[...] = jnp.sin(fr * (jnp.dot(h1, w2_ref[...], precision=HIGHEST,
                                            preferred_element_type=F32) + b2_ref[...]))

    hd = hd_ref[...]
    dec = dec_ref[...]
    f = jnp.dot(hd, w3f_ref[...], precision=HIGHEST, preferred_element_type=F32) * dec
    g = jnp.dot(hd, w3b_ref[...], precision=HIGHEST, preferred_element_type=F32) * dec
    row = lax.broadcasted_iota(jnp.int32, f.shape, 0)
    g = jnp.where(row == 0, 0.0, g)
    s = f + g
    d = f - g
    kr = jnp.dot(fc_ref[...], s.astype(BF16), preferred_element_type=F32)
    qq = jnp.dot(fs_ref[...], d.astype(BF16), preferred_element_type=F32)
    alt = jnp.where(row % 2 == 0, 1.0, -1.0)
    nyq = jnp.sum(alt * s, axis=0, keepdims=True)
    kr_ref[...] = kr
    q_ref[...] = jnp.where(row == 0, 0.0, qq)
    krn_ref[...] = jnp.where(row == 0, nyq, kr)


def _hyena_filter_spectrum(L, w1, b1, w2, b2, w3, freq, fc, fs, cblk=256):
    emb, decay = _filter_consts(L)
    ncb = D_MODEL // cblk
    n_emb = 128
    emb = jnp.pad(emb, ((0, 0), (0, n_emb - emb.shape[1])))
    w1 = jnp.pad(w1, ((0, n_emb - w1.shape[0]), (0, 0)))
    full = lambda shape: pl.BlockSpec(shape, lambda j: tuple(0 for _ in shape))
    out_sds = jax.ShapeDtypeStruct((L, D_MODEL), F32)
    out_spec = pl.BlockSpec((L, cblk), lambda j: (0, j))
    return pl.pallas_call(
        _filter_kernel,
        grid=(ncb,),
        in_specs=[
            full((L, n_emb)), full((n_emb, HY_FFN)), full((1, HY_FFN)), full((HY_FFN, HY_FFN)),
            full((1, HY_FFN)), full((1, HY_FFN)),
            pl.BlockSpec((HY_FFN, cblk), lambda j: (0, j)),
            pl.BlockSpec((HY_FFN, cblk), lambda j: (0, ncb + j)),
            pl.BlockSpec((L, cblk), lambda j: (0, j)),
            full((L, L)), full((L, L)),
        ],
        out_specs=(out_spec, out_spec, out_spec),
        out_shape=(out_sds, out_sds, out_sds),
        scratch_shapes=[pltpu.VMEM((L, HY_FFN), F32)],
        compiler_params=_params("arbitrary"),
        name=f"hyena_filter_{L}",
    )(emb, w1, b1.reshape(1, HY_FFN), w2, b2.reshape(1, HY_FFN), freq.reshape(1, HY_FFN), w3, w3, decay, fc, fs)


def _hyena_conv_kernel(x0_ref, x1_ref, v_ref, cw0_ref, cw1_ref, cwv_ref, cb0_ref, cb1_ref, cbv_ref,
                       kr_ref, q_ref, krn_ref, ds_ref, fc_ref, fs_ref, gc_ref, gs_ref, o_ref):
    L = fc_ref.shape[0]
    row = lax.broadcasted_iota(jnp.int32, (L, x0_ref.shape[1]), 0)

    def one_sequence(s, carry):
        rows = pl.ds(pl.multiple_of(s * L, L), L)

        def short_conv(u_ref, w_ref, b_ref):
            u = u_ref[rows, :].astype(F32)
            w = w_ref[...]
            prev = jnp.where(row == 0, 0.0, pltpu.roll(u, 1, axis=0))
            nxt = jnp.where(row == L - 1, 0.0, pltpu.roll(u, L - 1, axis=0))
            return prev * w[0:1, :] + u * w[1:2, :] + nxt * w[2:3, :] + b_ref[...]

        x0 = short_conv(x0_ref, cw0_ref, cb0_ref)
        x1 = short_conv(x1_ref, cw1_ref, cb1_ref)
        v = short_conv(v_ref, cwv_ref, cbv_ref)
        zz = v * x1
        zb = zz.astype(BF16)
        ur = jnp.dot(fc_ref[...], zb, preferred_element_type=F32)
        p = jnp.dot(fs_ref[...], zb, preferred_element_type=F32)
        qq = q_ref[...]
        yr = ur * kr_ref[...] - p * qq
        yw = ur * qq + p * krn_ref[...]
        y = jnp.dot(gc_ref[...], yr.astype(BF16), preferred_element_type=F32)
        y = y + jnp.dot(gs_ref[...], yw.astype(BF16), preferred_element_type=F32)
        o_ref[rows, :] = (x0 * (y + zz * ds_ref[...])).astype(o_ref.dtype)
        return carry

    lax.fori_loop(0, x0_ref.shape[0] // L, one_sequence, 0)


def _hyena_conv(u, conv_w, conv_b, dskip, spectrum, tables, *, latent):
    L = LATENT_LEN if latent else PROMPT_LEN
    n_seq = N_LATENT_SEQ if latent else N_PROMPT_SEQ
    cblk = 256 if latent else 512
    ncb = D_MODEL // cblk
    seqs = 1 if latent else 4
    row0 = (N_PROMPT_TOK // L) if latent else 0
    kr, qq, krn = spectrum
    fc, fs, gc, gs = tables

    def part(p, rows):
        if rows != L:
            return pl.BlockSpec((rows, cblk), lambda j, s: (0, p * ncb + j))
        return pl.BlockSpec((seqs * L, cblk), lambda j, s: (row0 // seqs + s, p * ncb + j))

    def const_cols(rows):
        return pl.BlockSpec((rows, cblk), lambda j, s: (0, j))

    mat = pl.BlockSpec((L, L), lambda j, s: (0, 0))
    conv_b2 = conv_b.reshape(1, 3 * D_MODEL)
    in_specs = [part(0, L), part(1, L), part(2, L),
                part(0, 3), part(1, 3), part(2, 3),
                part(0, 1), part(1, 1), part(2, 1),
                const_cols(L), const_cols(L), const_cols(L), const_cols(1),
                mat, mat, mat, mat]
    args = [u, u, u, conv_w, conv_w, conv_w, conv_b2, conv_b2, conv_b2,
            kr, qq, krn, dskip.reshape(1, D_MODEL), fc, fs, gc, gs]
    return pl.pallas_call(
        _hyena_conv_kernel,
        grid=(ncb, n_seq // seqs),
        in_specs=in_specs,
        out_specs=pl.BlockSpec((seqs * L, cblk), lambda j, s: (s, j)),
        out_shape=jax.ShapeDtypeStruct((n_seq * L, D_MODEL), BF16),
        compiler_params=_params("arbitrary", "arbitrary"),
        name="hyena_conv_latent" if latent else "hyena_conv_prompt",
    )(*args)


def kernel(x_prompt, x_sample, cache_k, cache_v, state_hgrn, c, c_ctx, norm_g, mod_w, mod_b, ab_in_w, hgrn_lb, hgrn_onorm_g, attn_qnorm_g, attn_knorm_g, ab_out_w, hy_in_w, hy_in_b, hy_conv_w, hy_conv_b, hy_f_w1, hy_f_b1, hy_f_w2, hy_f_b2, hy_f_w3, hy_f_freq, hy_dskip, hy_out_w, router_w, router_b, moe_w_gate, moe_w_up, moe_w_down):
    xp = x_prompt.reshape(N_PROMPT_TOK, D_MODEL)
    xl = x_sample.reshape(N_LATENT_TOK, D_MODEL)
    cond = jnp.concatenate([c_ctx[None, :], c, jnp.zeros((N_COND - 1 - N_LATENT_SEQ, D_MODEL), F32)], axis=0)
    mod = _modulation(cond, mod_w, mod_b)
    router_wp = jnp.pad(router_w, ((0, 0), (0, ROUTER_LANES - N_EXPERTS)))

    z = _in_proj0(xp, xl, norm_g[0, 0], mod[0], ab_in_w[0])
    oa_p, new_state = _hgrn(z, hgrn_lb, hgrn_onorm_g[0], None, latent=False)
    oa_l = _hgrn(z, hgrn_lb, hgrn_onorm_g[0], state_hgrn, latent=True)
    ob_p, k_prompt = _attention_prompt(z, attn_qnorm_g[0], attn_knorm_g[0])
    ob_l = _attention_latent(z, attn_qnorm_g[0], attn_knorm_g[0], cache_k, cache_v)
    x, h, logits_t = _out_proj([(oa_p, oa_l), (ob_p, ob_l)], ab_out_w[0], (xp, xl), norm_g[0, 1], mod[0],
                               router_wp)
    moe_out = _moe(h, logits_t, router_b, moe_w_gate, moe_w_up, moe_w_down, 0)

    x, u = _in_proj1(x, moe_out, mod[0], norm_g[1, 0], mod[1], hy_in_w[0], hy_in_b[0])
    pre = []
    for latent in (False, True):
        L = LATENT_LEN if latent else PROMPT_LEN
        tables = _dft_tables(L)
        spectrum = _hyena_filter_spectrum(L, hy_f_w1[0], hy_f_b1[0], hy_f_w2[0], hy_f_b2[0], hy_f_w3[0],
                                          hy_f_freq[0], tables[0], tables[1])
        pre.append(_hyena_conv(u, hy_conv_w[0], hy_conv_b[0], hy_dskip[0], spectrum, tables, latent=latent))
    x, h, logits_t = _out_proj([tuple(pre)], hy_out_w[0], (x,), norm_g[1, 1], mod[1], router_wp)
    moe_out = _moe(h, logits_t, router_b, moe_w_gate, moe_w_up, moe_w_down, 1)

    y_prompt = _combine(x, moe_out, mod[1], 0, N_PROMPT_TOK).reshape(N_PROMPT_SEQ, PROMPT_LEN, D_MODEL)
    y_sample = _combine(x, moe_out, mod[1], N_PROMPT_TOK, N_LATENT_TOK).reshape(N_LATENT_SEQ, LATENT_LEN, D_MODEL)
    kv_shape = (N_PROMPT_SEQ, PROMPT_LEN, KV_HEADS, HEAD_DIM)
    new_k = k_prompt.reshape(kv_shape).transpose(0, 2, 1, 3)[:, None]
    v_col = 5 * A_WIDTH + (Q_HEADS + KV_HEADS) * HEAD_DIM
    new_v = z[:N_PROMPT_TOK, v_col:].astype(F32).reshape(kv_shape).transpose(0, 2, 1, 3)[:, None]
    return (y_prompt, y_sample, new_k, new_v, new_state)
```

```python
import functools
import math

import numpy as np
import jax
import jax.numpy as jnp
from jax import lax
from jax.experimental import pallas as pl
from jax.experimental.pallas import tpu as pltpu
from jax.experimental.pallas import tpu_sc as plsc

F32 = jnp.float32
BF16 = jnp.bfloat16
HIGHEST = lax.Precision.HIGHEST

D_MODEL = 1024
N_PROMPT_SEQ = 32
PROMPT_LEN = 256
N_LATENT_SEQ = 2
LATENT_LEN = 1024
PAST_LEN = 512
GRID_W = 64
N_PROMPT_TOK = N_PROMPT_SEQ * PROMPT_LEN
N_LATENT_TOK = N_LATENT_SEQ * LATENT_LEN
N_TOK = N_PROMPT_TOK + N_LATENT_TOK
N_COND = 8
EPS = 1e-6

A_WIDTH = 512
A_HEADS = 4
A_DK = 128
CHUNK = 64
HGRN_BLOCK = 256
HGRN_HEADS_PER_STEP = 4
HEAD_DIM = 64
Q_HEADS = 8
KV_HEADS = 2
Q_PER_KV = Q_HEADS // KV_HEADS
Q_BLOCK = 256
ROPE_THETA = 10000.0
ROPE_PAIRS = HEAD_DIM // 4
AB_IN = 5 * A_WIDTH + (Q_HEADS + 2 * KV_HEADS) * HEAD_DIM

HY_BANDS = 16
HY_FFN = 64
HY_DECAY_TARGET = 1e-2
HY_FAST_PCT = 0.3
HY_SLOW_PCT = 1.5

N_EXPERTS = 16
N_GROUPS = 4
EXPERTS_PER_GROUP = 4
TOP_K = 2
D_EXPERT = 512
ROUTER_LANES = 128
OUT_PROJ_SUB_ROWS = 256
MOE_TILE = 512
MOE_ROWS = N_TOK * TOP_K + N_EXPERTS * MOE_TILE
PLAN_LANES = 128

SC_CORES = 2
SC_WORKERS = 32
SC_CHUNK = 80
ROW_WORDS = D_MODEL // 2

VMEM_LIMIT = 56 * 1024 * 1024


def _params(*sem):
    return pltpu.CompilerParams(dimension_semantics=sem, vmem_limit_bytes=VMEM_LIMIT)


def _pack_rows(x):
    n = x.shape[1] // 2
    bits = pltpu.bitcast(x.astype(BF16).astype(F32), jnp.uint32)
    return pltpu.bitcast(bits[:, :n] | (bits[:, n:] >> 16), jnp.int32)


def _unpack_rows(p):
    bits = pltpu.bitcast(p, jnp.uint32)
    hi = pltpu.bitcast(bits & jnp.uint32(0xFFFF0000), F32)
    lo = pltpu.bitcast(bits << 16, F32)
    return jnp.concatenate([hi, lo], axis=1)


def _cond_of_token_block(i, block_rows):
    start = i * block_rows
    return jnp.where(start < N_PROMPT_TOK, 0, 1 + (start - N_PROMPT_TOK) // LATENT_LEN)


def _mod_kernel(cond_ref, w_ref, b_ref, o_ref):
    cnd = cond_ref[...]
    s = cnd * jax.nn.sigmoid(cnd)
    s_hi = s.astype(BF16)
    s_lo = (s - s_hi.astype(F32)).astype(BF16)
    w = w_ref[...]
    w_hi = w.astype(BF16)
    w_lo = (w - w_hi.astype(F32)).astype(BF16)
    acc = jnp.dot(s_hi, w_hi, preferred_element_type=F32)
    acc = acc + jnp.dot(s_lo, w_hi, preferred_element_type=F32)
    acc = acc + jnp.dot(s_hi, w_lo, preferred_element_type=F32)
    o_ref[...] = acc + b_ref[...]


def _modulation(cond, mod_w, mod_b):
    depth = mod_w.shape[0]
    n_chunk = 6
    out = pl.pallas_call(
        _mod_kernel,
        grid=(depth, n_chunk),
        in_specs=[
            pl.BlockSpec((N_COND, D_MODEL), lambda l, j: (0, 0)),
            pl.BlockSpec((None, D_MODEL, D_MODEL), lambda l, j: (l, 0, j)),
            pl.BlockSpec((None, 1, D_MODEL), lambda l, j: (l, 0, j)),
        ],
        out_specs=pl.BlockSpec((None, N_COND, D_MODEL), lambda l, j: (l, 0, j)),
        out_shape=jax.ShapeDtypeStruct((depth, N_COND, n_chunk * D_MODEL), F32),
        compiler_params=_params("arbitrary", "arbitrary"),
        name="modulation",
    )(cond, mod_w, mod_b.reshape(depth, 1, n_chunk * D_MODEL))
    return out.reshape(depth, N_COND, n_chunk, D_MODEL)


def _modulated_norm(x, g, mod, shift_row, scale_row):
    ms = jnp.mean(x * x, axis=-1, keepdims=True)
    y = x * lax.rsqrt(ms + EPS) * g
    return y * (1.0 + mod[scale_row:scale_row + 1, :]) + mod[shift_row:shift_row + 1, :]


def _trunk_specs(block_rows, width):
    n_prompt_blocks = N_PROMPT_TOK // block_rows
    return (pl.BlockSpec((block_rows, width), lambda i: (jnp.minimum(i, n_prompt_blocks - 1), 0)),
            pl.BlockSpec((block_rows, width), lambda i: (jnp.maximum(i - n_prompt_blocks, 0), 0)))


def _select_trunk(p_ref, l_ref, rows=slice(None)):
    block_rows = p_ref.shape[0]
    return jnp.where(pl.program_id(0) < N_PROMPT_TOK // block_rows, p_ref[rows, :], l_ref[rows, :])


def _cast_once(w_ref, wb_ref):
    @pl.when(pl.program_id(0) == 0)
    def _():
        wb_ref[...] = w_ref[...].astype(BF16)


def _resident(shape):
    return pl.BlockSpec(shape, lambda i: tuple(0 for _ in shape), pipeline_mode=pl.Buffered(1))


def _mod_spec(block_rows):
    return pl.BlockSpec((None, 6, D_MODEL), lambda i: (_cond_of_token_block(i, block_rows), 0, 0))


def _in_proj0_kernel(xp_ref, xl_ref, g_ref, mod_ref, w_ref, o_ref, wb_ref):
    _cast_once(w_ref, wb_ref)
    h = _modulated_norm(_select_trunk(xp_ref, xl_ref), g_ref[...], mod_ref[...], 0, 1)
    o_ref[...] = jnp.dot(h.astype(BF16), wb_ref[...], preferred_element_type=F32).astype(o_ref.dtype)


def _in_proj0(x_prompt, x_latent, g, mod_l, w, block_rows=512):
    n = w.shape[1]
    return pl.pallas_call(
        _in_proj0_kernel,
        grid=(N_TOK // block_rows,),
        in_specs=[*_trunk_specs(block_rows, D_MODEL), _resident((1, D_MODEL)), _mod_spec(block_rows),
                  _resident((D_MODEL, n))],
        out_specs=pl.BlockSpec((block_rows, n), lambda i: (i, 0)),
        out_shape=jax.ShapeDtypeStruct((N_TOK, n), BF16),
        scratch_shapes=[pltpu.VMEM((D_MODEL, n), BF16)],
        compiler_params=_params("arbitrary"),
        name="in_proj0",
    )(x_prompt, x_latent, g.reshape(1, D_MODEL), mod_l, w)


def _moe_mix(x_ref, ya_ref, yb_ref, wt_ref, mod_ref):
    wt = wt_ref[...]
    mix = wt[:, 0:1] * _unpack_rows(ya_ref[...]) + wt[:, 1:2] * _unpack_rows(yb_ref[...])
    return x_ref[...] + mod_ref[5:6, :] * mix


def _in_proj1_kernel(x_ref, ya_ref, yb_ref, wt_ref, modp_ref, g_ref, mod_ref, w_ref, b_ref, xo_ref, o_ref, wb_ref):
    _cast_once(w_ref, wb_ref)
    x = _moe_mix(x_ref, ya_ref, yb_ref, wt_ref, modp_ref)
    xo_ref[...] = x
    h = _modulated_norm(x, g_ref[...], mod_ref[...], 0, 1)
    u = jnp.dot(h.astype(BF16), wb_ref[...], preferred_element_type=F32) + b_ref[...]
    o_ref[...] = u.astype(o_ref.dtype)


def _in_proj1(x, moe_out, mod_prev, g, mod_l, w, bias, block_rows=512):
    ya, yb, w_tok = moe_out
    n = w.shape[1]
    tok = pl.BlockSpec((block_rows, D_MODEL), lambda i: (i, 0))
    packed = pl.BlockSpec((block_rows, ROW_WORDS), lambda i: (i, 0))
    return pl.pallas_call(
        _in_proj1_kernel,
        grid=(N_TOK // block_rows,),
        in_specs=[tok, packed, packed, pl.BlockSpec((block_rows, TOP_K), lambda i: (i, 0)), _mod_spec(block_rows),
                  _resident((1, D_MODEL)), _mod_spec(block_rows), _resident((D_MODEL, n)), _resident((1, n))],
        out_specs=(tok, pl.BlockSpec((block_rows, n), lambda i: (i, 0))),
        out_shape=(jax.ShapeDtypeStruct((N_TOK, D_MODEL), F32), jax.ShapeDtypeStruct((N_TOK, n), BF16)),
        scratch_shapes=[pltpu.VMEM((D_MODEL, n), BF16)],
        compiler_params=_params("arbitrary"),
        name="in_proj1",
    )(x, ya, yb, w_tok, mod_prev, g.reshape(1, D_MODEL), mod_l, w, bias.reshape(1, n))


def _hgrn_kernel(*refs, seq_len, with_state):
    if with_state:
        (q_ref, zf_ref, zb_ref, i_ref, ga_ref, lb_ref, og_ref, s0_ref, o_ref, of_ref, ob_ref) = refs
    else:
        (q_ref, zf_ref, zb_ref, i_ref, ga_ref, lb_ref, og_ref, o_ref, s_ref, of_ref, ob_ref) = refs
    n_blocks = seq_len // HGRN_BLOCK
    chunks_per_block = HGRN_BLOCK // CHUNK

    lbr = lb_ref[...]
    mx = jnp.maximum(lbr[0], lbr[1])
    e0 = jnp.exp(lbr[0] - mx)
    e1 = jnp.exp(lbr[1] - mx)
    lb = e0 / (e0 + e1)

    row = lax.broadcasted_iota(jnp.int32, (HGRN_BLOCK, HGRN_BLOCK), 0)
    col = lax.broadcasted_iota(jnp.int32, (HGRN_BLOCK, HGRN_BLOCK), 1)
    same_chunk = (row // CHUNK) == (col // CHUNK)
    nt = (((1,), (1,)), ((), ()))
    tn = (((0,), (0,)), ((), ()))

    def per_chunk_row(x, idx):
        return jnp.concatenate(
            [jnp.broadcast_to(x[n * CHUNK + idx:n * CHUNK + idx + 1, :], (CHUNK, x.shape[1]))
             for n in range(chunks_per_block)], axis=0)

    def in_chunk_cumsum(tri, x):
        hi = x.astype(BF16)
        lo = (x - hi.astype(F32)).astype(BF16)
        return jnp.dot(tri, hi, preferred_element_type=F32) + jnp.dot(tri, lo, preferred_element_type=F32)

    def block(blk, cols, st, z_ref, lbd, forward, out_ref):
        rows = slice(blk * HGRN_BLOCK, (blk + 1) * HGRN_BLOCK)
        keep = (same_chunk & (col <= row)) if forward else (same_chunk & (col >= row))
        tri = jnp.where(keep, 1.0, 0.0).astype(BF16)
        mid = CHUNK // 2 if forward else CHUNK - 1 - CHUNK // 2
        last = CHUNK - 1 if forward else 0
        f = lbd + (1.0 - lbd) * jax.nn.sigmoid(z_ref[rows, cols].astype(F32))
        lf = jnp.log(f)
        k = 1.0 - f
        q = q_ref[rows, cols].astype(F32)
        vb = i_ref[rows, cols].astype(BF16)
        b = in_chunk_cumsum(tri, lf)
        bm = per_chunk_row(b, mid)
        bl = per_chunk_row(b, last)
        qe = (q * jnp.exp(b - bm)).astype(BF16)
        ke = (k * jnp.exp(bm - b)).astype(BF16)
        att = lax.dot_general(qe, ke, nt, preferred_element_type=F32)
        att = jnp.where(keep, att, 0.0)
        o_intra = jnp.dot(att.astype(BF16), vb, preferred_element_type=F32)
        qb = (q * jnp.exp(b)).astype(BF16)
        ks = (k * jnp.exp(bl - b)).astype(BF16)
        decay = jnp.exp(bl)
        order = range(chunks_per_block) if forward else range(chunks_per_block - 1, -1, -1)
        o_inter = [None] * chunks_per_block
        for n in order:
            cr = slice(n * CHUNK, (n + 1) * CHUNK)
            o_inter[n] = lax.dot_general(qb[cr], st.astype(BF16), nt, preferred_element_type=F32)
            upd = lax.dot_general(vb[cr], ks[cr], tn, preferred_element_type=F32)
            st = st * decay[n * CHUNK:n * CHUNK + 1, :] + upd
        out_ref[rows, cols] = o_intra + jnp.concatenate(o_inter, axis=0)
        return st

    for hd in range(q_ref.shape[1] // A_DK):
        cols = slice(hd * A_DK, (hd + 1) * A_DK)
        if with_state:
            st_f, st_b = s0_ref[0, hd].T, s0_ref[1, hd].T
        else:
            st_f, st_b = jnp.zeros((A_DK, A_DK), F32), jnp.zeros((A_DK, A_DK), F32)
        for step in range(n_blocks):
            st_f = block(step, cols, st_f, zf_ref, lb[0:1, cols], True, of_ref)
            st_b = block(n_blocks - 1 - step, cols, st_b, zb_ref, lb[1:2, cols], False, ob_ref)
        if not with_state:
            s_ref[0, hd] = st_f.T
            s_ref[1, hd] = st_b.T
        o = of_ref[:, cols] + ob_ref[:, cols]
        o = o * lax.rsqrt(jnp.mean(o * o, axis=-1, keepdims=True) + EPS) * og_ref[:, cols]
        ga = ga_ref[:, cols].astype(F32)
        o_ref[:, cols] = (o * (ga * jax.nn.sigmoid(ga))).astype(o_ref.dtype)


def _hgrn(z, hgrn_lb, onorm_g, state, *, latent):
    seq_len = LATENT_LEN if latent else PROMPT_LEN
    n_seq = N_LATENT_SEQ if latent else N_PROMPT_SEQ
    row0 = (N_PROMPT_TOK // seq_len) if latent else 0

    hw = HGRN_HEADS_PER_STEP * A_DK
    n_hg = A_HEADS // HGRN_HEADS_PER_STEP

    def zspec(part):
        return pl.BlockSpec((seq_len, hw), lambda s, h: (row0 + s, part * n_hg + h))

    in_specs = [zspec(0), zspec(1), zspec(2), zspec(3), zspec(4),
                pl.BlockSpec((2, 2, hw), lambda s, h: (0, 0, h)),
                pl.BlockSpec((1, hw), lambda s, h: (0, h))]
    args = [z, z, z, z, z, hgrn_lb, onorm_g.reshape(1, A_WIDTH)]
    state_spec = pl.BlockSpec((None, None, 2, HGRN_HEADS_PER_STEP, A_DK, A_DK), lambda s, h: (s, 0, 0, h, 0, 0))
    o_shape = jax.ShapeDtypeStruct((n_seq * seq_len, A_WIDTH), BF16)
    o_spec = pl.BlockSpec((seq_len, hw), lambda s, h: (s, h))
    if latent:
        in_specs.append(state_spec)
        args.append(state)
        out_shape, out_specs = o_shape, o_spec
    else:
        out_shape = (o_shape, jax.ShapeDtypeStruct((n_seq, 1, 2, A_HEADS, A_DK, A_DK), F32))
        out_specs = (o_spec, state_spec)
    return pl.pallas_call(
        functools.partial(_hgrn_kernel, seq_len=seq_len, with_state=latent),
        grid=(n_seq, n_hg),
        in_specs=in_specs,
        out_specs=out_specs,
        out_shape=out_shape,
        scratch_shapes=[pltpu.VMEM((seq_len, hw), F32), pltpu.VMEM((seq_len, hw), F32)],
        compiler_params=_params("arbitrary", "arbitrary"),
        name="hgrn_latent" if latent else "hgrn_prompt",
    )(*args)


def _rope_tables():
    pos = np.arange(LATENT_LEN)
    row, colp = pos // GRID_W, pos % GRID_W
    inv = ROPE_THETA ** (-np.arange(ROPE_PAIRS, dtype=np.float32) / ROPE_PAIRS)
    inv = inv.astype(np.float32)
    ang_r = (row.astype(np.float32)[:, None] * inv).astype(np.float32)
    ang_c = (colp.astype(np.float32)[:, None] * inv).astype(np.float32)
    cos = np.concatenate([np.cos(ang_r), np.cos(ang_r), np.cos(ang_c), np.cos(ang_c)], axis=1)
    sin = np.concatenate([-np.sin(ang_r), np.sin(ang_r), -np.sin(ang_c), np.sin(ang_c)], axis=1)
    return cos.astype(np.float32), sin.astype(np.float32)


def _head_mean_matrix(width):
    idx = np.arange(width) // HEAD_DIM
    return jnp.asarray((idx[:, None] == idx[None, :]).astype(np.float32) / HEAD_DIM).astype(BF16)


def _attn_kernel(*refs, latent):
    if latent:
        (q_ref, k_ref, v_ref, qg_ref, kg_ref, gq_ref, gk_ref, cosq_ref, sinq_ref, cosk_ref, sink_ref,
         ck_ref, cv_ref, o_ref) = refs
    else:
        (q_ref, k_ref, v_ref, qg_ref, kg_ref, gq_ref, gk_ref, o_ref, kout_ref, vout_ref) = refs
    pair_w = 2 * HEAD_DIM

    def head_norm(x, mean_ref, gain):
        sq = x * x
        hi = sq.astype(BF16)
        lo = (sq - hi.astype(F32)).astype(BF16)
        ms = jnp.dot(hi, mean_ref[...], preferred_element_type=F32)
        ms = ms + jnp.dot(lo, mean_ref[...], preferred_element_type=F32)
        return x * lax.rsqrt(ms + EPS) * gain

    def rope(x, cos, sin):
        n = x.shape[1]
        lane = lax.broadcasted_iota(jnp.int32, x.shape, 1)
        first_of_pair = (lane // ROPE_PAIRS) % 2 == 0
        swapped = jnp.where(first_of_pair, pltpu.roll(x, n - ROPE_PAIRS, axis=1), pltpu.roll(x, ROPE_PAIRS, axis=1))
        return x * cos + swapped * sin

    def attend(rows, seq_idx=None):
        q = head_norm(q_ref[rows, :].astype(F32), gq_ref, qg_ref[...])
        k = head_norm(k_ref[rows, :].astype(F32), gk_ref, kg_ref[...])
        if latent:
            q = rope(q, cosq_ref[...], sinq_ref[...])
            k = rope(k, cosk_ref[...], sink_ref[...])
        q = q * (HEAD_DIM ** -0.5)
        v = v_ref[rows, :].astype(F32)
        n_q = q.shape[0]
        low_kv = lax.broadcasted_iota(jnp.int32, k.shape, 1) < HEAD_DIM
        low_q = lax.broadcasted_iota(jnp.int32, (n_q, pair_w), 1) < HEAD_DIM
        k_swapped = pltpu.roll(k, HEAD_DIM, axis=1)
        v_swapped = pltpu.roll(v, HEAD_DIM, axis=1)
        if not latent:
            kout_ref[seq_idx] = k.T
            vout_ref[seq_idx] = v.T
        nt = (((1,), (1,)), ((), ()))
        for j in range(KV_HEADS):
            kd = (jnp.where(low_kv, k, k_swapped) if j == 0 else jnp.where(low_kv, k_swapped, k)).astype(BF16)
            vd = (jnp.where(low_kv, v, v_swapped) if j == 0 else jnp.where(low_kv, v_swapped, v)).astype(BF16)
            tiles = range(j * Q_PER_KV // 2, (j + 1) * Q_PER_KV // 2)
            parts = []
            for t in tiles:
                qt = q[:, t * pair_w:(t + 1) * pair_w]
                parts += [jnp.where(low_q, qt, 0.0), jnp.where(low_q, 0.0, qt)]
            qs = jnp.concatenate(parts, axis=0).astype(BF16)
            s_new = lax.dot_general(qs, kd, nt, preferred_element_type=F32)
            m = jnp.max(s_new, axis=-1, keepdims=True)
            if latent:
                ckd = jnp.concatenate([ck_ref[j], ck_ref[j]], axis=1).astype(BF16)
                cvd = jnp.concatenate([cv_ref[j], cv_ref[j]], axis=1).astype(BF16)
                s_old = lax.dot_general(qs, ckd, nt, preferred_element_type=F32)
                m = jnp.maximum(m, jnp.max(s_old, axis=-1, keepdims=True))
            p_new = jnp.exp(s_new - m)
            den = jnp.sum(p_new, axis=-1, keepdims=True)
            acc = jnp.dot(p_new.astype(BF16), vd, preferred_element_type=F32)
            if latent:
                p_old = jnp.exp(s_old - m)
                den = den + jnp.sum(p_old, axis=-1, keepdims=True)
                acc = acc + jnp.dot(p_old.astype(BF16), cvd, preferred_element_type=F32)
            out = acc / den
            for i, t in enumerate(tiles):
                lo_head = out[(2 * i) * n_q:(2 * i + 1) * n_q, :]
                hi_head = out[(2 * i + 1) * n_q:(2 * i + 2) * n_q, :]
                o_ref[rows, t * pair_w:(t + 1) * pair_w] = jnp.where(low_q, lo_head, hi_head).astype(o_ref.dtype)

    if latent:
        attend(slice(None))
    else:
        seq = PROMPT_LEN

        def one_sequence(s, carry):
            attend(pl.ds(pl.multiple_of(s * seq, seq), seq), s)
            return carry

        lax.fori_loop(0, q_ref.shape[0] // seq, one_sequence, 0)


def _attn_common_args(qn_g, kn_g):
    q_w, kv_w = Q_HEADS * HEAD_DIM, KV_HEADS * HEAD_DIM
    return (jnp.tile(qn_g, Q_HEADS).reshape(1, q_w), jnp.tile(kn_g, KV_HEADS).reshape(1, kv_w),
            _head_mean_matrix(q_w), _head_mean_matrix(kv_w))


def _attention_prompt(z, qn_g, kn_g):
    seqs = 4
    L = seqs * PROMPT_LEN
    cache_shape = jax.ShapeDtypeStruct((N_PROMPT_SEQ, KV_HEADS * HEAD_DIM, PROMPT_LEN), F32)
    cache_spec = pl.BlockSpec((seqs, KV_HEADS * HEAD_DIM, PROMPT_LEN), lambda s: (s, 0, 0))
    q_w, kv_w = Q_HEADS * HEAD_DIM, KV_HEADS * HEAD_DIM
    q_col = (5 * A_WIDTH) // q_w
    k_col = (5 * A_WIDTH + q_w) // kv_w
    const = lambda r, c: pl.BlockSpec((r, c), lambda s: (0, 0))
    return pl.pallas_call(
        functools.partial(_attn_kernel, latent=False),
        grid=(N_PROMPT_TOK // L,),
        in_specs=[
            pl.BlockSpec((L, q_w), lambda s: (s, q_col)),
            pl.BlockSpec((L, kv_w), lambda s: (s, k_col)),
            pl.BlockSpec((L, kv_w), lambda s: (s, k_col + 1)),
            const(1, q_w), const(1, kv_w), const(q_w, q_w), const(kv_w, kv_w),
        ],
        out_specs=(pl.BlockSpec((L, q_w), lambda s: (s, 0)), cache_spec, cache_spec),
        out_shape=(jax.ShapeDtypeStruct((N_PROMPT_TOK, q_w), BF16), cache_shape, cache_shape),
        compiler_params=_params("arbitrary"),
        name="attn_prompt",
    )(z, z, z, *_attn_common_args(qn_g, kn_g))


def _attention_latent(z, qn_g, kn_g, cache_k, cache_v):
    L = LATENT_LEN
    nqb = L // Q_BLOCK
    q_w, kv_w = Q_HEADS * HEAD_DIM, KV_HEADS * HEAD_DIM
    q_col = (5 * A_WIDTH) // q_w
    k_col = (5 * A_WIDTH + q_w) // kv_w
    qrow0 = N_PROMPT_TOK // Q_BLOCK
    krow0 = N_PROMPT_TOK // L
    cos, sin = _rope_tables()
    cos_q, sin_q = jnp.asarray(np.tile(cos, (1, Q_HEADS))), jnp.asarray(np.tile(sin, (1, Q_HEADS)))
    cos_k, sin_k = jnp.asarray(np.tile(cos, (1, KV_HEADS))), jnp.asarray(np.tile(sin, (1, KV_HEADS)))
    const = lambda r, c: pl.BlockSpec((r, c), lambda s, b: (0, 0))
    cache_spec = pl.BlockSpec((None, None, KV_HEADS, PAST_LEN, HEAD_DIM), lambda s, b: (s, 0, 0, 0, 0))
    return pl.pallas_call(
        functools.partial(_attn_kernel, latent=True),
        grid=(N_LATENT_SEQ, nqb),
        in_specs=[
            pl.BlockSpec((Q_BLOCK, q_w), lambda s, b: (qrow0 + s * nqb + b, q_col)),
            pl.BlockSpec((L, kv_w), lambda s, b: (krow0 + s, k_col)),
            pl.BlockSpec((L, kv_w), lambda s, b: (krow0 + s, k_col + 1)),
            const(1, q_w), const(1, kv_w), const(q_w, q_w), const(kv_w, kv_w),
            pl.BlockSpec((Q_BLOCK, q_w), lambda s, b: (b, 0)),
            pl.BlockSpec((Q_BLOCK, q_w), lambda s, b: (b, 0)),
            const(L, kv_w), const(L, kv_w),
            cache_spec, cache_spec,
        ],
        out_specs=pl.BlockSpec((Q_BLOCK, q_w), lambda s, b: (s * nqb + b, 0)),
        out_shape=jax.ShapeDtypeStruct((N_LATENT_TOK, q_w), BF16),
        compiler_params=_params("arbitrary", "arbitrary"),
        name="attn_latent",
    )(z, z, z, *_attn_common_args(qn_g, kn_g), cos_q, sin_q, cos_k, sin_k, cache_k, cache_v)


def _out_proj_kernel(*refs, n_act, n_x):
    a_refs = refs[:2 * n_act]
    x_refs = refs[2 * n_act:2 * n_act + n_x]
    g_ref, mod_ref, rw_ref, w_ref, xo_ref, h_ref, lg_ref, wb_ref, rws_ref = refs[2 * n_act + n_x:]
    _cast_once(w_ref, wb_ref)

    @pl.when(pl.program_id(0) == 0)
    def _():
        rw = rw_ref[...]
        hi = rw.astype(BF16).astype(F32)
        lo = (rw - hi).astype(BF16).astype(F32)
        rws_ref[...] = (hi + pltpu.roll(lo, N_EXPERTS, axis=1)).astype(BF16)

    mod = mod_ref[...]
    n = OUT_PROJ_SUB_ROWS

    def sub_block(r, carry):
        rows = pl.ds(pl.multiple_of(r * n, n), n)
        acc = None
        k0 = 0
        for ap_ref, al_ref in zip(a_refs[0::2], a_refs[1::2]):
            k1 = k0 + ap_ref.shape[1]
            part = jnp.dot(_select_trunk(ap_ref, al_ref, rows), wb_ref[k0:k1, :], preferred_element_type=F32)
            acc = part if acc is None else acc + part
            k0 = k1
        x_in = x_refs[0][rows, :] if n_x == 1 else _select_trunk(*x_refs, rows)
        x = x_in + mod[2:3, :] * acc
        xo_ref[rows, :] = x
        h = _modulated_norm(x, g_ref[...], mod, 3, 4)
        h_ref[rows, :] = _pack_rows(h)
        h_hi = h.astype(BF16)
        h_lo = (h - h_hi.astype(F32)).astype(BF16)
        both = jnp.dot(jnp.concatenate([h_hi, h_lo], axis=0), rws_ref[...], preferred_element_type=F32)
        from_hi, from_lo = both[:n], both[n:]
        lg = from_hi + pltpu.roll(from_hi, ROUTER_LANES - N_EXPERTS, axis=1) + from_lo
        lg_ref[:, rows] = lg.T[:N_EXPERTS, :]
        return carry

    lax.fori_loop(0, xo_ref.shape[0] // n, sub_block, 0)


def _out_proj(acts, w, xs, g, mod_l, router_wp, block_rows=1024):
    tok = lambda width: pl.BlockSpec((block_rows, width), lambda i: (i, 0))
    in_specs = [spec for ap, _ in acts for spec in _trunk_specs(block_rows, ap.shape[1])]
    in_specs += [tok(D_MODEL)] if len(xs) == 1 else list(_trunk_specs(block_rows, D_MODEL))
    in_specs += [_resident((1, D_MODEL)), _mod_spec(block_rows), _resident((D_MODEL, ROUTER_LANES)),
                 _resident(w.shape)]
    return pl.pallas_call(
        functools.partial(_out_proj_kernel, n_act=len(acts), n_x=len(xs)),
        grid=(N_TOK // block_rows,),
        in_specs=in_specs,
        out_specs=(tok(D_MODEL), tok(ROW_WORDS), pl.BlockSpec((N_EXPERTS, block_rows), lambda i: (0, i))),
        out_shape=(jax.ShapeDtypeStruct((N_TOK, D_MODEL), F32),
                   jax.ShapeDtypeStruct((N_TOK, ROW_WORDS), jnp.int32),
                   jax.ShapeDtypeStruct((N_EXPERTS, N_TOK), F32)),
        scratch_shapes=[pltpu.VMEM(w.shape, BF16), pltpu.VMEM((D_MODEL, ROUTER_LANES), BF16)],
        compiler_params=_params("arbitrary"),
        name="out_proj",
    )(*[a for pair in acts for a in pair], *xs, g.reshape(1, D_MODEL), mod_l, router_wp, w)


def _router_kernel(lg_ref, rb_ref, pos_ref, w_ref, plan_ref, rank_ref):
    lg = lg_ref[...]
    ex = jnp.exp(lg - jnp.max(lg, axis=0, keepdims=True))
    scores = ex / jnp.sum(ex, axis=0, keepdims=True)
    biased = scores + rb_ref[...]
    rows = [biased[e:e + 1, :] for e in range(N_EXPERTS)]
    selected = []
    group_score = []
    for gi in range(N_GROUPS):
        r = rows[gi * EXPERTS_PER_GROUP:(gi + 1) * EXPERTS_PER_GROUP]
        total = None
        for i in range(EXPERTS_PER_GROUP):
            rank = None
            for j in range(EXPERTS_PER_GROUP):
                if j == i:
                    continue
                ahead = (r[j] > r[i]) if j > i else (r[j] >= r[i])
                ahead = jnp.where(ahead, 1.0, 0.0)
                rank = ahead if rank is None else rank + ahead
            sel = rank < 1.5
            selected.append(sel)
            contrib = jnp.where(sel, r[i], 0.0)
            total = contrib if total is None else total + contrib
        group_score.append(total)
    best = group_score[0]
    best_group = jnp.zeros_like(best)
    for gi in range(1, N_GROUPS):
        better = group_score[gi] > best
        best_group = jnp.where(better, float(gi), best_group)
        best = jnp.where(better, group_score[gi], best)
    picked = []
    chosen = []
    den = None
    for e in range(N_EXPERTS):
        in_group = best_group == float(e // EXPERTS_PER_GROUP)
        use = jnp.where(selected[e], jnp.where(in_group, 1.0, 0.0), 0.0)
        w = use * scores[e:e + 1, :]
        chosen.append(use)
        picked.append(w)
        den = w if den is None else den + w
    lanes = 128
    n_blk = N_TOK // lanes
    li = lax.broadcasted_iota(jnp.int32, (lanes, lanes), 0)
    lj = lax.broadcasted_iota(jnp.int32, (lanes, lanes), 1)
    prefix = jnp.where(li <= lj, 1.0, 0.0).astype(BF16)
    carry = jnp.zeros((N_EXPERTS, 1), F32)
    for blk in range(n_blk):
        cols = slice(blk * lanes, (blk + 1) * lanes)
        m = jnp.concatenate([chosen[e][:, cols] for e in range(N_EXPERTS)], axis=0)
        incl = jnp.dot(m.astype(BF16), prefix, preferred_element_type=F32)
        rank_ref[:, cols] = incl - m + carry
        carry = carry + incl[:, lanes - 1:lanes]
    count = carry
    padded = jnp.floor((count + float(MOE_TILE - 1)) * (1.0 / MOE_TILE)) * float(MOE_TILE)
    erow = lax.broadcasted_iota(jnp.int32, (N_EXPERTS, 1), 0)
    offset = jnp.zeros((N_EXPERTS, 1), F32)
    for e in range(N_EXPERTS - 1):
        offset = offset + jnp.where(erow > e, padded[e:e + 1, :], 0.0)
    seen = jnp.zeros_like(den)
    pos_a = jnp.zeros_like(den)
    pos_b = jnp.zeros_like(den)
    w_a = jnp.zeros_like(den)
    w_b = jnp.zeros_like(den)
    for e in range(N_EXPERTS):
        pos_e = rank_ref[e:e + 1, :] + offset[e:e + 1, :]
        gate_e = picked[e] / den
        first = jnp.where(seen < 0.5, chosen[e], 0.0) > 0.5
        second = jnp.where(seen > 0.5, chosen[e], 0.0) > 0.5
        pos_a = jnp.where(first, pos_e, pos_a)
        w_a = jnp.where(first, gate_e, w_a)
        pos_b = jnp.where(second, pos_e, pos_b)
        w_b = jnp.where(second, gate_e, w_b)
        seen = seen + chosen[e]
    pos_ref[0:1, :] = pos_a.astype(jnp.int32)
    pos_ref[1:2, :] = pos_b.astype(jnp.int32)
    w_rows = jnp.concatenate([w_a, w_b, jnp.zeros((6, N_TOK), F32)], axis=0)
    ei = lax.broadcasted_iota(jnp.int32, (8, lanes), 0)
    ej = lax.broadcasted_iota(jnp.int32, (8, lanes), 1)
    eye = jnp.where(ei == ej, 1.0, 0.0).astype(BF16)
    tn = (((0,), (0,)), ((), ()))
    hi = w_rows.astype(BF16)
    r1 = w_rows - hi.astype(F32)
    mid = r1.astype(BF16)
    lo = (r1 - mid.astype(F32)).astype(BF16)
    w_cols = lax.dot_general(hi, eye, tn, preferred_element_type=F32)
    w_cols = w_cols + lax.dot_general(mid, eye, tn, preferred_element_type=F32)
    w_cols = w_cols + lax.dot_general(lo, eye, tn, preferred_element_type=F32)
    w_ref[...] = w_cols[:, :TOP_K]
    start = (lax.broadcasted_iota(jnp.int32, (N_EXPERTS, lanes), 1) * MOE_TILE).astype(F32)
    end = offset + padded
    tile_expert = jnp.sum(jnp.where(end <= start, 1.0, 0.0), axis=0, keepdims=True)
    inside = (offset <= start) & (start < end)
    real = jnp.clip(count - (start - offset), 0.0, float(MOE_TILE))
    tile_rows = jnp.sum(jnp.where(inside, real, 0.0), axis=0, keepdims=True)
    plan_ref[0:1, :] = jnp.minimum(tile_expert, float(N_EXPERTS - 1)).astype(jnp.int32)
    plan_ref[1:2, :] = tile_rows.astype(jnp.int32)


def _router(logits_t, router_b):
    whole = lambda shape: pl.BlockSpec(shape, lambda i: (0, 0))
    return pl.pallas_call(
        _router_kernel,
        grid=(1,),
        in_specs=[whole((N_EXPERTS, N_TOK)), whole((N_EXPERTS, 1))],
        out_specs=(whole((2, N_TOK)), whole((N_TOK, TOP_K)), whole((2, 128))),
        out_shape=(jax.ShapeDtypeStruct((2, N_TOK), jnp.int32),
                   jax.ShapeDtypeStruct((N_TOK, TOP_K), F32),
                   jax.ShapeDtypeStruct((2, 128), jnp.int32)),
        scratch_shapes=[pltpu.VMEM((N_EXPERTS, N_TOK), F32)],
        compiler_params=_params("arbitrary"),
        name="router",
    )(logits_t, router_b.reshape(N_EXPERTS, 1))


def _sc_mesh():
    return plsc.VectorSubcoreMesh(core_axis_name="c", subcore_axis_name="s")


def _sc_worker_base():
    return (lax.axis_index("s") * SC_CORES + lax.axis_index("c")) * (N_TOK // SC_WORKERS)


def _moe_dispatch(h, pos_a, pos_b):
    n_chunks = N_TOK // SC_WORKERS // SC_CHUNK

    @functools.partial(
        pl.kernel, mesh=_sc_mesh(),
        out_type=jax.ShapeDtypeStruct((MOE_ROWS, ROW_WORDS), jnp.int32),
        scratch_types=[pltpu.VMEM((SC_CHUNK,), jnp.int32), pltpu.VMEM((SC_CHUNK,), jnp.int32),
                       pltpu.VMEM((SC_CHUNK, ROW_WORDS), jnp.int32)],
        name="moe_dispatch",
    )
    def run(h_hbm, pa_hbm, pb_hbm, xs_hbm, ia_v, ib_v, rows_v):
        base = _sc_worker_base()

        @pl.loop(0, n_chunks)
        def _(ci):
            tok = pl.ds(pl.multiple_of(base + ci * SC_CHUNK, SC_CHUNK), SC_CHUNK)
            pltpu.sync_copy(pa_hbm.at[tok], ia_v)
            pltpu.sync_copy(pb_hbm.at[tok], ib_v)
            pltpu.sync_copy(h_hbm.at[tok], rows_v)
            pltpu.sync_copy(rows_v, xs_hbm.at[ia_v])
            pltpu.sync_copy(rows_v, xs_hbm.at[ib_v])

    return run(h, pos_a, pos_b)


def _moe_collect(ys, pos_a, pos_b):
    n_chunks = N_TOK // SC_WORKERS // SC_CHUNK
    out = jax.ShapeDtypeStruct((N_TOK, ROW_WORDS), jnp.int32)

    @functools.partial(
        pl.kernel, mesh=_sc_mesh(), out_type=(out, out),
        scratch_types=[pltpu.VMEM((SC_CHUNK,), jnp.int32), pltpu.VMEM((SC_CHUNK,), jnp.int32),
                       pltpu.VMEM((SC_CHUNK, ROW_WORDS), jnp.int32)],
        name="moe_collect",
    )
    def run(ys_hbm, pa_hbm, pb_hbm, ya_hbm, yb_hbm, ia_v, ib_v, rows_v):
        base = _sc_worker_base()

        @pl.loop(0, n_chunks)
        def _(ci):
            tok = pl.ds(pl.multiple_of(base + ci * SC_CHUNK, SC_CHUNK), SC_CHUNK)
            pltpu.sync_copy(pa_hbm.at[tok], ia_v)
            pltpu.sync_copy(pb_hbm.at[tok], ib_v)
            pltpu.sync_copy(ys_hbm.at[ia_v], rows_v)
            pltpu.sync_copy(rows_v, ya_hbm.at[tok])
            pltpu.sync_copy(ys_hbm.at[ib_v], rows_v)
            pltpu.sync_copy(rows_v, yb_hbm.at[tok])

    return run(ys, pos_a, pos_b)


def _experts_kernel(plan_ref, xs_ref, wg_hbm, wu_hbm, wd_hbm, y_ref,
                    sg_ref, su_ref, sd_ref, wgb_ref, wub_ref, wdb_ref, sems, seg_ref, *, layer):
    j = pl.program_id(0)
    n_tiles = pl.num_programs(0)
    expert = plan_ref[j]
    n_real = plan_ref[PLAN_LANES + j]
    fresh = jnp.logical_or(j == 0, expert != plan_ref[jnp.maximum(j - 1, 0)])

    def weight_copies(e, slot):
        return (pltpu.make_async_copy(wg_hbm.at[layer, e], sg_ref.at[slot], sems.at[slot, 0]),
                pltpu.make_async_copy(wu_hbm.at[layer, e], su_ref.at[slot], sems.at[slot, 1]),
                pltpu.make_async_copy(wd_hbm.at[layer, e], sd_ref.at[slot], sems.at[slot, 2]))

    @pl.when(j == 0)
    def _():
        seg_ref[0] = 0

        @pl.when(n_real > 0)
        def _():
            for cp in weight_copies(expert, 0):
                cp.start()

    @pl.when(jnp.logical_and(n_real > 0, fresh))
    def _():
        slot = seg_ref[0] % 2
        for cp in weight_copies(expert, slot):
            cp.wait()
        wgb_ref[...] = sg_ref[slot].astype(BF16)
        wub_ref[...] = su_ref[slot].astype(BF16)
        wdb_ref[...] = sd_ref[slot].astype(BF16)
        nxt = lax.while_loop(lambda t: jnp.logical_and(t < n_tiles, plan_ref[jnp.minimum(t, n_tiles - 1)] == expert),
                             lambda t: t + 1, j + 1)
        nxt_c = jnp.minimum(nxt, n_tiles - 1)

        @pl.when(jnp.logical_and(nxt < n_tiles, plan_ref[PLAN_LANES + nxt_c] > 0))
        def _():
            for cp in weight_copies(plan_ref[nxt_c], 1 - slot):
                cp.start()

        seg_ref[0] = seg_ref[0] + 1

    @pl.when(n_real > 0)
    def _():
        row = lax.broadcasted_iota(jnp.int32, xs_ref.shape, 0)
        words = jnp.where(row < n_real, xs_ref[...], 0)
        x = _unpack_rows(words).astype(BF16)
        a = jnp.dot(x, wgb_ref[...], preferred_element_type=F32)
        b = jnp.dot(x, wub_ref[...], preferred_element_type=F32)
        hid = (a * jax.nn.sigmoid(a)) * b
        y_ref[...] = _pack_rows(jnp.dot(hid.astype(BF16), wdb_ref[...], preferred_element_type=F32))


def _experts(plan, xs, w_gate, w_up, w_down, layer):
    hbm = pl.BlockSpec(memory_space=pl.ANY)
    return pl.pallas_call(
        functools.partial(_experts_kernel, layer=layer),
        grid_spec=pltpu.PrefetchScalarGridSpec(
            num_scalar_prefetch=1,
            grid=(MOE_ROWS // MOE_TILE,),
            in_specs=[pl.BlockSpec((MOE_TILE, ROW_WORDS), lambda j, plan: (j, 0)), hbm, hbm, hbm],
            out_specs=pl.BlockSpec((MOE_TILE, ROW_WORDS), lambda j, plan: (j, 0)),
            scratch_shapes=[pltpu.VMEM((2, D_MODEL, D_EXPERT), F32), pltpu.VMEM((2, D_MODEL, D_EXPERT), F32),
                            pltpu.VMEM((2, D_EXPERT, D_MODEL), F32),
                            pltpu.VMEM((D_MODEL, D_EXPERT), BF16), pltpu.VMEM((D_MODEL, D_EXPERT), BF16),
                            pltpu.VMEM((D_EXPERT, D_MODEL), BF16),
                            pltpu.SemaphoreType.DMA((2, 3)), pltpu.SMEM((1,), jnp.int32)],
        ),
        out_shape=jax.ShapeDtypeStruct((MOE_ROWS, ROW_WORDS), jnp.int32),
        compiler_params=_params("arbitrary"),
        name="experts",
    )(plan, xs, w_gate, w_up, w_down)


def _combine_kernel(x_ref, ya_ref, yb_ref, wt_ref, mod_ref, o_ref):
    o_ref[...] = _moe_mix(x_ref, ya_ref, yb_ref, wt_ref, mod_ref)


def _combine(x, moe_out, mod_l, tok0, n_tok, block_rows=512):
    ya, yb, w_tok = moe_out
    b0 = tok0 // block_rows
    rows = lambda width: pl.BlockSpec((block_rows, width), lambda i: (b0 + i, 0))
    return pl.pallas_call(
        _combine_kernel,
        grid=(n_tok // block_rows,),
        in_specs=[rows(D_MODEL), rows(ROW_WORDS), rows(ROW_WORDS), rows(TOP_K),
                  pl.BlockSpec((None, 6, D_MODEL), lambda i: (_cond_of_token_block(b0 + i, block_rows), 0, 0))],
        out_specs=pl.BlockSpec((block_rows, D_MODEL), lambda i: (i, 0)),
        out_shape=jax.ShapeDtypeStruct((n_tok, D_MODEL), F32),
        compiler_params=_params("arbitrary"),
        name="combine",
    )(x, ya, yb, w_tok, mod_l)


def _moe(h, logits_t, router_b, w_gate, w_up, w_down, layer):
    pos, w, plan = _router(logits_t, router_b)
    xs = _moe_dispatch(h, pos[0], pos[1])
    ys = _experts(plan.reshape(-1), xs, w_gate, w_up, w_down, layer)
    ya, yb = _moe_collect(ys, pos[0], pos[1])
    return ya, yb, w


def _dft_tables(L):
    k = np.arange(L)[:, None]
    m = np.arange(L)[None, :]
    r = (k * m) % (2 * L)
    ang = np.pi * r.astype(np.float64) / L
    fc = np.cos(ang)
    fs = np.sin(ang)
    fs[0, :] = np.where(np.arange(L) % 2 == 0, 1.0, -1.0)
    wk = np.full((L, 1), 1.0 / L)
    wk[0, 0] = 0.5 / L
    gc = (fc * wk).T
    gs = (fs * wk).T
    return [jnp.asarray(t.astype(np.float32)).astype(BF16) for t in (fc, fs, gc, gs)]


def _filter_consts(L):
    t = np.linspace(0.0, 1.0, L, dtype=np.float32)[:, None]
    w = (np.float32(2.0 * np.pi) * np.arange(L, dtype=np.float32)[:, None] / np.float32(L)).astype(np.float32)
    fb = np.linspace(1e-4, HY_BANDS - 1, HY_BANDS, dtype=np.float32)[None, :]
    emb = np.concatenate([t, np.cos(fb * w), -np.sin(fb * w)], axis=-1).astype(np.float32)
    lo = math.log(HY_DECAY_TARGET) / HY_SLOW_PCT
    hi = math.log(HY_DECAY_TARGET) / HY_FAST_PCT
    deltas = np.abs(np.linspace(lo, hi, D_MODEL, dtype=np.float32))
    decay = np.exp(-t * deltas).astype(np.float32)
    return jnp.asarray(emb), jnp.asarray(decay)


def _filter_kernel(emb_ref, w1_ref, b1_ref, w2_ref, b2_ref, fr_ref, w3f_ref, w3b_ref, dec_ref,
                   fc_ref, fs_ref, kr_ref, q_ref, krn_ref, hd_ref):
    @pl.when(pl.program_id(0) == 0)
    def _():
        fr = fr_ref[...]
        h1 = jnp.sin(fr * (jnp.dot(emb_ref[...], w1_ref[...], precision=HIGHEST,
                                   preferred_element_type=F32) + b1_ref[...]))
        hd_ref[...] = jnp.sin(fr * (jnp.dot(h1, w2_ref[...], precision=HIGHEST,
                                            preferred_element_type=F32) + b2_ref[...]))

    hd = hd_ref[...]
    dec = dec_ref[...]
    f = jnp.dot(hd, w3f_ref[...], precision=HIGHEST, preferred_element_type=F32) * dec
    g = jnp.dot(hd, w3b_ref[...], precision=HIGHEST, preferred_element_type=F32) * dec
    row = lax.broadcasted_iota(jnp.int32, f.shape, 0)
    g = jnp.where(row == 0, 0.0, g)
    s = f + g
    d = f - g
    kr = jnp.dot(fc_ref[...], s.astype(BF16), preferred_element_type=F32)
    qq = jnp.dot(fs_ref[...], d.astype(BF16), preferred_element_type=F32)
    alt = jnp.where(row % 2 == 0, 1.0, -1.0)
    nyq = jnp.sum(alt * s, axis=0, keepdims=True)
    kr_ref[...] = kr
    q_ref[...] = jnp.where(row == 0, 0.0, qq)
    krn_ref[...] = jnp.where(row == 0, nyq, kr)


def _hyena_filter_spectrum(L, w1, b1, w2, b2, w3, freq, fc, fs, cblk=256):
    emb, decay = _filter_consts(L)
    ncb = D_MODEL // cblk
    n_emb = 128
    emb = jnp.pad(emb, ((0, 0), (0, n_emb - emb.shape[1])))
    w1 = jnp.pad(w1, ((0, n_emb - w1.shape[0]), (0, 0)))
    full = lambda shape: pl.BlockSpec(shape, lambda j: tuple(0 for _ in shape))
    out_sds = jax.ShapeDtypeStruct((L, D_MODEL), F32)
    out_spec = pl.BlockSpec((L, cblk), lambda j: (0, j))
    return pl.pallas_call(
        _filter_kernel,
        grid=(ncb,),
        in_specs=[
            full((L, n_emb)), full((n_emb, HY_FFN)), full((1, HY_FFN)), full((HY_FFN, HY_FFN)),
            full((1, HY_FFN)), full((1, HY_FFN)),
            pl.BlockSpec((HY_FFN, cblk), lambda j: (0, j)),
            pl.BlockSpec((HY_FFN, cblk), lambda j: (0, ncb + j)),
            pl.BlockSpec((L, cblk), lambda j: (0, j)),
            full((L, L)), full((L, L)),
        ],
        out_specs=(out_spec, out_spec, out_spec),
        out_shape=(out_sds, out_sds, out_sds),
        scratch_shapes=[pltpu.VMEM((L, HY_FFN), F32)],
        compiler_params=_params("arbitrary"),
        name=f"hyena_filter_{L}",
    )(emb, w1, b1.reshape(1, HY_FFN), w2, b2.reshape(1, HY_FFN), freq.reshape(1, HY_FFN), w3, w3, decay, fc, fs)


def _hyena_conv_kernel(x0_ref, x1_ref, v_ref, cw0_ref, cw1_ref, cwv_ref, cb0_ref, cb1_ref, cbv_ref,
                       kr_ref, q_ref, krn_ref, ds_ref, fc_ref, fs_ref, gc_ref, gs_ref, o_ref):
    L = fc_ref.shape[0]
    row = lax.broadcasted_iota(jnp.int32, (L, x0_ref.shape[1]), 0)

    def one_sequence(s, carry):
        rows = pl.ds(pl.multiple_of(s * L, L), L)

        def short_conv(u_ref, w_ref, b_ref):
            u = u_ref[rows, :].astype(F32)
            w = w_ref[...]
            prev = jnp.where(row == 0, 0.0, pltpu.roll(u, 1, axis=0))
            nxt = jnp.where(row == L - 1, 0.0, pltpu.roll(u, L - 1, axis=0))
            return prev * w[0:1, :] + u * w[1:2, :] + nxt * w[2:3, :] + b_ref[...]

        x0 = short_conv(x0_ref, cw0_ref, cb0_ref)
        x1 = short_conv(x1_ref, cw1_ref, cb1_ref)
        v = short_conv(v_ref, cwv_ref, cbv_ref)
        zz = v * x1
        zb = zz.astype(BF16)
        ur = jnp.dot(fc_ref[...], zb, preferred_element_type=F32)
        p = jnp.dot(fs_ref[...], zb, preferred_element_type=F32)
        qq = q_ref[...]
        yr = ur * kr_ref[...] - p * qq
        yw = ur * qq + p * krn_ref[...]
        y = jnp.dot(gc_ref[...], yr.astype(BF16), preferred_element_type=F32)
        y = y + jnp.dot(gs_ref[...], yw.astype(BF16), preferred_element_type=F32)
        o_ref[rows, :] = (x0 * (y + zz * ds_ref[...])).astype(o_ref.dtype)
        return carry

    lax.fori_loop(0, x0_ref.shape[0] // L, one_sequence, 0)


def _hyena_conv(u, conv_w, conv_b, dskip, spectrum, tables, *, latent):
    L = LATENT_LEN if latent else PROMPT_LEN
    n_seq = N_LATENT_SEQ if latent else N_PROMPT_SEQ
    cblk = 256 if latent else 512
    ncb = D_MODEL // cblk
    seqs = 1 if latent else 4
    row0 = (N_PROMPT_TOK // L) if latent else 0
    kr, qq, krn = spectrum
    fc, fs, gc, gs = tables

    def part(p, rows):
        if rows != L:
            return pl.BlockSpec((rows, cblk), lambda j, s: (0, p * ncb + j))
        return pl.BlockSpec((seqs * L, cblk), lambda j, s: (row0 // seqs + s, p * ncb + j))

    def const_cols(rows):
        return pl.BlockSpec((rows, cblk), lambda j, s: (0, j))

    mat = pl.BlockSpec((L, L), lambda j, s: (0, 0))
    conv_b2 = conv_b.reshape(1, 3 * D_MODEL)
    in_specs = [part(0, L), part(1, L), part(2, L),
                part(0, 3), part(1, 3), part(2, 3),
                part(0, 1), part(1, 1), part(2, 1),
                const_cols(L), const_cols(L), const_cols(L), const_cols(1),
                mat, mat, mat, mat]
    args = [u, u, u, conv_w, conv_w, conv_w, conv_b2, conv_b2, conv_b2,
            kr, qq, krn, dskip.reshape(1, D_MODEL), fc, fs, gc, gs]
    return pl.pallas_call(
        _hyena_conv_kernel,
        grid=(ncb, n_seq // seqs),
        in_specs=in_specs,
        out_specs=pl.BlockSpec((seqs * L, cblk), lambda j, s: (s, j)),
        out_shape=jax.ShapeDtypeStruct((n_seq * L, D_MODEL), BF16),
        compiler_params=_params("arbitrary", "arbitrary"),
        name="hyena_conv_latent" if latent else "hyena_conv_prompt",
    )(*args)


def kernel(x_prompt, x_sample, cache_k, cache_v, state_hgrn, c, c_ctx, norm_g, mod_w, mod_b, ab_in_w, hgrn_lb, hgrn_onorm_g, attn_qnorm_g, attn_knorm_g, ab_out_w, hy_in_w, hy_in_b, hy_conv_w, hy_conv_b, hy_f_w1, hy_f_b1, hy_f_w2, hy_f_b2, hy_f_w3, hy_f_freq, hy_dskip, hy_out_w, router_w, router_b, moe_w_gate, moe_w_up, moe_w_down):
    xp = x_prompt.reshape(N_PROMPT_TOK, D_MODEL)
    xl = x_sample.reshape(N_LATENT_TOK, D_MODEL)
    cond = jnp.concatenate([c_ctx[None, :], c, jnp.zeros((N_COND - 1 - N_LATENT_SEQ, D_MODEL), F32)], axis=0)
    mod = _modulation(cond, mod_w, mod_b)
    router_wp = jnp.pad(router_w, ((0, 0), (0, ROUTER_LANES - N_EXPERTS)))

    z = _in_proj0(xp, xl, norm_g[0, 0], mod[0], ab_in_w[0])
    oa_p, new_state = _hgrn(z, hgrn_lb, hgrn_onorm_g[0], None, latent=False)
    oa_l = _hgrn(z, hgrn_lb, hgrn_onorm_g[0], state_hgrn, latent=True)
    ob_p, k_fm, v_fm = _attention_prompt(z, attn_qnorm_g[0], attn_knorm_g[0])
    fm_shape = (N_PROMPT_SEQ, 1, KV_HEADS, HEAD_DIM, PROMPT_LEN)
    new_k = jnp.swapaxes(k_fm.reshape(fm_shape), -1, -2)
    new_v = jnp.swapaxes(v_fm.reshape(fm_shape), -1, -2)
    ob_l = _attention_latent(z, attn_qnorm_g[0], attn_knorm_g[0], cache_k, cache_v)
    x, h, logits_t = _out_proj([(oa_p, oa_l), (ob_p, ob_l)], ab_out_w[0], (xp, xl), norm_g[0, 1], mod[0],
                               router_wp)
    moe_out = _moe(h, logits_t, router_b, moe_w_gate, moe_w_up, moe_w_down, 0)

    x, u = _in_proj1(x, moe_out, mod[0], norm_g[1, 0], mod[1], hy_in_w[0], hy_in_b[0])
    pre = []
    for latent in (False, True):
        L = LATENT_LEN if latent else PROMPT_LEN
        tables = _dft_tables(L)
        spectrum = _hyena_filter_spectrum(L, hy_f_w1[0], hy_f_b1[0], hy_f_w2[0], hy_f_b2[0], hy_f_w3[0],
                                          hy_f_freq[0], tables[0], tables[1])
        pre.append(_hyena_conv(u, hy_conv_w[0], hy_conv_b[0], hy_dskip[0], spectrum, tables, latent=latent))
    x, h, logits_t = _out_proj([tuple(pre)], hy_out_w[0], (x,), norm_g[1, 1], mod[1], router_wp)
    moe_out = _moe(h, logits_t, router_b, moe_w_gate, moe_w_up, moe_w_down, 1)

    y_prompt = _combine(x, moe_out, mod[1], 0, N_PROMPT_TOK).reshape(N_PROMPT_SEQ, PROMPT_LEN, D_MODEL)
    y_sample = _combine(x, moe_out, mod[1], N_PROMPT_TOK, N_LATENT_TOK).reshape(N_LATENT_SEQ, LATENT_LEN, D_MODEL)
    return (y_prompt, y_sample, new_k, new_v, new_state)
```

```python
import functools
import math

import numpy as np
import jax
import jax.numpy as jnp
from jax import lax
from jax.experimental import pallas as pl
from jax.experimental.pallas import tpu as pltpu
from jax.experimental.pallas import tpu_sc as plsc

F32 = jnp.float32
BF16 = jnp.bfloat16
HIGHEST = lax.Precision.HIGHEST

D_MODEL = 1024
N_PROMPT_SEQ = 32
PROMPT_LEN = 256
N_LATENT_SEQ = 2
LATENT_LEN = 1024
PAST_LEN = 512
GRID_W = 64
N_PROMPT_TOK = N_PROMPT_SEQ * PROMPT_LEN
N_LATENT_TOK = N_LATENT_SEQ * LATENT_LEN
N_TOK = N_PROMPT_TOK + N_LATENT_TOK
N_COND = 8
EPS = 1e-6

A_WIDTH = 512
A_HEADS = 4
A_DK = 128
CHUNK = 64
HGRN_BLOCK = 128
HGRN_HEADS_PER_STEP = 4
HEAD_DIM = 64
Q_HEADS = 8
KV_HEADS = 2
Q_PER_KV = Q_HEADS // KV_HEADS
Q_BLOCK = 256
ROPE_THETA = 10000.0
ROPE_PAIRS = HEAD_DIM // 4
AB_IN = 5 * A_WIDTH + (Q_HEADS + 2 * KV_HEADS) * HEAD_DIM

HY_BANDS = 16
HY_FFN = 64
HY_DECAY_TARGET = 1e-2
HY_FAST_PCT = 0.3
HY_SLOW_PCT = 1.5

N_EXPERTS = 16
N_GROUPS = 4
EXPERTS_PER_GROUP = 4
TOP_K = 2
D_EXPERT = 512
ROUTER_LANES = 128
OUT_PROJ_SUB_ROWS = 256
MOE_TILE = 512
MOE_ROWS = N_TOK * TOP_K + N_EXPERTS * MOE_TILE
PLAN_LANES = 128

SC_CORES = 2
SC_WORKERS = 32
SC_TOKENS_PER_WORKER = N_TOK // SC_WORKERS
SC_CHUNK = 40
ROW_WORDS = D_MODEL // 2

VMEM_LIMIT = 56 * 1024 * 1024


def _params(*sem):
    return pltpu.CompilerParams(dimension_semantics=sem, vmem_limit_bytes=VMEM_LIMIT)


def _pack_rows(x):
    n = x.shape[1] // 2
    bits = pltpu.bitcast(x.astype(BF16).astype(F32), jnp.uint32)
    return pltpu.bitcast(bits[:, :n] | (bits[:, n:] >> 16), jnp.int32)


def _unpack_rows(p):
    bits = pltpu.bitcast(p, jnp.uint32)
    hi = pltpu.bitcast(bits & jnp.uint32(0xFFFF0000), F32)
    lo = pltpu.bitcast(bits << 16, F32)
    return jnp.concatenate([hi, lo], axis=1)


def _cond_of_token_block(i, block_rows):
    start = i * block_rows
    return jnp.where(start < N_PROMPT_TOK, 0, 1 + (start - N_PROMPT_TOK) // LATENT_LEN)


def _mod_kernel(cond_ref, w_ref, b_ref, o_ref):
    cnd = cond_ref[...]
    s = cnd * jax.nn.sigmoid(cnd)
    s_hi = s.astype(BF16)
    s_lo = (s - s_hi.astype(F32)).astype(BF16)
    w = w_ref[...]
    w_hi = w.astype(BF16)
    w_lo = (w - w_hi.astype(F32)).astype(BF16)
    acc = jnp.dot(s_hi, w_hi, preferred_element_type=F32)
    acc = acc + jnp.dot(s_lo, w_hi, preferred_element_type=F32)
    acc = acc + jnp.dot(s_hi, w_lo, preferred_element_type=F32)
    o_ref[...] = acc + b_ref[...]


def _modulation(cond, mod_w, mod_b):
    depth = mod_w.shape[0]
    n_chunk = 6
    out = pl.pallas_call(
        _mod_kernel,
        grid=(depth, n_chunk),
        in_specs=[
            pl.BlockSpec((N_COND, D_MODEL), lambda l, j: (0, 0)),
            pl.BlockSpec((None, D_MODEL, D_MODEL), lambda l, j: (l, 0, j)),
            pl.BlockSpec((None, 1, D_MODEL), lambda l, j: (l, 0, j)),
        ],
        out_specs=pl.BlockSpec((None, N_COND, D_MODEL), lambda l, j: (l, 0, j)),
        out_shape=jax.ShapeDtypeStruct((depth, N_COND, n_chunk * D_MODEL), F32),
        compiler_params=_params("arbitrary", "arbitrary"),
        name="modulation",
    )(cond, mod_w, mod_b.reshape(depth, 1, n_chunk * D_MODEL))
    return out.reshape(depth, N_COND, n_chunk, D_MODEL)


def _modulated_norm(x, g, mod, shift_row, scale_row):
    ms = jnp.mean(x * x, axis=-1, keepdims=True)
    y = x * lax.rsqrt(ms + EPS) * g
    return y * (1.0 + mod[scale_row:scale_row + 1, :]) + mod[shift_row:shift_row + 1, :]


def _trunk_specs(block_rows, width):
    n_prompt_blocks = N_PROMPT_TOK // block_rows
    return (pl.BlockSpec((block_rows, width), lambda i: (jnp.minimum(i, n_prompt_blocks - 1), 0)),
            pl.BlockSpec((block_rows, width), lambda i: (jnp.maximum(i - n_prompt_blocks, 0), 0)))


def _select_trunk(p_ref, l_ref, rows=slice(None)):
    block_rows = p_ref.shape[0]
    return jnp.where(pl.program_id(0) < N_PROMPT_TOK // block_rows, p_ref[rows, :], l_ref[rows, :])


def _cast_once(w_ref, wb_ref):
    @pl.when(pl.program_id(0) == 0)
    def _():
        wb_ref[...] = w_ref[...].astype(BF16)


def _resident(shape):
    return pl.BlockSpec(shape, lambda i: tuple(0 for _ in shape), pipeline_mode=pl.Buffered(1))


def _mod_spec(block_rows):
    return pl.BlockSpec((None, 6, D_MODEL), lambda i: (_cond_of_token_block(i, block_rows), 0, 0))


def _in_proj0_kernel(xp_ref, xl_ref, g_ref, mod_ref, w_ref, o_ref, wb_ref):
    _cast_once(w_ref, wb_ref)
    h = _modulated_norm(_select_trunk(xp_ref, xl_ref), g_ref[...], mod_ref[...], 0, 1)
    o_ref[...] = jnp.dot(h.astype(BF16), wb_ref[...], preferred_element_type=F32).astype(o_ref.dtype)


def _in_proj0(x_prompt, x_latent, g, mod_l, w, block_rows=512):
    n = w.shape[1]
    return pl.pallas_call(
        _in_proj0_kernel,
        grid=(N_TOK // block_rows,),
        in_specs=[*_trunk_specs(block_rows, D_MODEL), _resident((1, D_MODEL)), _mod_spec(block_rows),
                  _resident((D_MODEL, n))],
        out_specs=pl.BlockSpec((block_rows, n), lambda i: (i, 0)),
        out_shape=jax.ShapeDtypeStruct((N_TOK, n), BF16),
        scratch_shapes=[pltpu.VMEM((D_MODEL, n), BF16)],
        compiler_params=_params("arbitrary"),
        name="in_proj0",
    )(x_prompt, x_latent, g.reshape(1, D_MODEL), mod_l, w)


def _moe_mix(x_ref, ya_ref, yb_ref, wt_ref, mod_ref):
    wt = wt_ref[...]
    mix = wt[:, 0:1] * _unpack_rows(ya_ref[...]) + wt[:, 1:2] * _unpack_rows(yb_ref[...])
    return x_ref[...] + mod_ref[5:6, :] * mix


def _in_proj1_kernel(x_ref, ya_ref, yb_ref, wt_ref, modp_ref, g_ref, mod_ref, w_ref, b_ref, xo_ref, o_ref, wb_ref):
    _cast_once(w_ref, wb_ref)
    x = _moe_mix(x_ref, ya_ref, yb_ref, wt_ref, modp_ref)
    xo_ref[...] = x
    h = _modulated_norm(x, g_ref[...], mod_ref[...], 0, 1)
    u = jnp.dot(h.astype(BF16), wb_ref[...], preferred_element_type=F32) + b_ref[...]
    o_ref[...] = u.astype(o_ref.dtype)


def _in_proj1(x, moe_out, mod_prev, g, mod_l, w, bias, block_rows=512):
    ya, yb, w_tok = moe_out
    n = w.shape[1]
    tok = pl.BlockSpec((block_rows, D_MODEL), lambda i: (i, 0))
    packed = pl.BlockSpec((block_rows, ROW_WORDS), lambda i: (i, 0))
    return pl.pallas_call(
        _in_proj1_kernel,
        grid=(N_TOK // block_rows,),
        in_specs=[tok, packed, packed, pl.BlockSpec((block_rows, TOP_K), lambda i: (i, 0)), _mod_spec(block_rows),
                  _resident((1, D_MODEL)), _mod_spec(block_rows), _resident((D_MODEL, n)), _resident((1, n))],
        out_specs=(tok, pl.BlockSpec((block_rows, n), lambda i: (i, 0))),
        out_shape=(jax.ShapeDtypeStruct((N_TOK, D_MODEL), F32), jax.ShapeDtypeStruct((N_TOK, n), BF16)),
        scratch_shapes=[pltpu.VMEM((D_MODEL, n), BF16)],
        compiler_params=_params("arbitrary"),
        name="in_proj1",
    )(x, ya, yb, w_tok, mod_prev, g.reshape(1, D_MODEL), mod_l, w, bias.reshape(1, n))


def _hgrn_kernel(*refs, seq_len, with_state):
    if with_state:
        (q_ref, zf_ref, zb_ref, i_ref, ga_ref, lb_ref, og_ref, s0_ref, o_ref, of_ref, ob_ref) = refs
    else:
        (q_ref, zf_ref, zb_ref, i_ref, ga_ref, lb_ref, og_ref, o_ref, s_ref, of_ref, ob_ref) = refs
    n_blocks = seq_len // HGRN_BLOCK
    chunks_per_block = HGRN_BLOCK // CHUNK

    lbr = lb_ref[...]
    mx = jnp.maximum(lbr[0], lbr[1])
    e0 = jnp.exp(lbr[0] - mx)
    e1 = jnp.exp(lbr[1] - mx)
    lb = e0 / (e0 + e1)

    row = lax.broadcasted_iota(jnp.int32, (HGRN_BLOCK, HGRN_BLOCK), 0)
    col = lax.broadcasted_iota(jnp.int32, (HGRN_BLOCK, HGRN_BLOCK), 1)
    same_chunk = (row // CHUNK) == (col // CHUNK)
    nt = (((1,), (1,)), ((), ()))
    tn = (((0,), (0,)), ((), ()))

    def per_chunk_row(x, idx):
        return jnp.concatenate(
            [jnp.broadcast_to(x[n * CHUNK + idx:n * CHUNK + idx + 1, :], (CHUNK, x.shape[1]))
             for n in range(chunks_per_block)], axis=0)

    def in_chunk_cumsum(tri, x):
        hi = x.astype(BF16)
        lo = (x - hi.astype(F32)).astype(BF16)
        return jnp.dot(tri, hi, preferred_element_type=F32) + jnp.dot(tri, lo, preferred_element_type=F32)

    def block(blk, cols, st, z_ref, lbd, forward, out_ref):
        rows = slice(blk * HGRN_BLOCK, (blk + 1) * HGRN_BLOCK)
        keep = (same_chunk & (col <= row)) if forward else (same_chunk & (col >= row))
        tri = jnp.where(keep, 1.0, 0.0).astype(BF16)
        mid = CHUNK // 2 if forward else CHUNK - 1 - CHUNK // 2
        last = CHUNK - 1 if forward else 0
        f = lbd + (1.0 - lbd) * jax.nn.sigmoid(z_ref[rows, cols].astype(F32))
        lf = jnp.log(f)
        k = 1.0 - f
        q = q_ref[rows, cols].astype(F32)
        vb = i_ref[rows, cols].astype(BF16)
        b = in_chunk_cumsum(tri, lf)
        bm = per_chunk_row(b, mid)
        bl = per_chunk_row(b, last)
        qe = (q * jnp.exp(b - bm)).astype(BF16)
        ke = (k * jnp.exp(bm - b)).astype(BF16)
        att = lax.dot_general(qe, ke, nt, preferred_element_type=F32)
        att = jnp.where(keep, att, 0.0)
        o_intra = jnp.dot(att.astype(BF16), vb, preferred_element_type=F32)
        qb = (q * jnp.exp(b)).astype(BF16)
        ks = (k * jnp.exp(bl - b)).astype(BF16)
        decay = jnp.exp(bl)
        order = range(chunks_per_block) if forward else range(chunks_per_block - 1, -1, -1)
        o_inter = [None] * chunks_per_block
        for n in order:
            cr = slice(n * CHUNK, (n + 1) * CHUNK)
            o_inter[n] = lax.dot_general(qb[cr], st.astype(BF16), nt, preferred_element_type=F32)
            upd = lax.dot_general(vb[cr], ks[cr], tn, preferred_element_type=F32)
            st = st * decay[n * CHUNK:n * CHUNK + 1, :] + upd
        out_ref[rows, cols] = o_intra + jnp.concatenate(o_inter, axis=0)
        return st

    for hd in range(q_ref.shape[1] // A_DK):
        cols = slice(hd * A_DK, (hd + 1) * A_DK)
        if with_state:
            st_f, st_b = s0_ref[0, hd].T, s0_ref[1, hd].T
        else:
            st_f, st_b = jnp.zeros((A_DK, A_DK), F32), jnp.zeros((A_DK, A_DK), F32)
        for step in range(n_blocks):
            st_f = block(step, cols, st_f, zf_ref, lb[0:1, cols], True, of_ref)
            st_b = block(n_blocks - 1 - step, cols, st_b, zb_ref, lb[1:2, cols], False, ob_ref)
        if not with_state:
            s_ref[0, hd] = st_f.T
            s_ref[1, hd] = st_b.T
        o = of_ref[:, cols] + ob_ref[:, cols]
        o = o * lax.rsqrt(jnp.mean(o * o, axis=-1, keepdims=True) + EPS) * og_ref[:, cols]
        ga = ga_ref[:, cols].astype(F32)
        o_ref[:, cols] = (o * (ga * jax.nn.sigmoid(ga))).astype(o_ref.dtype)


def _hgrn(z, hgrn_lb, onorm_g, state, *, latent):
    seq_len = LATENT_LEN if latent else PROMPT_LEN
    n_seq = N_LATENT_SEQ if latent else N_PROMPT_SEQ
    row0 = (N_PROMPT_TOK // seq_len) if latent else 0

    hw = HGRN_HEADS_PER_STEP * A_DK
    n_hg = A_HEADS // HGRN_HEADS_PER_STEP

    def zspec(part):
        return pl.BlockSpec((seq_len, hw), lambda s, h: (row0 + s, part * n_hg + h))

    in_specs = [zspec(0), zspec(1), zspec(2), zspec(3), zspec(4),
                pl.BlockSpec((2, 2, hw), lambda s, h: (0, 0, h)),
                pl.BlockSpec((1, hw), lambda s, h: (0, h))]
    args = [z, z, z, z, z, hgrn_lb, onorm_g.reshape(1, A_WIDTH)]
    state_spec = pl.BlockSpec((None, None, 2, HGRN_HEADS_PER_STEP, A_DK, A_DK), lambda s, h: (s, 0, 0, h, 0, 0))
    o_shape = jax.ShapeDtypeStruct((n_seq * seq_len, A_WIDTH), BF16)
    o_spec = pl.BlockSpec((seq_len, hw), lambda s, h: (s, h))
    if latent:
        in_specs.append(state_spec)
        args.append(state)
        out_shape, out_specs = o_shape, o_spec
    else:
        out_shape = (o_shape, jax.ShapeDtypeStruct((n_seq, 1, 2, A_HEADS, A_DK, A_DK), F32))
        out_specs = (o_spec, state_spec)
    return pl.pallas_call(
        functools.partial(_hgrn_kernel, seq_len=seq_len, with_state=latent),
        grid=(n_seq, n_hg),
        in_specs=in_specs,
        out_specs=out_specs,
        out_shape=out_shape,
        scratch_shapes=[pltpu.VMEM((seq_len, hw), F32), pltpu.VMEM((seq_len, hw), F32)],
        compiler_params=_params("arbitrary", "arbitrary"),
        name="hgrn_latent" if latent else "hgrn_prompt",
    )(*args)


def _rope_tables():
    pos = np.arange(LATENT_LEN)
    row, colp = pos // GRID_W, pos % GRID_W
    inv = ROPE_THETA ** (-np.arange(ROPE_PAIRS, dtype=np.float32) / ROPE_PAIRS)
    inv = inv.astype(np.float32)
    ang_r = (row.astype(np.float32)[:, None] * inv).astype(np.float32)
    ang_c = (colp.astype(np.float32)[:, None] * inv).astype(np.float32)
    cos = np.concatenate([np.cos(ang_r), np.cos(ang_r), np.cos(ang_c), np.cos(ang_c)], axis=1)
    sin = np.concatenate([-np.sin(ang_r), np.sin(ang_r), -np.sin(ang_c), np.sin(ang_c)], axis=1)
    return cos.astype(np.float32), sin.astype(np.float32)


def _head_mean_matrix(width):
    idx = np.arange(width) // HEAD_DIM
    return jnp.asarray((idx[:, None] == idx[None, :]).astype(np.float32) / HEAD_DIM).astype(BF16)


def _attn_kernel(*refs, latent):
    if latent:
        (q_ref, k_ref, v_ref, qg_ref, kg_ref, gq_ref, gk_ref, cosq_ref, sinq_ref, cosk_ref, sink_ref,
         ck_ref, cv_ref, o_ref) = refs
    else:
        (q_ref, k_ref, v_ref, qg_ref, kg_ref, gq_ref, gk_ref, o_ref, kout_ref, vout_ref) = refs
    pair_w = 2 * HEAD_DIM

    def head_norm(x, mean_ref, gain):
        sq = x * x
        hi = sq.astype(BF16)
        lo = (sq - hi.astype(F32)).astype(BF16)
        ms = jnp.dot(hi, mean_ref[...], preferred_element_type=F32)
        ms = ms + jnp.dot(lo, mean_ref[...], preferred_element_type=F32)
        return x * lax.rsqrt(ms + EPS) * gain

    def rope(x, cos, sin):
        n = x.shape[1]
        lane = lax.broadcasted_iota(jnp.int32, x.shape, 1)
        first_of_pair = (lane // ROPE_PAIRS) % 2 == 0
        swapped = jnp.where(first_of_pair, pltpu.roll(x, n - ROPE_PAIRS, axis=1), pltpu.roll(x, ROPE_PAIRS, axis=1))
        return x * cos + swapped * sin

    def attend(rows, seq_idx=None):
        q = head_norm(q_ref[rows, :].astype(F32), gq_ref, qg_ref[...])
        k = head_norm(k_ref[rows, :].astype(F32), gk_ref, kg_ref[...])
        if latent:
            q = rope(q, cosq_ref[...], sinq_ref[...])
            k = rope(k, cosk_ref[...], sink_ref[...])
        q = q * (HEAD_DIM ** -0.5)
        v = v_ref[rows, :].astype(F32)
        n_q = q.shape[0]
        low_kv = lax.broadcasted_iota(jnp.int32, k.shape, 1) < HEAD_DIM
        low_q = lax.broadcasted_iota(jnp.int32, (n_q, pair_w), 1) < HEAD_DIM
        k_swapped = pltpu.roll(k, HEAD_DIM, axis=1)
        v_swapped = pltpu.roll(v, HEAD_DIM, axis=1)
        if not latent:
            kout_ref[seq_idx] = k.T
            vout_ref[seq_idx] = v.T
        nt = (((1,), (1,)), ((), ()))
        for j in range(KV_HEADS):
            kd = (jnp.where(low_kv, k, k_swapped) if j == 0 else jnp.where(low_kv, k_swapped, k)).astype(BF16)
            vd = (jnp.where(low_kv, v, v_swapped) if j == 0 else jnp.where(low_kv, v_swapped, v)).astype(BF16)
            tiles = range(j * Q_PER_KV // 2, (j + 1) * Q_PER_KV // 2)
            parts = []
            for t in tiles:
                qt = q[:, t * pair_w:(t + 1) * pair_w]
                parts += [jnp.where(low_q, qt, 0.0), jnp.where(low_q, 0.0, qt)]
            qs = jnp.concatenate(parts, axis=0).astype(BF16)
            s_new = lax.dot_general(qs, kd, nt, preferred_element_type=F32)
            m = jnp.max(s_new, axis=-1, keepdims=True)
            if latent:
                ckd = jnp.concatenate([ck_ref[j], ck_ref[j]], axis=1).astype(BF16)
                cvd = jnp.concatenate([cv_ref[j], cv_ref[j]], axis=1).astype(BF16)
                s_old = lax.dot_general(qs, ckd, nt, preferred_element_type=F32)
                m = jnp.maximum(m, jnp.max(s_old, axis=-1, keepdims=True))
            p_new = jnp.exp(s_new - m)
            den = jnp.sum(p_new, axis=-1, keepdims=True)
            acc = jnp.dot(p_new.astype(BF16), vd, preferred_element_type=F32)
            if latent:
                p_old = jnp.exp(s_old - m)
                den = den + jnp.sum(p_old, axis=-1, keepdims=True)
                acc = acc + jnp.dot(p_old.astype(BF16), cvd, preferred_element_type=F32)
            out = acc / den
            for i, t in enumerate(tiles):
                lo_head = out[(2 * i) * n_q:(2 * i + 1) * n_q, :]
                hi_head = out[(2 * i + 1) * n_q:(2 * i + 2) * n_q, :]
                o_ref[rows, t * pair_w:(t + 1) * pair_w] = jnp.where(low_q, lo_head, hi_head).astype(o_ref.dtype)

    if latent:
        attend(slice(None))
    else:
        seq = PROMPT_LEN

        def one_sequence(s, carry):
            attend(pl.ds(pl.multiple_of(s * seq, seq), seq), s)
            return carry

        lax.fori_loop(0, q_ref.shape[0] // seq, one_sequence, 0)


def _attn_common_args(qn_g, kn_g):
    q_w, kv_w = Q_HEADS * HEAD_DIM, KV_HEADS * HEAD_DIM
    return (jnp.tile(qn_g, Q_HEADS).reshape(1, q_w), jnp.tile(kn_g, KV_HEADS).reshape(1, kv_w),
            _head_mean_matrix(q_w), _head_mean_matrix(kv_w))


def _attention_prompt(z, qn_g, kn_g):
    seqs = 4
    L = seqs * PROMPT_LEN
    cache_shape = jax.ShapeDtypeStruct((N_PROMPT_SEQ, KV_HEADS * HEAD_DIM, PROMPT_LEN), F32)
    cache_spec = pl.BlockSpec((seqs, KV_HEADS * HEAD_DIM, PROMPT_LEN), lambda s: (s, 0, 0))
    q_w, kv_w = Q_HEADS * HEAD_DIM, KV_HEADS * HEAD_DIM
    q_col = (5 * A_WIDTH) // q_w
    k_col = (5 * A_WIDTH + q_w) // kv_w
    const = lambda r, c: pl.BlockSpec((r, c), lambda s: (0, 0))
    return pl.pallas_call(
        functools.partial(_attn_kernel, latent=False),
        grid=(N_PROMPT_TOK // L,),
        in_specs=[
            pl.BlockSpec((L, q_w), lambda s: (s, q_col)),
            pl.BlockSpec((L, kv_w), lambda s: (s, k_col)),
            pl.BlockSpec((L, kv_w), lambda s: (s, k_col + 1)),
            const(1, q_w), const(1, kv_w), const(q_w, q_w), const(kv_w, kv_w),
        ],
        out_specs=(pl.BlockSpec((L, q_w), lambda s: (s, 0)), cache_spec, cache_spec),
        out_shape=(jax.ShapeDtypeStruct((N_PROMPT_TOK, q_w), BF16), cache_shape, cache_shape),
        compiler_params=_params("arbitrary"),
        name="attn_prompt",
    )(z, z, z, *_attn_common_args(qn_g, kn_g))


def _attention_latent(z, qn_g, kn_g, cache_k, cache_v):
    L = LATENT_LEN
    nqb = L // Q_BLOCK
    q_w, kv_w = Q_HEADS * HEAD_DIM, KV_HEADS * HEAD_DIM
    q_col = (5 * A_WIDTH) // q_w
    k_col = (5 * A_WIDTH + q_w) // kv_w
    qrow0 = N_PROMPT_TOK // Q_BLOCK
    krow0 = N_PROMPT_TOK // L
    cos, sin = _rope_tables()
    cos_q, sin_q = jnp.asarray(np.tile(cos, (1, Q_HEADS))), jnp.asarray(np.tile(sin, (1, Q_HEADS)))
    cos_k, sin_k = jnp.asarray(np.tile(cos, (1, KV_HEADS))), jnp.asarray(np.tile(sin, (1, KV_HEADS)))
    const = lambda r, c: pl.BlockSpec((r, c), lambda s, b: (0, 0))
    cache_spec = pl.BlockSpec((None, None, KV_HEADS, PAST_LEN, HEAD_DIM), lambda s, b: (s, 0, 0, 0, 0))
    return pl.pallas_call(
        functools.partial(_attn_kernel, latent=True),
        grid=(N_LATENT_SEQ, nqb),
        in_specs=[
            pl.BlockSpec((Q_BLOCK, q_w), lambda s, b: (qrow0 + s * nqb + b, q_col)),
            pl.BlockSpec((L, kv_w), lambda s, b: (krow0 + s, k_col)),
            pl.BlockSpec((L, kv_w), lambda s, b: (krow0 + s, k_col + 1)),
            const(1, q_w), const(1, kv_w), const(q_w, q_w), const(kv_w, kv_w),
            pl.BlockSpec((Q_BLOCK, q_w), lambda s, b: (b, 0)),
            pl.BlockSpec((Q_BLOCK, q_w), lambda s, b: (b, 0)),
            const(L, kv_w), const(L, kv_w),
            cache_spec, cache_spec,
        ],
        out_specs=pl.BlockSpec((Q_BLOCK, q_w), lambda s, b: (s * nqb + b, 0)),
        out_shape=jax.ShapeDtypeStruct((N_LATENT_TOK, q_w), BF16),
        compiler_params=_params("arbitrary", "arbitrary"),
        name="attn_latent",
    )(z, z, z, *_attn_common_args(qn_g, kn_g), cos_q, sin_q, cos_k, sin_k, cache_k, cache_v)


def _out_proj_kernel(*refs, n_act, n_x):
    a_refs = refs[:2 * n_act]
    x_refs = refs[2 * n_act:2 * n_act + n_x]
    g_ref, mod_ref, rw_ref, w_ref, xo_ref, h_ref, lg_ref, wb_ref, rws_ref = refs[2 * n_act + n_x:]
    _cast_once(w_ref, wb_ref)

    @pl.when(pl.program_id(0) == 0)
    def _():
        rw = rw_ref[...]
        hi = rw.astype(BF16).astype(F32)
        lo = (rw - hi).astype(BF16).astype(F32)
        rws_ref[...] = (hi + pltpu.roll(lo, N_EXPERTS, axis=1)).astype(BF16)

    mod = mod_ref[...]
    n = OUT_PROJ_SUB_ROWS

    def sub_block(r, carry):
        rows = pl.ds(pl.multiple_of(r * n, n), n)
        acc = None
        k0 = 0
        for ap_ref, al_ref in zip(a_refs[0::2], a_refs[1::2]):
            k1 = k0 + ap_ref.shape[1]
            part = jnp.dot(_select_trunk(ap_ref, al_ref, rows), wb_ref[k0:k1, :], preferred_element_type=F32)
            acc = part if acc is None else acc + part
            k0 = k1
        x_in = x_refs[0][rows, :] if n_x == 1 else _select_trunk(*x_refs, rows)
        x = x_in + mod[2:3, :] * acc
        xo_ref[rows, :] = x
        h = _modulated_norm(x, g_ref[...], mod, 3, 4)
        h_ref[rows, :] = _pack_rows(h)
        h_hi = h.astype(BF16)
        h_lo = (h - h_hi.astype(F32)).astype(BF16)
        both = jnp.dot(jnp.concatenate([h_hi, h_lo], axis=0), rws_ref[...], preferred_element_type=F32)
        from_hi, from_lo = both[:n], both[n:]
        lg = from_hi + pltpu.roll(from_hi, ROUTER_LANES - N_EXPERTS, axis=1) + from_lo
        lg_ref[:, rows] = lg.T[:N_EXPERTS, :]
        return carry

    lax.fori_loop(0, xo_ref.shape[0] // n, sub_block, 0)


def _out_proj(acts, w, xs, g, mod_l, router_wp, block_rows=1024):
    tok = lambda width: pl.BlockSpec((block_rows, width), lambda i: (i, 0))
    in_specs = [spec for ap, _ in acts for spec in _trunk_specs(block_rows, ap.shape[1])]
    in_specs += [tok(D_MODEL)] if len(xs) == 1 else list(_trunk_specs(block_rows, D_MODEL))
    in_specs += [_resident((1, D_MODEL)), _mod_spec(block_rows), _resident((D_MODEL, ROUTER_LANES)),
                 _resident(w.shape)]
    return pl.pallas_call(
        functools.partial(_out_proj_kernel, n_act=len(acts), n_x=len(xs)),
        grid=(N_TOK // block_rows,),
        in_specs=in_specs,
        out_specs=(tok(D_MODEL), tok(ROW_WORDS), pl.BlockSpec((N_EXPERTS, block_rows), lambda i: (0, i))),
        out_shape=(jax.ShapeDtypeStruct((N_TOK, D_MODEL), F32),
                   jax.ShapeDtypeStruct((N_TOK, ROW_WORDS), jnp.int32),
                   jax.ShapeDtypeStruct((N_EXPERTS, N_TOK), F32)),
        scratch_shapes=[pltpu.VMEM(w.shape, BF16), pltpu.VMEM((D_MODEL, ROUTER_LANES), BF16)],
        compiler_params=_params("arbitrary"),
        name="out_proj",
    )(*[a for pair in acts for a in pair], *xs, g.reshape(1, D_MODEL), mod_l, router_wp, w)


def _router_kernel(lg_ref, rb_ref, pos_ref, w_ref, plan_ref, rank_ref):
    lg = lg_ref[...]
    ex = jnp.exp(lg - jnp.max(lg, axis=0, keepdims=True))
    scores = ex / jnp.sum(ex, axis=0, keepdims=True)
    biased = scores + rb_ref[...]
    rows = [biased[e:e + 1, :] for e in range(N_EXPERTS)]
    selected = []
    group_score = []
    for gi in range(N_GROUPS):
        r = rows[gi * EXPERTS_PER_GROUP:(gi + 1) * EXPERTS_PER_GROUP]
        total = None
        for i in range(EXPERTS_PER_GROUP):
            rank = None
            for j in range(EXPERTS_PER_GROUP):
                if j == i:
                    continue
                ahead = (r[j] > r[i]) if j > i else (r[j] >= r[i])
                ahead = jnp.where(ahead, 1.0, 0.0)
                rank = ahead if rank is None else rank + ahead
            sel = rank < 1.5
            selected.append(sel)
            contrib = jnp.where(sel, r[i], 0.0)
            total = contrib if total is None else total + contrib
        group_score.append(total)
    best = group_score[0]
    best_group = jnp.zeros_like(best)
    for gi in range(1, N_GROUPS):
        better = group_score[gi] > best
        best_group = jnp.where(better, float(gi), best_group)
        best = jnp.where(better, group_score[gi], best)
    picked = []
    chosen = []
    den = None
    for e in range(N_EXPERTS):
        in_group = best_group == float(e // EXPERTS_PER_GROUP)
        use = jnp.where(selected[e], jnp.where(in_group, 1.0, 0.0), 0.0)
        w = use * scores[e:e + 1, :]
        chosen.append(use)
        picked.append(w)
        den = w if den is None else den + w
    lanes = 128
    n_blk = N_TOK // lanes
    li = lax.broadcasted_iota(jnp.int32, (lanes, lanes), 0)
    lj = lax.broadcasted_iota(jnp.int32, (lanes, lanes), 1)
    prefix = jnp.where(li <= lj, 1.0, 0.0).astype(BF16)
    carry = jnp.zeros((N_EXPERTS, 1), F32)
    for blk in range(n_blk):
        cols = slice(blk * lanes, (blk + 1) * lanes)
        m = jnp.concatenate([chosen[e][:, cols] for e in range(N_EXPERTS)], axis=0)
        incl = jnp.dot(m.astype(BF16), prefix, preferred_element_type=F32)
        rank_ref[:, cols] = incl - m + carry
        carry = carry + incl[:, lanes - 1:lanes]
    count = carry
    padded = jnp.floor((count + float(MOE_TILE - 1)) * (1.0 / MOE_TILE)) * float(MOE_TILE)
    erow = lax.broadcasted_iota(jnp.int32, (N_EXPERTS, 1), 0)
    offset = jnp.zeros((N_EXPERTS, 1), F32)
    for e in range(N_EXPERTS - 1):
        offset = offset + jnp.where(erow > e, padded[e:e + 1, :], 0.0)
    seen = jnp.zeros_like(den)
    pos_a = jnp.zeros_like(den)
    pos_b = jnp.zeros_like(den)
    w_a = jnp.zeros_like(den)
    w_b = jnp.zeros_like(den)
    for e in range(N_EXPERTS):
        pos_e = rank_ref[e:e + 1, :] + offset[e:e + 1, :]
        gate_e = picked[e] / den
        first = jnp.where(seen < 0.5, chosen[e], 0.0) > 0.5
        second = jnp.where(seen > 0.5, chosen[e], 0.0) > 0.5
        pos_a = jnp.where(first, pos_e, pos_a)
        w_a = jnp.where(first, gate_e, w_a)
        pos_b = jnp.where(second, pos_e, pos_b)
        w_b = jnp.where(second, gate_e, w_b)
        seen = seen + chosen[e]
    pos_ref[0:1, :] = pos_a.astype(jnp.int32)
    pos_ref[1:2, :] = pos_b.astype(jnp.int32)
    w_rows = jnp.concatenate([w_a, w_b, jnp.zeros((6, N_TOK), F32)], axis=0)
    ei = lax.broadcasted_iota(jnp.int32, (8, lanes), 0)
    ej = lax.broadcasted_iota(jnp.int32, (8, lanes), 1)
    eye = jnp.where(ei == ej, 1.0, 0.0).astype(BF16)
    tn = (((0,), (0,)), ((), ()))
    hi = w_rows.astype(BF16)
    r1 = w_rows - hi.astype(F32)
    mid = r1.astype(BF16)
    lo = (r1 - mid.astype(F32)).astype(BF16)
    w_cols = lax.dot_general(hi, eye, tn, preferred_element_type=F32)
    w_cols = w_cols + lax.dot_general(mid, eye, tn, preferred_element_type=F32)
    w_cols = w_cols + lax.dot_general(lo, eye, tn, preferred_element_type=F32)
    w_ref[...] = w_cols[:, :TOP_K]
    start = (lax.broadcasted_iota(jnp.int32, (N_EXPERTS, lanes), 1) * MOE_TILE).astype(F32)
    end = offset + padded
    tile_expert = jnp.sum(jnp.where(end <= start, 1.0, 0.0), axis=0, keepdims=True)
    inside = (offset <= start) & (start < end)
    real = jnp.clip(count - (start - offset), 0.0, float(MOE_TILE))
    tile_rows = jnp.sum(jnp.where(inside, real, 0.0), axis=0, keepdims=True)
    plan_ref[0:1, :] = jnp.minimum(tile_expert, float(N_EXPERTS - 1)).astype(jnp.int32)
    plan_ref[1:2, :] = tile_rows.astype(jnp.int32)


def _router(logits_t, router_b):
    whole = lambda shape: pl.BlockSpec(shape, lambda i: (0, 0))
    return pl.pallas_call(
        _router_kernel,
        grid=(1,),
        in_specs=[whole((N_EXPERTS, N_TOK)), whole((N_EXPERTS, 1))],
        out_specs=(whole((2, N_TOK)), whole((N_TOK, TOP_K)), whole((2, 128))),
        out_shape=(jax.ShapeDtypeStruct((2, N_TOK), jnp.int32),
                   jax.ShapeDtypeStruct((N_TOK, TOP_K), F32),
                   jax.ShapeDtypeStruct((2, 128), jnp.int32)),
        scratch_shapes=[pltpu.VMEM((N_EXPERTS, N_TOK), F32)],
        compiler_params=_params("arbitrary"),
        name="router",
    )(logits_t, router_b.reshape(N_EXPERTS, 1))


def _sc_mesh():
    return plsc.VectorSubcoreMesh(core_axis_name="c", subcore_axis_name="s")


def _sc_worker_base():
    return (lax.axis_index("s") * SC_CORES + lax.axis_index("c")) * SC_TOKENS_PER_WORKER


def _moe_dispatch(h, pos_a, pos_b):
    n_chunks = SC_TOKENS_PER_WORKER // SC_CHUNK
    idx = pltpu.VMEM((SC_CHUNK,), jnp.int32)

    @functools.partial(
        pl.kernel, mesh=_sc_mesh(),
        out_type=jax.ShapeDtypeStruct((MOE_ROWS, ROW_WORDS), jnp.int32),
        scratch_types=[idx, idx, idx, idx, pltpu.VMEM((2, SC_CHUNK, ROW_WORDS), jnp.int32),
                       pltpu.SemaphoreType.DMA((6,)), pltpu.SemaphoreType.DMA((4,))],
        name="moe_dispatch",
    )
    def run(h_hbm, pa_hbm, pb_hbm, xs_hbm, ia0, ib0, ia1, ib1, rows_v, sem_in, sem_out):
        base = _sc_worker_base()
        ia, ib = (ia0, ia1), (ib0, ib1)

        def start_loads(c):
            slot = c % 2
            tok = pl.ds(pl.multiple_of(base + c * SC_CHUNK, 8), SC_CHUNK)
            return (pltpu.async_copy(pa_hbm.at[tok], ia[slot], sem_in.at[3 * slot]),
                    pltpu.async_copy(pb_hbm.at[tok], ib[slot], sem_in.at[3 * slot + 1]),
                    pltpu.async_copy(h_hbm.at[tok], rows_v.at[slot], sem_in.at[3 * slot + 2]))

        loads = start_loads(0)
        scatters = [(), ()]
        for c in range(n_chunks):
            slot = c % 2
            for cp in loads:
                cp.wait()
            if c + 1 < n_chunks:
                for cp in scatters[1 - slot]:
                    cp.wait()
                scatters[1 - slot] = ()
                loads = start_loads(c + 1)
            scatters[slot] = (pltpu.async_copy(rows_v.at[slot], xs_hbm.at[ia[slot]], sem_out.at[2 * slot]),
                              pltpu.async_copy(rows_v.at[slot], xs_hbm.at[ib[slot]], sem_out.at[2 * slot + 1]))
        for pending in scatters:
            for cp in pending:
                cp.wait()

    return run(h, pos_a, pos_b)


def _moe_collect(ys, pos_a, pos_b):
    n_chunks = SC_TOKENS_PER_WORKER // SC_CHUNK
    out = jax.ShapeDtypeStruct((N_TOK, ROW_WORDS), jnp.int32)
    idx = pltpu.VMEM((SC_TOKENS_PER_WORKER,), jnp.int32)
    rows = pltpu.VMEM((2, SC_CHUNK, ROW_WORDS), jnp.int32)

    @functools.partial(
        pl.kernel, mesh=_sc_mesh(), out_type=(out, out),
        scratch_types=[idx, idx, rows, rows, pltpu.SemaphoreType.DMA((4,)), pltpu.SemaphoreType.DMA((4,))],
        name="moe_collect",
    )
    def run(ys_hbm, pa_hbm, pb_hbm, ya_hbm, yb_hbm, ia_v, ib_v, ra_v, rb_v, sem_g, sem_w):
        base = _sc_worker_base()
        mine = pl.ds(pl.multiple_of(base, 8), SC_TOKENS_PER_WORKER)
        pltpu.sync_copy(pa_hbm.at[mine], ia_v)
        pltpu.sync_copy(pb_hbm.at[mine], ib_v)
        writes = [(), ()]
        for c in range(n_chunks):
            slot = c % 2
            for cp in writes[slot]:
                cp.wait()
            part = pl.ds(c * SC_CHUNK, SC_CHUNK)
            tok = pl.ds(pl.multiple_of(base + c * SC_CHUNK, 8), SC_CHUNK)
            ga = pltpu.async_copy(ys_hbm.at[ia_v.at[part]], ra_v.at[slot], sem_g.at[slot])
            gb = pltpu.async_copy(ys_hbm.at[ib_v.at[part]], rb_v.at[slot], sem_g.at[2 + slot])
            ga.wait()
            wa = pltpu.async_copy(ra_v.at[slot], ya_hbm.at[tok], sem_w.at[slot])
            gb.wait()
            wb = pltpu.async_copy(rb_v.at[slot], yb_hbm.at[tok], sem_w.at[2 + slot])
            writes[slot] = (wa, wb)
        for pending in writes:
            for cp in pending:
                cp.wait()

    return run(ys, pos_a, pos_b)


def _experts_kernel(plan_ref, xs_ref, wg_hbm, wu_hbm, wd_hbm, y_ref,
                    sg_ref, su_ref, sd_ref, wgb_ref, wub_ref, wdb_ref, sems, seg_ref, *, layer):
    j = pl.program_id(0)
    n_tiles = pl.num_programs(0)
    expert = plan_ref[j]
    n_real = plan_ref[PLAN_LANES + j]
    fresh = jnp.logical_or(j == 0, expert != plan_ref[jnp.maximum(j - 1, 0)])

    def weight_copies(e, slot):
        return (pltpu.make_async_copy(wg_hbm.at[layer, e], sg_ref.at[slot], sems.at[slot, 0]),
                pltpu.make_async_copy(wu_hbm.at[layer, e], su_ref.at[slot], sems.at[slot, 1]),
                pltpu.make_async_copy(wd_hbm.at[layer, e], sd_ref.at[slot], sems.at[slot, 2]))

    @pl.when(j == 0)
    def _():
        seg_ref[0] = 0

        @pl.when(n_real > 0)
        def _():
            for cp in weight_copies(expert, 0):
                cp.start()

    @pl.when(jnp.logical_and(n_real > 0, fresh))
    def _():
        slot = seg_ref[0] % 2
        for cp in weight_copies(expert, slot):
            cp.wait()
        wgb_ref[...] = sg_ref[slot].astype(BF16)
        wub_ref[...] = su_ref[slot].astype(BF16)
        wdb_ref[...] = sd_ref[slot].astype(BF16)
        nxt = lax.while_loop(lambda t: jnp.logical_and(t < n_tiles, plan_ref[jnp.minimum(t, n_tiles - 1)] == expert),
                             lambda t: t + 1, j + 1)
        nxt_c = jnp.minimum(nxt, n_tiles - 1)

        @pl.when(jnp.logical_and(nxt < n_tiles, plan_ref[PLAN_LANES + nxt_c] > 0))
        def _():
            for cp in weight_copies(plan_ref[nxt_c], 1 - slot):
                cp.start()

        seg_ref[0] = seg_ref[0] + 1

    @pl.when(n_real > 0)
    def _():
        row = lax.broadcasted_iota(jnp.int32, xs_ref.shape, 0)
        words = jnp.where(row < n_real, xs_ref[...], 0)
        x = _unpack_rows(words).astype(BF16)
        a = jnp.dot(x, wgb_ref[...], preferred_element_type=F32)
        b = jnp.dot(x, wub_ref[...], preferred_element_type=F32)
        hid = (a * jax.nn.sigmoid(a)) * b
        y_ref[...] = _pack_rows(jnp.dot(hid.astype(BF16), wdb_ref[...], preferred_element_type=F32))


def _experts(plan, xs, w_gate, w_up, w_down, layer):
    hbm = pl.BlockSpec(memory_space=pl.ANY)
    return pl.pallas_call(
        functools.partial(_experts_kernel, layer=layer),
        grid_spec=pltpu.PrefetchScalarGridSpec(
            num_scalar_prefetch=1,
            grid=(MOE_ROWS // MOE_TILE,),
            in_specs=[pl.BlockSpec((MOE_TILE, ROW_WORDS), lambda j, plan: (j, 0)), hbm, hbm, hbm],
            out_specs=pl.BlockSpec((MOE_TILE, ROW_WORDS), lambda j, plan: (j, 0)),
            scratch_shapes=[pltpu.VMEM((2, D_MODEL, D_EXPERT), F32), pltpu.VMEM((2, D_MODEL, D_EXPERT), F32),
                            pltpu.VMEM((2, D_EXPERT, D_MODEL), F32),
                            pltpu.VMEM((D_MODEL, D_EXPERT), BF16), pltpu.VMEM((D_MODEL, D_EXPERT), BF16),
                            pltpu.VMEM((D_EXPERT, D_MODEL), BF16),
                            pltpu.SemaphoreType.DMA((2, 3)), pltpu.SMEM((1,), jnp.int32)],
        ),
        out_shape=jax.ShapeDtypeStruct((MOE_ROWS, ROW_WORDS), jnp.int32),
        compiler_params=_params("arbitrary"),
        name="experts",
    )(plan, xs, w_gate, w_up, w_down)


def _combine_kernel(x_ref, ya_ref, yb_ref, wt_ref, mod_ref, o_ref):
    o_ref[...] = _moe_mix(x_ref, ya_ref, yb_ref, wt_ref, mod_ref)


def _combine(x, moe_out, mod_l, tok0, n_tok, block_rows=512):
    ya, yb, w_tok = moe_out
    b0 = tok0 // block_rows
    rows = lambda width: pl.BlockSpec((block_rows, width), lambda i: (b0 + i, 0))
    return pl.pallas_call(
        _combine_kernel,
        grid=(n_tok // block_rows,),
        in_specs=[rows(D_MODEL), rows(ROW_WORDS), rows(ROW_WORDS), rows(TOP_K),
                  pl.BlockSpec((None, 6, D_MODEL), lambda i: (_cond_of_token_block(b0 + i, block_rows), 0, 0))],
        out_specs=pl.BlockSpec((block_rows, D_MODEL), lambda i: (i, 0)),
        out_shape=jax.ShapeDtypeStruct((n_tok, D_MODEL), F32),
        compiler_params=_params("arbitrary"),
        name="combine",
    )(x, ya, yb, w_tok, mod_l)


def _moe(h, logits_t, router_b, w_gate, w_up, w_down, layer):
    pos, w, plan = _router(logits_t, router_b)
    xs = _moe_dispatch(h, pos[0], pos[1])
    ys = _experts(plan.reshape(-1), xs, w_gate, w_up, w_down, layer)
    ya, yb = _moe_collect(ys, pos[0], pos[1])
    return ya, yb, w


def _dft_tables(L):
    k = np.arange(L)[:, None]
    m = np.arange(L)[None, :]
    r = (k * m) % (2 * L)
    ang = np.pi * r.astype(np.float64) / L
    fc = np.cos(ang)
    fs = np.sin(ang)
    fs[0, :] = np.where(np.arange(L) % 2 == 0, 1.0, -1.0)
    wk = np.full((L, 1), 1.0 / L)
    wk[0, 0] = 0.5 / L
    gc = (fc * wk).T
    gs = (fs * wk).T
    return [jnp.asarray(t.astype(np.float32)).astype(BF16) for t in (fc, fs, gc, gs)]


def _filter_consts(L):
    t = np.linspace(0.0, 1.0, L, dtype=np.float32)[:, None]
    w = (np.float32(2.0 * np.pi) * np.arange(L, dtype=np.float32)[:, None] / np.float32(L)).astype(np.float32)
    fb = np.linspace(1e-4, HY_BANDS - 1, HY_BANDS, dtype=np.float32)[None, :]
    emb = np.concatenate([t, np.cos(fb * w), -np.sin(fb * w)], axis=-1).astype(np.float32)
    lo = math.log(HY_DECAY_TARGET) / HY_SLOW_PCT
    hi = math.log(HY_DECAY_TARGET) / HY_FAST_PCT
    deltas = np.abs(np.linspace(lo, hi, D_MODEL, dtype=np.float32))
    decay = np.exp(-t * deltas).astype(np.float32)
    return jnp.asarray(emb), jnp.asarray(decay)


def _filter_kernel(emb_ref, w1_ref, b1_ref, w2_ref, b2_ref, fr_ref, w3f_ref, w3b_ref, dec_ref,
                   fc_ref, fs_ref, kr_ref, q_ref, krn_ref, hd_ref):
    @pl.when(pl.program_id(0) == 0)
    def _():
        fr = fr_ref[...]
        h1 = jnp.sin(fr * (jnp.dot(emb_ref[...], w1_ref[...], precision=HIGHEST,
                                   preferred_element_type=F32) + b1_ref[...]))
        hd_ref[...] = jnp.sin(fr * (jnp.dot(h1, w2_ref[...], precision=HIGHEST,
                                            preferred_element_type=F32) + b2_ref[...]))

    hd = hd_ref[...]
    dec = dec_ref[...]
    f = jnp.dot(hd, w3f_ref[...], precision=HIGHEST, preferred_element_type=F32) * dec
    g = jnp.dot(hd, w3b_ref[...], precision=HIGHEST, preferred_element_type=F32) * dec
    row = lax.broadcasted_iota(jnp.int32, f.shape, 0)
    g = jnp.where(row == 0, 0.0, g)
    s = f + g
    d = f - g
    kr = jnp.dot(fc_ref[...], s.astype(BF16), preferred_element_type=F32)
    qq = jnp.dot(fs_ref[...], d.astype(BF16), preferred_element_type=F32)
    alt = jnp.where(row % 2 == 0, 1.0, -1.0)
    nyq = jnp.sum(alt * s, axis=0, keepdims=True)
    kr_ref[...] = kr
    q_ref[...] = jnp.where(row == 0, 0.0, qq)
    krn_ref[...] = jnp.where(row == 0, nyq, kr)


def _hyena_filter_spectrum(L, w1, b1, w2, b2, w3, freq, fc, fs, cblk=256):
    emb, decay = _filter_consts(L)
    ncb = D_MODEL // cblk
    n_emb = 128
    emb = jnp.pad(emb, ((0, 0), (0, n_emb - emb.shape[1])))
    w1 = jnp.pad(w1, ((0, n_emb - w1.shape[0]), (0, 0)))
    full = lambda shape: pl.BlockSpec(shape, lambda j: tuple(0 for _ in shape))
    out_sds = jax.ShapeDtypeStruct((L, D_MODEL), F32)
    out_spec = pl.BlockSpec((L, cblk), lambda j: (0, j))
    return pl.pallas_call(
        _filter_kernel,
        grid=(ncb,),
        in_specs=[
            full((L, n_emb)), full((n_emb, HY_FFN)), full((1, HY_FFN)), full((HY_FFN, HY_FFN)),
            full((1, HY_FFN)), full((1, HY_FFN)),
            pl.BlockSpec((HY_FFN, cblk), lambda j: (0, j)),
            pl.BlockSpec((HY_FFN, cblk), lambda j: (0, ncb + j)),
            pl.BlockSpec((L, cblk), lambda j: (0, j)),
            full((L, L)), full((L, L)),
        ],
        out_specs=(out_spec, out_spec, out_spec),
        out_shape=(out_sds, out_sds, out_sds),
        scratch_shapes=[pltpu.VMEM((L, HY_FFN), F32)],
        compiler_params=_params("arbitrary"),
        name=f"hyena_filter_{L}",
    )(emb, w1, b1.reshape(1, HY_FFN), w2, b2.reshape(1, HY_FFN), freq.reshape(1, HY_FFN), w3, w3, decay, fc, fs)


def _hyena_conv_kernel(x0_ref, x1_ref, v_ref, cw0_ref, cw1_ref, cwv_ref, cb0_ref, cb1_ref, cbv_ref,
                       kr_ref, q_ref, krn_ref, ds_ref, fc_ref, fs_ref, gc_ref, gs_ref, o_ref):
    L = fc_ref.shape[0]
    row = lax.broadcasted_iota(jnp.int32, (L, x0_ref.shape[1]), 0)

    def one_sequence(s, carry):
        rows = pl.ds(pl.multiple_of(s * L, L), L)

        def short_conv(u_ref, w_ref, b_ref):
            u = u_ref[rows, :].astype(F32)
            w = w_ref[...]
            prev = jnp.where(row == 0, 0.0, pltpu.roll(u, 1, axis=0))
            nxt = jnp.where(row == L - 1, 0.0, pltpu.roll(u, L - 1, axis=0))
            return prev * w[0:1, :] + u * w[1:2, :] + nxt * w[2:3, :] + b_ref[...]

        x0 = short_conv(x0_ref, cw0_ref, cb0_ref)
        x1 = short_conv(x1_ref, cw1_ref, cb1_ref)
        v = short_conv(v_ref, cwv_ref, cbv_ref)
        zz = v * x1
        zb = zz.astype(BF16)
        ur = jnp.dot(fc_ref[...], zb, preferred_element_type=F32)
        p = jnp.dot(fs_ref[...], zb, preferred_element_type=F32)
        qq = q_ref[...]
        yr = ur * kr_ref[...] - p * qq
        yw = ur * qq + p * krn_ref[...]
        y = jnp.dot(gc_ref[...], yr.astype(BF16), preferred_element_type=F32)
        y = y + jnp.dot(gs_ref[...], yw.astype(BF16), preferred_element_type=F32)
        o_ref[rows, :] = (x0 * (y + zz * ds_ref[...])).astype(o_ref.dtype)
        return carry

    lax.fori_loop(0, x0_ref.shape[0] // L, one_sequence, 0)


def _hyena_conv(u, conv_w, conv_b, dskip, spectrum, tables, *, latent):
    L = LATENT_LEN if latent else PROMPT_LEN
    n_seq = N_LATENT_SEQ if latent else N_PROMPT_SEQ
    cblk = 256 if latent else 512
    ncb = D_MODEL // cblk
    seqs = 1 if latent else 4
    row0 = (N_PROMPT_TOK // L) if latent else 0
    kr, qq, krn = spectrum
    fc, fs, gc, gs = tables

    def part(p, rows):
        if rows != L:
            return pl.BlockSpec((rows, cblk), lambda j, s: (0, p * ncb + j))
        return pl.BlockSpec((seqs * L, cblk), lambda j, s: (row0 // seqs + s, p * ncb + j))

    def const_cols(rows):
        return pl.BlockSpec((rows, cblk), lambda j, s: (0, j))

    mat = pl.BlockSpec((L, L), lambda j, s: (0, 0))
    conv_b2 = conv_b.reshape(1, 3 * D_MODEL)
    in_specs = [part(0, L), part(1, L), part(2, L),
                part(0, 3), part(1, 3), part(2, 3),
                part(0, 1), part(1, 1), part(2, 1),
                const_cols(L), const_cols(L), const_cols(L), const_cols(1),
                mat, mat, mat, mat]
    args = [u, u, u, conv_w, conv_w, conv_w, conv_b2, conv_b2, conv_b2,
            kr, qq, krn, dskip.reshape(1, D_MODEL), fc, fs, gc, gs]
    return pl.pallas_call(
        _hyena_conv_kernel,
        grid=(ncb, n_seq // seqs),
        in_specs=in_specs,
        out_specs=pl.BlockSpec((seqs * L, cblk), lambda j, s: (s, j)),
        out_shape=jax.ShapeDtypeStruct((n_seq * L, D_MODEL), BF16),
        compiler_params=_params("arbitrary", "arbitrary"),
        name="hyena_conv_latent" if latent else "hyena_conv_prompt",
    )(*args)


def kernel(x_prompt, x_sample, cache_k, cache_v, state_hgrn, c, c_ctx, norm_g, mod_w, mod_b, ab_in_w, hgrn_lb, hgrn_onorm_g, attn_qnorm_g, attn_knorm_g, ab_out_w, hy_in_w, hy_in_b, hy_conv_w, hy_conv_b, hy_f_w1, hy_f_b1, hy_f_w2, hy_f_b2, hy_f_w3, hy_f_freq, hy_dskip, hy_out_w, router_w, router_b, moe_w_gate, moe_w_up, moe_w_down):
    xp = x_prompt.reshape(N_PROMPT_TOK, D_MODEL)
    xl = x_sample.reshape(N_LATENT_TOK, D_MODEL)
    cond = jnp.concatenate([c_ctx[None, :], c, jnp.zeros((N_COND - 1 - N_LATENT_SEQ, D_MODEL), F32)], axis=0)
    mod = _modulation(cond, mod_w, mod_b)
    router_wp = jnp.pad(router_w, ((0, 0), (0, ROUTER_LANES - N_EXPERTS)))

    z = _in_proj0(xp, xl, norm_g[0, 0], mod[0], ab_in_w[0])
    oa_p, new_state = _hgrn(z, hgrn_lb, hgrn_onorm_g[0], None, latent=False)
    oa_l = _hgrn(z, hgrn_lb, hgrn_onorm_g[0], state_hgrn, latent=True)
    ob_p, k_fm, v_fm = _attention_prompt(z, attn_qnorm_g[0], attn_knorm_g[0])
    fm_shape = (N_PROMPT_SEQ, 1, KV_HEADS, HEAD_DIM, PROMPT_LEN)
    new_k = jnp.swapaxes(k_fm.reshape(fm_shape), -1, -2)
    new_v = jnp.swapaxes(v_fm.reshape(fm_shape), -1, -2)
    ob_l = _attention_latent(z, attn_qnorm_g[0], attn_knorm_g[0], cache_k, cache_v)
    x, h, logits_t = _out_proj([(oa_p, oa_l), (ob_p, ob_l)], ab_out_w[0], (xp, xl), norm_g[0, 1], mod[0],
                               router_wp)
    moe_out = _moe(h, logits_t, router_b, moe_w_gate, moe_w_up, moe_w_down, 0)

    x, u = _in_proj1(x, moe_out, mod[0], norm_g[1, 0], mod[1], hy_in_w[0], hy_in_b[0])
    pre = []
    for latent in (False, True):
        L = LATENT_LEN if latent else PROMPT_LEN
        tables = _dft_tables(L)
        spectrum = _hyena_filter_spectrum(L, hy_f_w1[0], hy_f_b1[0], hy_f_w2[0], hy_f_b2[0], hy_f_w3[0],
                                          hy_f_freq[0], tables[0], tables[1])
        pre.append(_hyena_conv(u, hy_conv_w[0], hy_conv_b[0], hy_dskip[0], spectrum, tables, latent=latent))
    x, h, logits_t = _out_proj([tuple(pre)], hy_out_w[0], (x,), norm_g[1, 1], mod[1], router_wp)
    moe_out = _moe(h, logits_t, router_b, moe_w_gate, moe_w_up, moe_w_down, 1)

    y_prompt = _combine(x, moe_out, mod[1], 0, N_PROMPT_TOK).reshape(N_PROMPT_SEQ, PROMPT_LEN, D_MODEL)
    y_sample = _combine(x, moe_out, mod[1], N_PROMPT_TOK, N_LATENT_TOK).reshape(N_LATENT_SEQ, LATENT_LEN, D_MODEL)
    return (y_prompt, y_sample, new_k, new_v, new_state)
```

```python
import functools
import math

import numpy as np
import jax
import jax.numpy as jnp
from jax import lax
from jax.experimental import pallas as pl
from jax.experimental.pallas import tpu as pltpu
from jax.experimental.pallas import tpu_sc as plsc

F32 = jnp.float32
BF16 = jnp.bfloat16
HIGHEST = lax.Precision.HIGHEST

D_MODEL = 1024
N_PROMPT_SEQ = 32
PROMPT_LEN = 256
N_LATENT_SEQ = 2
LATENT_LEN = 1024
PAST_LEN = 512
GRID_W = 64
N_PROMPT_TOK = N_PROMPT_SEQ * PROMPT_LEN
N_LATENT_TOK = N_LATENT_SEQ * LATENT_LEN
N_TOK = N_PROMPT_TOK + N_LATENT_TOK
N_COND = 8
EPS = 1e-6

A_WIDTH = 512
A_HEADS = 4
A_DK = 128
CHUNK = 64
HGRN_BLOCK = 128
HGRN_HEADS_PER_STEP = 4
HEAD_DIM = 64
Q_HEADS = 8
KV_HEADS = 2
Q_PER_KV = Q_HEADS // KV_HEADS
Q_BLOCK = 256
ROPE_THETA = 10000.0
ROPE_PAIRS = HEAD_DIM // 4
AB_IN = 5 * A_WIDTH + (Q_HEADS + 2 * KV_HEADS) * HEAD_DIM

HY_BANDS = 16
HY_FFN = 64
HY_DECAY_TARGET = 1e-2
HY_FAST_PCT = 0.3
HY_SLOW_PCT = 1.5

N_EXPERTS = 16
N_GROUPS = 4
EXPERTS_PER_GROUP = 4
TOP_K = 2
D_EXPERT = 512
ROUTER_LANES = 128
OUT_PROJ_SUB_ROWS = 256
EXPERT_SUB_ROWS = 256
IN_PROJ_SUB_ROWS = 256
MOE_TILE = 512
MOE_ROWS = N_TOK * TOP_K + N_EXPERTS * MOE_TILE
PLAN_LANES = 128

SC_CORES = 2
SC_WORKERS = 32
SC_TOKENS_PER_WORKER = N_TOK // SC_WORKERS
SC_CHUNK = 40
ROW_WORDS = D_MODEL // 2

VMEM_LIMIT = 56 * 1024 * 1024


def _params(*sem):
    return pltpu.CompilerParams(dimension_semantics=sem, vmem_limit_bytes=VMEM_LIMIT)


def _pack_rows(x):
    n = x.shape[1] // 2
    bits = pltpu.bitcast(x.astype(BF16).astype(F32), jnp.uint32)
    return pltpu.bitcast(bits[:, :n] | (bits[:, n:] >> 16), jnp.int32)


def _unpack_rows(p):
    bits = pltpu.bitcast(p, jnp.uint32)
    hi = pltpu.bitcast(bits & jnp.uint32(0xFFFF0000), F32)
    lo = pltpu.bitcast(bits << 16, F32)
    return jnp.concatenate([hi, lo], axis=1)


def _cond_of_token_block(i, block_rows):
    start = i * block_rows
    return jnp.where(start < N_PROMPT_TOK, 0, 1 + (start - N_PROMPT_TOK) // LATENT_LEN)


def _mod_kernel(cond_ref, w_ref, b_ref, o_ref):
    cnd = cond_ref[...]
    s = cnd * jax.nn.sigmoid(cnd)
    s_hi = s.astype(BF16)
    s_lo = (s - s_hi.astype(F32)).astype(BF16)
    w = w_ref[...]
    w_hi = w.astype(BF16)
    w_lo = (w - w_hi.astype(F32)).astype(BF16)
    acc = jnp.dot(s_hi, w_hi, preferred_element_type=F32)
    acc = acc + jnp.dot(s_lo, w_hi, preferred_element_type=F32)
    acc = acc + jnp.dot(s_hi, w_lo, preferred_element_type=F32)
    o_ref[...] = acc + b_ref[...]


def _modulation(cond, mod_w, mod_b):
    depth = mod_w.shape[0]
    n_chunk = 6
    out = pl.pallas_call(
        _mod_kernel,
        grid=(depth, n_chunk),
        in_specs=[
            pl.BlockSpec((N_COND, D_MODEL), lambda l, j: (0, 0)),
            pl.BlockSpec((None, D_MODEL, D_MODEL), lambda l, j: (l, 0, j)),
            pl.BlockSpec((None, 1, D_MODEL), lambda l, j: (l, 0, j)),
        ],
        out_specs=pl.BlockSpec((None, N_COND, D_MODEL), lambda l, j: (l, 0, j)),
        out_shape=jax.ShapeDtypeStruct((depth, N_COND, n_chunk * D_MODEL), F32),
        compiler_params=_params("arbitrary", "arbitrary"),
        name="modulation",
    )(cond, mod_w, mod_b.reshape(depth, 1, n_chunk * D_MODEL))
    return out.reshape(depth, N_COND, n_chunk, D_MODEL)


def _modulated_norm(x, g, mod, shift_row, scale_row):
    ms = jnp.mean(x * x, axis=-1, keepdims=True)
    y = x * lax.rsqrt(ms + EPS) * g
    return y * (1.0 + mod[scale_row:scale_row + 1, :]) + mod[shift_row:shift_row + 1, :]


def _trunk_specs(block_rows, width):
    n_prompt_blocks = N_PROMPT_TOK // block_rows
    return (pl.BlockSpec((block_rows, width), lambda i: (jnp.minimum(i, n_prompt_blocks - 1), 0)),
            pl.BlockSpec((block_rows, width), lambda i: (jnp.maximum(i - n_prompt_blocks, 0), 0)))


def _select_trunk(p_ref, l_ref, rows=slice(None)):
    block_rows = p_ref.shape[0]
    return jnp.where(pl.program_id(0) < N_PROMPT_TOK // block_rows, p_ref[rows, :], l_ref[rows, :])


def _cast_once(w_ref, wb_ref):
    @pl.when(pl.program_id(0) == 0)
    def _():
        wb_ref[...] = w_ref[...].astype(BF16)


def _resident(shape):
    return pl.BlockSpec(shape, lambda i: tuple(0 for _ in shape), pipeline_mode=pl.Buffered(1))


def _mod_spec(block_rows):
    return pl.BlockSpec((None, 6, D_MODEL), lambda i: (_cond_of_token_block(i, block_rows), 0, 0))


def _in_proj0_kernel(xp_ref, xl_ref, g_ref, mod_ref, w_ref, o_ref, wb_ref, hb_ref):
    _cast_once(w_ref, wb_ref)
    n = IN_PROJ_SUB_ROWS
    n_sub = xp_ref.shape[0] // n

    def prepare(r):
        x = _select_trunk(xp_ref, xl_ref, slice(r * n, (r + 1) * n))
        hb_ref[r] = _modulated_norm(x, g_ref[...], mod_ref[...], 0, 1).astype(BF16)

    def project(r):
        u = jnp.dot(hb_ref[r], wb_ref[...], preferred_element_type=F32)
        o_ref[r * n:(r + 1) * n, :] = u.astype(o_ref.dtype)

    prepare(0)
    for r in range(1, n_sub):
        prepare(r)
        project(r - 1)
    project(n_sub - 1)


def _in_proj0(x_prompt, x_latent, g, mod_l, w, block_rows=512):
    n = w.shape[1]
    return pl.pallas_call(
        _in_proj0_kernel,
        grid=(N_TOK // block_rows,),
        in_specs=[*_trunk_specs(block_rows, D_MODEL), _resident((1, D_MODEL)), _mod_spec(block_rows),
                  _resident((D_MODEL, n))],
        out_specs=pl.BlockSpec((block_rows, n), lambda i: (i, 0)),
        out_shape=jax.ShapeDtypeStruct((N_TOK, n), BF16),
        scratch_shapes=[pltpu.VMEM((D_MODEL, n), BF16),
                        pltpu.VMEM((block_rows // IN_PROJ_SUB_ROWS, IN_PROJ_SUB_ROWS, D_MODEL), BF16)],
        compiler_params=_params("arbitrary"),
        name="in_proj0",
    )(x_prompt, x_latent, g.reshape(1, D_MODEL), mod_l, w)


def _moe_mix(x_ref, ya_ref, yb_ref, wt_ref, mod_ref, rows=slice(None)):
    wt = wt_ref[rows, :]
    mix = wt[:, 0:1] * _unpack_rows(ya_ref[rows, :]) + wt[:, 1:2] * _unpack_rows(yb_ref[rows, :])
    return x_ref[rows, :] + mod_ref[5:6, :] * mix


def _in_proj1_kernel(x_ref, ya_ref, yb_ref, wt_ref, modp_ref, g_ref, mod_ref, w_ref, b_ref, xo_ref, o_ref,
                     wb_ref, hb_ref):
    _cast_once(w_ref, wb_ref)
    n = IN_PROJ_SUB_ROWS
    n_sub = x_ref.shape[0] // n

    def prepare(r):
        rows = slice(r * n, (r + 1) * n)
        x = _moe_mix(x_ref, ya_ref, yb_ref, wt_ref, modp_ref, rows)
        xo_ref[rows, :] = x
        hb_ref[r] = _modulated_norm(x, g_ref[...], mod_ref[...], 0, 1).astype(BF16)

    def project(r):
        rows = slice(r * n, (r + 1) * n)
        u = jnp.dot(hb_ref[r], wb_ref[...], preferred_element_type=F32) + b_ref[...]
        o_ref[rows, :] = u.astype(o_ref.dtype)

    prepare(0)
    for r in range(1, n_sub):
        prepare(r)
        project(r - 1)
    project(n_sub - 1)


def _in_proj1(x, moe_out, mod_prev, g, mod_l, w, bias, block_rows=512):
    ya, yb, w_tok = moe_out
    n = w.shape[1]
    tok = pl.BlockSpec((block_rows, D_MODEL), lambda i: (i, 0))
    packed = pl.BlockSpec((block_rows, ROW_WORDS), lambda i: (i, 0))
    return pl.pallas_call(
        _in_proj1_kernel,
        grid=(N_TOK // block_rows,),
        in_specs=[tok, packed, packed, pl.BlockSpec((block_rows, TOP_K), lambda i: (i, 0)), _mod_spec(block_rows),
                  _resident((1, D_MODEL)), _mod_spec(block_rows), _resident((D_MODEL, n)), _resident((1, n))],
        out_specs=(tok, pl.BlockSpec((block_rows, n), lambda i: (i, 0))),
        out_shape=(jax.ShapeDtypeStruct((N_TOK, D_MODEL), F32), jax.ShapeDtypeStruct((N_TOK, n), BF16)),
        scratch_shapes=[pltpu.VMEM((D_MODEL, n), BF16),
                        pltpu.VMEM((block_rows // IN_PROJ_SUB_ROWS, IN_PROJ_SUB_ROWS, D_MODEL), BF16)],
        compiler_params=_params("arbitrary"),
        name="in_proj1",
    )(x, ya, yb, w_tok, mod_prev, g.reshape(1, D_MODEL), mod_l, w, bias.reshape(1, n))


def _hgrn_kernel(*refs, seq_len, with_state):
    if with_state:
        (q_ref, zf_ref, zb_ref, i_ref, ga_ref, lb_ref, og_ref, s0_ref, o_ref, of_ref, ob_ref) = refs
    else:
        (q_ref, zf_ref, zb_ref, i_ref, ga_ref, lb_ref, og_ref, o_ref, s_ref, of_ref, ob_ref) = refs
    n_blocks = seq_len // HGRN_BLOCK
    chunks_per_block = HGRN_BLOCK // CHUNK

    lbr = lb_ref[...]
    mx = jnp.maximum(lbr[0], lbr[1])
    e0 = jnp.exp(lbr[0] - mx)
    e1 = jnp.exp(lbr[1] - mx)
    lb = e0 / (e0 + e1)

    row = lax.broadcasted_iota(jnp.int32, (HGRN_BLOCK, HGRN_BLOCK), 0)
    col = lax.broadcasted_iota(jnp.int32, (HGRN_BLOCK, HGRN_BLOCK), 1)
    same_chunk = (row // CHUNK) == (col // CHUNK)
    nt = (((1,), (1,)), ((), ()))
    tn = (((0,), (0,)), ((), ()))

    def per_chunk_row(x, idx):
        return jnp.concatenate(
            [jnp.broadcast_to(x[n * CHUNK + idx:n * CHUNK + idx + 1, :], (CHUNK, x.shape[1]))
             for n in range(chunks_per_block)], axis=0)

    def in_chunk_cumsum(tri, x):
        hi = x.astype(BF16)
        lo = (x - hi.astype(F32)).astype(BF16)
        return jnp.dot(tri, hi, preferred_element_type=F32) + jnp.dot(tri, lo, preferred_element_type=F32)

    def block(blk, cols, st, z_ref, lbd, forward, out_ref):
        rows = slice(blk * HGRN_BLOCK, (blk + 1) * HGRN_BLOCK)
        keep = (same_chunk & (col <= row)) if forward else (same_chunk & (col >= row))
        tri = jnp.where(keep, 1.0, 0.0).astype(BF16)
        mid = CHUNK // 2 if forward else CHUNK - 1 - CHUNK // 2
        last = CHUNK - 1 if forward else 0
        f = lbd + (1.0 - lbd) * jax.nn.sigmoid(z_ref[rows, cols].astype(F32))
        lf = jnp.log(f)
        k = 1.0 - f
        q = q_ref[rows, cols].astype(F32)
        vb = i_ref[rows, cols].astype(BF16)
        b = in_chunk_cumsum(tri, lf)
        bm = per_chunk_row(b, mid)
        bl = per_chunk_row(b, last)
        qe = (q * jnp.exp(b - bm)).astype(BF16)
        ke = (k * jnp.exp(bm - b)).astype(BF16)
        att = lax.dot_general(qe, ke, nt, preferred_element_type=F32)
        att = jnp.where(keep, att, 0.0)
        o_intra = jnp.dot(att.astype(BF16), vb, preferred_element_type=F32)
        qb = (q * jnp.exp(b)).astype(BF16)
        ks = (k * jnp.exp(bl - b)).astype(BF16)
        decay = jnp.exp(bl)
        order = range(chunks_per_block) if forward else range(chunks_per_block - 1, -1, -1)
        o_inter = [None] * chunks_per_block
        for n in order:
            cr = slice(n * CHUNK, (n + 1) * CHUNK)
            o_inter[n] = lax.dot_general(qb[cr], st.astype(BF16), nt, preferred_element_type=F32)
            upd = lax.dot_general(vb[cr], ks[cr], tn, preferred_element_type=F32)
            st = st * decay[n * CHUNK:n * CHUNK + 1, :] + upd
        out_ref[rows, cols] = o_intra + jnp.concatenate(o_inter, axis=0)
        return st

    for hd in range(q_ref.shape[1] // A_DK):
        cols = slice(hd * A_DK, (hd + 1) * A_DK)
        if with_state:
            st_f, st_b = s0_ref[0, hd].T, s0_ref[1, hd].T
        else:
            st_f, st_b = jnp.zeros((A_DK, A_DK), F32), jnp.zeros((A_DK, A_DK), F32)
        for step in range(n_blocks):
            st_f = block(step, cols, st_f, zf_ref, lb[0:1, cols], True, of_ref)
            st_b = block(n_blocks - 1 - step, cols, st_b, zb_ref, lb[1:2, cols], False, ob_ref)
        if not with_state:
            s_ref[0, hd] = st_f.T
            s_ref[1, hd] = st_b.T
        o = of_ref[:, cols] + ob_ref[:, cols]
        o = o * lax.rsqrt(jnp.mean(o * o, axis=-1, keepdims=True) + EPS) * og_ref[:, cols]
        ga = ga_ref[:, cols].astype(F32)
        o_ref[:, cols] = (o * (ga * jax.nn.sigmoid(ga))).astype(o_ref.dtype)


def _hgrn(z, hgrn_lb, onorm_g, state, *, latent):
    seq_len = LATENT_LEN if latent else PROMPT_LEN
    n_seq = N_LATENT_SEQ if latent else N_PROMPT_SEQ
    row0 = (N_PROMPT_TOK // seq_len) if latent else 0

    hw = HGRN_HEADS_PER_STEP * A_DK
    n_hg = A_HEADS // HGRN_HEADS_PER_STEP

    def zspec(part):
        return pl.BlockSpec((seq_len, hw), lambda s, h: (row0 + s, part * n_hg + h))

    in_specs = [zspec(0), zspec(1), zspec(2), zspec(3), zspec(4),
                pl.BlockSpec((2, 2, hw), lambda s, h: (0, 0, h)),
                pl.BlockSpec((1, hw), lambda s, h: (0, h))]
    args = [z, z, z, z, z, hgrn_lb, onorm_g.reshape(1, A_WIDTH)]
    state_spec = pl.BlockSpec((None, None, 2, HGRN_HEADS_PER_STEP, A_DK, A_DK), lambda s, h: (s, 0, 0, h, 0, 0))
    o_shape = jax.ShapeDtypeStruct((n_seq * seq_len, A_WIDTH), BF16)
    o_spec = pl.BlockSpec((seq_len, hw), lambda s, h: (s, h))
    if latent:
        in_specs.append(state_spec)
        args.append(state)
        out_shape, out_specs = o_shape, o_spec
    else:
        out_shape = (o_shape, jax.ShapeDtypeStruct((n_seq, 1, 2, A_HEADS, A_DK, A_DK), F32))
        out_specs = (o_spec, state_spec)
    return pl.pallas_call(
        functools.partial(_hgrn_kernel, seq_len=seq_len, with_state=latent),
        grid=(n_seq, n_hg),
        in_specs=in_specs,
        out_specs=out_specs,
        out_shape=out_shape,
        scratch_shapes=[pltpu.VMEM((seq_len, hw), F32), pltpu.VMEM((seq_len, hw), F32)],
        compiler_params=_params("arbitrary", "arbitrary"),
        name="hgrn_latent" if latent else "hgrn_prompt",
    )(*args)


def _rope_tables():
    pos = np.arange(LATENT_LEN)
    row, colp = pos // GRID_W, pos % GRID_W
    inv = ROPE_THETA ** (-np.arange(ROPE_PAIRS, dtype=np.float32) / ROPE_PAIRS)
    inv = inv.astype(np.float32)
    ang_r = (row.astype(np.float32)[:, None] * inv).astype(np.float32)
    ang_c = (colp.astype(np.float32)[:, None] * inv).astype(np.float32)
    cos = np.concatenate([np.cos(ang_r), np.cos(ang_r), np.cos(ang_c), np.cos(ang_c)], axis=1)
    sin = np.concatenate([-np.sin(ang_r), np.sin(ang_r), -np.sin(ang_c), np.sin(ang_c)], axis=1)
    return cos.astype(np.float32), sin.astype(np.float32)


def _head_mean_matrix(width):
    idx = np.arange(width) // HEAD_DIM
    return jnp.asarray((idx[:, None] == idx[None, :]).astype(np.float32) / HEAD_DIM).astype(BF16)


def _attn_kernel(*refs, latent):
    if latent:
        (q_ref, k_ref, v_ref, qg_ref, kg_ref, gq_ref, gk_ref, cosq_ref, sinq_ref, cosk_ref, sink_ref,
         ck_ref, cv_ref, o_ref) = refs
    else:
        (q_ref, k_ref, v_ref, qg_ref, kg_ref, gq_ref, gk_ref, o_ref, kout_ref, vout_ref) = refs
    pair_w = 2 * HEAD_DIM

    def head_norm(x, mean_ref, gain):
        sq = x * x
        hi = sq.astype(BF16)
        lo = (sq - hi.astype(F32)).astype(BF16)
        ms = jnp.dot(hi, mean_ref[...], preferred_element_type=F32)
        ms = ms + jnp.dot(lo, mean_ref[...], preferred_element_type=F32)
        return x * lax.rsqrt(ms + EPS) * gain

    def rope(x, cos, sin):
        n = x.shape[1]
        lane = lax.broadcasted_iota(jnp.int32, x.shape, 1)
        first_of_pair = (lane // ROPE_PAIRS) % 2 == 0
        swapped = jnp.where(first_of_pair, pltpu.roll(x, n - ROPE_PAIRS, axis=1), pltpu.roll(x, ROPE_PAIRS, axis=1))
        return x * cos + swapped * sin

    def attend(rows, seq_idx=None):
        q = head_norm(q_ref[rows, :].astype(F32), gq_ref, qg_ref[...])
        k = head_norm(k_ref[rows, :].astype(F32), gk_ref, kg_ref[...])
        if latent:
            q = rope(q, cosq_ref[...], sinq_ref[...])
            k = rope(k, cosk_ref[...], sink_ref[...])
        q = q * (HEAD_DIM ** -0.5)
        v = v_ref[rows, :].astype(F32)
        n_q = q.shape[0]
        low_kv = lax.broadcasted_iota(jnp.int32, k.shape, 1) < HEAD_DIM
        low_q = lax.broadcasted_iota(jnp.int32, (n_q, pair_w), 1) < HEAD_DIM
        k_swapped = pltpu.roll(k, HEAD_DIM, axis=1)
        v_swapped = pltpu.roll(v, HEAD_DIM, axis=1)
        if not latent:
            kout_ref[seq_idx] = k.T
            vout_ref[seq_idx] = v.T
        nt = (((1,), (1,)), ((), ()))
        for j in range(KV_HEADS):
            kd = (jnp.where(low_kv, k, k_swapped) if j == 0 else jnp.where(low_kv, k_swapped, k)).astype(BF16)
            vd = (jnp.where(low_kv, v, v_swapped) if j == 0 else jnp.where(low_kv, v_swapped, v)).astype(BF16)
            tiles = range(j * Q_PER_KV // 2, (j + 1) * Q_PER_KV // 2)
            parts = []
            for t in tiles:
                qt = q[:, t * pair_w:(t + 1) * pair_w]
                parts += [jnp.where(low_q, qt, 0.0), jnp.where(low_q, 0.0, qt)]
            qs = jnp.concatenate(parts, axis=0).astype(BF16)
            s_new = lax.dot_general(qs, kd, nt, preferred_element_type=F32)
            m = jnp.max(s_new, axis=-1, keepdims=True)
            if latent:
                ckd = jnp.concatenate([ck_ref[j], ck_ref[j]], axis=1).astype(BF16)
                cvd = jnp.concatenate([cv_ref[j], cv_ref[j]], axis=1).astype(BF16)
                s_old = lax.dot_general(qs, ckd, nt, preferred_element_type=F32)
                m = jnp.maximum(m, jnp.max(s_old, axis=-1, keepdims=True))
            p_new = jnp.exp(s_new - m)
            den = jnp.sum(p_new, axis=-1, keepdims=True)
            acc = jnp.dot(p_new.astype(BF16), vd, preferred_element_type=F32)
            if latent:
                p_old = jnp.exp(s_old - m)
                den = den + jnp.sum(p_old, axis=-1, keepdims=True)
                acc = acc + jnp.dot(p_old.astype(BF16), cvd, preferred_element_type=F32)
            out = acc / den
            for i, t in enumerate(tiles):
                lo_head = out[(2 * i) * n_q:(2 * i + 1) * n_q, :]
                hi_head = out[(2 * i + 1) * n_q:(2 * i + 2) * n_q, :]
                o_ref[rows, t * pair_w:(t + 1) * pair_w] = jnp.where(low_q, lo_head, hi_head).astype(o_ref.dtype)

    if latent:
        attend(slice(None))
    else:
        seq = PROMPT_LEN

        def one_sequence(s, carry):
            attend(pl.ds(pl.multiple_of(s * seq, seq), seq), s)
            return carry

        lax.fori_loop(0, q_ref.shape[0] // seq, one_sequence, 0)


def _attn_common_args(qn_g, kn_g):
    q_w, kv_w = Q_HEADS * HEAD_DIM, KV_HEADS * HEAD_DIM
    return (jnp.tile(qn_g, Q_HEADS).reshape(1, q_w), jnp.tile(kn_g, KV_HEADS).reshape(1, kv_w),
            _head_mean_matrix(q_w), _head_mean_matrix(kv_w))


def _attention_prompt(z, qn_g, kn_g):
    seqs = 4
    L = seqs * PROMPT_LEN
    cache_shape = jax.ShapeDtypeStruct((N_PROMPT_SEQ, KV_HEADS * HEAD_DIM, PROMPT_LEN), F32)
    cache_spec = pl.BlockSpec((seqs, KV_HEADS * HEAD_DIM, PROMPT_LEN), lambda s: (s, 0, 0))
    q_w, kv_w = Q_HEADS * HEAD_DIM, KV_HEADS * HEAD_DIM
    q_col = (5 * A_WIDTH) // q_w
    k_col = (5 * A_WIDTH + q_w) // kv_w
    const = lambda r, c: pl.BlockSpec((r, c), lambda s: (0, 0))
    return pl.pallas_call(
        functools.partial(_attn_kernel, latent=False),
        grid=(N_PROMPT_TOK // L,),
        in_specs=[
            pl.BlockSpec((L, q_w), lambda s: (s, q_col)),
            pl.BlockSpec((L, kv_w), lambda s: (s, k_col)),
            pl.BlockSpec((L, kv_w), lambda s: (s, k_col + 1)),
            const(1, q_w), const(1, kv_w), const(q_w, q_w), const(kv_w, kv_w),
        ],
        out_specs=(pl.BlockSpec((L, q_w), lambda s: (s, 0)), cache_spec, cache_spec),
        out_shape=(jax.ShapeDtypeStruct((N_PROMPT_TOK, q_w), BF16), cache_shape, cache_shape),
        compiler_params=_params("arbitrary"),
        name="attn_prompt",
    )(z, z, z, *_attn_common_args(qn_g, kn_g))


def _attention_latent(z, qn_g, kn_g, cache_k, cache_v):
    L = LATENT_LEN
    nqb = L // Q_BLOCK
    q_w, kv_w = Q_HEADS * HEAD_DIM, KV_HEADS * HEAD_DIM
    q_col = (5 * A_WIDTH) // q_w
    k_col = (5 * A_WIDTH + q_w) // kv_w
    qrow0 = N_PROMPT_TOK // Q_BLOCK
    krow0 = N_PROMPT_TOK // L
    cos, sin = _rope_tables()
    cos_q, sin_q = jnp.asarray(np.tile(cos, (1, Q_HEADS))), jnp.asarray(np.tile(sin, (1, Q_HEADS)))
    cos_k, sin_k = jnp.asarray(np.tile(cos, (1, KV_HEADS))), jnp.asarray(np.tile(sin, (1, KV_HEADS)))
    const = lambda r, c: pl.BlockSpec((r, c), lambda s, b: (0, 0))
    cache_spec = pl.BlockSpec((None, None, KV_HEADS, PAST_LEN, HEAD_DIM), lambda s, b: (s, 0, 0, 0, 0))
    return pl.pallas_call(
        functools.partial(_attn_kernel, latent=True),
        grid=(N_LATENT_SEQ, nqb),
        in_specs=[
            pl.BlockSpec((Q_BLOCK, q_w), lambda s, b: (qrow0 + s * nqb + b, q_col)),
            pl.BlockSpec((L, kv_w), lambda s, b: (krow0 + s, k_col)),
            pl.BlockSpec((L, kv_w), lambda s, b: (krow0 + s, k_col + 1)),
            const(1, q_w), const(1, kv_w), const(q_w, q_w), const(kv_w, kv_w),
            pl.BlockSpec((Q_BLOCK, q_w), lambda s, b: (b, 0)),
            pl.BlockSpec((Q_BLOCK, q_w), lambda s, b: (b, 0)),
            const(L, kv_w), const(L, kv_w),
            cache_spec, cache_spec,
        ],
        out_specs=pl.BlockSpec((Q_BLOCK, q_w), lambda s, b: (s * nqb + b, 0)),
        out_shape=jax.ShapeDtypeStruct((N_LATENT_TOK, q_w), BF16),
        compiler_params=_params("arbitrary", "arbitrary"),
        name="attn_latent",
    )(z, z, z, *_attn_common_args(qn_g, kn_g), cos_q, sin_q, cos_k, sin_k, cache_k, cache_v)


def _out_proj_kernel(*refs, n_act, n_x):
    a_refs = refs[:2 * n_act]
    x_refs = refs[2 * n_act:2 * n_act + n_x]
    g_ref, mod_ref, rw_ref, w_ref, xo_ref, h_ref, lg_ref, wb_ref, rws_ref, acc_ref = refs[2 * n_act + n_x:]
    _cast_once(w_ref, wb_ref)

    @pl.when(pl.program_id(0) == 0)
    def _():
        rw = rw_ref[...]
        hi = rw.astype(BF16).astype(F32)
        lo = (rw - hi).astype(BF16).astype(F32)
        rws_ref[...] = (hi + pltpu.roll(lo, N_EXPERTS, axis=1)).astype(BF16)

    mod = mod_ref[...]
    n = OUT_PROJ_SUB_ROWS

    n_sub = xo_ref.shape[0] // n

    def sub_rows(r):
        if isinstance(r, int):
            return slice(r * n, (r + 1) * n)
        return pl.ds(pl.multiple_of(r * n, n), n)

    def project(r):
        rows = sub_rows(r)
        acc = None
        k0 = 0
        for ap_ref, al_ref in zip(a_refs[0::2], a_refs[1::2]):
            k1 = k0 + ap_ref.shape[1]
            part = jnp.dot(_select_trunk(ap_ref, al_ref, rows), wb_ref[k0:k1, :], preferred_element_type=F32)
            acc = part if acc is None else acc + part
            k0 = k1
        acc_ref[r % 2] = acc

    def finish(r):
        rows = sub_rows(r)
        x_in = x_refs[0][rows, :] if n_x == 1 else _select_trunk(*x_refs, rows)
        x = x_in + mod[2:3, :] * acc_ref[r % 2]
        xo_ref[rows, :] = x
        h = _modulated_norm(x, g_ref[...], mod, 3, 4)
        h_ref[rows, :] = _pack_rows(h)
        h_hi = h.astype(BF16)
        h_lo = (h - h_hi.astype(F32)).astype(BF16)
        both = jnp.dot(jnp.concatenate([h_hi, h_lo], axis=0), rws_ref[...], preferred_element_type=F32)
        from_hi, from_lo = both[:n], both[n:]
        lg = from_hi + pltpu.roll(from_hi, ROUTER_LANES - N_EXPERTS, axis=1) + from_lo
        lg_ref[:, rows] = lg.T[:N_EXPERTS, :]

    project(0)
    for r in range(n_sub - 1):
        project(r + 1)
        finish(r)
    finish(n_sub - 1)


def _out_proj(acts, w, xs, g, mod_l, router_wp, block_rows=1024):
    tok = lambda width: pl.BlockSpec((block_rows, width), lambda i: (i, 0))
    in_specs = [spec for ap, _ in acts for spec in _trunk_specs(block_rows, ap.shape[1])]
    in_specs += [tok(D_MODEL)] if len(xs) == 1 else list(_trunk_specs(block_rows, D_MODEL))
    in_specs += [_resident((1, D_MODEL)), _mod_spec(block_rows), _resident((D_MODEL, ROUTER_LANES)),
                 _resident(w.shape)]
    return pl.pallas_call(
        functools.partial(_out_proj_kernel, n_act=len(acts), n_x=len(xs)),
        grid=(N_TOK // block_rows,),
        in_specs=in_specs,
        out_specs=(tok(D_MODEL), tok(ROW_WORDS), pl.BlockSpec((N_EXPERTS, block_rows), lambda i: (0, i))),
        out_shape=(jax.ShapeDtypeStruct((N_TOK, D_MODEL), F32),
                   jax.ShapeDtypeStruct((N_TOK, ROW_WORDS), jnp.int32),
                   jax.ShapeDtypeStruct((N_EXPERTS, N_TOK), F32)),
        scratch_shapes=[pltpu.VMEM(w.shape, BF16), pltpu.VMEM((D_MODEL, ROUTER_LANES), BF16),
                        pltpu.VMEM((2, OUT_PROJ_SUB_ROWS, D_MODEL), F32)],
        compiler_params=_params("arbitrary"),
        name="out_proj",
    )(*[a for pair in acts for a in pair], *xs, g.reshape(1, D_MODEL), mod_l, router_wp, w)


def _router_kernel(lg_ref, rb_ref, pos_ref, w_ref, plan_ref, rank_ref):
    lg = lg_ref[...]
    ex = jnp.exp(lg - jnp.max(lg, axis=0, keepdims=True))
    scores = ex / jnp.sum(ex, axis=0, keepdims=True)
    biased = scores + rb_ref[...]
    rows = [biased[e:e + 1, :] for e in range(N_EXPERTS)]
    selected = []
    group_score = []
    for gi in range(N_GROUPS):
        r = rows[gi * EXPERTS_PER_GROUP:(gi + 1) * EXPERTS_PER_GROUP]
        total = None
        for i in range(EXPERTS_PER_GROUP):
            rank = None
            for j in range(EXPERTS_PER_GROUP):
                if j == i:
                    continue
                ahead = (r[j] > r[i]) if j > i else (r[j] >= r[i])
                ahead = jnp.where(ahead, 1.0, 0.0)
                rank = ahead if rank is None else rank + ahead
            sel = rank < 1.5
            selected.append(sel)
            contrib = jnp.where(sel, r[i], 0.0)
            total = contrib if total is None else total + contrib
        group_score.append(total)
    best = group_score[0]
    best_group = jnp.zeros_like(best)
    for gi in range(1, N_GROUPS):
        better = group_score[gi] > best
        best_group = jnp.where(better, float(gi), best_group)
        best = jnp.where(better, group_score[gi], best)
    picked = []
    chosen = []
    den = None
    for e in range(N_EXPERTS):
        in_group = best_group == float(e // EXPERTS_PER_GROUP)
        use = jnp.where(selected[e], jnp.where(in_group, 1.0, 0.0), 0.0)
        w = use * scores[e:e + 1, :]
        chosen.append(use)
        picked.append(w)
        den = w if den is None else den + w
    lanes = 128
    n_blk = N_TOK // lanes
    li = lax.broadcasted_iota(jnp.int32, (lanes, lanes), 0)
    lj = lax.broadcasted_iota(jnp.int32, (lanes, lanes), 1)
    prefix = jnp.where(li <= lj, 1.0, 0.0).astype(BF16)
    carry = jnp.zeros((N_EXPERTS, 1), F32)
    for blk in range(n_blk):
        cols = slice(blk * lanes, (blk + 1) * lanes)
        m = jnp.concatenate([chosen[e][:, cols] for e in range(N_EXPERTS)], axis=0)
        incl = jnp.dot(m.astype(BF16), prefix, preferred_element_type=F32)
        rank_ref[:, cols] = incl - m + carry
        carry = carry + incl[:, lanes - 1:lanes]
    count = carry
    padded = jnp.floor((count + float(MOE_TILE - 1)) * (1.0 / MOE_TILE)) * float(MOE_TILE)
    erow = lax.broadcasted_iota(jnp.int32, (N_EXPERTS, 1), 0)
    offset = jnp.zeros((N_EXPERTS, 1), F32)
    for e in range(N_EXPERTS - 1):
        offset = offset + jnp.where(erow > e, padded[e:e + 1, :], 0.0)
    seen = jnp.zeros_like(den)
    pos_a = jnp.zeros_like(den)
    pos_b = jnp.zeros_like(den)
    w_a = jnp.zeros_like(den)
    w_b = jnp.zeros_like(den)
    for e in range(N_EXPERTS):
        pos_e = rank_ref[e:e + 1, :] + offset[e:e + 1, :]
        gate_e = picked[e] / den
        first = jnp.where(seen < 0.5, chosen[e], 0.0) > 0.5
        second = jnp.where(seen > 0.5, chosen[e], 0.0) > 0.5
        pos_a = jnp.where(first, pos_e, pos_a)
        w_a = jnp.where(first, gate_e, w_a)
        pos_b = jnp.where(second, pos_e, pos_b)
        w_b = jnp.where(second, gate_e, w_b)
        seen = seen + chosen[e]
    pos_ref[0:1, :] = pos_a.astype(jnp.int32)
    pos_ref[1:2, :] = pos_b.astype(jnp.int32)
    w_rows = jnp.concatenate([w_a, w_b, jnp.zeros((6, N_TOK), F32)], axis=0)
    ei = lax.broadcasted_iota(jnp.int32, (8, lanes), 0)
    ej = lax.broadcasted_iota(jnp.int32, (8, lanes), 1)
    eye = jnp.where(ei == ej, 1.0, 0.0).astype(BF16)
    tn = (((0,), (0,)), ((), ()))
    hi = w_rows.astype(BF16)
    r1 = w_rows - hi.astype(F32)
    mid = r1.astype(BF16)
    lo = (r1 - mid.astype(F32)).astype(BF16)
    w_cols = lax.dot_general(hi, eye, tn, preferred_element_type=F32)
    w_cols = w_cols + lax.dot_general(mid, eye, tn, preferred_element_type=F32)
    w_cols = w_cols + lax.dot_general(lo, eye, tn, preferred_element_type=F32)
    w_ref[...] = w_cols[:, :TOP_K]
    start = (lax.broadcasted_iota(jnp.int32, (N_EXPERTS, lanes), 1) * MOE_TILE).astype(F32)
    end = offset + padded
    tile_expert = jnp.sum(jnp.where(end <= start, 1.0, 0.0), axis=0, keepdims=True)
    inside = (offset <= start) & (start < end)
    real = jnp.clip(count - (start - offset), 0.0, float(MOE_TILE))
    tile_rows = jnp.sum(jnp.where(inside, real, 0.0), axis=0, keepdims=True)
    plan_ref[0:1, :] = jnp.minimum(tile_expert, float(N_EXPERTS - 1)).astype(jnp.int32)
    plan_ref[1:2, :] = tile_rows.astype(jnp.int32)


def _router(logits_t, router_b):
    whole = lambda shape: pl.BlockSpec(shape, lambda i: (0, 0))
    return pl.pallas_call(
        _router_kernel,
        grid=(1,),
        in_specs=[whole((N_EXPERTS, N_TOK)), whole((N_EXPERTS, 1))],
        out_specs=(whole((2, N_TOK)), whole((N_TOK, TOP_K)), whole((2, 128))),
        out_shape=(jax.ShapeDtypeStruct((2, N_TOK), jnp.int32),
                   jax.ShapeDtypeStruct((N_TOK, TOP_K), F32),
                   jax.ShapeDtypeStruct((2, 128), jnp.int32)),
        scratch_shapes=[pltpu.VMEM((N_EXPERTS, N_TOK), F32)],
        compiler_params=_params("arbitrary"),
        name="router",
    )(logits_t, router_b.reshape(N_EXPERTS, 1))


def _sc_mesh():
    return plsc.VectorSubcoreMesh(core_axis_name="c", subcore_axis_name="s")


def _sc_worker_base():
    return (lax.axis_index("s") * SC_CORES + lax.axis_index("c")) * SC_TOKENS_PER_WORKER


def _moe_dispatch(h, pos_a, pos_b):
    n_chunks = SC_TOKENS_PER_WORKER // SC_CHUNK
    idx = pltpu.VMEM((SC_CHUNK,), jnp.int32)

    @functools.partial(
        pl.kernel, mesh=_sc_mesh(),
        out_type=jax.ShapeDtypeStruct((MOE_ROWS, ROW_WORDS), jnp.int32),
        scratch_types=[idx, idx, idx, idx, pltpu.VMEM((2, SC_CHUNK, ROW_WORDS), jnp.int32),
                       pltpu.SemaphoreType.DMA((6,)), pltpu.SemaphoreType.DMA((4,))],
        name="moe_dispatch",
    )
    def run(h_hbm, pa_hbm, pb_hbm, xs_hbm, ia0, ib0, ia1, ib1, rows_v, sem_in, sem_out):
        base = _sc_worker_base()
        ia, ib = (ia0, ia1), (ib0, ib1)

        def start_loads(c):
            slot = c % 2
            tok = pl.ds(pl.multiple_of(base + c * SC_CHUNK, 8), SC_CHUNK)
            return (pltpu.async_copy(pa_hbm.at[tok], ia[slot], sem_in.at[3 * slot]),
                    pltpu.async_copy(pb_hbm.at[tok], ib[slot], sem_in.at[3 * slot + 1]),
                    pltpu.async_copy(h_hbm.at[tok], rows_v.at[slot], sem_in.at[3 * slot + 2]))

        loads = start_loads(0)
        scatters = [(), ()]
        for c in range(n_chunks):
            slot = c % 2
            for cp in loads:
                cp.wait()
            if c + 1 < n_chunks:
                for cp in scatters[1 - slot]:
                    cp.wait()
                scatters[1 - slot] = ()
                loads = start_loads(c + 1)
            scatters[slot] = (pltpu.async_copy(rows_v.at[slot], xs_hbm.at[ia[slot]], sem_out.at[2 * slot]),
                              pltpu.async_copy(rows_v.at[slot], xs_hbm.at[ib[slot]], sem_out.at[2 * slot + 1]))
        for pending in scatters:
            for cp in pending:
                cp.wait()

    return run(h, pos_a, pos_b)


def _moe_collect(ys, pos_a, pos_b):
    n_chunks = SC_TOKENS_PER_WORKER // SC_CHUNK
    out = jax.ShapeDtypeStruct((N_TOK, ROW_WORDS), jnp.int32)
    idx = pltpu.VMEM((SC_TOKENS_PER_WORKER,), jnp.int32)
    rows = pltpu.VMEM((2, SC_CHUNK, ROW_WORDS), jnp.int32)

    @functools.partial(
        pl.kernel, mesh=_sc_mesh(), out_type=(out, out),
        scratch_types=[idx, idx, rows, rows, pltpu.SemaphoreType.DMA((4,)), pltpu.SemaphoreType.DMA((4,))],
        name="moe_collect",
    )
    def run(ys_hbm, pa_hbm, pb_hbm, ya_hbm, yb_hbm, ia_v, ib_v, ra_v, rb_v, sem_g, sem_w):
        base = _sc_worker_base()
        mine = pl.ds(pl.multiple_of(base, 8), SC_TOKENS_PER_WORKER)
        pltpu.sync_copy(pa_hbm.at[mine], ia_v)
        pltpu.sync_copy(pb_hbm.at[mine], ib_v)
        writes = [(), ()]
        for c in range(n_chunks):
            slot = c % 2
            for cp in writes[slot]:
                cp.wait()
            part = pl.ds(c * SC_CHUNK, SC_CHUNK)
            tok = pl.ds(pl.multiple_of(base + c * SC_CHUNK, 8), SC_CHUNK)
            ga = pltpu.async_copy(ys_hbm.at[ia_v.at[part]], ra_v.at[slot], sem_g.at[slot])
            gb = pltpu.async_copy(ys_hbm.at[ib_v.at[part]], rb_v.at[slot], sem_g.at[2 + slot])
            ga.wait()
            wa = pltpu.async_copy(ra_v.at[slot], ya_hbm.at[tok], sem_w.at[slot])
            gb.wait()
            wb = pltpu.async_copy(rb_v.at[slot], yb_hbm.at[tok], sem_w.at[2 + slot])
            writes[slot] = (wa, wb)
        for pending in writes:
            for cp in pending:
                cp.wait()

    return run(ys, pos_a, pos_b)


def _experts_kernel(plan_ref, xs_ref, wg_hbm, wu_hbm, wd_hbm, y_ref,
                    sg_ref, su_ref, sd_ref, wgb_ref, wub_ref, wdb_ref, hid_ref, sems, seg_ref, *, layer):
    j = pl.program_id(0)
    n_tiles = pl.num_programs(0)
    expert = plan_ref[j]
    n_real = plan_ref[PLAN_LANES + j]
    fresh = jnp.logical_or(j == 0, expert != plan_ref[jnp.maximum(j - 1, 0)])

    def weight_copies(e, slot):
        return (pltpu.make_async_copy(wg_hbm.at[layer, e], sg_ref.at[slot], sems.at[slot, 0]),
                pltpu.make_async_copy(wu_hbm.at[layer, e], su_ref.at[slot], sems.at[slot, 1]),
                pltpu.make_async_copy(wd_hbm.at[layer, e], sd_ref.at[slot], sems.at[slot, 2]))

    @pl.when(j == 0)
    def _():
        seg_ref[0] = 0

        @pl.when(n_real > 0)
        def _():
            for cp in weight_copies(expert, 0):
                cp.start()

    @pl.when(jnp.logical_and(n_real > 0, fresh))
    def _():
        slot = seg_ref[0] % 2
        for cp in weight_copies(expert, slot):
            cp.wait()
        wgb_ref[...] = sg_ref[slot].astype(BF16)
        wub_ref[...] = su_ref[slot].astype(BF16)
        wdb_ref[...] = sd_ref[slot].astype(BF16)
        nxt = lax.while_loop(lambda t: jnp.logical_and(t < n_tiles, plan_ref[jnp.minimum(t, n_tiles - 1)] == expert),
                             lambda t: t + 1, j + 1)
        nxt_c = jnp.minimum(nxt, n_tiles - 1)

        @pl.when(jnp.logical_and(nxt < n_tiles, plan_ref[PLAN_LANES + nxt_c] > 0))
        def _():
            for cp in weight_copies(plan_ref[nxt_c], 1 - slot):
                cp.start()

        seg_ref[0] = seg_ref[0] + 1

    @pl.when(n_real > 0)
    def _():
        n = EXPERT_SUB_ROWS
        n_sub = xs_ref.shape[0] // n
        row = lax.broadcasted_iota(jnp.int32, (n, xs_ref.shape[1]), 0)

        def up(r):
            rows = slice(r * n, (r + 1) * n)
            words = jnp.where(row < n_real - r * n, xs_ref[rows, :], 0)
            x = _unpack_rows(words).astype(BF16)
            a = jnp.dot(x, wgb_ref[...], preferred_element_type=F32)
            b = jnp.dot(x, wub_ref[...], preferred_element_type=F32)
            hid_ref[r] = ((a * jax.nn.sigmoid(a)) * b).astype(BF16)

        def down(r):
            rows = slice(r * n, (r + 1) * n)
            y_ref[rows, :] = _pack_rows(jnp.dot(hid_ref[r], wdb_ref[...], preferred_element_type=F32))

        up(0)
        for r in range(1, n_sub):
            up(r)
            down(r - 1)
        down(n_sub - 1)


def _experts(plan, xs, w_gate, w_up, w_down, layer):
    hbm = pl.BlockSpec(memory_space=pl.ANY)
    return pl.pallas_call(
        functools.partial(_experts_kernel, layer=layer),
        grid_spec=pltpu.PrefetchScalarGridSpec(
            num_scalar_prefetch=1,
            grid=(MOE_ROWS // MOE_TILE,),
            in_specs=[pl.BlockSpec((MOE_TILE, ROW_WORDS), lambda j, plan: (j, 0)), hbm, hbm, hbm],
            out_specs=pl.BlockSpec((MOE_TILE, ROW_WORDS), lambda j, plan: (j, 0)),
            scratch_shapes=[pltpu.VMEM((2, D_MODEL, D_EXPERT), F32), pltpu.VMEM((2, D_MODEL, D_EXPERT), F32),
                            pltpu.VMEM((2, D_EXPERT, D_MODEL), F32),
                            pltpu.VMEM((D_MODEL, D_EXPERT), BF16), pltpu.VMEM((D_MODEL, D_EXPERT), BF16),
                            pltpu.VMEM((D_EXPERT, D_MODEL), BF16),
                            pltpu.VMEM((MOE_TILE // EXPERT_SUB_ROWS, EXPERT_SUB_ROWS, D_EXPERT), BF16),
                            pltpu.SemaphoreType.DMA((2, 3)), pltpu.SMEM((1,), jnp.int32)],
        ),
        out_shape=jax.ShapeDtypeStruct((MOE_ROWS, ROW_WORDS), jnp.int32),
        compiler_params=_params("arbitrary"),
        name="experts",
    )(plan, xs, w_gate, w_up, w_down)


def _combine_kernel(x_ref, ya_ref, yb_ref, wt_ref, mod_ref, o_ref):
    o_ref[...] = _moe_mix(x_ref, ya_ref, yb_ref, wt_ref, mod_ref)


def _combine(x, moe_out, mod_l, tok0, n_tok, block_rows=512):
    ya, yb, w_tok = moe_out
    b0 = tok0 // block_rows
    rows = lambda width: pl.BlockSpec((block_rows, width), lambda i: (b0 + i, 0))
    return pl.pallas_call(
        _combine_kernel,
        grid=(n_tok // block_rows,),
        in_specs=[rows(D_MODEL), rows(ROW_WORDS), rows(ROW_WORDS), rows(TOP_K),
                  pl.BlockSpec((None, 6, D_MODEL), lambda i: (_cond_of_token_block(b0 + i, block_rows), 0, 0))],
        out_specs=pl.BlockSpec((block_rows, D_MODEL), lambda i: (i, 0)),
        out_shape=jax.ShapeDtypeStruct((n_tok, D_MODEL), F32),
        compiler_params=_params("arbitrary"),
        name="combine",
    )(x, ya, yb, w_tok, mod_l)


def _moe(h, logits_t, router_b, w_gate, w_up, w_down, layer):
    pos, w, plan = _router(logits_t, router_b)
    xs = _moe_dispatch(h, pos[0], pos[1])
    ys = _experts(plan.reshape(-1), xs, w_gate, w_up, w_down, layer)
    ya, yb = _moe_collect(ys, pos[0], pos[1])
    return ya, yb, w


def _dft_tables(L):
    k = np.arange(L)[:, None]
    m = np.arange(L)[None, :]
    r = (k * m) % (2 * L)
    ang = np.pi * r.astype(np.float64) / L
    fc = np.cos(ang)
    fs = np.sin(ang)
    fs[0, :] = np.where(np.arange(L) % 2 == 0, 1.0, -1.0)
    wk = np.full((L, 1), 1.0 / L)
    wk[0, 0] = 0.5 / L
    gc = (fc * wk).T
    gs = (fs * wk).T
    return [jnp.asarray(t.astype(np.float32)).astype(BF16) for t in (fc, fs, gc, gs)]


def _filter_consts(L):
    t = np.linspace(0.0, 1.0, L, dtype=np.float32)[:, None]
    w = (np.float32(2.0 * np.pi) * np.arange(L, dtype=np.float32)[:, None] / np.float32(L)).astype(np.float32)
    fb = np.linspace(1e-4, HY_BANDS - 1, HY_BANDS, dtype=np.float32)[None, :]
    emb = np.concatenate([t, np.cos(fb * w), -np.sin(fb * w)], axis=-1).astype(np.float32)
    lo = math.log(HY_DECAY_TARGET) / HY_SLOW_PCT
    hi = math.log(HY_DECAY_TARGET) / HY_FAST_PCT
    deltas = np.abs(np.linspace(lo, hi, D_MODEL, dtype=np.float32))
    decay = np.exp(-t * deltas).astype(np.float32)
    return jnp.asarray(emb), jnp.asarray(decay)


def _filter_kernel(emb_ref, w1_ref, b1_ref, w2_ref, b2_ref, fr_ref, w3f_ref, w3b_ref, dec_ref,
                   fc_ref, fs_ref, kr_ref, q_ref, krn_ref, hd_ref):
    @pl.when(pl.program_id(0) == 0)
    def _():
        fr = fr_ref[...]
        h1 = jnp.sin(fr * (jnp.dot(emb_ref[...], w1_ref[...], precision=HIGHEST,
                                   preferred_element_type=F32) + b1_ref[...]))
        hd_ref[...] = jnp.sin(fr * (jnp.dot(h1, w2_ref[...], precision=HIGHEST,
                                            preferred_element_type=F32) + b2_ref[...]))

    hd = hd_ref[...]
    dec = dec_ref[...]
    f = jnp.dot(hd, w3f_ref[...], precision=HIGHEST, preferred_element_type=F32) * dec
    g = jnp.dot(hd, w3b_ref[...], precision=HIGHEST, preferred_element_type=F32) * dec
    row = lax.broadcasted_iota(jnp.int32, f.shape, 0)
    g = jnp.where(row == 0, 0.0, g)
    s = f + g
    d = f - g
    kr = jnp.dot(fc_ref[...], s.astype(BF16), preferred_element_type=F32)
    qq = jnp.dot(fs_ref[...], d.astype(BF16), preferred_element_type=F32)
    alt = jnp.where(row % 2 == 0, 1.0, -1.0)
    nyq = jnp.sum(alt * s, axis=0, keepdims=True)
    kr_ref[...] = kr
    q_ref[...] = jnp.where(row == 0, 0.0, qq)
    krn_ref[...] = jnp.where(row == 0, nyq, kr)


def _hyena_filter_spectrum(L, w1, b1, w2, b2, w3, freq, fc, fs, cblk=256):
    emb, decay = _filter_consts(L)
    ncb = D_MODEL // cblk
    n_emb = 128
    emb = jnp.pad(emb, ((0, 0), (0, n_emb - emb.shape[1])))
    w1 = jnp.pad(w1, ((0, n_emb - w1.shape[0]), (0, 0)))
    full = lambda shape: pl.BlockSpec(shape, lambda j: tuple(0 for _ in shape))
    out_sds = jax.ShapeDtypeStruct((L, D_MODEL), F32)
    out_spec = pl.BlockSpec((L, cblk), lambda j: (0, j))
    return pl.pallas_call(
        _filter_kernel,
        grid=(ncb,),
        in_specs=[
            full((L, n_emb)), full((n_emb, HY_FFN)), full((1, HY_FFN)), full((HY_FFN, HY_FFN)),
            full((1, HY_FFN)), full((1, HY_FFN)),
            pl.BlockSpec((HY_FFN, cblk), lambda j: (0, j)),
            pl.BlockSpec((HY_FFN, cblk), lambda j: (0, ncb + j)),
            pl.BlockSpec((L, cblk), lambda j: (0, j)),
            full((L, L)), full((L, L)),
        ],
        out_specs=(out_spec, out_spec, out_spec),
        out_shape=(out_sds, out_sds, out_sds),
        scratch_shapes=[pltpu.VMEM((L, HY_FFN), F32)],
        compiler_params=_params("arbitrary"),
        name=f"hyena_filter_{L}",
    )(emb, w1, b1.reshape(1, HY_FFN), w2, b2.reshape(1, HY_FFN), freq.reshape(1, HY_FFN), w3, w3, decay, fc, fs)


def _hyena_conv_kernel(x0_ref, x1_ref, v_ref, cw0_ref, cw1_ref, cwv_ref, cb0_ref, cb1_ref, cbv_ref,
                       kr_ref, q_ref, krn_ref, ds_ref, fc_ref, fs_ref, gc_ref, gs_ref, o_ref):
    L = fc_ref.shape[0]
    row = lax.broadcasted_iota(jnp.int32, (L, x0_ref.shape[1]), 0)

    def one_sequence(s, carry):
        rows = pl.ds(pl.multiple_of(s * L, L), L)

        def short_conv(u_ref, w_ref, b_ref):
            u = u_ref[rows, :].astype(F32)
            w = w_ref[...]
            prev = jnp.where(row == 0, 0.0, pltpu.roll(u, 1, axis=0))
            nxt = jnp.where(row == L - 1, 0.0, pltpu.roll(u, L - 1, axis=0))
            return prev * w[0:1, :] + u * w[1:2, :] + nxt * w[2:3, :] + b_ref[...]

        x0 = short_conv(x0_ref, cw0_ref, cb0_ref)
        x1 = short_conv(x1_ref, cw1_ref, cb1_ref)
        v = short_conv(v_ref, cwv_ref, cbv_ref)
        zz = v * x1
        zb = zz.astype(BF16)
        ur = jnp.dot(fc_ref[...], zb, preferred_element_type=F32)
        p = jnp.dot(fs_ref[...], zb, preferred_element_type=F32)
        qq = q_ref[...]
        yr = ur * kr_ref[...] - p * qq
        yw = ur * qq + p * krn_ref[...]
        y = jnp.dot(gc_ref[...], yr.astype(BF16), preferred_element_type=F32)
        y = y + jnp.dot(gs_ref[...], yw.astype(BF16), preferred_element_type=F32)
        o_ref[rows, :] = (x0 * (y + zz * ds_ref[...])).astype(o_ref.dtype)
        return carry

    lax.fori_loop(0, x0_ref.shape[0] // L, one_sequence, 0)


def _hyena_conv(u, conv_w, conv_b, dskip, spectrum, tables, *, latent):
    L = LATENT_LEN if latent else PROMPT_LEN
    n_seq = N_LATENT_SEQ if latent else N_PROMPT_SEQ
    cblk = 256 if latent else 512
    ncb = D_MODEL // cblk
    seqs = 1 if latent else 4
    row0 = (N_PROMPT_TOK // L) if latent else 0
    kr, qq, krn = spectrum
    fc, fs, gc, gs = tables

    def part(p, rows):
        if rows != L:
            return pl.BlockSpec((rows, cblk), lambda j, s: (0, p * ncb + j))
        return pl.BlockSpec((seqs * L, cblk), lambda j, s: (row0 // seqs + s, p * ncb + j))

    def const_cols(rows):
        return pl.BlockSpec((rows, cblk), lambda j, s: (0, j))

    mat = pl.BlockSpec((L, L), lambda j, s: (0, 0))
    conv_b2 = conv_b.reshape(1, 3 * D_MODEL)
    in_specs = [part(0, L), part(1, L), part(2, L),
                part(0, 3), part(1, 3), part(2, 3),
                part(0, 1), part(1, 1), part(2, 1),
                const_cols(L), const_cols(L), const_cols(L), const_cols(1),
                mat, mat, mat, mat]
    args = [u, u, u, conv_w, conv_w, conv_w, conv_b2, conv_b2, conv_b2,
            kr, qq, krn, dskip.reshape(1, D_MODEL), fc, fs, gc, gs]
    return pl.pallas_call(
        _hyena_conv_kernel,
        grid=(ncb, n_seq // seqs),
        in_specs=in_specs,
        out_specs=pl.BlockSpec((seqs * L, cblk), lambda j, s: (s, j)),
        out_shape=jax.ShapeDtypeStruct((n_seq * L, D_MODEL), BF16),
        compiler_params=_params("arbitrary", "arbitrary"),
        name="hyena_conv_latent" if latent else "hyena_conv_prompt",
    )(*args)


def kernel(x_prompt, x_sample, cache_k, cache_v, state_hgrn, c, c_ctx, norm_g, mod_w, mod_b, ab_in_w, hgrn_lb, hgrn_onorm_g, attn_qnorm_g, attn_knorm_g, ab_out_w, hy_in_w, hy_in_b, hy_conv_w, hy_conv_b, hy_f_w1, hy_f_b1, hy_f_w2, hy_f_b2, hy_f_w3, hy_f_freq, hy_dskip, hy_out_w, router_w, router_b, moe_w_gate, moe_w_up, moe_w_down):
    xp = x_prompt.reshape(N_PROMPT_TOK, D_MODEL)
    xl = x_sample.reshape(N_LATENT_TOK, D_MODEL)
    cond = jnp.concatenate([c_ctx[None, :], c, jnp.zeros((N_COND - 1 - N_LATENT_SEQ, D_MODEL), F32)], axis=0)
    mod = _modulation(cond, mod_w, mod_b)
    router_wp = jnp.pad(router_w, ((0, 0), (0, ROUTER_LANES - N_EXPERTS)))

    z = _in_proj0(xp, xl, norm_g[0, 0], mod[0], ab_in_w[0])
    oa_p, new_state = _hgrn(z, hgrn_lb, hgrn_onorm_g[0], None, latent=False)
    oa_l = _hgrn(z, hgrn_lb, hgrn_onorm_g[0], state_hgrn, latent=True)
    ob_p, k_fm, v_fm = _attention_prompt(z, attn_qnorm_g[0], attn_knorm_g[0])
    fm_shape = (N_PROMPT_SEQ, 1, KV_HEADS, HEAD_DIM, PROMPT_LEN)
    new_k = jnp.swapaxes(k_fm.reshape(fm_shape), -1, -2)
    new_v = jnp.swapaxes(v_fm.reshape(fm_shape), -1, -2)
    ob_l = _attention_latent(z, attn_qnorm_g[0], attn_knorm_g[0], cache_k, cache_v)
    x, h, logits_t = _out_proj([(oa_p, oa_l), (ob_p, ob_l)], ab_out_w[0], (xp, xl), norm_g[0, 1], mod[0],
                               router_wp)
    moe_out = _moe(h, logits_t, router_b, moe_w_gate, moe_w_up, moe_w_down, 0)

    x, u = _in_proj1(x, moe_out, mod[0], norm_g[1, 0], mod[1], hy_in_w[0], hy_in_b[0])
    pre = []
    for latent in (False, True):
        L = LATENT_LEN if latent else PROMPT_LEN
        tables = _dft_tables(L)
        spectrum = _hyena_filter_spectrum(L, hy_f_w1[0], hy_f_b1[0], hy_f_w2[0], hy_f_b2[0], hy_f_w3[0],
                                          hy_f_freq[0], tables[0], tables[1])
        pre.append(_hyena_conv(u, hy_conv_w[0], hy_conv_b[0], hy_dskip[0], spectrum, tables, latent=latent))
    x, h, logits_t = _out_proj([tuple(pre)], hy_out_w[0], (x,), norm_g[1, 1], mod[1], router_wp)
    moe_out = _moe(h, logits_t, router_b, moe_w_gate, moe_w_up, moe_w_down, 1)

    y_prompt = _combine(x, moe_out, mod[1], 0, N_PROMPT_TOK).reshape(N_PROMPT_SEQ, PROMPT_LEN, D_MODEL)
    y_sample = _combine(x, moe_out, mod[1], N_PROMPT_TOK, N_LATENT_TOK).reshape(N_LATENT_SEQ, LATENT_LEN, D_MODEL)
    return (y_prompt, y_sample, new_k, new_v, new_state)
```

```python
import functools
import math

import numpy as np
import jax
import jax.numpy as jnp
from jax import lax
from jax.experimental import pallas as pl
from jax.experimental.pallas import tpu as pltpu
from jax.experimental.pallas import tpu_sc as plsc

F32 = jnp.float32
BF16 = jnp.bfloat16
HIGHEST = lax.Precision.HIGHEST

D_MODEL = 1024
N_PROMPT_SEQ = 32
PROMPT_LEN = 256
N_LATENT_SEQ = 2
LATENT_LEN = 1024
PAST_LEN = 512
GRID_W = 64
N_PROMPT_TOK = N_PROMPT_SEQ * PROMPT_LEN
N_LATENT_TOK = N_LATENT_SEQ * LATENT_LEN
N_TOK = N_PROMPT_TOK + N_LATENT_TOK
N_COND = 8
EPS = 1e-6

A_WIDTH = 512
A_HEADS = 4
A_DK = 128
CHUNK = 64
HGRN_BLOCK = 128
HGRN_HEADS_PER_STEP = 4
HEAD_DIM = 64
Q_HEADS = 8
KV_HEADS = 2
Q_PER_KV = Q_HEADS // KV_HEADS
Q_BLOCK = 256
ROPE_THETA = 10000.0
ROPE_PAIRS = HEAD_DIM // 4
AB_IN = 5 * A_WIDTH + (Q_HEADS + 2 * KV_HEADS) * HEAD_DIM

HY_BANDS = 16
HY_FFN = 64
HY_DECAY_TARGET = 1e-2
HY_FAST_PCT = 0.3
HY_SLOW_PCT = 1.5

N_EXPERTS = 16
N_GROUPS = 4
EXPERTS_PER_GROUP = 4
TOP_K = 2
D_EXPERT = 512
ROUTER_LANES = 128
OUT_PROJ_SUB_ROWS = 256
EXPERT_SUB_ROWS = 256
IN_PROJ_SUB_ROWS = 256
MOE_TILE = 512
MOE_ROWS = N_TOK * TOP_K + N_EXPERTS * MOE_TILE
PLAN_LANES = 128

SC_CORES = 2
SC_WORKERS = 32
SC_TOKENS_PER_WORKER = N_TOK // SC_WORKERS
SC_CHUNK = 40
ROW_WORDS = D_MODEL // 2

VMEM_LIMIT = 56 * 1024 * 1024


def _params(*sem):
    return pltpu.CompilerParams(dimension_semantics=sem, vmem_limit_bytes=VMEM_LIMIT)


def _pack_rows(x):
    n = x.shape[1] // 2
    bits = pltpu.bitcast(x.astype(BF16).astype(F32), jnp.uint32)
    return pltpu.bitcast(bits[:, :n] | (bits[:, n:] >> 16), jnp.int32)


def _unpack_rows(p):
    bits = pltpu.bitcast(p, jnp.uint32)
    hi = pltpu.bitcast(bits & jnp.uint32(0xFFFF0000), F32)
    lo = pltpu.bitcast(bits << 16, F32)
    return jnp.concatenate([hi, lo], axis=1)


def _cond_of_token_block(i, block_rows):
    start = i * block_rows
    return jnp.where(start < N_PROMPT_TOK, 0, 1 + (start - N_PROMPT_TOK) // LATENT_LEN)


def _mod_kernel(cond_ref, w_ref, b_ref, o_ref):
    cnd = cond_ref[...]
    s = cnd * jax.nn.sigmoid(cnd)
    s_hi = s.astype(BF16)
    s_lo = (s - s_hi.astype(F32)).astype(BF16)
    w = w_ref[...]
    w_hi = w.astype(BF16)
    w_lo = (w - w_hi.astype(F32)).astype(BF16)
    acc = jnp.dot(s_hi, w_hi, preferred_element_type=F32)
    acc = acc + jnp.dot(s_lo, w_hi, preferred_element_type=F32)
    acc = acc + jnp.dot(s_hi, w_lo, preferred_element_type=F32)
    o_ref[...] = acc + b_ref[...]


def _modulation(cond, mod_w, mod_b):
    depth = mod_w.shape[0]
    n_chunk = 6
    out = pl.pallas_call(
        _mod_kernel,
        grid=(depth, n_chunk),
        in_specs=[
            pl.BlockSpec((N_COND, D_MODEL), lambda l, j: (0, 0)),
            pl.BlockSpec((None, D_MODEL, D_MODEL), lambda l, j: (l, 0, j)),
            pl.BlockSpec((None, 1, D_MODEL), lambda l, j: (l, 0, j)),
        ],
        out_specs=pl.BlockSpec((None, N_COND, D_MODEL), lambda l, j: (l, 0, j)),
        out_shape=jax.ShapeDtypeStruct((depth, N_COND, n_chunk * D_MODEL), F32),
        compiler_params=_params("arbitrary", "arbitrary"),
        name="modulation",
    )(cond, mod_w, mod_b.reshape(depth, 1, n_chunk * D_MODEL))
    return out.reshape(depth, N_COND, n_chunk, D_MODEL)


def _modulated_norm(x, g, mod, shift_row, scale_row):
    ms = jnp.mean(x * x, axis=-1, keepdims=True)
    y = x * lax.rsqrt(ms + EPS) * g
    return y * (1.0 + mod[scale_row:scale_row + 1, :]) + mod[shift_row:shift_row + 1, :]


def _trunk_specs(block_rows, width):
    n_prompt_blocks = N_PROMPT_TOK // block_rows
    return (pl.BlockSpec((block_rows, width), lambda i: (jnp.minimum(i, n_prompt_blocks - 1), 0)),
            pl.BlockSpec((block_rows, width), lambda i: (jnp.maximum(i - n_prompt_blocks, 0), 0)))


def _select_trunk(p_ref, l_ref, rows=slice(None)):
    block_rows = p_ref.shape[0]
    return jnp.where(pl.program_id(0) < N_PROMPT_TOK // block_rows, p_ref[rows, :], l_ref[rows, :])


def _cast_once(w_ref, wb_ref):
    @pl.when(pl.program_id(0) == 0)
    def _():
        wb_ref[...] = w_ref[...].astype(BF16)


def _resident(shape):
    return pl.BlockSpec(shape, lambda i: tuple(0 for _ in shape), pipeline_mode=pl.Buffered(1))


def _mod_spec(block_rows):
    return pl.BlockSpec((None, 6, D_MODEL), lambda i: (_cond_of_token_block(i, block_rows), 0, 0))


def _in_proj0_kernel(xp_ref, xl_ref, g_ref, mod_ref, w_ref, o_ref, wb_ref, hb_ref):
    _cast_once(w_ref, wb_ref)
    n = IN_PROJ_SUB_ROWS
    n_sub = xp_ref.shape[0] // n

    def prepare(r):
        x = _select_trunk(xp_ref, xl_ref, slice(r * n, (r + 1) * n))
        hb_ref[r] = _modulated_norm(x, g_ref[...], mod_ref[...], 0, 1).astype(BF16)

    def project(r):
        u = jnp.dot(hb_ref[r], wb_ref[...], preferred_element_type=F32)
        o_ref[r * n:(r + 1) * n, :] = u.astype(o_ref.dtype)

    prepare(0)
    for r in range(1, n_sub):
        prepare(r)
        project(r - 1)
    project(n_sub - 1)


def _in_proj0(x_prompt, x_latent, g, mod_l, w, block_rows=512):
    n = w.shape[1]
    return pl.pallas_call(
        _in_proj0_kernel,
        grid=(N_TOK // block_rows,),
        in_specs=[*_trunk_specs(block_rows, D_MODEL), _resident((1, D_MODEL)), _mod_spec(block_rows),
                  _resident((D_MODEL, n))],
        out_specs=pl.BlockSpec((block_rows, n), lambda i: (i, 0)),
        out_shape=jax.ShapeDtypeStruct((N_TOK, n), BF16),
        scratch_shapes=[pltpu.VMEM((D_MODEL, n), BF16),
                        pltpu.VMEM((block_rows // IN_PROJ_SUB_ROWS, IN_PROJ_SUB_ROWS, D_MODEL), BF16)],
        compiler_params=_params("arbitrary"),
        name="in_proj0",
    )(x_prompt, x_latent, g.reshape(1, D_MODEL), mod_l, w)


def _moe_mix(x_ref, ya_ref, yb_ref, wt_ref, mod_ref, rows=slice(None)):
    wt = wt_ref[rows, :]
    mix = wt[:, 0:1] * _unpack_rows(ya_ref[rows, :]) + wt[:, 1:2] * _unpack_rows(yb_ref[rows, :])
    return x_ref[rows, :] + mod_ref[5:6, :] * mix


def _in_proj1_kernel(x_ref, ya_ref, yb_ref, wt_ref, modp_ref, g_ref, mod_ref, w_ref, b_ref, xo_ref, o_ref,
                     wb_ref, hb_ref):
    _cast_once(w_ref, wb_ref)
    n = IN_PROJ_SUB_ROWS
    n_sub = x_ref.shape[0] // n

    def prepare(r):
        rows = slice(r * n, (r + 1) * n)
        x = _moe_mix(x_ref, ya_ref, yb_ref, wt_ref, modp_ref, rows)
        xo_ref[rows, :] = x
        hb_ref[r] = _modulated_norm(x, g_ref[...], mod_ref[...], 0, 1).astype(BF16)

    def project(r):
        rows = slice(r * n, (r + 1) * n)
        u = jnp.dot(hb_ref[r], wb_ref[...], preferred_element_type=F32) + b_ref[...]
        o_ref[rows, :] = u.astype(o_ref.dtype)

    prepare(0)
    for r in range(1, n_sub):
        prepare(r)
        project(r - 1)
    project(n_sub - 1)


def _in_proj1(x, moe_out, mod_prev, g, mod_l, w, bias, block_rows=512):
    ya, yb, w_tok = moe_out
    n = w.shape[1]
    tok = pl.BlockSpec((block_rows, D_MODEL), lambda i: (i, 0))
    packed = pl.BlockSpec((block_rows, ROW_WORDS), lambda i: (i, 0))
    return pl.pallas_call(
        _in_proj1_kernel,
        grid=(N_TOK // block_rows,),
        in_specs=[tok, packed, packed, pl.BlockSpec((block_rows, TOP_K), lambda i: (i, 0)), _mod_spec(block_rows),
                  _resident((1, D_MODEL)), _mod_spec(block_rows), _resident((D_MODEL, n)), _resident((1, n))],
        out_specs=(tok, pl.BlockSpec((block_rows, n), lambda i: (i, 0))),
        out_shape=(jax.ShapeDtypeStruct((N_TOK, D_MODEL), F32), jax.ShapeDtypeStruct((N_TOK, n), BF16)),
        scratch_shapes=[pltpu.VMEM((D_MODEL, n), BF16),
                        pltpu.VMEM((block_rows // IN_PROJ_SUB_ROWS, IN_PROJ_SUB_ROWS, D_MODEL), BF16)],
        compiler_params=_params("arbitrary"),
        name="in_proj1",
    )(x, ya, yb, w_tok, mod_prev, g.reshape(1, D_MODEL), mod_l, w, bias.reshape(1, n))


def _hgrn_kernel(*refs, seq_len, with_state):
    if with_state:
        (q_ref, zf_ref, zb_ref, i_ref, ga_ref, lb_ref, og_ref, s0_ref, o_ref, of_ref, ob_ref) = refs
    else:
        (q_ref, zf_ref, zb_ref, i_ref, ga_ref, lb_ref, og_ref, o_ref, s_ref, of_ref, ob_ref) = refs
    n_blocks = seq_len // HGRN_BLOCK
    chunks_per_block = HGRN_BLOCK // CHUNK

    lbr = lb_ref[...]
    mx = jnp.maximum(lbr[0], lbr[1])
    e0 = jnp.exp(lbr[0] - mx)
    e1 = jnp.exp(lbr[1] - mx)
    lb = e0 / (e0 + e1)

    row = lax.broadcasted_iota(jnp.int32, (HGRN_BLOCK, HGRN_BLOCK), 0)
    col = lax.broadcasted_iota(jnp.int32, (HGRN_BLOCK, HGRN_BLOCK), 1)
    same_chunk = (row // CHUNK) == (col // CHUNK)
    nt = (((1,), (1,)), ((), ()))
    tn = (((0,), (0,)), ((), ()))

    def per_chunk_row(x, idx):
        return jnp.concatenate(
            [jnp.broadcast_to(x[n * CHUNK + idx:n * CHUNK + idx + 1, :], (CHUNK, x.shape[1]))
             for n in range(chunks_per_block)], axis=0)

    def in_chunk_cumsum(tri, x):
        hi = x.astype(BF16)
        lo = (x - hi.astype(F32)).astype(BF16)
        return jnp.dot(tri, hi, preferred_element_type=F32) + jnp.dot(tri, lo, preferred_element_type=F32)

    def block(blk, cols, st, z_ref, lbd, forward, out_ref):
        rows = slice(blk * HGRN_BLOCK, (blk + 1) * HGRN_BLOCK)
        keep = (same_chunk & (col <= row)) if forward else (same_chunk & (col >= row))
        tri = jnp.where(keep, 1.0, 0.0).astype(BF16)
        mid = CHUNK // 2 if forward else CHUNK - 1 - CHUNK // 2
        last = CHUNK - 1 if forward else 0
        f = lbd + (1.0 - lbd) * jax.nn.sigmoid(z_ref[rows, cols].astype(F32))
        lf = jnp.log(f)
        k = 1.0 - f
        q = q_ref[rows, cols].astype(F32)
        vb = i_ref[rows, cols].astype(BF16)
        b = in_chunk_cumsum(tri, lf)
        bm = per_chunk_row(b, mid)
        bl = per_chunk_row(b, last)
        qe = (q * jnp.exp(b - bm)).astype(BF16)
        ke = (k * jnp.exp(bm - b)).astype(BF16)
        att = lax.dot_general(qe, ke, nt, preferred_element_type=F32)
        att = jnp.where(keep, att, 0.0)
        o_intra = jnp.dot(att.astype(BF16), vb, preferred_element_type=F32)
        qb = (q * jnp.exp(b)).astype(BF16)
        ks = (k * jnp.exp(bl - b)).astype(BF16)
        decay = jnp.exp(bl)
        order = range(chunks_per_block) if forward else range(chunks_per_block - 1, -1, -1)
        o_inter = [None] * chunks_per_block
        for n in order:
            cr = slice(n * CHUNK, (n + 1) * CHUNK)
            o_inter[n] = lax.dot_general(qb[cr], st.astype(BF16), nt, preferred_element_type=F32)
            upd = lax.dot_general(vb[cr], ks[cr], tn, preferred_element_type=F32)
            st = st * decay[n * CHUNK:n * CHUNK + 1, :] + upd
        out_ref[rows, cols] = o_intra + jnp.concatenate(o_inter, axis=0)
        return st

    for hd in range(q_ref.shape[1] // A_DK):
        cols = slice(hd * A_DK, (hd + 1) * A_DK)
        if with_state:
            st_f, st_b = s0_ref[0, hd].T, s0_ref[1, hd].T
        else:
            st_f, st_b = jnp.zeros((A_DK, A_DK), F32), jnp.zeros((A_DK, A_DK), F32)
        for step in range(n_blocks):
            st_f = block(step, cols, st_f, zf_ref, lb[0:1, cols], True, of_ref)
            st_b = block(n_blocks - 1 - step, cols, st_b, zb_ref, lb[1:2, cols], False, ob_ref)
        if not with_state:
            s_ref[0, hd] = st_f.T
            s_ref[1, hd] = st_b.T
        o = of_ref[:, cols] + ob_ref[:, cols]
        o = o * lax.rsqrt(jnp.mean(o * o, axis=-1, keepdims=True) + EPS) * og_ref[:, cols]
        ga = ga_ref[:, cols].astype(F32)
        o_ref[:, cols] = (o * (ga * jax.nn.sigmoid(ga))).astype(o_ref.dtype)


def _hgrn(z, hgrn_lb, onorm_g, state, *, latent):
    seq_len = LATENT_LEN if latent else PROMPT_LEN
    n_seq = N_LATENT_SEQ if latent else N_PROMPT_SEQ
    row0 = (N_PROMPT_TOK // seq_len) if latent else 0

    hw = HGRN_HEADS_PER_STEP * A_DK
    n_hg = A_HEADS // HGRN_HEADS_PER_STEP

    def zspec(part):
        return pl.BlockSpec((seq_len, hw), lambda s, h: (row0 + s, part * n_hg + h))

    in_specs = [zspec(0), zspec(1), zspec(2), zspec(3), zspec(4),
                pl.BlockSpec((2, 2, hw), lambda s, h: (0, 0, h)),
                pl.BlockSpec((1, hw), lambda s, h: (0, h))]
    args = [z, z, z, z, z, hgrn_lb, onorm_g.reshape(1, A_WIDTH)]
    state_spec = pl.BlockSpec((None, None, 2, HGRN_HEADS_PER_STEP, A_DK, A_DK), lambda s, h: (s, 0, 0, h, 0, 0))
    o_shape = jax.ShapeDtypeStruct((n_seq * seq_len, A_WIDTH), BF16)
    o_spec = pl.BlockSpec((seq_len, hw), lambda s, h: (s, h))
    if latent:
        in_specs.append(state_spec)
        args.append(state)
        out_shape, out_specs = o_shape, o_spec
    else:
        out_shape = (o_shape, jax.ShapeDtypeStruct((n_seq, 1, 2, A_HEADS, A_DK, A_DK), F32))
        out_specs = (o_spec, state_spec)
    return pl.pallas_call(
        functools.partial(_hgrn_kernel, seq_len=seq_len, with_state=latent),
        grid=(n_seq, n_hg),
        in_specs=in_specs,
        out_specs=out_specs,
        out_shape=out_shape,
        scratch_shapes=[pltpu.VMEM((seq_len, hw), F32), pltpu.VMEM((seq_len, hw), F32)],
        compiler_params=_params("arbitrary", "arbitrary"),
        name="hgrn_latent" if latent else "hgrn_prompt",
    )(*args)


def _rope_tables():
    pos = np.arange(LATENT_LEN)
    row, colp = pos // GRID_W, pos % GRID_W
    inv = ROPE_THETA ** (-np.arange(ROPE_PAIRS, dtype=np.float32) / ROPE_PAIRS)
    inv = inv.astype(np.float32)
    ang_r = (row.astype(np.float32)[:, None] * inv).astype(np.float32)
    ang_c = (colp.astype(np.float32)[:, None] * inv).astype(np.float32)
    cos = np.concatenate([np.cos(ang_r), np.cos(ang_r), np.cos(ang_c), np.cos(ang_c)], axis=1)
    sin = np.concatenate([-np.sin(ang_r), np.sin(ang_r), -np.sin(ang_c), np.sin(ang_c)], axis=1)
    return cos.astype(np.float32), sin.astype(np.float32)


def _head_mean_matrix(width):
    idx = np.arange(width) // HEAD_DIM
    return jnp.asarray((idx[:, None] == idx[None, :]).astype(np.float32) / HEAD_DIM).astype(BF16)


def _attn_kernel(*refs, latent):
    if latent:
        (q_ref, k_ref, v_ref, qg_ref, kg_ref, gq_ref, gk_ref, cosq_ref, sinq_ref, cosk_ref, sink_ref,
         ck_ref, cv_ref, o_ref) = refs
    else:
        (q_ref, k_ref, v_ref, qg_ref, kg_ref, gq_ref, gk_ref, o_ref, kout_ref, vout_ref) = refs
    pair_w = 2 * HEAD_DIM

    def head_norm(x, mean_ref, gain):
        sq = x * x
        hi = sq.astype(BF16)
        lo = (sq - hi.astype(F32)).astype(BF16)
        ms = jnp.dot(hi, mean_ref[...], preferred_element_type=F32)
        ms = ms + jnp.dot(lo, mean_ref[...], preferred_element_type=F32)
        return x * lax.rsqrt(ms + EPS) * gain

    def rope(x, cos, sin):
        n = x.shape[1]
        lane = lax.broadcasted_iota(jnp.int32, x.shape, 1)
        first_of_pair = (lane // ROPE_PAIRS) % 2 == 0
        swapped = jnp.where(first_of_pair, pltpu.roll(x, n - ROPE_PAIRS, axis=1), pltpu.roll(x, ROPE_PAIRS, axis=1))
        return x * cos + swapped * sin

    def attend(rows, seq_idx=None):
        q = head_norm(q_ref[rows, :].astype(F32), gq_ref, qg_ref[...])
        k = head_norm(k_ref[rows, :].astype(F32), gk_ref, kg_ref[...])
        if latent:
            q = rope(q, cosq_ref[...], sinq_ref[...])
            k = rope(k, cosk_ref[...], sink_ref[...])
        q = q * (HEAD_DIM ** -0.5)
        v = v_ref[rows, :].astype(F32)
        n_q = q.shape[0]
        low_kv = lax.broadcasted_iota(jnp.int32, k.shape, 1) < HEAD_DIM
        low_q = lax.broadcasted_iota(jnp.int32, (n_q, pair_w), 1) < HEAD_DIM
        k_swapped = pltpu.roll(k, HEAD_DIM, axis=1)
        v_swapped = pltpu.roll(v, HEAD_DIM, axis=1)
        if not latent:
            kout_ref[seq_idx] = k.T
            vout_ref[seq_idx] = v.T
        nt = (((1,), (1,)), ((), ()))
        for j in range(KV_HEADS):
            kd = (jnp.where(low_kv, k, k_swapped) if j == 0 else jnp.where(low_kv, k_swapped, k)).astype(BF16)
            vd = (jnp.where(low_kv, v, v_swapped) if j == 0 else jnp.where(low_kv, v_swapped, v)).astype(BF16)
            tiles = range(j * Q_PER_KV // 2, (j + 1) * Q_PER_KV // 2)
            parts = []
            for t in tiles:
                qt = q[:, t * pair_w:(t + 1) * pair_w]
                parts += [jnp.where(low_q, qt, 0.0), jnp.where(low_q, 0.0, qt)]
            qs = jnp.concatenate(parts, axis=0).astype(BF16)
            s_new = lax.dot_general(qs, kd, nt, preferred_element_type=F32)
            m = jnp.max(s_new, axis=-1, keepdims=True)
            if latent:
                ckd = jnp.concatenate([ck_ref[j], ck_ref[j]], axis=1).astype(BF16)
                cvd = jnp.concatenate([cv_ref[j], cv_ref[j]], axis=1).astype(BF16)
                s_old = lax.dot_general(qs, ckd, nt, preferred_element_type=F32)
                m = jnp.maximum(m, jnp.max(s_old, axis=-1, keepdims=True))
            p_new = jnp.exp(s_new - m)
            den = jnp.sum(p_new, axis=-1, keepdims=True)
            acc = jnp.dot(p_new.astype(BF16), vd, preferred_element_type=F32)
            if latent:
                p_old = jnp.exp(s_old - m)
                den = den + jnp.sum(p_old, axis=-1, keepdims=True)
                acc = acc + jnp.dot(p_old.astype(BF16), cvd, preferred_element_type=F32)
            out = acc / den
            for i, t in enumerate(tiles):
                lo_head = out[(2 * i) * n_q:(2 * i + 1) * n_q, :]
                hi_head = out[(2 * i + 1) * n_q:(2 * i + 2) * n_q, :]
                o_ref[rows, t * pair_w:(t + 1) * pair_w] = jnp.where(low_q, lo_head, hi_head).astype(o_ref.dtype)

    if latent:
        attend(slice(None))
    else:
        seq = PROMPT_LEN

        for s in range(q_ref.shape[0] // seq):
            attend(slice(s * seq, (s + 1) * seq), s)


def _attn_common_args(qn_g, kn_g):
    q_w, kv_w = Q_HEADS * HEAD_DIM, KV_HEADS * HEAD_DIM
    return (jnp.tile(qn_g, Q_HEADS).reshape(1, q_w), jnp.tile(kn_g, KV_HEADS).reshape(1, kv_w),
            _head_mean_matrix(q_w), _head_mean_matrix(kv_w))


def _attention_prompt(z, qn_g, kn_g):
    seqs = 4
    L = seqs * PROMPT_LEN
    cache_shape = jax.ShapeDtypeStruct((N_PROMPT_SEQ, KV_HEADS * HEAD_DIM, PROMPT_LEN), F32)
    cache_spec = pl.BlockSpec((seqs, KV_HEADS * HEAD_DIM, PROMPT_LEN), lambda s: (s, 0, 0))
    q_w, kv_w = Q_HEADS * HEAD_DIM, KV_HEADS * HEAD_DIM
    q_col = (5 * A_WIDTH) // q_w
    k_col = (5 * A_WIDTH + q_w) // kv_w
    const = lambda r, c: pl.BlockSpec((r, c), lambda s: (0, 0))
    return pl.pallas_call(
        functools.partial(_attn_kernel, latent=False),
        grid=(N_PROMPT_TOK // L,),
        in_specs=[
            pl.BlockSpec((L, q_w), lambda s: (s, q_col)),
            pl.BlockSpec((L, kv_w), lambda s: (s, k_col)),
            pl.BlockSpec((L, kv_w), lambda s: (s, k_col + 1)),
            const(1, q_w), const(1, kv_w), const(q_w, q_w), const(kv_w, kv_w),
        ],
        out_specs=(pl.BlockSpec((L, q_w), lambda s: (s, 0)), cache_spec, cache_spec),
        out_shape=(jax.ShapeDtypeStruct((N_PROMPT_TOK, q_w), BF16), cache_shape, cache_shape),
        compiler_params=_params("arbitrary"),
        name="attn_prompt",
    )(z, z, z, *_attn_common_args(qn_g, kn_g))


def _attention_latent(z, qn_g, kn_g, cache_k, cache_v):
    L = LATENT_LEN
    nqb = L // Q_BLOCK
    q_w, kv_w = Q_HEADS * HEAD_DIM, KV_HEADS * HEAD_DIM
    q_col = (5 * A_WIDTH) // q_w
    k_col = (5 * A_WIDTH + q_w) // kv_w
    qrow0 = N_PROMPT_TOK // Q_BLOCK
    krow0 = N_PROMPT_TOK // L
    cos, sin = _rope_tables()
    cos_q, sin_q = jnp.asarray(np.tile(cos, (1, Q_HEADS))), jnp.asarray(np.tile(sin, (1, Q_HEADS)))
    cos_k, sin_k = jnp.asarray(np.tile(cos, (1, KV_HEADS))), jnp.asarray(np.tile(sin, (1, KV_HEADS)))
    const = lambda r, c: pl.BlockSpec((r, c), lambda s, b: (0, 0))
    cache_spec = pl.BlockSpec((None, None, KV_HEADS, PAST_LEN, HEAD_DIM), lambda s, b: (s, 0, 0, 0, 0))
    return pl.pallas_call(
        functools.partial(_attn_kernel, latent=True),
        grid=(N_LATENT_SEQ, nqb),
        in_specs=[
            pl.BlockSpec((Q_BLOCK, q_w), lambda s, b: (qrow0 + s * nqb + b, q_col)),
            pl.BlockSpec((L, kv_w), lambda s, b: (krow0 + s, k_col)),
            pl.BlockSpec((L, kv_w), lambda s, b: (krow0 + s, k_col + 1)),
            const(1, q_w), const(1, kv_w), const(q_w, q_w), const(kv_w, kv_w),
            pl.BlockSpec((Q_BLOCK, q_w), lambda s, b: (b, 0)),
            pl.BlockSpec((Q_BLOCK, q_w), lambda s, b: (b, 0)),
            const(L, kv_w), const(L, kv_w),
            cache_spec, cache_spec,
        ],
        out_specs=pl.BlockSpec((Q_BLOCK, q_w), lambda s, b: (s * nqb + b, 0)),
        out_shape=jax.ShapeDtypeStruct((N_LATENT_TOK, q_w), BF16),
        compiler_params=_params("arbitrary", "arbitrary"),
        name="attn_latent",
    )(z, z, z, *_attn_common_args(qn_g, kn_g), cos_q, sin_q, cos_k, sin_k, cache_k, cache_v)


def _out_proj_kernel(*refs, n_act, n_x):
    a_refs = refs[:2 * n_act]
    x_refs = refs[2 * n_act:2 * n_act + n_x]
    g_ref, mod_ref, rw_ref, w_ref, xo_ref, h_ref, lg_ref, wb_ref, rws_ref, acc_ref = refs[2 * n_act + n_x:]
    _cast_once(w_ref, wb_ref)

    @pl.when(pl.program_id(0) == 0)
    def _():
        rw = rw_ref[...]
        hi = rw.astype(BF16).astype(F32)
        lo = (rw - hi).astype(BF16).astype(F32)
        rws_ref[...] = (hi + pltpu.roll(lo, N_EXPERTS, axis=1)).astype(BF16)

    mod = mod_ref[...]
    n = OUT_PROJ_SUB_ROWS

    n_sub = xo_ref.shape[0] // n

    def sub_rows(r):
        if isinstance(r, int):
            return slice(r * n, (r + 1) * n)
        return pl.ds(pl.multiple_of(r * n, n), n)

    def project(r):
        rows = sub_rows(r)
        acc = None
        k0 = 0
        for ap_ref, al_ref in zip(a_refs[0::2], a_refs[1::2]):
            k1 = k0 + ap_ref.shape[1]
            part = jnp.dot(_select_trunk(ap_ref, al_ref, rows), wb_ref[k0:k1, :], preferred_element_type=F32)
            acc = part if acc is None else acc + part
            k0 = k1
        acc_ref[r % 2] = acc

    def finish(r):
        rows = sub_rows(r)
        x_in = x_refs[0][rows, :] if n_x == 1 else _select_trunk(*x_refs, rows)
        x = x_in + mod[2:3, :] * acc_ref[r % 2]
        xo_ref[rows, :] = x
        h = _modulated_norm(x, g_ref[...], mod, 3, 4)
        h_ref[rows, :] = _pack_rows(h)
        h_hi = h.astype(BF16)
        h_lo = (h - h_hi.astype(F32)).astype(BF16)
        both = jnp.dot(jnp.concatenate([h_hi, h_lo], axis=0), rws_ref[...], preferred_element_type=F32)
        from_hi, from_lo = both[:n], both[n:]
        lg = from_hi + pltpu.roll(from_hi, ROUTER_LANES - N_EXPERTS, axis=1) + from_lo
        lg_ref[:, rows] = lg.T[:N_EXPERTS, :]

    project(0)
    for r in range(n_sub - 1):
        project(r + 1)
        finish(r)
    finish(n_sub - 1)


def _out_proj(acts, w, xs, g, mod_l, router_wp, block_rows=1024):
    tok = lambda width: pl.BlockSpec((block_rows, width), lambda i: (i, 0))
    in_specs = [spec for ap, _ in acts for spec in _trunk_specs(block_rows, ap.shape[1])]
    in_specs += [tok(D_MODEL)] if len(xs) == 1 else list(_trunk_specs(block_rows, D_MODEL))
    in_specs += [_resident((1, D_MODEL)), _mod_spec(block_rows), _resident((D_MODEL, ROUTER_LANES)),
                 _resident(w.shape)]
    return pl.pallas_call(
        functools.partial(_out_proj_kernel, n_act=len(acts), n_x=len(xs)),
        grid=(N_TOK // block_rows,),
        in_specs=in_specs,
        out_specs=(tok(D_MODEL), tok(ROW_WORDS), pl.BlockSpec((N_EXPERTS, block_rows), lambda i: (0, i))),
        out_shape=(jax.ShapeDtypeStruct((N_TOK, D_MODEL), F32),
                   jax.ShapeDtypeStruct((N_TOK, ROW_WORDS), jnp.int32),
                   jax.ShapeDtypeStruct((N_EXPERTS, N_TOK), F32)),
        scratch_shapes=[pltpu.VMEM(w.shape, BF16), pltpu.VMEM((D_MODEL, ROUTER_LANES), BF16),
                        pltpu.VMEM((2, OUT_PROJ_SUB_ROWS, D_MODEL), F32)],
        compiler_params=_params("arbitrary"),
        name="out_proj",
    )(*[a for pair in acts for a in pair], *xs, g.reshape(1, D_MODEL), mod_l, router_wp, w)


def _router_kernel(lg_ref, rb_ref, pos_ref, w_ref, plan_ref, rank_ref):
    lg = lg_ref[...]
    ex = jnp.exp(lg - jnp.max(lg, axis=0, keepdims=True))
    scores = ex / jnp.sum(ex, axis=0, keepdims=True)
    biased = scores + rb_ref[...]
    rows = [biased[e:e + 1, :] for e in range(N_EXPERTS)]
    selected = []
    group_score = []
    for gi in range(N_GROUPS):
        r = rows[gi * EXPERTS_PER_GROUP:(gi + 1) * EXPERTS_PER_GROUP]
        total = None
        for i in range(EXPERTS_PER_GROUP):
            rank = None
            for j in range(EXPERTS_PER_GROUP):
                if j == i:
                    continue
                ahead = (r[j] > r[i]) if j > i else (r[j] >= r[i])
                ahead = jnp.where(ahead, 1.0, 0.0)
                rank = ahead if rank is None else rank + ahead
            sel = rank < 1.5
            selected.append(sel)
            contrib = jnp.where(sel, r[i], 0.0)
            total = contrib if total is None else total + contrib
        group_score.append(total)
    best = group_score[0]
    best_group = jnp.zeros_like(best)
    for gi in range(1, N_GROUPS):
        better = group_score[gi] > best
        best_group = jnp.where(better, float(gi), best_group)
        best = jnp.where(better, group_score[gi], best)
    picked = []
    chosen = []
    den = None
    for e in range(N_EXPERTS):
        in_group = best_group == float(e // EXPERTS_PER_GROUP)
        use = jnp.where(selected[e], jnp.where(in_group, 1.0, 0.0), 0.0)
        w = use * scores[e:e + 1, :]
        chosen.append(use)
        picked.append(w)
        den = w if den is None else den + w
    lanes = 128
    n_blk = N_TOK // lanes
    li = lax.broadcasted_iota(jnp.int32, (lanes, lanes), 0)
    lj = lax.broadcasted_iota(jnp.int32, (lanes, lanes), 1)
    prefix = jnp.where(li <= lj, 1.0, 0.0).astype(BF16)
    carry = jnp.zeros((N_EXPERTS, 1), F32)
    for blk in range(n_blk):
        cols = slice(blk * lanes, (blk + 1) * lanes)
        m = jnp.concatenate([chosen[e][:, cols] for e in range(N_EXPERTS)], axis=0)
        incl = jnp.dot(m.astype(BF16), prefix, preferred_element_type=F32)
        rank_ref[:, cols] = incl - m + carry
        carry = carry + incl[:, lanes - 1:lanes]
    count = carry
    padded = jnp.floor((count + float(MOE_TILE - 1)) * (1.0 / MOE_TILE)) * float(MOE_TILE)
    erow = lax.broadcasted_iota(jnp.int32, (N_EXPERTS, 1), 0)
    offset = jnp.zeros((N_EXPERTS, 1), F32)
    for e in range(N_EXPERTS - 1):
        offset = offset + jnp.where(erow > e, padded[e:e + 1, :], 0.0)
    seen = jnp.zeros_like(den)
    pos_a = jnp.zeros_like(den)
    pos_b = jnp.zeros_like(den)
    w_a = jnp.zeros_like(den)
    w_b = jnp.zeros_like(den)
    for e in range(N_EXPERTS):
        pos_e = rank_ref[e:e + 1, :] + offset[e:e + 1, :]
        gate_e = picked[e] / den
        first = jnp.where(seen < 0.5, chosen[e], 0.0) > 0.5
        second = jnp.where(seen > 0.5, chosen[e], 0.0) > 0.5
        pos_a = jnp.where(first, pos_e, pos_a)
        w_a = jnp.where(first, gate_e, w_a)
        pos_b = jnp.where(second, pos_e, pos_b)
        w_b = jnp.where(second, gate_e, w_b)
        seen = seen + chosen[e]
    pos_ref[0:1, :] = pos_a.astype(jnp.int32)
    pos_ref[1:2, :] = pos_b.astype(jnp.int32)
    w_rows = jnp.concatenate([w_a, w_b, jnp.zeros((6, N_TOK), F32)], axis=0)
    ei = lax.broadcasted_iota(jnp.int32, (8, lanes), 0)
    ej = lax.broadcasted_iota(jnp.int32, (8, lanes), 1)
    eye = jnp.where(ei == ej, 1.0, 0.0).astype(BF16)
    tn = (((0,), (0,)), ((), ()))
    hi = w_rows.astype(BF16)
    r1 = w_rows - hi.astype(F32)
    mid = r1.astype(BF16)
    lo = (r1 - mid.astype(F32)).astype(BF16)
    w_cols = lax.dot_general(hi, eye, tn, preferred_element_type=F32)
    w_cols = w_cols + lax.dot_general(mid, eye, tn, preferred_element_type=F32)
    w_cols = w_cols + lax.dot_general(lo, eye, tn, preferred_element_type=F32)
    w_ref[...] = w_cols[:, :TOP_K]
    start = (lax.broadcasted_iota(jnp.int32, (N_EXPERTS, lanes), 1) * MOE_TILE).astype(F32)
    end = offset + padded
    tile_expert = jnp.sum(jnp.where(end <= start, 1.0, 0.0), axis=0, keepdims=True)
    inside = (offset <= start) & (start < end)
    real = jnp.clip(count - (start - offset), 0.0, float(MOE_TILE))
    tile_rows = jnp.sum(jnp.where(inside, real, 0.0), axis=0, keepdims=True)
    plan_ref[0:1, :] = jnp.minimum(tile_expert, float(N_EXPERTS - 1)).astype(jnp.int32)
    plan_ref[1:2, :] = tile_rows.astype(jnp.int32)


def _router(logits_t, router_b):
    whole = lambda shape: pl.BlockSpec(shape, lambda i: (0, 0))
    return pl.pallas_call(
        _router_kernel,
        grid=(1,),
        in_specs=[whole((N_EXPERTS, N_TOK)), whole((N_EXPERTS, 1))],
        out_specs=(whole((2, N_TOK)), whole((N_TOK, TOP_K)), whole((2, 128))),
        out_shape=(jax.ShapeDtypeStruct((2, N_TOK), jnp.int32),
                   jax.ShapeDtypeStruct((N_TOK, TOP_K), F32),
                   jax.ShapeDtypeStruct((2, 128), jnp.int32)),
        scratch_shapes=[pltpu.VMEM((N_EXPERTS, N_TOK), F32)],
        compiler_params=_params("arbitrary"),
        name="router",
    )(logits_t, router_b.reshape(N_EXPERTS, 1))


def _sc_mesh():
    return plsc.VectorSubcoreMesh(core_axis_name="c", subcore_axis_name="s")


def _sc_worker_base():
    return (lax.axis_index("s") * SC_CORES + lax.axis_index("c")) * SC_TOKENS_PER_WORKER


def _moe_dispatch(h, pos_a, pos_b):
    n_chunks = SC_TOKENS_PER_WORKER // SC_CHUNK
    idx = pltpu.VMEM((SC_CHUNK,), jnp.int32)

    @functools.partial(
        pl.kernel, mesh=_sc_mesh(),
        out_type=jax.ShapeDtypeStruct((MOE_ROWS, ROW_WORDS), jnp.int32),
        scratch_types=[idx, idx, idx, idx, pltpu.VMEM((2, SC_CHUNK, ROW_WORDS), jnp.int32),
                       pltpu.SemaphoreType.DMA((6,)), pltpu.SemaphoreType.DMA((4,))],
        name="moe_dispatch",
    )
    def run(h_hbm, pa_hbm, pb_hbm, xs_hbm, ia0, ib0, ia1, ib1, rows_v, sem_in, sem_out):
        base = _sc_worker_base()
        ia, ib = (ia0, ia1), (ib0, ib1)

        def start_loads(c):
            slot = c % 2
            tok = pl.ds(pl.multiple_of(base + c * SC_CHUNK, 8), SC_CHUNK)
            return (pltpu.async_copy(pa_hbm.at[tok], ia[slot], sem_in.at[3 * slot]),
                    pltpu.async_copy(pb_hbm.at[tok], ib[slot], sem_in.at[3 * slot + 1]),
                    pltpu.async_copy(h_hbm.at[tok], rows_v.at[slot], sem_in.at[3 * slot + 2]))

        loads = start_loads(0)
        scatters = [(), ()]
        for c in range(n_chunks):
            slot = c % 2
            for cp in loads:
                cp.wait()
            if c + 1 < n_chunks:
                for cp in scatters[1 - slot]:
                    cp.wait()
                scatters[1 - slot] = ()
                loads = start_loads(c + 1)
            scatters[slot] = (pltpu.async_copy(rows_v.at[slot], xs_hbm.at[ia[slot]], sem_out.at[2 * slot]),
                              pltpu.async_copy(rows_v.at[slot], xs_hbm.at[ib[slot]], sem_out.at[2 * slot + 1]))
        for pending in scatters:
            for cp in pending:
                cp.wait()

    return run(h, pos_a, pos_b)


def _moe_collect(ys, pos_a, pos_b):
    n_chunks = SC_TOKENS_PER_WORKER // SC_CHUNK
    out = jax.ShapeDtypeStruct((N_TOK, ROW_WORDS), jnp.int32)
    idx = pltpu.VMEM((SC_TOKENS_PER_WORKER,), jnp.int32)
    rows = pltpu.VMEM((2, SC_CHUNK, ROW_WORDS), jnp.int32)

    @functools.partial(
        pl.kernel, mesh=_sc_mesh(), out_type=(out, out),
        scratch_types=[idx, idx, rows, rows, pltpu.SemaphoreType.DMA((4,)), pltpu.SemaphoreType.DMA((4,))],
        name="moe_collect",
    )
    def run(ys_hbm, pa_hbm, pb_hbm, ya_hbm, yb_hbm, ia_v, ib_v, ra_v, rb_v, sem_g, sem_w):
        base = _sc_worker_base()
        mine = pl.ds(pl.multiple_of(base, 8), SC_TOKENS_PER_WORKER)
        pltpu.sync_copy(pa_hbm.at[mine], ia_v)
        pltpu.sync_copy(pb_hbm.at[mine], ib_v)
        writes = [(), ()]
        for c in range(n_chunks):
            slot = c % 2
            for cp in writes[slot]:
                cp.wait()
            part = pl.ds(c * SC_CHUNK, SC_CHUNK)
            tok = pl.ds(pl.multiple_of(base + c * SC_CHUNK, 8), SC_CHUNK)
            ga = pltpu.async_copy(ys_hbm.at[ia_v.at[part]], ra_v.at[slot], sem_g.at[slot])
            gb = pltpu.async_copy(ys_hbm.at[ib_v.at[part]], rb_v.at[slot], sem_g.at[2 + slot])
            ga.wait()
            wa = pltpu.async_copy(ra_v.at[slot], ya_hbm.at[tok], sem_w.at[slot])
            gb.wait()
            wb = pltpu.async_copy(rb_v.at[slot], yb_hbm.at[tok], sem_w.at[2 + slot])
            writes[slot] = (wa, wb)
        for pending in writes:
            for cp in pending:
                cp.wait()

    return run(ys, pos_a, pos_b)


def _experts_kernel(plan_ref, xs_ref, wg_hbm, wu_hbm, wd_hbm, y_ref,
                    sg_ref, su_ref, sd_ref, wgb_ref, wub_ref, wdb_ref, hid_ref, sems, seg_ref, *, layer):
    j = pl.program_id(0)
    n_tiles = pl.num_programs(0)
    expert = plan_ref[j]
    n_real = plan_ref[PLAN_LANES + j]
    fresh = jnp.logical_or(j == 0, expert != plan_ref[jnp.maximum(j - 1, 0)])

    def weight_copies(e, slot):
        return (pltpu.make_async_copy(wg_hbm.at[layer, e], sg_ref.at[slot], sems.at[slot, 0]),
                pltpu.make_async_copy(wu_hbm.at[layer, e], su_ref.at[slot], sems.at[slot, 1]),
                pltpu.make_async_copy(wd_hbm.at[layer, e], sd_ref.at[slot], sems.at[slot, 2]))

    @pl.when(j == 0)
    def _():
        seg_ref[0] = 0

        @pl.when(n_real > 0)
        def _():
            for cp in weight_copies(expert, 0):
                cp.start()

    @pl.when(jnp.logical_and(n_real > 0, fresh))
    def _():
        slot = seg_ref[0] % 2
        for cp in weight_copies(expert, slot):
            cp.wait()
        wgb_ref[...] = sg_ref[slot].astype(BF16)
        wub_ref[...] = su_ref[slot].astype(BF16)
        wdb_ref[...] = sd_ref[slot].astype(BF16)
        nxt = lax.while_loop(lambda t: jnp.logical_and(t < n_tiles, plan_ref[jnp.minimum(t, n_tiles - 1)] == expert),
                             lambda t: t + 1, j + 1)
        nxt_c = jnp.minimum(nxt, n_tiles - 1)

        @pl.when(jnp.logical_and(nxt < n_tiles, plan_ref[PLAN_LANES + nxt_c] > 0))
        def _():
            for cp in weight_copies(plan_ref[nxt_c], 1 - slot):
                cp.start()

        seg_ref[0] = seg_ref[0] + 1

    @pl.when(n_real > 0)
    def _():
        n = EXPERT_SUB_ROWS
        n_sub = xs_ref.shape[0] // n
        row = lax.broadcasted_iota(jnp.int32, (n, xs_ref.shape[1]), 0)

        def up(r):
            rows = slice(r * n, (r + 1) * n)
            words = jnp.where(row < n_real - r * n, xs_ref[rows, :], 0)
            x = _unpack_rows(words).astype(BF16)
            a = jnp.dot(x, wgb_ref[...], preferred_element_type=F32)
            b = jnp.dot(x, wub_ref[...], preferred_element_type=F32)
            hid_ref[r] = ((a * jax.nn.sigmoid(a)) * b).astype(BF16)

        def down(r):
            rows = slice(r * n, (r + 1) * n)
            y_ref[rows, :] = _pack_rows(jnp.dot(hid_ref[r], wdb_ref[...], preferred_element_type=F32))

        up(0)
        for r in range(1, n_sub):
            up(r)
            down(r - 1)
        down(n_sub - 1)


def _experts(plan, xs, w_gate, w_up, w_down, layer):
    hbm = pl.BlockSpec(memory_space=pl.ANY)
    return pl.pallas_call(
        functools.partial(_experts_kernel, layer=layer),
        grid_spec=pltpu.PrefetchScalarGridSpec(
            num_scalar_prefetch=1,
            grid=(MOE_ROWS // MOE_TILE,),
            in_specs=[pl.BlockSpec((MOE_TILE, ROW_WORDS), lambda j, plan: (j, 0)), hbm, hbm, hbm],
            out_specs=pl.BlockSpec((MOE_TILE, ROW_WORDS), lambda j, plan: (j, 0)),
            scratch_shapes=[pltpu.VMEM((2, D_MODEL, D_EXPERT), F32), pltpu.VMEM((2, D_MODEL, D_EXPERT), F32),
                            pltpu.VMEM((2, D_EXPERT, D_MODEL), F32),
                            pltpu.VMEM((D_MODEL, D_EXPERT), BF16), pltpu.VMEM((D_MODEL, D_EXPERT), BF16),
                            pltpu.VMEM((D_EXPERT, D_MODEL), BF16),
                            pltpu.VMEM((MOE_TILE // EXPERT_SUB_ROWS, EXPERT_SUB_ROWS, D_EXPERT), BF16),
                            pltpu.SemaphoreType.DMA((2, 3)), pltpu.SMEM((1,), jnp.int32)],
        ),
        out_shape=jax.ShapeDtypeStruct((MOE_ROWS, ROW_WORDS), jnp.int32),
        compiler_params=_params("arbitrary"),
        name="experts",
    )(plan, xs, w_gate, w_up, w_down)


def _combine_kernel(x_ref, ya_ref, yb_ref, wt_ref, mod_ref, o_ref):
    o_ref[...] = _moe_mix(x_ref, ya_ref, yb_ref, wt_ref, mod_ref)


def _combine(x, moe_out, mod_l, tok0, n_tok, block_rows=512):
    ya, yb, w_tok = moe_out
    b0 = tok0 // block_rows
    rows = lambda width: pl.BlockSpec((block_rows, width), lambda i: (b0 + i, 0))
    return pl.pallas_call(
        _combine_kernel,
        grid=(n_tok // block_rows,),
        in_specs=[rows(D_MODEL), rows(ROW_WORDS), rows(ROW_WORDS), rows(TOP_K),
                  pl.BlockSpec((None, 6, D_MODEL), lambda i: (_cond_of_token_block(b0 + i, block_rows), 0, 0))],
        out_specs=pl.BlockSpec((block_rows, D_MODEL), lambda i: (i, 0)),
        out_shape=jax.ShapeDtypeStruct((n_tok, D_MODEL), F32),
        compiler_params=_params("arbitrary"),
        name="combine",
    )(x, ya, yb, w_tok, mod_l)


def _moe(h, logits_t, router_b, w_gate, w_up, w_down, layer):
    pos, w, plan = _router(logits_t, router_b)
    xs = _moe_dispatch(h, pos[0], pos[1])
    ys = _experts(plan.reshape(-1), xs, w_gate, w_up, w_down, layer)
    ya, yb = _moe_collect(ys, pos[0], pos[1])
    return ya, yb, w


def _dft_tables(L):
    k = np.arange(L)[:, None]
    m = np.arange(L)[None, :]
    r = (k * m) % (2 * L)
    ang = np.pi * r.astype(np.float64) / L
    fc = np.cos(ang)
    fs = np.sin(ang)
    fs[0, :] = np.where(np.arange(L) % 2 == 0, 1.0, -1.0)
    wk = np.full((L, 1), 1.0 / L)
    wk[0, 0] = 0.5 / L
    gc = (fc * wk).T
    gs = (fs * wk).T
    return [jnp.asarray(t.astype(np.float32)).astype(BF16) for t in (fc, fs, gc, gs)]


def _filter_consts(L):
    t = np.linspace(0.0, 1.0, L, dtype=np.float32)[:, None]
    w = (np.float32(2.0 * np.pi) * np.arange(L, dtype=np.float32)[:, None] / np.float32(L)).astype(np.float32)
    fb = np.linspace(1e-4, HY_BANDS - 1, HY_BANDS, dtype=np.float32)[None, :]
    emb = np.concatenate([t, np.cos(fb * w), -np.sin(fb * w)], axis=-1).astype(np.float32)
    lo = math.log(HY_DECAY_TARGET) / HY_SLOW_PCT
    hi = math.log(HY_DECAY_TARGET) / HY_FAST_PCT
    deltas = np.abs(np.linspace(lo, hi, D_MODEL, dtype=np.float32))
    decay = np.exp(-t * deltas).astype(np.float32)
    return jnp.asarray(emb), jnp.asarray(decay)


def _filter_kernel(emb_ref, w1_ref, b1_ref, w2_ref, b2_ref, fr_ref, w3f_ref, w3b_ref, dec_ref,
                   fc_ref, fs_ref, kr_ref, q_ref, krn_ref, hd_ref):
    @pl.when(pl.program_id(0) == 0)
    def _():
        fr = fr_ref[...]
        h1 = jnp.sin(fr * (jnp.dot(emb_ref[...], w1_ref[...], precision=HIGHEST,
                                   preferred_element_type=F32) + b1_ref[...]))
        hd_ref[...] = jnp.sin(fr * (jnp.dot(h1, w2_ref[...], precision=HIGHEST,
                                            preferred_element_type=F32) + b2_ref[...]))

    hd = hd_ref[...]
    dec = dec_ref[...]
    f = jnp.dot(hd, w3f_ref[...], precision=HIGHEST, preferred_element_type=F32) * dec
    g = jnp.dot(hd, w3b_ref[...], precision=HIGHEST, preferred_element_type=F32) * dec
    row = lax.broadcasted_iota(jnp.int32, f.shape, 0)
    g = jnp.where(row == 0, 0.0, g)
    s = f + g
    d = f - g
    kr = jnp.dot(fc_ref[...], s.astype(BF16), preferred_element_type=F32)
    qq = jnp.dot(fs_ref[...], d.astype(BF16), preferred_element_type=F32)
    alt = jnp.where(row % 2 == 0, 1.0, -1.0)
    nyq = jnp.sum(alt * s, axis=0, keepdims=True)
    kr_ref[...] = kr
    q_ref[...] = jnp.where(row == 0, 0.0, qq)
    krn_ref[...] = jnp.where(row == 0, nyq, kr)


def _hyena_filter_spectrum(L, w1, b1, w2, b2, w3, freq, fc, fs, cblk=256):
    emb, decay = _filter_consts(L)
    ncb = D_MODEL // cblk
    n_emb = 128
    emb = jnp.pad(emb, ((0, 0), (0, n_emb - emb.shape[1])))
    w1 = jnp.pad(w1, ((0, n_emb - w1.shape[0]), (0, 0)))
    full = lambda shape: pl.BlockSpec(shape, lambda j: tuple(0 for _ in shape))
    out_sds = jax.ShapeDtypeStruct((L, D_MODEL), F32)
    out_spec = pl.BlockSpec((L, cblk), lambda j: (0, j))
    return pl.pallas_call(
        _filter_kernel,
        grid=(ncb,),
        in_specs=[
            full((L, n_emb)), full((n_emb, HY_FFN)), full((1, HY_FFN)), full((HY_FFN, HY_FFN)),
            full((1, HY_FFN)), full((1, HY_FFN)),
            pl.BlockSpec((HY_FFN, cblk), lambda j: (0, j)),
            pl.BlockSpec((HY_FFN, cblk), lambda j: (0, ncb + j)),
            pl.BlockSpec((L, cblk), lambda j: (0, j)),
            full((L, L)), full((L, L)),
        ],
        out_specs=(out_spec, out_spec, out_spec),
        out_shape=(out_sds, out_sds, out_sds),
        scratch_shapes=[pltpu.VMEM((L, HY_FFN), F32)],
        compiler_params=_params("arbitrary"),
        name=f"hyena_filter_{L}",
    )(emb, w1, b1.reshape(1, HY_FFN), w2, b2.reshape(1, HY_FFN), freq.reshape(1, HY_FFN), w3, w3, decay, fc, fs)


def _hyena_conv_kernel(x0_ref, x1_ref, v_ref, cw0_ref, cw1_ref, cwv_ref, cb0_ref, cb1_ref, cbv_ref,
                       kr_ref, q_ref, krn_ref, ds_ref, fc_ref, fs_ref, gc_ref, gs_ref, o_ref,
                       zz_ref, gate_ref, skip_ref, yr_ref, yw_ref):
    L = fc_ref.shape[0]
    n_seq = x0_ref.shape[0] // L
    row = lax.broadcasted_iota(jnp.int32, (L, x0_ref.shape[1]), 0)

    def gating(s):
        rows = slice(s * L, (s + 1) * L)

        def short_conv(u_ref, w_ref, b_ref):
            u = u_ref[rows, :].astype(F32)
            w = w_ref[...]
            prev = jnp.where(row == 0, 0.0, pltpu.roll(u, 1, axis=0))
            nxt = jnp.where(row == L - 1, 0.0, pltpu.roll(u, L - 1, axis=0))
            return prev * w[0:1, :] + u * w[1:2, :] + nxt * w[2:3, :] + b_ref[...]

        x0 = short_conv(x0_ref, cw0_ref, cb0_ref)
        zz = short_conv(v_ref, cwv_ref, cbv_ref) * short_conv(x1_ref, cw1_ref, cb1_ref)
        zz_ref[s] = zz.astype(BF16)
        gate_ref[s] = x0
        skip_ref[s] = x0 * zz * ds_ref[...]

    def spectrum(s):
        ur = jnp.dot(fc_ref[...], zz_ref[s], preferred_element_type=F32)
        p = jnp.dot(fs_ref[...], zz_ref[s], preferred_element_type=F32)
        qq = q_ref[...]
        yr_ref[s] = (ur * kr_ref[...] - p * qq).astype(BF16)
        yw_ref[s] = (ur * qq + p * krn_ref[...]).astype(BF16)

    def synthesis(s):
        y = jnp.dot(gc_ref[...], yr_ref[s], preferred_element_type=F32)
        y = y + jnp.dot(gs_ref[...], yw_ref[s], preferred_element_type=F32)
        o_ref[s * L:(s + 1) * L, :] = (gate_ref[s] * y + skip_ref[s]).astype(o_ref.dtype)

    for t in range(n_seq + 2):
        if t < n_seq:
            gating(t)
        if 0 <= t - 1 < n_seq:
            spectrum(t - 1)
        if 0 <= t - 2 < n_seq:
            synthesis(t - 2)


def _hyena_conv(u, conv_w, conv_b, dskip, spectrum, tables, *, latent):
    L = LATENT_LEN if latent else PROMPT_LEN
    n_seq = N_LATENT_SEQ if latent else N_PROMPT_SEQ
    cblk = 256 if latent else 512
    ncb = D_MODEL // cblk
    seqs = 1 if latent else 8
    row0 = (N_PROMPT_TOK // L) if latent else 0
    kr, qq, krn = spectrum
    fc, fs, gc, gs = tables

    def part(p, rows):
        if rows != L:
            return pl.BlockSpec((rows, cblk), lambda j, s: (0, p * ncb + j))
        return pl.BlockSpec((seqs * L, cblk), lambda j, s: (row0 // seqs + s, p * ncb + j))

    def const_cols(rows):
        return pl.BlockSpec((rows, cblk), lambda j, s: (0, j))

    mat = pl.BlockSpec((L, L), lambda j, s: (0, 0))
    conv_b2 = conv_b.reshape(1, 3 * D_MODEL)
    in_specs = [part(0, L), part(1, L), part(2, L),
                part(0, 3), part(1, 3), part(2, 3),
                part(0, 1), part(1, 1), part(2, 1),
                const_cols(L), const_cols(L), const_cols(L), const_cols(1),
                mat, mat, mat, mat]
    args = [u, u, u, conv_w, conv_w, conv_w, conv_b2, conv_b2, conv_b2,
            kr, qq, krn, dskip.reshape(1, D_MODEL), fc, fs, gc, gs]
    return pl.pallas_call(
        _hyena_conv_kernel,
        grid=(ncb, n_seq // seqs),
        in_specs=in_specs,
        out_specs=pl.BlockSpec((seqs * L, cblk), lambda j, s: (s, j)),
        out_shape=jax.ShapeDtypeStruct((n_seq * L, D_MODEL), BF16),
        scratch_shapes=[pltpu.VMEM((seqs, L, cblk), BF16), pltpu.VMEM((seqs, L, cblk), F32),
                        pltpu.VMEM((seqs, L, cblk), F32), pltpu.VMEM((seqs, L, cblk), BF16),
                        pltpu.VMEM((seqs, L, cblk), BF16)],
        compiler_params=_params("arbitrary", "arbitrary"),
        name="hyena_conv_latent" if latent else "hyena_conv_prompt",
    )(*args)


def kernel(x_prompt, x_sample, cache_k, cache_v, state_hgrn, c, c_ctx, norm_g, mod_w, mod_b, ab_in_w, hgrn_lb, hgrn_onorm_g, attn_qnorm_g, attn_knorm_g, ab_out_w, hy_in_w, hy_in_b, hy_conv_w, hy_conv_b, hy_f_w1, hy_f_b1, hy_f_w2, hy_f_b2, hy_f_w3, hy_f_freq, hy_dskip, hy_out_w, router_w, router_b, moe_w_gate, moe_w_up, moe_w_down):
    xp = x_prompt.reshape(N_PROMPT_TOK, D_MODEL)
    xl = x_sample.reshape(N_LATENT_TOK, D_MODEL)
    cond = jnp.concatenate([c_ctx[None, :], c, jnp.zeros((N_COND - 1 - N_LATENT_SEQ, D_MODEL), F32)], axis=0)
    mod = _modulation(cond, mod_w, mod_b)
    router_wp = jnp.pad(router_w, ((0, 0), (0, ROUTER_LANES - N_EXPERTS)))

    z = _in_proj0(xp, xl, norm_g[0, 0], mod[0], ab_in_w[0])
    oa_p, new_state = _hgrn(z, hgrn_lb, hgrn_onorm_g[0], None, latent=False)
    oa_l = _hgrn(z, hgrn_lb, hgrn_onorm_g[0], state_hgrn, latent=True)
    ob_p, k_fm, v_fm = _attention_prompt(z, attn_qnorm_g[0], attn_knorm_g[0])
    fm_shape = (N_PROMPT_SEQ, 1, KV_HEADS, HEAD_DIM, PROMPT_LEN)
    new_k = jnp.swapaxes(k_fm.reshape(fm_shape), -1, -2)
    new_v = jnp.swapaxes(v_fm.reshape(fm_shape), -1, -2)
    ob_l = _attention_latent(z, attn_qnorm_g[0], attn_knorm_g[0], cache_k, cache_v)
    x, h, logits_t = _out_proj([(oa_p, oa_l), (ob_p, ob_l)], ab_out_w[0], (xp, xl), norm_g[0, 1], mod[0],
                               router_wp)
    moe_out = _moe(h, logits_t, router_b, moe_w_gate, moe_w_up, moe_w_down, 0)

    x, u = _in_proj1(x, moe_out, mod[0], norm_g[1, 0], mod[1], hy_in_w[0], hy_in_b[0])
    pre = []
    for latent in (False, True):
        L = LATENT_LEN if latent else PROMPT_LEN
        tables = _dft_tables(L)
        spectrum = _hyena_filter_spectrum(L, hy_f_w1[0], hy_f_b1[0], hy_f_w2[0], hy_f_b2[0], hy_f_w3[0],
                                          hy_f_freq[0], tables[0], tables[1])
        pre.append(_hyena_conv(u, hy_conv_w[0], hy_conv_b[0], hy_dskip[0], spectrum, tables, latent=latent))
    x, h, logits_t = _out_proj([tuple(pre)], hy_out_w[0], (x,), norm_g[1, 1], mod[1], router_wp)
    moe_out = _moe(h, logits_t, router_b, moe_w_gate, moe_w_up, moe_w_down, 1)

    y_prompt = _combine(x, moe_out, mod[1], 0, N_PROMPT_TOK).reshape(N_PROMPT_SEQ, PROMPT_LEN, D_MODEL)
    y_sample = _combine(x, moe_out, mod[1], N_PROMPT_TOK, N_LATENT_TOK).reshape(N_LATENT_SEQ, LATENT_LEN, D_MODEL)
    return (y_prompt, y_sample, new_k, new_v, new_state)
```

```python
import functools
import math

import numpy as np
import jax
import jax.numpy as jnp
from jax import lax
from jax.experimental import pallas as pl
from jax.experimental.pallas import tpu as pltpu
from jax.experimental.pallas import tpu_sc as plsc

F32 = jnp.float32
BF16 = jnp.bfloat16
HIGHEST = lax.Precision.HIGHEST

D_MODEL = 1024
N_PROMPT_SEQ = 32
PROMPT_LEN = 256
N_LATENT_SEQ = 2
LATENT_LEN = 1024
PAST_LEN = 512
GRID_W = 64
N_PROMPT_TOK = N_PROMPT_SEQ * PROMPT_LEN
N_LATENT_TOK = N_LATENT_SEQ * LATENT_LEN
N_TOK = N_PROMPT_TOK + N_LATENT_TOK
N_COND = 8
EPS = 1e-6

A_WIDTH = 512
A_HEADS = 4
A_DK = 128
CHUNK = 64
HGRN_BLOCK = 128
HGRN_HEADS_PER_STEP = 4
HEAD_DIM = 64
Q_HEADS = 8
KV_HEADS = 2
Q_PER_KV = Q_HEADS // KV_HEADS
Q_BLOCK = 256
ROPE_THETA = 10000.0
ROPE_PAIRS = HEAD_DIM // 4
AB_IN = 5 * A_WIDTH + (Q_HEADS + 2 * KV_HEADS) * HEAD_DIM

HY_BANDS = 16
HY_FFN = 64
HY_DECAY_TARGET = 1e-2
HY_FAST_PCT = 0.3
HY_SLOW_PCT = 1.5

N_EXPERTS = 16
N_GROUPS = 4
EXPERTS_PER_GROUP = 4
TOP_K = 2
D_EXPERT = 512
ROUTER_LANES = 128
OUT_PROJ_SUB_ROWS = 256
EXPERT_SUB_ROWS = 256
IN_PROJ_SUB_ROWS = 256
MOE_TILE = 512
MOE_ROWS = N_TOK * TOP_K + N_EXPERTS * MOE_TILE
PLAN_LANES = 128

SC_CORES = 2
SC_WORKERS = 32
SC_TOKENS_PER_WORKER = N_TOK // SC_WORKERS
SC_CHUNK = 40
ROW_WORDS = D_MODEL // 2

VMEM_LIMIT = 56 * 1024 * 1024


def _params(*sem):
    return pltpu.CompilerParams(dimension_semantics=sem, vmem_limit_bytes=VMEM_LIMIT)


def _pack_rows(x):
    n = x.shape[1] // 2
    bits = pltpu.bitcast(x.astype(BF16).astype(F32), jnp.uint32)
    return pltpu.bitcast(bits[:, :n] | (bits[:, n:] >> 16), jnp.int32)


def _unpack_rows(p):
    bits = pltpu.bitcast(p, jnp.uint32)
    hi = pltpu.bitcast(bits & jnp.uint32(0xFFFF0000), F32)
    lo = pltpu.bitcast(bits << 16, F32)
    return jnp.concatenate([hi, lo], axis=1)


def _cond_of_token_block(i, block_rows):
    start = i * block_rows
    return jnp.where(start < N_PROMPT_TOK, 0, 1 + (start - N_PROMPT_TOK) // LATENT_LEN)


def _mod_kernel(cond_ref, w_ref, b_ref, o_ref):
    cnd = cond_ref[...]
    s = cnd * jax.nn.sigmoid(cnd)
    s_hi = s.astype(BF16)
    s_lo = (s - s_hi.astype(F32)).astype(BF16)
    w = w_ref[...]
    w_hi = w.astype(BF16)
    w_lo = (w - w_hi.astype(F32)).astype(BF16)
    acc = jnp.dot(s_hi, w_hi, preferred_element_type=F32)
    acc = acc + jnp.dot(s_lo, w_hi, preferred_element_type=F32)
    acc = acc + jnp.dot(s_hi, w_lo, preferred_element_type=F32)
    o_ref[...] = acc + b_ref[...]


def _modulation(cond, mod_w, mod_b):
    depth = mod_w.shape[0]
    n_chunk = 6
    out = pl.pallas_call(
        _mod_kernel,
        grid=(depth, n_chunk),
        in_specs=[
            pl.BlockSpec((N_COND, D_MODEL), lambda l, j: (0, 0)),
            pl.BlockSpec((None, D_MODEL, D_MODEL), lambda l, j: (l, 0, j)),
            pl.BlockSpec((None, 1, D_MODEL), lambda l, j: (l, 0, j)),
        ],
        out_specs=pl.BlockSpec((None, N_COND, D_MODEL), lambda l, j: (l, 0, j)),
        out_shape=jax.ShapeDtypeStruct((depth, N_COND, n_chunk * D_MODEL), F32),
        compiler_params=_params("arbitrary", "arbitrary"),
        name="modulation",
    )(cond, mod_w, mod_b.reshape(depth, 1, n_chunk * D_MODEL))
    return out.reshape(depth, N_COND, n_chunk, D_MODEL)


def _modulated_norm(x, g, mod, shift_row, scale_row):
    ms = jnp.mean(x * x, axis=-1, keepdims=True)
    y = x * lax.rsqrt(ms + EPS) * g
    return y * (1.0 + mod[scale_row:scale_row + 1, :]) + mod[shift_row:shift_row + 1, :]


def _trunk_specs(block_rows, width):
    n_prompt_blocks = N_PROMPT_TOK // block_rows
    return (pl.BlockSpec((block_rows, width), lambda i: (jnp.minimum(i, n_prompt_blocks - 1), 0)),
            pl.BlockSpec((block_rows, width), lambda i: (jnp.maximum(i - n_prompt_blocks, 0), 0)))


def _select_trunk(p_ref, l_ref, rows=slice(None)):
    block_rows = p_ref.shape[0]
    return jnp.where(pl.program_id(0) < N_PROMPT_TOK // block_rows, p_ref[rows, :], l_ref[rows, :])


def _cast_once(w_ref, wb_ref):
    @pl.when(pl.program_id(0) == 0)
    def _():
        wb_ref[...] = w_ref[...].astype(BF16)


def _resident(shape):
    return pl.BlockSpec(shape, lambda i: tuple(0 for _ in shape), pipeline_mode=pl.Buffered(1))


def _mod_spec(block_rows):
    return pl.BlockSpec((None, 6, D_MODEL), lambda i: (_cond_of_token_block(i, block_rows), 0, 0))


def _in_proj0_kernel(xp_ref, xl_ref, g_ref, mod_ref, w_ref, o_ref, wb_ref, hb_ref):
    _cast_once(w_ref, wb_ref)
    n = IN_PROJ_SUB_ROWS
    n_sub = xp_ref.shape[0] // n

    def prepare(r):
        x = _select_trunk(xp_ref, xl_ref, slice(r * n, (r + 1) * n))
        hb_ref[r] = _modulated_norm(x, g_ref[...], mod_ref[...], 0, 1).astype(BF16)

    def project(r):
        u = jnp.dot(hb_ref[r], wb_ref[...], preferred_element_type=F32)
        o_ref[r * n:(r + 1) * n, :] = u.astype(o_ref.dtype)

    prepare(0)
    for r in range(1, n_sub):
        prepare(r)
        project(r - 1)
    project(n_sub - 1)


def _in_proj0(x_prompt, x_latent, g, mod_l, w, block_rows=512):
    n = w.shape[1]
    return pl.pallas_call(
        _in_proj0_kernel,
        grid=(N_TOK // block_rows,),
        in_specs=[*_trunk_specs(block_rows, D_MODEL), _resident((1, D_MODEL)), _mod_spec(block_rows),
                  _resident((D_MODEL, n))],
        out_specs=pl.BlockSpec((block_rows, n), lambda i: (i, 0)),
        out_shape=jax.ShapeDtypeStruct((N_TOK, n), BF16),
        scratch_shapes=[pltpu.VMEM((D_MODEL, n), BF16),
                        pltpu.VMEM((block_rows // IN_PROJ_SUB_ROWS, IN_PROJ_SUB_ROWS, D_MODEL), BF16)],
        compiler_params=_params("arbitrary"),
        name="in_proj0",
    )(x_prompt, x_latent, g.reshape(1, D_MODEL), mod_l, w)


def _moe_mix(x_ref, ya_ref, yb_ref, wt_ref, mod_ref, rows=slice(None)):
    wt = wt_ref[rows, :]
    mix = wt[:, 0:1] * _unpack_rows(ya_ref[rows, :]) + wt[:, 1:2] * _unpack_rows(yb_ref[rows, :])
    return x_ref[rows, :] + mod_ref[5:6, :] * mix


def _in_proj1_kernel(x_ref, ya_ref, yb_ref, wt_ref, modp_ref, g_ref, mod_ref, w_ref, b_ref, xo_ref, o_ref,
                     wb_ref, hb_ref):
    _cast_once(w_ref, wb_ref)
    n = IN_PROJ_SUB_ROWS
    n_sub = x_ref.shape[0] // n

    def prepare(r):
        rows = slice(r * n, (r + 1) * n)
        x = _moe_mix(x_ref, ya_ref, yb_ref, wt_ref, modp_ref, rows)
        xo_ref[rows, :] = x
        hb_ref[r] = _modulated_norm(x, g_ref[...], mod_ref[...], 0, 1).astype(BF16)

    def project(r):
        rows = slice(r * n, (r + 1) * n)
        u = jnp.dot(hb_ref[r], wb_ref[...], preferred_element_type=F32) + b_ref[...]
        o_ref[rows, :] = u.astype(o_ref.dtype)

    prepare(0)
    for r in range(1, n_sub):
        prepare(r)
        project(r - 1)
    project(n_sub - 1)


def _in_proj1(x, moe_out, mod_prev, g, mod_l, w, bias, block_rows=512):
    ya, yb, w_tok = moe_out
    n = w.shape[1]
    tok = pl.BlockSpec((block_rows, D_MODEL), lambda i: (i, 0))
    packed = pl.BlockSpec((block_rows, ROW_WORDS), lambda i: (i, 0))
    return pl.pallas_call(
        _in_proj1_kernel,
        grid=(N_TOK // block_rows,),
        in_specs=[tok, packed, packed, pl.BlockSpec((block_rows, TOP_K), lambda i: (i, 0)), _mod_spec(block_rows),
                  _resident((1, D_MODEL)), _mod_spec(block_rows), _resident((D_MODEL, n)), _resident((1, n))],
        out_specs=(tok, pl.BlockSpec((block_rows, n), lambda i: (i, 0))),
        out_shape=(jax.ShapeDtypeStruct((N_TOK, D_MODEL), F32), jax.ShapeDtypeStruct((N_TOK, n), BF16)),
        scratch_shapes=[pltpu.VMEM((D_MODEL, n), BF16),
                        pltpu.VMEM((block_rows // IN_PROJ_SUB_ROWS, IN_PROJ_SUB_ROWS, D_MODEL), BF16)],
        compiler_params=_params("arbitrary"),
        name="in_proj1",
    )(x, ya, yb, w_tok, mod_prev, g.reshape(1, D_MODEL), mod_l, w, bias.reshape(1, n))


def _hgrn_kernel(*refs, seq_len, with_state):
    if with_state:
        (q_ref, zf_ref, zb_ref, i_ref, ga_ref, lb_ref, og_ref, s0_ref, o_ref, of_ref, ob_ref) = refs
    else:
        (q_ref, zf_ref, zb_ref, i_ref, ga_ref, lb_ref, og_ref, o_ref, s_ref, of_ref, ob_ref) = refs
    n_blocks = seq_len // HGRN_BLOCK
    chunks_per_block = HGRN_BLOCK // CHUNK

    lbr = lb_ref[...]
    mx = jnp.maximum(lbr[0], lbr[1])
    e0 = jnp.exp(lbr[0] - mx)
    e1 = jnp.exp(lbr[1] - mx)
    lb = e0 / (e0 + e1)

    row = lax.broadcasted_iota(jnp.int32, (HGRN_BLOCK, HGRN_BLOCK), 0)
    col = lax.broadcasted_iota(jnp.int32, (HGRN_BLOCK, HGRN_BLOCK), 1)
    same_chunk = (row // CHUNK) == (col // CHUNK)
    nt = (((1,), (1,)), ((), ()))
    tn = (((0,), (0,)), ((), ()))

    def per_chunk_row(x, idx):
        return jnp.concatenate(
            [jnp.broadcast_to(x[n * CHUNK + idx:n * CHUNK + idx + 1, :], (CHUNK, x.shape[1]))
             for n in range(chunks_per_block)], axis=0)

    def in_chunk_cumsum(tri, x):
        hi = x.astype(BF16)
        lo = (x - hi.astype(F32)).astype(BF16)
        return jnp.dot(tri, hi, preferred_element_type=F32) + jnp.dot(tri, lo, preferred_element_type=F32)

    def prepare(blk, cols, z_ref, lbd, forward):
        rows = slice(blk * HGRN_BLOCK, (blk + 1) * HGRN_BLOCK)
        keep = (same_chunk & (col <= row)) if forward else (same_chunk & (col >= row))
        tri = jnp.where(keep, 1.0, 0.0).astype(BF16)
        mid = CHUNK // 2 if forward else CHUNK - 1 - CHUNK // 2
        last = CHUNK - 1 if forward else 0
        f = lbd + (1.0 - lbd) * jax.nn.sigmoid(z_ref[rows, cols].astype(F32))
        lf = jnp.log(f)
        k = 1.0 - f
        q = q_ref[rows, cols].astype(F32)
        b = in_chunk_cumsum(tri, lf)
        bm = per_chunk_row(b, mid)
        bl = per_chunk_row(b, last)
        return dict(
            rows=rows, cols=cols, keep=keep, forward=forward,
            vb=i_ref[rows, cols].astype(BF16),
            qe=(q * jnp.exp(b - bm)).astype(BF16), ke=(k * jnp.exp(bm - b)).astype(BF16),
            qb=(q * jnp.exp(b)).astype(BF16), ks=(k * jnp.exp(bl - b)).astype(BF16), decay=jnp.exp(bl))

    def within_chunks(u):
        att = lax.dot_general(u["qe"], u["ke"], nt, preferred_element_type=F32)
        att = jnp.where(u["keep"], att, 0.0)
        u["o_intra"] = jnp.dot(att.astype(BF16), u["vb"], preferred_element_type=F32)
        u["upd"] = [lax.dot_general(u["vb"][n * CHUNK:(n + 1) * CHUNK], u["ks"][n * CHUNK:(n + 1) * CHUNK], tn,
                                    preferred_element_type=F32) for n in range(chunks_per_block)]

    def across_chunks(u, st, out_ref):
        order = range(chunks_per_block) if u["forward"] else range(chunks_per_block - 1, -1, -1)
        o_inter = [None] * chunks_per_block
        for n in order:
            cr = slice(n * CHUNK, (n + 1) * CHUNK)
            o_inter[n] = lax.dot_general(u["qb"][cr], st.astype(BF16), nt, preferred_element_type=F32)
            st = st * u["decay"][n * CHUNK:n * CHUNK + 1, :] + u["upd"][n]
        out_ref[u["rows"], u["cols"]] = u["o_intra"] + jnp.concatenate(o_inter, axis=0)
        return st

    n_heads = q_ref.shape[1] // A_DK
    head_cols = [slice(hd * A_DK, (hd + 1) * A_DK) for hd in range(n_heads)]
    if with_state:
        states = {(hd, d): s0_ref[d, hd].T for hd in range(n_heads) for d in range(2)}
    else:
        states = {(hd, d): jnp.zeros((A_DK, A_DK), F32) for hd in range(n_heads) for d in range(2)}
    for step in range(n_blocks):
        units = {}
        for hd, cols in enumerate(head_cols):
            units[hd, 0] = prepare(step, cols, zf_ref, lb[0:1, cols], True)
            units[hd, 1] = prepare(n_blocks - 1 - step, cols, zb_ref, lb[1:2, cols], False)
        for u in units.values():
            within_chunks(u)
        for key, u in units.items():
            states[key] = across_chunks(u, states[key], of_ref if key[1] == 0 else ob_ref)
    for hd, cols in enumerate(head_cols):
        if not with_state:
            s_ref[0, hd] = states[hd, 0].T
            s_ref[1, hd] = states[hd, 1].T
        o = of_ref[:, cols] + ob_ref[:, cols]
        o = o * lax.rsqrt(jnp.mean(o * o, axis=-1, keepdims=True) + EPS) * og_ref[:, cols]
        ga = ga_ref[:, cols].astype(F32)
        o_ref[:, cols] = (o * (ga * jax.nn.sigmoid(ga))).astype(o_ref.dtype)


def _hgrn(z, hgrn_lb, onorm_g, state, *, latent):
    seq_len = LATENT_LEN if latent else PROMPT_LEN
    n_seq = N_LATENT_SEQ if latent else N_PROMPT_SEQ
    row0 = (N_PROMPT_TOK // seq_len) if latent else 0

    hw = HGRN_HEADS_PER_STEP * A_DK
    n_hg = A_HEADS // HGRN_HEADS_PER_STEP

    def zspec(part):
        return pl.BlockSpec((seq_len, hw), lambda s, h: (row0 + s, part * n_hg + h))

    in_specs = [zspec(0), zspec(1), zspec(2), zspec(3), zspec(4),
                pl.BlockSpec((2, 2, hw), lambda s, h: (0, 0, h)),
                pl.BlockSpec((1, hw), lambda s, h: (0, h))]
    args = [z, z, z, z, z, hgrn_lb, onorm_g.reshape(1, A_WIDTH)]
    state_spec = pl.BlockSpec((None, None, 2, HGRN_HEADS_PER_STEP, A_DK, A_DK), lambda s, h: (s, 0, 0, h, 0, 0))
    o_shape = jax.ShapeDtypeStruct((n_seq * seq_len, A_WIDTH), BF16)
    o_spec = pl.BlockSpec((seq_len, hw), lambda s, h: (s, h))
    if latent:
        in_specs.append(state_spec)
        args.append(state)
        out_shape, out_specs = o_shape, o_spec
    else:
        out_shape = (o_shape, jax.ShapeDtypeStruct((n_seq, 1, 2, A_HEADS, A_DK, A_DK), F32))
        out_specs = (o_spec, state_spec)
    return pl.pallas_call(
        functools.partial(_hgrn_kernel, seq_len=seq_len, with_state=latent),
        grid=(n_seq, n_hg),
        in_specs=in_specs,
        out_specs=out_specs,
        out_shape=out_shape,
        scratch_shapes=[pltpu.VMEM((seq_len, hw), F32), pltpu.VMEM((seq_len, hw), F32)],
        compiler_params=_params("arbitrary", "arbitrary"),
        name="hgrn_latent" if latent else "hgrn_prompt",
    )(*args)


def _rope_tables():
    pos = np.arange(LATENT_LEN)
    row, colp = pos // GRID_W, pos % GRID_W
    inv = ROPE_THETA ** (-np.arange(ROPE_PAIRS, dtype=np.float32) / ROPE_PAIRS)
    inv = inv.astype(np.float32)
    ang_r = (row.astype(np.float32)[:, None] * inv).astype(np.float32)
    ang_c = (colp.astype(np.float32)[:, None] * inv).astype(np.float32)
    cos = np.concatenate([np.cos(ang_r), np.cos(ang_r), np.cos(ang_c), np.cos(ang_c)], axis=1)
    sin = np.concatenate([-np.sin(ang_r), np.sin(ang_r), -np.sin(ang_c), np.sin(ang_c)], axis=1)
    return cos.astype(np.float32), sin.astype(np.float32)


def _head_mean_matrix(width):
    idx = np.arange(width) // HEAD_DIM
    return jnp.asarray((idx[:, None] == idx[None, :]).astype(np.float32) / HEAD_DIM).astype(BF16)


def _attn_kernel(*refs, latent):
    if latent:
        (q_ref, k_ref, v_ref, qg_ref, kg_ref, gq_ref, gk_ref, cosq_ref, sinq_ref, cosk_ref, sink_ref,
         ck_ref, cv_ref, o_ref) = refs
    else:
        (q_ref, k_ref, v_ref, qg_ref, kg_ref, gq_ref, gk_ref, o_ref, kout_ref, vout_ref) = refs
    pair_w = 2 * HEAD_DIM

    def head_norm(x, mean_ref, gain):
        sq = x * x
        hi = sq.astype(BF16)
        lo = (sq - hi.astype(F32)).astype(BF16)
        ms = jnp.dot(hi, mean_ref[...], preferred_element_type=F32)
        ms = ms + jnp.dot(lo, mean_ref[...], preferred_element_type=F32)
        return x * lax.rsqrt(ms + EPS) * gain

    def rope(x, cos, sin):
        n = x.shape[1]
        lane = lax.broadcasted_iota(jnp.int32, x.shape, 1)
        first_of_pair = (lane // ROPE_PAIRS) % 2 == 0
        swapped = jnp.where(first_of_pair, pltpu.roll(x, n - ROPE_PAIRS, axis=1), pltpu.roll(x, ROPE_PAIRS, axis=1))
        return x * cos + swapped * sin

    def attend(rows, seq_idx=None):
        q = head_norm(q_ref[rows, :].astype(F32), gq_ref, qg_ref[...])
        k = head_norm(k_ref[rows, :].astype(F32), gk_ref, kg_ref[...])
        if latent:
            q = rope(q, cosq_ref[...], sinq_ref[...])
            k = rope(k, cosk_ref[...], sink_ref[...])
        q = q * (HEAD_DIM ** -0.5)
        v = v_ref[rows, :].astype(F32)
        n_q = q.shape[0]
        low_kv = lax.broadcasted_iota(jnp.int32, k.shape, 1) < HEAD_DIM
        low_q = lax.broadcasted_iota(jnp.int32, (n_q, pair_w), 1) < HEAD_DIM
        k_swapped = pltpu.roll(k, HEAD_DIM, axis=1)
        v_swapped = pltpu.roll(v, HEAD_DIM, axis=1)
        if not latent:
            kout_ref[seq_idx] = k.T
            vout_ref[seq_idx] = v.T
        nt = (((1,), (1,)), ((), ()))
        for j in range(KV_HEADS):
            kd = (jnp.where(low_kv, k, k_swapped) if j == 0 else jnp.where(low_kv, k_swapped, k)).astype(BF16)
            vd = (jnp.where(low_kv, v, v_swapped) if j == 0 else jnp.where(low_kv, v_swapped, v)).astype(BF16)
            tiles = range(j * Q_PER_KV // 2, (j + 1) * Q_PER_KV // 2)
            parts = []
            for t in tiles:
                qt = q[:, t * pair_w:(t + 1) * pair_w]
                parts += [jnp.where(low_q, qt, 0.0), jnp.where(low_q, 0.0, qt)]
            qs = jnp.concatenate(parts, axis=0).astype(BF16)
            s_new = lax.dot_general(qs, kd, nt, preferred_element_type=F32)
            m = jnp.max(s_new, axis=-1, keepdims=True)
            if latent:
                ckd = jnp.concatenate([ck_ref[j], ck_ref[j]], axis=1).astype(BF16)
                cvd = jnp.concatenate([cv_ref[j], cv_ref[j]], axis=1).astype(BF16)
                s_old = lax.dot_general(qs, ckd, nt, preferred_element_type=F32)
                m = jnp.maximum(m, jnp.max(s_old, axis=-1, keepdims=True))
            p_new = jnp.exp(s_new - m)
            den = jnp.sum(p_new, axis=-1, keepdims=True)
            acc = jnp.dot(p_new.astype(BF16), vd, preferred_element_type=F32)
            if latent:
                p_old = jnp.exp(s_old - m)
                den = den + jnp.sum(p_old, axis=-1, keepdims=True)
                acc = acc + jnp.dot(p_old.astype(BF16), cvd, preferred_element_type=F32)
            out = acc / den
            for i, t in enumerate(tiles):
                lo_head = out[(2 * i) * n_q:(2 * i + 1) * n_q, :]
                hi_head = out[(2 * i + 1) * n_q:(2 * i + 2) * n_q, :]
                o_ref[rows, t * pair_w:(t + 1) * pair_w] = jnp.where(low_q, lo_head, hi_head).astype(o_ref.dtype)

    if latent:
        attend(slice(None))
    else:
        seq = PROMPT_LEN

        for s in range(q_ref.shape[0] // seq):
            attend(slice(s * seq, (s + 1) * seq), s)


def _attn_common_args(qn_g, kn_g):
    q_w, kv_w = Q_HEADS * HEAD_DIM, KV_HEADS * HEAD_DIM
    return (jnp.tile(qn_g, Q_HEADS).reshape(1, q_w), jnp.tile(kn_g, KV_HEADS).reshape(1, kv_w),
            _head_mean_matrix(q_w), _head_mean_matrix(kv_w))


def _attention_prompt(z, qn_g, kn_g):
    seqs = 4
    L = seqs * PROMPT_LEN
    cache_shape = jax.ShapeDtypeStruct((N_PROMPT_SEQ, KV_HEADS * HEAD_DIM, PROMPT_LEN), F32)
    cache_spec = pl.BlockSpec((seqs, KV_HEADS * HEAD_DIM, PROMPT_LEN), lambda s: (s, 0, 0))
    q_w, kv_w = Q_HEADS * HEAD_DIM, KV_HEADS * HEAD_DIM
    q_col = (5 * A_WIDTH) // q_w
    k_col = (5 * A_WIDTH + q_w) // kv_w
    const = lambda r, c: pl.BlockSpec((r, c), lambda s: (0, 0))
    return pl.pallas_call(
        functools.partial(_attn_kernel, latent=False),
        grid=(N_PROMPT_TOK // L,),
        in_specs=[
            pl.BlockSpec((L, q_w), lambda s: (s, q_col)),
            pl.BlockSpec((L, kv_w), lambda s: (s, k_col)),
            pl.BlockSpec((L, kv_w), lambda s: (s, k_col + 1)),
            const(1, q_w), const(1, kv_w), const(q_w, q_w), const(kv_w, kv_w),
        ],
        out_specs=(pl.BlockSpec((L, q_w), lambda s: (s, 0)), cache_spec, cache_spec),
        out_shape=(jax.ShapeDtypeStruct((N_PROMPT_TOK, q_w), BF16), cache_shape, cache_shape),
        compiler_params=_params("arbitrary"),
        name="attn_prompt",
    )(z, z, z, *_attn_common_args(qn_g, kn_g))


def _attention_latent(z, qn_g, kn_g, cache_k, cache_v):
    L = LATENT_LEN
    nqb = L // Q_BLOCK
    q_w, kv_w = Q_HEADS * HEAD_DIM, KV_HEADS * HEAD_DIM
    q_col = (5 * A_WIDTH) // q_w
    k_col = (5 * A_WIDTH + q_w) // kv_w
    qrow0 = N_PROMPT_TOK // Q_BLOCK
    krow0 = N_PROMPT_TOK // L
    cos, sin = _rope_tables()
    cos_q, sin_q = jnp.asarray(np.tile(cos, (1, Q_HEADS))), jnp.asarray(np.tile(sin, (1, Q_HEADS)))
    cos_k, sin_k = jnp.asarray(np.tile(cos, (1, KV_HEADS))), jnp.asarray(np.tile(sin, (1, KV_HEADS)))
    const = lambda r, c: pl.BlockSpec((r, c), lambda s, b: (0, 0))
    cache_spec = pl.BlockSpec((None, None, KV_HEADS, PAST_LEN, HEAD_DIM), lambda s, b: (s, 0, 0, 0, 0))
    return pl.pallas_call(
        functools.partial(_attn_kernel, latent=True),
        grid=(N_LATENT_SEQ, nqb),
        in_specs=[
            pl.BlockSpec((Q_BLOCK, q_w), lambda s, b: (qrow0 + s * nqb + b, q_col)),
            pl.BlockSpec((L, kv_w), lambda s, b: (krow0 + s, k_col)),
            pl.BlockSpec((L, kv_w), lambda s, b: (krow0 + s, k_col + 1)),
            const(1, q_w), const(1, kv_w), const(q_w, q_w), const(kv_w, kv_w),
            pl.BlockSpec((Q_BLOCK, q_w), lambda s, b: (b, 0)),
            pl.BlockSpec((Q_BLOCK, q_w), lambda s, b: (b, 0)),
            const(L, kv_w), const(L, kv_w),
            cache_spec, cache_spec,
        ],
        out_specs=pl.BlockSpec((Q_BLOCK, q_w), lambda s, b: (s * nqb + b, 0)),
        out_shape=jax.ShapeDtypeStruct((N_LATENT_TOK, q_w), BF16),
        compiler_params=_params("arbitrary", "arbitrary"),
        name="attn_latent",
    )(z, z, z, *_attn_common_args(qn_g, kn_g), cos_q, sin_q, cos_k, sin_k, cache_k, cache_v)


def _out_proj_kernel(*refs, n_act, n_x):
    a_refs = refs[:2 * n_act]
    x_refs = refs[2 * n_act:2 * n_act + n_x]
    g_ref, mod_ref, rw_ref, w_ref, xo_ref, h_ref, lg_ref, wb_ref, rws_ref, acc_ref = refs[2 * n_act + n_x:]
    _cast_once(w_ref, wb_ref)

    @pl.when(pl.program_id(0) == 0)
    def _():
        rw = rw_ref[...]
        hi = rw.astype(BF16).astype(F32)
        lo = (rw - hi).astype(BF16).astype(F32)
        rws_ref[...] = (hi + pltpu.roll(lo, N_EXPERTS, axis=1)).astype(BF16)

    mod = mod_ref[...]
    n = OUT_PROJ_SUB_ROWS

    n_sub = xo_ref.shape[0] // n

    def sub_rows(r):
        if isinstance(r, int):
            return slice(r * n, (r + 1) * n)
        return pl.ds(pl.multiple_of(r * n, n), n)

    def project(r):
        rows = sub_rows(r)
        acc = None
        k0 = 0
        for ap_ref, al_ref in zip(a_refs[0::2], a_refs[1::2]):
            k1 = k0 + ap_ref.shape[1]
            part = jnp.dot(_select_trunk(ap_ref, al_ref, rows), wb_ref[k0:k1, :], preferred_element_type=F32)
            acc = part if acc is None else acc + part
            k0 = k1
        acc_ref[r % 2] = acc

    def finish(r):
        rows = sub_rows(r)
        x_in = x_refs[0][rows, :] if n_x == 1 else _select_trunk(*x_refs, rows)
        x = x_in + mod[2:3, :] * acc_ref[r % 2]
        xo_ref[rows, :] = x
        h = _modulated_norm(x, g_ref[...], mod, 3, 4)
        h_ref[rows, :] = _pack_rows(h)
        h_hi = h.astype(BF16)
        h_lo = (h - h_hi.astype(F32)).astype(BF16)
        both = jnp.dot(jnp.concatenate([h_hi, h_lo], axis=0), rws_ref[...], preferred_element_type=F32)
        from_hi, from_lo = both[:n], both[n:]
        lg = from_hi + pltpu.roll(from_hi, ROUTER_LANES - N_EXPERTS, axis=1) + from_lo
        lg_ref[:, rows] = lg.T[:N_EXPERTS, :]

    project(0)
    for r in range(n_sub - 1):
        project(r + 1)
        finish(r)
    finish(n_sub - 1)


def _out_proj(acts, w, xs, g, mod_l, router_wp, block_rows=1024):
    tok = lambda width: pl.BlockSpec((block_rows, width), lambda i: (i, 0))
    in_specs = [spec for ap, _ in acts for spec in _trunk_specs(block_rows, ap.shape[1])]
    in_specs += [tok(D_MODEL)] if len(xs) == 1 else list(_trunk_specs(block_rows, D_MODEL))
    in_specs += [_resident((1, D_MODEL)), _mod_spec(block_rows), _resident((D_MODEL, ROUTER_LANES)),
                 _resident(w.shape)]
    return pl.pallas_call(
        functools.partial(_out_proj_kernel, n_act=len(acts), n_x=len(xs)),
        grid=(N_TOK // block_rows,),
        in_specs=in_specs,
        out_specs=(tok(D_MODEL), tok(ROW_WORDS), pl.BlockSpec((N_EXPERTS, block_rows), lambda i: (0, i))),
        out_shape=(jax.ShapeDtypeStruct((N_TOK, D_MODEL), F32),
                   jax.ShapeDtypeStruct((N_TOK, ROW_WORDS), jnp.int32),
                   jax.ShapeDtypeStruct((N_EXPERTS, N_TOK), F32)),
        scratch_shapes=[pltpu.VMEM(w.shape, BF16), pltpu.VMEM((D_MODEL, ROUTER_LANES), BF16),
                        pltpu.VMEM((2, OUT_PROJ_SUB_ROWS, D_MODEL), F32)],
        compiler_params=_params("arbitrary"),
        name="out_proj",
    )(*[a for pair in acts for a in pair], *xs, g.reshape(1, D_MODEL), mod_l, router_wp, w)


def _router_kernel(lg_ref, rb_ref, pos_ref, w_ref, plan_ref, rank_ref):
    lg = lg_ref[...]
    ex = jnp.exp(lg - jnp.max(lg, axis=0, keepdims=True))
    scores = ex / jnp.sum(ex, axis=0, keepdims=True)
    biased = scores + rb_ref[...]
    rows = [biased[e:e + 1, :] for e in range(N_EXPERTS)]
    selected = []
    group_score = []
    for gi in range(N_GROUPS):
        r = rows[gi * EXPERTS_PER_GROUP:(gi + 1) * EXPERTS_PER_GROUP]
        total = None
        for i in range(EXPERTS_PER_GROUP):
            rank = None
            for j in range(EXPERTS_PER_GROUP):
                if j == i:
                    continue
                ahead = (r[j] > r[i]) if j > i else (r[j] >= r[i])
                ahead = jnp.where(ahead, 1.0, 0.0)
                rank = ahead if rank is None else rank + ahead
            sel = rank < 1.5
            selected.append(sel)
            contrib = jnp.where(sel, r[i], 0.0)
            total = contrib if total is None else total + contrib
        group_score.append(total)
    best = group_score[0]
    best_group = jnp.zeros_like(best)
    for gi in range(1, N_GROUPS):
        better = group_score[gi] > best
        best_group = jnp.where(better, float(gi), best_group)
        best = jnp.where(better, group_score[gi], best)
    picked = []
    chosen = []
    den = None
    for e in range(N_EXPERTS):
        in_group = best_group == float(e // EXPERTS_PER_GROUP)
        use = jnp.where(selected[e], jnp.where(in_group, 1.0, 0.0), 0.0)
        w = use * scores[e:e + 1, :]
        chosen.append(use)
        picked.append(w)
        den = w if den is None else den + w
    lanes = 128
    n_blk = N_TOK // lanes
    li = lax.broadcasted_iota(jnp.int32, (lanes, lanes), 0)
    lj = lax.broadcasted_iota(jnp.int32, (lanes, lanes), 1)
    prefix = jnp.where(li <= lj, 1.0, 0.0).astype(BF16)
    carry = jnp.zeros((N_EXPERTS, 1), F32)
    for blk in range(n_blk):
        cols = slice(blk * lanes, (blk + 1) * lanes)
        m = jnp.concatenate([chosen[e][:, cols] for e in range(N_EXPERTS)], axis=0)
        incl = jnp.dot(m.astype(BF16), prefix, preferred_element_type=F32)
        rank_ref[:, cols] = incl - m + carry
        carry = carry + incl[:, lanes - 1:lanes]
    count = carry
    padded = jnp.floor((count + float(MOE_TILE - 1)) * (1.0 / MOE_TILE)) * float(MOE_TILE)
    erow = lax.broadcasted_iota(jnp.int32, (N_EXPERTS, 1), 0)
    offset = jnp.zeros((N_EXPERTS, 1), F32)
    for e in range(N_EXPERTS - 1):
        offset = offset + jnp.where(erow > e, padded[e:e + 1, :], 0.0)
    seen = jnp.zeros_like(den)
    pos_a = jnp.zeros_like(den)
    pos_b = jnp.zeros_like(den)
    w_a = jnp.zeros_like(den)
    w_b = jnp.zeros_like(den)
    for e in range(N_EXPERTS):
        pos_e = rank_ref[e:e + 1, :] + offset[e:e + 1, :]
        gate_e = picked[e] / den
        first = jnp.where(seen < 0.5, chosen[e], 0.0) > 0.5
        second = jnp.where(seen > 0.5, chosen[e], 0.0) > 0.5
        pos_a = jnp.where(first, pos_e, pos_a)
        w_a = jnp.where(first, gate_e, w_a)
        pos_b = jnp.where(second, pos_e, pos_b)
        w_b = jnp.where(second, gate_e, w_b)
        seen = seen + chosen[e]
    pos_ref[0:1, :] = pos_a.astype(jnp.int32)
    pos_ref[1:2, :] = pos_b.astype(jnp.int32)
    w_rows = jnp.concatenate([w_a, w_b, jnp.zeros((6, N_TOK), F32)], axis=0)
    ei = lax.broadcasted_iota(jnp.int32, (8, lanes), 0)
    ej = lax.broadcasted_iota(jnp.int32, (8, lanes), 1)
    eye = jnp.where(ei == ej, 1.0, 0.0).astype(BF16)
    tn = (((0,), (0,)), ((), ()))
    hi = w_rows.astype(BF16)
    r1 = w_rows - hi.astype(F32)
    mid = r1.astype(BF16)
    lo = (r1 - mid.astype(F32)).astype(BF16)
    w_cols = lax.dot_general(hi, eye, tn, preferred_element_type=F32)
    w_cols = w_cols + lax.dot_general(mid, eye, tn, preferred_element_type=F32)
    w_cols = w_cols + lax.dot_general(lo, eye, tn, preferred_element_type=F32)
    w_ref[...] = w_cols[:, :TOP_K]
    start = (lax.broadcasted_iota(jnp.int32, (N_EXPERTS, lanes), 1) * MOE_TILE).astype(F32)
    end = offset + padded
    tile_expert = jnp.sum(jnp.where(end <= start, 1.0, 0.0), axis=0, keepdims=True)
    inside = (offset <= start) & (start < end)
    real = jnp.clip(count - (start - offset), 0.0, float(MOE_TILE))
    tile_rows = jnp.sum(jnp.where(inside, real, 0.0), axis=0, keepdims=True)
    plan_ref[0:1, :] = jnp.minimum(tile_expert, float(N_EXPERTS - 1)).astype(jnp.int32)
    plan_ref[1:2, :] = tile_rows.astype(jnp.int32)


def _router(logits_t, router_b):
    whole = lambda shape: pl.BlockSpec(shape, lambda i: (0, 0))
    return pl.pallas_call(
        _router_kernel,
        grid=(1,),
        in_specs=[whole((N_EXPERTS, N_TOK)), whole((N_EXPERTS, 1))],
        out_specs=(whole((2, N_TOK)), whole((N_TOK, TOP_K)), whole((2, 128))),
        out_shape=(jax.ShapeDtypeStruct((2, N_TOK), jnp.int32),
                   jax.ShapeDtypeStruct((N_TOK, TOP_K), F32),
                   jax.ShapeDtypeStruct((2, 128), jnp.int32)),
        scratch_shapes=[pltpu.VMEM((N_EXPERTS, N_TOK), F32)],
        compiler_params=_params("arbitrary"),
        name="router",
    )(logits_t, router_b.reshape(N_EXPERTS, 1))


def _sc_mesh():
    return plsc.VectorSubcoreMesh(core_axis_name="c", subcore_axis_name="s")


def _sc_worker_base():
    return (lax.axis_index("s") * SC_CORES + lax.axis_index("c")) * SC_TOKENS_PER_WORKER


def _moe_dispatch(h, pos_a, pos_b):
    n_chunks = SC_TOKENS_PER_WORKER // SC_CHUNK
    idx = pltpu.VMEM((SC_CHUNK,), jnp.int32)

    @functools.partial(
        pl.kernel, mesh=_sc_mesh(),
        out_type=jax.ShapeDtypeStruct((MOE_ROWS, ROW_WORDS), jnp.int32),
        scratch_types=[idx, idx, idx, idx, pltpu.VMEM((2, SC_CHUNK, ROW_WORDS), jnp.int32),
                       pltpu.SemaphoreType.DMA((6,)), pltpu.SemaphoreType.DMA((4,))],
        name="moe_dispatch",
    )
    def run(h_hbm, pa_hbm, pb_hbm, xs_hbm, ia0, ib0, ia1, ib1, rows_v, sem_in, sem_out):
        base = _sc_worker_base()
        ia, ib = (ia0, ia1), (ib0, ib1)

        def start_loads(c):
            slot = c % 2
            tok = pl.ds(pl.multiple_of(base + c * SC_CHUNK, 8), SC_CHUNK)
            return (pltpu.async_copy(pa_hbm.at[tok], ia[slot], sem_in.at[3 * slot]),
                    pltpu.async_copy(pb_hbm.at[tok], ib[slot], sem_in.at[3 * slot + 1]),
                    pltpu.async_copy(h_hbm.at[tok], rows_v.at[slot], sem_in.at[3 * slot + 2]))

        loads = start_loads(0)
        scatters = [(), ()]
        for c in range(n_chunks):
            slot = c % 2
            for cp in loads:
                cp.wait()
            if c + 1 < n_chunks:
                for cp in scatters[1 - slot]:
                    cp.wait()
                scatters[1 - slot] = ()
                loads = start_loads(c + 1)
            scatters[slot] = (pltpu.async_copy(rows_v.at[slot], xs_hbm.at[ia[slot]], sem_out.at[2 * slot]),
                              pltpu.async_copy(rows_v.at[slot], xs_hbm.at[ib[slot]], sem_out.at[2 * slot + 1]))
        for pending in scatters:
            for cp in pending:
                cp.wait()

    return run(h, pos_a, pos_b)


def _moe_collect(ys, pos_a, pos_b):
    n_chunks = SC_TOKENS_PER_WORKER // SC_CHUNK
    out = jax.ShapeDtypeStruct((N_TOK, ROW_WORDS), jnp.int32)
    idx = pltpu.VMEM((SC_TOKENS_PER_WORKER,), jnp.int32)
    rows = pltpu.VMEM((2, SC_CHUNK, ROW_WORDS), jnp.int32)

    @functools.partial(
        pl.kernel, mesh=_sc_mesh(), out_type=(out, out),
        scratch_types=[idx, idx, rows, rows, pltpu.SemaphoreType.DMA((4,)), pltpu.SemaphoreType.DMA((4,))],
        name="moe_collect",
    )
    def run(ys_hbm, pa_hbm, pb_hbm, ya_hbm, yb_hbm, ia_v, ib_v, ra_v, rb_v, sem_g, sem_w):
        base = _sc_worker_base()
        mine = pl.ds(pl.multiple_of(base, 8), SC_TOKENS_PER_WORKER)
        pltpu.sync_copy(pa_hbm.at[mine], ia_v)
        pltpu.sync_copy(pb_hbm.at[mine], ib_v)
        writes = [(), ()]
        for c in range(n_chunks):
            slot = c % 2
            for cp in writes[slot]:
                cp.wait()
            part = pl.ds(c * SC_CHUNK, SC_CHUNK)
            tok = pl.ds(pl.multiple_of(base + c * SC_CHUNK, 8), SC_CHUNK)
            ga = pltpu.async_copy(ys_hbm.at[ia_v.at[part]], ra_v.at[slot], sem_g.at[slot])
            gb = pltpu.async_copy(ys_hbm.at[ib_v.at[part]], rb_v.at[slot], sem_g.at[2 + slot])
            ga.wait()
            wa = pltpu.async_copy(ra_v.at[slot], ya_hbm.at[tok], sem_w.at[slot])
            gb.wait()
            wb = pltpu.async_copy(rb_v.at[slot], yb_hbm.at[tok], sem_w.at[2 + slot])
            writes[slot] = (wa, wb)
        for pending in writes:
            for cp in pending:
                cp.wait()

    return run(ys, pos_a, pos_b)


def _experts_kernel(plan_ref, xs_ref, wg_hbm, wu_hbm, wd_hbm, y_ref,
                    sg_ref, su_ref, sd_ref, wgb_ref, wub_ref, wdb_ref, hid_ref, sems, seg_ref, *, layer):
    j = pl.program_id(0)
    n_tiles = pl.num_programs(0)
    expert = plan_ref[j]
    n_real = plan_ref[PLAN_LANES + j]
    fresh = jnp.logical_or(j == 0, expert != plan_ref[jnp.maximum(j - 1, 0)])

    def weight_copies(e, slot):
        return (pltpu.make_async_copy(wg_hbm.at[layer, e], sg_ref.at[slot], sems.at[slot, 0]),
                pltpu.make_async_copy(wu_hbm.at[layer, e], su_ref.at[slot], sems.at[slot, 1]),
                pltpu.make_async_copy(wd_hbm.at[layer, e], sd_ref.at[slot], sems.at[slot, 2]))

    @pl.when(j == 0)
    def _():
        seg_ref[0] = 0

        @pl.when(n_real > 0)
        def _():
            for cp in weight_copies(expert, 0):
                cp.start()

    @pl.when(jnp.logical_and(n_real > 0, fresh))
    def _():
        slot = seg_ref[0] % 2
        for cp in weight_copies(expert, slot):
            cp.wait()
        wgb_ref[...] = sg_ref[slot].astype(BF16)
        wub_ref[...] = su_ref[slot].astype(BF16)
        wdb_ref[...] = sd_ref[slot].astype(BF16)
        nxt = lax.while_loop(lambda t: jnp.logical_and(t < n_tiles, plan_ref[jnp.minimum(t, n_tiles - 1)] == expert),
                             lambda t: t + 1, j + 1)
        nxt_c = jnp.minimum(nxt, n_tiles - 1)

        @pl.when(jnp.logical_and(nxt < n_tiles, plan_ref[PLAN_LANES + nxt_c] > 0))
        def _():
            for cp in weight_copies(plan_ref[nxt_c], 1 - slot):
                cp.start()

        seg_ref[0] = seg_ref[0] + 1

    @pl.when(n_real > 0)
    def _():
        n = EXPERT_SUB_ROWS
        n_sub = xs_ref.shape[0] // n
        row = lax.broadcasted_iota(jnp.int32, (n, xs_ref.shape[1]), 0)

        def up(r):
            rows = slice(r * n, (r + 1) * n)
            words = jnp.where(row < n_real - r * n, xs_ref[rows, :], 0)
            x = _unpack_rows(words).astype(BF16)
            a = jnp.dot(x, wgb_ref[...], preferred_element_type=F32)
            b = jnp.dot(x, wub_ref[...], preferred_element_type=F32)
            hid_ref[r] = ((a * jax.nn.sigmoid(a)) * b).astype(BF16)

        def down(r):
            rows = slice(r * n, (r + 1) * n)
            y_ref[rows, :] = _pack_rows(jnp.dot(hid_ref[r], wdb_ref[...], preferred_element_type=F32))

        up(0)
        for r in range(1, n_sub):
            up(r)
            down(r - 1)
        down(n_sub - 1)


def _experts(plan, xs, w_gate, w_up, w_down, layer):
    hbm = pl.BlockSpec(memory_space=pl.ANY)
    return pl.pallas_call(
        functools.partial(_experts_kernel, layer=layer),
        grid_spec=pltpu.PrefetchScalarGridSpec(
            num_scalar_prefetch=1,
            grid=(MOE_ROWS // MOE_TILE,),
            in_specs=[pl.BlockSpec((MOE_TILE, ROW_WORDS), lambda j, plan: (j, 0)), hbm, hbm, hbm],
            out_specs=pl.BlockSpec((MOE_TILE, ROW_WORDS), lambda j, plan: (j, 0)),
            scratch_shapes=[pltpu.VMEM((2, D_MODEL, D_EXPERT), F32), pltpu.VMEM((2, D_MODEL, D_EXPERT), F32),
                            pltpu.VMEM((2, D_EXPERT, D_MODEL), F32),
                            pltpu.VMEM((D_MODEL, D_EXPERT), BF16), pltpu.VMEM((D_MODEL, D_EXPERT), BF16),
                            pltpu.VMEM((D_EXPERT, D_MODEL), BF16),
                            pltpu.VMEM((MOE_TILE // EXPERT_SUB_ROWS, EXPERT_SUB_ROWS, D_EXPERT), BF16),
                            pltpu.SemaphoreType.DMA((2, 3)), pltpu.SMEM((1,), jnp.int32)],
        ),
        out_shape=jax.ShapeDtypeStruct((MOE_ROWS, ROW_WORDS), jnp.int32),
        compiler_params=_params("arbitrary"),
        name="experts",
    )(plan, xs, w_gate, w_up, w_down)


def _combine_kernel(x_ref, ya_ref, yb_ref, wt_ref, mod_ref, o_ref):
    o_ref[...] = _moe_mix(x_ref, ya_ref, yb_ref, wt_ref, mod_ref)


def _combine(x, moe_out, mod_l, tok0, n_tok, block_rows=512):
    ya, yb, w_tok = moe_out
    b0 = tok0 // block_rows
    rows = lambda width: pl.BlockSpec((block_rows, width), lambda i: (b0 + i, 0))
    return pl.pallas_call(
        _combine_kernel,
        grid=(n_tok // block_rows,),
        in_specs=[rows(D_MODEL), rows(ROW_WORDS), rows(ROW_WORDS), rows(TOP_K),
                  pl.BlockSpec((None, 6, D_MODEL), lambda i: (_cond_of_token_block(b0 + i, block_rows), 0, 0))],
        out_specs=pl.BlockSpec((block_rows, D_MODEL), lambda i: (i, 0)),
        out_shape=jax.ShapeDtypeStruct((n_tok, D_MODEL), F32),
        compiler_params=_params("arbitrary"),
        name="combine",
    )(x, ya, yb, w_tok, mod_l)


def _moe(h, logits_t, router_b, w_gate, w_up, w_down, layer):
    pos, w, plan = _router(logits_t, router_b)
    xs = _moe_dispatch(h, pos[0], pos[1])
    ys = _experts(plan.reshape(-1), xs, w_gate, w_up, w_down, layer)
    ya, yb = _moe_collect(ys, pos[0], pos[1])
    return ya, yb, w


def _dft_tables(L):
    k = np.arange(L)[:, None]
    m = np.arange(L)[None, :]
    r = (k * m) % (2 * L)
    ang = np.pi * r.astype(np.float64) / L
    fc = np.cos(ang)
    fs = np.sin(ang)
    fs[0, :] = np.where(np.arange(L) % 2 == 0, 1.0, -1.0)
    wk = np.full((L, 1), 1.0 / L)
    wk[0, 0] = 0.5 / L
    gc = (fc * wk).T
    gs = (fs * wk).T
    return [jnp.asarray(t.astype(np.float32)).astype(BF16) for t in (fc, fs, gc, gs)]


def _filter_consts(L):
    t = np.linspace(0.0, 1.0, L, dtype=np.float32)[:, None]
    w = (np.float32(2.0 * np.pi) * np.arange(L, dtype=np.float32)[:, None] / np.float32(L)).astype(np.float32)
    fb = np.linspace(1e-4, HY_BANDS - 1, HY_BANDS, dtype=np.float32)[None, :]
    emb = np.concatenate([t, np.cos(fb * w), -np.sin(fb * w)], axis=-1).astype(np.float32)
    lo = math.log(HY_DECAY_TARGET) / HY_SLOW_PCT
    hi = math.log(HY_DECAY_TARGET) / HY_FAST_PCT
    deltas = np.abs(np.linspace(lo, hi, D_MODEL, dtype=np.float32))
    decay = np.exp(-t * deltas).astype(np.float32)
    return jnp.asarray(emb), jnp.asarray(decay)


def _filter_kernel(emb_ref, w1_ref, b1_ref, w2_ref, b2_ref, fr_ref, w3f_ref, w3b_ref, dec_ref,
                   fc_ref, fs_ref, kr_ref, q_ref, krn_ref, hd_ref):
    @pl.when(pl.program_id(0) == 0)
    def _():
        fr = fr_ref[...]
        h1 = jnp.sin(fr * (jnp.dot(emb_ref[...], w1_ref[...], precision=HIGHEST,
                                   preferred_element_type=F32) + b1_ref[...]))
        hd_ref[...] = jnp.sin(fr * (jnp.dot(h1, w2_ref[...], precision=HIGHEST,
                                            preferred_element_type=F32) + b2_ref[...]))

    hd = hd_ref[...]
    dec = dec_ref[...]
    f = jnp.dot(hd, w3f_ref[...], precision=HIGHEST, preferred_element_type=F32) * dec
    g = jnp.dot(hd, w3b_ref[...], precision=HIGHEST, preferred_element_type=F32) * dec
    row = lax.broadcasted_iota(jnp.int32, f.shape, 0)
    g = jnp.where(row == 0, 0.0, g)
    s = f + g
    d = f - g
    kr = jnp.dot(fc_ref[...], s.astype(BF16), preferred_element_type=F32)
    qq = jnp.dot(fs_ref[...], d.astype(BF16), preferred_element_type=F32)
    alt = jnp.where(row % 2 == 0, 1.0, -1.0)
    nyq = jnp.sum(alt * s, axis=0, keepdims=True)
    kr_ref[...] = kr
    q_ref[...] = jnp.where(row == 0, 0.0, qq)
    krn_ref[...] = jnp.where(row == 0, nyq, kr)


def _hyena_filter_spectrum(L, w1, b1, w2, b2, w3, freq, fc, fs, cblk=256):
    emb, decay = _filter_consts(L)
    ncb = D_MODEL // cblk
    n_emb = 128
    emb = jnp.pad(emb, ((0, 0), (0, n_emb - emb.shape[1])))
    w1 = jnp.pad(w1, ((0, n_emb - w1.shape[0]), (0, 0)))
    full = lambda shape: pl.BlockSpec(shape, lambda j: tuple(0 for _ in shape))
    out_sds = jax.ShapeDtypeStruct((L, D_MODEL), F32)
    out_spec = pl.BlockSpec((L, cblk), lambda j: (0, j))
    return pl.pallas_call(
        _filter_kernel,
        grid=(ncb,),
        in_specs=[
            full((L, n_emb)), full((n_emb, HY_FFN)), full((1, HY_FFN)), full((HY_FFN, HY_FFN)),
            full((1, HY_FFN)), full((1, HY_FFN)),
            pl.BlockSpec((HY_FFN, cblk), lambda j: (0, j)),
            pl.BlockSpec((HY_FFN, cblk), lambda j: (0, ncb + j)),
            pl.BlockSpec((L, cblk), lambda j: (0, j)),
            full((L, L)), full((L, L)),
        ],
        out_specs=(out_spec, out_spec, out_spec),
        out_shape=(out_sds, out_sds, out_sds),
        scratch_shapes=[pltpu.VMEM((L, HY_FFN), F32)],
        compiler_params=_params("arbitrary"),
        name=f"hyena_filter_{L}",
    )(emb, w1, b1.reshape(1, HY_FFN), w2, b2.reshape(1, HY_FFN), freq.reshape(1, HY_FFN), w3, w3, decay, fc, fs)


def _hyena_conv_kernel(x0_ref, x1_ref, v_ref, cw0_ref, cw1_ref, cwv_ref, cb0_ref, cb1_ref, cbv_ref,
                       kr_ref, q_ref, krn_ref, ds_ref, fc_ref, fs_ref, gc_ref, gs_ref, o_ref,
                       zz_ref, gate_ref, skip_ref, yr_ref, yw_ref):
    L = fc_ref.shape[0]
    n_seq = x0_ref.shape[0] // L
    row = lax.broadcasted_iota(jnp.int32, (L, x0_ref.shape[1]), 0)

    def gating(s):
        rows = slice(s * L, (s + 1) * L)

        def short_conv(u_ref, w_ref, b_ref):
            u = u_ref[rows, :].astype(F32)
            w = w_ref[...]
            prev = jnp.where(row == 0, 0.0, pltpu.roll(u, 1, axis=0))
            nxt = jnp.where(row == L - 1, 0.0, pltpu.roll(u, L - 1, axis=0))
            return prev * w[0:1, :] + u * w[1:2, :] + nxt * w[2:3, :] + b_ref[...]

        x0 = short_conv(x0_ref, cw0_ref, cb0_ref)
        zz = short_conv(v_ref, cwv_ref, cbv_ref) * short_conv(x1_ref, cw1_ref, cb1_ref)
        zz_ref[s] = zz.astype(BF16)
        gate_ref[s] = x0
        skip_ref[s] = x0 * zz * ds_ref[...]

    def spectrum(s):
        ur = jnp.dot(fc_ref[...], zz_ref[s], preferred_element_type=F32)
        p = jnp.dot(fs_ref[...], zz_ref[s], preferred_element_type=F32)
        qq = q_ref[...]
        yr_ref[s] = (ur * kr_ref[...] - p * qq).astype(BF16)
        yw_ref[s] = (ur * qq + p * krn_ref[...]).astype(BF16)

    def synthesis(s):
        y = jnp.dot(gc_ref[...], yr_ref[s], preferred_element_type=F32)
        y = y + jnp.dot(gs_ref[...], yw_ref[s], preferred_element_type=F32)
        o_ref[s * L:(s + 1) * L, :] = (gate_ref[s] * y + skip_ref[s]).astype(o_ref.dtype)

    for t in range(n_seq + 2):
        if t < n_seq:
            gating(t)
        if 0 <= t - 1 < n_seq:
            spectrum(t - 1)
        if 0 <= t - 2 < n_seq:
            synthesis(t - 2)


def _hyena_conv(u, conv_w, conv_b, dskip, spectrum, tables, *, latent):
    L = LATENT_LEN if latent else PROMPT_LEN
    n_seq = N_LATENT_SEQ if latent else N_PROMPT_SEQ
    cblk = 256 if latent else 512
    ncb = D_MODEL // cblk
    seqs = 1 if latent else 8
    row0 = (N_PROMPT_TOK // L) if latent else 0
    kr, qq, krn = spectrum
    fc, fs, gc, gs = tables

    def part(p, rows):
        if rows != L:
            return pl.BlockSpec((rows, cblk), lambda j, s: (0, p * ncb + j))
        return pl.BlockSpec((seqs * L, cblk), lambda j, s: (row0 // seqs + s, p * ncb + j))

    def const_cols(rows):
        return pl.BlockSpec((rows, cblk), lambda j, s: (0, j))

    mat = pl.BlockSpec((L, L), lambda j, s: (0, 0))
    conv_b2 = conv_b.reshape(1, 3 * D_MODEL)
    in_specs = [part(0, L), part(1, L), part(2, L),
                part(0, 3), part(1, 3), part(2, 3),
                part(0, 1), part(1, 1), part(2, 1),
                const_cols(L), const_cols(L), const_cols(L), const_cols(1),
                mat, mat, mat, mat]
    args = [u, u, u, conv_w, conv_w, conv_w, conv_b2, conv_b2, conv_b2,
            kr, qq, krn, dskip.reshape(1, D_MODEL), fc, fs, gc, gs]
    return pl.pallas_call(
        _hyena_conv_kernel,
        grid=(ncb, n_seq // seqs),
        in_specs=in_specs,
        out_specs=pl.BlockSpec((seqs * L, cblk), lambda j, s: (s, j)),
        out_shape=jax.ShapeDtypeStruct((n_seq * L, D_MODEL), BF16),
        scratch_shapes=[pltpu.VMEM((seqs, L, cblk), BF16), pltpu.VMEM((seqs, L, cblk), F32),
                        pltpu.VMEM((seqs, L, cblk), F32), pltpu.VMEM((seqs, L, cblk), BF16),
                        pltpu.VMEM((seqs, L, cblk), BF16)],
        compiler_params=_params("arbitrary", "arbitrary"),
        name="hyena_conv_latent" if latent else "hyena_conv_prompt",
    )(*args)


def kernel(x_prompt, x_sample, cache_k, cache_v, state_hgrn, c, c_ctx, norm_g, mod_w, mod_b, ab_in_w, hgrn_lb, hgrn_onorm_g, attn_qnorm_g, attn_knorm_g, ab_out_w, hy_in_w, hy_in_b, hy_conv_w, hy_conv_b, hy_f_w1, hy_f_b1, hy_f_w2, hy_f_b2, hy_f_w3, hy_f_freq, hy_dskip, hy_out_w, router_w, router_b, moe_w_gate, moe_w_up, moe_w_down):
    xp = x_prompt.reshape(N_PROMPT_TOK, D_MODEL)
    xl = x_sample.reshape(N_LATENT_TOK, D_MODEL)
    cond = jnp.concatenate([c_ctx[None, :], c, jnp.zeros((N_COND - 1 - N_LATENT_SEQ, D_MODEL), F32)], axis=0)
    mod = _modulation(cond, mod_w, mod_b)
    router_wp = jnp.pad(router_w, ((0, 0), (0, ROUTER_LANES - N_EXPERTS)))

    z = _in_proj0(xp, xl, norm_g[0, 0], mod[0], ab_in_w[0])
    oa_p, new_state = _hgrn(z, hgrn_lb, hgrn_onorm_g[0], None, latent=False)
    oa_l = _hgrn(z, hgrn_lb, hgrn_onorm_g[0], state_hgrn, latent=True)
    ob_p, k_fm, v_fm = _attention_prompt(z, attn_qnorm_g[0], attn_knorm_g[0])
    fm_shape = (N_PROMPT_SEQ, 1, KV_HEADS, HEAD_DIM, PROMPT_LEN)
    new_k = jnp.swapaxes(k_fm.reshape(fm_shape), -1, -2)
    new_v = jnp.swapaxes(v_fm.reshape(fm_shape), -1, -2)
    ob_l = _attention_latent(z, attn_qnorm_g[0], attn_knorm_g[0], cache_k, cache_v)
    x, h, logits_t = _out_proj([(oa_p, oa_l), (ob_p, ob_l)], ab_out_w[0], (xp, xl), norm_g[0, 1], mod[0],
                               router_wp)
    moe_out = _moe(h, logits_t, router_b, moe_w_gate, moe_w_up, moe_w_down, 0)

    x, u = _in_proj1(x, moe_out, mod[0], norm_g[1, 0], mod[1], hy_in_w[0], hy_in_b[0])
    pre = []
    for latent in (False, True):
        L = LATENT_LEN if latent else PROMPT_LEN
        tables = _dft_tables(L)
        spectrum = _hyena_filter_spectrum(L, hy_f_w1[0], hy_f_b1[0], hy_f_w2[0], hy_f_b2[0], hy_f_w3[0],
                                          hy_f_freq[0], tables[0], tables[1])
        pre.append(_hyena_conv(u, hy_conv_w[0], hy_conv_b[0], hy_dskip[0], spectrum, tables, latent=latent))
    x, h, logits_t = _out_proj([tuple(pre)], hy_out_w[0], (x,), norm_g[1, 1], mod[1], router_wp)
    moe_out = _moe(h, logits_t, router_b, moe_w_gate, moe_w_up, moe_w_down, 1)

    y_prompt = _combine(x, moe_out, mod[1], 0, N_PROMPT_TOK).reshape(N_PROMPT_SEQ, PROMPT_LEN, D_MODEL)
    y_sample = _combine(x, moe_out, mod[1], N_PROMPT_TOK, N_LATENT_TOK).reshape(N_LATENT_SEQ, LATENT_LEN, D_MODEL)
    return (y_prompt, y_sample, new_k, new_v, new_state)
```

```python
import functools
import math

import numpy as np
import jax
import jax.numpy as jnp
from jax import lax
from jax.experimental import pallas as pl
from jax.experimental.pallas import tpu as pltpu
from jax.experimental.pallas import tpu_sc as plsc

F32 = jnp.float32
BF16 = jnp.bfloat16
HIGHEST = lax.Precision.HIGHEST

D_MODEL = 1024
N_PROMPT_SEQ = 32
PROMPT_LEN = 256
N_LATENT_SEQ = 2
LATENT_LEN = 1024
PAST_LEN = 512
GRID_W = 64
N_PROMPT_TOK = N_PROMPT_SEQ * PROMPT_LEN
N_LATENT_TOK = N_LATENT_SEQ * LATENT_LEN
N_TOK = N_PROMPT_TOK + N_LATENT_TOK
N_COND = 8
EPS = 1e-6

A_WIDTH = 512
A_HEADS = 4
A_DK = 128
CHUNK = 64
HGRN_BLOCK = 128
HGRN_HEADS_PER_STEP = 4
HEAD_DIM = 64
Q_HEADS = 8
KV_HEADS = 2
Q_PER_KV = Q_HEADS // KV_HEADS
Q_BLOCK = 256
ROPE_THETA = 10000.0
ROPE_PAIRS = HEAD_DIM // 4
AB_IN = 5 * A_WIDTH + (Q_HEADS + 2 * KV_HEADS) * HEAD_DIM

HY_BANDS = 16
HY_FFN = 64
HY_DECAY_TARGET = 1e-2
HY_FAST_PCT = 0.3
HY_SLOW_PCT = 1.5

N_EXPERTS = 16
N_GROUPS = 4
EXPERTS_PER_GROUP = 4
TOP_K = 2
D_EXPERT = 512
ROUTER_LANES = 128
OUT_PROJ_SUB_ROWS = 256
EXPERT_SUB_ROWS = 256
IN_PROJ_SUB_ROWS = 256
MOE_TILE = 512
MOE_ROWS = N_TOK * TOP_K + N_EXPERTS * MOE_TILE
PLAN_LANES = 128

SC_CORES = 2
SC_WORKERS = 32
SC_TOKENS_PER_WORKER = N_TOK // SC_WORKERS
SC_CHUNK = 40
ROW_WORDS = D_MODEL // 2

VMEM_LIMIT = 56 * 1024 * 1024


def _params(*sem):
    return pltpu.CompilerParams(dimension_semantics=sem, vmem_limit_bytes=VMEM_LIMIT)


def _pack_rows(x):
    n = x.shape[1] // 2
    bits = pltpu.bitcast(x.astype(BF16).astype(F32), jnp.uint32)
    return pltpu.bitcast(bits[:, :n] | (bits[:, n:] >> 16), jnp.int32)


def _unpack_rows(p):
    bits = pltpu.bitcast(p, jnp.uint32)
    hi = pltpu.bitcast(bits & jnp.uint32(0xFFFF0000), F32)
    lo = pltpu.bitcast(bits << 16, F32)
    return jnp.concatenate([hi, lo], axis=1)


def _cond_of_token_block(i, block_rows):
    start = i * block_rows
    return jnp.where(start < N_PROMPT_TOK, 0, 1 + (start - N_PROMPT_TOK) // LATENT_LEN)


def _mod_kernel(cond_ref, w_ref, b_ref, o_ref):
    cnd = cond_ref[...]
    s = cnd * jax.nn.sigmoid(cnd)
    s_hi = s.astype(BF16)
    s_lo = (s - s_hi.astype(F32)).astype(BF16)
    w = w_ref[...]
    w_hi = w.astype(BF16)
    w_lo = (w - w_hi.astype(F32)).astype(BF16)
    acc = jnp.dot(s_hi, w_hi, preferred_element_type=F32)
    acc = acc + jnp.dot(s_lo, w_hi, preferred_element_type=F32)
    acc = acc + jnp.dot(s_hi, w_lo, preferred_element_type=F32)
    o_ref[...] = acc + b_ref[...]


def _modulation(cond, mod_w, mod_b):
    depth = mod_w.shape[0]
    n_chunk = 6
    out = pl.pallas_call(
        _mod_kernel,
        grid=(depth, n_chunk),
        in_specs=[
            pl.BlockSpec((N_COND, D_MODEL), lambda l, j: (0, 0)),
            pl.BlockSpec((None, D_MODEL, D_MODEL), lambda l, j: (l, 0, j)),
            pl.BlockSpec((None, 1, D_MODEL), lambda l, j: (l, 0, j)),
        ],
        out_specs=pl.BlockSpec((None, N_COND, D_MODEL), lambda l, j: (l, 0, j)),
        out_shape=jax.ShapeDtypeStruct((depth, N_COND, n_chunk * D_MODEL), F32),
        compiler_params=_params("arbitrary", "arbitrary"),
        name="modulation",
    )(cond, mod_w, mod_b.reshape(depth, 1, n_chunk * D_MODEL))
    return out.reshape(depth, N_COND, n_chunk, D_MODEL)


def _modulated_norm(x, g, mod, shift_row, scale_row):
    ms = jnp.mean(x * x, axis=-1, keepdims=True)
    y = x * lax.rsqrt(ms + EPS) * g
    return y * (1.0 + mod[scale_row:scale_row + 1, :]) + mod[shift_row:shift_row + 1, :]


def _trunk_specs(block_rows, width):
    n_prompt_blocks = N_PROMPT_TOK // block_rows
    return (pl.BlockSpec((block_rows, width), lambda i: (jnp.minimum(i, n_prompt_blocks - 1), 0)),
            pl.BlockSpec((block_rows, width), lambda i: (jnp.maximum(i - n_prompt_blocks, 0), 0)))


def _select_trunk(p_ref, l_ref, rows=slice(None)):
    block_rows = p_ref.shape[0]
    return jnp.where(pl.program_id(0) < N_PROMPT_TOK // block_rows, p_ref[rows, :], l_ref[rows, :])


def _cast_once(w_ref, wb_ref):
    @pl.when(pl.program_id(0) == 0)
    def _():
        wb_ref[...] = w_ref[...].astype(BF16)


def _resident(shape):
    return pl.BlockSpec(shape, lambda i: tuple(0 for _ in shape), pipeline_mode=pl.Buffered(1))


def _mod_spec(block_rows):
    return pl.BlockSpec((None, 6, D_MODEL), lambda i: (_cond_of_token_block(i, block_rows), 0, 0))


def _in_proj0_kernel(xp_ref, xl_ref, g_ref, mod_ref, w_ref, o_ref, wb_ref, hb_ref):
    _cast_once(w_ref, wb_ref)
    n = IN_PROJ_SUB_ROWS
    n_sub = xp_ref.shape[0] // n

    def prepare(r):
        x = _select_trunk(xp_ref, xl_ref, slice(r * n, (r + 1) * n))
        hb_ref[r] = _modulated_norm(x, g_ref[...], mod_ref[...], 0, 1).astype(BF16)

    def project(r):
        u = jnp.dot(hb_ref[r], wb_ref[...], preferred_element_type=F32)
        o_ref[r * n:(r + 1) * n, :] = u.astype(o_ref.dtype)

    prepare(0)
    for r in range(1, n_sub):
        prepare(r)
        project(r - 1)
    project(n_sub - 1)


def _in_proj0(x_prompt, x_latent, g, mod_l, w, block_rows=512):
    n = w.shape[1]
    return pl.pallas_call(
        _in_proj0_kernel,
        grid=(N_TOK // block_rows,),
        in_specs=[*_trunk_specs(block_rows, D_MODEL), _resident((1, D_MODEL)), _mod_spec(block_rows),
                  _resident((D_MODEL, n))],
        out_specs=pl.BlockSpec((block_rows, n), lambda i: (i, 0)),
        out_shape=jax.ShapeDtypeStruct((N_TOK, n), BF16),
        scratch_shapes=[pltpu.VMEM((D_MODEL, n), BF16),
                        pltpu.VMEM((block_rows // IN_PROJ_SUB_ROWS, IN_PROJ_SUB_ROWS, D_MODEL), BF16)],
        compiler_params=_params("arbitrary"),
        name="in_proj0",
    )(x_prompt, x_latent, g.reshape(1, D_MODEL), mod_l, w)


def _moe_mix(x_ref, ya_ref, yb_ref, wt_ref, mod_ref, rows=slice(None)):
    wt = wt_ref[rows, :]
    mix = wt[:, 0:1] * _unpack_rows(ya_ref[rows, :]) + wt[:, 1:2] * _unpack_rows(yb_ref[rows, :])
    return x_ref[rows, :] + mod_ref[5:6, :] * mix


def _in_proj1_kernel(x_ref, ya_ref, yb_ref, wt_ref, modp_ref, g_ref, mod_ref, w_ref, b_ref, xo_ref, o_ref,
                     wb_ref, hb_ref):
    _cast_once(w_ref, wb_ref)
    n = IN_PROJ_SUB_ROWS
    n_sub = x_ref.shape[0] // n

    def prepare(r):
        rows = slice(r * n, (r + 1) * n)
        x = _moe_mix(x_ref, ya_ref, yb_ref, wt_ref, modp_ref, rows)
        xo_ref[rows, :] = x
        hb_ref[r] = _modulated_norm(x, g_ref[...], mod_ref[...], 0, 1).astype(BF16)

    def project(r):
        rows = slice(r * n, (r + 1) * n)
        u = jnp.dot(hb_ref[r], wb_ref[...], preferred_element_type=F32) + b_ref[...]
        o_ref[rows, :] = u.astype(o_ref.dtype)

    prepare(0)
    for r in range(1, n_sub):
        prepare(r)
        project(r - 1)
    project(n_sub - 1)


def _in_proj1(x, moe_out, mod_prev, g, mod_l, w, bias, block_rows=512):
    ya, yb, w_tok = moe_out
    n = w.shape[1]
    tok = pl.BlockSpec((block_rows, D_MODEL), lambda i: (i, 0))
    packed = pl.BlockSpec((block_rows, ROW_WORDS), lambda i: (i, 0))
    return pl.pallas_call(
        _in_proj1_kernel,
        grid=(N_TOK // block_rows,),
        in_specs=[tok, packed, packed, pl.BlockSpec((block_rows, TOP_K), lambda i: (i, 0)), _mod_spec(block_rows),
                  _resident((1, D_MODEL)), _mod_spec(block_rows), _resident((D_MODEL, n)), _resident((1, n))],
        out_specs=(tok, pl.BlockSpec((block_rows, n), lambda i: (i, 0))),
        out_shape=(jax.ShapeDtypeStruct((N_TOK, D_MODEL), F32), jax.ShapeDtypeStruct((N_TOK, n), BF16)),
        scratch_shapes=[pltpu.VMEM((D_MODEL, n), BF16),
                        pltpu.VMEM((block_rows // IN_PROJ_SUB_ROWS, IN_PROJ_SUB_ROWS, D_MODEL), BF16)],
        compiler_params=_params("arbitrary"),
        name="in_proj1",
    )(x, ya, yb, w_tok, mod_prev, g.reshape(1, D_MODEL), mod_l, w, bias.reshape(1, n))


def _hgrn_kernel(*refs, seq_len, with_state):
    if with_state:
        (q_ref, zf_ref, zb_ref, i_ref, ga_ref, lb_ref, og_ref, s0_ref, o_ref, of_ref, ob_ref) = refs
    else:
        (q_ref, zf_ref, zb_ref, i_ref, ga_ref, lb_ref, og_ref, o_ref, s_ref, of_ref, ob_ref) = refs
    n_blocks = seq_len // HGRN_BLOCK
    chunks_per_block = HGRN_BLOCK // CHUNK

    lbr = lb_ref[...]
    mx = jnp.maximum(lbr[0], lbr[1])
    e0 = jnp.exp(lbr[0] - mx)
    e1 = jnp.exp(lbr[1] - mx)
    lb = e0 / (e0 + e1)

    row = lax.broadcasted_iota(jnp.int32, (HGRN_BLOCK, HGRN_BLOCK), 0)
    col = lax.broadcasted_iota(jnp.int32, (HGRN_BLOCK, HGRN_BLOCK), 1)
    same_chunk = (row // CHUNK) == (col // CHUNK)
    nt = (((1,), (1,)), ((), ()))
    tn = (((0,), (0,)), ((), ()))

    def per_chunk_row(x, idx):
        return jnp.concatenate(
            [jnp.broadcast_to(x[n * CHUNK + idx:n * CHUNK + idx + 1, :], (CHUNK, x.shape[1]))
             for n in range(chunks_per_block)], axis=0)

    def in_chunk_cumsum(tri, x):
        hi = x.astype(BF16)
        lo = (x - hi.astype(F32)).astype(BF16)
        return jnp.dot(tri, hi, preferred_element_type=F32) + jnp.dot(tri, lo, preferred_element_type=F32)

    def prepare(blk, cols, z_ref, lbd, forward):
        rows = slice(blk * HGRN_BLOCK, (blk + 1) * HGRN_BLOCK)
        keep = (same_chunk & (col <= row)) if forward else (same_chunk & (col >= row))
        tri = jnp.where(keep, 1.0, 0.0).astype(BF16)
        mid = CHUNK // 2 if forward else CHUNK - 1 - CHUNK // 2
        last = CHUNK - 1 if forward else 0
        f = lbd + (1.0 - lbd) * jax.nn.sigmoid(z_ref[rows, cols].astype(F32))
        lf = jnp.log(f)
        k = 1.0 - f
        q = q_ref[rows, cols].astype(F32)
        b = in_chunk_cumsum(tri, lf)
        bm = per_chunk_row(b, mid)
        bl = per_chunk_row(b, last)
        return dict(
            rows=rows, cols=cols, keep=keep, forward=forward,
            vb=i_ref[rows, cols].astype(BF16),
            qe=(q * jnp.exp(b - bm)).astype(BF16), ke=(k * jnp.exp(bm - b)).astype(BF16),
            qb=(q * jnp.exp(b)).astype(BF16), ks=(k * jnp.exp(bl - b)).astype(BF16), decay=jnp.exp(bl))

    def within_chunks(u):
        att = lax.dot_general(u["qe"], u["ke"], nt, preferred_element_type=F32)
        att = jnp.where(u["keep"], att, 0.0)
        u["o_intra"] = jnp.dot(att.astype(BF16), u["vb"], preferred_element_type=F32)
        u["upd"] = [lax.dot_general(u["vb"][n * CHUNK:(n + 1) * CHUNK], u["ks"][n * CHUNK:(n + 1) * CHUNK], tn,
                                    preferred_element_type=F32) for n in range(chunks_per_block)]

    def across_chunks(u, st, out_ref):
        order = range(chunks_per_block) if u["forward"] else range(chunks_per_block - 1, -1, -1)
        o_inter = [None] * chunks_per_block
        for n in order:
            cr = slice(n * CHUNK, (n + 1) * CHUNK)
            o_inter[n] = lax.dot_general(u["qb"][cr], st.astype(BF16), nt, preferred_element_type=F32)
            st = st * u["decay"][n * CHUNK:n * CHUNK + 1, :] + u["upd"][n]
        out_ref[u["rows"], u["cols"]] = u["o_intra"] + jnp.concatenate(o_inter, axis=0)
        return st

    n_heads = q_ref.shape[1] // A_DK
    head_cols = [slice(hd * A_DK, (hd + 1) * A_DK) for hd in range(n_heads)]
    if with_state:
        states = {(hd, d): s0_ref[d, hd].T for hd in range(n_heads) for d in range(2)}
    else:
        states = {(hd, d): jnp.zeros((A_DK, A_DK), F32) for hd in range(n_heads) for d in range(2)}
    for step in range(n_blocks):
        units = {}
        for hd, cols in enumerate(head_cols):
            units[hd, 0] = prepare(step, cols, zf_ref, lb[0:1, cols], True)
            units[hd, 1] = prepare(n_blocks - 1 - step, cols, zb_ref, lb[1:2, cols], False)
        for u in units.values():
            within_chunks(u)
        for key, u in units.items():
            states[key] = across_chunks(u, states[key], of_ref if key[1] == 0 else ob_ref)
    for hd, cols in enumerate(head_cols):
        if not with_state:
            s_ref[0, hd] = states[hd, 0].T
            s_ref[1, hd] = states[hd, 1].T
        o = of_ref[:, cols] + ob_ref[:, cols]
        o = o * lax.rsqrt(jnp.mean(o * o, axis=-1, keepdims=True) + EPS) * og_ref[:, cols]
        ga = ga_ref[:, cols].astype(F32)
        o_ref[:, cols] = (o * (ga * jax.nn.sigmoid(ga))).astype(o_ref.dtype)


def _hgrn(z, hgrn_lb, onorm_g, state, *, latent):
    seq_len = LATENT_LEN if latent else PROMPT_LEN
    n_seq = N_LATENT_SEQ if latent else N_PROMPT_SEQ
    row0 = (N_PROMPT_TOK // seq_len) if latent else 0

    hw = HGRN_HEADS_PER_STEP * A_DK
    n_hg = A_HEADS // HGRN_HEADS_PER_STEP

    def zspec(part):
        return pl.BlockSpec((seq_len, hw), lambda s, h: (row0 + s, part * n_hg + h))

    in_specs = [zspec(0), zspec(1), zspec(2), zspec(3), zspec(4),
                pl.BlockSpec((2, 2, hw), lambda s, h: (0, 0, h)),
                pl.BlockSpec((1, hw), lambda s, h: (0, h))]
    args = [z, z, z, z, z, hgrn_lb, onorm_g.reshape(1, A_WIDTH)]
    state_spec = pl.BlockSpec((None, None, 2, HGRN_HEADS_PER_STEP, A_DK, A_DK), lambda s, h: (s, 0, 0, h, 0, 0))
    o_shape = jax.ShapeDtypeStruct((n_seq * seq_len, A_WIDTH), BF16)
    o_spec = pl.BlockSpec((seq_len, hw), lambda s, h: (s, h))
    if latent:
        in_specs.append(state_spec)
        args.append(state)
        out_shape, out_specs = o_shape, o_spec
    else:
        out_shape = (o_shape, jax.ShapeDtypeStruct((n_seq, 1, 2, A_HEADS, A_DK, A_DK), F32))
        out_specs = (o_spec, state_spec)
    return pl.pallas_call(
        functools.partial(_hgrn_kernel, seq_len=seq_len, with_state=latent),
        grid=(n_seq, n_hg),
        in_specs=in_specs,
        out_specs=out_specs,
        out_shape=out_shape,
        scratch_shapes=[pltpu.VMEM((seq_len, hw), F32), pltpu.VMEM((seq_len, hw), F32)],
        compiler_params=_params("arbitrary", "arbitrary"),
        name="hgrn_latent" if latent else "hgrn_prompt",
    )(*args)


def _rope_tables():
    pos = np.arange(LATENT_LEN)
    row, colp = pos // GRID_W, pos % GRID_W
    inv = ROPE_THETA ** (-np.arange(ROPE_PAIRS, dtype=np.float32) / ROPE_PAIRS)
    inv = inv.astype(np.float32)
    ang_r = (row.astype(np.float32)[:, None] * inv).astype(np.float32)
    ang_c = (colp.astype(np.float32)[:, None] * inv).astype(np.float32)
    cos = np.concatenate([np.cos(ang_r), np.cos(ang_r), np.cos(ang_c), np.cos(ang_c)], axis=1)
    sin = np.concatenate([-np.sin(ang_r), np.sin(ang_r), -np.sin(ang_c), np.sin(ang_c)], axis=1)
    return cos.astype(np.float32), sin.astype(np.float32)


def _head_mean_matrix(width):
    idx = np.arange(width) // HEAD_DIM
    return jnp.asarray((idx[:, None] == idx[None, :]).astype(np.float32) / HEAD_DIM).astype(BF16)


def _attn_kernel(*refs, latent):
    if latent:
        (q_ref, k_ref, v_ref, qg_ref, kg_ref, gq_ref, gk_ref, cosq_ref, sinq_ref, cosk_ref, sink_ref,
         ck_ref, cv_ref, o_ref) = refs
    else:
        (q_ref, k_ref, v_ref, qg_ref, kg_ref, gq_ref, gk_ref, o_ref, kout_ref, vout_ref) = refs
    pair_w = 2 * HEAD_DIM

    def head_norm(x, mean_ref, gain):
        sq = x * x
        hi = sq.astype(BF16)
        lo = (sq - hi.astype(F32)).astype(BF16)
        ms = jnp.dot(hi, mean_ref[...], preferred_element_type=F32)
        ms = ms + jnp.dot(lo, mean_ref[...], preferred_element_type=F32)
        return x * lax.rsqrt(ms + EPS) * gain

    def rope(x, cos, sin):
        n = x.shape[1]
        lane = lax.broadcasted_iota(jnp.int32, x.shape, 1)
        first_of_pair = (lane // ROPE_PAIRS) % 2 == 0
        swapped = jnp.where(first_of_pair, pltpu.roll(x, n - ROPE_PAIRS, axis=1), pltpu.roll(x, ROPE_PAIRS, axis=1))
        return x * cos + swapped * sin

    nt = (((1,), (1,)), ((), ()))

    def prepare(rows, seq_idx):
        q = head_norm(q_ref[rows, :].astype(F32), gq_ref, qg_ref[...])
        k = head_norm(k_ref[rows, :].astype(F32), gk_ref, kg_ref[...])
        if latent:
            q = rope(q, cosq_ref[...], sinq_ref[...])
            k = rope(k, cosk_ref[...], sink_ref[...])
        q = q * (HEAD_DIM ** -0.5)
        v = v_ref[rows, :].astype(F32)
        n_q = q.shape[0]
        low_kv = lax.broadcasted_iota(jnp.int32, k.shape, 1) < HEAD_DIM
        low_q = lax.broadcasted_iota(jnp.int32, (n_q, pair_w), 1) < HEAD_DIM
        k_swapped = pltpu.roll(k, HEAD_DIM, axis=1)
        v_swapped = pltpu.roll(v, HEAD_DIM, axis=1)
        if not latent:
            kout_ref[seq_idx] = k.T
            vout_ref[seq_idx] = v.T
        units = []
        for j in range(KV_HEADS):
            kd = (jnp.where(low_kv, k, k_swapped) if j == 0 else jnp.where(low_kv, k_swapped, k)).astype(BF16)
            vd = (jnp.where(low_kv, v, v_swapped) if j == 0 else jnp.where(low_kv, v_swapped, v)).astype(BF16)
            vd = jnp.concatenate([vd, jnp.ones_like(vd)], axis=1)
            tiles = range(j * Q_PER_KV // 2, (j + 1) * Q_PER_KV // 2)
            parts = []
            for t in tiles:
                qt = q[:, t * pair_w:(t + 1) * pair_w]
                parts += [jnp.where(low_q, qt, 0.0), jnp.where(low_q, 0.0, qt)]
            units.append(dict(j=j, rows=rows, tiles=tiles, n_q=n_q, low_q=low_q, kd=kd, vd=vd,
                              qs=jnp.concatenate(parts, axis=0).astype(BF16)))
        return units

    def scores(u):
        u["s_new"] = lax.dot_general(u["qs"], u["kd"], nt, preferred_element_type=F32)
        if latent:
            j = u["j"]
            cvd = jnp.concatenate([cv_ref[j], cv_ref[j]], axis=1).astype(BF16)
            u["cvd"] = jnp.concatenate([cvd, jnp.ones_like(cvd)], axis=1)
            ckd = jnp.concatenate([ck_ref[j], ck_ref[j]], axis=1).astype(BF16)
            u["s_old"] = lax.dot_general(u["qs"], ckd, nt, preferred_element_type=F32)

    def softmax(u):
        m = jnp.max(u["s_new"], axis=-1, keepdims=True)
        if latent:
            m = jnp.maximum(m, jnp.max(u["s_old"], axis=-1, keepdims=True))
        u["p_new"] = jnp.exp(u.pop("s_new") - m).astype(BF16)
        if latent:
            u["p_old"] = jnp.exp(u.pop("s_old") - m).astype(BF16)

    def weighted_values(u):
        acc = jnp.dot(u["p_new"], u["vd"], preferred_element_type=F32)
        if latent:
            acc = acc + jnp.dot(u["p_old"], u["cvd"], preferred_element_type=F32)
        out = acc[:, :pair_w] / acc[:, pair_w:]
        n_q = u["n_q"]
        for i, t in enumerate(u["tiles"]):
            lo_head = out[(2 * i) * n_q:(2 * i + 1) * n_q, :]
            hi_head = out[(2 * i + 1) * n_q:(2 * i + 2) * n_q, :]
            o_ref[u["rows"], t * pair_w:(t + 1) * pair_w] = jnp.where(u["low_q"], lo_head, hi_head).astype(o_ref.dtype)

    if latent:
        units = prepare(slice(None), None)
    else:
        seq = PROMPT_LEN
        units = [u for s in range(q_ref.shape[0] // seq) for u in prepare(slice(s * seq, (s + 1) * seq), s)]
    for phase in (scores, softmax, weighted_values):
        for u in units:
            phase(u)


def _attn_common_args(qn_g, kn_g):
    q_w, kv_w = Q_HEADS * HEAD_DIM, KV_HEADS * HEAD_DIM
    return (jnp.tile(qn_g, Q_HEADS).reshape(1, q_w), jnp.tile(kn_g, KV_HEADS).reshape(1, kv_w),
            _head_mean_matrix(q_w), _head_mean_matrix(kv_w))


def _attention_prompt(z, qn_g, kn_g):
    seqs = 4
    L = seqs * PROMPT_LEN
    cache_shape = jax.ShapeDtypeStruct((N_PROMPT_SEQ, KV_HEADS * HEAD_DIM, PROMPT_LEN), F32)
    cache_spec = pl.BlockSpec((seqs, KV_HEADS * HEAD_DIM, PROMPT_LEN), lambda s: (s, 0, 0))
    q_w, kv_w = Q_HEADS * HEAD_DIM, KV_HEADS * HEAD_DIM
    q_col = (5 * A_WIDTH) // q_w
    k_col = (5 * A_WIDTH + q_w) // kv_w
    const = lambda r, c: pl.BlockSpec((r, c), lambda s: (0, 0))
    return pl.pallas_call(
        functools.partial(_attn_kernel, latent=False),
        grid=(N_PROMPT_TOK // L,),
        in_specs=[
            pl.BlockSpec((L, q_w), lambda s: (s, q_col)),
            pl.BlockSpec((L, kv_w), lambda s: (s, k_col)),
            pl.BlockSpec((L, kv_w), lambda s: (s, k_col + 1)),
            const(1, q_w), const(1, kv_w), const(q_w, q_w), const(kv_w, kv_w),
        ],
        out_specs=(pl.BlockSpec((L, q_w), lambda s: (s, 0)), cache_spec, cache_spec),
        out_shape=(jax.ShapeDtypeStruct((N_PROMPT_TOK, q_w), BF16), cache_shape, cache_shape),
        compiler_params=_params("arbitrary"),
        name="attn_prompt",
    )(z, z, z, *_attn_common_args(qn_g, kn_g))


def _attention_latent(z, qn_g, kn_g, cache_k, cache_v):
    L = LATENT_LEN
    nqb = L // Q_BLOCK
    q_w, kv_w = Q_HEADS * HEAD_DIM, KV_HEADS * HEAD_DIM
    q_col = (5 * A_WIDTH) // q_w
    k_col = (5 * A_WIDTH + q_w) // kv_w
    qrow0 = N_PROMPT_TOK // Q_BLOCK
    krow0 = N_PROMPT_TOK // L
    cos, sin = _rope_tables()
    cos_q, sin_q = jnp.asarray(np.tile(cos, (1, Q_HEADS))), jnp.asarray(np.tile(sin, (1, Q_HEADS)))
    cos_k, sin_k = jnp.asarray(np.tile(cos, (1, KV_HEADS))), jnp.asarray(np.tile(sin, (1, KV_HEADS)))
    const = lambda r, c: pl.BlockSpec((r, c), lambda s, b: (0, 0))
    cache_spec = pl.BlockSpec((None, None, KV_HEADS, PAST_LEN, HEAD_DIM), lambda s, b: (s, 0, 0, 0, 0))
    return pl.pallas_call(
        functools.partial(_attn_kernel, latent=True),
        grid=(N_LATENT_SEQ, nqb),
        in_specs=[
            pl.BlockSpec((Q_BLOCK, q_w), lambda s, b: (qrow0 + s * nqb + b, q_col)),
            pl.BlockSpec((L, kv_w), lambda s, b: (krow0 + s, k_col)),
            pl.BlockSpec((L, kv_w), lambda s, b: (krow0 + s, k_col + 1)),
            const(1, q_w), const(1, kv_w), const(q_w, q_w), const(kv_w, kv_w),
            pl.BlockSpec((Q_BLOCK, q_w), lambda s, b: (b, 0)),
            pl.BlockSpec((Q_BLOCK, q_w), lambda s, b: (b, 0)),
            const(L, kv_w), const(L, kv_w),
            cache_spec, cache_spec,
        ],
        out_specs=pl.BlockSpec((Q_BLOCK, q_w), lambda s, b: (s * nqb + b, 0)),
        out_shape=jax.ShapeDtypeStruct((N_LATENT_TOK, q_w), BF16),
        compiler_params=_params("arbitrary", "arbitrary"),
        name="attn_latent",
    )(z, z, z, *_attn_common_args(qn_g, kn_g), cos_q, sin_q, cos_k, sin_k, cache_k, cache_v)


def _out_proj_kernel(*refs, n_act, n_x):
    a_refs = refs[:2 * n_act]
    x_refs = refs[2 * n_act:2 * n_act + n_x]
    g_ref, mod_ref, rw_ref, w_ref, xo_ref, h_ref, lg_ref, wb_ref, rws_ref, acc_ref = refs[2 * n_act + n_x:]
    _cast_once(w_ref, wb_ref)

    @pl.when(pl.program_id(0) == 0)
    def _():
        rw = rw_ref[...]
        hi = rw.astype(BF16).astype(F32)
        lo = (rw - hi).astype(BF16).astype(F32)
        rws_ref[...] = (hi + pltpu.roll(lo, N_EXPERTS, axis=1)).astype(BF16)

    mod = mod_ref[...]
    n = OUT_PROJ_SUB_ROWS

    n_sub = xo_ref.shape[0] // n

    def sub_rows(r):
        if isinstance(r, int):
            return slice(r * n, (r + 1) * n)
        return pl.ds(pl.multiple_of(r * n, n), n)

    def project(r):
        rows = sub_rows(r)
        acc = None
        k0 = 0
        for ap_ref, al_ref in zip(a_refs[0::2], a_refs[1::2]):
            k1 = k0 + ap_ref.shape[1]
            part = jnp.dot(_select_trunk(ap_ref, al_ref, rows), wb_ref[k0:k1, :], preferred_element_type=F32)
            acc = part if acc is None else acc + part
            k0 = k1
        acc_ref[r % 2] = acc

    def finish(r):
        rows = sub_rows(r)
        x_in = x_refs[0][rows, :] if n_x == 1 else _select_trunk(*x_refs, rows)
        x = x_in + mod[2:3, :] * acc_ref[r % 2]
        xo_ref[rows, :] = x
        h = _modulated_norm(x, g_ref[...], mod, 3, 4)
        h_ref[rows, :] = _pack_rows(h)
        h_hi = h.astype(BF16)
        h_lo = (h - h_hi.astype(F32)).astype(BF16)
        both = jnp.dot(jnp.concatenate([h_hi, h_lo], axis=0), rws_ref[...], preferred_element_type=F32)
        from_hi, from_lo = both[:n], both[n:]
        lg = from_hi + pltpu.roll(from_hi, ROUTER_LANES - N_EXPERTS, axis=1) + from_lo
        lg_ref[:, rows] = lg.T[:N_EXPERTS, :]

    project(0)
    for r in range(n_sub - 1):
        project(r + 1)
        finish(r)
    finish(n_sub - 1)


def _out_proj(acts, w, xs, g, mod_l, router_wp, block_rows=1024):
    tok = lambda width: pl.BlockSpec((block_rows, width), lambda i: (i, 0))
    in_specs = [spec for ap, _ in acts for spec in _trunk_specs(block_rows, ap.shape[1])]
    in_specs += [tok(D_MODEL)] if len(xs) == 1 else list(_trunk_specs(block_rows, D_MODEL))
    in_specs += [_resident((1, D_MODEL)), _mod_spec(block_rows), _resident((D_MODEL, ROUTER_LANES)),
                 _resident(w.shape)]
    return pl.pallas_call(
        functools.partial(_out_proj_kernel, n_act=len(acts), n_x=len(xs)),
        grid=(N_TOK // block_rows,),
        in_specs=in_specs,
        out_specs=(tok(D_MODEL), tok(ROW_WORDS), pl.BlockSpec((N_EXPERTS, block_rows), lambda i: (0, i))),
        out_shape=(jax.ShapeDtypeStruct((N_TOK, D_MODEL), F32),
                   jax.ShapeDtypeStruct((N_TOK, ROW_WORDS), jnp.int32),
                   jax.ShapeDtypeStruct((N_EXPERTS, N_TOK), F32)),
        scratch_shapes=[pltpu.VMEM(w.shape, BF16), pltpu.VMEM((D_MODEL, ROUTER_LANES), BF16),
                        pltpu.VMEM((2, OUT_PROJ_SUB_ROWS, D_MODEL), F32)],
        compiler_params=_params("arbitrary"),
        name="out_proj",
    )(*[a for pair in acts for a in pair], *xs, g.reshape(1, D_MODEL), mod_l, router_wp, w)


def _router_kernel(lg_ref, rb_ref, pos_ref, w_ref, plan_ref, rank_ref):
    lg = lg_ref[...]
    ex = jnp.exp(lg - jnp.max(lg, axis=0, keepdims=True))
    scores = ex / jnp.sum(ex, axis=0, keepdims=True)
    biased = scores + rb_ref[...]
    rows = [biased[e:e + 1, :] for e in range(N_EXPERTS)]
    selected = []
    group_score = []
    for gi in range(N_GROUPS):
        r = rows[gi * EXPERTS_PER_GROUP:(gi + 1) * EXPERTS_PER_GROUP]
        total = None
        for i in range(EXPERTS_PER_GROUP):
            rank = None
            for j in range(EXPERTS_PER_GROUP):
                if j == i:
                    continue
                ahead = (r[j] > r[i]) if j > i else (r[j] >= r[i])
                ahead = jnp.where(ahead, 1.0, 0.0)
                rank = ahead if rank is None else rank + ahead
            sel = rank < 1.5
            selected.append(sel)
            contrib = jnp.where(sel, r[i], 0.0)
            total = contrib if total is None else total + contrib
        group_score.append(total)
    best = group_score[0]
    best_group = jnp.zeros_like(best)
    for gi in range(1, N_GROUPS):
        better = group_score[gi] > best
        best_group = jnp.where(better, float(gi), best_group)
        best = jnp.where(better, group_score[gi], best)
    picked = []
    chosen = []
    den = None
    for e in range(N_EXPERTS):
        in_group = best_group == float(e // EXPERTS_PER_GROUP)
        use = jnp.where(selected[e], jnp.where(in_group, 1.0, 0.0), 0.0)
        w = use * scores[e:e + 1, :]
        chosen.append(use)
        picked.append(w)
        den = w if den is None else den + w
    lanes = 128
    n_blk = N_TOK // lanes
    li = lax.broadcasted_iota(jnp.int32, (lanes, lanes), 0)
    lj = lax.broadcasted_iota(jnp.int32, (lanes, lanes), 1)
    prefix = jnp.where(li <= lj, 1.0, 0.0).astype(BF16)
    carry = jnp.zeros((N_EXPERTS, 1), F32)
    for blk in range(n_blk):
        cols = slice(blk * lanes, (blk + 1) * lanes)
        m = jnp.concatenate([chosen[e][:, cols] for e in range(N_EXPERTS)], axis=0)
        incl = jnp.dot(m.astype(BF16), prefix, preferred_element_type=F32)
        rank_ref[:, cols] = incl - m + carry
        carry = carry + incl[:, lanes - 1:lanes]
    count = carry
    padded = jnp.floor((count + float(MOE_TILE - 1)) * (1.0 / MOE_TILE)) * float(MOE_TILE)
    erow = lax.broadcasted_iota(jnp.int32, (N_EXPERTS, 1), 0)
    offset = jnp.zeros((N_EXPERTS, 1), F32)
    for e in range(N_EXPERTS - 1):
        offset = offset + jnp.where(erow > e, padded[e:e + 1, :], 0.0)
    seen = jnp.zeros_like(den)
    pos_a = jnp.zeros_like(den)
    pos_b = jnp.zeros_like(den)
    w_a = jnp.zeros_like(den)
    w_b = jnp.zeros_like(den)
    for e in range(N_EXPERTS):
        pos_e = rank_ref[e:e + 1, :] + offset[e:e + 1, :]
        gate_e = picked[e] / den
        first = jnp.where(seen < 0.5, chosen[e], 0.0) > 0.5
        second = jnp.where(seen > 0.5, chosen[e], 0.0) > 0.5
        pos_a = jnp.where(first, pos_e, pos_a)
        w_a = jnp.where(first, gate_e, w_a)
        pos_b = jnp.where(second, pos_e, pos_b)
        w_b = jnp.where(second, gate_e, w_b)
        seen = seen + chosen[e]
    pos_ref[0:1, :] = pos_a.astype(jnp.int32)
    pos_ref[1:2, :] = pos_b.astype(jnp.int32)
    w_rows = jnp.concatenate([w_a, w_b, jnp.zeros((6, N_TOK), F32)], axis=0)
    ei = lax.broadcasted_iota(jnp.int32, (8, lanes), 0)
    ej = lax.broadcasted_iota(jnp.int32, (8, lanes), 1)
    eye = jnp.where(ei == ej, 1.0, 0.0).astype(BF16)
    tn = (((0,), (0,)), ((), ()))
    hi = w_rows.astype(BF16)
    r1 = w_rows - hi.astype(F32)
    mid = r1.astype(BF16)
    lo = (r1 - mid.astype(F32)).astype(BF16)
    w_cols = lax.dot_general(hi, eye, tn, preferred_element_type=F32)
    w_cols = w_cols + lax.dot_general(mid, eye, tn, preferred_element_type=F32)
    w_cols = w_cols + lax.dot_general(lo, eye, tn, preferred_element_type=F32)
    w_ref[...] = w_cols[:, :TOP_K]
    start = (lax.broadcasted_iota(jnp.int32, (N_EXPERTS, lanes), 1) * MOE_TILE).astype(F32)
    end = offset + padded
    tile_expert = jnp.sum(jnp.where(end <= start, 1.0, 0.0), axis=0, keepdims=True)
    inside = (offset <= start) & (start < end)
    real = jnp.clip(count - (start - offset), 0.0, float(MOE_TILE))
    tile_rows = jnp.sum(jnp.where(inside, real, 0.0), axis=0, keepdims=True)
    plan_ref[0:1, :] = jnp.minimum(tile_expert, float(N_EXPERTS - 1)).astype(jnp.int32)
    plan_ref[1:2, :] = tile_rows.astype(jnp.int32)


def _router(logits_t, router_b):
    whole = lambda shape: pl.BlockSpec(shape, lambda i: (0, 0))
    return pl.pallas_call(
        _router_kernel,
        grid=(1,),
        in_specs=[whole((N_EXPERTS, N_TOK)), whole((N_EXPERTS, 1))],
        out_specs=(whole((2, N_TOK)), whole((N_TOK, TOP_K)), whole((2, 128))),
        out_shape=(jax.ShapeDtypeStruct((2, N_TOK), jnp.int32),
                   jax.ShapeDtypeStruct((N_TOK, TOP_K), F32),
                   jax.ShapeDtypeStruct((2, 128), jnp.int32)),
        scratch_shapes=[pltpu.VMEM((N_EXPERTS, N_TOK), F32)],
        compiler_params=_params("arbitrary"),
        name="router",
    )(logits_t, router_b.reshape(N_EXPERTS, 1))


def _sc_mesh():
    return plsc.VectorSubcoreMesh(core_axis_name="c", subcore_axis_name="s")


def _sc_worker_base():
    return (lax.axis_index("s") * SC_CORES + lax.axis_index("c")) * SC_TOKENS_PER_WORKER


def _moe_dispatch(h, pos_a, pos_b):
    n_chunks = SC_TOKENS_PER_WORKER // SC_CHUNK
    idx = pltpu.VMEM((SC_CHUNK,), jnp.int32)

    @functools.partial(
        pl.kernel, mesh=_sc_mesh(),
        out_type=jax.ShapeDtypeStruct((MOE_ROWS, ROW_WORDS), jnp.int32),
        scratch_types=[idx, idx, idx, idx, pltpu.VMEM((2, SC_CHUNK, ROW_WORDS), jnp.int32),
                       pltpu.SemaphoreType.DMA((6,)), pltpu.SemaphoreType.DMA((4,))],
        name="moe_dispatch",
    )
    def run(h_hbm, pa_hbm, pb_hbm, xs_hbm, ia0, ib0, ia1, ib1, rows_v, sem_in, sem_out):
        base = _sc_worker_base()
        ia, ib = (ia0, ia1), (ib0, ib1)

        def start_loads(c):
            slot = c % 2
            tok = pl.ds(pl.multiple_of(base + c * SC_CHUNK, 8), SC_CHUNK)
            return (pltpu.async_copy(pa_hbm.at[tok], ia[slot], sem_in.at[3 * slot]),
                    pltpu.async_copy(pb_hbm.at[tok], ib[slot], sem_in.at[3 * slot + 1]),
                    pltpu.async_copy(h_hbm.at[tok], rows_v.at[slot], sem_in.at[3 * slot + 2]))

        loads = start_loads(0)
        scatters = [(), ()]
        for c in range(n_chunks):
            slot = c % 2
            for cp in loads:
                cp.wait()
            if c + 1 < n_chunks:
                for cp in scatters[1 - slot]:
                    cp.wait()
                scatters[1 - slot] = ()
                loads = start_loads(c + 1)
            scatters[slot] = (pltpu.async_copy(rows_v.at[slot], xs_hbm.at[ia[slot]], sem_out.at[2 * slot]),
                              pltpu.async_copy(rows_v.at[slot], xs_hbm.at[ib[slot]], sem_out.at[2 * slot + 1]))
        for pending in scatters:
            for cp in pending:
                cp.wait()

    return run(h, pos_a, pos_b)


def _moe_collect(ys, pos_a, pos_b):
    n_chunks = SC_TOKENS_PER_WORKER // SC_CHUNK
    out = jax.ShapeDtypeStruct((N_TOK, ROW_WORDS), jnp.int32)
    idx = pltpu.VMEM((SC_TOKENS_PER_WORKER,), jnp.int32)
    rows = pltpu.VMEM((2, SC_CHUNK, ROW_WORDS), jnp.int32)

    @functools.partial(
        pl.kernel, mesh=_sc_mesh(), out_type=(out, out),
        scratch_types=[idx, idx, rows, rows, pltpu.SemaphoreType.DMA((4,)), pltpu.SemaphoreType.DMA((4,))],
        name="moe_collect",
    )
    def run(ys_hbm, pa_hbm, pb_hbm, ya_hbm, yb_hbm, ia_v, ib_v, ra_v, rb_v, sem_g, sem_w):
        base = _sc_worker_base()
        mine = pl.ds(pl.multiple_of(base, 8), SC_TOKENS_PER_WORKER)
        pltpu.sync_copy(pa_hbm.at[mine], ia_v)
        pltpu.sync_copy(pb_hbm.at[mine], ib_v)
        writes = [(), ()]
        for c in range(n_chunks):
            slot = c % 2
            for cp in writes[slot]:
                cp.wait()
            part = pl.ds(c * SC_CHUNK, SC_CHUNK)
            tok = pl.ds(pl.multiple_of(base + c * SC_CHUNK, 8), SC_CHUNK)
            ga = pltpu.async_copy(ys_hbm.at[ia_v.at[part]], ra_v.at[slot], sem_g.at[slot])
            gb = pltpu.async_copy(ys_hbm.at[ib_v.at[part]], rb_v.at[slot], sem_g.at[2 + slot])
            ga.wait()
            wa = pltpu.async_copy(ra_v.at[slot], ya_hbm.at[tok], sem_w.at[slot])
            gb.wait()
            wb = pltpu.async_copy(rb_v.at[slot], yb_hbm.at[tok], sem_w.at[2 + slot])
            writes[slot] = (wa, wb)
        for pending in writes:
            for cp in pending:
                cp.wait()

    return run(ys, pos_a, pos_b)


def _experts_kernel(plan_ref, xs_ref, wg_hbm, wu_hbm, wd_hbm, y_ref,
                    sg_ref, su_ref, sd_ref, wgb_ref, wub_ref, wdb_ref, hid_ref, sems, seg_ref, *, layer):
    j = pl.program_id(0)
    n_tiles = pl.num_programs(0)
    expert = plan_ref[j]
    n_real = plan_ref[PLAN_LANES + j]
    fresh = jnp.logical_or(j == 0, expert != plan_ref[jnp.maximum(j - 1, 0)])

    def weight_copies(e, slot):
        return (pltpu.make_async_copy(wg_hbm.at[layer, e], sg_ref.at[slot], sems.at[slot, 0]),
                pltpu.make_async_copy(wu_hbm.at[layer, e], su_ref.at[slot], sems.at[slot, 1]),
                pltpu.make_async_copy(wd_hbm.at[layer, e], sd_ref.at[slot], sems.at[slot, 2]))

    @pl.when(j == 0)
    def _():
        seg_ref[0] = 0

        @pl.when(n_real > 0)
        def _():
            for cp in weight_copies(expert, 0):
                cp.start()

    @pl.when(jnp.logical_and(n_real > 0, fresh))
    def _():
        slot = seg_ref[0] % 2
        for cp in weight_copies(expert, slot):
            cp.wait()
        wgb_ref[...] = sg_ref[slot].astype(BF16)
        wub_ref[...] = su_ref[slot].astype(BF16)
        wdb_ref[...] = sd_ref[slot].astype(BF16)
        nxt = lax.while_loop(lambda t: jnp.logical_and(t < n_tiles, plan_ref[jnp.minimum(t, n_tiles - 1)] == expert),
                             lambda t: t + 1, j + 1)
        nxt_c = jnp.minimum(nxt, n_tiles - 1)

        @pl.when(jnp.logical_and(nxt < n_tiles, plan_ref[PLAN_LANES + nxt_c] > 0))
        def _():
            for cp in weight_copies(plan_ref[nxt_c], 1 - slot):
                cp.start()

        seg_ref[0] = seg_ref[0] + 1

    @pl.when(n_real > 0)
    def _():
        n = EXPERT_SUB_ROWS
        n_sub = xs_ref.shape[0] // n
        row = lax.broadcasted_iota(jnp.int32, (n, xs_ref.shape[1]), 0)

        def up(r):
            rows = slice(r * n, (r + 1) * n)
            words = jnp.where(row < n_real - r * n, xs_ref[rows, :], 0)
            x = _unpack_rows(words).astype(BF16)
            a = jnp.dot(x, wgb_ref[...], preferred_element_type=F32)
            b = jnp.dot(x, wub_ref[...], preferred_element_type=F32)
            hid_ref[r] = ((a * jax.nn.sigmoid(a)) * b).astype(BF16)

        def down(r):
            rows = slice(r * n, (r + 1) * n)
            y_ref[rows, :] = _pack_rows(jnp.dot(hid_ref[r], wdb_ref[...], preferred_element_type=F32))

        up(0)
        for r in range(1, n_sub):
            up(r)
            down(r - 1)
        down(n_sub - 1)


def _experts(plan, xs, w_gate, w_up, w_down, layer):
    hbm = pl.BlockSpec(memory_space=pl.ANY)
    return pl.pallas_call(
        functools.partial(_experts_kernel, layer=layer),
        grid_spec=pltpu.PrefetchScalarGridSpec(
            num_scalar_prefetch=1,
            grid=(MOE_ROWS // MOE_TILE,),
            in_specs=[pl.BlockSpec((MOE_TILE, ROW_WORDS), lambda j, plan: (j, 0)), hbm, hbm, hbm],
            out_specs=pl.BlockSpec((MOE_TILE, ROW_WORDS), lambda j, plan: (j, 0)),
            scratch_shapes=[pltpu.VMEM((2, D_MODEL, D_EXPERT), F32), pltpu.VMEM((2, D_MODEL, D_EXPERT), F32),
                            pltpu.VMEM((2, D_EXPERT, D_MODEL), F32),
                            pltpu.VMEM((D_MODEL, D_EXPERT), BF16), pltpu.VMEM((D_MODEL, D_EXPERT), BF16),
                            pltpu.VMEM((D_EXPERT, D_MODEL), BF16),
                            pltpu.VMEM((MOE_TILE // EXPERT_SUB_ROWS, EXPERT_SUB_ROWS, D_EXPERT), BF16),
                            pltpu.SemaphoreType.DMA((2, 3)), pltpu.SMEM((1,), jnp.int32)],
        ),
        out_shape=jax.ShapeDtypeStruct((MOE_ROWS, ROW_WORDS), jnp.int32),
        compiler_params=_params("arbitrary"),
        name="experts",
    )(plan, xs, w_gate, w_up, w_down)


def _combine_kernel(x_ref, ya_ref, yb_ref, wt_ref, mod_ref, o_ref):
    o_ref[...] = _moe_mix(x_ref, ya_ref, yb_ref, wt_ref, mod_ref)


def _combine(x, moe_out, mod_l, tok0, n_tok, block_rows=512):
    ya, yb, w_tok = moe_out
    b0 = tok0 // block_rows
    rows = lambda width: pl.BlockSpec((block_rows, width), lambda i: (b0 + i, 0))
    return pl.pallas_call(
        _combine_kernel,
        grid=(n_tok // block_rows,),
        in_specs=[rows(D_MODEL), rows(ROW_WORDS), rows(ROW_WORDS), rows(TOP_K),
                  pl.BlockSpec((None, 6, D_MODEL), lambda i: (_cond_of_token_block(b0 + i, block_rows), 0, 0))],
        out_specs=pl.BlockSpec((block_rows, D_MODEL), lambda i: (i, 0)),
        out_shape=jax.ShapeDtypeStruct((n_tok, D_MODEL), F32),
        compiler_params=_params("arbitrary"),
        name="combine",
    )(x, ya, yb, w_tok, mod_l)


def _moe(h, logits_t, router_b, w_gate, w_up, w_down, layer):
    pos, w, plan = _router(logits_t, router_b)
    xs = _moe_dispatch(h, pos[0], pos[1])
    ys = _experts(plan.reshape(-1), xs, w_gate, w_up, w_down, layer)
    ya, yb = _moe_collect(ys, pos[0], pos[1])
    return ya, yb, w


def _dft_tables(L):
    k = np.arange(L)[:, None]
    m = np.arange(L)[None, :]
    r = (k * m) % (2 * L)
    ang = np.pi * r.astype(np.float64) / L
    fc = np.cos(ang)
    fs = np.sin(ang)
    fs[0, :] = np.where(np.arange(L) % 2 == 0, 1.0, -1.0)
    wk = np.full((L, 1), 1.0 / L)
    wk[0, 0] = 0.5 / L
    gc = (fc * wk).T
    gs = (fs * wk).T
    return [jnp.asarray(t.astype(np.float32)).astype(BF16) for t in (fc, fs, gc, gs)]


def _filter_consts(L):
    t = np.linspace(0.0, 1.0, L, dtype=np.float32)[:, None]
    w = (np.float32(2.0 * np.pi) * np.arange(L, dtype=np.float32)[:, None] / np.float32(L)).astype(np.float32)
    fb = np.linspace(1e-4, HY_BANDS - 1, HY_BANDS, dtype=np.float32)[None, :]
    emb = np.concatenate([t, np.cos(fb * w), -np.sin(fb * w)], axis=-1).astype(np.float32)
    lo = math.log(HY_DECAY_TARGET) / HY_SLOW_PCT
    hi = math.log(HY_DECAY_TARGET) / HY_FAST_PCT
    deltas = np.abs(np.linspace(lo, hi, D_MODEL, dtype=np.float32))
    decay = np.exp(-t * deltas).astype(np.float32)
    return jnp.asarray(emb), jnp.asarray(decay)


def _filter_kernel(emb_ref, w1_ref, b1_ref, w2_ref, b2_ref, fr_ref, w3f_ref, w3b_ref, dec_ref,
                   fc_ref, fs_ref, kr_ref, q_ref, krn_ref, hd_ref):
    @pl.when(pl.program_id(0) == 0)
    def _():
        fr = fr_ref[...]
        h1 = jnp.sin(fr * (jnp.dot(emb_ref[...], w1_ref[...], precision=HIGHEST,
                                   preferred_element_type=F32) + b1_ref[...]))
        hd_ref[...] = jnp.sin(fr * (jnp.dot(h1, w2_ref[...], precision=HIGHEST,
                                            preferred_element_type=F32) + b2_ref[...]))

    hd = hd_ref[...]
    dec = dec_ref[...]
    f = jnp.dot(hd, w3f_ref[...], precision=HIGHEST, preferred_element_type=F32) * dec
    g = jnp.dot(hd, w3b_ref[...], precision=HIGHEST, preferred_element_type=F32) * dec
    row = lax.broadcasted_iota(jnp.int32, f.shape, 0)
    g = jnp.where(row == 0, 0.0, g)
    s = f + g
    d = f - g
    kr = jnp.dot(fc_ref[...], s.astype(BF16), preferred_element_type=F32)
    qq = jnp.dot(fs_ref[...], d.astype(BF16), preferred_element_type=F32)
    alt = jnp.where(row % 2 == 0, 1.0, -1.0)
    nyq = jnp.sum(alt * s, axis=0, keepdims=True)
    kr_ref[...] = kr
    q_ref[...] = jnp.where(row == 0, 0.0, qq)
    krn_ref[...] = jnp.where(row == 0, nyq, kr)


def _hyena_filter_spectrum(L, w1, b1, w2, b2, w3, freq, fc, fs, cblk=256):
    emb, decay = _filter_consts(L)
    ncb = D_MODEL // cblk
    n_emb = 128
    emb = jnp.pad(emb, ((0, 0), (0, n_emb - emb.shape[1])))
    w1 = jnp.pad(w1, ((0, n_emb - w1.shape[0]), (0, 0)))
    full = lambda shape: pl.BlockSpec(shape, lambda j: tuple(0 for _ in shape))
    out_sds = jax.ShapeDtypeStruct((L, D_MODEL), F32)
    out_spec = pl.BlockSpec((L, cblk), lambda j: (0, j))
    return pl.pallas_call(
        _filter_kernel,
        grid=(ncb,),
        in_specs=[
            full((L, n_emb)), full((n_emb, HY_FFN)), full((1, HY_FFN)), full((HY_FFN, HY_FFN)),
            full((1, HY_FFN)), full((1, HY_FFN)),
            pl.BlockSpec((HY_FFN, cblk), lambda j: (0, j)),
            pl.BlockSpec((HY_FFN, cblk), lambda j: (0, ncb + j)),
            pl.BlockSpec((L, cblk), lambda j: (0, j)),
            full((L, L)), full((L, L)),
        ],
        out_specs=(out_spec, out_spec, out_spec),
        out_shape=(out_sds, out_sds, out_sds),
        scratch_shapes=[pltpu.VMEM((L, HY_FFN), F32)],
        compiler_params=_params("arbitrary"),
        name=f"hyena_filter_{L}",
    )(emb, w1, b1.reshape(1, HY_FFN), w2, b2.reshape(1, HY_FFN), freq.reshape(1, HY_FFN), w3, w3, decay, fc, fs)


def _hyena_conv_kernel(x0_ref, x1_ref, v_ref, cw0_ref, cw1_ref, cwv_ref, cb0_ref, cb1_ref, cbv_ref,
                       kr_ref, q_ref, krn_ref, ds_ref, fc_ref, fs_ref, gc_ref, gs_ref, o_ref,
                       zz_ref, gate_ref, skip_ref, yr_ref, yw_ref):
    L = fc_ref.shape[0]
    unit_w = zz_ref.shape[2]
    units = [(slice(s * L, (s + 1) * L), slice(c * unit_w, (c + 1) * unit_w))
             for s in range(x0_ref.shape[0] // L) for c in range(x0_ref.shape[1] // unit_w)]
    row = lax.broadcasted_iota(jnp.int32, (L, unit_w), 0)

    def gating(i):
        rows, cols = units[i]

        def short_conv(u_ref, w_ref, b_ref):
            u = u_ref[rows, cols].astype(F32)
            w = w_ref[:, cols]
            prev = jnp.where(row == 0, 0.0, pltpu.roll(u, 1, axis=0))
            nxt = jnp.where(row == L - 1, 0.0, pltpu.roll(u, L - 1, axis=0))
            return prev * w[0:1, :] + u * w[1:2, :] + nxt * w[2:3, :] + b_ref[:, cols]

        x0 = short_conv(x0_ref, cw0_ref, cb0_ref)
        zz = short_conv(v_ref, cwv_ref, cbv_ref) * short_conv(x1_ref, cw1_ref, cb1_ref)
        zz_ref[i] = zz.astype(BF16)
        gate_ref[i] = x0
        skip_ref[i] = x0 * zz * ds_ref[:, cols]

    def spectrum(i):
        cols = units[i][1]
        ur = jnp.dot(fc_ref[...], zz_ref[i], preferred_element_type=F32)
        p = jnp.dot(fs_ref[...], zz_ref[i], preferred_element_type=F32)
        qq = q_ref[:, cols]
        yr_ref[i] = (ur * kr_ref[:, cols] - p * qq).astype(BF16)
        yw_ref[i] = (ur * qq + p * krn_ref[:, cols]).astype(BF16)

    def synthesis(i):
        rows, cols = units[i]
        y = jnp.dot(gc_ref[...], yr_ref[i], preferred_element_type=F32)
        y = y + jnp.dot(gs_ref[...], yw_ref[i], preferred_element_type=F32)
        o_ref[rows, cols] = (gate_ref[i] * y + skip_ref[i]).astype(o_ref.dtype)

    for t in range(len(units) + 2):
        if t < len(units):
            gating(t)
        if 0 <= t - 1 < len(units):
            spectrum(t - 1)
        if 0 <= t - 2 < len(units):
            synthesis(t - 2)


def _hyena_conv(u, conv_w, conv_b, dskip, spectrum, tables, *, latent):
    L = LATENT_LEN if latent else PROMPT_LEN
    n_seq = N_LATENT_SEQ if latent else N_PROMPT_SEQ
    cblk = 512
    unit_w = 256 if latent else 512
    ncb = D_MODEL // cblk
    seqs = 1 if latent else 8
    unit = (seqs * cblk // unit_w, L, unit_w)
    row0 = (N_PROMPT_TOK // L) if latent else 0
    kr, qq, krn = spectrum
    fc, fs, gc, gs = tables

    def part(p, rows):
        if rows != L:
            return pl.BlockSpec((rows, cblk), lambda j, s: (0, p * ncb + j))
        return pl.BlockSpec((seqs * L, cblk), lambda j, s: (row0 // seqs + s, p * ncb + j))

    def const_cols(rows):
        return pl.BlockSpec((rows, cblk), lambda j, s: (0, j))

    mat = pl.BlockSpec((L, L), lambda j, s: (0, 0))
    conv_b2 = conv_b.reshape(1, 3 * D_MODEL)
    in_specs = [part(0, L), part(1, L), part(2, L),
                part(0, 3), part(1, 3), part(2, 3),
                part(0, 1), part(1, 1), part(2, 1),
                const_cols(L), const_cols(L), const_cols(L), const_cols(1),
                mat, mat, mat, mat]
    args = [u, u, u, conv_w, conv_w, conv_w, conv_b2, conv_b2, conv_b2,
            kr, qq, krn, dskip.reshape(1, D_MODEL), fc, fs, gc, gs]
    return pl.pallas_call(
        _hyena_conv_kernel,
        grid=(ncb, n_seq // seqs),
        in_specs=in_specs,
        out_specs=pl.BlockSpec((seqs * L, cblk), lambda j, s: (s, j)),
        out_shape=jax.ShapeDtypeStruct((n_seq * L, D_MODEL), BF16),
        scratch_shapes=[pltpu.VMEM(unit, BF16), pltpu.VMEM(unit, F32), pltpu.VMEM(unit, F32),
                        pltpu.VMEM(unit, BF16), pltpu.VMEM(unit, BF16)],
        compiler_params=_params("arbitrary", "arbitrary"),
        name="hyena_conv_latent" if latent else "hyena_conv_prompt",
    )(*args)


def kernel(x_prompt, x_sample, cache_k, cache_v, state_hgrn, c, c_ctx, norm_g, mod_w, mod_b, ab_in_w, hgrn_lb, hgrn_onorm_g, attn_qnorm_g, attn_knorm_g, ab_out_w, hy_in_w, hy_in_b, hy_conv_w, hy_conv_b, hy_f_w1, hy_f_b1, hy_f_w2, hy_f_b2, hy_f_w3, hy_f_freq, hy_dskip, hy_out_w, router_w, router_b, moe_w_gate, moe_w_up, moe_w_down):
    xp = x_prompt.reshape(N_PROMPT_TOK, D_MODEL)
    xl = x_sample.reshape(N_LATENT_TOK, D_MODEL)
    cond = jnp.concatenate([c_ctx[None, :], c, jnp.zeros((N_COND - 1 - N_LATENT_SEQ, D_MODEL), F32)], axis=0)
    mod = _modulation(cond, mod_w, mod_b)
    router_wp = jnp.pad(router_w, ((0, 0), (0, ROUTER_LANES - N_EXPERTS)))

    z = _in_proj0(xp, xl, norm_g[0, 0], mod[0], ab_in_w[0])
    oa_p, new_state = _hgrn(z, hgrn_lb, hgrn_onorm_g[0], None, latent=False)
    oa_l = _hgrn(z, hgrn_lb, hgrn_onorm_g[0], state_hgrn, latent=True)
    ob_p, k_fm, v_fm = _attention_prompt(z, attn_qnorm_g[0], attn_knorm_g[0])
    fm_shape = (N_PROMPT_SEQ, 1, KV_HEADS, HEAD_DIM, PROMPT_LEN)
    new_k = jnp.swapaxes(k_fm.reshape(fm_shape), -1, -2)
    new_v = jnp.swapaxes(v_fm.reshape(fm_shape), -1, -2)
    ob_l = _attention_latent(z, attn_qnorm_g[0], attn_knorm_g[0], cache_k, cache_v)
    x, h, logits_t = _out_proj([(oa_p, oa_l), (ob_p, ob_l)], ab_out_w[0], (xp, xl), norm_g[0, 1], mod[0],
                               router_wp)
    moe_out = _moe(h, logits_t, router_b, moe_w_gate, moe_w_up, moe_w_down, 0)

    x, u = _in_proj1(x, moe_out, mod[0], norm_g[1, 0], mod[1], hy_in_w[0], hy_in_b[0])
    pre = []
    for latent in (False, True):
        L = LATENT_LEN if latent else PROMPT_LEN
        tables = _dft_tables(L)
        spectrum = _hyena_filter_spectrum(L, hy_f_w1[0], hy_f_b1[0], hy_f_w2[0], hy_f_b2[0], hy_f_w3[0],
                                          hy_f_freq[0], tables[0], tables[1])
        pre.append(_hyena_conv(u, hy_conv_w[0], hy_conv_b[0], hy_dskip[0], spectrum, tables, latent=latent))
    x, h, logits_t = _out_proj([tuple(pre)], hy_out_w[0], (x,), norm_g[1, 1], mod[1], router_wp)
    moe_out = _moe(h, logits_t, router_b, moe_w_gate, moe_w_up, moe_w_down, 1)

    y_prompt = _combine(x, moe_out, mod[1], 0, N_PROMPT_TOK).reshape(N_PROMPT_SEQ, PROMPT_LEN, D_MODEL)
    y_sample = _combine(x, moe_out, mod[1], N_PROMPT_TOK, N_LATENT_TOK).reshape(N_LATENT_SEQ, LATENT_LEN, D_MODEL)
    return (y_prompt, y_sample, new_k, new_v, new_state)
```

```python
import functools
import math

import numpy as np
import jax
import jax.numpy as jnp
from jax import lax
from jax.experimental import pallas as pl
from jax.experimental.pallas import tpu as pltpu
from jax.experimental.pallas import tpu_sc as plsc

F32 = jnp.float32
BF16 = jnp.bfloat16
HIGHEST = lax.Precision.HIGHEST

D_MODEL = 1024
N_PROMPT_SEQ = 32
PROMPT_LEN = 256
N_LATENT_SEQ = 2
LATENT_LEN = 1024
PAST_LEN = 512
GRID_W = 64
N_PROMPT_TOK = N_PROMPT_SEQ * PROMPT_LEN
N_LATENT_TOK = N_LATENT_SEQ * LATENT_LEN
N_TOK = N_PROMPT_TOK + N_LATENT_TOK
N_COND = 8
EPS = 1e-6

A_WIDTH = 512
A_HEADS = 4
A_DK = 128
CHUNK = 64
HGRN_BLOCK = 128
HGRN_HEADS_PER_STEP = 4
HEAD_DIM = 64
Q_HEADS = 8
KV_HEADS = 2
Q_PER_KV = Q_HEADS // KV_HEADS
Q_BLOCK = 256
ROPE_THETA = 10000.0
ROPE_PAIRS = HEAD_DIM // 4
AB_IN = 5 * A_WIDTH + (Q_HEADS + 2 * KV_HEADS) * HEAD_DIM

HY_BANDS = 16
HY_FFN = 64
HY_DECAY_TARGET = 1e-2
HY_FAST_PCT = 0.3
HY_SLOW_PCT = 1.5

N_EXPERTS = 16
N_GROUPS = 4
EXPERTS_PER_GROUP = 4
TOP_K = 2
D_EXPERT = 512
ROUTER_LANES = 128
OUT_PROJ_SUB_ROWS = 256
EXPERT_SUB_ROWS = 256
IN_PROJ_SUB_ROWS = 256
MOE_TILE = 512
MOE_ROWS = N_TOK * TOP_K + N_EXPERTS * MOE_TILE
PLAN_LANES = 128

SC_CORES = 2
SC_WORKERS = 32
SC_TOKENS_PER_WORKER = N_TOK // SC_WORKERS
SC_CHUNK = 40
ROW_WORDS = D_MODEL // 2

VMEM_LIMIT = 56 * 1024 * 1024


def _params(*sem):
    return pltpu.CompilerParams(dimension_semantics=sem, vmem_limit_bytes=VMEM_LIMIT)


def _pack_rows(x):
    n = x.shape[1] // 2
    bits = pltpu.bitcast(x.astype(BF16).astype(F32), jnp.uint32)
    return pltpu.bitcast(bits[:, :n] | (bits[:, n:] >> 16), jnp.int32)


def _unpack_rows(p):
    bits = pltpu.bitcast(p, jnp.uint32)
    hi = pltpu.bitcast(bits & jnp.uint32(0xFFFF0000), F32)
    lo = pltpu.bitcast(bits << 16, F32)
    return jnp.concatenate([hi, lo], axis=1)


def _cond_of_token_block(i, block_rows):
    start = i * block_rows
    return jnp.where(start < N_PROMPT_TOK, 0, 1 + (start - N_PROMPT_TOK) // LATENT_LEN)


def _mod_kernel(cond_ref, w_ref, b_ref, *rest):
    o_ref = rest[-1]
    cnd = cond_ref[...]
    s = cnd * jax.nn.sigmoid(cnd)
    s_hi = s.astype(BF16)
    s_lo = (s - s_hi.astype(F32)).astype(BF16)
    w = w_ref[...]
    w_hi = w.astype(BF16)
    w_lo = (w - w_hi.astype(F32)).astype(BF16)
    acc = jnp.dot(s_hi, w_hi, preferred_element_type=F32)
    acc = acc + jnp.dot(s_lo, w_hi, preferred_element_type=F32)
    acc = acc + jnp.dot(s_hi, w_lo, preferred_element_type=F32)
    o_ref[...] = acc + b_ref[...]


def _after(anchor):
    return ([], []) if anchor is None else ([pl.BlockSpec(memory_space=pl.ANY)], [anchor])


def _modulation(cond, mod_w, mod_b, layer, after=None):
    depth = mod_w.shape[0]
    n_chunk = 6
    anchor_specs, anchor_args = _after(after)
    out = pl.pallas_call(
        _mod_kernel,
        grid=(n_chunk,),
        in_specs=[
            pl.BlockSpec((N_COND, D_MODEL), lambda j: (0, 0)),
            pl.BlockSpec((None, D_MODEL, D_MODEL), lambda j: (layer, 0, j)),
            pl.BlockSpec((None, 1, D_MODEL), lambda j: (layer, 0, j)),
            *anchor_specs,
        ],
        out_specs=pl.BlockSpec((N_COND, D_MODEL), lambda j: (0, j)),
        out_shape=jax.ShapeDtypeStruct((N_COND, n_chunk * D_MODEL), F32),
        compiler_params=_params("arbitrary"),
        name="modulation",
    )(cond, mod_w, mod_b.reshape(depth, 1, n_chunk * D_MODEL), *anchor_args)
    return out.reshape(N_COND, n_chunk, D_MODEL)


def _modulated_norm(x, g, mod, shift_row, scale_row):
    ms = jnp.mean(x * x, axis=-1, keepdims=True)
    y = x * lax.rsqrt(ms + EPS) * g
    return y * (1.0 + mod[scale_row:scale_row + 1, :]) + mod[shift_row:shift_row + 1, :]


def _trunk_specs(block_rows, width):
    n_prompt_blocks = N_PROMPT_TOK // block_rows
    return (pl.BlockSpec((block_rows, width), lambda i: (jnp.minimum(i, n_prompt_blocks - 1), 0)),
            pl.BlockSpec((block_rows, width), lambda i: (jnp.maximum(i - n_prompt_blocks, 0), 0)))


def _select_trunk(p_ref, l_ref, rows=slice(None)):
    block_rows = p_ref.shape[0]
    return jnp.where(pl.program_id(0) < N_PROMPT_TOK // block_rows, p_ref[rows, :], l_ref[rows, :])


def _cast_once(w_ref, wb_ref):
    @pl.when(pl.program_id(0) == 0)
    def _():
        wb_ref[...] = w_ref[...].astype(BF16)


def _resident(shape):
    return pl.BlockSpec(shape, lambda i: tuple(0 for _ in shape), pipeline_mode=pl.Buffered(1))


def _mod_spec(block_rows):
    return pl.BlockSpec((None, 6, D_MODEL), lambda i: (_cond_of_token_block(i, block_rows), 0, 0))


def _in_proj0_kernel(xp_ref, xl_ref, g_ref, mod_ref, w_ref, o_ref, wb_ref, hb_ref):
    _cast_once(w_ref, wb_ref)
    n = IN_PROJ_SUB_ROWS
    n_sub = xp_ref.shape[0] // n

    def prepare(r):
        x = _select_trunk(xp_ref, xl_ref, slice(r * n, (r + 1) * n))
        hb_ref[r] = _modulated_norm(x, g_ref[...], mod_ref[...], 0, 1).astype(BF16)

    def project(r):
        u = jnp.dot(hb_ref[r], wb_ref[...], preferred_element_type=F32)
        o_ref[r * n:(r + 1) * n, :] = u.astype(o_ref.dtype)

    prepare(0)
    for r in range(1, n_sub):
        prepare(r)
        project(r - 1)
    project(n_sub - 1)


def _in_proj0(x_prompt, x_latent, g, mod_l, w, block_rows=512):
    n = w.shape[1]
    return pl.pallas_call(
        _in_proj0_kernel,
        grid=(N_TOK // block_rows,),
        in_specs=[*_trunk_specs(block_rows, D_MODEL), _resident((1, D_MODEL)), _mod_spec(block_rows),
                  _resident((D_MODEL, n))],
        out_specs=pl.BlockSpec((block_rows, n), lambda i: (i, 0)),
        out_shape=jax.ShapeDtypeStruct((N_TOK, n), BF16),
        scratch_shapes=[pltpu.VMEM((D_MODEL, n), BF16),
                        pltpu.VMEM((block_rows // IN_PROJ_SUB_ROWS, IN_PROJ_SUB_ROWS, D_MODEL), BF16)],
        compiler_params=_params("arbitrary"),
        name="in_proj0",
    )(x_prompt, x_latent, g.reshape(1, D_MODEL), mod_l, w)


def _moe_mix(x_ref, ya_ref, yb_ref, wt_ref, mod_ref, rows=slice(None)):
    wt = wt_ref[rows, :]
    mix = wt[:, 0:1] * _unpack_rows(ya_ref[rows, :]) + wt[:, 1:2] * _unpack_rows(yb_ref[rows, :])
    return x_ref[rows, :] + mod_ref[5:6, :] * mix


def _in_proj1_kernel(x_ref, ya_ref, yb_ref, wt_ref, modp_ref, g_ref, mod_ref, w_ref, b_ref, xo_ref, o_ref,
                     wb_ref, hb_ref):
    _cast_once(w_ref, wb_ref)
    n = IN_PROJ_SUB_ROWS
    n_sub = x_ref.shape[0] // n

    def prepare(r):
        rows = slice(r * n, (r + 1) * n)
        x = _moe_mix(x_ref, ya_ref, yb_ref, wt_ref, modp_ref, rows)
        xo_ref[rows, :] = x
        hb_ref[r] = _modulated_norm(x, g_ref[...], mod_ref[...], 0, 1).astype(BF16)

    def project(r):
        rows = slice(r * n, (r + 1) * n)
        u = jnp.dot(hb_ref[r], wb_ref[...], preferred_element_type=F32) + b_ref[...]
        o_ref[rows, :] = u.astype(o_ref.dtype)

    prepare(0)
    for r in range(1, n_sub):
        prepare(r)
        project(r - 1)
    project(n_sub - 1)


def _in_proj1(x, moe_out, mod_prev, g, mod_l, w, bias, block_rows=512):
    ya, yb, w_tok = moe_out
    n = w.shape[1]
    tok = pl.BlockSpec((block_rows, D_MODEL), lambda i: (i, 0))
    packed = pl.BlockSpec((block_rows, ROW_WORDS), lambda i: (i, 0))
    return pl.pallas_call(
        _in_proj1_kernel,
        grid=(N_TOK // block_rows,),
        in_specs=[tok, packed, packed, pl.BlockSpec((block_rows, TOP_K), lambda i: (i, 0)), _mod_spec(block_rows),
                  _resident((1, D_MODEL)), _mod_spec(block_rows), _resident((D_MODEL, n)), _resident((1, n))],
        out_specs=(tok, pl.BlockSpec((block_rows, n), lambda i: (i, 0))),
        out_shape=(jax.ShapeDtypeStruct((N_TOK, D_MODEL), F32), jax.ShapeDtypeStruct((N_TOK, n), BF16)),
        scratch_shapes=[pltpu.VMEM((D_MODEL, n), BF16),
                        pltpu.VMEM((block_rows // IN_PROJ_SUB_ROWS, IN_PROJ_SUB_ROWS, D_MODEL), BF16)],
        compiler_params=_params("arbitrary"),
        name="in_proj1",
    )(x, ya, yb, w_tok, mod_prev, g.reshape(1, D_MODEL), mod_l, w, bias.reshape(1, n))


def _hgrn_kernel(*refs, seq_len, with_state):
    if with_state:
        (q_ref, zf_ref, zb_ref, i_ref, ga_ref, lb_ref, og_ref, s0_ref, o_ref, of_ref, ob_ref) = refs
    else:
        (q_ref, zf_ref, zb_ref, i_ref, ga_ref, lb_ref, og_ref, o_ref, s_ref, of_ref, ob_ref) = refs
    n_blocks = seq_len // HGRN_BLOCK
    chunks_per_block = HGRN_BLOCK // CHUNK

    lbr = lb_ref[...]
    mx = jnp.maximum(lbr[0], lbr[1])
    e0 = jnp.exp(lbr[0] - mx)
    e1 = jnp.exp(lbr[1] - mx)
    lb = e0 / (e0 + e1)

    row = lax.broadcasted_iota(jnp.int32, (HGRN_BLOCK, HGRN_BLOCK), 0)
    col = lax.broadcasted_iota(jnp.int32, (HGRN_BLOCK, HGRN_BLOCK), 1)
    same_chunk = (row // CHUNK) == (col // CHUNK)
    nt = (((1,), (1,)), ((), ()))
    tn = (((0,), (0,)), ((), ()))

    def per_chunk_row(x, idx):
        return jnp.concatenate(
            [jnp.broadcast_to(x[n * CHUNK + idx:n * CHUNK + idx + 1, :], (CHUNK, x.shape[1]))
             for n in range(chunks_per_block)], axis=0)

    def in_chunk_cumsum(tri, x):
        hi = x.astype(BF16)
        lo = (x - hi.astype(F32)).astype(BF16)
        return jnp.dot(tri, hi, preferred_element_type=F32) + jnp.dot(tri, lo, preferred_element_type=F32)

    def prepare(blk, cols, z_ref, lbd, forward):
        rows = slice(blk * HGRN_BLOCK, (blk + 1) * HGRN_BLOCK)
        keep = (same_chunk & (col <= row)) if forward else (same_chunk & (col >= row))
        tri = jnp.where(keep, 1.0, 0.0).astype(BF16)
        mid = CHUNK // 2 if forward else CHUNK - 1 - CHUNK // 2
        last = CHUNK - 1 if forward else 0
        f = lbd + (1.0 - lbd) * jax.nn.sigmoid(z_ref[rows, cols].astype(F32))
        lf = jnp.log(f)
        k = 1.0 - f
        q = q_ref[rows, cols].astype(F32)
        b = in_chunk_cumsum(tri, lf)
        bm = per_chunk_row(b, mid)
        bl = per_chunk_row(b, last)
        return dict(
            rows=rows, cols=cols, keep=keep, forward=forward,
            vb=i_ref[rows, cols].astype(BF16),
            qe=(q * jnp.exp(b - bm)).astype(BF16), ke=(k * jnp.exp(bm - b)).astype(BF16),
            qb=(q * jnp.exp(b)).astype(BF16), ks=(k * jnp.exp(bl - b)).astype(BF16), decay=jnp.exp(bl))

    def within_chunks(u):
        att = lax.dot_general(u["qe"], u["ke"], nt, preferred_element_type=F32)
        att = jnp.where(u["keep"], att, 0.0)
        u["o_intra"] = jnp.dot(att.astype(BF16), u["vb"], preferred_element_type=F32)
        u["upd"] = [lax.dot_general(u["vb"][n * CHUNK:(n + 1) * CHUNK], u["ks"][n * CHUNK:(n + 1) * CHUNK], tn,
                                    preferred_element_type=F32) for n in range(chunks_per_block)]

    def across_chunks(u, st, out_ref):
        order = range(chunks_per_block) if u["forward"] else range(chunks_per_block - 1, -1, -1)
        o_inter = [None] * chunks_per_block
        for n in order:
            cr = slice(n * CHUNK, (n + 1) * CHUNK)
            o_inter[n] = lax.dot_general(u["qb"][cr], st.astype(BF16), nt, preferred_element_type=F32)
            st = st * u["decay"][n * CHUNK:n * CHUNK + 1, :] + u["upd"][n]
        out_ref[u["rows"], u["cols"]] = u["o_intra"] + jnp.concatenate(o_inter, axis=0)
        return st

    n_heads = q_ref.shape[1] // A_DK
    head_cols = [slice(hd * A_DK, (hd + 1) * A_DK) for hd in range(n_heads)]
    if with_state:
        states = {(hd, d): s0_ref[d, hd].T for hd in range(n_heads) for d in range(2)}
    else:
        states = {(hd, d): jnp.zeros((A_DK, A_DK), F32) for hd in range(n_heads) for d in range(2)}
    for step in range(n_blocks):
        units = {}
        for hd, cols in enumerate(head_cols):
            units[hd, 0] = prepare(step, cols, zf_ref, lb[0:1, cols], True)
            units[hd, 1] = prepare(n_blocks - 1 - step, cols, zb_ref, lb[1:2, cols], False)
        for u in units.values():
            within_chunks(u)
        for key, u in units.items():
            states[key] = across_chunks(u, states[key], of_ref if key[1] == 0 else ob_ref)
    for hd, cols in enumerate(head_cols):
        if not with_state:
            s_ref[0, hd] = states[hd, 0].T
            s_ref[1, hd] = states[hd, 1].T
        o = of_ref[:, cols] + ob_ref[:, cols]
        o = o * lax.rsqrt(jnp.mean(o * o, axis=-1, keepdims=True) + EPS) * og_ref[:, cols]
        ga = ga_ref[:, cols].astype(F32)
        o_ref[:, cols] = (o * (ga * jax.nn.sigmoid(ga))).astype(o_ref.dtype)


def _hgrn(z, hgrn_lb, onorm_g, state, *, latent):
    seq_len = LATENT_LEN if latent else PROMPT_LEN
    n_seq = N_LATENT_SEQ if latent else N_PROMPT_SEQ
    row0 = (N_PROMPT_TOK // seq_len) if latent else 0

    hw = HGRN_HEADS_PER_STEP * A_DK
    n_hg = A_HEADS // HGRN_HEADS_PER_STEP

    def zspec(part):
        return pl.BlockSpec((seq_len, hw), lambda s, h: (row0 + s, part * n_hg + h))

    in_specs = [zspec(0), zspec(1), zspec(2), zspec(3), zspec(4),
                pl.BlockSpec((2, 2, hw), lambda s, h: (0, 0, h)),
                pl.BlockSpec((1, hw), lambda s, h: (0, h))]
    args = [z, z, z, z, z, hgrn_lb, onorm_g.reshape(1, A_WIDTH)]
    state_spec = pl.BlockSpec((None, None, 2, HGRN_HEADS_PER_STEP, A_DK, A_DK), lambda s, h: (s, 0, 0, h, 0, 0))
    o_shape = jax.ShapeDtypeStruct((n_seq * seq_len, A_WIDTH), BF16)
    o_spec = pl.BlockSpec((seq_len, hw), lambda s, h: (s, h))
    if latent:
        in_specs.append(state_spec)
        args.append(state)
        out_shape, out_specs = o_shape, o_spec
    else:
        out_shape = (o_shape, jax.ShapeDtypeStruct((n_seq, 1, 2, A_HEADS, A_DK, A_DK), F32))
        out_specs = (o_spec, state_spec)
    return pl.pallas_call(
        functools.partial(_hgrn_kernel, seq_len=seq_len, with_state=latent),
        grid=(n_seq, n_hg),
        in_specs=in_specs,
        out_specs=out_specs,
        out_shape=out_shape,
        scratch_shapes=[pltpu.VMEM((seq_len, hw), F32), pltpu.VMEM((seq_len, hw), F32)],
        compiler_params=_params("arbitrary", "arbitrary"),
        name="hgrn_latent" if latent else "hgrn_prompt",
    )(*args)


def _rope_tables():
    pos = np.arange(LATENT_LEN)
    row, colp = pos // GRID_W, pos % GRID_W
    inv = ROPE_THETA ** (-np.arange(ROPE_PAIRS, dtype=np.float32) / ROPE_PAIRS)
    inv = inv.astype(np.float32)
    ang_r = (row.astype(np.float32)[:, None] * inv).astype(np.float32)
    ang_c = (colp.astype(np.float32)[:, None] * inv).astype(np.float32)
    cos = np.concatenate([np.cos(ang_r), np.cos(ang_r), np.cos(ang_c), np.cos(ang_c)], axis=1)
    sin = np.concatenate([-np.sin(ang_r), np.sin(ang_r), -np.sin(ang_c), np.sin(ang_c)], axis=1)
    return cos.astype(np.float32), sin.astype(np.float32)


def _head_mean_matrix(width):
    idx = np.arange(width) // HEAD_DIM
    return jnp.asarray((idx[:, None] == idx[None, :]).astype(np.float32) / HEAD_DIM).astype(BF16)


def _attn_kernel(*refs, latent):
    if latent:
        (q_ref, k_ref, v_ref, qg_ref, kg_ref, gq_ref, gk_ref, cosq_ref, sinq_ref, cosk_ref, sink_ref,
         ck_ref, cv_ref, o_ref) = refs
    else:
        (q_ref, k_ref, v_ref, qg_ref, kg_ref, gq_ref, gk_ref, o_ref, kout_ref, vout_ref) = refs
    pair_w = 2 * HEAD_DIM

    def head_norm(x, mean_ref, gain):
        sq = x * x
        hi = sq.astype(BF16)
        lo = (sq - hi.astype(F32)).astype(BF16)
        ms = jnp.dot(hi, mean_ref[...], preferred_element_type=F32)
        ms = ms + jnp.dot(lo, mean_ref[...], preferred_element_type=F32)
        return x * lax.rsqrt(ms + EPS) * gain

    def rope(x, cos, sin):
        n = x.shape[1]
        lane = lax.broadcasted_iota(jnp.int32, x.shape, 1)
        first_of_pair = (lane // ROPE_PAIRS) % 2 == 0
        swapped = jnp.where(first_of_pair, pltpu.roll(x, n - ROPE_PAIRS, axis=1), pltpu.roll(x, ROPE_PAIRS, axis=1))
        return x * cos + swapped * sin

    nt = (((1,), (1,)), ((), ()))

    def prepare(rows, seq_idx):
        q = head_norm(q_ref[rows, :].astype(F32), gq_ref, qg_ref[...])
        k = head_norm(k_ref[rows, :].astype(F32), gk_ref, kg_ref[...])
        if latent:
            q = rope(q, cosq_ref[...], sinq_ref[...])
            k = rope(k, cosk_ref[...], sink_ref[...])
        q = q * (HEAD_DIM ** -0.5)
        v = v_ref[rows, :].astype(F32)
        n_q = q.shape[0]
        low_kv = lax.broadcasted_iota(jnp.int32, k.shape, 1) < HEAD_DIM
        low_q = lax.broadcasted_iota(jnp.int32, (n_q, pair_w), 1) < HEAD_DIM
        k_swapped = pltpu.roll(k, HEAD_DIM, axis=1)
        v_swapped = pltpu.roll(v, HEAD_DIM, axis=1)
        if not latent:
            kout_ref[seq_idx] = k.T
            vout_ref[seq_idx] = v.T
        units = []
        for j in range(KV_HEADS):
            kd = (jnp.where(low_kv, k, k_swapped) if j == 0 else jnp.where(low_kv, k_swapped, k)).astype(BF16)
            vd = (jnp.where(low_kv, v, v_swapped) if j == 0 else jnp.where(low_kv, v_swapped, v)).astype(BF16)
            vd = jnp.concatenate([vd, jnp.ones_like(vd)], axis=1)
            tiles = range(j * Q_PER_KV // 2, (j + 1) * Q_PER_KV // 2)
            parts = []
            for t in tiles:
                qt = q[:, t * pair_w:(t + 1) * pair_w]
                parts += [jnp.where(low_q, qt, 0.0), jnp.where(low_q, 0.0, qt)]
            units.append(dict(j=j, rows=rows, tiles=tiles, n_q=n_q, low_q=low_q, kd=kd, vd=vd,
                              qs=jnp.concatenate(parts, axis=0).astype(BF16)))
        return units

    def scores(u):
        u["s_new"] = lax.dot_general(u["qs"], u["kd"], nt, preferred_element_type=F32)
        if latent:
            j = u["j"]
            cvd = jnp.concatenate([cv_ref[j], cv_ref[j]], axis=1).astype(BF16)
            u["cvd"] = jnp.concatenate([cvd, jnp.ones_like(cvd)], axis=1)
            ckd = jnp.concatenate([ck_ref[j], ck_ref[j]], axis=1).astype(BF16)
            u["s_old"] = lax.dot_general(u["qs"], ckd, nt, preferred_element_type=F32)

    def softmax(u):
        m = jnp.max(u["s_new"], axis=-1, keepdims=True)
        if latent:
            m = jnp.maximum(m, jnp.max(u["s_old"], axis=-1, keepdims=True))
        u["p_new"] = jnp.exp(u.pop("s_new") - m).astype(BF16)
        if latent:
            u["p_old"] = jnp.exp(u.pop("s_old") - m).astype(BF16)

    def weighted_values(u):
        acc = jnp.dot(u["p_new"], u["vd"], preferred_element_type=F32)
        if latent:
            acc = acc + jnp.dot(u["p_old"], u["cvd"], preferred_element_type=F32)
        out = acc[:, :pair_w] / acc[:, pair_w:]
        n_q = u["n_q"]
        for i, t in enumerate(u["tiles"]):
            lo_head = out[(2 * i) * n_q:(2 * i + 1) * n_q, :]
            hi_head = out[(2 * i + 1) * n_q:(2 * i + 2) * n_q, :]
            o_ref[u["rows"], t * pair_w:(t + 1) * pair_w] = jnp.where(u["low_q"], lo_head, hi_head).astype(o_ref.dtype)

    if latent:
        units = prepare(slice(None), None)
    else:
        seq = PROMPT_LEN
        units = [u for s in range(q_ref.shape[0] // seq) for u in prepare(slice(s * seq, (s + 1) * seq), s)]
    for phase in (scores, softmax, weighted_values):
        for u in units:
            phase(u)


def _attn_common_args(qn_g, kn_g):
    q_w, kv_w = Q_HEADS * HEAD_DIM, KV_HEADS * HEAD_DIM
    return (jnp.tile(qn_g, Q_HEADS).reshape(1, q_w), jnp.tile(kn_g, KV_HEADS).reshape(1, kv_w),
            _head_mean_matrix(q_w), _head_mean_matrix(kv_w))


def _attention_prompt(z, qn_g, kn_g):
    seqs = 4
    L = seqs * PROMPT_LEN
    cache_shape = jax.ShapeDtypeStruct((N_PROMPT_SEQ, KV_HEADS * HEAD_DIM, PROMPT_LEN), F32)
    cache_spec = pl.BlockSpec((seqs, KV_HEADS * HEAD_DIM, PROMPT_LEN), lambda s: (s, 0, 0))
    q_w, kv_w = Q_HEADS * HEAD_DIM, KV_HEADS * HEAD_DIM
    q_col = (5 * A_WIDTH) // q_w
    k_col = (5 * A_WIDTH + q_w) // kv_w
    const = lambda r, c: pl.BlockSpec((r, c), lambda s: (0, 0))
    return pl.pallas_call(
        functools.partial(_attn_kernel, latent=False),
        grid=(N_PROMPT_TOK // L,),
        in_specs=[
            pl.BlockSpec((L, q_w), lambda s: (s, q_col)),
            pl.BlockSpec((L, kv_w), lambda s: (s, k_col)),
            pl.BlockSpec((L, kv_w), lambda s: (s, k_col + 1)),
            const(1, q_w), const(1, kv_w), const(q_w, q_w), const(kv_w, kv_w),
        ],
        out_specs=(pl.BlockSpec((L, q_w), lambda s: (s, 0)), cache_spec, cache_spec),
        out_shape=(jax.ShapeDtypeStruct((N_PROMPT_TOK, q_w), BF16), cache_shape, cache_shape),
        compiler_params=_params("arbitrary"),
        name="attn_prompt",
    )(z, z, z, *_attn_common_args(qn_g, kn_g))


def _attention_latent(z, qn_g, kn_g, cache_k, cache_v):
    L = LATENT_LEN
    nqb = L // Q_BLOCK
    q_w, kv_w = Q_HEADS * HEAD_DIM, KV_HEADS * HEAD_DIM
    q_col = (5 * A_WIDTH) // q_w
    k_col = (5 * A_WIDTH + q_w) // kv_w
    qrow0 = N_PROMPT_TOK // Q_BLOCK
    krow0 = N_PROMPT_TOK // L
    cos, sin = _rope_tables()
    cos_q, sin_q = jnp.asarray(np.tile(cos, (1, Q_HEADS))), jnp.asarray(np.tile(sin, (1, Q_HEADS)))
    cos_k, sin_k = jnp.asarray(np.tile(cos, (1, KV_HEADS))), jnp.asarray(np.tile(sin, (1, KV_HEADS)))
    const = lambda r, c: pl.BlockSpec((r, c), lambda s, b: (0, 0))
    cache_spec = pl.BlockSpec((None, None, KV_HEADS, PAST_LEN, HEAD_DIM), lambda s, b: (s, 0, 0, 0, 0))
    return pl.pallas_call(
        functools.partial(_attn_kernel, latent=True),
        grid=(N_LATENT_SEQ, nqb),
        in_specs=[
            pl.BlockSpec((Q_BLOCK, q_w), lambda s, b: (qrow0 + s * nqb + b, q_col)),
            pl.BlockSpec((L, kv_w), lambda s, b: (krow0 + s, k_col)),
            pl.BlockSpec((L, kv_w), lambda s, b: (krow0 + s, k_col + 1)),
            const(1, q_w), const(1, kv_w), const(q_w, q_w), const(kv_w, kv_w),
            pl.BlockSpec((Q_BLOCK, q_w), lambda s, b: (b, 0)),
            pl.BlockSpec((Q_BLOCK, q_w), lambda s, b: (b, 0)),
            const(L, kv_w), const(L, kv_w),
            cache_spec, cache_spec,
        ],
        out_specs=pl.BlockSpec((Q_BLOCK, q_w), lambda s, b: (s * nqb + b, 0)),
        out_shape=jax.ShapeDtypeStruct((N_LATENT_TOK, q_w), BF16),
        compiler_params=_params("arbitrary", "arbitrary"),
        name="attn_latent",
    )(z, z, z, *_attn_common_args(qn_g, kn_g), cos_q, sin_q, cos_k, sin_k, cache_k, cache_v)


def _out_proj_kernel(*refs, n_act, n_x):
    a_refs = refs[:2 * n_act]
    x_refs = refs[2 * n_act:2 * n_act + n_x]
    g_ref, mod_ref, rw_ref, w_ref, xo_ref, h_ref, lg_ref, wb_ref, rws_ref, acc_ref = refs[2 * n_act + n_x:]
    _cast_once(w_ref, wb_ref)

    @pl.when(pl.program_id(0) == 0)
    def _():
        rw = rw_ref[...]
        hi = rw.astype(BF16).astype(F32)
        lo = (rw - hi).astype(BF16).astype(F32)
        rws_ref[...] = (hi + pltpu.roll(lo, N_EXPERTS, axis=1)).astype(BF16)

    mod = mod_ref[...]
    n = OUT_PROJ_SUB_ROWS

    n_sub = xo_ref.shape[0] // n

    def sub_rows(r):
        if isinstance(r, int):
            return slice(r * n, (r + 1) * n)
        return pl.ds(pl.multiple_of(r * n, n), n)

    def project(r):
        rows = sub_rows(r)
        acc = None
        k0 = 0
        for ap_ref, al_ref in zip(a_refs[0::2], a_refs[1::2]):
            k1 = k0 + ap_ref.shape[1]
            part = jnp.dot(_select_trunk(ap_ref, al_ref, rows), wb_ref[k0:k1, :], preferred_element_type=F32)
            acc = part if acc is None else acc + part
            k0 = k1
        acc_ref[r % 2] = acc

    def finish(r):
        rows = sub_rows(r)
        x_in = x_refs[0][rows, :] if n_x == 1 else _select_trunk(*x_refs, rows)
        x = x_in + mod[2:3, :] * acc_ref[r % 2]
        xo_ref[rows, :] = x
        h = _modulated_norm(x, g_ref[...], mod, 3, 4)
        h_ref[rows, :] = _pack_rows(h)
        h_hi = h.astype(BF16)
        h_lo = (h - h_hi.astype(F32)).astype(BF16)
        both = jnp.dot(jnp.concatenate([h_hi, h_lo], axis=0), rws_ref[...], preferred_element_type=F32)
        from_hi, from_lo = both[:n], both[n:]
        lg = from_hi + pltpu.roll(from_hi, ROUTER_LANES - N_EXPERTS, axis=1) + from_lo
        lg_ref[:, rows] = lg.T[:N_EXPERTS, :]

    project(0)
    for r in range(n_sub - 1):
        project(r + 1)
        finish(r)
    finish(n_sub - 1)


def _out_proj(acts, w, xs, g, mod_l, router_wp, block_rows=1024):
    tok = lambda width: pl.BlockSpec((block_rows, width), lambda i: (i, 0))
    in_specs = [spec for ap, _ in acts for spec in _trunk_specs(block_rows, ap.shape[1])]
    in_specs += [tok(D_MODEL)] if len(xs) == 1 else list(_trunk_specs(block_rows, D_MODEL))
    in_specs += [_resident((1, D_MODEL)), _mod_spec(block_rows), _resident((D_MODEL, ROUTER_LANES)),
                 _resident(w.shape)]
    return pl.pallas_call(
        functools.partial(_out_proj_kernel, n_act=len(acts), n_x=len(xs)),
        grid=(N_TOK // block_rows,),
        in_specs=in_specs,
        out_specs=(tok(D_MODEL), tok(ROW_WORDS), pl.BlockSpec((N_EXPERTS, block_rows), lambda i: (0, i))),
        out_shape=(jax.ShapeDtypeStruct((N_TOK, D_MODEL), F32),
                   jax.ShapeDtypeStruct((N_TOK, ROW_WORDS), jnp.int32),
                   jax.ShapeDtypeStruct((N_EXPERTS, N_TOK), F32)),
        scratch_shapes=[pltpu.VMEM(w.shape, BF16), pltpu.VMEM((D_MODEL, ROUTER_LANES), BF16),
                        pltpu.VMEM((2, OUT_PROJ_SUB_ROWS, D_MODEL), F32)],
        compiler_params=_params("arbitrary"),
        name="out_proj",
    )(*[a for pair in acts for a in pair], *xs, g.reshape(1, D_MODEL), mod_l, router_wp, w)


def _router_kernel(lg_ref, rb_ref, pos_ref, w_ref, plan_ref, rank_ref):
    lg = lg_ref[...]
    ex = jnp.exp(lg - jnp.max(lg, axis=0, keepdims=True))
    scores = ex / jnp.sum(ex, axis=0, keepdims=True)
    biased = scores + rb_ref[...]
    rows = [biased[e:e + 1, :] for e in range(N_EXPERTS)]
    selected = []
    group_score = []
    for gi in range(N_GROUPS):
        r = rows[gi * EXPERTS_PER_GROUP:(gi + 1) * EXPERTS_PER_GROUP]
        total = None
        for i in range(EXPERTS_PER_GROUP):
            rank = None
            for j in range(EXPERTS_PER_GROUP):
                if j == i:
                    continue
                ahead = (r[j] > r[i]) if j > i else (r[j] >= r[i])
                ahead = jnp.where(ahead, 1.0, 0.0)
                rank = ahead if rank is None else rank + ahead
            sel = rank < 1.5
            selected.append(sel)
            contrib = jnp.where(sel, r[i], 0.0)
            total = contrib if total is None else total + contrib
        group_score.append(total)
    best = group_score[0]
    best_group = jnp.zeros_like(best)
    for gi in range(1, N_GROUPS):
        better = group_score[gi] > best
        best_group = jnp.where(better, float(gi), best_group)
        best = jnp.where(better, group_score[gi], best)
    picked = []
    chosen = []
    den = None
    for e in range(N_EXPERTS):
        in_group = best_group == float(e // EXPERTS_PER_GROUP)
        use = jnp.where(selected[e], jnp.where(in_group, 1.0, 0.0), 0.0)
        w = use * scores[e:e + 1, :]
        chosen.append(use)
        picked.append(w)
        den = w if den is None else den + w
    lanes = 128
    n_blk = N_TOK // lanes
    li = lax.broadcasted_iota(jnp.int32, (lanes, lanes), 0)
    lj = lax.broadcasted_iota(jnp.int32, (lanes, lanes), 1)
    prefix = jnp.where(li <= lj, 1.0, 0.0).astype(BF16)
    carry = jnp.zeros((N_EXPERTS, 1), F32)
    for blk in range(n_blk):
        cols = slice(blk * lanes, (blk + 1) * lanes)
        m = jnp.concatenate([chosen[e][:, cols] for e in range(N_EXPERTS)], axis=0)
        incl = jnp.dot(m.astype(BF16), prefix, preferred_element_type=F32)
        rank_ref[:, cols] = incl - m + carry
        carry = carry + incl[:, lanes - 1:lanes]
    count = carry
    padded = jnp.floor((count + float(MOE_TILE - 1)) * (1.0 / MOE_TILE)) * float(MOE_TILE)
    erow = lax.broadcasted_iota(jnp.int32, (N_EXPERTS, 1), 0)
    offset = jnp.zeros((N_EXPERTS, 1), F32)
    for e in range(N_EXPERTS - 1):
        offset = offset + jnp.where(erow > e, padded[e:e + 1, :], 0.0)
    seen = jnp.zeros_like(den)
    pos_a = jnp.zeros_like(den)
    pos_b = jnp.zeros_like(den)
    w_a = jnp.zeros_like(den)
    w_b = jnp.zeros_like(den)
    for e in range(N_EXPERTS):
        pos_e = rank_ref[e:e + 1, :] + offset[e:e + 1, :]
        gate_e = picked[e] / den
        first = jnp.where(seen < 0.5, chosen[e], 0.0) > 0.5
        second = jnp.where(seen > 0.5, chosen[e], 0.0) > 0.5
        pos_a = jnp.where(first, pos_e, pos_a)
        w_a = jnp.where(first, gate_e, w_a)
        pos_b = jnp.where(second, pos_e, pos_b)
        w_b = jnp.where(second, gate_e, w_b)
        seen = seen + chosen[e]
    pos_ref[0:1, :] = pos_a.astype(jnp.int32)
    pos_ref[1:2, :] = pos_b.astype(jnp.int32)
    w_rows = jnp.concatenate([w_a, w_b, jnp.zeros((6, N_TOK), F32)], axis=0)
    ei = lax.broadcasted_iota(jnp.int32, (8, lanes), 0)
    ej = lax.broadcasted_iota(jnp.int32, (8, lanes), 1)
    eye = jnp.where(ei == ej, 1.0, 0.0).astype(BF16)
    tn = (((0,), (0,)), ((), ()))
    hi = w_rows.astype(BF16)
    r1 = w_rows - hi.astype(F32)
    mid = r1.astype(BF16)
    lo = (r1 - mid.astype(F32)).astype(BF16)
    w_cols = lax.dot_general(hi, eye, tn, preferred_element_type=F32)
    w_cols = w_cols + lax.dot_general(mid, eye, tn, preferred_element_type=F32)
    w_cols = w_cols + lax.dot_general(lo, eye, tn, preferred_element_type=F32)
    w_ref[...] = w_cols[:, :TOP_K]
    start = (lax.broadcasted_iota(jnp.int32, (N_EXPERTS, lanes), 1) * MOE_TILE).astype(F32)
    end = offset + padded
    tile_expert = jnp.sum(jnp.where(end <= start, 1.0, 0.0), axis=0, keepdims=True)
    inside = (offset <= start) & (start < end)
    real = jnp.clip(count - (start - offset), 0.0, float(MOE_TILE))
    tile_rows = jnp.sum(jnp.where(inside, real, 0.0), axis=0, keepdims=True)
    plan_ref[0:1, :] = jnp.minimum(tile_expert, float(N_EXPERTS - 1)).astype(jnp.int32)
    plan_ref[1:2, :] = tile_rows.astype(jnp.int32)


def _router(logits_t, router_b):
    whole = lambda shape: pl.BlockSpec(shape, lambda i: (0, 0))
    return pl.pallas_call(
        _router_kernel,
        grid=(1,),
        in_specs=[whole((N_EXPERTS, N_TOK)), whole((N_EXPERTS, 1))],
        out_specs=(whole((2, N_TOK)), whole((N_TOK, TOP_K)), whole((2, 128))),
        out_shape=(jax.ShapeDtypeStruct((2, N_TOK), jnp.int32),
                   jax.ShapeDtypeStruct((N_TOK, TOP_K), F32),
                   jax.ShapeDtypeStruct((2, 128), jnp.int32)),
        scratch_shapes=[pltpu.VMEM((N_EXPERTS, N_TOK), F32)],
        compiler_params=_params("arbitrary"),
        name="router",
    )(logits_t, router_b.reshape(N_EXPERTS, 1))


def _sc_mesh():
    return plsc.VectorSubcoreMesh(core_axis_name="c", subcore_axis_name="s")


def _sc_worker_base():
    return (lax.axis_index("s") * SC_CORES + lax.axis_index("c")) * SC_TOKENS_PER_WORKER


def _moe_dispatch(h, pos_a, pos_b):
    n_chunks = SC_TOKENS_PER_WORKER // SC_CHUNK
    idx = pltpu.VMEM((SC_CHUNK,), jnp.int32)

    @functools.partial(
        pl.kernel, mesh=_sc_mesh(),
        out_type=jax.ShapeDtypeStruct((MOE_ROWS, ROW_WORDS), jnp.int32),
        scratch_types=[idx, idx, idx, idx, pltpu.VMEM((2, SC_CHUNK, ROW_WORDS), jnp.int32),
                       pltpu.SemaphoreType.DMA((6,)), pltpu.SemaphoreType.DMA((4,))],
        name="moe_dispatch",
    )
    def run(h_hbm, pa_hbm, pb_hbm, xs_hbm, ia0, ib0, ia1, ib1, rows_v, sem_in, sem_out):
        base = _sc_worker_base()
        ia, ib = (ia0, ia1), (ib0, ib1)

        def start_loads(c):
            slot = c % 2
            tok = pl.ds(pl.multiple_of(base + c * SC_CHUNK, 8), SC_CHUNK)
            return (pltpu.async_copy(pa_hbm.at[tok], ia[slot], sem_in.at[3 * slot]),
                    pltpu.async_copy(pb_hbm.at[tok], ib[slot], sem_in.at[3 * slot + 1]),
                    pltpu.async_copy(h_hbm.at[tok], rows_v.at[slot], sem_in.at[3 * slot + 2]))

        loads = start_loads(0)
        scatters = [(), ()]
        for c in range(n_chunks):
            slot = c % 2
            for cp in loads:
                cp.wait()
            if c + 1 < n_chunks:
                for cp in scatters[1 - slot]:
                    cp.wait()
                scatters[1 - slot] = ()
                loads = start_loads(c + 1)
            scatters[slot] = (pltpu.async_copy(rows_v.at[slot], xs_hbm.at[ia[slot]], sem_out.at[2 * slot]),
                              pltpu.async_copy(rows_v.at[slot], xs_hbm.at[ib[slot]], sem_out.at[2 * slot + 1]))
        for pending in scatters:
            for cp in pending:
                cp.wait()

    return run(h, pos_a, pos_b)


def _moe_collect(ys, pos_a, pos_b):
    n_chunks = SC_TOKENS_PER_WORKER // SC_CHUNK
    out = jax.ShapeDtypeStruct((N_TOK, ROW_WORDS), jnp.int32)
    idx = pltpu.VMEM((SC_TOKENS_PER_WORKER,), jnp.int32)
    rows = pltpu.VMEM((2, SC_CHUNK, ROW_WORDS), jnp.int32)

    @functools.partial(
        pl.kernel, mesh=_sc_mesh(), out_type=(out, out),
        scratch_types=[idx, idx, rows, rows, pltpu.SemaphoreType.DMA((4,)), pltpu.SemaphoreType.DMA((4,))],
        name="moe_collect",
    )
    def run(ys_hbm, pa_hbm, pb_hbm, ya_hbm, yb_hbm, ia_v, ib_v, ra_v, rb_v, sem_g, sem_w):
        base = _sc_worker_base()
        mine = pl.ds(pl.multiple_of(base, 8), SC_TOKENS_PER_WORKER)
        pltpu.sync_copy(pa_hbm.at[mine], ia_v)
        pltpu.sync_copy(pb_hbm.at[mine], ib_v)
        writes = [(), ()]
        for c in range(n_chunks):
            slot = c % 2
            for cp in writes[slot]:
                cp.wait()
            part = pl.ds(c * SC_CHUNK, SC_CHUNK)
            tok = pl.ds(pl.multiple_of(base + c * SC_CHUNK, 8), SC_CHUNK)
            ga = pltpu.async_copy(ys_hbm.at[ia_v.at[part]], ra_v.at[slot], sem_g.at[slot])
            gb = pltpu.async_copy(ys_hbm.at[ib_v.at[part]], rb_v.at[slot], sem_g.at[2 + slot])
            ga.wait()
            wa = pltpu.async_copy(ra_v.at[slot], ya_hbm.at[tok], sem_w.at[slot])
            gb.wait()
            wb = pltpu.async_copy(rb_v.at[slot], yb_hbm.at[tok], sem_w.at[2 + slot])
            writes[slot] = (wa, wb)
        for pending in writes:
            for cp in pending:
                cp.wait()

    return run(ys, pos_a, pos_b)


def _experts_kernel(plan_ref, xs_ref, wg_hbm, wu_hbm, wd_hbm, y_ref,
                    sg_ref, su_ref, sd_ref, wgb_ref, wub_ref, wdb_ref, hid_ref, sems, seg_ref, *, layer):
    j = pl.program_id(0)
    n_tiles = pl.num_programs(0)
    expert = plan_ref[j]
    n_real = plan_ref[PLAN_LANES + j]
    fresh = jnp.logical_or(j == 0, expert != plan_ref[jnp.maximum(j - 1, 0)])

    def weight_copies(e, slot):
        return (pltpu.make_async_copy(wg_hbm.at[layer, e], sg_ref.at[slot], sems.at[slot, 0]),
                pltpu.make_async_copy(wu_hbm.at[layer, e], su_ref.at[slot], sems.at[slot, 1]),
                pltpu.make_async_copy(wd_hbm.at[layer, e], sd_ref.at[slot], sems.at[slot, 2]))

    @pl.when(j == 0)
    def _():
        seg_ref[0] = 0

        @pl.when(n_real > 0)
        def _():
            for cp in weight_copies(expert, 0):
                cp.start()

    @pl.when(jnp.logical_and(n_real > 0, fresh))
    def _():
        slot = seg_ref[0] % 2
        for cp in weight_copies(expert, slot):
            cp.wait()
        wgb_ref[...] = sg_ref[slot].astype(BF16)
        wub_ref[...] = su_ref[slot].astype(BF16)
        wdb_ref[...] = sd_ref[slot].astype(BF16)
        nxt = lax.while_loop(lambda t: jnp.logical_and(t < n_tiles, plan_ref[jnp.minimum(t, n_tiles - 1)] == expert),
                             lambda t: t + 1, j + 1)
        nxt_c = jnp.minimum(nxt, n_tiles - 1)

        @pl.when(jnp.logical_and(nxt < n_tiles, plan_ref[PLAN_LANES + nxt_c] > 0))
        def _():
            for cp in weight_copies(plan_ref[nxt_c], 1 - slot):
                cp.start()

        seg_ref[0] = seg_ref[0] + 1

    @pl.when(n_real > 0)
    def _():
        n = EXPERT_SUB_ROWS
        n_sub = xs_ref.shape[0] // n
        row = lax.broadcasted_iota(jnp.int32, (n, xs_ref.shape[1]), 0)

        def up(r):
            rows = slice(r * n, (r + 1) * n)
            words = jnp.where(row < n_real - r * n, xs_ref[rows, :], 0)
            x = _unpack_rows(words).astype(BF16)
            a = jnp.dot(x, wgb_ref[...], preferred_element_type=F32)
            b = jnp.dot(x, wub_ref[...], preferred_element_type=F32)
            hid_ref[r] = ((a * jax.nn.sigmoid(a)) * b).astype(BF16)

        def down(r):
            rows = slice(r * n, (r + 1) * n)
            y_ref[rows, :] = _pack_rows(jnp.dot(hid_ref[r], wdb_ref[...], preferred_element_type=F32))

        up(0)
        for r in range(1, n_sub):
            up(r)
            down(r - 1)
        down(n_sub - 1)


def _experts(plan, xs, w_gate, w_up, w_down, layer):
    hbm = pl.BlockSpec(memory_space=pl.ANY)
    return pl.pallas_call(
        functools.partial(_experts_kernel, layer=layer),
        grid_spec=pltpu.PrefetchScalarGridSpec(
            num_scalar_prefetch=1,
            grid=(MOE_ROWS // MOE_TILE,),
            in_specs=[pl.BlockSpec((MOE_TILE, ROW_WORDS), lambda j, plan: (j, 0)), hbm, hbm, hbm],
            out_specs=pl.BlockSpec((MOE_TILE, ROW_WORDS), lambda j, plan: (j, 0)),
            scratch_shapes=[pltpu.VMEM((2, D_MODEL, D_EXPERT), F32), pltpu.VMEM((2, D_MODEL, D_EXPERT), F32),
                            pltpu.VMEM((2, D_EXPERT, D_MODEL), F32),
                            pltpu.VMEM((D_MODEL, D_EXPERT), BF16), pltpu.VMEM((D_MODEL, D_EXPERT), BF16),
                            pltpu.VMEM((D_EXPERT, D_MODEL), BF16),
                            pltpu.VMEM((MOE_TILE // EXPERT_SUB_ROWS, EXPERT_SUB_ROWS, D_EXPERT), BF16),
                            pltpu.SemaphoreType.DMA((2, 3)), pltpu.SMEM((1,), jnp.int32)],
        ),
        out_shape=jax.ShapeDtypeStruct((MOE_ROWS, ROW_WORDS), jnp.int32),
        compiler_params=_params("arbitrary"),
        name="experts",
    )(plan, xs, w_gate, w_up, w_down)


def _combine_kernel(x_ref, ya_ref, yb_ref, wt_ref, mod_ref, o_ref):
    o_ref[...] = _moe_mix(x_ref, ya_ref, yb_ref, wt_ref, mod_ref)


def _combine(x, moe_out, mod_l, tok0, n_tok, block_rows=512):
    ya, yb, w_tok = moe_out
    b0 = tok0 // block_rows
    rows = lambda width: pl.BlockSpec((block_rows, width), lambda i: (b0 + i, 0))
    return pl.pallas_call(
        _combine_kernel,
        grid=(n_tok // block_rows,),
        in_specs=[rows(D_MODEL), rows(ROW_WORDS), rows(ROW_WORDS), rows(TOP_K),
                  pl.BlockSpec((None, 6, D_MODEL), lambda i: (_cond_of_token_block(b0 + i, block_rows), 0, 0))],
        out_specs=pl.BlockSpec((block_rows, D_MODEL), lambda i: (i, 0)),
        out_shape=jax.ShapeDtypeStruct((n_tok, D_MODEL), F32),
        compiler_params=_params("arbitrary"),
        name="combine",
    )(x, ya, yb, w_tok, mod_l)


def _moe(h, logits_t, router_b, w_gate, w_up, w_down, layer):
    pos, w, plan = _router(logits_t, router_b)
    xs = _moe_dispatch(h, pos[0], pos[1])
    ys = _experts(plan.reshape(-1), xs, w_gate, w_up, w_down, layer)
    ya, yb = _moe_collect(ys, pos[0], pos[1])
    return (ya, yb, w), plan, ys


def _dft_tables(L):
    k = np.arange(L)[:, None]
    m = np.arange(L)[None, :]
    r = (k * m) % (2 * L)
    ang = np.pi * r.astype(np.float64) / L
    fc = np.cos(ang)
    fs = np.sin(ang)
    fs[0, :] = np.where(np.arange(L) % 2 == 0, 1.0, -1.0)
    wk = np.full((L, 1), 1.0 / L)
    wk[0, 0] = 0.5 / L
    gc = (fc * wk).T
    gs = (fs * wk).T
    return [jnp.asarray(t.astype(np.float32)).astype(BF16) for t in (fc, fs, gc, gs)]


def _filter_consts(L):
    t = np.linspace(0.0, 1.0, L, dtype=np.float32)[:, None]
    w = (np.float32(2.0 * np.pi) * np.arange(L, dtype=np.float32)[:, None] / np.float32(L)).astype(np.float32)
    fb = np.linspace(1e-4, HY_BANDS - 1, HY_BANDS, dtype=np.float32)[None, :]
    emb = np.concatenate([t, np.cos(fb * w), -np.sin(fb * w)], axis=-1).astype(np.float32)
    lo = math.log(HY_DECAY_TARGET) / HY_SLOW_PCT
    hi = math.log(HY_DECAY_TARGET) / HY_FAST_PCT
    deltas = np.abs(np.linspace(lo, hi, D_MODEL, dtype=np.float32))
    decay = np.exp(-t * deltas).astype(np.float32)
    return jnp.asarray(emb), jnp.asarray(decay)


def _filter_kernel(emb_ref, w1_ref, b1_ref, w2_ref, b2_ref, fr_ref, w3f_ref, w3b_ref, dec_ref,
                   fc_ref, fs_ref, *rest):
    kr_ref, q_ref, krn_ref, hd_ref = rest[-4:]

    @pl.when(pl.program_id(0) == 0)
    def _():
        fr = fr_ref[...]
        h1 = jnp.sin(fr * (jnp.dot(emb_ref[...], w1_ref[...], precision=HIGHEST,
                                   preferred_element_type=F32) + b1_ref[...]))
        hd_ref[...] = jnp.sin(fr * (jnp.dot(h1, w2_ref[...], precision=HIGHEST,
                                            preferred_element_type=F32) + b2_ref[...]))

    hd = hd_ref[...]
    dec = dec_ref[...]
    f = jnp.dot(hd, w3f_ref[...], precision=HIGHEST, preferred_element_type=F32) * dec
    g = jnp.dot(hd, w3b_ref[...], precision=HIGHEST, preferred_element_type=F32) * dec
    row = lax.broadcasted_iota(jnp.int32, f.shape, 0)
    g = jnp.where(row == 0, 0.0, g)
    s = f + g
    d = f - g
    kr = jnp.dot(fc_ref[...], s.astype(BF16), preferred_element_type=F32)
    qq = jnp.dot(fs_ref[...], d.astype(BF16), preferred_element_type=F32)
    alt = jnp.where(row % 2 == 0, 1.0, -1.0)
    nyq = jnp.sum(alt * s, axis=0, keepdims=True)
    kr_ref[...] = kr
    q_ref[...] = jnp.where(row == 0, 0.0, qq)
    krn_ref[...] = jnp.where(row == 0, nyq, kr)


def _hyena_filter_spectrum(L, w1, b1, w2, b2, w3, freq, fc, fs, after=None, cblk=256):
    emb, decay = _filter_consts(L)
    anchor_specs, anchor_args = _after(after)
    ncb = D_MODEL // cblk
    n_emb = 128
    emb = jnp.pad(emb, ((0, 0), (0, n_emb - emb.shape[1])))
    w1 = jnp.pad(w1, ((0, n_emb - w1.shape[0]), (0, 0)))
    full = lambda shape: pl.BlockSpec(shape, lambda j: tuple(0 for _ in shape))
    out_sds = jax.ShapeDtypeStruct((L, D_MODEL), F32)
    out_spec = pl.BlockSpec((L, cblk), lambda j: (0, j))
    return pl.pallas_call(
        _filter_kernel,
        grid=(ncb,),
        in_specs=[
            full((L, n_emb)), full((n_emb, HY_FFN)), full((1, HY_FFN)), full((HY_FFN, HY_FFN)),
            full((1, HY_FFN)), full((1, HY_FFN)),
            pl.BlockSpec((HY_FFN, cblk), lambda j: (0, j)),
            pl.BlockSpec((HY_FFN, cblk), lambda j: (0, ncb + j)),
            pl.BlockSpec((L, cblk), lambda j: (0, j)),
            full((L, L)), full((L, L)), *anchor_specs,
        ],
        out_specs=(out_spec, out_spec, out_spec),
        out_shape=(out_sds, out_sds, out_sds),
        scratch_shapes=[pltpu.VMEM((L, HY_FFN), F32)],
        compiler_params=_params("arbitrary"),
        name=f"hyena_filter_{L}",
    )(emb, w1, b1.reshape(1, HY_FFN), w2, b2.reshape(1, HY_FFN), freq.reshape(1, HY_FFN), w3, w3, decay, fc, fs,
      *anchor_args)


def _hyena_conv_kernel(x0_ref, x1_ref, v_ref, cw0_ref, cw1_ref, cwv_ref, cb0_ref, cb1_ref, cbv_ref,
                       kr_ref, q_ref, krn_ref, ds_ref, fc_ref, fs_ref, gc_ref, gs_ref, o_ref,
                       zz_ref, gate_ref, skip_ref, yr_ref, yw_ref):
    L = fc_ref.shape[0]
    unit_w = zz_ref.shape[2]
    units = [(slice(s * L, (s + 1) * L), slice(c * unit_w, (c + 1) * unit_w))
             for s in range(x0_ref.shape[0] // L) for c in range(x0_ref.shape[1] // unit_w)]
    row = lax.broadcasted_iota(jnp.int32, (L, unit_w), 0)

    def gating(i):
        rows, cols = units[i]

        def short_conv(u_ref, w_ref, b_ref):
            u = u_ref[rows, cols].astype(F32)
            w = w_ref[:, cols]
            prev = jnp.where(row == 0, 0.0, pltpu.roll(u, 1, axis=0))
            nxt = jnp.where(row == L - 1, 0.0, pltpu.roll(u, L - 1, axis=0))
            return prev * w[0:1, :] + u * w[1:2, :] + nxt * w[2:3, :] + b_ref[:, cols]

        x0 = short_conv(x0_ref, cw0_ref, cb0_ref)
        zz = short_conv(v_ref, cwv_ref, cbv_ref) * short_conv(x1_ref, cw1_ref, cb1_ref)
        zz_ref[i] = zz.astype(BF16)
        gate_ref[i] = x0
        skip_ref[i] = x0 * zz * ds_ref[:, cols]

    def spectrum(i):
        cols = units[i][1]
        ur = jnp.dot(fc_ref[...], zz_ref[i], preferred_element_type=F32)
        p = jnp.dot(fs_ref[...], zz_ref[i], preferred_element_type=F32)
        qq = q_ref[:, cols]
        yr_ref[i] = (ur * kr_ref[:, cols] - p * qq).astype(BF16)
        yw_ref[i] = (ur * qq + p * krn_ref[:, cols]).astype(BF16)

    def synthesis(i):
        rows, cols = units[i]
        y = jnp.dot(gc_ref[...], yr_ref[i], preferred_element_type=F32)
        y = y + jnp.dot(gs_ref[...], yw_ref[i], preferred_element_type=F32)
        o_ref[rows, cols] = (gate_ref[i] * y + skip_ref[i]).astype(o_ref.dtype)

    for t in range(len(units) + 2):
        if t < len(units):
            gating(t)
        if 0 <= t - 1 < len(units):
            spectrum(t - 1)
        if 0 <= t - 2 < len(units):
            synthesis(t - 2)


def _hyena_conv(u, conv_w, conv_b, dskip, spectrum, tables, *, latent):
    L = LATENT_LEN if latent else PROMPT_LEN
    n_seq = N_LATENT_SEQ if latent else N_PROMPT_SEQ
    cblk = 512
    unit_w = 256 if latent else 512
    ncb = D_MODEL // cblk
    seqs = 1 if latent else 8
    unit = (seqs * cblk // unit_w, L, unit_w)
    row0 = (N_PROMPT_TOK // L) if latent else 0
    kr, qq, krn = spectrum
    fc, fs, gc, gs = tables

    def part(p, rows):
        if rows != L:
            return pl.BlockSpec((rows, cblk), lambda j, s: (0, p * ncb + j))
        return pl.BlockSpec((seqs * L, cblk), lambda j, s: (row0 // seqs + s, p * ncb + j))

    def const_cols(rows):
        return pl.BlockSpec((rows, cblk), lambda j, s: (0, j))

    mat = pl.BlockSpec((L, L), lambda j, s: (0, 0))
    conv_b2 = conv_b.reshape(1, 3 * D_MODEL)
    in_specs = [part(0, L), part(1, L), part(2, L),
                part(0, 3), part(1, 3), part(2, 3),
                part(0, 1), part(1, 1), part(2, 1),
                const_cols(L), const_cols(L), const_cols(L), const_cols(1),
                mat, mat, mat, mat]
    args = [u, u, u, conv_w, conv_w, conv_w, conv_b2, conv_b2, conv_b2,
            kr, qq, krn, dskip.reshape(1, D_MODEL), fc, fs, gc, gs]
    return pl.pallas_call(
        _hyena_conv_kernel,
        grid=(ncb, n_seq // seqs),
        in_specs=in_specs,
        out_specs=pl.BlockSpec((seqs * L, cblk), lambda j, s: (s, j)),
        out_shape=jax.ShapeDtypeStruct((n_seq * L, D_MODEL), BF16),
        scratch_shapes=[pltpu.VMEM(unit, BF16), pltpu.VMEM(unit, F32), pltpu.VMEM(unit, F32),
                        pltpu.VMEM(unit, BF16), pltpu.VMEM(unit, BF16)],
        compiler_params=_params("arbitrary", "arbitrary"),
        name="hyena_conv_latent" if latent else "hyena_conv_prompt",
    )(*args)


def kernel(x_prompt, x_sample, cache_k, cache_v, state_hgrn, c, c_ctx, norm_g, mod_w, mod_b, ab_in_w, hgrn_lb, hgrn_onorm_g, attn_qnorm_g, attn_knorm_g, ab_out_w, hy_in_w, hy_in_b, hy_conv_w, hy_conv_b, hy_f_w1, hy_f_b1, hy_f_w2, hy_f_b2, hy_f_w3, hy_f_freq, hy_dskip, hy_out_w, router_w, router_b, moe_w_gate, moe_w_up, moe_w_down):
    xp = x_prompt.reshape(N_PROMPT_TOK, D_MODEL)
    xl = x_sample.reshape(N_LATENT_TOK, D_MODEL)
    cond = jnp.concatenate([c_ctx[None, :], c, jnp.zeros((N_COND - 1 - N_LATENT_SEQ, D_MODEL), F32)], axis=0)
    mod = [_modulation(cond, mod_w, mod_b, 0), None]
    router_wp = jnp.pad(router_w, ((0, 0), (0, ROUTER_LANES - N_EXPERTS)))

    z = _in_proj0(xp, xl, norm_g[0, 0], mod[0], ab_in_w[0])
    oa_p, new_state = _hgrn(z, hgrn_lb, hgrn_onorm_g[0], None, latent=False)
    oa_l = _hgrn(z, hgrn_lb, hgrn_onorm_g[0], state_hgrn, latent=True)
    ob_p, k_fm, v_fm = _attention_prompt(z, attn_qnorm_g[0], attn_knorm_g[0])
    fm_shape = (N_PROMPT_SEQ, 1, KV_HEADS, HEAD_DIM, PROMPT_LEN)
    new_k = jnp.swapaxes(k_fm.reshape(fm_shape), -1, -2)
    new_v = jnp.swapaxes(v_fm.reshape(fm_shape), -1, -2)
    ob_l = _attention_latent(z, attn_qnorm_g[0], attn_knorm_g[0], cache_k, cache_v)
    x, h, logits_t = _out_proj([(oa_p, oa_l), (ob_p, ob_l)], ab_out_w[0], (xp, xl), norm_g[0, 1], mod[0],
                               router_wp)
    moe_out, routed, computed = _moe(h, logits_t, router_b, moe_w_gate, moe_w_up, moe_w_down, 0)

    tables = {L: _dft_tables(L) for L in (PROMPT_LEN, LATENT_LEN)}
    anchors = {LATENT_LEN: routed, PROMPT_LEN: computed}
    spectra = {L: _hyena_filter_spectrum(L, hy_f_w1[0], hy_f_b1[0], hy_f_w2[0], hy_f_b2[0], hy_f_w3[0],
                                         hy_f_freq[0], tables[L][0], tables[L][1], after=anchors[L])
               for L in (LATENT_LEN, PROMPT_LEN)}
    mod[1] = _modulation(cond, mod_w, mod_b, 1, after=computed)

    x, u = _in_proj1(x, moe_out, mod[0], norm_g[1, 0], mod[1], hy_in_w[0], hy_in_b[0])
    pre = []
    for latent in (False, True):
        L = LATENT_LEN if latent else PROMPT_LEN
        pre.append(_hyena_conv(u, hy_conv_w[0], hy_conv_b[0], hy_dskip[0], spectra[L], tables[L], latent=latent))
    x, h, logits_t = _out_proj([tuple(pre)], hy_out_w[0], (x,), norm_g[1, 1], mod[1], router_wp)
    moe_out, _, _ = _moe(h, logits_t, router_b, moe_w_gate, moe_w_up, moe_w_down, 1)

    y_prompt = _combine(x, moe_out, mod[1], 0, N_PROMPT_TOK).reshape(N_PROMPT_SEQ, PROMPT_LEN, D_MODEL)
    y_sample = _combine(x, moe_out, mod[1], N_PROMPT_TOK, N_LATENT_TOK).reshape(N_LATENT_SEQ, LATENT_LEN, D_MODEL)
    return (y_prompt, y_sample, new_k, new_v, new_state)
```

```python
import functools
import math

import numpy as np
import jax
import jax.numpy as jnp
from jax import lax
from jax.experimental import pallas as pl
from jax.experimental.pallas import tpu as pltpu
from jax.experimental.pallas import tpu_sc as plsc

F32 = jnp.float32
BF16 = jnp.bfloat16
HIGHEST = lax.Precision.HIGHEST

D_MODEL = 1024
N_PROMPT_SEQ = 32
PROMPT_LEN = 256
N_LATENT_SEQ = 2
LATENT_LEN = 1024
PAST_LEN = 512
GRID_W = 64
N_PROMPT_TOK = N_PROMPT_SEQ * PROMPT_LEN
N_LATENT_TOK = N_LATENT_SEQ * LATENT_LEN
N_TOK = N_PROMPT_TOK + N_LATENT_TOK
N_COND = 8
EPS = 1e-6

A_WIDTH = 512
A_HEADS = 4
A_DK = 128
CHUNK = 64
HGRN_BLOCK = 128
HGRN_HEADS_PER_STEP = 4
HEAD_DIM = 64
Q_HEADS = 8
KV_HEADS = 2
Q_PER_KV = Q_HEADS // KV_HEADS
Q_BLOCK = 256
ROPE_THETA = 10000.0
ROPE_PAIRS = HEAD_DIM // 4
AB_IN = 5 * A_WIDTH + (Q_HEADS + 2 * KV_HEADS) * HEAD_DIM

HY_BANDS = 16
HY_FFN = 64
HY_DECAY_TARGET = 1e-2
HY_FAST_PCT = 0.3
HY_SLOW_PCT = 1.5

N_EXPERTS = 16
N_GROUPS = 4
EXPERTS_PER_GROUP = 4
TOP_K = 2
D_EXPERT = 512
ROUTER_LANES = 128
OUT_PROJ_SUB_ROWS = 256
EXPERT_SUB_ROWS = 256
IN_PROJ_SUB_ROWS = 256
MOE_TILE = 512
MOE_ROWS = N_TOK * TOP_K + N_EXPERTS * MOE_TILE
PLAN_LANES = 128

SC_CORES = 2
SC_WORKERS = 32
SC_TOKENS_PER_WORKER = N_TOK // SC_WORKERS
SC_CHUNK = 40
ROW_WORDS = D_MODEL // 2

VMEM_LIMIT = 56 * 1024 * 1024


def _params(*sem):
    return pltpu.CompilerParams(dimension_semantics=sem, vmem_limit_bytes=VMEM_LIMIT)


def _pack_rows(x):
    n = x.shape[1] // 2
    bits = pltpu.bitcast(x.astype(BF16).astype(F32), jnp.uint32)
    return pltpu.bitcast(bits[:, :n] | (bits[:, n:] >> 16), jnp.int32)


def _unpack_rows(p):
    bits = pltpu.bitcast(p, jnp.uint32)
    hi = pltpu.bitcast(bits & jnp.uint32(0xFFFF0000), F32)
    lo = pltpu.bitcast(bits << 16, F32)
    return jnp.concatenate([hi, lo], axis=1)


def _cond_of_token_block(i, block_rows):
    start = i * block_rows
    return jnp.where(start < N_PROMPT_TOK, 0, 1 + (start - N_PROMPT_TOK) // LATENT_LEN)


def _mod_kernel(cond_ref, w_ref, b_ref, o_ref):
    cnd = cond_ref[...]
    s = cnd * jax.nn.sigmoid(cnd)
    s_hi = s.astype(BF16)
    s_lo = (s - s_hi.astype(F32)).astype(BF16)
    w = w_ref[...]
    w_hi = w.astype(BF16)
    w_lo = (w - w_hi.astype(F32)).astype(BF16)
    acc = jnp.dot(s_hi, w_hi, preferred_element_type=F32)
    acc = acc + jnp.dot(s_lo, w_hi, preferred_element_type=F32)
    acc = acc + jnp.dot(s_hi, w_lo, preferred_element_type=F32)
    o_ref[...] = acc + b_ref[...]


def _modulation(cond, mod_w, mod_b):
    depth = mod_w.shape[0]
    n_chunk = 6
    out = pl.pallas_call(
        _mod_kernel,
        grid=(depth, n_chunk),
        in_specs=[
            pl.BlockSpec((N_COND, D_MODEL), lambda l, j: (0, 0)),
            pl.BlockSpec((None, D_MODEL, D_MODEL), lambda l, j: (l, 0, j)),
            pl.BlockSpec((None, 1, D_MODEL), lambda l, j: (l, 0, j)),
        ],
        out_specs=pl.BlockSpec((None, N_COND, D_MODEL), lambda l, j: (l, 0, j)),
        out_shape=jax.ShapeDtypeStruct((depth, N_COND, n_chunk * D_MODEL), F32),
        compiler_params=_params("arbitrary", "arbitrary"),
        name="modulation",
    )(cond, mod_w, mod_b.reshape(depth, 1, n_chunk * D_MODEL))
    return out.reshape(depth, N_COND, n_chunk, D_MODEL)


def _modulated_norm(x, g, mod, shift_row, scale_row):
    ms = jnp.mean(x * x, axis=-1, keepdims=True)
    y = x * lax.rsqrt(ms + EPS) * g
    return y * (1.0 + mod[scale_row:scale_row + 1, :]) + mod[shift_row:shift_row + 1, :]


def _trunk_specs(block_rows, width):
    n_prompt_blocks = N_PROMPT_TOK // block_rows
    return (pl.BlockSpec((block_rows, width), lambda i: (jnp.minimum(i, n_prompt_blocks - 1), 0)),
            pl.BlockSpec((block_rows, width), lambda i: (jnp.maximum(i - n_prompt_blocks, 0), 0)))


def _select_trunk(p_ref, l_ref, rows=slice(None)):
    block_rows = p_ref.shape[0]
    return jnp.where(pl.program_id(0) < N_PROMPT_TOK // block_rows, p_ref[rows, :], l_ref[rows, :])


def _cast_once(w_ref, wb_ref):
    @pl.when(pl.program_id(0) == 0)
    def _():
        wb_ref[...] = w_ref[...].astype(BF16)


def _resident(shape):
    return pl.BlockSpec(shape, lambda i: tuple(0 for _ in shape), pipeline_mode=pl.Buffered(1))


def _mod_spec(block_rows):
    return pl.BlockSpec((None, 6, D_MODEL), lambda i: (_cond_of_token_block(i, block_rows), 0, 0))


def _in_proj0_kernel(xp_ref, xl_ref, g_ref, mod_ref, w_ref, o_ref, wb_ref, hb_ref):
    _cast_once(w_ref, wb_ref)
    n = IN_PROJ_SUB_ROWS
    n_sub = xp_ref.shape[0] // n

    def prepare(r):
        x = _select_trunk(xp_ref, xl_ref, slice(r * n, (r + 1) * n))
        hb_ref[r] = _modulated_norm(x, g_ref[...], mod_ref[...], 0, 1).astype(BF16)

    def project(r):
        u = jnp.dot(hb_ref[r], wb_ref[...], preferred_element_type=F32)
        o_ref[r * n:(r + 1) * n, :] = u.astype(o_ref.dtype)

    prepare(0)
    for r in range(1, n_sub):
        prepare(r)
        project(r - 1)
    project(n_sub - 1)


def _in_proj0(x_prompt, x_latent, g, mod_l, w, block_rows=512):
    n = w.shape[1]
    return pl.pallas_call(
        _in_proj0_kernel,
        grid=(N_TOK // block_rows,),
        in_specs=[*_trunk_specs(block_rows, D_MODEL), _resident((1, D_MODEL)), _mod_spec(block_rows),
                  _resident((D_MODEL, n))],
        out_specs=pl.BlockSpec((block_rows, n), lambda i: (i, 0)),
        out_shape=jax.ShapeDtypeStruct((N_TOK, n), BF16),
        scratch_shapes=[pltpu.VMEM((D_MODEL, n), BF16),
                        pltpu.VMEM((block_rows // IN_PROJ_SUB_ROWS, IN_PROJ_SUB_ROWS, D_MODEL), BF16)],
        compiler_params=_params("arbitrary"),
        name="in_proj0",
    )(x_prompt, x_latent, g.reshape(1, D_MODEL), mod_l, w)


def _moe_mix(x_ref, ya_ref, yb_ref, wt_ref, mod_ref, rows=slice(None)):
    wt = wt_ref[rows, :]
    mix = wt[:, 0:1] * _unpack_rows(ya_ref[rows, :]) + wt[:, 1:2] * _unpack_rows(yb_ref[rows, :])
    return x_ref[rows, :] + mod_ref[5:6, :] * mix


def _in_proj1_kernel(x_ref, ya_ref, yb_ref, wt_ref, modp_ref, g_ref, mod_ref, w_ref, b_ref, xo_ref, o_ref,
                     wb_ref, hb_ref):
    _cast_once(w_ref, wb_ref)
    n = IN_PROJ_SUB_ROWS
    n_sub = x_ref.shape[0] // n

    def prepare(r):
        rows = slice(r * n, (r + 1) * n)
        x = _moe_mix(x_ref, ya_ref, yb_ref, wt_ref, modp_ref, rows)
        xo_ref[rows, :] = x
        hb_ref[r] = _modulated_norm(x, g_ref[...], mod_ref[...], 0, 1).astype(BF16)

    def project(r):
        rows = slice(r * n, (r + 1) * n)
        u = jnp.dot(hb_ref[r], wb_ref[...], preferred_element_type=F32) + b_ref[...]
        o_ref[rows, :] = u.astype(o_ref.dtype)

    prepare(0)
    for r in range(1, n_sub):
        prepare(r)
        project(r - 1)
    project(n_sub - 1)


def _in_proj1(x, moe_out, mod_prev, g, mod_l, w, bias, block_rows=512):
    ya, yb, w_tok = moe_out
    n = w.shape[1]
    tok = pl.BlockSpec((block_rows, D_MODEL), lambda i: (i, 0))
    packed = pl.BlockSpec((block_rows, ROW_WORDS), lambda i: (i, 0))
    return pl.pallas_call(
        _in_proj1_kernel,
        grid=(N_TOK // block_rows,),
        in_specs=[tok, packed, packed, pl.BlockSpec((block_rows, TOP_K), lambda i: (i, 0)), _mod_spec(block_rows),
                  _resident((1, D_MODEL)), _mod_spec(block_rows), _resident((D_MODEL, n)), _resident((1, n))],
        out_specs=(tok, pl.BlockSpec((block_rows, n), lambda i: (i, 0))),
        out_shape=(jax.ShapeDtypeStruct((N_TOK, D_MODEL), F32), jax.ShapeDtypeStruct((N_TOK, n), BF16)),
        scratch_shapes=[pltpu.VMEM((D_MODEL, n), BF16),
                        pltpu.VMEM((block_rows // IN_PROJ_SUB_ROWS, IN_PROJ_SUB_ROWS, D_MODEL), BF16)],
        compiler_params=_params("arbitrary"),
        name="in_proj1",
    )(x, ya, yb, w_tok, mod_prev, g.reshape(1, D_MODEL), mod_l, w, bias.reshape(1, n))


def _hgrn_kernel(*refs, seq_len, with_state):
    if with_state:
        (q_ref, zf_ref, zb_ref, i_ref, ga_ref, lb_ref, og_ref, s0_ref, o_ref, of_ref, ob_ref) = refs
    else:
        (q_ref, zf_ref, zb_ref, i_ref, ga_ref, lb_ref, og_ref, o_ref, s_ref, of_ref, ob_ref) = refs
    n_blocks = seq_len // HGRN_BLOCK
    chunks_per_block = HGRN_BLOCK // CHUNK

    lbr = lb_ref[...]
    mx = jnp.maximum(lbr[0], lbr[1])
    e0 = jnp.exp(lbr[0] - mx)
    e1 = jnp.exp(lbr[1] - mx)
    lb = e0 / (e0 + e1)

    row = lax.broadcasted_iota(jnp.int32, (HGRN_BLOCK, HGRN_BLOCK), 0)
    col = lax.broadcasted_iota(jnp.int32, (HGRN_BLOCK, HGRN_BLOCK), 1)
    same_chunk = (row // CHUNK) == (col // CHUNK)
    nt = (((1,), (1,)), ((), ()))
    tn = (((0,), (0,)), ((), ()))

    def per_chunk_row(x, idx):
        return jnp.concatenate(
            [jnp.broadcast_to(x[n * CHUNK + idx:n * CHUNK + idx + 1, :], (CHUNK, x.shape[1]))
             for n in range(chunks_per_block)], axis=0)

    def in_chunk_cumsum(tri, x):
        hi = x.astype(BF16)
        lo = (x - hi.astype(F32)).astype(BF16)
        return jnp.dot(tri, hi, preferred_element_type=F32) + jnp.dot(tri, lo, preferred_element_type=F32)

    def prepare(blk, cols, z_ref, lbd, forward):
        rows = slice(blk * HGRN_BLOCK, (blk + 1) * HGRN_BLOCK)
        keep = (same_chunk & (col <= row)) if forward else (same_chunk & (col >= row))
        tri = jnp.where(keep, 1.0, 0.0).astype(BF16)
        mid = CHUNK // 2 if forward else CHUNK - 1 - CHUNK // 2
        last = CHUNK - 1 if forward else 0
        f = lbd + (1.0 - lbd) * jax.nn.sigmoid(z_ref[rows, cols].astype(F32))
        lf = jnp.log(f)
        k = 1.0 - f
        q = q_ref[rows, cols].astype(F32)
        b = in_chunk_cumsum(tri, lf)
        bm = per_chunk_row(b, mid)
        bl = per_chunk_row(b, last)
        return dict(
            rows=rows, cols=cols, keep=keep, forward=forward,
            vb=i_ref[rows, cols].astype(BF16),
            qe=(q * jnp.exp(b - bm)).astype(BF16), ke=(k * jnp.exp(bm - b)).astype(BF16),
            qb=(q * jnp.exp(b)).astype(BF16), ks=(k * jnp.exp(bl - b)).astype(BF16), decay=jnp.exp(bl))

    def within_chunks(u):
        att = lax.dot_general(u["qe"], u["ke"], nt, preferred_element_type=F32)
        att = jnp.where(u["keep"], att, 0.0)
        u["o_intra"] = jnp.dot(att.astype(BF16), u["vb"], preferred_element_type=F32)
        u["upd"] = [lax.dot_general(u["vb"][n * CHUNK:(n + 1) * CHUNK], u["ks"][n * CHUNK:(n + 1) * CHUNK], tn,
                                    preferred_element_type=F32) for n in range(chunks_per_block)]

    def across_chunks(u, st, out_ref):
        order = range(chunks_per_block) if u["forward"] else range(chunks_per_block - 1, -1, -1)
        o_inter = [None] * chunks_per_block
        for n in order:
            cr = slice(n * CHUNK, (n + 1) * CHUNK)
            o_inter[n] = lax.dot_general(u["qb"][cr], st.astype(BF16), nt, preferred_element_type=F32)
            st = st * u["decay"][n * CHUNK:n * CHUNK + 1, :] + u["upd"][n]
        out_ref[u["rows"], u["cols"]] = u["o_intra"] + jnp.concatenate(o_inter, axis=0)
        return st

    n_heads = q_ref.shape[1] // A_DK
    head_cols = [slice(hd * A_DK, (hd + 1) * A_DK) for hd in range(n_heads)]
    if with_state:
        states = {(hd, d): s0_ref[d, hd].T for hd in range(n_heads) for d in range(2)}
    else:
        states = {(hd, d): jnp.zeros((A_DK, A_DK), F32) for hd in range(n_heads) for d in range(2)}
    for step in range(n_blocks):
        units = {}
        for hd, cols in enumerate(head_cols):
            units[hd, 0] = prepare(step, cols, zf_ref, lb[0:1, cols], True)
            units[hd, 1] = prepare(n_blocks - 1 - step, cols, zb_ref, lb[1:2, cols], False)
        for u in units.values():
            within_chunks(u)
        for key, u in units.items():
            states[key] = across_chunks(u, states[key], of_ref if key[1] == 0 else ob_ref)
    for hd, cols in enumerate(head_cols):
        if not with_state:
            s_ref[0, hd] = states[hd, 0].T
            s_ref[1, hd] = states[hd, 1].T
        o = of_ref[:, cols] + ob_ref[:, cols]
        o = o * lax.rsqrt(jnp.mean(o * o, axis=-1, keepdims=True) + EPS) * og_ref[:, cols]
        ga = ga_ref[:, cols].astype(F32)
        o_ref[:, cols] = (o * (ga * jax.nn.sigmoid(ga))).astype(o_ref.dtype)


def _hgrn(z, hgrn_lb, onorm_g, state, *, latent):
    seq_len = LATENT_LEN if latent else PROMPT_LEN
    n_seq = N_LATENT_SEQ if latent else N_PROMPT_SEQ
    row0 = (N_PROMPT_TOK // seq_len) if latent else 0

    hw = HGRN_HEADS_PER_STEP * A_DK
    n_hg = A_HEADS // HGRN_HEADS_PER_STEP

    def zspec(part):
        return pl.BlockSpec((seq_len, hw), lambda s, h: (row0 + s, part * n_hg + h))

    in_specs = [zspec(0), zspec(1), zspec(2), zspec(3), zspec(4),
                pl.BlockSpec((2, 2, hw), lambda s, h: (0, 0, h)),
                pl.BlockSpec((1, hw), lambda s, h: (0, h))]
    args = [z, z, z, z, z, hgrn_lb, onorm_g.reshape(1, A_WIDTH)]
    state_spec = pl.BlockSpec((None, None, 2, HGRN_HEADS_PER_STEP, A_DK, A_DK), lambda s, h: (s, 0, 0, h, 0, 0))
    o_shape = jax.ShapeDtypeStruct((n_seq * seq_len, A_WIDTH), BF16)
    o_spec = pl.BlockSpec((seq_len, hw), lambda s, h: (s, h))
    if latent:
        in_specs.append(state_spec)
        args.append(state)
        out_shape, out_specs = o_shape, o_spec
    else:
        out_shape = (o_shape, jax.ShapeDtypeStruct((n_seq, 1, 2, A_HEADS, A_DK, A_DK), F32))
        out_specs = (o_spec, state_spec)
    return pl.pallas_call(
        functools.partial(_hgrn_kernel, seq_len=seq_len, with_state=latent),
        grid=(n_seq, n_hg),
        in_specs=in_specs,
        out_specs=out_specs,
        out_shape=out_shape,
        scratch_shapes=[pltpu.VMEM((seq_len, hw), F32), pltpu.VMEM((seq_len, hw), F32)],
        compiler_params=_params("arbitrary", "arbitrary"),
        name="hgrn_latent" if latent else "hgrn_prompt",
    )(*args)


def _rope_tables():
    pos = np.arange(LATENT_LEN)
    row, colp = pos // GRID_W, pos % GRID_W
    inv = ROPE_THETA ** (-np.arange(ROPE_PAIRS, dtype=np.float32) / ROPE_PAIRS)
    inv = inv.astype(np.float32)
    ang_r = (row.astype(np.float32)[:, None] * inv).astype(np.float32)
    ang_c = (colp.astype(np.float32)[:, None] * inv).astype(np.float32)
    cos = np.concatenate([np.cos(ang_r), np.cos(ang_r), np.cos(ang_c), np.cos(ang_c)], axis=1)
    sin = np.concatenate([-np.sin(ang_r), np.sin(ang_r), -np.sin(ang_c), np.sin(ang_c)], axis=1)
    return cos.astype(np.float32), sin.astype(np.float32)


def _head_mean_matrix(width):
    idx = np.arange(width) // HEAD_DIM
    return jnp.asarray((idx[:, None] == idx[None, :]).astype(np.float32) / HEAD_DIM).astype(BF16)


def _attn_kernel(*refs, latent):
    if latent:
        (q_ref, k_ref, v_ref, qg_ref, kg_ref, gq_ref, gk_ref, cosq_ref, sinq_ref, cosk_ref, sink_ref,
         ck_ref, cv_ref, o_ref) = refs
    else:
        (q_ref, k_ref, v_ref, qg_ref, kg_ref, gq_ref, gk_ref, o_ref, kout_ref, vout_ref) = refs
    pair_w = 2 * HEAD_DIM

    def head_norm(x, mean_ref, gain):
        sq = x * x
        hi = sq.astype(BF16)
        lo = (sq - hi.astype(F32)).astype(BF16)
        ms = jnp.dot(hi, mean_ref[...], preferred_element_type=F32)
        ms = ms + jnp.dot(lo, mean_ref[...], preferred_element_type=F32)
        return x * lax.rsqrt(ms + EPS) * gain

    def rope(x, cos, sin):
        n = x.shape[1]
        lane = lax.broadcasted_iota(jnp.int32, x.shape, 1)
        first_of_pair = (lane // ROPE_PAIRS) % 2 == 0
        swapped = jnp.where(first_of_pair, pltpu.roll(x, n - ROPE_PAIRS, axis=1), pltpu.roll(x, ROPE_PAIRS, axis=1))
        return x * cos + swapped * sin

    nt = (((1,), (1,)), ((), ()))

    def prepare(rows, seq_idx):
        q = head_norm(q_ref[rows, :].astype(F32), gq_ref, qg_ref[...])
        k = head_norm(k_ref[rows, :].astype(F32), gk_ref, kg_ref[...])
        if latent:
            q = rope(q, cosq_ref[...], sinq_ref[...])
            k = rope(k, cosk_ref[...], sink_ref[...])
        q = q * (HEAD_DIM ** -0.5)
        v = v_ref[rows, :].astype(F32)
        n_q = q.shape[0]
        low_kv = lax.broadcasted_iota(jnp.int32, k.shape, 1) < HEAD_DIM
        low_q = lax.broadcasted_iota(jnp.int32, (n_q, pair_w), 1) < HEAD_DIM
        k_swapped = pltpu.roll(k, HEAD_DIM, axis=1)
        v_swapped = pltpu.roll(v, HEAD_DIM, axis=1)
        if not latent:
            kout_ref[seq_idx] = k.T
            vout_ref[seq_idx] = v.T
        units = []
        for j in range(KV_HEADS):
            kd = (jnp.where(low_kv, k, k_swapped) if j == 0 else jnp.where(low_kv, k_swapped, k)).astype(BF16)
            vd = (jnp.where(low_kv, v, v_swapped) if j == 0 else jnp.where(low_kv, v_swapped, v)).astype(BF16)
            vd = jnp.concatenate([vd, jnp.ones_like(vd)], axis=1)
            tiles = range(j * Q_PER_KV // 2, (j + 1) * Q_PER_KV // 2)
            parts = []
            for t in tiles:
                qt = q[:, t * pair_w:(t + 1) * pair_w]
                parts += [jnp.where(low_q, qt, 0.0), jnp.where(low_q, 0.0, qt)]
            units.append(dict(j=j, rows=rows, tiles=tiles, n_q=n_q, low_q=low_q, kd=kd, vd=vd,
                              qs=jnp.concatenate(parts, axis=0).astype(BF16)))
        return units

    def scores(u):
        u["s_new"] = lax.dot_general(u["qs"], u["kd"], nt, preferred_element_type=F32)
        if latent:
            j = u["j"]
            cvd = jnp.concatenate([cv_ref[j], cv_ref[j]], axis=1).astype(BF16)
            u["cvd"] = jnp.concatenate([cvd, jnp.ones_like(cvd)], axis=1)
            ckd = jnp.concatenate([ck_ref[j], ck_ref[j]], axis=1).astype(BF16)
            u["s_old"] = lax.dot_general(u["qs"], ckd, nt, preferred_element_type=F32)

    def softmax(u):
        m = jnp.max(u["s_new"], axis=-1, keepdims=True)
        if latent:
            m = jnp.maximum(m, jnp.max(u["s_old"], axis=-1, keepdims=True))
        u["p_new"] = jnp.exp(u.pop("s_new") - m).astype(BF16)
        if latent:
            u["p_old"] = jnp.exp(u.pop("s_old") - m).astype(BF16)

    def weighted_values(u):
        acc = jnp.dot(u["p_new"], u["vd"], preferred_element_type=F32)
        if latent:
            acc = acc + jnp.dot(u["p_old"], u["cvd"], preferred_element_type=F32)
        out = acc[:, :pair_w] / acc[:, pair_w:]
        n_q = u["n_q"]
        for i, t in enumerate(u["tiles"]):
            lo_head = out[(2 * i) * n_q:(2 * i + 1) * n_q, :]
            hi_head = out[(2 * i + 1) * n_q:(2 * i + 2) * n_q, :]
            o_ref[u["rows"], t * pair_w:(t + 1) * pair_w] = jnp.where(u["low_q"], lo_head, hi_head).astype(o_ref.dtype)

    if latent:
        units = prepare(slice(None), None)
    else:
        seq = PROMPT_LEN
        units = [u for s in range(q_ref.shape[0] // seq) for u in prepare(slice(s * seq, (s + 1) * seq), s)]
    for phase in (scores, softmax, weighted_values):
        for u in units:
            phase(u)


def _attn_common_args(qn_g, kn_g):
    q_w, kv_w = Q_HEADS * HEAD_DIM, KV_HEADS * HEAD_DIM
    return (jnp.tile(qn_g, Q_HEADS).reshape(1, q_w), jnp.tile(kn_g, KV_HEADS).reshape(1, kv_w),
            _head_mean_matrix(q_w), _head_mean_matrix(kv_w))


def _attention_prompt(z, qn_g, kn_g):
    seqs = 4
    L = seqs * PROMPT_LEN
    cache_shape = jax.ShapeDtypeStruct((N_PROMPT_SEQ, KV_HEADS * HEAD_DIM, PROMPT_LEN), F32)
    cache_spec = pl.BlockSpec((seqs, KV_HEADS * HEAD_DIM, PROMPT_LEN), lambda s: (s, 0, 0))
    q_w, kv_w = Q_HEADS * HEAD_DIM, KV_HEADS * HEAD_DIM
    q_col = (5 * A_WIDTH) // q_w
    k_col = (5 * A_WIDTH + q_w) // kv_w
    const = lambda r, c: pl.BlockSpec((r, c), lambda s: (0, 0))
    return pl.pallas_call(
        functools.partial(_attn_kernel, latent=False),
        grid=(N_PROMPT_TOK // L,),
        in_specs=[
            pl.BlockSpec((L, q_w), lambda s: (s, q_col)),
            pl.BlockSpec((L, kv_w), lambda s: (s, k_col)),
            pl.BlockSpec((L, kv_w), lambda s: (s, k_col + 1)),
            const(1, q_w), const(1, kv_w), const(q_w, q_w), const(kv_w, kv_w),
        ],
        out_specs=(pl.BlockSpec((L, q_w), lambda s: (s, 0)), cache_spec, cache_spec),
        out_shape=(jax.ShapeDtypeStruct((N_PROMPT_TOK, q_w), BF16), cache_shape, cache_shape),
        compiler_params=_params("arbitrary"),
        name="attn_prompt",
    )(z, z, z, *_attn_common_args(qn_g, kn_g))


def _attention_latent(z, qn_g, kn_g, cache_k, cache_v):
    L = LATENT_LEN
    nqb = L // Q_BLOCK
    q_w, kv_w = Q_HEADS * HEAD_DIM, KV_HEADS * HEAD_DIM
    q_col = (5 * A_WIDTH) // q_w
    k_col = (5 * A_WIDTH + q_w) // kv_w
    qrow0 = N_PROMPT_TOK // Q_BLOCK
    krow0 = N_PROMPT_TOK // L
    cos, sin = _rope_tables()
    cos_q, sin_q = jnp.asarray(np.tile(cos, (1, Q_HEADS))), jnp.asarray(np.tile(sin, (1, Q_HEADS)))
    cos_k, sin_k = jnp.asarray(np.tile(cos, (1, KV_HEADS))), jnp.asarray(np.tile(sin, (1, KV_HEADS)))
    const = lambda r, c: pl.BlockSpec((r, c), lambda s, b: (0, 0))
    cache_spec = pl.BlockSpec((None, None, KV_HEADS, PAST_LEN, HEAD_DIM), lambda s, b: (s, 0, 0, 0, 0))
    return pl.pallas_call(
        functools.partial(_attn_kernel, latent=True),
        grid=(N_LATENT_SEQ, nqb),
        in_specs=[
            pl.BlockSpec((Q_BLOCK, q_w), lambda s, b: (qrow0 + s * nqb + b, q_col)),
            pl.BlockSpec((L, kv_w), lambda s, b: (krow0 + s, k_col)),
            pl.BlockSpec((L, kv_w), lambda s, b: (krow0 + s, k_col + 1)),
            const(1, q_w), const(1, kv_w), const(q_w, q_w), const(kv_w, kv_w),
            pl.BlockSpec((Q_BLOCK, q_w), lambda s, b: (b, 0)),
            pl.BlockSpec((Q_BLOCK, q_w), lambda s, b: (b, 0)),
            const(L, kv_w), const(L, kv_w),
            cache_spec, cache_spec,
        ],
        out_specs=pl.BlockSpec((Q_BLOCK, q_w), lambda s, b: (s * nqb + b, 0)),
        out_shape=jax.ShapeDtypeStruct((N_LATENT_TOK, q_w), BF16),
        compiler_params=_params("arbitrary", "arbitrary"),
        name="attn_latent",
    )(z, z, z, *_attn_common_args(qn_g, kn_g), cos_q, sin_q, cos_k, sin_k, cache_k, cache_v)


def _out_proj_kernel(*refs, n_act, n_x):
    a_refs = refs[:2 * n_act]
    x_refs = refs[2 * n_act:2 * n_act + n_x]
    g_ref, mod_ref, rw_ref, w_ref, xo_ref, h_ref, lg_ref, wb_ref, rws_ref, acc_ref = refs[2 * n_act + n_x:]
    _cast_once(w_ref, wb_ref)

    @pl.when(pl.program_id(0) == 0)
    def _():
        rw = rw_ref[...]
        hi = rw.astype(BF16).astype(F32)
        lo = (rw - hi).astype(BF16).astype(F32)
        rws_ref[...] = (hi + pltpu.roll(lo, N_EXPERTS, axis=1)).astype(BF16)

    mod = mod_ref[...]
    n = OUT_PROJ_SUB_ROWS

    n_sub = xo_ref.shape[0] // n

    def sub_rows(r):
        if isinstance(r, int):
            return slice(r * n, (r + 1) * n)
        return pl.ds(pl.multiple_of(r * n, n), n)

    def project(r):
        rows = sub_rows(r)
        acc = None
        k0 = 0
        for ap_ref, al_ref in zip(a_refs[0::2], a_refs[1::2]):
            k1 = k0 + ap_ref.shape[1]
            part = jnp.dot(_select_trunk(ap_ref, al_ref, rows), wb_ref[k0:k1, :], preferred_element_type=F32)
            acc = part if acc is None else acc + part
            k0 = k1
        acc_ref[r % 2] = acc

    def finish(r):
        rows = sub_rows(r)
        x_in = x_refs[0][rows, :] if n_x == 1 else _select_trunk(*x_refs, rows)
        x = x_in + mod[2:3, :] * acc_ref[r % 2]
        xo_ref[rows, :] = x
        h = _modulated_norm(x, g_ref[...], mod, 3, 4)
        h_ref[rows, :] = _pack_rows(h)
        h_hi = h.astype(BF16)
        h_lo = (h - h_hi.astype(F32)).astype(BF16)
        both = jnp.dot(jnp.concatenate([h_hi, h_lo], axis=0), rws_ref[...], preferred_element_type=F32)
        from_hi, from_lo = both[:n], both[n:]
        lg = from_hi + pltpu.roll(from_hi, ROUTER_LANES - N_EXPERTS, axis=1) + from_lo
        lg_ref[:, rows] = lg.T[:N_EXPERTS, :]

    project(0)
    for r in range(n_sub - 1):
        project(r + 1)
        finish(r)
    finish(n_sub - 1)


def _out_proj(acts, w, xs, g, mod_l, router_wp, block_rows=1024):
    tok = lambda width: pl.BlockSpec((block_rows, width), lambda i: (i, 0))
    in_specs = [spec for ap, _ in acts for spec in _trunk_specs(block_rows, ap.shape[1])]
    in_specs += [tok(D_MODEL)] if len(xs) == 1 else list(_trunk_specs(block_rows, D_MODEL))
    in_specs += [_resident((1, D_MODEL)), _mod_spec(block_rows), _resident((D_MODEL, ROUTER_LANES)),
                 _resident(w.shape)]
    return pl.pallas_call(
        functools.partial(_out_proj_kernel, n_act=len(acts), n_x=len(xs)),
        grid=(N_TOK // block_rows,),
        in_specs=in_specs,
        out_specs=(tok(D_MODEL), tok(ROW_WORDS), pl.BlockSpec((N_EXPERTS, block_rows), lambda i: (0, i))),
        out_shape=(jax.ShapeDtypeStruct((N_TOK, D_MODEL), F32),
                   jax.ShapeDtypeStruct((N_TOK, ROW_WORDS), jnp.int32),
                   jax.ShapeDtypeStruct((N_EXPERTS, N_TOK), F32)),
        scratch_shapes=[pltpu.VMEM(w.shape, BF16), pltpu.VMEM((D_MODEL, ROUTER_LANES), BF16),
                        pltpu.VMEM((2, OUT_PROJ_SUB_ROWS, D_MODEL), F32)],
        compiler_params=_params("arbitrary"),
        name="out_proj",
    )(*[a for pair in acts for a in pair], *xs, g.reshape(1, D_MODEL), mod_l, router_wp, w)


def _router_kernel(lg_ref, rb_ref, pos_ref, w_ref, plan_ref, rank_ref):
    lg = lg_ref[...]
    ex = jnp.exp(lg - jnp.max(lg, axis=0, keepdims=True))
    scores = ex / jnp.sum(ex, axis=0, keepdims=True)
    biased = scores + rb_ref[...]
    expert = lax.broadcasted_iota(jnp.int32, biased.shape, 0)
    in_pos = expert % EXPERTS_PER_GROUP
    rank = jnp.zeros_like(biased)
    for d in range(1, EXPERTS_PER_GROUP):
        wraps = in_pos + d >= EXPERTS_PER_GROUP
        partner = jnp.where(wraps, pltpu.roll(biased, EXPERTS_PER_GROUP - d, axis=0),
                            pltpu.roll(biased, N_EXPERTS - d, axis=0))
        rank = rank + jnp.where(wraps, jnp.where(partner >= biased, 1.0, 0.0), jnp.where(partner > biased, 1.0, 0.0))
    selected = rank < 1.5
    contrib = jnp.where(selected, biased, 0.0)
    group_score = []
    for gi in range(N_GROUPS):
        r = [contrib[gi * EXPERTS_PER_GROUP + i:gi * EXPERTS_PER_GROUP + i + 1, :] for i in range(EXPERTS_PER_GROUP)]
        group_score.append(((r[0] + r[1]) + r[2]) + r[3])
    best = group_score[0]
    best_group = jnp.zeros_like(best)
    for gi in range(1, N_GROUPS):
        better = group_score[gi] > best
        best_group = jnp.where(better, float(gi), best_group)
        best = jnp.where(better, group_score[gi], best)
    in_group = (expert // EXPERTS_PER_GROUP).astype(F32) == best_group
    chosen = jnp.where(selected, jnp.where(in_group, 1.0, 0.0), 0.0)
    picked = chosen * scores
    gates = picked / jnp.sum(picked, axis=0, keepdims=True)
    lanes = 128
    n_blk = N_TOK // lanes
    li = lax.broadcasted_iota(jnp.int32, (lanes, lanes), 0)
    lj = lax.broadcasted_iota(jnp.int32, (lanes, lanes), 1)
    prefix = jnp.where(li <= lj, 1.0, 0.0).astype(BF16)
    stacked = jnp.concatenate([chosen[:, blk * lanes:(blk + 1) * lanes] for blk in range(n_blk)], axis=0)
    incl_all = jnp.dot(stacked.astype(BF16), prefix, preferred_element_type=F32)
    carry = jnp.zeros((N_EXPERTS, 1), F32)
    for blk in range(n_blk):
        cols = slice(blk * lanes, (blk + 1) * lanes)
        incl = incl_all[blk * N_EXPERTS:(blk + 1) * N_EXPERTS, :]
        rank_ref[:, cols] = incl - chosen[:, cols] + carry
        carry = carry + incl[:, lanes - 1:lanes]
    count = carry
    padded = jnp.floor((count + float(MOE_TILE - 1)) * (1.0 / MOE_TILE)) * float(MOE_TILE)
    erow = lax.broadcasted_iota(jnp.int32, (N_EXPERTS, 1), 0)
    offset = jnp.zeros((N_EXPERTS, 1), F32)
    for e in range(N_EXPERTS - 1):
        offset = offset + jnp.where(erow > e, padded[e:e + 1, :], 0.0)
    position = rank_ref[...] + offset
    ei = lax.broadcasted_iota(jnp.int32, (N_EXPERTS, N_EXPERTS), 0)
    ej = lax.broadcasted_iota(jnp.int32, (N_EXPERTS, N_EXPERTS), 1)
    lower = jnp.where(ej <= ei, 1.0, 0.0).astype(BF16)
    seen = jnp.dot(lower, chosen.astype(BF16), preferred_element_type=F32)
    first = (chosen > 0.5) & (seen < 1.5)
    second = (chosen > 0.5) & (seen > 1.5)
    pick = lambda flag, x: jnp.sum(jnp.where(flag, x, 0.0), axis=0, keepdims=True)
    pos_ref[0:1, :] = pick(first, position).astype(jnp.int32)
    pos_ref[1:2, :] = pick(second, position).astype(jnp.int32)
    w_rows = jnp.concatenate([pick(first, gates), pick(second, gates), jnp.zeros((6, N_TOK), F32)], axis=0)
    ti = lax.broadcasted_iota(jnp.int32, (8, lanes), 0)
    tj = lax.broadcasted_iota(jnp.int32, (8, lanes), 1)
    eye = jnp.where(ti == tj, 1.0, 0.0).astype(BF16)
    tn = (((0,), (0,)), ((), ()))
    hi = w_rows.astype(BF16)
    r1 = w_rows - hi.astype(F32)
    mid = r1.astype(BF16)
    lo = (r1 - mid.astype(F32)).astype(BF16)
    w_cols = lax.dot_general(hi, eye, tn, preferred_element_type=F32)
    w_cols = w_cols + lax.dot_general(mid, eye, tn, preferred_element_type=F32)
    w_cols = w_cols + lax.dot_general(lo, eye, tn, preferred_element_type=F32)
    w_ref[...] = w_cols[:, :TOP_K]
    start = (lax.broadcasted_iota(jnp.int32, (N_EXPERTS, lanes), 1) * MOE_TILE).astype(F32)
    end = offset + padded
    tile_expert = jnp.sum(jnp.where(end <= start, 1.0, 0.0), axis=0, keepdims=True)
    inside = (offset <= start) & (start < end)
    real = jnp.clip(count - (start - offset), 0.0, float(MOE_TILE))
    tile_rows = jnp.sum(jnp.where(inside, real, 0.0), axis=0, keepdims=True)
    plan_ref[0:1, :] = jnp.minimum(tile_expert, float(N_EXPERTS - 1)).astype(jnp.int32)
    plan_ref[1:2, :] = tile_rows.astype(jnp.int32)


def _router(logits_t, router_b):
    whole = lambda shape: pl.BlockSpec(shape, lambda i: (0, 0))
    return pl.pallas_call(
        _router_kernel,
        grid=(1,),
        in_specs=[whole((N_EXPERTS, N_TOK)), whole((N_EXPERTS, 1))],
        out_specs=(whole((2, N_TOK)), whole((N_TOK, TOP_K)), whole((2, 128))),
        out_shape=(jax.ShapeDtypeStruct((2, N_TOK), jnp.int32),
                   jax.ShapeDtypeStruct((N_TOK, TOP_K), F32),
                   jax.ShapeDtypeStruct((2, 128), jnp.int32)),
        scratch_shapes=[pltpu.VMEM((N_EXPERTS, N_TOK), F32)],
        compiler_params=_params("arbitrary"),
        name="router",
    )(logits_t, router_b.reshape(N_EXPERTS, 1))


def _sc_mesh():
    return plsc.VectorSubcoreMesh(core_axis_name="c", subcore_axis_name="s")


def _sc_worker_base():
    return (lax.axis_index("s") * SC_CORES + lax.axis_index("c")) * SC_TOKENS_PER_WORKER


def _moe_dispatch(h, pos_a, pos_b):
    n_chunks = SC_TOKENS_PER_WORKER // SC_CHUNK
    idx = pltpu.VMEM((SC_CHUNK,), jnp.int32)

    @functools.partial(
        pl.kernel, mesh=_sc_mesh(),
        out_type=jax.ShapeDtypeStruct((MOE_ROWS, ROW_WORDS), jnp.int32),
        scratch_types=[idx, idx, idx, idx, pltpu.VMEM((2, SC_CHUNK, ROW_WORDS), jnp.int32),
                       pltpu.SemaphoreType.DMA((6,)), pltpu.SemaphoreType.DMA((4,))],
        name="moe_dispatch",
    )
    def run(h_hbm, pa_hbm, pb_hbm, xs_hbm, ia0, ib0, ia1, ib1, rows_v, sem_in, sem_out):
        base = _sc_worker_base()
        ia, ib = (ia0, ia1), (ib0, ib1)

        def start_loads(c):
            slot = c % 2
            tok = pl.ds(pl.multiple_of(base + c * SC_CHUNK, 8), SC_CHUNK)
            return (pltpu.async_copy(pa_hbm.at[tok], ia[slot], sem_in.at[3 * slot]),
                    pltpu.async_copy(pb_hbm.at[tok], ib[slot], sem_in.at[3 * slot + 1]),
                    pltpu.async_copy(h_hbm.at[tok], rows_v.at[slot], sem_in.at[3 * slot + 2]))

        loads = start_loads(0)
        scatters = [(), ()]
        for c in range(n_chunks):
            slot = c % 2
            for cp in loads:
                cp.wait()
            if c + 1 < n_chunks:
                for cp in scatters[1 - slot]:
                    cp.wait()
                scatters[1 - slot] = ()
                loads = start_loads(c + 1)
            scatters[slot] = (pltpu.async_copy(rows_v.at[slot], xs_hbm.at[ia[slot]], sem_out.at[2 * slot]),
                              pltpu.async_copy(rows_v.at[slot], xs_hbm.at[ib[slot]], sem_out.at[2 * slot + 1]))
        for pending in scatters:
            for cp in pending:
                cp.wait()

    return run(h, pos_a, pos_b)


def _moe_collect(ys, pos_a, pos_b):
    n_chunks = SC_TOKENS_PER_WORKER // SC_CHUNK
    out = jax.ShapeDtypeStruct((N_TOK, ROW_WORDS), jnp.int32)
    idx = pltpu.VMEM((SC_TOKENS_PER_WORKER,), jnp.int32)
    rows = pltpu.VMEM((2, SC_CHUNK, ROW_WORDS), jnp.int32)

    @functools.partial(
        pl.kernel, mesh=_sc_mesh(), out_type=(out, out),
        scratch_types=[idx, idx, rows, rows, pltpu.SemaphoreType.DMA((4,)), pltpu.SemaphoreType.DMA((4,))],
        name="moe_collect",
    )
    def run(ys_hbm, pa_hbm, pb_hbm, ya_hbm, yb_hbm, ia_v, ib_v, ra_v, rb_v, sem_g, sem_w):
        base = _sc_worker_base()
        mine = pl.ds(pl.multiple_of(base, 8), SC_TOKENS_PER_WORKER)
        pltpu.sync_copy(pa_hbm.at[mine], ia_v)
        pltpu.sync_copy(pb_hbm.at[mine], ib_v)
        writes = [(), ()]
        for c in range(n_chunks):
            slot = c % 2
            for cp in writes[slot]:
                cp.wait()
            part = pl.ds(c * SC_CHUNK, SC_CHUNK)
            tok = pl.ds(pl.multiple_of(base + c * SC_CHUNK, 8), SC_CHUNK)
            ga = pltpu.async_copy(ys_hbm.at[ia_v.at[part]], ra_v.at[slot], sem_g.at[slot])
            gb = pltpu.async_copy(ys_hbm.at[ib_v.at[part]], rb_v.at[slot], sem_g.at[2 + slot])
            ga.wait()
            wa = pltpu.async_copy(ra_v.at[slot], ya_hbm.at[tok], sem_w.at[slot])
            gb.wait()
            wb = pltpu.async_copy(rb_v.at[slot], yb_hbm.at[tok], sem_w.at[2 + slot])
            writes[slot] = (wa, wb)
        for pending in writes:
            for cp in pending:
                cp.wait()

    return run(ys, pos_a, pos_b)


def _experts_kernel(plan_ref, xs_ref, wg_hbm, wu_hbm, wd_hbm, y_ref,
                    sg_ref, su_ref, sd_ref, wgb_ref, wub_ref, wdb_ref, hid_ref, sems, seg_ref, *, layer):
    j = pl.program_id(0)
    n_tiles = pl.num_programs(0)
    expert = plan_ref[j]
    n_real = plan_ref[PLAN_LANES + j]
    fresh = jnp.logical_or(j == 0, expert != plan_ref[jnp.maximum(j - 1, 0)])

    def weight_copies(e, slot):
        return (pltpu.make_async_copy(wg_hbm.at[layer, e], sg_ref.at[slot], sems.at[slot, 0]),
                pltpu.make_async_copy(wu_hbm.at[layer, e], su_ref.at[slot], sems.at[slot, 1]),
                pltpu.make_async_copy(wd_hbm.at[layer, e], sd_ref.at[slot], sems.at[slot, 2]))

    @pl.when(j == 0)
    def _():
        seg_ref[0] = 0

        @pl.when(n_real > 0)
        def _():
            for cp in weight_copies(expert, 0):
                cp.start()

    @pl.when(jnp.logical_and(n_real > 0, fresh))
    def _():
        slot = seg_ref[0] % 2
        for cp in weight_copies(expert, slot):
            cp.wait()
        wgb_ref[...] = sg_ref[slot].astype(BF16)
        wub_ref[...] = su_ref[slot].astype(BF16)
        wdb_ref[...] = sd_ref[slot].astype(BF16)
        nxt = lax.while_loop(lambda t: jnp.logical_and(t < n_tiles, plan_ref[jnp.minimum(t, n_tiles - 1)] == expert),
                             lambda t: t + 1, j + 1)
        nxt_c = jnp.minimum(nxt, n_tiles - 1)

        @pl.when(jnp.logical_and(nxt < n_tiles, plan_ref[PLAN_LANES + nxt_c] > 0))
        def _():
            for cp in weight_copies(plan_ref[nxt_c], 1 - slot):
                cp.start()

        seg_ref[0] = seg_ref[0] + 1

    @pl.when(n_real > 0)
    def _():
        n = EXPERT_SUB_ROWS
        n_sub = xs_ref.shape[0] // n
        row = lax.broadcasted_iota(jnp.int32, (n, xs_ref.shape[1]), 0)

        def up(r):
            rows = slice(r * n, (r + 1) * n)
            words = jnp.where(row < n_real - r * n, xs_ref[rows, :], 0)
            x = _unpack_rows(words).astype(BF16)
            a = jnp.dot(x, wgb_ref[...], preferred_element_type=F32)
            b = jnp.dot(x, wub_ref[...], preferred_element_type=F32)
            hid_ref[r] = ((a * jax.nn.sigmoid(a)) * b).astype(BF16)

        def down(r):
            rows = slice(r * n, (r + 1) * n)
            y_ref[rows, :] = _pack_rows(jnp.dot(hid_ref[r], wdb_ref[...], preferred_element_type=F32))

        up(0)
        for r in range(1, n_sub):
            up(r)
            down(r - 1)
        down(n_sub - 1)


def _experts(plan, xs, w_gate, w_up, w_down, layer):
    hbm = pl.BlockSpec(memory_space=pl.ANY)
    return pl.pallas_call(
        functools.partial(_experts_kernel, layer=layer),
        grid_spec=pltpu.PrefetchScalarGridSpec(
            num_scalar_prefetch=1,
            grid=(MOE_ROWS // MOE_TILE,),
            in_specs=[pl.BlockSpec((MOE_TILE, ROW_WORDS), lambda j, plan: (j, 0)), hbm, hbm, hbm],
            out_specs=pl.BlockSpec((MOE_TILE, ROW_WORDS), lambda j, plan: (j, 0)),
            scratch_shapes=[pltpu.VMEM((2, D_MODEL, D_EXPERT), F32), pltpu.VMEM((2, D_MODEL, D_EXPERT), F32),
                            pltpu.VMEM((2, D_EXPERT, D_MODEL), F32),
                            pltpu.VMEM((D_MODEL, D_EXPERT), BF16), pltpu.VMEM((D_MODEL, D_EXPERT), BF16),
                            pltpu.VMEM((D_EXPERT, D_MODEL), BF16),
                            pltpu.VMEM((MOE_TILE // EXPERT_SUB_ROWS, EXPERT_SUB_ROWS, D_EXPERT), BF16),
                            pltpu.SemaphoreType.DMA((2, 3)), pltpu.SMEM((1,), jnp.int32)],
        ),
        out_shape=jax.ShapeDtypeStruct((MOE_ROWS, ROW_WORDS), jnp.int32),
        compiler_params=_params("arbitrary"),
        name="experts",
    )(plan, xs, w_gate, w_up, w_down)


def _combine_kernel(x_ref, ya_ref, yb_ref, wt_ref, mod_ref, o_ref):
    o_ref[...] = _moe_mix(x_ref, ya_ref, yb_ref, wt_ref, mod_ref)


def _combine(x, moe_out, mod_l, tok0, n_tok, block_rows=512):
    ya, yb, w_tok = moe_out
    b0 = tok0 // block_rows
    rows = lambda width: pl.BlockSpec((block_rows, width), lambda i: (b0 + i, 0))
    return pl.pallas_call(
        _combine_kernel,
        grid=(n_tok // block_rows,),
        in_specs=[rows(D_MODEL), rows(ROW_WORDS), rows(ROW_WORDS), rows(TOP_K),
                  pl.BlockSpec((None, 6, D_MODEL), lambda i: (_cond_of_token_block(b0 + i, block_rows), 0, 0))],
        out_specs=pl.BlockSpec((block_rows, D_MODEL), lambda i: (i, 0)),
        out_shape=jax.ShapeDtypeStruct((n_tok, D_MODEL), F32),
        compiler_params=_params("arbitrary"),
        name="combine",
    )(x, ya, yb, w_tok, mod_l)


def _moe(h, logits_t, router_b, w_gate, w_up, w_down, layer):
    pos, w, plan = _router(logits_t, router_b)
    xs = _moe_dispatch(h, pos[0], pos[1])
    ys = _experts(plan.reshape(-1), xs, w_gate, w_up, w_down, layer)
    ya, yb = _moe_collect(ys, pos[0], pos[1])
    return ya, yb, w


def _dft_tables(L):
    k = np.arange(L)[:, None]
    m = np.arange(L)[None, :]
    r = (k * m) % (2 * L)
    ang = np.pi * r.astype(np.float64) / L
    fc = np.cos(ang)
    fs = np.sin(ang)
    fs[0, :] = np.where(np.arange(L) % 2 == 0, 1.0, -1.0)
    wk = np.full((L, 1), 1.0 / L)
    wk[0, 0] = 0.5 / L
    gc = (fc * wk).T
    gs = (fs * wk).T
    return [jnp.asarray(t.astype(np.float32)).astype(BF16) for t in (fc, fs, gc, gs)]


def _filter_consts(L):
    t = np.linspace(0.0, 1.0, L, dtype=np.float32)[:, None]
    w = (np.float32(2.0 * np.pi) * np.arange(L, dtype=np.float32)[:, None] / np.float32(L)).astype(np.float32)
    fb = np.linspace(1e-4, HY_BANDS - 1, HY_BANDS, dtype=np.float32)[None, :]
    emb = np.concatenate([t, np.cos(fb * w), -np.sin(fb * w)], axis=-1).astype(np.float32)
    lo = math.log(HY_DECAY_TARGET) / HY_SLOW_PCT
    hi = math.log(HY_DECAY_TARGET) / HY_FAST_PCT
    deltas = np.abs(np.linspace(lo, hi, D_MODEL, dtype=np.float32))
    decay = np.exp(-t * deltas).astype(np.float32)
    return jnp.asarray(emb), jnp.asarray(decay)


def _filter_kernel(emb_ref, w1_ref, b1_ref, w2_ref, b2_ref, fr_ref, w3f_ref, w3b_ref, dec_ref,
                   fc_ref, fs_ref, kr_ref, q_ref, krn_ref, hd_ref):
    @pl.when(pl.program_id(0) == 0)
    def _():
        fr = fr_ref[...]
        h1 = jnp.sin(fr * (jnp.dot(emb_ref[...], w1_ref[...], precision=HIGHEST,
                                   preferred_element_type=F32) + b1_ref[...]))
        hd_ref[...] = jnp.sin(fr * (jnp.dot(h1, w2_ref[...], precision=HIGHEST,
                                            preferred_element_type=F32) + b2_ref[...]))

    hd = hd_ref[...]
    dec = dec_ref[...]
    f = jnp.dot(hd, w3f_ref[...], precision=HIGHEST, preferred_element_type=F32) * dec
    g = jnp.dot(hd, w3b_ref[...], precision=HIGHEST, preferred_element_type=F32) * dec
    row = lax.broadcasted_iota(jnp.int32, f.shape, 0)
    g = jnp.where(row == 0, 0.0, g)
    s = f + g
    d = f - g
    kr = jnp.dot(fc_ref[...], s.astype(BF16), preferred_element_type=F32)
    qq = jnp.dot(fs_ref[...], d.astype(BF16), preferred_element_type=F32)
    alt = jnp.where(row % 2 == 0, 1.0, -1.0)
    nyq = jnp.sum(alt * s, axis=0, keepdims=True)
    kr_ref[...] = kr
    q_ref[...] = jnp.where(row == 0, 0.0, qq)
    krn_ref[...] = jnp.where(row == 0, nyq, kr)


def _hyena_filter_spectrum(L, w1, b1, w2, b2, w3, freq, fc, fs, cblk=256):
    emb, decay = _filter_consts(L)
    ncb = D_MODEL // cblk
    n_emb = 128
    emb = jnp.pad(emb, ((0, 0), (0, n_emb - emb.shape[1])))
    w1 = jnp.pad(w1, ((0, n_emb - w1.shape[0]), (0, 0)))
    full = lambda shape: pl.BlockSpec(shape, lambda j: tuple(0 for _ in shape))
    out_sds = jax.ShapeDtypeStruct((L, D_MODEL), F32)
    out_spec = pl.BlockSpec((L, cblk), lambda j: (0, j))
    return pl.pallas_call(
        _filter_kernel,
        grid=(ncb,),
        in_specs=[
            full((L, n_emb)), full((n_emb, HY_FFN)), full((1, HY_FFN)), full((HY_FFN, HY_FFN)),
            full((1, HY_FFN)), full((1, HY_FFN)),
            pl.BlockSpec((HY_FFN, cblk), lambda j: (0, j)),
            pl.BlockSpec((HY_FFN, cblk), lambda j: (0, ncb + j)),
            pl.BlockSpec((L, cblk), lambda j: (0, j)),
            full((L, L)), full((L, L)),
        ],
        out_specs=(out_spec, out_spec, out_spec),
        out_shape=(out_sds, out_sds, out_sds),
        scratch_shapes=[pltpu.VMEM((L, HY_FFN), F32)],
        compiler_params=_params("arbitrary"),
        name=f"hyena_filter_{L}",
    )(emb, w1, b1.reshape(1, HY_FFN), w2, b2.reshape(1, HY_FFN), freq.reshape(1, HY_FFN), w3, w3, decay, fc, fs)


def _hyena_conv_kernel(x0_ref, x1_ref, v_ref, cw0_ref, cw1_ref, cwv_ref, cb0_ref, cb1_ref, cbv_ref,
                       kr_ref, q_ref, krn_ref, ds_ref, fc_ref, fs_ref, gc_ref, gs_ref, o_ref,
                       zz_ref, gate_ref, skip_ref, yr_ref, yw_ref):
    L = fc_ref.shape[0]
    unit_w = zz_ref.shape[2]
    units = [(slice(s * L, (s + 1) * L), slice(c * unit_w, (c + 1) * unit_w))
             for s in range(x0_ref.shape[0] // L) for c in range(x0_ref.shape[1] // unit_w)]
    row = lax.broadcasted_iota(jnp.int32, (L, unit_w), 0)

    def gating(i):
        rows, cols = units[i]

        def short_conv(u_ref, w_ref, b_ref):
            u = u_ref[rows, cols].astype(F32)
            w = w_ref[:, cols]
            prev = jnp.where(row == 0, 0.0, pltpu.roll(u, 1, axis=0))
            nxt = jnp.where(row == L - 1, 0.0, pltpu.roll(u, L - 1, axis=0))
            return prev * w[0:1, :] + u * w[1:2, :] + nxt * w[2:3, :] + b_ref[:, cols]

        x0 = short_conv(x0_ref, cw0_ref, cb0_ref)
        zz = short_conv(v_ref, cwv_ref, cbv_ref) * short_conv(x1_ref, cw1_ref, cb1_ref)
        zz_ref[i] = zz.astype(BF16)
        gate_ref[i] = x0
        skip_ref[i] = x0 * zz * ds_ref[:, cols]

    def spectrum(i):
        cols = units[i][1]
        ur = jnp.dot(fc_ref[...], zz_ref[i], preferred_element_type=F32)
        p = jnp.dot(fs_ref[...], zz_ref[i], preferred_element_type=F32)
        qq = q_ref[:, cols]
        yr_ref[i] = (ur * kr_ref[:, cols] - p * qq).astype(BF16)
        yw_ref[i] = (ur * qq + p * krn_ref[:, cols]).astype(BF16)

    def synthesis(i):
        rows, cols = units[i]
        y = jnp.dot(gc_ref[...], yr_ref[i], preferred_element_type=F32)
        y = y + jnp.dot(gs_ref[...], yw_ref[i], preferred_element_type=F32)
        o_ref[rows, cols] = (gate_ref[i] * y + skip_ref[i]).astype(o_ref.dtype)

    for t in range(len(units) + 2):
        if t < len(units):
            gating(t)
        if 0 <= t - 1 < len(units):
            spectrum(t - 1)
        if 0 <= t - 2 < len(units):
            synthesis(t - 2)


def _hyena_conv(u, conv_w, conv_b, dskip, spectrum, tables, *, latent):
    L = LATENT_LEN if latent else PROMPT_LEN
    n_seq = N_LATENT_SEQ if latent else N_PROMPT_SEQ
    cblk = 512
    unit_w = 256 if latent else 512
    ncb = D_MODEL // cblk
    seqs = 1 if latent else 8
    unit = (seqs * cblk // unit_w, L, unit_w)
    row0 = (N_PROMPT_TOK // L) if latent else 0
    kr, qq, krn = spectrum
    fc, fs, gc, gs = tables

    def part(p, rows):
        if rows != L:
            return pl.BlockSpec((rows, cblk), lambda j, s: (0, p * ncb + j))
        return pl.BlockSpec((seqs * L, cblk), lambda j, s: (row0 // seqs + s, p * ncb + j))

    def const_cols(rows):
        return pl.BlockSpec((rows, cblk), lambda j, s: (0, j))

    mat = pl.BlockSpec((L, L), lambda j, s: (0, 0))
    conv_b2 = conv_b.reshape(1, 3 * D_MODEL)
    in_specs = [part(0, L), part(1, L), part(2, L),
                part(0, 3), part(1, 3), part(2, 3),
                part(0, 1), part(1, 1), part(2, 1),
                const_cols(L), const_cols(L), const_cols(L), const_cols(1),
                mat, mat, mat, mat]
    args = [u, u, u, conv_w, conv_w, conv_w, conv_b2, conv_b2, conv_b2,
            kr, qq, krn, dskip.reshape(1, D_MODEL), fc, fs, gc, gs]
    return pl.pallas_call(
        _hyena_conv_kernel,
        grid=(ncb, n_seq // seqs),
        in_specs=in_specs,
        out_specs=pl.BlockSpec((seqs * L, cblk), lambda j, s: (s, j)),
        out_shape=jax.ShapeDtypeStruct((n_seq * L, D_MODEL), BF16),
        scratch_shapes=[pltpu.VMEM(unit, BF16), pltpu.VMEM(unit, F32), pltpu.VMEM(unit, F32),
                        pltpu.VMEM(unit, BF16), pltpu.VMEM(unit, BF16)],
        compiler_params=_params("arbitrary", "arbitrary"),
        name="hyena_conv_latent" if latent else "hyena_conv_prompt",
    )(*args)


def kernel(x_prompt, x_sample, cache_k, cache_v, state_hgrn, c, c_ctx, norm_g, mod_w, mod_b, ab_in_w, hgrn_lb, hgrn_onorm_g, attn_qnorm_g, attn_knorm_g, ab_out_w, hy_in_w, hy_in_b, hy_conv_w, hy_conv_b, hy_f_w1, hy_f_b1, hy_f_w2, hy_f_b2, hy_f_w3, hy_f_freq, hy_dskip, hy_out_w, router_w, router_b, moe_w_gate, moe_w_up, moe_w_down):
    xp = x_prompt.reshape(N_PROMPT_TOK, D_MODEL)
    xl = x_sample.reshape(N_LATENT_TOK, D_MODEL)
    cond = jnp.concatenate([c_ctx[None, :], c, jnp.zeros((N_COND - 1 - N_LATENT_SEQ, D_MODEL), F32)], axis=0)
    mod = _modulation(cond, mod_w, mod_b)
    router_wp = jnp.pad(router_w, ((0, 0), (0, ROUTER_LANES - N_EXPERTS)))

    z = _in_proj0(xp, xl, norm_g[0, 0], mod[0], ab_in_w[0])
    oa_p, new_state = _hgrn(z, hgrn_lb, hgrn_onorm_g[0], None, latent=False)
    oa_l = _hgrn(z, hgrn_lb, hgrn_onorm_g[0], state_hgrn, latent=True)
    ob_p, k_fm, v_fm = _attention_prompt(z, attn_qnorm_g[0], attn_knorm_g[0])
    fm_shape = (N_PROMPT_SEQ, 1, KV_HEADS, HEAD_DIM, PROMPT_LEN)
    new_k = jnp.swapaxes(k_fm.reshape(fm_shape), -1, -2)
    new_v = jnp.swapaxes(v_fm.reshape(fm_shape), -1, -2)
    ob_l = _attention_latent(z, attn_qnorm_g[0], attn_knorm_g[0], cache_k, cache_v)
    x, h, logits_t = _out_proj([(oa_p, oa_l), (ob_p, ob_l)], ab_out_w[0], (xp, xl), norm_g[0, 1], mod[0],
                               router_wp)
    moe_out = _moe(h, logits_t, router_b, moe_w_gate, moe_w_up, moe_w_down, 0)

    x, u = _in_proj1(x, moe_out, mod[0], norm_g[1, 0], mod[1], hy_in_w[0], hy_in_b[0])
    pre = []
    for latent in (False, True):
        L = LATENT_LEN if latent else PROMPT_LEN
        tables = _dft_tables(L)
        spectrum = _hyena_filter_spectrum(L, hy_f_w1[0], hy_f_b1[0], hy_f_w2[0], hy_f_b2[0], hy_f_w3[0],
                                          hy_f_freq[0], tables[0], tables[1])
        pre.append(_hyena_conv(u, hy_conv_w[0], hy_conv_b[0], hy_dskip[0], spectrum, tables, latent=latent))
    x, h, logits_t = _out_proj([tuple(pre)], hy_out_w[0], (x,), norm_g[1, 1], mod[1], router_wp)
    moe_out = _moe(h, logits_t, router_b, moe_w_gate, moe_w_up, moe_w_down, 1)

    y_prompt = _combine(x, moe_out, mod[1], 0, N_PROMPT_TOK).reshape(N_PROMPT_SEQ, PROMPT_LEN, D_MODEL)
    y_sample = _combine(x, moe_out, mod[1], N_PROMPT_TOK, N_LATENT_TOK).reshape(N_LATENT_SEQ, LATENT_LEN, D_MODEL)
    return (y_prompt, y_sample, new_k, new_v, new_state)
```

```python
import functools
import math

import numpy as np
import jax
import jax.numpy as jnp
from jax import lax
from jax.experimental import pallas as pl
from jax.experimental.pallas import tpu as pltpu
from jax.experimental.pallas import tpu_sc as plsc

F32 = jnp.float32
BF16 = jnp.bfloat16
HIGHEST = lax.Precision.HIGHEST

D_MODEL = 1024
N_PROMPT_SEQ = 32
PROMPT_LEN = 256
N_LATENT_SEQ = 2
LATENT_LEN = 1024
PAST_LEN = 512
GRID_W = 64
N_PROMPT_TOK = N_PROMPT_SEQ * PROMPT_LEN
N_LATENT_TOK = N_LATENT_SEQ * LATENT_LEN
N_TOK = N_PROMPT_TOK + N_LATENT_TOK
N_COND = 8
EPS = 1e-6

A_WIDTH = 512
A_HEADS = 4
A_DK = 128
CHUNK = 64
HGRN_BLOCK = 128
HGRN_HEADS_PER_STEP = 4
HEAD_DIM = 64
Q_HEADS = 8
KV_HEADS = 2
Q_PER_KV = Q_HEADS // KV_HEADS
Q_BLOCK = 256
ROPE_THETA = 10000.0
ROPE_PAIRS = HEAD_DIM // 4
AB_IN = 5 * A_WIDTH + (Q_HEADS + 2 * KV_HEADS) * HEAD_DIM

HY_BANDS = 16
HY_FFN = 64
HY_DECAY_TARGET = 1e-2
HY_FAST_PCT = 0.3
HY_SLOW_PCT = 1.5

N_EXPERTS = 16
N_GROUPS = 4
EXPERTS_PER_GROUP = 4
TOP_K = 2
D_EXPERT = 512
ROUTER_LANES = 128
OUT_PROJ_SUB_ROWS = 256
EXPERT_SUB_ROWS = 256
IN_PROJ_SUB_ROWS = 256
MOE_TILE = 512
MOE_ROWS = N_TOK * TOP_K + N_EXPERTS * MOE_TILE
PLAN_LANES = 128

SC_CORES = 2
SC_WORKERS = 32
SC_TOKENS_PER_WORKER = N_TOK // SC_WORKERS
SC_CHUNK = 40
ROW_WORDS = D_MODEL // 2

VMEM_LIMIT = 56 * 1024 * 1024


def _params(*sem):
    return pltpu.CompilerParams(dimension_semantics=sem, vmem_limit_bytes=VMEM_LIMIT)


def _pack_rows(x):
    n = x.shape[1] // 2
    bits = pltpu.bitcast(x.astype(BF16).astype(F32), jnp.uint32)
    return pltpu.bitcast(bits[:, :n] | (bits[:, n:] >> 16), jnp.int32)


def _unpack_rows(p):
    bits = pltpu.bitcast(p, jnp.uint32)
    hi = pltpu.bitcast(bits & jnp.uint32(0xFFFF0000), F32)
    lo = pltpu.bitcast(bits << 16, F32)
    return jnp.concatenate([hi, lo], axis=1)


def _cond_of_token_block(i, block_rows):
    start = i * block_rows
    return jnp.where(start < N_PROMPT_TOK, 0, 1 + (start - N_PROMPT_TOK) // LATENT_LEN)


def _mod_kernel(cond_ref, w_ref, b_ref, o_ref):
    cnd = cond_ref[...]
    s = cnd * jax.nn.sigmoid(cnd)
    s_hi = s.astype(BF16)
    s_lo = (s - s_hi.astype(F32)).astype(BF16)
    w = w_ref[...]
    w_hi = w.astype(BF16)
    w_lo = (w - w_hi.astype(F32)).astype(BF16)
    acc = jnp.dot(s_hi, w_hi, preferred_element_type=F32)
    acc = acc + jnp.dot(s_lo, w_hi, preferred_element_type=F32)
    acc = acc + jnp.dot(s_hi, w_lo, preferred_element_type=F32)
    o_ref[...] = acc + b_ref[...]


def _modulation(cond, mod_w, mod_b):
    depth = mod_w.shape[0]
    n_mod = 6
    cols = 2 * D_MODEL
    n_step = n_mod * D_MODEL // cols
    out = pl.pallas_call(
        _mod_kernel,
        grid=(depth, n_step),
        in_specs=[
            pl.BlockSpec((N_COND, D_MODEL), lambda l, j: (0, 0)),
            pl.BlockSpec((None, D_MODEL, cols), lambda l, j: (l, 0, j)),
            pl.BlockSpec((None, 1, cols), lambda l, j: (l, 0, j)),
        ],
        out_specs=pl.BlockSpec((None, N_COND, cols), lambda l, j: (l, 0, j)),
        out_shape=jax.ShapeDtypeStruct((depth, N_COND, n_mod * D_MODEL), F32),
        compiler_params=_params("arbitrary", "arbitrary"),
        name="modulation",
    )(cond, mod_w, mod_b.reshape(depth, 1, n_mod * D_MODEL))
    return out.reshape(depth, N_COND, n_mod, D_MODEL)


def _modulated_norm(x, g, mod, shift_row, scale_row):
    ms = jnp.mean(x * x, axis=-1, keepdims=True)
    y = x * lax.rsqrt(ms + EPS) * g
    return y * (1.0 + mod[scale_row:scale_row + 1, :]) + mod[shift_row:shift_row + 1, :]


def _trunk_specs(block_rows, width):
    n_prompt_blocks = N_PROMPT_TOK // block_rows
    return (pl.BlockSpec((block_rows, width), lambda i: (jnp.minimum(i, n_prompt_blocks - 1), 0)),
            pl.BlockSpec((block_rows, width), lambda i: (jnp.maximum(i - n_prompt_blocks, 0), 0)))


def _select_trunk(p_ref, l_ref, rows=slice(None)):
    block_rows = p_ref.shape[0]
    return jnp.where(pl.program_id(0) < N_PROMPT_TOK // block_rows, p_ref[rows, :], l_ref[rows, :])


def _cast_once(w_ref, wb_ref):
    @pl.when(pl.program_id(0) == 0)
    def _():
        wb_ref[...] = w_ref[...].astype(BF16)


def _resident(shape):
    return pl.BlockSpec(shape, lambda i: tuple(0 for _ in shape), pipeline_mode=pl.Buffered(1))


def _mod_spec(block_rows):
    return pl.BlockSpec((None, 6, D_MODEL), lambda i: (_cond_of_token_block(i, block_rows), 0, 0))


def _in_proj0_kernel(xp_ref, xl_ref, g_ref, mod_ref, w_ref, o_ref, wb_ref, hb_ref):
    _cast_once(w_ref, wb_ref)
    n = IN_PROJ_SUB_ROWS
    n_sub = xp_ref.shape[0] // n

    def prepare(r):
        x = _select_trunk(xp_ref, xl_ref, slice(r * n, (r + 1) * n))
        hb_ref[r] = _modulated_norm(x, g_ref[...], mod_ref[...], 0, 1).astype(BF16)

    def project(r):
        u = jnp.dot(hb_ref[r], wb_ref[...], preferred_element_type=F32)
        o_ref[r * n:(r + 1) * n, :] = u.astype(o_ref.dtype)

    prepare(0)
    for r in range(1, n_sub):
        prepare(r)
        project(r - 1)
    project(n_sub - 1)


def _in_proj0(x_prompt, x_latent, g, mod_l, w, block_rows=512):
    n = w.shape[1]
    return pl.pallas_call(
        _in_proj0_kernel,
        grid=(N_TOK // block_rows,),
        in_specs=[*_trunk_specs(block_rows, D_MODEL), _resident((1, D_MODEL)), _mod_spec(block_rows),
                  _resident((D_MODEL, n))],
        out_specs=pl.BlockSpec((block_rows, n), lambda i: (i, 0)),
        out_shape=jax.ShapeDtypeStruct((N_TOK, n), BF16),
        scratch_shapes=[pltpu.VMEM((D_MODEL, n), BF16),
                        pltpu.VMEM((block_rows // IN_PROJ_SUB_ROWS, IN_PROJ_SUB_ROWS, D_MODEL), BF16)],
        compiler_params=_params("arbitrary"),
        name="in_proj0",
    )(x_prompt, x_latent, g.reshape(1, D_MODEL), mod_l, w)


def _moe_mix(x_ref, ya_ref, yb_ref, wt_ref, mod_ref, rows=slice(None)):
    wt = wt_ref[rows, :]
    mix = wt[:, 0:1] * _unpack_rows(ya_ref[rows, :]) + wt[:, 1:2] * _unpack_rows(yb_ref[rows, :])
    return x_ref[rows, :] + mod_ref[5:6, :] * mix


def _in_proj1_kernel(x_ref, ya_ref, yb_ref, wt_ref, modp_ref, g_ref, mod_ref, w_ref, b_ref, xo_ref, o_ref,
                     wb_ref, hb_ref):
    _cast_once(w_ref, wb_ref)
    n = IN_PROJ_SUB_ROWS
    n_sub = x_ref.shape[0] // n

    def prepare(r):
        rows = slice(r * n, (r + 1) * n)
        x = _moe_mix(x_ref, ya_ref, yb_ref, wt_ref, modp_ref, rows)
        xo_ref[rows, :] = x
        hb_ref[r] = _modulated_norm(x, g_ref[...], mod_ref[...], 0, 1).astype(BF16)

    def project(r):
        rows = slice(r * n, (r + 1) * n)
        u = jnp.dot(hb_ref[r], wb_ref[...], preferred_element_type=F32) + b_ref[...]
        o_ref[rows, :] = u.astype(o_ref.dtype)

    prepare(0)
    for r in range(1, n_sub):
        prepare(r)
        project(r - 1)
    project(n_sub - 1)


def _in_proj1(x, moe_out, mod_prev, g, mod_l, w, bias, block_rows=512):
    ya, yb, w_tok = moe_out
    n = w.shape[1]
    tok = pl.BlockSpec((block_rows, D_MODEL), lambda i: (i, 0))
    packed = pl.BlockSpec((block_rows, ROW_WORDS), lambda i: (i, 0))
    return pl.pallas_call(
        _in_proj1_kernel,
        grid=(N_TOK // block_rows,),
        in_specs=[tok, packed, packed, pl.BlockSpec((block_rows, TOP_K), lambda i: (i, 0)), _mod_spec(block_rows),
                  _resident((1, D_MODEL)), _mod_spec(block_rows), _resident((D_MODEL, n)), _resident((1, n))],
        out_specs=(tok, pl.BlockSpec((block_rows, n), lambda i: (i, 0))),
        out_shape=(jax.ShapeDtypeStruct((N_TOK, D_MODEL), F32), jax.ShapeDtypeStruct((N_TOK, n), BF16)),
        scratch_shapes=[pltpu.VMEM((D_MODEL, n), BF16),
                        pltpu.VMEM((block_rows // IN_PROJ_SUB_ROWS, IN_PROJ_SUB_ROWS, D_MODEL), BF16)],
        compiler_params=_params("arbitrary"),
        name="in_proj1",
    )(x, ya, yb, w_tok, mod_prev, g.reshape(1, D_MODEL), mod_l, w, bias.reshape(1, n))


def _hgrn_kernel(*refs, seq_len, with_state):
    if with_state:
        (q_ref, zf_ref, zb_ref, i_ref, ga_ref, lb_ref, og_ref, s0_ref, o_ref, of_ref, ob_ref) = refs
    else:
        (q_ref, zf_ref, zb_ref, i_ref, ga_ref, lb_ref, og_ref, o_ref, s_ref, of_ref, ob_ref) = refs
    n_blocks = seq_len // HGRN_BLOCK
    chunks_per_block = HGRN_BLOCK // CHUNK

    lbr = lb_ref[...]
    mx = jnp.maximum(lbr[0], lbr[1])
    e0 = jnp.exp(lbr[0] - mx)
    e1 = jnp.exp(lbr[1] - mx)
    lb = e0 / (e0 + e1)

    row = lax.broadcasted_iota(jnp.int32, (HGRN_BLOCK, HGRN_BLOCK), 0)
    col = lax.broadcasted_iota(jnp.int32, (HGRN_BLOCK, HGRN_BLOCK), 1)
    same_chunk = (row // CHUNK) == (col // CHUNK)
    nt = (((1,), (1,)), ((), ()))
    tn = (((0,), (0,)), ((), ()))

    def per_chunk_row(x, idx):
        return jnp.concatenate(
            [jnp.broadcast_to(x[n * CHUNK + idx:n * CHUNK + idx + 1, :], (CHUNK, x.shape[1]))
             for n in range(chunks_per_block)], axis=0)

    def in_chunk_cumsum(tri, x):
        hi = x.astype(BF16)
        lo = (x - hi.astype(F32)).astype(BF16)
        return jnp.dot(tri, hi, preferred_element_type=F32) + jnp.dot(tri, lo, preferred_element_type=F32)

    def prepare(blk, cols, z_ref, lbd, forward):
        rows = slice(blk * HGRN_BLOCK, (blk + 1) * HGRN_BLOCK)
        keep = (same_chunk & (col <= row)) if forward else (same_chunk & (col >= row))
        tri = jnp.where(keep, 1.0, 0.0).astype(BF16)
        mid = CHUNK // 2 if forward else CHUNK - 1 - CHUNK // 2
        last = CHUNK - 1 if forward else 0
        f = lbd + (1.0 - lbd) * jax.nn.sigmoid(z_ref[rows, cols].astype(F32))
        lf = jnp.log(f)
        k = 1.0 - f
        q = q_ref[rows, cols].astype(F32)
        b = in_chunk_cumsum(tri, lf)
        bm = per_chunk_row(b, mid)
        bl = per_chunk_row(b, last)
        return dict(
            rows=rows, cols=cols, keep=keep, forward=forward,
            vb=i_ref[rows, cols].astype(BF16),
            qe=(q * jnp.exp(b - bm)).astype(BF16), ke=(k * jnp.exp(bm - b)).astype(BF16),
            qb=(q * jnp.exp(b)).astype(BF16), ks=(k * jnp.exp(bl - b)).astype(BF16), decay=jnp.exp(bl))

    def within_chunks(u):
        att = lax.dot_general(u["qe"], u["ke"], nt, preferred_element_type=F32)
        att = jnp.where(u["keep"], att, 0.0)
        u["o_intra"] = jnp.dot(att.astype(BF16), u["vb"], preferred_element_type=F32)
        u["upd"] = [lax.dot_general(u["vb"][n * CHUNK:(n + 1) * CHUNK], u["ks"][n * CHUNK:(n + 1) * CHUNK], tn,
                                    preferred_element_type=F32) for n in range(chunks_per_block)]

    def across_chunks(u, st, out_ref):
        order = range(chunks_per_block) if u["forward"] else range(chunks_per_block - 1, -1, -1)
        o_inter = [None] * chunks_per_block
        for n in order:
            cr = slice(n * CHUNK, (n + 1) * CHUNK)
            o_inter[n] = lax.dot_general(u["qb"][cr], st.astype(BF16), nt, preferred_element_type=F32)
            st = st * u["decay"][n * CHUNK:n * CHUNK + 1, :] + u["upd"][n]
        out_ref[u["rows"], u["cols"]] = u["o_intra"] + jnp.concatenate(o_inter, axis=0)
        return st

    n_heads = q_ref.shape[1] // A_DK
    head_cols = [slice(hd * A_DK, (hd + 1) * A_DK) for hd in range(n_heads)]
    if with_state:
        states = {(hd, d): s0_ref[d, hd].T for hd in range(n_heads) for d in range(2)}
    else:
        states = {(hd, d): jnp.zeros((A_DK, A_DK), F32) for hd in range(n_heads) for d in range(2)}
    for step in range(n_blocks):
        units = {}
        for hd, cols in enumerate(head_cols):
            units[hd, 0] = prepare(step, cols, zf_ref, lb[0:1, cols], True)
            units[hd, 1] = prepare(n_blocks - 1 - step, cols, zb_ref, lb[1:2, cols], False)
        for u in units.values():
            within_chunks(u)
        for key, u in units.items():
            states[key] = across_chunks(u, states[key], of_ref if key[1] == 0 else ob_ref)
    for hd, cols in enumerate(head_cols):
        if not with_state:
            s_ref[0, hd] = states[hd, 0].T
            s_ref[1, hd] = states[hd, 1].T
        o = of_ref[:, cols] + ob_ref[:, cols]
        o = o * lax.rsqrt(jnp.mean(o * o, axis=-1, keepdims=True) + EPS) * og_ref[:, cols]
        ga = ga_ref[:, cols].astype(F32)
        o_ref[:, cols] = (o * (ga * jax.nn.sigmoid(ga))).astype(o_ref.dtype)


def _hgrn(z, hgrn_lb, onorm_g, state, *, latent):
    seq_len = LATENT_LEN if latent else PROMPT_LEN
    n_seq = N_LATENT_SEQ if latent else N_PROMPT_SEQ
    row0 = (N_PROMPT_TOK // seq_len) if latent else 0

    hw = HGRN_HEADS_PER_STEP * A_DK
    n_hg = A_HEADS // HGRN_HEADS_PER_STEP

    def zspec(part):
        return pl.BlockSpec((seq_len, hw), lambda s, h: (row0 + s, part * n_hg + h))

    in_specs = [zspec(0), zspec(1), zspec(2), zspec(3), zspec(4),
                pl.BlockSpec((2, 2, hw), lambda s, h: (0, 0, h)),
                pl.BlockSpec((1, hw), lambda s, h: (0, h))]
    args = [z, z, z, z, z, hgrn_lb, onorm_g.reshape(1, A_WIDTH)]
    state_spec = pl.BlockSpec((None, None, 2, HGRN_HEADS_PER_STEP, A_DK, A_DK), lambda s, h: (s, 0, 0, h, 0, 0))
    o_shape = jax.ShapeDtypeStruct((n_seq * seq_len, A_WIDTH), BF16)
    o_spec = pl.BlockSpec((seq_len, hw), lambda s, h: (s, h))
    if latent:
        in_specs.append(state_spec)
        args.append(state)
        out_shape, out_specs = o_shape, o_spec
    else:
        out_shape = (o_shape, jax.ShapeDtypeStruct((n_seq, 1, 2, A_HEADS, A_DK, A_DK), F32))
        out_specs = (o_spec, state_spec)
    return pl.pallas_call(
        functools.partial(_hgrn_kernel, seq_len=seq_len, with_state=latent),
        grid=(n_seq, n_hg),
        in_specs=in_specs,
        out_specs=out_specs,
        out_shape=out_shape,
        scratch_shapes=[pltpu.VMEM((seq_len, hw), F32), pltpu.VMEM((seq_len, hw), F32)],
        compiler_params=_params("arbitrary", "arbitrary"),
        name="hgrn_latent" if latent else "hgrn_prompt",
    )(*args)


def _rope_tables():
    pos = np.arange(LATENT_LEN)
    row, colp = pos // GRID_W, pos % GRID_W
    inv = ROPE_THETA ** (-np.arange(ROPE_PAIRS, dtype=np.float32) / ROPE_PAIRS)
    inv = inv.astype(np.float32)
    ang_r = (row.astype(np.float32)[:, None] * inv).astype(np.float32)
    ang_c = (colp.astype(np.float32)[:, None] * inv).astype(np.float32)
    cos = np.concatenate([np.cos(ang_r), np.cos(ang_r), np.cos(ang_c), np.cos(ang_c)], axis=1)
    sin = np.concatenate([-np.sin(ang_r), np.sin(ang_r), -np.sin(ang_c), np.sin(ang_c)], axis=1)
    return cos.astype(np.float32), sin.astype(np.float32)


def _head_mean_matrix(width):
    idx = np.arange(width) // HEAD_DIM
    return jnp.asarray((idx[:, None] == idx[None, :]).astype(np.float32) / HEAD_DIM).astype(BF16)


def _attn_kernel(*refs, latent):
    if latent:
        (q_ref, k_ref, v_ref, qg_ref, kg_ref, gq_ref, gk_ref, cosq_ref, sinq_ref, cosk_ref, sink_ref,
         ck_ref, cv_ref, o_ref) = refs
    else:
        (q_ref, k_ref, v_ref, qg_ref, kg_ref, gq_ref, gk_ref, o_ref, kout_ref, vout_ref) = refs
    pair_w = 2 * HEAD_DIM

    def head_norm(x, mean_ref, gain):
        sq = x * x
        hi = sq.astype(BF16)
        lo = (sq - hi.astype(F32)).astype(BF16)
        ms = jnp.dot(hi, mean_ref[...], preferred_element_type=F32)
        ms = ms + jnp.dot(lo, mean_ref[...], preferred_element_type=F32)
        return x * lax.rsqrt(ms + EPS) * gain

    def rope(x, cos, sin):
        n = x.shape[1]
        lane = lax.broadcasted_iota(jnp.int32, x.shape, 1)
        first_of_pair = (lane // ROPE_PAIRS) % 2 == 0
        swapped = jnp.where(first_of_pair, pltpu.roll(x, n - ROPE_PAIRS, axis=1), pltpu.roll(x, ROPE_PAIRS, axis=1))
        return x * cos + swapped * sin

    nt = (((1,), (1,)), ((), ()))

    def prepare(rows, seq_idx):
        q = head_norm(q_ref[rows, :].astype(F32), gq_ref, qg_ref[...])
        k = head_norm(k_ref[rows, :].astype(F32), gk_ref, kg_ref[...])
        if latent:
            q = rope(q, cosq_ref[...], sinq_ref[...])
            k = rope(k, cosk_ref[...], sink_ref[...])
        q = q * (HEAD_DIM ** -0.5)
        v = v_ref[rows, :].astype(F32)
        n_q = q.shape[0]
        low_kv = lax.broadcasted_iota(jnp.int32, k.shape, 1) < HEAD_DIM
        low_q = lax.broadcasted_iota(jnp.int32, (n_q, pair_w), 1) < HEAD_DIM
        k_swapped = pltpu.roll(k, HEAD_DIM, axis=1)
        v_swapped = pltpu.roll(v, HEAD_DIM, axis=1)
        if not latent:
            kout_ref[seq_idx] = k.T
            vout_ref[seq_idx] = v.T
        units = []
        for j in range(KV_HEADS):
            kd = (jnp.where(low_kv, k, k_swapped) if j == 0 else jnp.where(low_kv, k_swapped, k)).astype(BF16)
            vd = (jnp.where(low_kv, v, v_swapped) if j == 0 else jnp.where(low_kv, v_swapped, v)).astype(BF16)
            vd = jnp.concatenate([vd, jnp.ones_like(vd)], axis=1)
            tiles = range(j * Q_PER_KV // 2, (j + 1) * Q_PER_KV // 2)
            parts = []
            for t in tiles:
                qt = q[:, t * pair_w:(t + 1) * pair_w]
                parts += [jnp.where(low_q, qt, 0.0), jnp.where(low_q, 0.0, qt)]
            units.append(dict(j=j, rows=rows, tiles=tiles, n_q=n_q, low_q=low_q, kd=kd, vd=vd,
                              qs=jnp.concatenate(parts, axis=0).astype(BF16)))
        return units

    def scores(u):
        u["s_new"] = lax.dot_general(u["qs"], u["kd"], nt, preferred_element_type=F32)
        if latent:
            j = u["j"]
            cvd = jnp.concatenate([cv_ref[j], cv_ref[j]], axis=1).astype(BF16)
            u["cvd"] = jnp.concatenate([cvd, jnp.ones_like(cvd)], axis=1)
            ckd = jnp.concatenate([ck_ref[j], ck_ref[j]], axis=1).astype(BF16)
            u["s_old"] = lax.dot_general(u["qs"], ckd, nt, preferred_element_type=F32)

    def softmax(u):
        m = jnp.max(u["s_new"], axis=-1, keepdims=True)
        if latent:
            m = jnp.maximum(m, jnp.max(u["s_old"], axis=-1, keepdims=True))
        u["p_new"] = jnp.exp(u.pop("s_new") - m).astype(BF16)
        if latent:
            u["p_old"] = jnp.exp(u.pop("s_old") - m).astype(BF16)

    def weighted_values(u):
        acc = jnp.dot(u["p_new"], u["vd"], preferred_element_type=F32)
        if latent:
            acc = acc + jnp.dot(u["p_old"], u["cvd"], preferred_element_type=F32)
        out = acc[:, :pair_w] / acc[:, pair_w:]
        n_q = u["n_q"]
        for i, t in enumerate(u["tiles"]):
            lo_head = out[(2 * i) * n_q:(2 * i + 1) * n_q, :]
            hi_head = out[(2 * i + 1) * n_q:(2 * i + 2) * n_q, :]
            o_ref[u["rows"], t * pair_w:(t + 1) * pair_w] = jnp.where(u["low_q"], lo_head, hi_head).astype(o_ref.dtype)

    if latent:
        units = prepare(slice(None), None)
    else:
        seq = PROMPT_LEN
        units = [u for s in range(q_ref.shape[0] // seq) for u in prepare(slice(s * seq, (s + 1) * seq), s)]
    for phase in (scores, softmax, weighted_values):
        for u in units:
            phase(u)


def _attn_common_args(qn_g, kn_g):
    q_w, kv_w = Q_HEADS * HEAD_DIM, KV_HEADS * HEAD_DIM
    return (jnp.tile(qn_g, Q_HEADS).reshape(1, q_w), jnp.tile(kn_g, KV_HEADS).reshape(1, kv_w),
            _head_mean_matrix(q_w), _head_mean_matrix(kv_w))


def _attention_prompt(z, qn_g, kn_g):
    seqs = 8
    L = seqs * PROMPT_LEN
    cache_shape = jax.ShapeDtypeStruct((N_PROMPT_SEQ, KV_HEADS * HEAD_DIM, PROMPT_LEN), F32)
    cache_spec = pl.BlockSpec((seqs, KV_HEADS * HEAD_DIM, PROMPT_LEN), lambda s: (s, 0, 0))
    q_w, kv_w = Q_HEADS * HEAD_DIM, KV_HEADS * HEAD_DIM
    q_col = (5 * A_WIDTH) // q_w
    k_col = (5 * A_WIDTH + q_w) // kv_w
    const = lambda r, c: pl.BlockSpec((r, c), lambda s: (0, 0))
    return pl.pallas_call(
        functools.partial(_attn_kernel, latent=False),
        grid=(N_PROMPT_TOK // L,),
        in_specs=[
            pl.BlockSpec((L, q_w), lambda s: (s, q_col)),
            pl.BlockSpec((L, kv_w), lambda s: (s, k_col)),
            pl.BlockSpec((L, kv_w), lambda s: (s, k_col + 1)),
            const(1, q_w), const(1, kv_w), const(q_w, q_w), const(kv_w, kv_w),
        ],
        out_specs=(pl.BlockSpec((L, q_w), lambda s: (s, 0)), cache_spec, cache_spec),
        out_shape=(jax.ShapeDtypeStruct((N_PROMPT_TOK, q_w), BF16), cache_shape, cache_shape),
        compiler_params=_params("arbitrary"),
        name="attn_prompt",
    )(z, z, z, *_attn_common_args(qn_g, kn_g))


def _attention_latent(z, qn_g, kn_g, cache_k, cache_v):
    L = LATENT_LEN
    nqb = L // Q_BLOCK
    q_w, kv_w = Q_HEADS * HEAD_DIM, KV_HEADS * HEAD_DIM
    q_col = (5 * A_WIDTH) // q_w
    k_col = (5 * A_WIDTH + q_w) // kv_w
    qrow0 = N_PROMPT_TOK // Q_BLOCK
    krow0 = N_PROMPT_TOK // L
    cos, sin = _rope_tables()
    cos_q, sin_q = jnp.asarray(np.tile(cos, (1, Q_HEADS))), jnp.asarray(np.tile(sin, (1, Q_HEADS)))
    cos_k, sin_k = jnp.asarray(np.tile(cos, (1, KV_HEADS))), jnp.asarray(np.tile(sin, (1, KV_HEADS)))
    const = lambda r, c: pl.BlockSpec((r, c), lambda s, b: (0, 0))
    cache_spec = pl.BlockSpec((None, None, KV_HEADS, PAST_LEN, HEAD_DIM), lambda s, b: (s, 0, 0, 0, 0))
    return pl.pallas_call(
        functools.partial(_attn_kernel, latent=True),
        grid=(N_LATENT_SEQ, nqb),
        in_specs=[
            pl.BlockSpec((Q_BLOCK, q_w), lambda s, b: (qrow0 + s * nqb + b, q_col)),
            pl.BlockSpec((L, kv_w), lambda s, b: (krow0 + s, k_col)),
            pl.BlockSpec((L, kv_w), lambda s, b: (krow0 + s, k_col + 1)),
            const(1, q_w), const(1, kv_w), const(q_w, q_w), const(kv_w, kv_w),
            pl.BlockSpec((Q_BLOCK, q_w), lambda s, b: (b, 0)),
            pl.BlockSpec((Q_BLOCK, q_w), lambda s, b: (b, 0)),
            const(L, kv_w), const(L, kv_w),
            cache_spec, cache_spec,
        ],
        out_specs=pl.BlockSpec((Q_BLOCK, q_w), lambda s, b: (s * nqb + b, 0)),
        out_shape=jax.ShapeDtypeStruct((N_LATENT_TOK, q_w), BF16),
        compiler_params=_params("arbitrary", "arbitrary"),
        name="attn_latent",
    )(z, z, z, *_attn_common_args(qn_g, kn_g), cos_q, sin_q, cos_k, sin_k, cache_k, cache_v)


def _out_proj_kernel(*refs, n_act, n_x):
    a_refs = refs[:2 * n_act]
    x_refs = refs[2 * n_act:2 * n_act + n_x]
    g_ref, mod_ref, rw_ref, w_ref, xo_ref, h_ref, lg_ref, wb_ref, rws_ref, acc_ref = refs[2 * n_act + n_x:]
    _cast_once(w_ref, wb_ref)

    @pl.when(pl.program_id(0) == 0)
    def _():
        rw = rw_ref[...]
        hi = rw.astype(BF16).astype(F32)
        lo = (rw - hi).astype(BF16).astype(F32)
        rws_ref[...] = (hi + pltpu.roll(lo, N_EXPERTS, axis=1)).astype(BF16)

    mod = mod_ref[...]
    n = OUT_PROJ_SUB_ROWS

    n_sub = xo_ref.shape[0] // n

    def sub_rows(r):
        if isinstance(r, int):
            return slice(r * n, (r + 1) * n)
        return pl.ds(pl.multiple_of(r * n, n), n)

    def project(r):
        rows = sub_rows(r)
        acc = None
        k0 = 0
        for ap_ref, al_ref in zip(a_refs[0::2], a_refs[1::2]):
            k1 = k0 + ap_ref.shape[1]
            part = jnp.dot(_select_trunk(ap_ref, al_ref, rows), wb_ref[k0:k1, :], preferred_element_type=F32)
            acc = part if acc is None else acc + part
            k0 = k1
        acc_ref[r % 2] = acc

    def finish(r):
        rows = sub_rows(r)
        x_in = x_refs[0][rows, :] if n_x == 1 else _select_trunk(*x_refs, rows)
        x = x_in + mod[2:3, :] * acc_ref[r % 2]
        xo_ref[rows, :] = x
        h = _modulated_norm(x, g_ref[...], mod, 3, 4)
        h_ref[rows, :] = _pack_rows(h)
        h_hi = h.astype(BF16)
        h_lo = (h - h_hi.astype(F32)).astype(BF16)
        both = jnp.dot(jnp.concatenate([h_hi, h_lo], axis=0), rws_ref[...], preferred_element_type=F32)
        from_hi, from_lo = both[:n], both[n:]
        lg = from_hi + pltpu.roll(from_hi, ROUTER_LANES - N_EXPERTS, axis=1) + from_lo
        lg_ref[:, rows] = lg.T[:N_EXPERTS, :]

    project(0)
    for r in range(n_sub - 1):
        project(r + 1)
        finish(r)
    finish(n_sub - 1)


def _out_proj(acts, w, xs, g, mod_l, router_wp, block_rows=1024):
    tok = lambda width: pl.BlockSpec((block_rows, width), lambda i: (i, 0))
    in_specs = [spec for ap, _ in acts for spec in _trunk_specs(block_rows, ap.shape[1])]
    in_specs += [tok(D_MODEL)] if len(xs) == 1 else list(_trunk_specs(block_rows, D_MODEL))
    in_specs += [_resident((1, D_MODEL)), _mod_spec(block_rows), _resident((D_MODEL, ROUTER_LANES)),
                 _resident(w.shape)]
    return pl.pallas_call(
        functools.partial(_out_proj_kernel, n_act=len(acts), n_x=len(xs)),
        grid=(N_TOK // block_rows,),
        in_specs=in_specs,
        out_specs=(tok(D_MODEL), tok(ROW_WORDS), pl.BlockSpec((N_EXPERTS, block_rows), lambda i: (0, i))),
        out_shape=(jax.ShapeDtypeStruct((N_TOK, D_MODEL), F32),
                   jax.ShapeDtypeStruct((N_TOK, ROW_WORDS), jnp.int32),
                   jax.ShapeDtypeStruct((N_EXPERTS, N_TOK), F32)),
        scratch_shapes=[pltpu.VMEM(w.shape, BF16), pltpu.VMEM((D_MODEL, ROUTER_LANES), BF16),
                        pltpu.VMEM((2, OUT_PROJ_SUB_ROWS, D_MODEL), F32)],
        compiler_params=_params("arbitrary"),
        name="out_proj",
    )(*[a for pair in acts for a in pair], *xs, g.reshape(1, D_MODEL), mod_l, router_wp, w)


def _router_kernel(lg_ref, rb_ref, pos_ref, w_ref, plan_ref, rank_ref):
    lg = lg_ref[...]
    ex = jnp.exp(lg - jnp.max(lg, axis=0, keepdims=True))
    scores = ex / jnp.sum(ex, axis=0, keepdims=True)
    biased = scores + rb_ref[...]
    expert = lax.broadcasted_iota(jnp.int32, biased.shape, 0)
    in_pos = expert % EXPERTS_PER_GROUP
    rank = jnp.zeros_like(biased)
    for d in range(1, EXPERTS_PER_GROUP):
        wraps = in_pos + d >= EXPERTS_PER_GROUP
        partner = jnp.where(wraps, pltpu.roll(biased, EXPERTS_PER_GROUP - d, axis=0),
                            pltpu.roll(biased, N_EXPERTS - d, axis=0))
        rank = rank + jnp.where(wraps, jnp.where(partner >= biased, 1.0, 0.0), jnp.where(partner > biased, 1.0, 0.0))
    selected = rank < 1.5
    contrib = jnp.where(selected, biased, 0.0)
    group_score = []
    for gi in range(N_GROUPS):
        r = [contrib[gi * EXPERTS_PER_GROUP + i:gi * EXPERTS_PER_GROUP + i + 1, :] for i in range(EXPERTS_PER_GROUP)]
        group_score.append(((r[0] + r[1]) + r[2]) + r[3])
    best = group_score[0]
    best_group = jnp.zeros_like(best)
    for gi in range(1, N_GROUPS):
        better = group_score[gi] > best
        best_group = jnp.where(better, float(gi), best_group)
        best = jnp.where(better, group_score[gi], best)
    in_group = (expert // EXPERTS_PER_GROUP).astype(F32) == best_group
    chosen = jnp.where(selected, jnp.where(in_group, 1.0, 0.0), 0.0)
    picked = chosen * scores
    gates = picked / jnp.sum(picked, axis=0, keepdims=True)
    lanes = 128
    n_blk = N_TOK // lanes
    li = lax.broadcasted_iota(jnp.int32, (lanes, lanes), 0)
    lj = lax.broadcasted_iota(jnp.int32, (lanes, lanes), 1)
    prefix = jnp.where(li <= lj, 1.0, 0.0).astype(BF16)
    stacked = jnp.concatenate([chosen[:, blk * lanes:(blk + 1) * lanes] for blk in range(n_blk)], axis=0)
    incl_all = jnp.dot(stacked.astype(BF16), prefix, preferred_element_type=F32)
    carry = jnp.zeros((N_EXPERTS, 1), F32)
    for blk in range(n_blk):
        cols = slice(blk * lanes, (blk + 1) * lanes)
        incl = incl_all[blk * N_EXPERTS:(blk + 1) * N_EXPERTS, :]
        rank_ref[:, cols] = incl - chosen[:, cols] + carry
        carry = carry + incl[:, lanes - 1:lanes]
    count = carry
    padded = jnp.floor((count + float(MOE_TILE - 1)) * (1.0 / MOE_TILE)) * float(MOE_TILE)
    erow = lax.broadcasted_iota(jnp.int32, (N_EXPERTS, 1), 0)
    offset = jnp.zeros((N_EXPERTS, 1), F32)
    for e in range(N_EXPERTS - 1):
        offset = offset + jnp.where(erow > e, padded[e:e + 1, :], 0.0)
    position = rank_ref[...] + offset
    ei = lax.broadcasted_iota(jnp.int32, (N_EXPERTS, N_EXPERTS), 0)
    ej = lax.broadcasted_iota(jnp.int32, (N_EXPERTS, N_EXPERTS), 1)
    lower = jnp.where(ej <= ei, 1.0, 0.0).astype(BF16)
    seen = jnp.dot(lower, chosen.astype(BF16), preferred_element_type=F32)
    first = (chosen > 0.5) & (seen < 1.5)
    second = (chosen > 0.5) & (seen > 1.5)
    pick = lambda flag, x: jnp.sum(jnp.where(flag, x, 0.0), axis=0, keepdims=True)
    pos_ref[0:1, :] = pick(first, position).astype(jnp.int32)
    pos_ref[1:2, :] = pick(second, position).astype(jnp.int32)
    w_rows = jnp.concatenate([pick(first, gates), pick(second, gates), jnp.zeros((6, N_TOK), F32)], axis=0)
    ti = lax.broadcasted_iota(jnp.int32, (8, lanes), 0)
    tj = lax.broadcasted_iota(jnp.int32, (8, lanes), 1)
    eye = jnp.where(ti == tj, 1.0, 0.0).astype(BF16)
    tn = (((0,), (0,)), ((), ()))
    hi = w_rows.astype(BF16)
    r1 = w_rows - hi.astype(F32)
    mid = r1.astype(BF16)
    lo = (r1 - mid.astype(F32)).astype(BF16)
    w_cols = lax.dot_general(hi, eye, tn, preferred_element_type=F32)
    w_cols = w_cols + lax.dot_general(mid, eye, tn, preferred_element_type=F32)
    w_cols = w_cols + lax.dot_general(lo, eye, tn, preferred_element_type=F32)
    w_ref[...] = w_cols[:, :TOP_K]
    start = (lax.broadcasted_iota(jnp.int32, (N_EXPERTS, lanes), 1) * MOE_TILE).astype(F32)
    end = offset + padded
    tile_expert = jnp.sum(jnp.where(end <= start, 1.0, 0.0), axis=0, keepdims=True)
    inside = (offset <= start) & (start < end)
    real = jnp.clip(count - (start - offset), 0.0, float(MOE_TILE))
    tile_rows = jnp.sum(jnp.where(inside, real, 0.0), axis=0, keepdims=True)
    plan_ref[0:1, :] = jnp.minimum(tile_expert, float(N_EXPERTS - 1)).astype(jnp.int32)
    plan_ref[1:2, :] = tile_rows.astype(jnp.int32)


def _router(logits_t, router_b):
    whole = lambda shape: pl.BlockSpec(shape, lambda i: (0, 0))
    return pl.pallas_call(
        _router_kernel,
        grid=(1,),
        in_specs=[whole((N_EXPERTS, N_TOK)), whole((N_EXPERTS, 1))],
        out_specs=(whole((2, N_TOK)), whole((N_TOK, TOP_K)), whole((2, 128))),
        out_shape=(jax.ShapeDtypeStruct((2, N_TOK), jnp.int32),
                   jax.ShapeDtypeStruct((N_TOK, TOP_K), F32),
                   jax.ShapeDtypeStruct((2, 128), jnp.int32)),
        scratch_shapes=[pltpu.VMEM((N_EXPERTS, N_TOK), F32)],
        compiler_params=_params("arbitrary"),
        name="router",
    )(logits_t, router_b.reshape(N_EXPERTS, 1))


def _sc_mesh():
    return plsc.VectorSubcoreMesh(core_axis_name="c", subcore_axis_name="s")


def _sc_worker_base():
    return (lax.axis_index("s") * SC_CORES + lax.axis_index("c")) * SC_TOKENS_PER_WORKER


def _moe_dispatch(h, pos_a, pos_b):
    n_chunks = SC_TOKENS_PER_WORKER // SC_CHUNK
    idx = pltpu.VMEM((SC_CHUNK,), jnp.int32)

    @functools.partial(
        pl.kernel, mesh=_sc_mesh(),
        out_type=jax.ShapeDtypeStruct((MOE_ROWS, ROW_WORDS), jnp.int32),
        scratch_types=[idx, idx, idx, idx, pltpu.VMEM((2, SC_CHUNK, ROW_WORDS), jnp.int32),
                       pltpu.SemaphoreType.DMA((6,)), pltpu.SemaphoreType.DMA((4,))],
        name="moe_dispatch",
    )
    def run(h_hbm, pa_hbm, pb_hbm, xs_hbm, ia0, ib0, ia1, ib1, rows_v, sem_in, sem_out):
        base = _sc_worker_base()
        ia, ib = (ia0, ia1), (ib0, ib1)

        def start_loads(c):
            slot = c % 2
            tok = pl.ds(pl.multiple_of(base + c * SC_CHUNK, 8), SC_CHUNK)
            return (pltpu.async_copy(pa_hbm.at[tok], ia[slot], sem_in.at[3 * slot]),
                    pltpu.async_copy(pb_hbm.at[tok], ib[slot], sem_in.at[3 * slot + 1]),
                    pltpu.async_copy(h_hbm.at[tok], rows_v.at[slot], sem_in.at[3 * slot + 2]))

        loads = start_loads(0)
        scatters = [(), ()]
        for c in range(n_chunks):
            slot = c % 2
            for cp in loads:
                cp.wait()
            if c + 1 < n_chunks:
                for cp in scatters[1 - slot]:
                    cp.wait()
                scatters[1 - slot] = ()
                loads = start_loads(c + 1)
            scatters[slot] = (pltpu.async_copy(rows_v.at[slot], xs_hbm.at[ia[slot]], sem_out.at[2 * slot]),
                              pltpu.async_copy(rows_v.at[slot], xs_hbm.at[ib[slot]], sem_out.at[2 * slot + 1]))
        for pending in scatters:
            for cp in pending:
                cp.wait()

    return run(h, pos_a, pos_b)


def _moe_collect(ys, pos_a, pos_b):
    n_chunks = SC_TOKENS_PER_WORKER // SC_CHUNK
    out = jax.ShapeDtypeStruct((N_TOK, ROW_WORDS), jnp.int32)
    idx = pltpu.VMEM((SC_TOKENS_PER_WORKER,), jnp.int32)
    rows = pltpu.VMEM((2, SC_CHUNK, ROW_WORDS), jnp.int32)

    @functools.partial(
        pl.kernel, mesh=_sc_mesh(), out_type=(out, out),
        scratch_types=[idx, idx, rows, rows, pltpu.SemaphoreType.DMA((4,)), pltpu.SemaphoreType.DMA((4,))],
        name="moe_collect",
    )
    def run(ys_hbm, pa_hbm, pb_hbm, ya_hbm, yb_hbm, ia_v, ib_v, ra_v, rb_v, sem_g, sem_w):
        base = _sc_worker_base()
        mine = pl.ds(pl.multiple_of(base, 8), SC_TOKENS_PER_WORKER)
        pltpu.sync_copy(pa_hbm.at[mine], ia_v)
        pltpu.sync_copy(pb_hbm.at[mine], ib_v)
        writes = [(), ()]
        for c in range(n_chunks):
            slot = c % 2
            for cp in writes[slot]:
                cp.wait()
            part = pl.ds(c * SC_CHUNK, SC_CHUNK)
            tok = pl.ds(pl.multiple_of(base + c * SC_CHUNK, 8), SC_CHUNK)
            ga = pltpu.async_copy(ys_hbm.at[ia_v.at[part]], ra_v.at[slot], sem_g.at[slot])
            gb = pltpu.async_copy(ys_hbm.at[ib_v.at[part]], rb_v.at[slot], sem_g.at[2 + slot])
            ga.wait()
            wa = pltpu.async_copy(ra_v.at[slot], ya_hbm.at[tok], sem_w.at[slot])
            gb.wait()
            wb = pltpu.async_copy(rb_v.at[slot], yb_hbm.at[tok], sem_w.at[2 + slot])
            writes[slot] = (wa, wb)
        for pending in writes:
            for cp in pending:
                cp.wait()

    return run(ys, pos_a, pos_b)


def _experts_kernel(plan_ref, xs_ref, wg_hbm, wu_hbm, wd_hbm, y_ref,
                    sg_ref, su_ref, sd_ref, wgb_ref, wub_ref, wdb_ref, hid_ref, sems, seg_ref, *, layer):
    j = pl.program_id(0)
    n_tiles = pl.num_programs(0)
    expert = plan_ref[j]
    n_real = plan_ref[PLAN_LANES + j]
    fresh = jnp.logical_or(j == 0, expert != plan_ref[jnp.maximum(j - 1, 0)])

    def weight_copies(e, slot):
        return (pltpu.make_async_copy(wg_hbm.at[layer, e], sg_ref.at[slot], sems.at[slot, 0]),
                pltpu.make_async_copy(wu_hbm.at[layer, e], su_ref.at[slot], sems.at[slot, 1]),
                pltpu.make_async_copy(wd_hbm.at[layer, e], sd_ref.at[slot], sems.at[slot, 2]))

    @pl.when(j == 0)
    def _():
        seg_ref[0] = 0

        @pl.when(n_real > 0)
        def _():
            for cp in weight_copies(expert, 0):
                cp.start()

    @pl.when(jnp.logical_and(n_real > 0, fresh))
    def _():
        slot = seg_ref[0] % 2
        for cp in weight_copies(expert, slot):
            cp.wait()
        wgb_ref[...] = sg_ref[slot].astype(BF16)
        wub_ref[...] = su_ref[slot].astype(BF16)
        wdb_ref[...] = sd_ref[slot].astype(BF16)
        nxt = lax.while_loop(lambda t: jnp.logical_and(t < n_tiles, plan_ref[jnp.minimum(t, n_tiles - 1)] == expert),
                             lambda t: t + 1, j + 1)
        nxt_c = jnp.minimum(nxt, n_tiles - 1)

        @pl.when(jnp.logical_and(nxt < n_tiles, plan_ref[PLAN_LANES + nxt_c] > 0))
        def _():
            for cp in weight_copies(plan_ref[nxt_c], 1 - slot):
                cp.start()

        seg_ref[0] = seg_ref[0] + 1

    @pl.when(n_real > 0)
    def _():
        n = EXPERT_SUB_ROWS
        n_sub = xs_ref.shape[0] // n
        row = lax.broadcasted_iota(jnp.int32, (n, xs_ref.shape[1]), 0)

        def up(r):
            rows = slice(r * n, (r + 1) * n)
            words = jnp.where(row < n_real - r * n, xs_ref[rows, :], 0)
            x = _unpack_rows(words).astype(BF16)
            a = jnp.dot(x, wgb_ref[...], preferred_element_type=F32)
            b = jnp.dot(x, wub_ref[...], preferred_element_type=F32)
            hid_ref[r] = ((a * jax.nn.sigmoid(a)) * b).astype(BF16)

        def down(r):
            rows = slice(r * n, (r + 1) * n)
            y_ref[rows, :] = _pack_rows(jnp.dot(hid_ref[r], wdb_ref[...], preferred_element_type=F32))

        up(0)
        for r in range(1, n_sub):
            up(r)
            down(r - 1)
        down(n_sub - 1)


def _experts(plan, xs, w_gate, w_up, w_down, layer):
    hbm = pl.BlockSpec(memory_space=pl.ANY)
    return pl.pallas_call(
        functools.partial(_experts_kernel, layer=layer),
        grid_spec=pltpu.PrefetchScalarGridSpec(
            num_scalar_prefetch=1,
            grid=(MOE_ROWS // MOE_TILE,),
            in_specs=[pl.BlockSpec((MOE_TILE, ROW_WORDS), lambda j, plan: (j, 0)), hbm, hbm, hbm],
            out_specs=pl.BlockSpec((MOE_TILE, ROW_WORDS), lambda j, plan: (j, 0)),
            scratch_shapes=[pltpu.VMEM((2, D_MODEL, D_EXPERT), F32), pltpu.VMEM((2, D_MODEL, D_EXPERT), F32),
                            pltpu.VMEM((2, D_EXPERT, D_MODEL), F32),
                            pltpu.VMEM((D_MODEL, D_EXPERT), BF16), pltpu.VMEM((D_MODEL, D_EXPERT), BF16),
                            pltpu.VMEM((D_EXPERT, D_MODEL), BF16),
                            pltpu.VMEM((MOE_TILE // EXPERT_SUB_ROWS, EXPERT_SUB_ROWS, D_EXPERT), BF16),
                            pltpu.SemaphoreType.DMA((2, 3)), pltpu.SMEM((1,), jnp.int32)],
        ),
        out_shape=jax.ShapeDtypeStruct((MOE_ROWS, ROW_WORDS), jnp.int32),
        compiler_params=_params("arbitrary"),
        name="experts",
    )(plan, xs, w_gate, w_up, w_down)


def _combine_kernel(x_ref, ya_ref, yb_ref, wt_ref, mod_ref, o_ref):
    o_ref[...] = _moe_mix(x_ref, ya_ref, yb_ref, wt_ref, mod_ref)


def _combine(x, moe_out, mod_l, tok0, n_tok, block_rows=1024):
    ya, yb, w_tok = moe_out
    b0 = tok0 // block_rows
    rows = lambda width: pl.BlockSpec((block_rows, width), lambda i: (b0 + i, 0))
    return pl.pallas_call(
        _combine_kernel,
        grid=(n_tok // block_rows,),
        in_specs=[rows(D_MODEL), rows(ROW_WORDS), rows(ROW_WORDS), rows(TOP_K),
                  pl.BlockSpec((None, 6, D_MODEL), lambda i: (_cond_of_token_block(b0 + i, block_rows), 0, 0))],
        out_specs=pl.BlockSpec((block_rows, D_MODEL), lambda i: (i, 0)),
        out_shape=jax.ShapeDtypeStruct((n_tok, D_MODEL), F32),
        compiler_params=_params("arbitrary"),
        name="combine",
    )(x, ya, yb, w_tok, mod_l)


def _moe(h, logits_t, router_b, w_gate, w_up, w_down, layer):
    pos, w, plan = _router(logits_t, router_b)
    xs = _moe_dispatch(h, pos[0], pos[1])
    ys = _experts(plan.reshape(-1), xs, w_gate, w_up, w_down, layer)
    ya, yb = _moe_collect(ys, pos[0], pos[1])
    return ya, yb, w


def _dft_tables(L):
    k = np.arange(L)[:, None]
    m = np.arange(L)[None, :]
    r = (k * m) % (2 * L)
    ang = np.pi * r.astype(np.float64) / L
    fc = np.cos(ang)
    fs = np.sin(ang)
    fs[0, :] = np.where(np.arange(L) % 2 == 0, 1.0, -1.0)
    wk = np.full((L, 1), 1.0 / L)
    wk[0, 0] = 0.5 / L
    gc = (fc * wk).T
    gs = (fs * wk).T
    return [jnp.asarray(t.astype(np.float32)).astype(BF16) for t in (fc, fs, gc, gs)]


def _filter_consts(L):
    t = np.linspace(0.0, 1.0, L, dtype=np.float32)[:, None]
    w = (np.float32(2.0 * np.pi) * np.arange(L, dtype=np.float32)[:, None] / np.float32(L)).astype(np.float32)
    fb = np.linspace(1e-4, HY_BANDS - 1, HY_BANDS, dtype=np.float32)[None, :]
    emb = np.concatenate([t, np.cos(fb * w), -np.sin(fb * w)], axis=-1).astype(np.float32)
    lo = math.log(HY_DECAY_TARGET) / HY_SLOW_PCT
    hi = math.log(HY_DECAY_TARGET) / HY_FAST_PCT
    deltas = np.abs(np.linspace(lo, hi, D_MODEL, dtype=np.float32))
    decay = np.exp(-t * deltas).astype(np.float32)
    return jnp.asarray(emb), jnp.asarray(decay)


def _filter_kernel(emb_ref, w1_ref, b1_ref, w2_ref, b2_ref, fr_ref, w3f_ref, w3b_ref, dec_ref,
                   fc_ref, fs_ref, kr_ref, q_ref, krn_ref, hd_ref):
    @pl.when(pl.program_id(0) == 0)
    def _():
        fr = fr_ref[...]
        h1 = jnp.sin(fr * (jnp.dot(emb_ref[...], w1_ref[...], precision=HIGHEST,
                                   preferred_element_type=F32) + b1_ref[...]))
        hd_ref[...] = jnp.sin(fr * (jnp.dot(h1, w2_ref[...], precision=HIGHEST,
                                            preferred_element_type=F32) + b2_ref[...]))

    hd = hd_ref[...]
    dec = dec_ref[...]
    f = jnp.dot(hd, w3f_ref[...], precision=HIGHEST, preferred_element_type=F32) * dec
    g = jnp.dot(hd, w3b_ref[...], precision=HIGHEST, preferred_element_type=F32) * dec
    row = lax.broadcasted_iota(jnp.int32, f.shape, 0)
    g = jnp.where(row == 0, 0.0, g)
    s = f + g
    d = f - g
    kr = jnp.dot(fc_ref[...], s.astype(BF16), preferred_element_type=F32)
    qq = jnp.dot(fs_ref[...], d.astype(BF16), preferred_element_type=F32)
    alt = jnp.where(row % 2 == 0, 1.0, -1.0)
    nyq = jnp.sum(alt * s, axis=0, keepdims=True)
    kr_ref[...] = kr
    q_ref[...] = jnp.where(row == 0, 0.0, qq)
    krn_ref[...] = jnp.where(row == 0, nyq, kr)


def _hyena_filter_spectrum(L, w1, b1, w2, b2, w3, freq, fc, fs, cblk=256):
    emb, decay = _filter_consts(L)
    ncb = D_MODEL // cblk
    n_emb = 128
    emb = jnp.pad(emb, ((0, 0), (0, n_emb - emb.shape[1])))
    w1 = jnp.pad(w1, ((0, n_emb - w1.shape[0]), (0, 0)))
    full = lambda shape: pl.BlockSpec(shape, lambda j: tuple(0 for _ in shape))
    out_sds = jax.ShapeDtypeStruct((L, D_MODEL), F32)
    out_spec = pl.BlockSpec((L, cblk), lambda j: (0, j))
    return pl.pallas_call(
        _filter_kernel,
        grid=(ncb,),
        in_specs=[
            full((L, n_emb)), full((n_emb, HY_FFN)), full((1, HY_FFN)), full((HY_FFN, HY_FFN)),
            full((1, HY_FFN)), full((1, HY_FFN)),
            pl.BlockSpec((HY_FFN, cblk), lambda j: (0, j)),
            pl.BlockSpec((HY_FFN, cblk), lambda j: (0, ncb + j)),
            pl.BlockSpec((L, cblk), lambda j: (0, j)),
            full((L, L)), full((L, L)),
        ],
        out_specs=(out_spec, out_spec, out_spec),
        out_shape=(out_sds, out_sds, out_sds),
        scratch_shapes=[pltpu.VMEM((L, HY_FFN), F32)],
        compiler_params=_params("arbitrary"),
        name=f"hyena_filter_{L}",
    )(emb, w1, b1.reshape(1, HY_FFN), w2, b2.reshape(1, HY_FFN), freq.reshape(1, HY_FFN), w3, w3, decay, fc, fs)


def _hyena_conv_kernel(x0_ref, x1_ref, v_ref, cw0_ref, cw1_ref, cwv_ref, cb0_ref, cb1_ref, cbv_ref,
                       kr_ref, q_ref, krn_ref, ds_ref, fc_ref, fs_ref, gc_ref, gs_ref, o_ref,
                       zz_ref, gate_ref, skip_ref, yr_ref, yw_ref):
    L = fc_ref.shape[0]
    unit_w = zz_ref.shape[2]
    units = [(slice(s * L, (s + 1) * L), slice(c * unit_w, (c + 1) * unit_w))
             for s in range(x0_ref.shape[0] // L) for c in range(x0_ref.shape[1] // unit_w)]
    row = lax.broadcasted_iota(jnp.int32, (L, unit_w), 0)

    def gating(i):
        rows, cols = units[i]

        def short_conv(u_ref, w_ref, b_ref):
            u = u_ref[rows, cols].astype(F32)
            w = w_ref[:, cols]
            prev = jnp.where(row == 0, 0.0, pltpu.roll(u, 1, axis=0))
            nxt = jnp.where(row == L - 1, 0.0, pltpu.roll(u, L - 1, axis=0))
            return prev * w[0:1, :] + u * w[1:2, :] + nxt * w[2:3, :] + b_ref[:, cols]

        x0 = short_conv(x0_ref, cw0_ref, cb0_ref)
        zz = short_conv(v_ref, cwv_ref, cbv_ref) * short_conv(x1_ref, cw1_ref, cb1_ref)
        zz_ref[i] = zz.astype(BF16)
        gate_ref[i] = x0
        skip_ref[i] = x0 * zz * ds_ref[:, cols]

    def spectrum(i):
        cols = units[i][1]
        ur = jnp.dot(fc_ref[...], zz_ref[i], preferred_element_type=F32)
        p = jnp.dot(fs_ref[...], zz_ref[i], preferred_element_type=F32)
        qq = q_ref[:, cols]
        yr_ref[i] = (ur * kr_ref[:, cols] - p * qq).astype(BF16)
        yw_ref[i] = (ur * qq + p * krn_ref[:, cols]).astype(BF16)

    def synthesis(i):
        rows, cols = units[i]
        y = jnp.dot(gc_ref[...], yr_ref[i], preferred_element_type=F32)
        y = y + jnp.dot(gs_ref[...], yw_ref[i], preferred_element_type=F32)
        o_ref[rows, cols] = (gate_ref[i] * y + skip_ref[i]).astype(o_ref.dtype)

    for t in range(len(units) + 2):
        if t < len(units):
            gating(t)
        if 0 <= t - 1 < len(units):
            spectrum(t - 1)
        if 0 <= t - 2 < len(units):
            synthesis(t - 2)


def _hyena_conv(u, conv_w, conv_b, dskip, spectrum, tables, *, latent):
    L = LATENT_LEN if latent else PROMPT_LEN
    n_seq = N_LATENT_SEQ if latent else N_PROMPT_SEQ
    cblk = 512
    unit_w = 256 if latent else 512
    ncb = D_MODEL // cblk
    seqs = 1 if latent else 8
    unit = (seqs * cblk // unit_w, L, unit_w)
    row0 = (N_PROMPT_TOK // L) if latent else 0
    kr, qq, krn = spectrum
    fc, fs, gc, gs = tables

    def part(p, rows):
        if rows != L:
            return pl.BlockSpec((rows, cblk), lambda j, s: (0, p * ncb + j))
        return pl.BlockSpec((seqs * L, cblk), lambda j, s: (row0 // seqs + s, p * ncb + j))

    def const_cols(rows):
        return pl.BlockSpec((rows, cblk), lambda j, s: (0, j))

    mat = pl.BlockSpec((L, L), lambda j, s: (0, 0))
    conv_b2 = conv_b.reshape(1, 3 * D_MODEL)
    in_specs = [part(0, L), part(1, L), part(2, L),
                part(0, 3), part(1, 3), part(2, 3),
                part(0, 1), part(1, 1), part(2, 1),
                const_cols(L), const_cols(L), const_cols(L), const_cols(1),
                mat, mat, mat, mat]
    args = [u, u, u, conv_w, conv_w, conv_w, conv_b2, conv_b2, conv_b2,
            kr, qq, krn, dskip.reshape(1, D_MODEL), fc, fs, gc, gs]
    return pl.pallas_call(
        _hyena_conv_kernel,
        grid=(ncb, n_seq // seqs),
        in_specs=in_specs,
        out_specs=pl.BlockSpec((seqs * L, cblk), lambda j, s: (s, j)),
        out_shape=jax.ShapeDtypeStruct((n_seq * L, D_MODEL), BF16),
        scratch_shapes=[pltpu.VMEM(unit, BF16), pltpu.VMEM(unit, F32), pltpu.VMEM(unit, F32),
                        pltpu.VMEM(unit, BF16), pltpu.VMEM(unit, BF16)],
        compiler_params=_params("arbitrary", "arbitrary"),
        name="hyena_conv_latent" if latent else "hyena_conv_prompt",
    )(*args)


def kernel(x_prompt, x_sample, cache_k, cache_v, state_hgrn, c, c_ctx, norm_g, mod_w, mod_b, ab_in_w, hgrn_lb, hgrn_onorm_g, attn_qnorm_g, attn_knorm_g, ab_out_w, hy_in_w, hy_in_b, hy_conv_w, hy_conv_b, hy_f_w1, hy_f_b1, hy_f_w2, hy_f_b2, hy_f_w3, hy_f_freq, hy_dskip, hy_out_w, router_w, router_b, moe_w_gate, moe_w_up, moe_w_down):
    xp = x_prompt.reshape(N_PROMPT_TOK, D_MODEL)
    xl = x_sample.reshape(N_LATENT_TOK, D_MODEL)
    cond = jnp.concatenate([c_ctx[None, :], c, jnp.zeros((N_COND - 1 - N_LATENT_SEQ, D_MODEL), F32)], axis=0)
    mod = _modulation(cond, mod_w, mod_b)
    router_wp = jnp.pad(router_w, ((0, 0), (0, ROUTER_LANES - N_EXPERTS)))

    z = _in_proj0(xp, xl, norm_g[0, 0], mod[0], ab_in_w[0])
    oa_p, new_state = _hgrn(z, hgrn_lb, hgrn_onorm_g[0], None, latent=False)
    oa_l = _hgrn(z, hgrn_lb, hgrn_onorm_g[0], state_hgrn, latent=True)
    ob_p, k_fm, v_fm = _attention_prompt(z, attn_qnorm_g[0], attn_knorm_g[0])
    fm_shape = (N_PROMPT_SEQ, 1, KV_HEADS, HEAD_DIM, PROMPT_LEN)
    new_k = jnp.swapaxes(k_fm.reshape(fm_shape), -1, -2)
    new_v = jnp.swapaxes(v_fm.reshape(fm_shape), -1, -2)
    ob_l = _attention_latent(z, attn_qnorm_g[0], attn_knorm_g[0], cache_k, cache_v)
    x, h, logits_t = _out_proj([(oa_p, oa_l), (ob_p, ob_l)], ab_out_w[0], (xp, xl), norm_g[0, 1], mod[0],
                               router_wp)
    moe_out = _moe(h, logits_t, router_b, moe_w_gate, moe_w_up, moe_w_down, 0)

    x, u = _in_proj1(x, moe_out, mod[0], norm_g[1, 0], mod[1], hy_in_w[0], hy_in_b[0])
    pre = []
    for latent in (False, True):
        L = LATENT_LEN if latent else PROMPT_LEN
        tables = _dft_tables(L)
        spectrum = _hyena_filter_spectrum(L, hy_f_w1[0], hy_f_b1[0], hy_f_w2[0], hy_f_b2[0], hy_f_w3[0],
                                          hy_f_freq[0], tables[0], tables[1])
        pre.append(_hyena_conv(u, hy_conv_w[0], hy_conv_b[0], hy_dskip[0], spectrum, tables, latent=latent))
    x, h, logits_t = _out_proj([tuple(pre)], hy_out_w[0], (x,), norm_g[1, 1], mod[1], router_wp)
    moe_out = _moe(h, logits_t, router_b, moe_w_gate, moe_w_up, moe_w_down, 1)

    y_prompt = _combine(x, moe_out, mod[1], 0, N_PROMPT_TOK).reshape(N_PROMPT_SEQ, PROMPT_LEN, D_MODEL)
    y_sample = _combine(x, moe_out, mod[1], N_PROMPT_TOK, N_LATENT_TOK).reshape(N_LATENT_SEQ, LATENT_LEN, D_MODEL)
    return (y_prompt, y_sample, new_k, new_v, new_state)
```

```python
import functools
import math

import numpy as np
import jax
import jax.numpy as jnp
from jax import lax
from jax.experimental import pallas as pl
from jax.experimental.pallas import tpu as pltpu
from jax.experimental.pallas import tpu_sc as plsc

F32 = jnp.float32
BF16 = jnp.bfloat16
HIGHEST = lax.Precision.HIGHEST

D_MODEL = 1024
N_PROMPT_SEQ = 32
PROMPT_LEN = 256
N_LATENT_SEQ = 2
LATENT_LEN = 1024
PAST_LEN = 512
GRID_W = 64
N_PROMPT_TOK = N_PROMPT_SEQ * PROMPT_LEN
N_LATENT_TOK = N_LATENT_SEQ * LATENT_LEN
N_TOK = N_PROMPT_TOK + N_LATENT_TOK
N_COND = 8
EPS = 1e-6

A_WIDTH = 512
A_HEADS = 4
A_DK = 128
CHUNK = 64
HGRN_BLOCK = 128
HGRN_HEADS_PER_STEP = 4
HEAD_DIM = 64
Q_HEADS = 8
KV_HEADS = 2
Q_PER_KV = Q_HEADS // KV_HEADS
Q_BLOCK = 256
ROPE_THETA = 10000.0
ROPE_PAIRS = HEAD_DIM // 4
AB_IN = 5 * A_WIDTH + (Q_HEADS + 2 * KV_HEADS) * HEAD_DIM

HY_BANDS = 16
HY_FFN = 64
HY_DECAY_TARGET = 1e-2
HY_FAST_PCT = 0.3
HY_SLOW_PCT = 1.5

N_EXPERTS = 16
N_GROUPS = 4
EXPERTS_PER_GROUP = 4
TOP_K = 2
D_EXPERT = 512
ROUTER_LANES = 128
OUT_PROJ_SUB_ROWS = 256
EXPERT_SUB_ROWS = 256
EXPERT_TILES_PER_STEP = 2
IN_PROJ_SUB_ROWS = 256
MOE_TILE = 512
MOE_ROWS = N_TOK * TOP_K + N_EXPERTS * MOE_TILE
PLAN_LANES = 128

SC_CORES = 2
SC_WORKERS = 32
SC_TOKENS_PER_WORKER = N_TOK // SC_WORKERS
SC_CHUNK = 40
ROW_WORDS = D_MODEL // 2

VMEM_LIMIT = 56 * 1024 * 1024


def _params(*sem):
    return pltpu.CompilerParams(dimension_semantics=sem, vmem_limit_bytes=VMEM_LIMIT)


def _pack_rows(x):
    n = x.shape[1] // 2
    bits = pltpu.bitcast(x.astype(BF16).astype(F32), jnp.uint32)
    return pltpu.bitcast(bits[:, :n] | (bits[:, n:] >> 16), jnp.int32)


def _unpack_rows(p):
    bits = pltpu.bitcast(p, jnp.uint32)
    hi = pltpu.bitcast(bits & jnp.uint32(0xFFFF0000), F32)
    lo = pltpu.bitcast(bits << 16, F32)
    return jnp.concatenate([hi, lo], axis=1)


def _cond_of_token_block(i, block_rows):
    start = i * block_rows
    return jnp.where(start < N_PROMPT_TOK, 0, 1 + (start - N_PROMPT_TOK) // LATENT_LEN)


def _mod_kernel(cond_ref, w_ref, b_ref, o_ref):
    cnd = cond_ref[...]
    s = cnd * jax.nn.sigmoid(cnd)
    s_hi = s.astype(BF16)
    s_lo = (s - s_hi.astype(F32)).astype(BF16)
    w = w_ref[...]
    w_hi = w.astype(BF16)
    w_lo = (w - w_hi.astype(F32)).astype(BF16)
    acc = jnp.dot(s_hi, w_hi, preferred_element_type=F32)
    acc = acc + jnp.dot(s_lo, w_hi, preferred_element_type=F32)
    acc = acc + jnp.dot(s_hi, w_lo, preferred_element_type=F32)
    o_ref[...] = acc + b_ref[...]


def _modulation(cond, mod_w, mod_b):
    depth = mod_w.shape[0]
    n_mod = 6
    cols = 2 * D_MODEL
    n_step = n_mod * D_MODEL // cols
    out = pl.pallas_call(
        _mod_kernel,
        grid=(depth, n_step),
        in_specs=[
            pl.BlockSpec((N_COND, D_MODEL), lambda l, j: (0, 0)),
            pl.BlockSpec((None, D_MODEL, cols), lambda l, j: (l, 0, j)),
            pl.BlockSpec((None, 1, cols), lambda l, j: (l, 0, j)),
        ],
        out_specs=pl.BlockSpec((None, N_COND, cols), lambda l, j: (l, 0, j)),
        out_shape=jax.ShapeDtypeStruct((depth, N_COND, n_mod * D_MODEL), F32),
        compiler_params=_params("arbitrary", "arbitrary"),
        name="modulation",
    )(cond, mod_w, mod_b.reshape(depth, 1, n_mod * D_MODEL))
    return out.reshape(depth, N_COND, n_mod, D_MODEL)


def _modulated_norm(x, g, mod, shift_row, scale_row):
    ms = jnp.mean(x * x, axis=-1, keepdims=True)
    y = x * lax.rsqrt(ms + EPS) * g
    return y * (1.0 + mod[scale_row:scale_row + 1, :]) + mod[shift_row:shift_row + 1, :]


def _trunk_specs(block_rows, width):
    n_prompt_blocks = N_PROMPT_TOK // block_rows
    return (pl.BlockSpec((block_rows, width), lambda i: (jnp.minimum(i, n_prompt_blocks - 1), 0)),
            pl.BlockSpec((block_rows, width), lambda i: (jnp.maximum(i - n_prompt_blocks, 0), 0)))


def _select_trunk(p_ref, l_ref, rows=slice(None)):
    block_rows = p_ref.shape[0]
    return jnp.where(pl.program_id(0) < N_PROMPT_TOK // block_rows, p_ref[rows, :], l_ref[rows, :])


def _cast_once(w_ref, wb_ref):
    @pl.when(pl.program_id(0) == 0)
    def _():
        wb_ref[...] = w_ref[...].astype(BF16)


def _resident(shape):
    return pl.BlockSpec(shape, lambda i: tuple(0 for _ in shape), pipeline_mode=pl.Buffered(1))


def _mod_spec(block_rows):
    return pl.BlockSpec((None, 6, D_MODEL), lambda i: (_cond_of_token_block(i, block_rows), 0, 0))


def _in_proj0_kernel(xp_ref, xl_ref, g_ref, mod_ref, w_ref, o_ref, wb_ref, hb_ref):
    _cast_once(w_ref, wb_ref)
    n = IN_PROJ_SUB_ROWS
    n_sub = xp_ref.shape[0] // n

    def prepare(r):
        x = _select_trunk(xp_ref, xl_ref, slice(r * n, (r + 1) * n))
        hb_ref[r] = _modulated_norm(x, g_ref[...], mod_ref[...], 0, 1).astype(BF16)

    def project(r):
        u = jnp.dot(hb_ref[r], wb_ref[...], preferred_element_type=F32)
        o_ref[r * n:(r + 1) * n, :] = u.astype(o_ref.dtype)

    prepare(0)
    for r in range(1, n_sub):
        prepare(r)
        project(r - 1)
    project(n_sub - 1)


def _in_proj0(x_prompt, x_latent, g, mod_l, w, block_rows=512):
    n = w.shape[1]
    return pl.pallas_call(
        _in_proj0_kernel,
        grid=(N_TOK // block_rows,),
        in_specs=[*_trunk_specs(block_rows, D_MODEL), _resident((1, D_MODEL)), _mod_spec(block_rows),
                  _resident((D_MODEL, n))],
        out_specs=pl.BlockSpec((block_rows, n), lambda i: (i, 0)),
        out_shape=jax.ShapeDtypeStruct((N_TOK, n), BF16),
        scratch_shapes=[pltpu.VMEM((D_MODEL, n), BF16),
                        pltpu.VMEM((block_rows // IN_PROJ_SUB_ROWS, IN_PROJ_SUB_ROWS, D_MODEL), BF16)],
        compiler_params=_params("arbitrary"),
        name="in_proj0",
    )(x_prompt, x_latent, g.reshape(1, D_MODEL), mod_l, w)


def _moe_mix(x_ref, ya_ref, yb_ref, wt_ref, mod_ref, rows=slice(None)):
    wt = wt_ref[rows, :]
    mix = wt[:, 0:1] * _unpack_rows(ya_ref[rows, :]) + wt[:, 1:2] * _unpack_rows(yb_ref[rows, :])
    return x_ref[rows, :] + mod_ref[5:6, :] * mix


def _in_proj1_kernel(x_ref, ya_ref, yb_ref, wt_ref, modp_ref, g_ref, mod_ref, w_ref, b_ref, xo_ref, o_ref,
                     wb_ref, hb_ref):
    _cast_once(w_ref, wb_ref)
    n = IN_PROJ_SUB_ROWS
    n_sub = x_ref.shape[0] // n

    def prepare(r):
        rows = slice(r * n, (r + 1) * n)
        x = _moe_mix(x_ref, ya_ref, yb_ref, wt_ref, modp_ref, rows)
        xo_ref[rows, :] = x
        hb_ref[r] = _modulated_norm(x, g_ref[...], mod_ref[...], 0, 1).astype(BF16)

    def project(r):
        rows = slice(r * n, (r + 1) * n)
        u = jnp.dot(hb_ref[r], wb_ref[...], preferred_element_type=F32) + b_ref[...]
        o_ref[rows, :] = u.astype(o_ref.dtype)

    prepare(0)
    for r in range(1, n_sub):
        prepare(r)
        project(r - 1)
    project(n_sub - 1)


def _in_proj1(x, moe_out, mod_prev, g, mod_l, w, bias, block_rows=512):
    ya, yb, w_tok = moe_out
    n = w.shape[1]
    tok = pl.BlockSpec((block_rows, D_MODEL), lambda i: (i, 0))
    packed = pl.BlockSpec((block_rows, ROW_WORDS), lambda i: (i, 0))
    return pl.pallas_call(
        _in_proj1_kernel,
        grid=(N_TOK // block_rows,),
        in_specs=[tok, packed, packed, pl.BlockSpec((block_rows, TOP_K), lambda i: (i, 0)), _mod_spec(block_rows),
                  _resident((1, D_MODEL)), _mod_spec(block_rows), _resident((D_MODEL, n)), _resident((1, n))],
        out_specs=(tok, pl.BlockSpec((block_rows, n), lambda i: (i, 0))),
        out_shape=(jax.ShapeDtypeStruct((N_TOK, D_MODEL), F32), jax.ShapeDtypeStruct((N_TOK, n), BF16)),
        scratch_shapes=[pltpu.VMEM((D_MODEL, n), BF16),
                        pltpu.VMEM((block_rows // IN_PROJ_SUB_ROWS, IN_PROJ_SUB_ROWS, D_MODEL), BF16)],
        compiler_params=_params("arbitrary"),
        name="in_proj1",
    )(x, ya, yb, w_tok, mod_prev, g.reshape(1, D_MODEL), mod_l, w, bias.reshape(1, n))


def _hgrn_kernel(*refs, seq_len, with_state):
    if with_state:
        (q_ref, zf_ref, zb_ref, i_ref, ga_ref, lb_ref, og_ref, s0_ref, o_ref, of_ref, ob_ref) = refs
    else:
        (q_ref, zf_ref, zb_ref, i_ref, ga_ref, lb_ref, og_ref, o_ref, s_ref, of_ref, ob_ref) = refs
    n_blocks = seq_len // HGRN_BLOCK
    chunks_per_block = HGRN_BLOCK // CHUNK

    lbr = lb_ref[...]
    mx = jnp.maximum(lbr[0], lbr[1])
    e0 = jnp.exp(lbr[0] - mx)
    e1 = jnp.exp(lbr[1] - mx)
    lb = e0 / (e0 + e1)

    row = lax.broadcasted_iota(jnp.int32, (HGRN_BLOCK, HGRN_BLOCK), 0)
    col = lax.broadcasted_iota(jnp.int32, (HGRN_BLOCK, HGRN_BLOCK), 1)
    same_chunk = (row // CHUNK) == (col // CHUNK)
    nt = (((1,), (1,)), ((), ()))
    tn = (((0,), (0,)), ((), ()))

    def per_chunk_row(x, idx):
        return jnp.concatenate(
            [jnp.broadcast_to(x[n * CHUNK + idx:n * CHUNK + idx + 1, :], (CHUNK, x.shape[1]))
             for n in range(chunks_per_block)], axis=0)

    def in_chunk_cumsum(tri, x):
        hi = x.astype(BF16)
        lo = (x - hi.astype(F32)).astype(BF16)
        return jnp.dot(tri, hi, preferred_element_type=F32) + jnp.dot(tri, lo, preferred_element_type=F32)

    def prepare(blk, cols, z_ref, lbd, forward):
        rows = slice(blk * HGRN_BLOCK, (blk + 1) * HGRN_BLOCK)
        keep = (same_chunk & (col <= row)) if forward else (same_chunk & (col >= row))
        tri = jnp.where(keep, 1.0, 0.0).astype(BF16)
        mid = CHUNK // 2 if forward else CHUNK - 1 - CHUNK // 2
        last = CHUNK - 1 if forward else 0
        f = lbd + (1.0 - lbd) * jax.nn.sigmoid(z_ref[rows, cols].astype(F32))
        lf = jnp.log(f)
        k = 1.0 - f
        q = q_ref[rows, cols].astype(F32)
        b = in_chunk_cumsum(tri, lf)
        bm = per_chunk_row(b, mid)
        bl = per_chunk_row(b, last)
        return dict(
            rows=rows, cols=cols, keep=keep, forward=forward,
            vb=i_ref[rows, cols].astype(BF16),
            qe=(q * jnp.exp(b - bm)).astype(BF16), ke=(k * jnp.exp(bm - b)).astype(BF16),
            qb=(q * jnp.exp(b)).astype(BF16), ks=(k * jnp.exp(bl - b)).astype(BF16), decay=jnp.exp(bl))

    def within_chunks(u):
        att = lax.dot_general(u["qe"], u["ke"], nt, preferred_element_type=F32)
        att = jnp.where(u["keep"], att, 0.0)
        u["o_intra"] = jnp.dot(att.astype(BF16), u["vb"], preferred_element_type=F32)
        u["upd"] = [lax.dot_general(u["vb"][n * CHUNK:(n + 1) * CHUNK], u["ks"][n * CHUNK:(n + 1) * CHUNK], tn,
                                    preferred_element_type=F32) for n in range(chunks_per_block)]

    def across_chunks(u, st, out_ref):
        order = range(chunks_per_block) if u["forward"] else range(chunks_per_block - 1, -1, -1)
        o_inter = [None] * chunks_per_block
        for n in order:
            cr = slice(n * CHUNK, (n + 1) * CHUNK)
            o_inter[n] = lax.dot_general(u["qb"][cr], st.astype(BF16), nt, preferred_element_type=F32)
            st = st * u["decay"][n * CHUNK:n * CHUNK + 1, :] + u["upd"][n]
        out_ref[u["rows"], u["cols"]] = u["o_intra"] + jnp.concatenate(o_inter, axis=0)
        return st

    n_heads = q_ref.shape[1] // A_DK
    head_cols = [slice(hd * A_DK, (hd + 1) * A_DK) for hd in range(n_heads)]
    if with_state:
        states = {(hd, d): s0_ref[d, hd].T for hd in range(n_heads) for d in range(2)}
    else:
        states = {(hd, d): jnp.zeros((A_DK, A_DK), F32) for hd in range(n_heads) for d in range(2)}
    for step in range(n_blocks):
        units = {}
        for hd, cols in enumerate(head_cols):
            units[hd, 0] = prepare(step, cols, zf_ref, lb[0:1, cols], True)
            units[hd, 1] = prepare(n_blocks - 1 - step, cols, zb_ref, lb[1:2, cols], False)
        for u in units.values():
            within_chunks(u)
        for key, u in units.items():
            states[key] = across_chunks(u, states[key], of_ref if key[1] == 0 else ob_ref)
    for hd, cols in enumerate(head_cols):
        if not with_state:
            s_ref[0, hd] = states[hd, 0].T
            s_ref[1, hd] = states[hd, 1].T
        o = of_ref[:, cols] + ob_ref[:, cols]
        o = o * lax.rsqrt(jnp.mean(o * o, axis=-1, keepdims=True) + EPS) * og_ref[:, cols]
        ga = ga_ref[:, cols].astype(F32)
        o_ref[:, cols] = (o * (ga * jax.nn.sigmoid(ga))).astype(o_ref.dtype)


def _hgrn(z, hgrn_lb, onorm_g, state, *, latent):
    seq_len = LATENT_LEN if latent else PROMPT_LEN
    n_seq = N_LATENT_SEQ if latent else N_PROMPT_SEQ
    row0 = (N_PROMPT_TOK // seq_len) if latent else 0

    hw = HGRN_HEADS_PER_STEP * A_DK
    n_hg = A_HEADS // HGRN_HEADS_PER_STEP

    def zspec(part):
        return pl.BlockSpec((seq_len, hw), lambda s, h: (row0 + s, part * n_hg + h))

    in_specs = [zspec(0), zspec(1), zspec(2), zspec(3), zspec(4),
                pl.BlockSpec((2, 2, hw), lambda s, h: (0, 0, h)),
                pl.BlockSpec((1, hw), lambda s, h: (0, h))]
    args = [z, z, z, z, z, hgrn_lb, onorm_g.reshape(1, A_WIDTH)]
    state_spec = pl.BlockSpec((None, None, 2, HGRN_HEADS_PER_STEP, A_DK, A_DK), lambda s, h: (s, 0, 0, h, 0, 0))
    o_shape = jax.ShapeDtypeStruct((n_seq * seq_len, A_WIDTH), BF16)
    o_spec = pl.BlockSpec((seq_len, hw), lambda s, h: (s, h))
    if latent:
        in_specs.append(state_spec)
        args.append(state)
        out_shape, out_specs = o_shape, o_spec
    else:
        out_shape = (o_shape, jax.ShapeDtypeStruct((n_seq, 1, 2, A_HEADS, A_DK, A_DK), F32))
        out_specs = (o_spec, state_spec)
    return pl.pallas_call(
        functools.partial(_hgrn_kernel, seq_len=seq_len, with_state=latent),
        grid=(n_seq, n_hg),
        in_specs=in_specs,
        out_specs=out_specs,
        out_shape=out_shape,
        scratch_shapes=[pltpu.VMEM((seq_len, hw), F32), pltpu.VMEM((seq_len, hw), F32)],
        compiler_params=_params("arbitrary", "arbitrary"),
        name="hgrn_latent" if latent else "hgrn_prompt",
    )(*args)


def _rope_tables():
    pos = np.arange(LATENT_LEN)
    row, colp = pos // GRID_W, pos % GRID_W
    inv = ROPE_THETA ** (-np.arange(ROPE_PAIRS, dtype=np.float32) / ROPE_PAIRS)
    inv = inv.astype(np.float32)
    ang_r = (row.astype(np.float32)[:, None] * inv).astype(np.float32)
    ang_c = (colp.astype(np.float32)[:, None] * inv).astype(np.float32)
    cos = np.concatenate([np.cos(ang_r), np.cos(ang_r), np.cos(ang_c), np.cos(ang_c)], axis=1)
    sin = np.concatenate([-np.sin(ang_r), np.sin(ang_r), -np.sin(ang_c), np.sin(ang_c)], axis=1)
    return cos.astype(np.float32), sin.astype(np.float32)


def _head_mean_matrix(width):
    idx = np.arange(width) // HEAD_DIM
    return jnp.asarray((idx[:, None] == idx[None, :]).astype(np.float32) / HEAD_DIM).astype(BF16)


def _attn_kernel(*refs, latent):
    if latent:
        (q_ref, k_ref, v_ref, qg_ref, kg_ref, gq_ref, gk_ref, cosq_ref, sinq_ref, cosk_ref, sink_ref,
         ck_ref, cv_ref, o_ref) = refs
    else:
        (q_ref, k_ref, v_ref, qg_ref, kg_ref, gq_ref, gk_ref, o_ref, kout_ref, vout_ref) = refs
    pair_w = 2 * HEAD_DIM

    def head_norm(x, mean_ref, gain):
        sq = x * x
        hi = sq.astype(BF16)
        lo = (sq - hi.astype(F32)).astype(BF16)
        ms = jnp.dot(hi, mean_ref[...], preferred_element_type=F32)
        ms = ms + jnp.dot(lo, mean_ref[...], preferred_element_type=F32)
        return x * lax.rsqrt(ms + EPS) * gain

    def rope(x, cos, sin):
        n = x.shape[1]
        lane = lax.broadcasted_iota(jnp.int32, x.shape, 1)
        first_of_pair = (lane // ROPE_PAIRS) % 2 == 0
        swapped = jnp.where(first_of_pair, pltpu.roll(x, n - ROPE_PAIRS, axis=1), pltpu.roll(x, ROPE_PAIRS, axis=1))
        return x * cos + swapped * sin

    nt = (((1,), (1,)), ((), ()))

    def prepare(rows, seq_idx):
        q = head_norm(q_ref[rows, :].astype(F32), gq_ref, qg_ref[...])
        k = head_norm(k_ref[rows, :].astype(F32), gk_ref, kg_ref[...])
        if latent:
            q = rope(q, cosq_ref[...], sinq_ref[...])
            k = rope(k, cosk_ref[...], sink_ref[...])
        q = q * (HEAD_DIM ** -0.5)
        v = v_ref[rows, :].astype(F32)
        n_q = q.shape[0]
        low_kv = lax.broadcasted_iota(jnp.int32, k.shape, 1) < HEAD_DIM
        low_q = lax.broadcasted_iota(jnp.int32, (n_q, pair_w), 1) < HEAD_DIM
        k_swapped = pltpu.roll(k, HEAD_DIM, axis=1)
        v_swapped = pltpu.roll(v, HEAD_DIM, axis=1)
        if not latent:
            kout_ref[seq_idx] = k.T
            vout_ref[seq_idx] = v.T
        units = []
        for j in range(KV_HEADS):
            kd = (jnp.where(low_kv, k, k_swapped) if j == 0 else jnp.where(low_kv, k_swapped, k)).astype(BF16)
            vd = (jnp.where(low_kv, v, v_swapped) if j == 0 else jnp.where(low_kv, v_swapped, v)).astype(BF16)
            vd = jnp.concatenate([vd, jnp.ones_like(vd)], axis=1)
            tiles = range(j * Q_PER_KV // 2, (j + 1) * Q_PER_KV // 2)
            parts = []
            for t in tiles:
                qt = q[:, t * pair_w:(t + 1) * pair_w]
                parts += [jnp.where(low_q, qt, 0.0), jnp.where(low_q, 0.0, qt)]
            units.append(dict(j=j, rows=rows, tiles=tiles, n_q=n_q, low_q=low_q, kd=kd, vd=vd,
                              qs=jnp.concatenate(parts, axis=0).astype(BF16)))
        return units

    def scores(u):
        u["s_new"] = lax.dot_general(u["qs"], u["kd"], nt, preferred_element_type=F32)
        if latent:
            j = u["j"]
            cvd = jnp.concatenate([cv_ref[j], cv_ref[j]], axis=1).astype(BF16)
            u["cvd"] = jnp.concatenate([cvd, jnp.ones_like(cvd)], axis=1)
            ckd = jnp.concatenate([ck_ref[j], ck_ref[j]], axis=1).astype(BF16)
            u["s_old"] = lax.dot_general(u["qs"], ckd, nt, preferred_element_type=F32)

    def softmax(u):
        m = jnp.max(u["s_new"], axis=-1, keepdims=True)
        if latent:
            m = jnp.maximum(m, jnp.max(u["s_old"], axis=-1, keepdims=True))
        u["p_new"] = jnp.exp(u.pop("s_new") - m).astype(BF16)
        if latent:
            u["p_old"] = jnp.exp(u.pop("s_old") - m).astype(BF16)

    def weighted_values(u):
        acc = jnp.dot(u["p_new"], u["vd"], preferred_element_type=F32)
        if latent:
            acc = acc + jnp.dot(u["p_old"], u["cvd"], preferred_element_type=F32)
        out = acc[:, :pair_w] / acc[:, pair_w:]
        n_q = u["n_q"]
        for i, t in enumerate(u["tiles"]):
            lo_head = out[(2 * i) * n_q:(2 * i + 1) * n_q, :]
            hi_head = out[(2 * i + 1) * n_q:(2 * i + 2) * n_q, :]
            o_ref[u["rows"], t * pair_w:(t + 1) * pair_w] = jnp.where(u["low_q"], lo_head, hi_head).astype(o_ref.dtype)

    if latent:
        units = prepare(slice(None), None)
    else:
        seq = PROMPT_LEN
        units = [u for s in range(q_ref.shape[0] // seq) for u in prepare(slice(s * seq, (s + 1) * seq), s)]
    for phase in (scores, softmax, weighted_values):
        for u in units:
            phase(u)


def _attn_common_args(qn_g, kn_g):
    q_w, kv_w = Q_HEADS * HEAD_DIM, KV_HEADS * HEAD_DIM
    return (jnp.tile(qn_g, Q_HEADS).reshape(1, q_w), jnp.tile(kn_g, KV_HEADS).reshape(1, kv_w),
            _head_mean_matrix(q_w), _head_mean_matrix(kv_w))


def _attention_prompt(z, qn_g, kn_g):
    seqs = 8
    L = seqs * PROMPT_LEN
    cache_shape = jax.ShapeDtypeStruct((N_PROMPT_SEQ, KV_HEADS * HEAD_DIM, PROMPT_LEN), F32)
    cache_spec = pl.BlockSpec((seqs, KV_HEADS * HEAD_DIM, PROMPT_LEN), lambda s: (s, 0, 0))
    q_w, kv_w = Q_HEADS * HEAD_DIM, KV_HEADS * HEAD_DIM
    q_col = (5 * A_WIDTH) // q_w
    k_col = (5 * A_WIDTH + q_w) // kv_w
    const = lambda r, c: pl.BlockSpec((r, c), lambda s: (0, 0))
    return pl.pallas_call(
        functools.partial(_attn_kernel, latent=False),
        grid=(N_PROMPT_TOK // L,),
        in_specs=[
            pl.BlockSpec((L, q_w), lambda s: (s, q_col)),
            pl.BlockSpec((L, kv_w), lambda s: (s, k_col)),
            pl.BlockSpec((L, kv_w), lambda s: (s, k_col + 1)),
            const(1, q_w), const(1, kv_w), const(q_w, q_w), const(kv_w, kv_w),
        ],
        out_specs=(pl.BlockSpec((L, q_w), lambda s: (s, 0)), cache_spec, cache_spec),
        out_shape=(jax.ShapeDtypeStruct((N_PROMPT_TOK, q_w), BF16), cache_shape, cache_shape),
        compiler_params=_params("arbitrary"),
        name="attn_prompt",
    )(z, z, z, *_attn_common_args(qn_g, kn_g))


def _attention_latent(z, qn_g, kn_g, cache_k, cache_v):
    L = LATENT_LEN
    nqb = L // Q_BLOCK
    q_w, kv_w = Q_HEADS * HEAD_DIM, KV_HEADS * HEAD_DIM
    q_col = (5 * A_WIDTH) // q_w
    k_col = (5 * A_WIDTH + q_w) // kv_w
    qrow0 = N_PROMPT_TOK // Q_BLOCK
    krow0 = N_PROMPT_TOK // L
    cos, sin = _rope_tables()
    cos_q, sin_q = jnp.asarray(np.tile(cos, (1, Q_HEADS))), jnp.asarray(np.tile(sin, (1, Q_HEADS)))
    cos_k, sin_k = jnp.asarray(np.tile(cos, (1, KV_HEADS))), jnp.asarray(np.tile(sin, (1, KV_HEADS)))
    const = lambda r, c: pl.BlockSpec((r, c), lambda s, b: (0, 0))
    cache_spec = pl.BlockSpec((None, None, KV_HEADS, PAST_LEN, HEAD_DIM), lambda s, b: (s, 0, 0, 0, 0))
    return pl.pallas_call(
        functools.partial(_attn_kernel, latent=True),
        grid=(N_LATENT_SEQ, nqb),
        in_specs=[
            pl.BlockSpec((Q_BLOCK, q_w), lambda s, b: (qrow0 + s * nqb + b, q_col)),
            pl.BlockSpec((L, kv_w), lambda s, b: (krow0 + s, k_col)),
            pl.BlockSpec((L, kv_w), lambda s, b: (krow0 + s, k_col + 1)),
            const(1, q_w), const(1, kv_w), const(q_w, q_w), const(kv_w, kv_w),
            pl.BlockSpec((Q_BLOCK, q_w), lambda s, b: (b, 0)),
            pl.BlockSpec((Q_BLOCK, q_w), lambda s, b: (b, 0)),
            const(L, kv_w), const(L, kv_w),
            cache_spec, cache_spec,
        ],
        out_specs=pl.BlockSpec((Q_BLOCK, q_w), lambda s, b: (s * nqb + b, 0)),
        out_shape=jax.ShapeDtypeStruct((N_LATENT_TOK, q_w), BF16),
        compiler_params=_params("arbitrary", "arbitrary"),
        name="attn_latent",
    )(z, z, z, *_attn_common_args(qn_g, kn_g), cos_q, sin_q, cos_k, sin_k, cache_k, cache_v)


def _out_proj_kernel(*refs, n_act, n_x):
    a_refs = refs[:2 * n_act]
    x_refs = refs[2 * n_act:2 * n_act + n_x]
    g_ref, mod_ref, rw_ref, w_ref, xo_ref, h_ref, lg_ref, wb_ref, rws_ref, acc_ref = refs[2 * n_act + n_x:]
    _cast_once(w_ref, wb_ref)

    @pl.when(pl.program_id(0) == 0)
    def _():
        rw = rw_ref[...]
        hi = rw.astype(BF16).astype(F32)
        lo = (rw - hi).astype(BF16).astype(F32)
        rws_ref[...] = (hi + pltpu.roll(lo, N_EXPERTS, axis=1)).astype(BF16)

    mod = mod_ref[...]
    n = OUT_PROJ_SUB_ROWS

    n_sub = xo_ref.shape[0] // n

    def sub_rows(r):
        if isinstance(r, int):
            return slice(r * n, (r + 1) * n)
        return pl.ds(pl.multiple_of(r * n, n), n)

    def project(r):
        rows = sub_rows(r)
        acc = None
        k0 = 0
        for ap_ref, al_ref in zip(a_refs[0::2], a_refs[1::2]):
            k1 = k0 + ap_ref.shape[1]
            part = jnp.dot(_select_trunk(ap_ref, al_ref, rows), wb_ref[k0:k1, :], preferred_element_type=F32)
            acc = part if acc is None else acc + part
            k0 = k1
        acc_ref[r % 2] = acc

    def finish(r):
        rows = sub_rows(r)
        x_in = x_refs[0][rows, :] if n_x == 1 else _select_trunk(*x_refs, rows)
        x = x_in + mod[2:3, :] * acc_ref[r % 2]
        xo_ref[rows, :] = x
        h = _modulated_norm(x, g_ref[...], mod, 3, 4)
        h_ref[rows, :] = _pack_rows(h)
        h_hi = h.astype(BF16)
        h_lo = (h - h_hi.astype(F32)).astype(BF16)
        both = jnp.dot(jnp.concatenate([h_hi, h_lo], axis=0), rws_ref[...], preferred_element_type=F32)
        from_hi, from_lo = both[:n], both[n:]
        lg = from_hi + pltpu.roll(from_hi, ROUTER_LANES - N_EXPERTS, axis=1) + from_lo
        lg_ref[:, rows] = lg.T[:N_EXPERTS, :]

    project(0)
    for r in range(n_sub - 1):
        project(r + 1)
        finish(r)
    finish(n_sub - 1)


def _out_proj(acts, w, xs, g, mod_l, router_wp, block_rows=1024):
    tok = lambda width: pl.BlockSpec((block_rows, width), lambda i: (i, 0))
    in_specs = [spec for ap, _ in acts for spec in _trunk_specs(block_rows, ap.shape[1])]
    in_specs += [tok(D_MODEL)] if len(xs) == 1 else list(_trunk_specs(block_rows, D_MODEL))
    in_specs += [_resident((1, D_MODEL)), _mod_spec(block_rows), _resident((D_MODEL, ROUTER_LANES)),
                 _resident(w.shape)]
    return pl.pallas_call(
        functools.partial(_out_proj_kernel, n_act=len(acts), n_x=len(xs)),
        grid=(N_TOK // block_rows,),
        in_specs=in_specs,
        out_specs=(tok(D_MODEL), tok(ROW_WORDS), pl.BlockSpec((N_EXPERTS, block_rows), lambda i: (0, i))),
        out_shape=(jax.ShapeDtypeStruct((N_TOK, D_MODEL), F32),
                   jax.ShapeDtypeStruct((N_TOK, ROW_WORDS), jnp.int32),
                   jax.ShapeDtypeStruct((N_EXPERTS, N_TOK), F32)),
        scratch_shapes=[pltpu.VMEM(w.shape, BF16), pltpu.VMEM((D_MODEL, ROUTER_LANES), BF16),
                        pltpu.VMEM((2, OUT_PROJ_SUB_ROWS, D_MODEL), F32)],
        compiler_params=_params("arbitrary"),
        name="out_proj",
    )(*[a for pair in acts for a in pair], *xs, g.reshape(1, D_MODEL), mod_l, router_wp, w)


def _router_kernel(lg_ref, rb_ref, pos_ref, w_ref, plan_ref, rank_ref):
    lg = lg_ref[...]
    ex = jnp.exp(lg - jnp.max(lg, axis=0, keepdims=True))
    scores = ex / jnp.sum(ex, axis=0, keepdims=True)
    biased = scores + rb_ref[...]
    expert = lax.broadcasted_iota(jnp.int32, biased.shape, 0)
    in_pos = expert % EXPERTS_PER_GROUP
    rank = jnp.zeros_like(biased)
    for d in range(1, EXPERTS_PER_GROUP):
        wraps = in_pos + d >= EXPERTS_PER_GROUP
        partner = jnp.where(wraps, pltpu.roll(biased, EXPERTS_PER_GROUP - d, axis=0),
                            pltpu.roll(biased, N_EXPERTS - d, axis=0))
        rank = rank + jnp.where(wraps, jnp.where(partner >= biased, 1.0, 0.0), jnp.where(partner > biased, 1.0, 0.0))
    selected = rank < 1.5
    contrib = jnp.where(selected, biased, 0.0)
    group_score = []
    for gi in range(N_GROUPS):
        r = [contrib[gi * EXPERTS_PER_GROUP + i:gi * EXPERTS_PER_GROUP + i + 1, :] for i in range(EXPERTS_PER_GROUP)]
        group_score.append(((r[0] + r[1]) + r[2]) + r[3])
    best = group_score[0]
    best_group = jnp.zeros_like(best)
    for gi in range(1, N_GROUPS):
        better = group_score[gi] > best
        best_group = jnp.where(better, float(gi), best_group)
        best = jnp.where(better, group_score[gi], best)
    in_group = (expert // EXPERTS_PER_GROUP).astype(F32) == best_group
    chosen = jnp.where(selected, jnp.where(in_group, 1.0, 0.0), 0.0)
    picked = chosen * scores
    gates = picked / jnp.sum(picked, axis=0, keepdims=True)
    lanes = 128
    n_blk = N_TOK // lanes
    li = lax.broadcasted_iota(jnp.int32, (lanes, lanes), 0)
    lj = lax.broadcasted_iota(jnp.int32, (lanes, lanes), 1)
    prefix = jnp.where(li <= lj, 1.0, 0.0).astype(BF16)
    stacked = jnp.concatenate([chosen[:, blk * lanes:(blk + 1) * lanes] for blk in range(n_blk)], axis=0)
    incl_all = jnp.dot(stacked.astype(BF16), prefix, preferred_element_type=F32)
    carry = jnp.zeros((N_EXPERTS, 1), F32)
    for blk in range(n_blk):
        cols = slice(blk * lanes, (blk + 1) * lanes)
        incl = incl_all[blk * N_EXPERTS:(blk + 1) * N_EXPERTS, :]
        rank_ref[:, cols] = incl - chosen[:, cols] + carry
        carry = carry + incl[:, lanes - 1:lanes]
    count = carry
    padded = jnp.floor((count + float(MOE_TILE - 1)) * (1.0 / MOE_TILE)) * float(MOE_TILE)
    erow = lax.broadcasted_iota(jnp.int32, (N_EXPERTS, 1), 0)
    offset = jnp.zeros((N_EXPERTS, 1), F32)
    for e in range(N_EXPERTS - 1):
        offset = offset + jnp.where(erow > e, padded[e:e + 1, :], 0.0)
    position = rank_ref[...] + offset
    ei = lax.broadcasted_iota(jnp.int32, (N_EXPERTS, N_EXPERTS), 0)
    ej = lax.broadcasted_iota(jnp.int32, (N_EXPERTS, N_EXPERTS), 1)
    lower = jnp.where(ej <= ei, 1.0, 0.0).astype(BF16)
    seen = jnp.dot(lower, chosen.astype(BF16), preferred_element_type=F32)
    first = (chosen > 0.5) & (seen < 1.5)
    second = (chosen > 0.5) & (seen > 1.5)
    pick = lambda flag, x: jnp.sum(jnp.where(flag, x, 0.0), axis=0, keepdims=True)
    pos_ref[0:1, :] = pick(first, position).astype(jnp.int32)
    pos_ref[1:2, :] = pick(second, position).astype(jnp.int32)
    w_rows = jnp.concatenate([pick(first, gates), pick(second, gates), jnp.zeros((6, N_TOK), F32)], axis=0)
    ti = lax.broadcasted_iota(jnp.int32, (8, lanes), 0)
    tj = lax.broadcasted_iota(jnp.int32, (8, lanes), 1)
    eye = jnp.where(ti == tj, 1.0, 0.0).astype(BF16)
    tn = (((0,), (0,)), ((), ()))
    hi = w_rows.astype(BF16)
    r1 = w_rows - hi.astype(F32)
    mid = r1.astype(BF16)
    lo = (r1 - mid.astype(F32)).astype(BF16)
    w_cols = lax.dot_general(hi, eye, tn, preferred_element_type=F32)
    w_cols = w_cols + lax.dot_general(mid, eye, tn, preferred_element_type=F32)
    w_cols = w_cols + lax.dot_general(lo, eye, tn, preferred_element_type=F32)
    w_ref[...] = w_cols[:, :TOP_K]
    start = (lax.broadcasted_iota(jnp.int32, (N_EXPERTS, lanes), 1) * MOE_TILE).astype(F32)
    end = offset + padded
    tile_expert = jnp.sum(jnp.where(end <= start, 1.0, 0.0), axis=0, keepdims=True)
    inside = (offset <= start) & (start < end)
    real = jnp.clip(count - (start - offset), 0.0, float(MOE_TILE))
    tile_rows = jnp.sum(jnp.where(inside, real, 0.0), axis=0, keepdims=True)
    plan_ref[0:1, :] = jnp.minimum(tile_expert, float(N_EXPERTS - 1)).astype(jnp.int32)
    plan_ref[1:2, :] = tile_rows.astype(jnp.int32)


def _router(logits_t, router_b):
    whole = lambda shape: pl.BlockSpec(shape, lambda i: (0, 0))
    return pl.pallas_call(
        _router_kernel,
        grid=(1,),
        in_specs=[whole((N_EXPERTS, N_TOK)), whole((N_EXPERTS, 1))],
        out_specs=(whole((2, N_TOK)), whole((N_TOK, TOP_K)), whole((2, 128))),
        out_shape=(jax.ShapeDtypeStruct((2, N_TOK), jnp.int32),
                   jax.ShapeDtypeStruct((N_TOK, TOP_K), F32),
                   jax.ShapeDtypeStruct((2, 128), jnp.int32)),
        scratch_shapes=[pltpu.VMEM((N_EXPERTS, N_TOK), F32)],
        compiler_params=_params("arbitrary"),
        name="router",
    )(logits_t, router_b.reshape(N_EXPERTS, 1))


def _sc_mesh():
    return plsc.VectorSubcoreMesh(core_axis_name="c", subcore_axis_name="s")


def _sc_worker_base():
    return (lax.axis_index("s") * SC_CORES + lax.axis_index("c")) * SC_TOKENS_PER_WORKER


def _moe_dispatch(h, pos_a, pos_b):
    n_chunks = SC_TOKENS_PER_WORKER // SC_CHUNK
    idx = pltpu.VMEM((SC_CHUNK,), jnp.int32)

    @functools.partial(
        pl.kernel, mesh=_sc_mesh(),
        out_type=jax.ShapeDtypeStruct((MOE_ROWS, ROW_WORDS), jnp.int32),
        scratch_types=[idx, idx, idx, idx, pltpu.VMEM((2, SC_CHUNK, ROW_WORDS), jnp.int32),
                       pltpu.SemaphoreType.DMA((6,)), pltpu.SemaphoreType.DMA((4,))],
        name="moe_dispatch",
    )
    def run(h_hbm, pa_hbm, pb_hbm, xs_hbm, ia0, ib0, ia1, ib1, rows_v, sem_in, sem_out):
        base = _sc_worker_base()
        ia, ib = (ia0, ia1), (ib0, ib1)

        def start_loads(c):
            slot = c % 2
            tok = pl.ds(pl.multiple_of(base + c * SC_CHUNK, 8), SC_CHUNK)
            return (pltpu.async_copy(pa_hbm.at[tok], ia[slot], sem_in.at[3 * slot]),
                    pltpu.async_copy(pb_hbm.at[tok], ib[slot], sem_in.at[3 * slot + 1]),
                    pltpu.async_copy(h_hbm.at[tok], rows_v.at[slot], sem_in.at[3 * slot + 2]))

        loads = start_loads(0)
        scatters = [(), ()]
        for c in range(n_chunks):
            slot = c % 2
            for cp in loads:
                cp.wait()
            if c + 1 < n_chunks:
                for cp in scatters[1 - slot]:
                    cp.wait()
                scatters[1 - slot] = ()
                loads = start_loads(c + 1)
            scatters[slot] = (pltpu.async_copy(rows_v.at[slot], xs_hbm.at[ia[slot]], sem_out.at[2 * slot]),
                              pltpu.async_copy(rows_v.at[slot], xs_hbm.at[ib[slot]], sem_out.at[2 * slot + 1]))
        for pending in scatters:
            for cp in pending:
                cp.wait()

    return run(h, pos_a, pos_b)


def _moe_collect(ys, pos_a, pos_b, tok0=0, n_tok=N_TOK):
    per_worker = n_tok // SC_WORKERS
    chunk = SC_CHUNK if per_worker % SC_CHUNK == 0 else 32
    n_chunks = per_worker // chunk
    out = jax.ShapeDtypeStruct((n_tok, ROW_WORDS), jnp.int32)
    idx = pltpu.VMEM((per_worker,), jnp.int32)
    rows = pltpu.VMEM((2, chunk, ROW_WORDS), jnp.int32)

    @functools.partial(
        pl.kernel, mesh=_sc_mesh(), out_type=(out, out),
        scratch_types=[idx, idx, rows, rows, pltpu.SemaphoreType.DMA((4,)), pltpu.SemaphoreType.DMA((4,))],
        name="moe_collect",
    )
    def run(ys_hbm, pa_hbm, pb_hbm, ya_hbm, yb_hbm, ia_v, ib_v, ra_v, rb_v, sem_g, sem_w):
        base = (lax.axis_index("s") * SC_CORES + lax.axis_index("c")) * per_worker
        mine = pl.ds(pl.multiple_of(tok0 + base, 8), per_worker)
        pltpu.sync_copy(pa_hbm.at[mine], ia_v)
        pltpu.sync_copy(pb_hbm.at[mine], ib_v)
        writes = [(), ()]
        for c in range(n_chunks):
            slot = c % 2
            for cp in writes[slot]:
                cp.wait()
            part = pl.ds(c * chunk, chunk)
            tok = pl.ds(pl.multiple_of(base + c * chunk, 8), chunk)
            ga = pltpu.async_copy(ys_hbm.at[ia_v.at[part]], ra_v.at[slot], sem_g.at[slot])
            gb = pltpu.async_copy(ys_hbm.at[ib_v.at[part]], rb_v.at[slot], sem_g.at[2 + slot])
            ga.wait()
            wa = pltpu.async_copy(ra_v.at[slot], ya_hbm.at[tok], sem_w.at[slot])
            gb.wait()
            wb = pltpu.async_copy(rb_v.at[slot], yb_hbm.at[tok], sem_w.at[2 + slot])
            writes[slot] = (wa, wb)
        for pending in writes:
            for cp in pending:
                cp.wait()

    return run(ys, pos_a, pos_b)


def _experts_kernel(plan_ref, xs_ref, wg_hbm, wu_hbm, wd_hbm, y_ref,
                    sg_ref, su_ref, sd_ref, wgb_ref, wub_ref, wdb_ref, hid_ref, sems, seg_ref, *, layer):
    n_tiles = pl.num_programs(0) * EXPERT_TILES_PER_STEP

    def weight_copies(e, slot):
        return (pltpu.make_async_copy(wg_hbm.at[layer, e], sg_ref.at[slot], sems.at[slot, 0]),
                pltpu.make_async_copy(wu_hbm.at[layer, e], su_ref.at[slot], sems.at[slot, 1]),
                pltpu.make_async_copy(wd_hbm.at[layer, e], sd_ref.at[slot], sems.at[slot, 2]))

    def tile(t, row0):
        expert = plan_ref[t]
        n_real = plan_ref[PLAN_LANES + t]
        fresh = jnp.logical_or(t == 0, expert != plan_ref[jnp.maximum(t - 1, 0)])

        @pl.when(t == 0)
        def _():
            seg_ref[0] = 0

            @pl.when(n_real > 0)
            def _():
                for cp in weight_copies(expert, 0):
                    cp.start()

        @pl.when(jnp.logical_and(n_real > 0, fresh))
        def _():
            slot = seg_ref[0] % 2
            for cp in weight_copies(expert, slot):
                cp.wait()
            wgb_ref[...] = sg_ref[slot].astype(BF16)
            wub_ref[...] = su_ref[slot].astype(BF16)
            wdb_ref[...] = sd_ref[slot].astype(BF16)
            nxt = lax.while_loop(
                lambda u: jnp.logical_and(u < n_tiles, plan_ref[jnp.minimum(u, n_tiles - 1)] == expert),
                lambda u: u + 1, t + 1)
            nxt_c = jnp.minimum(nxt, n_tiles - 1)

            @pl.when(jnp.logical_and(nxt < n_tiles, plan_ref[PLAN_LANES + nxt_c] > 0))
            def _():
                for cp in weight_copies(plan_ref[nxt_c], 1 - slot):
                    cp.start()

            seg_ref[0] = seg_ref[0] + 1

        @pl.when(n_real > 0)
        def _():
            n = EXPERT_SUB_ROWS
            n_sub = MOE_TILE // n
            row = lax.broadcasted_iota(jnp.int32, (n, xs_ref.shape[1]), 0)

            def up(r):
                rows = slice(row0 + r * n, row0 + (r + 1) * n)
                words = jnp.where(row < n_real - r * n, xs_ref[rows, :], 0)
                x = _unpack_rows(words).astype(BF16)
                a = jnp.dot(x, wgb_ref[...], preferred_element_type=F32)
                b = jnp.dot(x, wub_ref[...], preferred_element_type=F32)
                hid_ref[r] = ((a * jax.nn.sigmoid(a)) * b).astype(BF16)

            def down(r):
                rows = slice(row0 + r * n, row0 + (r + 1) * n)
                y_ref[rows, :] = _pack_rows(jnp.dot(hid_ref[r], wdb_ref[...], preferred_element_type=F32))

            up(0)
            for r in range(1, n_sub):
                up(r)
                down(r - 1)
            down(n_sub - 1)

    for q in range(EXPERT_TILES_PER_STEP):
        tile(pl.program_id(0) * EXPERT_TILES_PER_STEP + q, q * MOE_TILE)


def _experts(plan, xs, w_gate, w_up, w_down, layer):
    hbm = pl.BlockSpec(memory_space=pl.ANY)
    step_rows = MOE_TILE * EXPERT_TILES_PER_STEP
    return pl.pallas_call(
        functools.partial(_experts_kernel, layer=layer),
        grid_spec=pltpu.PrefetchScalarGridSpec(
            num_scalar_prefetch=1,
            grid=(MOE_ROWS // step_rows,),
            in_specs=[pl.BlockSpec((step_rows, ROW_WORDS), lambda j, plan: (j, 0)), hbm, hbm, hbm],
            out_specs=pl.BlockSpec((step_rows, ROW_WORDS), lambda j, plan: (j, 0)),
            scratch_shapes=[pltpu.VMEM((2, D_MODEL, D_EXPERT), F32), pltpu.VMEM((2, D_MODEL, D_EXPERT), F32),
                            pltpu.VMEM((2, D_EXPERT, D_MODEL), F32),
                            pltpu.VMEM((D_MODEL, D_EXPERT), BF16), pltpu.VMEM((D_MODEL, D_EXPERT), BF16),
                            pltpu.VMEM((D_EXPERT, D_MODEL), BF16),
                            pltpu.VMEM((MOE_TILE // EXPERT_SUB_ROWS, EXPERT_SUB_ROWS, D_EXPERT), BF16),
                            pltpu.SemaphoreType.DMA((2, 3)), pltpu.SMEM((1,), jnp.int32)],
        ),
        out_shape=jax.ShapeDtypeStruct((MOE_ROWS, ROW_WORDS), jnp.int32),
        compiler_params=_params("arbitrary"),
        name="experts",
    )(plan, xs, w_gate, w_up, w_down)


def _combine_kernel(x_ref, ya_ref, yb_ref, wt_ref, mod_ref, o_ref):
    o_ref[...] = _moe_mix(x_ref, ya_ref, yb_ref, wt_ref, mod_ref)


def _combine(x, moe_out, mod_l, tok0, n_tok, block_rows=1024):
    ya, yb, w_tok = moe_out
    b0 = tok0 // block_rows
    rows = lambda width: pl.BlockSpec((block_rows, width), lambda i: (b0 + i, 0))
    local = pl.BlockSpec((block_rows, ROW_WORDS), lambda i: (i, 0))
    return pl.pallas_call(
        _combine_kernel,
        grid=(n_tok // block_rows,),
        in_specs=[rows(D_MODEL), local, local, rows(TOP_K),
                  pl.BlockSpec((None, 6, D_MODEL), lambda i: (_cond_of_token_block(b0 + i, block_rows), 0, 0))],
        out_specs=pl.BlockSpec((block_rows, D_MODEL), lambda i: (i, 0)),
        out_shape=jax.ShapeDtypeStruct((n_tok, D_MODEL), F32),
        compiler_params=_params("arbitrary"),
        name="combine",
    )(x, ya, yb, w_tok, mod_l)


def _moe(h, logits_t, router_b, w_gate, w_up, w_down, layer, ranges=((0, N_TOK),)):
    pos, w, plan = _router(logits_t, router_b)
    xs = _moe_dispatch(h, pos[0], pos[1])
    ys = _experts(plan.reshape(-1), xs, w_gate, w_up, w_down, layer)
    return [(*_moe_collect(ys, pos[0], pos[1], tok0, n_tok), w) for tok0, n_tok in ranges]


def _dft_tables(L):
    k = np.arange(L)[:, None]
    m = np.arange(L)[None, :]
    r = (k * m) % (2 * L)
    ang = np.pi * r.astype(np.float64) / L
    fc = np.cos(ang)
    fs = np.sin(ang)
    fs[0, :] = np.where(np.arange(L) % 2 == 0, 1.0, -1.0)
    wk = np.full((L, 1), 1.0 / L)
    wk[0, 0] = 0.5 / L
    gc = (fc * wk).T
    gs = (fs * wk).T
    return [jnp.asarray(t.astype(np.float32)).astype(BF16) for t in (fc, fs, gc, gs)]


def _filter_consts(L):
    t = np.linspace(0.0, 1.0, L, dtype=np.float32)[:, None]
    w = (np.float32(2.0 * np.pi) * np.arange(L, dtype=np.float32)[:, None] / np.float32(L)).astype(np.float32)
    fb = np.linspace(1e-4, HY_BANDS - 1, HY_BANDS, dtype=np.float32)[None, :]
    emb = np.concatenate([t, np.cos(fb * w), -np.sin(fb * w)], axis=-1).astype(np.float32)
    lo = math.log(HY_DECAY_TARGET) / HY_SLOW_PCT
    hi = math.log(HY_DECAY_TARGET) / HY_FAST_PCT
    deltas = np.abs(np.linspace(lo, hi, D_MODEL, dtype=np.float32))
    decay = np.exp(-t * deltas).astype(np.float32)
    return jnp.asarray(emb), jnp.asarray(decay)


def _filter_kernel(emb_ref, w1_ref, b1_ref, w2_ref, b2_ref, fr_ref, w3f_ref, w3b_ref, dec_ref,
                   fc_ref, fs_ref, kr_ref, q_ref, krn_ref, hd_ref):
    @pl.when(pl.program_id(0) == 0)
    def _():
        fr = fr_ref[...]
        h1 = jnp.sin(fr * (jnp.dot(emb_ref[...], w1_ref[...], precision=HIGHEST,
                                   preferred_element_type=F32) + b1_ref[...]))
        hd_ref[...] = jnp.sin(fr * (jnp.dot(h1, w2_ref[...], precision=HIGHEST,
                                            preferred_element_type=F32) + b2_ref[...]))

    hd = hd_ref[...]
    dec = dec_ref[...]
    f = jnp.dot(hd, w3f_ref[...], precision=HIGHEST, preferred_element_type=F32) * dec
    g = jnp.dot(hd, w3b_ref[...], precision=HIGHEST, preferred_element_type=F32) * dec
    row = lax.broadcasted_iota(jnp.int32, f.shape, 0)
    g = jnp.where(row == 0, 0.0, g)
    s = f + g
    d = f - g
    kr = jnp.dot(fc_ref[...], s.astype(BF16), preferred_element_type=F32)
    qq = jnp.dot(fs_ref[...], d.astype(BF16), preferred_element_type=F32)
    alt = jnp.where(row % 2 == 0, 1.0, -1.0)
    nyq = jnp.sum(alt * s, axis=0, keepdims=True)
    kr_ref[...] = kr
    q_ref[...] = jnp.where(row == 0, 0.0, qq)
    krn_ref[...] = jnp.where(row == 0, nyq, kr)


def _hyena_filter_spectrum(L, w1, b1, w2, b2, w3, freq, fc, fs, cblk=256):
    emb, decay = _filter_consts(L)
    ncb = D_MODEL // cblk
    n_emb = 128
    emb = jnp.pad(emb, ((0, 0), (0, n_emb - emb.shape[1])))
    w1 = jnp.pad(w1, ((0, n_emb - w1.shape[0]), (0, 0)))
    full = lambda shape: pl.BlockSpec(shape, lambda j: tuple(0 for _ in shape))
    out_sds = jax.ShapeDtypeStruct((L, D_MODEL), F32)
    out_spec = pl.BlockSpec((L, cblk), lambda j: (0, j))
    return pl.pallas_call(
        _filter_kernel,
        grid=(ncb,),
        in_specs=[
            full((L, n_emb)), full((n_emb, HY_FFN)), full((1, HY_FFN)), full((HY_FFN, HY_FFN)),
            full((1, HY_FFN)), full((1, HY_FFN)),
            pl.BlockSpec((HY_FFN, cblk), lambda j: (0, j)),
            pl.BlockSpec((HY_FFN, cblk), lambda j: (0, ncb + j)),
            pl.BlockSpec((L, cblk), lambda j: (0, j)),
            full((L, L)), full((L, L)),
        ],
        out_specs=(out_spec, out_spec, out_spec),
        out_shape=(out_sds, out_sds, out_sds),
        scratch_shapes=[pltpu.VMEM((L, HY_FFN), F32)],
        compiler_params=_params("arbitrary"),
        name=f"hyena_filter_{L}",
    )(emb, w1, b1.reshape(1, HY_FFN), w2, b2.reshape(1, HY_FFN), freq.reshape(1, HY_FFN), w3, w3, decay, fc, fs)


def _hyena_conv_kernel(x0_ref, x1_ref, v_ref, cw0_ref, cw1_ref, cwv_ref, cb0_ref, cb1_ref, cbv_ref,
                       kr_ref, q_ref, krn_ref, ds_ref, fc_ref, fs_ref, gc_ref, gs_ref, o_ref,
                       zz_ref, gate_ref, skip_ref, yr_ref, yw_ref):
    L = fc_ref.shape[0]
    unit_w = zz_ref.shape[2]
    units = [(slice(s * L, (s + 1) * L), slice(c * unit_w, (c + 1) * unit_w))
             for s in range(x0_ref.shape[0] // L) for c in range(x0_ref.shape[1] // unit_w)]
    row = lax.broadcasted_iota(jnp.int32, (L, unit_w), 0)

    def gating(i):
        rows, cols = units[i]

        def short_conv(u_ref, w_ref, b_ref):
            u = u_ref[rows, cols].astype(F32)
            w = w_ref[:, cols]
            prev = jnp.where(row == 0, 0.0, pltpu.roll(u, 1, axis=0))
            nxt = jnp.where(row == L - 1, 0.0, pltpu.roll(u, L - 1, axis=0))
            return prev * w[0:1, :] + u * w[1:2, :] + nxt * w[2:3, :] + b_ref[:, cols]

        x0 = short_conv(x0_ref, cw0_ref, cb0_ref)
        zz = short_conv(v_ref, cwv_ref, cbv_ref) * short_conv(x1_ref, cw1_ref, cb1_ref)
        zz_ref[i] = zz.astype(BF16)
        gate_ref[i] = x0
        skip_ref[i] = x0 * zz * ds_ref[:, cols]

    def spectrum(i):
        cols = units[i][1]
        ur = jnp.dot(fc_ref[...], zz_ref[i], preferred_element_type=F32)
        p = jnp.dot(fs_ref[...], zz_ref[i], preferred_element_type=F32)
        qq = q_ref[:, cols]
        yr_ref[i] = (ur * kr_ref[:, cols] - p * qq).astype(BF16)
        yw_ref[i] = (ur * qq + p * krn_ref[:, cols]).astype(BF16)

    def synthesis(i):
        rows, cols = units[i]
        y = jnp.dot(gc_ref[...], yr_ref[i], preferred_element_type=F32)
        y = y + jnp.dot(gs_ref[...], yw_ref[i], preferred_element_type=F32)
        o_ref[rows, cols] = (gate_ref[i] * y + skip_ref[i]).astype(o_ref.dtype)

    for t in range(len(units) + 2):
        if t < len(units):
            gating(t)
        if 0 <= t - 1 < len(units):
            spectrum(t - 1)
        if 0 <= t - 2 < len(units):
            synthesis(t - 2)


def _hyena_conv(u, conv_w, conv_b, dskip, spectrum, tables, *, latent):
    L = LATENT_LEN if latent else PROMPT_LEN
    n_seq = N_LATENT_SEQ if latent else N_PROMPT_SEQ
    cblk = 512
    unit_w = 256 if latent else 512
    ncb = D_MODEL // cblk
    seqs = 1 if latent else 8
    unit = (seqs * cblk // unit_w, L, unit_w)
    row0 = (N_PROMPT_TOK // L) if latent else 0
    kr, qq, krn = spectrum
    fc, fs, gc, gs = tables

    def part(p, rows):
        if rows != L:
            return pl.BlockSpec((rows, cblk), lambda j, s: (0, p * ncb + j))
        return pl.BlockSpec((seqs * L, cblk), lambda j, s: (row0 // seqs + s, p * ncb + j))

    def const_cols(rows):
        return pl.BlockSpec((rows, cblk), lambda j, s: (0, j))

    mat = pl.BlockSpec((L, L), lambda j, s: (0, 0))
    conv_b2 = conv_b.reshape(1, 3 * D_MODEL)
    in_specs = [part(0, L), part(1, L), part(2, L),
                part(0, 3), part(1, 3), part(2, 3),
                part(0, 1), part(1, 1), part(2, 1),
                const_cols(L), const_cols(L), const_cols(L), const_cols(1),
                mat, mat, mat, mat]
    args = [u, u, u, conv_w, conv_w, conv_w, conv_b2, conv_b2, conv_b2,
            kr, qq, krn, dskip.reshape(1, D_MODEL), fc, fs, gc, gs]
    return pl.pallas_call(
        _hyena_conv_kernel,
        grid=(ncb, n_seq // seqs),
        in_specs=in_specs,
        out_specs=pl.BlockSpec((seqs * L, cblk), lambda j, s: (s, j)),
        out_shape=jax.ShapeDtypeStruct((n_seq * L, D_MODEL), BF16),
        scratch_shapes=[pltpu.VMEM(unit, BF16), pltpu.VMEM(unit, F32), pltpu.VMEM(unit, F32),
                        pltpu.VMEM(unit, BF16), pltpu.VMEM(unit, BF16)],
        compiler_params=_params("arbitrary", "arbitrary"),
        name="hyena_conv_latent" if latent else "hyena_conv_prompt",
    )(*args)


def kernel(x_prompt, x_sample, cache_k, cache_v, state_hgrn, c, c_ctx, norm_g, mod_w, mod_b, ab_in_w, hgrn_lb, hgrn_onorm_g, attn_qnorm_g, attn_knorm_g, ab_out_w, hy_in_w, hy_in_b, hy_conv_w, hy_conv_b, hy_f_w1, hy_f_b1, hy_f_w2, hy_f_b2, hy_f_w3, hy_f_freq, hy_dskip, hy_out_w, router_w, router_b, moe_w_gate, moe_w_up, moe_w_down):
    xp = x_prompt.reshape(N_PROMPT_TOK, D_MODEL)
    xl = x_sample.reshape(N_LATENT_TOK, D_MODEL)
    cond = jnp.concatenate([c_ctx[None, :], c, jnp.zeros((N_COND - 1 - N_LATENT_SEQ, D_MODEL), F32)], axis=0)
    mod = _modulation(cond, mod_w, mod_b)
    router_wp = jnp.pad(router_w, ((0, 0), (0, ROUTER_LANES - N_EXPERTS)))

    z = _in_proj0(xp, xl, norm_g[0, 0], mod[0], ab_in_w[0])
    oa_p, new_state = _hgrn(z, hgrn_lb, hgrn_onorm_g[0], None, latent=False)
    oa_l = _hgrn(z, hgrn_lb, hgrn_onorm_g[0], state_hgrn, latent=True)
    ob_p, k_fm, v_fm = _attention_prompt(z, attn_qnorm_g[0], attn_knorm_g[0])
    fm_shape = (N_PROMPT_SEQ, 1, KV_HEADS, HEAD_DIM, PROMPT_LEN)
    new_k = jnp.swapaxes(k_fm.reshape(fm_shape), -1, -2)
    new_v = jnp.swapaxes(v_fm.reshape(fm_shape), -1, -2)
    ob_l = _attention_latent(z, attn_qnorm_g[0], attn_knorm_g[0], cache_k, cache_v)
    x, h, logits_t = _out_proj([(oa_p, oa_l), (ob_p, ob_l)], ab_out_w[0], (xp, xl), norm_g[0, 1], mod[0],
                               router_wp)
    (moe_out,) = _moe(h, logits_t, router_b, moe_w_gate, moe_w_up, moe_w_down, 0)

    x, u = _in_proj1(x, moe_out, mod[0], norm_g[1, 0], mod[1], hy_in_w[0], hy_in_b[0])
    pre = []
    for latent in (False, True):
        L = LATENT_LEN if latent else PROMPT_LEN
        tables = _dft_tables(L)
        spectrum = _hyena_filter_spectrum(L, hy_f_w1[0], hy_f_b1[0], hy_f_w2[0], hy_f_b2[0], hy_f_w3[0],
                                          hy_f_freq[0], tables[0], tables[1])
        pre.append(_hyena_conv(u, hy_conv_w[0], hy_conv_b[0], hy_dskip[0], spectrum, tables, latent=latent))
    x, h, logits_t = _out_proj([tuple(pre)], hy_out_w[0], (x,), norm_g[1, 1], mod[1], router_wp)
    trunks = ((0, N_PROMPT_TOK), (N_PROMPT_TOK, N_LATENT_TOK))
    out_p, out_l = _moe(h, logits_t, router_b, moe_w_gate, moe_w_up, moe_w_down, 1, ranges=trunks)

    y_prompt = _combine(x, out_p, mod[1], *trunks[0]).reshape(N_PROMPT_SEQ, PROMPT_LEN, D_MODEL)
    y_sample = _combine(x, out_l, mod[1], *trunks[1]).reshape(N_LATENT_SEQ, LATENT_LEN, D_MODEL)
    return (y_prompt, y_sample, new_k, new_v, new_state)
```

```python
import functools
import math

import numpy as np
import jax
import jax.numpy as jnp
from jax import lax
from jax.experimental import pallas as pl
from jax.experimental.pallas import tpu as pltpu
from jax.experimental.pallas import tpu_sc as plsc

F32 = jnp.float32
BF16 = jnp.bfloat16
HIGHEST = lax.Precision.HIGHEST

D_MODEL = 1024
N_PROMPT_SEQ = 32
PROMPT_LEN = 256
N_LATENT_SEQ = 2
LATENT_LEN = 1024
PAST_LEN = 512
GRID_W = 64
N_PROMPT_TOK = N_PROMPT_SEQ * PROMPT_LEN
N_LATENT_TOK = N_LATENT_SEQ * LATENT_LEN
N_TOK = N_PROMPT_TOK + N_LATENT_TOK
N_COND = 8
EPS = 1e-6

A_WIDTH = 512
A_HEADS = 4
A_DK = 128
CHUNK = 64
HGRN_BLOCK = 128
HGRN_HEADS_PER_STEP = 4
HEAD_DIM = 64
Q_HEADS = 8
KV_HEADS = 2
Q_PER_KV = Q_HEADS // KV_HEADS
Q_BLOCK = 256
ROPE_THETA = 10000.0
ROPE_PAIRS = HEAD_DIM // 4

HY_BANDS = 16
HY_FFN = 64
HY_DECAY_TARGET = 1e-2
HY_FAST_PCT = 0.3
HY_SLOW_PCT = 1.5

N_EXPERTS = 16
N_GROUPS = 4
EXPERTS_PER_GROUP = 4
TOP_K = 2
D_EXPERT = 512
ROUTER_LANES = 128
OUT_PROJ_SUB_ROWS = 256
EXPERT_SUB_ROWS = 256
EXPERT_TILES_PER_STEP = 2
IN_PROJ_SUB_ROWS = 256
MOE_TILE = 512
MOE_ROWS = N_TOK * TOP_K + N_EXPERTS * MOE_TILE
PLAN_LANES = 128

SC_CORES = 2
SC_WORKERS = 32
SC_TOKENS_PER_WORKER = N_TOK // SC_WORKERS
SC_CHUNK = 40
ROW_WORDS = D_MODEL // 2

VMEM_LIMIT = 56 * 1024 * 1024


def _params(*sem):
    return pltpu.CompilerParams(dimension_semantics=sem, vmem_limit_bytes=VMEM_LIMIT)


def _pack_rows(x):
    n = x.shape[1] // 2
    bits = pltpu.bitcast(x.astype(BF16).astype(F32), jnp.uint32)
    return pltpu.bitcast(bits[:, :n] | (bits[:, n:] >> 16), jnp.int32)


def _unpack_rows(p):
    bits = pltpu.bitcast(p, jnp.uint32)
    hi = pltpu.bitcast(bits & jnp.uint32(0xFFFF0000), F32)
    lo = pltpu.bitcast(bits << 16, F32)
    return jnp.concatenate([hi, lo], axis=1)


def _cond_of_token_block(i, block_rows):
    start = i * block_rows
    return jnp.where(start < N_PROMPT_TOK, 0, 1 + (start - N_PROMPT_TOK) // LATENT_LEN)


def _mod_kernel(cond_ref, w_ref, b_ref, o_ref):
    cnd = cond_ref[...]
    s = cnd * jax.nn.sigmoid(cnd)
    s_hi = s.astype(BF16)
    s_lo = (s - s_hi.astype(F32)).astype(BF16)
    w = w_ref[...]
    w_hi = w.astype(BF16)
    w_lo = (w - w_hi.astype(F32)).astype(BF16)
    acc = jnp.dot(s_hi, w_hi, preferred_element_type=F32)
    acc = acc + jnp.dot(s_lo, w_hi, preferred_element_type=F32)
    acc = acc + jnp.dot(s_hi, w_lo, preferred_element_type=F32)
    o_ref[...] = acc + b_ref[...]


def _modulation(cond, mod_w, mod_b):
    depth = mod_w.shape[0]
    n_mod = 6
    cols = 2 * D_MODEL
    n_step = n_mod * D_MODEL // cols
    out = pl.pallas_call(
        _mod_kernel,
        grid=(depth, n_step),
        in_specs=[
            pl.BlockSpec((N_COND, D_MODEL), lambda l, j: (0, 0)),
            pl.BlockSpec((None, D_MODEL, cols), lambda l, j: (l, 0, j)),
            pl.BlockSpec((None, 1, cols), lambda l, j: (l, 0, j)),
        ],
        out_specs=pl.BlockSpec((None, N_COND, cols), lambda l, j: (l, 0, j)),
        out_shape=jax.ShapeDtypeStruct((depth, N_COND, n_mod * D_MODEL), F32),
        compiler_params=_params("arbitrary", "arbitrary"),
        name="modulation",
    )(cond, mod_w, mod_b.reshape(depth, 1, n_mod * D_MODEL))
    return out.reshape(depth, N_COND, n_mod, D_MODEL)


def _modulated_norm(x, g, mod, shift_row, scale_row):
    ms = jnp.mean(x * x, axis=-1, keepdims=True)
    y = x * lax.rsqrt(ms + EPS) * g
    return y * (1.0 + mod[scale_row:scale_row + 1, :]) + mod[shift_row:shift_row + 1, :]


def _trunk_specs(block_rows, width):
    n_prompt_blocks = N_PROMPT_TOK // block_rows
    return (pl.BlockSpec((block_rows, width), lambda i: (jnp.minimum(i, n_prompt_blocks - 1), 0)),
            pl.BlockSpec((block_rows, width), lambda i: (jnp.maximum(i - n_prompt_blocks, 0), 0)))


def _select_trunk(p_ref, l_ref, rows=slice(None)):
    block_rows = p_ref.shape[0]
    return jnp.where(pl.program_id(0) < N_PROMPT_TOK // block_rows, p_ref[rows, :], l_ref[rows, :])


def _cast_once(w_ref, wb_ref):
    @pl.when(pl.program_id(0) == 0)
    def _():
        wb_ref[...] = w_ref[...].astype(BF16)


def _resident(shape):
    return pl.BlockSpec(shape, lambda i: tuple(0 for _ in shape), pipeline_mode=pl.Buffered(1))


def _mod_spec(block_rows):
    return pl.BlockSpec((None, 6, D_MODEL), lambda i: (_cond_of_token_block(i, block_rows), 0, 0))


def _in_proj0_kernel(xp_ref, xl_ref, g_ref, mod_ref, w_ref, o_ref, wb_ref, hb_ref):
    _cast_once(w_ref, wb_ref)
    n = IN_PROJ_SUB_ROWS
    n_sub = xp_ref.shape[0] // n

    def prepare(r):
        x = _select_trunk(xp_ref, xl_ref, slice(r * n, (r + 1) * n))
        hb_ref[r] = _modulated_norm(x, g_ref[...], mod_ref[...], 0, 1).astype(BF16)

    def project(r):
        u = jnp.dot(hb_ref[r], wb_ref[...], preferred_element_type=F32)
        o_ref[r * n:(r + 1) * n, :] = u.astype(o_ref.dtype)

    prepare(0)
    for r in range(1, n_sub):
        prepare(r)
        project(r - 1)
    project(n_sub - 1)


def _in_proj0(x_prompt, x_latent, g, mod_l, w, block_rows=512):
    n = w.shape[1]
    return pl.pallas_call(
        _in_proj0_kernel,
        grid=(N_TOK // block_rows,),
        in_specs=[*_trunk_specs(block_rows, D_MODEL), _resident((1, D_MODEL)), _mod_spec(block_rows),
                  _resident((D_MODEL, n))],
        out_specs=pl.BlockSpec((block_rows, n), lambda i: (i, 0)),
        out_shape=jax.ShapeDtypeStruct((N_TOK, n), BF16),
        scratch_shapes=[pltpu.VMEM((D_MODEL, n), BF16),
                        pltpu.VMEM((block_rows // IN_PROJ_SUB_ROWS, IN_PROJ_SUB_ROWS, D_MODEL), BF16)],
        compiler_params=_params("arbitrary"),
        name="in_proj0",
    )(x_prompt, x_latent, g.reshape(1, D_MODEL), mod_l, w)


def _moe_mix(x_ref, ya_ref, yb_ref, wt_ref, mod_ref, rows=slice(None)):
    wt = wt_ref[rows, :]
    mix = wt[:, 0:1] * _unpack_rows(ya_ref[rows, :]) + wt[:, 1:2] * _unpack_rows(yb_ref[rows, :])
    return x_ref[rows, :] + mod_ref[5:6, :] * mix


def _in_proj1_kernel(x_ref, ya_ref, yb_ref, wt_ref, modp_ref, g_ref, mod_ref, w_ref, b_ref, xo_ref, o_ref,
                     wb_ref, hb_ref):
    _cast_once(w_ref, wb_ref)
    n = IN_PROJ_SUB_ROWS
    n_sub = x_ref.shape[0] // n

    def prepare(r):
        rows = slice(r * n, (r + 1) * n)
        x = _moe_mix(x_ref, ya_ref, yb_ref, wt_ref, modp_ref, rows)
        xo_ref[rows, :] = x
        hb_ref[r] = _modulated_norm(x, g_ref[...], mod_ref[...], 0, 1).astype(BF16)

    def project(r):
        rows = slice(r * n, (r + 1) * n)
        u = jnp.dot(hb_ref[r], wb_ref[...], preferred_element_type=F32) + b_ref[...]
        o_ref[rows, :] = u.astype(o_ref.dtype)

    prepare(0)
    for r in range(1, n_sub):
        prepare(r)
        project(r - 1)
    project(n_sub - 1)


def _in_proj1(x, moe_out, mod_prev, g, mod_l, w, bias, block_rows=512):
    ya, yb, w_tok = moe_out
    n = w.shape[1]
    tok = pl.BlockSpec((block_rows, D_MODEL), lambda i: (i, 0))
    packed = pl.BlockSpec((block_rows, ROW_WORDS), lambda i: (i, 0))
    return pl.pallas_call(
        _in_proj1_kernel,
        grid=(N_TOK // block_rows,),
        in_specs=[tok, packed, packed, pl.BlockSpec((block_rows, TOP_K), lambda i: (i, 0)), _mod_spec(block_rows),
                  _resident((1, D_MODEL)), _mod_spec(block_rows), _resident((D_MODEL, n)), _resident((1, n))],
        out_specs=(tok, pl.BlockSpec((block_rows, n), lambda i: (i, 0))),
        out_shape=(jax.ShapeDtypeStruct((N_TOK, D_MODEL), F32), jax.ShapeDtypeStruct((N_TOK, n), BF16)),
        scratch_shapes=[pltpu.VMEM((D_MODEL, n), BF16),
                        pltpu.VMEM((block_rows // IN_PROJ_SUB_ROWS, IN_PROJ_SUB_ROWS, D_MODEL), BF16)],
        compiler_params=_params("arbitrary"),
        name="in_proj1",
    )(x, ya, yb, w_tok, mod_prev, g.reshape(1, D_MODEL), mod_l, w, bias.reshape(1, n))


def _hgrn_kernel(*refs, seq_len, with_state):
    if with_state:
        (q_ref, zf_ref, zb_ref, i_ref, ga_ref, lb_ref, og_ref, s0_ref, o_ref, of_ref, ob_ref) = refs
    else:
        (q_ref, zf_ref, zb_ref, i_ref, ga_ref, lb_ref, og_ref, o_ref, s_ref, of_ref, ob_ref) = refs
    n_blocks = seq_len // HGRN_BLOCK
    chunks_per_block = HGRN_BLOCK // CHUNK

    lbr = lb_ref[...]
    mx = jnp.maximum(lbr[0], lbr[1])
    e0 = jnp.exp(lbr[0] - mx)
    e1 = jnp.exp(lbr[1] - mx)
    lb = e0 / (e0 + e1)

    row = lax.broadcasted_iota(jnp.int32, (HGRN_BLOCK, HGRN_BLOCK), 0)
    col = lax.broadcasted_iota(jnp.int32, (HGRN_BLOCK, HGRN_BLOCK), 1)
    same_chunk = (row // CHUNK) == (col // CHUNK)
    nt = (((1,), (1,)), ((), ()))
    tn = (((0,), (0,)), ((), ()))

    def per_chunk_row(x, idx):
        return jnp.concatenate(
            [jnp.broadcast_to(x[n * CHUNK + idx:n * CHUNK + idx + 1, :], (CHUNK, x.shape[1]))
             for n in range(chunks_per_block)], axis=0)

    def in_chunk_cumsum(tri, x):
        hi = x.astype(BF16)
        lo = (x - hi.astype(F32)).astype(BF16)
        return jnp.dot(tri, hi, preferred_element_type=F32) + jnp.dot(tri, lo, preferred_element_type=F32)

    def prepare(blk, cols, z_ref, lbd, forward):
        rows = slice(blk * HGRN_BLOCK, (blk + 1) * HGRN_BLOCK)
        keep = (same_chunk & (col <= row)) if forward else (same_chunk & (col >= row))
        tri = jnp.where(keep, 1.0, 0.0).astype(BF16)
        mid = CHUNK // 2 if forward else CHUNK - 1 - CHUNK // 2
        last = CHUNK - 1 if forward else 0
        f = lbd + (1.0 - lbd) * jax.nn.sigmoid(z_ref[rows, cols].astype(F32))
        lf = jnp.log(f)
        k = 1.0 - f
        q = q_ref[rows, cols].astype(F32)
        b = in_chunk_cumsum(tri, lf)
        bm = per_chunk_row(b, mid)
        bl = per_chunk_row(b, last)
        return dict(
            rows=rows, cols=cols, keep=keep, forward=forward,
            vb=i_ref[rows, cols].astype(BF16),
            qe=(q * jnp.exp(b - bm)).astype(BF16), ke=(k * jnp.exp(bm - b)).astype(BF16),
            qb=(q * jnp.exp(b)).astype(BF16), ks=(k * jnp.exp(bl - b)).astype(BF16), decay=jnp.exp(bl))

    def within_chunks(u):
        att = lax.dot_general(u["qe"], u["ke"], nt, preferred_element_type=F32)
        att = jnp.where(u["keep"], att, 0.0)
        u["o_intra"] = jnp.dot(att.astype(BF16), u["vb"], preferred_element_type=F32)
        u["upd"] = [lax.dot_general(u["vb"][n * CHUNK:(n + 1) * CHUNK], u["ks"][n * CHUNK:(n + 1) * CHUNK], tn,
                                    preferred_element_type=F32) for n in range(chunks_per_block)]

    def across_chunks(u, st, out_ref):
        order = range(chunks_per_block) if u["forward"] else range(chunks_per_block - 1, -1, -1)
        o_inter = [None] * chunks_per_block
        for n in order:
            cr = slice(n * CHUNK, (n + 1) * CHUNK)
            o_inter[n] = lax.dot_general(u["qb"][cr], st.astype(BF16), nt, preferred_element_type=F32)
            st = st * u["decay"][n * CHUNK:n * CHUNK + 1, :] + u["upd"][n]
        out_ref[u["rows"], u["cols"]] = u["o_intra"] + jnp.concatenate(o_inter, axis=0)
        return st

    n_heads = q_ref.shape[1] // A_DK
    head_cols = [slice(hd * A_DK, (hd + 1) * A_DK) for hd in range(n_heads)]
    if with_state:
        states = {(hd, d): s0_ref[d, hd].T for hd in range(n_heads) for d in range(2)}
    else:
        states = {(hd, d): jnp.zeros((A_DK, A_DK), F32) for hd in range(n_heads) for d in range(2)}
    for step in range(n_blocks):
        units = {}
        for hd, cols in enumerate(head_cols):
            units[hd, 0] = prepare(step, cols, zf_ref, lb[0:1, cols], True)
            units[hd, 1] = prepare(n_blocks - 1 - step, cols, zb_ref, lb[1:2, cols], False)
        for u in units.values():
            within_chunks(u)
        for key, u in units.items():
            states[key] = across_chunks(u, states[key], of_ref if key[1] == 0 else ob_ref)
    for hd, cols in enumerate(head_cols):
        if not with_state:
            s_ref[0, hd] = states[hd, 0].T
            s_ref[1, hd] = states[hd, 1].T
        o = of_ref[:, cols] + ob_ref[:, cols]
        o = o * lax.rsqrt(jnp.mean(o * o, axis=-1, keepdims=True) + EPS) * og_ref[:, cols]
        ga = ga_ref[:, cols].astype(F32)
        o_ref[:, cols] = (o * (ga * jax.nn.sigmoid(ga))).astype(o_ref.dtype)


def _hgrn(z, hgrn_lb, onorm_g, state, *, latent):
    seq_len = LATENT_LEN if latent else PROMPT_LEN
    n_seq = N_LATENT_SEQ if latent else N_PROMPT_SEQ
    row0 = (N_PROMPT_TOK // seq_len) if latent else 0

    hw = HGRN_HEADS_PER_STEP * A_DK
    n_hg = A_HEADS // HGRN_HEADS_PER_STEP

    def zspec(part):
        return pl.BlockSpec((seq_len, hw), lambda s, h: (row0 + s, part * n_hg + h))

    in_specs = [zspec(0), zspec(1), zspec(2), zspec(3), zspec(4),
                pl.BlockSpec((2, 2, hw), lambda s, h: (0, 0, h)),
                pl.BlockSpec((1, hw), lambda s, h: (0, h))]
    args = [z, z, z, z, z, hgrn_lb, onorm_g.reshape(1, A_WIDTH)]
    state_spec = pl.BlockSpec((None, None, 2, HGRN_HEADS_PER_STEP, A_DK, A_DK), lambda s, h: (s, 0, 0, h, 0, 0))
    o_shape = jax.ShapeDtypeStruct((n_seq * seq_len, A_WIDTH), BF16)
    o_spec = pl.BlockSpec((seq_len, hw), lambda s, h: (s, h))
    if latent:
        in_specs.append(state_spec)
        args.append(state)
        out_shape, out_specs = o_shape, o_spec
    else:
        out_shape = (o_shape, jax.ShapeDtypeStruct((n_seq, 1, 2, A_HEADS, A_DK, A_DK), F32))
        out_specs = (o_spec, state_spec)
    return pl.pallas_call(
        functools.partial(_hgrn_kernel, seq_len=seq_len, with_state=latent),
        grid=(n_seq, n_hg),
        in_specs=in_specs,
        out_specs=out_specs,
        out_shape=out_shape,
        scratch_shapes=[pltpu.VMEM((seq_len, hw), F32), pltpu.VMEM((seq_len, hw), F32)],
        compiler_params=_params("arbitrary", "arbitrary"),
        name="hgrn_latent" if latent else "hgrn_prompt",
    )(*args)


def _rope_tables():
    pos = np.arange(LATENT_LEN)
    row, colp = pos // GRID_W, pos % GRID_W
    inv = ROPE_THETA ** (-np.arange(ROPE_PAIRS, dtype=np.float32) / ROPE_PAIRS)
    inv = inv.astype(np.float32)
    ang_r = (row.astype(np.float32)[:, None] * inv).astype(np.float32)
    ang_c = (colp.astype(np.float32)[:, None] * inv).astype(np.float32)
    cos = np.concatenate([np.cos(ang_r), np.cos(ang_r), np.cos(ang_c), np.cos(ang_c)], axis=1)
    sin = np.concatenate([-np.sin(ang_r), np.sin(ang_r), -np.sin(ang_c), np.sin(ang_c)], axis=1)
    return cos.astype(np.float32), sin.astype(np.float32)


def _head_mean_matrix(width):
    idx = np.arange(width) // HEAD_DIM
    return jnp.asarray((idx[:, None] == idx[None, :]).astype(np.float32) / HEAD_DIM).astype(BF16)


def _attn_kernel(*refs, latent):
    if latent:
        (q_ref, k_ref, v_ref, qg_ref, kg_ref, gq_ref, gk_ref, cosq_ref, sinq_ref, cosk_ref, sink_ref,
         ck_ref, cv_ref, o_ref) = refs
    else:
        (q_ref, k_ref, v_ref, qg_ref, kg_ref, gq_ref, gk_ref, o_ref, kout_ref, vout_ref) = refs
    pair_w = 2 * HEAD_DIM

    def head_norm(x, mean_ref, gain):
        sq = x * x
        hi = sq.astype(BF16)
        lo = (sq - hi.astype(F32)).astype(BF16)
        ms = jnp.dot(hi, mean_ref[...], preferred_element_type=F32)
        ms = ms + jnp.dot(lo, mean_ref[...], preferred_element_type=F32)
        return x * lax.rsqrt(ms + EPS) * gain

    def rope(x, cos, sin):
        n = x.shape[1]
        lane = lax.broadcasted_iota(jnp.int32, x.shape, 1)
        first_of_pair = (lane // ROPE_PAIRS) % 2 == 0
        swapped = jnp.where(first_of_pair, pltpu.roll(x, n - ROPE_PAIRS, axis=1), pltpu.roll(x, ROPE_PAIRS, axis=1))
        return x * cos + swapped * sin

    nt = (((1,), (1,)), ((), ()))

    def prepare(rows, seq_idx):
        q = head_norm(q_ref[rows, :].astype(F32), gq_ref, qg_ref[...])
        k = head_norm(k_ref[rows, :].astype(F32), gk_ref, kg_ref[...])
        if latent:
            q = rope(q, cosq_ref[...], sinq_ref[...])
            k = rope(k, cosk_ref[...], sink_ref[...])
        q = q * (HEAD_DIM ** -0.5)
        v = v_ref[rows, :].astype(F32)
        n_q = q.shape[0]
        low_kv = lax.broadcasted_iota(jnp.int32, k.shape, 1) < HEAD_DIM
        low_q = lax.broadcasted_iota(jnp.int32, (n_q, pair_w), 1) < HEAD_DIM
        k_swapped = pltpu.roll(k, HEAD_DIM, axis=1)
        v_swapped = pltpu.roll(v, HEAD_DIM, axis=1)
        if not latent:
            kout_ref[seq_idx] = k.T
            vout_ref[seq_idx] = v.T
        units = []
        for j in range(KV_HEADS):
            kd = (jnp.where(low_kv, k, k_swapped) if j == 0 else jnp.where(low_kv, k_swapped, k)).astype(BF16)
            vd = (jnp.where(low_kv, v, v_swapped) if j == 0 else jnp.where(low_kv, v_swapped, v)).astype(BF16)
            vd = jnp.concatenate([vd, jnp.ones_like(vd)], axis=1)
            tiles = range(j * Q_PER_KV // 2, (j + 1) * Q_PER_KV // 2)
            parts = []
            for t in tiles:
                qt = q[:, t * pair_w:(t + 1) * pair_w]
                parts += [jnp.where(low_q, qt, 0.0), jnp.where(low_q, 0.0, qt)]
            units.append(dict(j=j, rows=rows, tiles=tiles, n_q=n_q, low_q=low_q, kd=kd, vd=vd,
                              qs=jnp.concatenate(parts, axis=0).astype(BF16)))
        return units

    def scores(u):
        u["s_new"] = lax.dot_general(u["qs"], u["kd"], nt, preferred_element_type=F32)
        if latent:
            j = u["j"]
            cvd = jnp.concatenate([cv_ref[j], cv_ref[j]], axis=1).astype(BF16)
            u["cvd"] = jnp.concatenate([cvd, jnp.ones_like(cvd)], axis=1)
            ckd = jnp.concatenate([ck_ref[j], ck_ref[j]], axis=1).astype(BF16)
            u["s_old"] = lax.dot_general(u["qs"], ckd, nt, preferred_element_type=F32)

    def softmax(u):
        m = jnp.max(u["s_new"], axis=-1, keepdims=True)
        if latent:
            m = jnp.maximum(m, jnp.max(u["s_old"], axis=-1, keepdims=True))
        u["p_new"] = jnp.exp(u.pop("s_new") - m).astype(BF16)
        if latent:
            u["p_old"] = jnp.exp(u.pop("s_old") - m).astype(BF16)

    def weighted_values(u):
        acc = jnp.dot(u["p_new"], u["vd"], preferred_element_type=F32)
        if latent:
            acc = acc + jnp.dot(u["p_old"], u["cvd"], preferred_element_type=F32)
        out = acc[:, :pair_w] / acc[:, pair_w:]
        n_q = u["n_q"]
        for i, t in enumerate(u["tiles"]):
            lo_head = out[(2 * i) * n_q:(2 * i + 1) * n_q, :]
            hi_head = out[(2 * i + 1) * n_q:(2 * i + 2) * n_q, :]
            o_ref[u["rows"], t * pair_w:(t + 1) * pair_w] = jnp.where(u["low_q"], lo_head, hi_head).astype(o_ref.dtype)

    if latent:
        units = prepare(slice(None), None)
    else:
        seq = PROMPT_LEN
        units = [u for s in range(q_ref.shape[0] // seq) for u in prepare(slice(s * seq, (s + 1) * seq), s)]
    for phase in (scores, softmax, weighted_values):
        for u in units:
            phase(u)


def _attn_common_args(qn_g, kn_g):
    q_w, kv_w = Q_HEADS * HEAD_DIM, KV_HEADS * HEAD_DIM
    return (jnp.tile(qn_g, Q_HEADS).reshape(1, q_w), jnp.tile(kn_g, KV_HEADS).reshape(1, kv_w),
            _head_mean_matrix(q_w), _head_mean_matrix(kv_w))


def _attention_prompt(z, qn_g, kn_g):
    seqs = 8
    L = seqs * PROMPT_LEN
    cache_shape = jax.ShapeDtypeStruct((N_PROMPT_SEQ, KV_HEADS * HEAD_DIM, PROMPT_LEN), F32)
    cache_spec = pl.BlockSpec((seqs, KV_HEADS * HEAD_DIM, PROMPT_LEN), lambda s: (s, 0, 0))
    q_w, kv_w = Q_HEADS * HEAD_DIM, KV_HEADS * HEAD_DIM
    q_col = (5 * A_WIDTH) // q_w
    k_col = (5 * A_WIDTH + q_w) // kv_w
    const = lambda r, c: pl.BlockSpec((r, c), lambda s: (0, 0))
    return pl.pallas_call(
        functools.partial(_attn_kernel, latent=False),
        grid=(N_PROMPT_TOK // L,),
        in_specs=[
            pl.BlockSpec((L, q_w), lambda s: (s, q_col)),
            pl.BlockSpec((L, kv_w), lambda s: (s, k_col)),
            pl.BlockSpec((L, kv_w), lambda s: (s, k_col + 1)),
            const(1, q_w), const(1, kv_w), const(q_w, q_w), const(kv_w, kv_w),
        ],
        out_specs=(pl.BlockSpec((L, q_w), lambda s: (s, 0)), cache_spec, cache_spec),
        out_shape=(jax.ShapeDtypeStruct((N_PROMPT_TOK, q_w), BF16), cache_shape, cache_shape),
        compiler_params=_params("arbitrary"),
        name="attn_prompt",
    )(z, z, z, *_attn_common_args(qn_g, kn_g))


def _attention_latent(z, qn_g, kn_g, cache_k, cache_v):
    L = LATENT_LEN
    nqb = L // Q_BLOCK
    q_w, kv_w = Q_HEADS * HEAD_DIM, KV_HEADS * HEAD_DIM
    q_col = (5 * A_WIDTH) // q_w
    k_col = (5 * A_WIDTH + q_w) // kv_w
    qrow0 = N_PROMPT_TOK // Q_BLOCK
    krow0 = N_PROMPT_TOK // L
    cos, sin = _rope_tables()
    cos_q, sin_q = jnp.asarray(np.tile(cos, (1, Q_HEADS))), jnp.asarray(np.tile(sin, (1, Q_HEADS)))
    cos_k, sin_k = jnp.asarray(np.tile(cos, (1, KV_HEADS))), jnp.asarray(np.tile(sin, (1, KV_HEADS)))
    const = lambda r, c: pl.BlockSpec((r, c), lambda s, b: (0, 0))
    cache_spec = pl.BlockSpec((None, None, KV_HEADS, PAST_LEN, HEAD_DIM), lambda s, b: (s, 0, 0, 0, 0))
    return pl.pallas_call(
        functools.partial(_attn_kernel, latent=True),
        grid=(N_LATENT_SEQ, nqb),
        in_specs=[
            pl.BlockSpec((Q_BLOCK, q_w), lambda s, b: (qrow0 + s * nqb + b, q_col)),
            pl.BlockSpec((L, kv_w), lambda s, b: (krow0 + s, k_col)),
            pl.BlockSpec((L, kv_w), lambda s, b: (krow0 + s, k_col + 1)),
            const(1, q_w), const(1, kv_w), const(q_w, q_w), const(kv_w, kv_w),
            pl.BlockSpec((Q_BLOCK, q_w), lambda s, b: (b, 0)),
            pl.BlockSpec((Q_BLOCK, q_w), lambda s, b: (b, 0)),
            const(L, kv_w), const(L, kv_w),
            cache_spec, cache_spec,
        ],
        out_specs=pl.BlockSpec((Q_BLOCK, q_w), lambda s, b: (s * nqb + b, 0)),
        out_shape=jax.ShapeDtypeStruct((N_LATENT_TOK, q_w), BF16),
        compiler_params=_params("arbitrary", "arbitrary"),
        name="attn_latent",
    )(z, z, z, *_attn_common_args(qn_g, kn_g), cos_q, sin_q, cos_k, sin_k, cache_k, cache_v)


def _out_proj_kernel(*refs, n_act, n_x):
    a_refs = refs[:2 * n_act]
    x_refs = refs[2 * n_act:2 * n_act + n_x]
    g_ref, mod_ref, rw_ref, w_ref, xo_ref, h_ref, lg_ref, wb_ref, rws_ref, acc_ref = refs[2 * n_act + n_x:]
    _cast_once(w_ref, wb_ref)

    @pl.when(pl.program_id(0) == 0)
    def _():
        rw = rw_ref[...]
        hi = rw.astype(BF16).astype(F32)
        lo = (rw - hi).astype(BF16).astype(F32)
        rws_ref[...] = (hi + pltpu.roll(lo, N_EXPERTS, axis=1)).astype(BF16)

    mod = mod_ref[...]
    n = OUT_PROJ_SUB_ROWS

    n_sub = xo_ref.shape[0] // n

    def sub_rows(r):
        if isinstance(r, int):
            return slice(r * n, (r + 1) * n)
        return pl.ds(pl.multiple_of(r * n, n), n)

    def project(r):
        rows = sub_rows(r)
        acc = None
        k0 = 0
        for ap_ref, al_ref in zip(a_refs[0::2], a_refs[1::2]):
            k1 = k0 + ap_ref.shape[1]
            part = jnp.dot(_select_trunk(ap_ref, al_ref, rows), wb_ref[k0:k1, :], preferred_element_type=F32)
            acc = part if acc is None else acc + part
            k0 = k1
        acc_ref[r % 2] = acc

    def finish(r):
        rows = sub_rows(r)
        x_in = x_refs[0][rows, :] if n_x == 1 else _select_trunk(*x_refs, rows)
        x = x_in + mod[2:3, :] * acc_ref[r % 2]
        xo_ref[rows, :] = x
        h = _modulated_norm(x, g_ref[...], mod, 3, 4)
        h_ref[rows, :] = _pack_rows(h)
        h_hi = h.astype(BF16)
        h_lo = (h - h_hi.astype(F32)).astype(BF16)
        both = jnp.dot(jnp.concatenate([h_hi, h_lo], axis=0), rws_ref[...], preferred_element_type=F32)
        from_hi, from_lo = both[:n], both[n:]
        lg = from_hi + pltpu.roll(from_hi, ROUTER_LANES - N_EXPERTS, axis=1) + from_lo
        lg_ref[:, rows] = lg.T[:N_EXPERTS, :]

    project(0)
    for r in range(n_sub - 1):
        project(r + 1)
        finish(r)
    finish(n_sub - 1)


def _out_proj(acts, w, xs, g, mod_l, router_wp, block_rows=1024):
    tok = lambda width: pl.BlockSpec((block_rows, width), lambda i: (i, 0))
    in_specs = [spec for ap, _ in acts for spec in _trunk_specs(block_rows, ap.shape[1])]
    in_specs += [tok(D_MODEL)] if len(xs) == 1 else list(_trunk_specs(block_rows, D_MODEL))
    in_specs += [_resident((1, D_MODEL)), _mod_spec(block_rows), _resident((D_MODEL, ROUTER_LANES)),
                 _resident(w.shape)]
    return pl.pallas_call(
        functools.partial(_out_proj_kernel, n_act=len(acts), n_x=len(xs)),
        grid=(N_TOK // block_rows,),
        in_specs=in_specs,
        out_specs=(tok(D_MODEL), tok(ROW_WORDS), pl.BlockSpec((N_EXPERTS, block_rows), lambda i: (0, i))),
        out_shape=(jax.ShapeDtypeStruct((N_TOK, D_MODEL), F32),
                   jax.ShapeDtypeStruct((N_TOK, ROW_WORDS), jnp.int32),
                   jax.ShapeDtypeStruct((N_EXPERTS, N_TOK), F32)),
        scratch_shapes=[pltpu.VMEM(w.shape, BF16), pltpu.VMEM((D_MODEL, ROUTER_LANES), BF16),
                        pltpu.VMEM((2, OUT_PROJ_SUB_ROWS, D_MODEL), F32)],
        compiler_params=_params("arbitrary"),
        name="out_proj",
    )(*[a for pair in acts for a in pair], *xs, g.reshape(1, D_MODEL), mod_l, router_wp, w)


def _router_kernel(lg_ref, rb_ref, pos_ref, w_ref, plan_ref, rank_ref):
    lg = lg_ref[...]
    ex = jnp.exp(lg - jnp.max(lg, axis=0, keepdims=True))
    scores = ex / jnp.sum(ex, axis=0, keepdims=True)
    biased = scores + rb_ref[...]
    expert = lax.broadcasted_iota(jnp.int32, biased.shape, 0)
    in_pos = expert % EXPERTS_PER_GROUP
    rank = jnp.zeros_like(biased)
    for d in range(1, EXPERTS_PER_GROUP):
        wraps = in_pos + d >= EXPERTS_PER_GROUP
        partner = jnp.where(wraps, pltpu.roll(biased, EXPERTS_PER_GROUP - d, axis=0),
                            pltpu.roll(biased, N_EXPERTS - d, axis=0))
        rank = rank + jnp.where(wraps, jnp.where(partner >= biased, 1.0, 0.0), jnp.where(partner > biased, 1.0, 0.0))
    selected = rank < 1.5
    contrib = jnp.where(selected, biased, 0.0)
    group_score = []
    for gi in range(N_GROUPS):
        r = [contrib[gi * EXPERTS_PER_GROUP + i:gi * EXPERTS_PER_GROUP + i + 1, :] for i in range(EXPERTS_PER_GROUP)]
        group_score.append(((r[0] + r[1]) + r[2]) + r[3])
    best = group_score[0]
    best_group = jnp.zeros_like(best)
    for gi in range(1, N_GROUPS):
        better = group_score[gi] > best
        best_group = jnp.where(better, float(gi), best_group)
        best = jnp.where(better, group_score[gi], best)
    in_group = (expert // EXPERTS_PER_GROUP).astype(F32) == best_group
    chosen = jnp.where(selected, jnp.where(in_group, 1.0, 0.0), 0.0)
    picked = chosen * scores
    gates = picked / jnp.sum(picked, axis=0, keepdims=True)
    lanes = 128
    n_blk = N_TOK // lanes
    li = lax.broadcasted_iota(jnp.int32, (lanes, lanes), 0)
    lj = lax.broadcasted_iota(jnp.int32, (lanes, lanes), 1)
    prefix = jnp.where(li <= lj, 1.0, 0.0).astype(BF16)
    stacked = jnp.concatenate([chosen[:, blk * lanes:(blk + 1) * lanes] for blk in range(n_blk)], axis=0)
    incl_all = jnp.dot(stacked.astype(BF16), prefix, preferred_element_type=F32)
    carry = jnp.zeros((N_EXPERTS, 1), F32)
    for blk in range(n_blk):
        cols = slice(blk * lanes, (blk + 1) * lanes)
        incl = incl_all[blk * N_EXPERTS:(blk + 1) * N_EXPERTS, :]
        rank_ref[:, cols] = incl - chosen[:, cols] + carry
        carry = carry + incl[:, lanes - 1:lanes]
    count = carry
    padded = jnp.floor((count + float(MOE_TILE - 1)) * (1.0 / MOE_TILE)) * float(MOE_TILE)
    erow = lax.broadcasted_iota(jnp.int32, (N_EXPERTS, 1), 0)
    offset = jnp.zeros((N_EXPERTS, 1), F32)
    for e in range(N_EXPERTS - 1):
        offset = offset + jnp.where(erow > e, padded[e:e + 1, :], 0.0)
    position = rank_ref[...] + offset
    ei = lax.broadcasted_iota(jnp.int32, (N_EXPERTS, N_EXPERTS), 0)
    ej = lax.broadcasted_iota(jnp.int32, (N_EXPERTS, N_EXPERTS), 1)
    lower = jnp.where(ej <= ei, 1.0, 0.0).astype(BF16)
    seen = jnp.dot(lower, chosen.astype(BF16), preferred_element_type=F32)
    first = (chosen > 0.5) & (seen < 1.5)
    second = (chosen > 0.5) & (seen > 1.5)
    pick = lambda flag, x: jnp.sum(jnp.where(flag, x, 0.0), axis=0, keepdims=True)
    pos_ref[0:1, :] = pick(first, position).astype(jnp.int32)
    pos_ref[1:2, :] = pick(second, position).astype(jnp.int32)
    w_rows = jnp.concatenate([pick(first, gates), pick(second, gates), jnp.zeros((6, N_TOK), F32)], axis=0)
    ti = lax.broadcasted_iota(jnp.int32, (8, lanes), 0)
    tj = lax.broadcasted_iota(jnp.int32, (8, lanes), 1)
    eye = jnp.where(ti == tj, 1.0, 0.0).astype(BF16)
    tn = (((0,), (0,)), ((), ()))
    hi = w_rows.astype(BF16)
    r1 = w_rows - hi.astype(F32)
    mid = r1.astype(BF16)
    lo = (r1 - mid.astype(F32)).astype(BF16)
    w_cols = lax.dot_general(hi, eye, tn, preferred_element_type=F32)
    w_cols = w_cols + lax.dot_general(mid, eye, tn, preferred_element_type=F32)
    w_cols = w_cols + lax.dot_general(lo, eye, tn, preferred_element_type=F32)
    w_ref[...] = w_cols[:, :TOP_K]
    start = (lax.broadcasted_iota(jnp.int32, (N_EXPERTS, lanes), 1) * MOE_TILE).astype(F32)
    end = offset + padded
    tile_expert = jnp.sum(jnp.where(end <= start, 1.0, 0.0), axis=0, keepdims=True)
    inside = (offset <= start) & (start < end)
    real = jnp.clip(count - (start - offset), 0.0, float(MOE_TILE))
    tile_rows = jnp.sum(jnp.where(inside, real, 0.0), axis=0, keepdims=True)
    plan_ref[0:1, :] = jnp.minimum(tile_expert, float(N_EXPERTS - 1)).astype(jnp.int32)
    plan_ref[1:2, :] = tile_rows.astype(jnp.int32)


def _router(logits_t, router_b):
    whole = lambda shape: pl.BlockSpec(shape, lambda i: (0, 0))
    return pl.pallas_call(
        _router_kernel,
        grid=(1,),
        in_specs=[whole((N_EXPERTS, N_TOK)), whole((N_EXPERTS, 1))],
        out_specs=(whole((2, N_TOK)), whole((N_TOK, TOP_K)), whole((2, 128))),
        out_shape=(jax.ShapeDtypeStruct((2, N_TOK), jnp.int32),
                   jax.ShapeDtypeStruct((N_TOK, TOP_K), F32),
                   jax.ShapeDtypeStruct((2, 128), jnp.int32)),
        scratch_shapes=[pltpu.VMEM((N_EXPERTS, N_TOK), F32)],
        compiler_params=_params("arbitrary"),
        name="router",
    )(logits_t, router_b.reshape(N_EXPERTS, 1))


def _sc_mesh():
    return plsc.VectorSubcoreMesh(core_axis_name="c", subcore_axis_name="s")


def _sc_worker_base():
    return (lax.axis_index("s") * SC_CORES + lax.axis_index("c")) * SC_TOKENS_PER_WORKER


def _moe_dispatch(h, pos_a, pos_b):
    n_chunks = SC_TOKENS_PER_WORKER // SC_CHUNK
    idx = pltpu.VMEM((SC_CHUNK,), jnp.int32)

    @functools.partial(
        pl.kernel, mesh=_sc_mesh(),
        out_type=jax.ShapeDtypeStruct((MOE_ROWS, ROW_WORDS), jnp.int32),
        scratch_types=[idx, idx, idx, idx, pltpu.VMEM((2, SC_CHUNK, ROW_WORDS), jnp.int32),
                       pltpu.SemaphoreType.DMA((6,)), pltpu.SemaphoreType.DMA((4,))],
        name="moe_dispatch",
    )
    def run(h_hbm, pa_hbm, pb_hbm, xs_hbm, ia0, ib0, ia1, ib1, rows_v, sem_in, sem_out):
        base = _sc_worker_base()
        ia, ib = (ia0, ia1), (ib0, ib1)

        def start_loads(c):
            slot = c % 2
            tok = pl.ds(pl.multiple_of(base + c * SC_CHUNK, 8), SC_CHUNK)
            return (pltpu.async_copy(pa_hbm.at[tok], ia[slot], sem_in.at[3 * slot]),
                    pltpu.async_copy(pb_hbm.at[tok], ib[slot], sem_in.at[3 * slot + 1]),
                    pltpu.async_copy(h_hbm.at[tok], rows_v.at[slot], sem_in.at[3 * slot + 2]))

        loads = start_loads(0)
        scatters = [(), ()]
        for c in range(n_chunks):
            slot = c % 2
            for cp in loads:
                cp.wait()
            if c + 1 < n_chunks:
                for cp in scatters[1 - slot]:
                    cp.wait()
                scatters[1 - slot] = ()
                loads = start_loads(c + 1)
            scatters[slot] = (pltpu.async_copy(rows_v.at[slot], xs_hbm.at[ia[slot]], sem_out.at[2 * slot]),
                              pltpu.async_copy(rows_v.at[slot], xs_hbm.at[ib[slot]], sem_out.at[2 * slot + 1]))
        for pending in scatters:
            for cp in pending:
                cp.wait()

    return run(h, pos_a, pos_b)


def _moe_collect(ys, pos_a, pos_b, tok0=0, n_tok=N_TOK):
    per_worker = n_tok // SC_WORKERS
    chunk = SC_CHUNK if per_worker % SC_CHUNK == 0 else 32
    n_chunks = per_worker // chunk
    out = jax.ShapeDtypeStruct((n_tok, ROW_WORDS), jnp.int32)
    idx = pltpu.VMEM((per_worker,), jnp.int32)
    rows = pltpu.VMEM((2, chunk, ROW_WORDS), jnp.int32)

    @functools.partial(
        pl.kernel, mesh=_sc_mesh(), out_type=(out, out),
        scratch_types=[idx, idx, rows, rows, pltpu.SemaphoreType.DMA((4,)), pltpu.SemaphoreType.DMA((4,))],
        name="moe_collect",
    )
    def run(ys_hbm, pa_hbm, pb_hbm, ya_hbm, yb_hbm, ia_v, ib_v, ra_v, rb_v, sem_g, sem_w):
        base = (lax.axis_index("s") * SC_CORES + lax.axis_index("c")) * per_worker
        mine = pl.ds(pl.multiple_of(tok0 + base, 8), per_worker)
        pltpu.sync_copy(pa_hbm.at[mine], ia_v)
        pltpu.sync_copy(pb_hbm.at[mine], ib_v)
        writes = [(), ()]
        for c in range(n_chunks):
            slot = c % 2
            for cp in writes[slot]:
                cp.wait()
            part = pl.ds(c * chunk, chunk)
            tok = pl.ds(pl.multiple_of(base + c * chunk, 8), chunk)
            ga = pltpu.async_copy(ys_hbm.at[ia_v.at[part]], ra_v.at[slot], sem_g.at[slot])
            gb = pltpu.async_copy(ys_hbm.at[ib_v.at[part]], rb_v.at[slot], sem_g.at[2 + slot])
            ga.wait()
            wa = pltpu.async_copy(ra_v.at[slot], ya_hbm.at[tok], sem_w.at[slot])
            gb.wait()
            wb = pltpu.async_copy(rb_v.at[slot], yb_hbm.at[tok], sem_w.at[2 + slot])
            writes[slot] = (wa, wb)
        for pending in writes:
            for cp in pending:
                cp.wait()

    return run(ys, pos_a, pos_b)


def _experts_kernel(plan_ref, xs_ref, wg_hbm, wu_hbm, wd_hbm, y_ref,
                    sg_ref, su_ref, sd_ref, wgb_ref, wub_ref, wdb_ref, hid_ref, sems, seg_ref, *, layer):
    n_tiles = pl.num_programs(0) * EXPERT_TILES_PER_STEP

    def weight_copies(e, slot):
        return (pltpu.make_async_copy(wg_hbm.at[layer, e], sg_ref.at[slot], sems.at[slot, 0]),
                pltpu.make_async_copy(wu_hbm.at[layer, e], su_ref.at[slot], sems.at[slot, 1]),
                pltpu.make_async_copy(wd_hbm.at[layer, e], sd_ref.at[slot], sems.at[slot, 2]))

    def tile(t, row0):
        expert = plan_ref[t]
        n_real = plan_ref[PLAN_LANES + t]
        fresh = jnp.logical_or(t == 0, expert != plan_ref[jnp.maximum(t - 1, 0)])

        @pl.when(t == 0)
        def _():
            seg_ref[0] = 0

            @pl.when(n_real > 0)
            def _():
                for cp in weight_copies(expert, 0):
                    cp.start()

        @pl.when(jnp.logical_and(n_real > 0, fresh))
        def _():
            slot = seg_ref[0] % 2
            for cp in weight_copies(expert, slot):
                cp.wait()
            wgb_ref[...] = sg_ref[slot].astype(BF16)
            wub_ref[...] = su_ref[slot].astype(BF16)
            wdb_ref[...] = sd_ref[slot].astype(BF16)
            nxt = lax.while_loop(
                lambda u: jnp.logical_and(u < n_tiles, plan_ref[jnp.minimum(u, n_tiles - 1)] == expert),
                lambda u: u + 1, t + 1)
            nxt_c = jnp.minimum(nxt, n_tiles - 1)

            @pl.when(jnp.logical_and(nxt < n_tiles, plan_ref[PLAN_LANES + nxt_c] > 0))
            def _():
                for cp in weight_copies(plan_ref[nxt_c], 1 - slot):
                    cp.start()

            seg_ref[0] = seg_ref[0] + 1

        @pl.when(n_real > 0)
        def _():
            n = EXPERT_SUB_ROWS
            n_sub = MOE_TILE // n
            row = lax.broadcasted_iota(jnp.int32, (n, xs_ref.shape[1]), 0)

            def up(r):
                rows = slice(row0 + r * n, row0 + (r + 1) * n)
                words = jnp.where(row < n_real - r * n, xs_ref[rows, :], 0)
                x = _unpack_rows(words).astype(BF16)
                a = jnp.dot(x, wgb_ref[...], preferred_element_type=F32)
                b = jnp.dot(x, wub_ref[...], preferred_element_type=F32)
                hid_ref[r] = ((a * jax.nn.sigmoid(a)) * b).astype(BF16)

            def down(r):
                rows = slice(row0 + r * n, row0 + (r + 1) * n)
                y_ref[rows, :] = _pack_rows(jnp.dot(hid_ref[r], wdb_ref[...], preferred_element_type=F32))

            up(0)
            for r in range(1, n_sub):
                up(r)
                down(r - 1)
            down(n_sub - 1)

    for q in range(EXPERT_TILES_PER_STEP):
        tile(pl.program_id(0) * EXPERT_TILES_PER_STEP + q, q * MOE_TILE)


def _experts(plan, xs, w_gate, w_up, w_down, layer):
    hbm = pl.BlockSpec(memory_space=pl.ANY)
    step_rows = MOE_TILE * EXPERT_TILES_PER_STEP
    return pl.pallas_call(
        functools.partial(_experts_kernel, layer=layer),
        grid_spec=pltpu.PrefetchScalarGridSpec(
            num_scalar_prefetch=1,
            grid=(MOE_ROWS // step_rows,),
            in_specs=[pl.BlockSpec((step_rows, ROW_WORDS), lambda j, plan: (j, 0)), hbm, hbm, hbm],
            out_specs=pl.BlockSpec((step_rows, ROW_WORDS), lambda j, plan: (j, 0)),
            scratch_shapes=[pltpu.VMEM((2, D_MODEL, D_EXPERT), F32), pltpu.VMEM((2, D_MODEL, D_EXPERT), F32),
                            pltpu.VMEM((2, D_EXPERT, D_MODEL), F32),
                            pltpu.VMEM((D_MODEL, D_EXPERT), BF16), pltpu.VMEM((D_MODEL, D_EXPERT), BF16),
                            pltpu.VMEM((D_EXPERT, D_MODEL), BF16),
                            pltpu.VMEM((MOE_TILE // EXPERT_SUB_ROWS, EXPERT_SUB_ROWS, D_EXPERT), BF16),
                            pltpu.SemaphoreType.DMA((2, 3)), pltpu.SMEM((1,), jnp.int32)],
        ),
        out_shape=jax.ShapeDtypeStruct((MOE_ROWS, ROW_WORDS), jnp.int32),
        compiler_params=_params("arbitrary"),
        name="experts",
    )(plan, xs, w_gate, w_up, w_down)


def _combine_kernel(x_ref, ya_ref, yb_ref, wt_ref, mod_ref, o_ref):
    o_ref[...] = _moe_mix(x_ref, ya_ref, yb_ref, wt_ref, mod_ref)


def _combine(x, moe_out, mod_l, tok0, n_tok, block_rows=1024):
    ya, yb, w_tok = moe_out
    b0 = tok0 // block_rows
    rows = lambda width: pl.BlockSpec((block_rows, width), lambda i: (b0 + i, 0))
    local = pl.BlockSpec((block_rows, ROW_WORDS), lambda i: (i, 0))
    return pl.pallas_call(
        _combine_kernel,
        grid=(n_tok // block_rows,),
        in_specs=[rows(D_MODEL), local, local, rows(TOP_K),
                  pl.BlockSpec((None, 6, D_MODEL), lambda i: (_cond_of_token_block(b0 + i, block_rows), 0, 0))],
        out_specs=pl.BlockSpec((block_rows, D_MODEL), lambda i: (i, 0)),
        out_shape=jax.ShapeDtypeStruct((n_tok, D_MODEL), F32),
        compiler_params=_params("arbitrary"),
        name="combine",
    )(x, ya, yb, w_tok, mod_l)


def _moe(h, logits_t, router_b, w_gate, w_up, w_down, layer, ranges=((0, N_TOK),)):
    pos, w, plan = _router(logits_t, router_b)
    xs = _moe_dispatch(h, pos[0], pos[1])
    ys = _experts(plan.reshape(-1), xs, w_gate, w_up, w_down, layer)
    return [(*_moe_collect(ys, pos[0], pos[1], tok0, n_tok), w) for tok0, n_tok in ranges]


def _dft_tables(L):
    k = np.arange(L)[:, None]
    m = np.arange(L)[None, :]
    r = (k * m) % (2 * L)
    ang = np.pi * r.astype(np.float64) / L
    fc = np.cos(ang)
    fs = np.sin(ang)
    fs[0, :] = np.where(np.arange(L) % 2 == 0, 1.0, -1.0)
    wk = np.full((L, 1), 1.0 / L)
    wk[0, 0] = 0.5 / L
    gc = (fc * wk).T
    gs = (fs * wk).T
    return [jnp.asarray(t.astype(np.float32)).astype(BF16) for t in (fc, fs, gc, gs)]


def _filter_consts(L):
    t = np.linspace(0.0, 1.0, L, dtype=np.float32)[:, None]
    w = (np.float32(2.0 * np.pi) * np.arange(L, dtype=np.float32)[:, None] / np.float32(L)).astype(np.float32)
    fb = np.linspace(1e-4, HY_BANDS - 1, HY_BANDS, dtype=np.float32)[None, :]
    emb = np.concatenate([t, np.cos(fb * w), -np.sin(fb * w)], axis=-1).astype(np.float32)
    lo = math.log(HY_DECAY_TARGET) / HY_SLOW_PCT
    hi = math.log(HY_DECAY_TARGET) / HY_FAST_PCT
    deltas = np.abs(np.linspace(lo, hi, D_MODEL, dtype=np.float32))
    decay = np.exp(-t * deltas).astype(np.float32)
    return jnp.asarray(emb), jnp.asarray(decay)


def _filter_kernel(emb_ref, w1_ref, b1_ref, w2_ref, b2_ref, fr_ref, w3f_ref, w3b_ref, dec_ref,
                   fc_ref, fs_ref, kr_ref, q_ref, krn_ref, hd_ref):
    @pl.when(pl.program_id(0) == 0)
    def _():
        fr = fr_ref[...]
        h1 = jnp.sin(fr * (jnp.dot(emb_ref[...], w1_ref[...], precision=HIGHEST,
                                   preferred_element_type=F32) + b1_ref[...]))
        hd_ref[...] = jnp.sin(fr * (jnp.dot(h1, w2_ref[...], precision=HIGHEST,
                                            preferred_element_type=F32) + b2_ref[...]))

    hd = hd_ref[...]
    dec = dec_ref[...]
    f = jnp.dot(hd, w3f_ref[...], precision=HIGHEST, preferred_element_type=F32) * dec
    g = jnp.dot(hd, w3b_ref[...], precision=HIGHEST, preferred_element_type=F32) * dec
    row = lax.broadcasted_iota(jnp.int32, f.shape, 0)
    g = jnp.where(row == 0, 0.0, g)
    s = f + g
    d = f - g
    kr = jnp.dot(fc_ref[...], s.astype(BF16), preferred_element_type=F32)
    qq = jnp.dot(fs_ref[...], d.astype(BF16), preferred_element_type=F32)
    alt = jnp.where(row % 2 == 0, 1.0, -1.0)
    nyq = jnp.sum(alt * s, axis=0, keepdims=True)
    kr_ref[...] = kr
    q_ref[...] = jnp.where(row == 0, 0.0, qq)
    krn_ref[...] = jnp.where(row == 0, nyq, kr)


def _hyena_filter_spectrum(L, w1, b1, w2, b2, w3, freq, fc, fs, cblk=256):
    emb, decay = _filter_consts(L)
    ncb = D_MODEL // cblk
    n_emb = 128
    emb = jnp.pad(emb, ((0, 0), (0, n_emb - emb.shape[1])))
    w1 = jnp.pad(w1, ((0, n_emb - w1.shape[0]), (0, 0)))
    full = lambda shape: pl.BlockSpec(shape, lambda j: tuple(0 for _ in shape))
    out_sds = jax.ShapeDtypeStruct((L, D_MODEL), F32)
    out_spec = pl.BlockSpec((L, cblk), lambda j: (0, j))
    return pl.pallas_call(
        _filter_kernel,
        grid=(ncb,),
        in_specs=[
            full((L, n_emb)), full((n_emb, HY_FFN)), full((1, HY_FFN)), full((HY_FFN, HY_FFN)),
            full((1, HY_FFN)), full((1, HY_FFN)),
            pl.BlockSpec((HY_FFN, cblk), lambda j: (0, j)),
            pl.BlockSpec((HY_FFN, cblk), lambda j: (0, ncb + j)),
            pl.BlockSpec((L, cblk), lambda j: (0, j)),
            full((L, L)), full((L, L)),
        ],
        out_specs=(out_spec, out_spec, out_spec),
        out_shape=(out_sds, out_sds, out_sds),
        scratch_shapes=[pltpu.VMEM((L, HY_FFN), F32)],
        compiler_params=_params("arbitrary"),
        name=f"hyena_filter_{L}",
    )(emb, w1, b1.reshape(1, HY_FFN), w2, b2.reshape(1, HY_FFN), freq.reshape(1, HY_FFN), w3, w3, decay, fc, fs)


def _hyena_conv_kernel(x0_ref, x1_ref, v_ref, cw0_ref, cw1_ref, cwv_ref, cb0_ref, cb1_ref, cbv_ref,
                       kr_ref, q_ref, krn_ref, ds_ref, fc_ref, fs_ref, gc_ref, gs_ref, o_ref,
                       zz_ref, gate_ref, skip_ref, yr_ref, yw_ref):
    L = fc_ref.shape[0]
    unit_w = zz_ref.shape[2]
    units = [(slice(s * L, (s + 1) * L), slice(c * unit_w, (c + 1) * unit_w))
             for s in range(x0_ref.shape[0] // L) for c in range(x0_ref.shape[1] // unit_w)]
    row = lax.broadcasted_iota(jnp.int32, (L, unit_w), 0)

    def gating(i):
        rows, cols = units[i]

        def short_conv(u_ref, w_ref, b_ref):
            u = u_ref[rows, cols].astype(F32)
            w = w_ref[:, cols]
            prev = jnp.where(row == 0, 0.0, pltpu.roll(u, 1, axis=0))
            nxt = jnp.where(row == L - 1, 0.0, pltpu.roll(u, L - 1, axis=0))
            return prev * w[0:1, :] + u * w[1:2, :] + nxt * w[2:3, :] + b_ref[:, cols]

        x0 = short_conv(x0_ref, cw0_ref, cb0_ref)
        zz = short_conv(v_ref, cwv_ref, cbv_ref) * short_conv(x1_ref, cw1_ref, cb1_ref)
        zz_ref[i] = zz.astype(BF16)
        gate_ref[i] = x0
        skip_ref[i] = x0 * zz * ds_ref[:, cols]

    def spectrum(i):
        cols = units[i][1]
        ur = jnp.dot(fc_ref[...], zz_ref[i], preferred_element_type=F32)
        p = jnp.dot(fs_ref[...], zz_ref[i], preferred_element_type=F32)
        qq = q_ref[:, cols]
        yr_ref[i] = (ur * kr_ref[:, cols] - p * qq).astype(BF16)
        yw_ref[i] = (ur * qq + p * krn_ref[:, cols]).astype(BF16)

    def synthesis(i):
        rows, cols = units[i]
        y = jnp.dot(gc_ref[...], yr_ref[i], preferred_element_type=F32)
        y = y + jnp.dot(gs_ref[...], yw_ref[i], preferred_element_type=F32)
        o_ref[rows, cols] = (gate_ref[i] * y + skip_ref[i]).astype(o_ref.dtype)

    for t in range(len(units) + 2):
        if t < len(units):
            gating(t)
        if 0 <= t - 1 < len(units):
            spectrum(t - 1)
        if 0 <= t - 2 < len(units):
            synthesis(t - 2)


def _hyena_conv(u, conv_w, conv_b, dskip, spectrum, tables, *, latent):
    L = LATENT_LEN if latent else PROMPT_LEN
    n_seq = N_LATENT_SEQ if latent else N_PROMPT_SEQ
    cblk = 512
    unit_w = 256 if latent else 512
    ncb = D_MODEL // cblk
    seqs = 1 if latent else 8
    unit = (seqs * cblk // unit_w, L, unit_w)
    row0 = (N_PROMPT_TOK // L) if latent else 0
    kr, qq, krn = spectrum
    fc, fs, gc, gs = tables

    def part(p, rows):
        if rows != L:
            return pl.BlockSpec((rows, cblk), lambda j, s: (0, p * ncb + j))
        return pl.BlockSpec((seqs * L, cblk), lambda j, s: (row0 // seqs + s, p * ncb + j))

    def const_cols(rows):
        return pl.BlockSpec((rows, cblk), lambda j, s: (0, j))

    mat = pl.BlockSpec((L, L), lambda j, s: (0, 0))
    conv_b2 = conv_b.reshape(1, 3 * D_MODEL)
    in_specs = [part(0, L), part(1, L), part(2, L),
                part(0, 3), part(1, 3), part(2, 3),
                part(0, 1), part(1, 1), part(2, 1),
                const_cols(L), const_cols(L), const_cols(L), const_cols(1),
                mat, mat, mat, mat]
    args = [u, u, u, conv_w, conv_w, conv_w, conv_b2, conv_b2, conv_b2,
            kr, qq, krn, dskip.reshape(1, D_MODEL), fc, fs, gc, gs]
    return pl.pallas_call(
        _hyena_conv_kernel,
        grid=(ncb, n_seq // seqs),
        in_specs=in_specs,
        out_specs=pl.BlockSpec((seqs * L, cblk), lambda j, s: (s, j)),
        out_shape=jax.ShapeDtypeStruct((n_seq * L, D_MODEL), BF16),
        scratch_shapes=[pltpu.VMEM(unit, BF16), pltpu.VMEM(unit, F32), pltpu.VMEM(unit, F32),
                        pltpu.VMEM(unit, BF16), pltpu.VMEM(unit, BF16)],
        compiler_params=_params("arbitrary", "arbitrary"),
        name="hyena_conv_latent" if latent else "hyena_conv_prompt",
    )(*args)


def kernel(x_prompt, x_sample, cache_k, cache_v, state_hgrn, c, c_ctx, norm_g, mod_w, mod_b, ab_in_w, hgrn_lb, hgrn_onorm_g, attn_qnorm_g, attn_knorm_g, ab_out_w, hy_in_w, hy_in_b, hy_conv_w, hy_conv_b, hy_f_w1, hy_f_b1, hy_f_w2, hy_f_b2, hy_f_w3, hy_f_freq, hy_dskip, hy_out_w, router_w, router_b, moe_w_gate, moe_w_up, moe_w_down):
    xp = x_prompt.reshape(N_PROMPT_TOK, D_MODEL)
    xl = x_sample.reshape(N_LATENT_TOK, D_MODEL)
    cond = jnp.concatenate([c_ctx[None, :], c, jnp.zeros((N_COND - 1 - N_LATENT_SEQ, D_MODEL), F32)], axis=0)
    mod = _modulation(cond, mod_w, mod_b)
    router_wp = jnp.pad(router_w, ((0, 0), (0, ROUTER_LANES - N_EXPERTS)))

    z = _in_proj0(xp, xl, norm_g[0, 0], mod[0], ab_in_w[0])
    oa_p, new_state = _hgrn(z, hgrn_lb, hgrn_onorm_g[0], None, latent=False)
    oa_l = _hgrn(z, hgrn_lb, hgrn_onorm_g[0], state_hgrn, latent=True)
    ob_p, k_fm, v_fm = _attention_prompt(z, attn_qnorm_g[0], attn_knorm_g[0])
    fm_shape = (N_PROMPT_SEQ, 1, KV_HEADS, HEAD_DIM, PROMPT_LEN)
    new_k = jnp.swapaxes(k_fm.reshape(fm_shape), -1, -2)
    new_v = jnp.swapaxes(v_fm.reshape(fm_shape), -1, -2)
    ob_l = _attention_latent(z, attn_qnorm_g[0], attn_knorm_g[0], cache_k, cache_v)
    x, h, logits_t = _out_proj([(oa_p, oa_l), (ob_p, ob_l)], ab_out_w[0], (xp, xl), norm_g[0, 1], mod[0],
                               router_wp)
    (moe_out,) = _moe(h, logits_t, router_b, moe_w_gate, moe_w_up, moe_w_down, 0)

    x, u = _in_proj1(x, moe_out, mod[0], norm_g[1, 0], mod[1], hy_in_w[0], hy_in_b[0])
    pre = []
    for latent in (False, True):
        L = LATENT_LEN if latent else PROMPT_LEN
        tables = _dft_tables(L)
        spectrum = _hyena_filter_spectrum(L, hy_f_w1[0], hy_f_b1[0], hy_f_w2[0], hy_f_b2[0], hy_f_w3[0],
                                          hy_f_freq[0], tables[0], tables[1])
        pre.append(_hyena_conv(u, hy_conv_w[0], hy_conv_b[0], hy_dskip[0], spectrum, tables, latent=latent))
    x, h, logits_t = _out_proj([tuple(pre)], hy_out_w[0], (x,), norm_g[1, 1], mod[1], router_wp)
    trunks = ((0, N_PROMPT_TOK), (N_PROMPT_TOK, N_LATENT_TOK))
    out_p, out_l = _moe(h, logits_t, router_b, moe_w_gate, moe_w_up, moe_w_down, 1, ranges=trunks)

    y_prompt = _combine(x, out_p, mod[1], *trunks[0]).reshape(N_PROMPT_SEQ, PROMPT_LEN, D_MODEL)
    y_sample = _combine(x, out_l, mod[1], *trunks[1]).reshape(N_LATENT_SEQ, LATENT_LEN, D_MODEL)
    return (y_prompt, y_sample, new_k, new_v, new_state)
```

```python
import functools
import math

import numpy as np
import jax
import jax.numpy as jnp
from jax import lax
from jax.experimental import pallas as pl
from jax.experimental.pallas import tpu as pltpu
from jax.experimental.pallas import tpu_sc as plsc

F32 = jnp.float32
BF16 = jnp.bfloat16
HIGHEST = lax.Precision.HIGHEST

D_MODEL = 1024
N_PROMPT_SEQ = 32
PROMPT_LEN = 256
N_LATENT_SEQ = 2
LATENT_LEN = 1024
PAST_LEN = 512
GRID_W = 64
N_PROMPT_TOK = N_PROMPT_SEQ * PROMPT_LEN
N_LATENT_TOK = N_LATENT_SEQ * LATENT_LEN
N_TOK = N_PROMPT_TOK + N_LATENT_TOK
N_COND = 8
EPS = 1e-6

A_WIDTH = 512
A_HEADS = 4
A_DK = 128
CHUNK = 64
HGRN_BLOCK = 128
HGRN_HEADS_PER_STEP = 4
HEAD_DIM = 64
Q_HEADS = 8
KV_HEADS = 2
Q_PER_KV = Q_HEADS // KV_HEADS
Q_BLOCK = 256
ROPE_THETA = 10000.0
ROPE_PAIRS = HEAD_DIM // 4

HY_BANDS = 16
HY_FFN = 64
HY_DECAY_TARGET = 1e-2
HY_FAST_PCT = 0.3
HY_SLOW_PCT = 1.5

N_EXPERTS = 16
N_GROUPS = 4
EXPERTS_PER_GROUP = 4
TOP_K = 2
D_EXPERT = 512
ROUTER_LANES = 128
OUT_PROJ_SUB_ROWS = 256
EXPERT_SUB_ROWS = 256
EXPERT_TILES_PER_STEP = 2
IN_PROJ_SUB_ROWS = 128
MOE_TILE = 512
MOE_ROWS = N_TOK * TOP_K + N_EXPERTS * MOE_TILE
PLAN_LANES = 128

SC_CORES = 2
SC_WORKERS = 32
SC_TOKENS_PER_WORKER = N_TOK // SC_WORKERS
SC_CHUNK = 40
ROW_WORDS = D_MODEL // 2

VMEM_LIMIT = 56 * 1024 * 1024


def _params(*sem):
    return pltpu.CompilerParams(dimension_semantics=sem, vmem_limit_bytes=VMEM_LIMIT)


def _pack_rows(x):
    n = x.shape[1] // 2
    bits = pltpu.bitcast(x.astype(BF16).astype(F32), jnp.uint32)
    return pltpu.bitcast(bits[:, :n] | (bits[:, n:] >> 16), jnp.int32)


def _unpack_rows(p):
    bits = pltpu.bitcast(p, jnp.uint32)
    hi = pltpu.bitcast(bits & jnp.uint32(0xFFFF0000), F32)
    lo = pltpu.bitcast(bits << 16, F32)
    return jnp.concatenate([hi, lo], axis=1)


def _cond_of_token_block(i, block_rows):
    start = i * block_rows
    return jnp.where(start < N_PROMPT_TOK, 0, 1 + (start - N_PROMPT_TOK) // LATENT_LEN)


def _mod_kernel(cond_ref, w_ref, b_ref, o_ref):
    cnd = cond_ref[...]
    s = cnd * jax.nn.sigmoid(cnd)
    s_hi = s.astype(BF16)
    s_lo = (s - s_hi.astype(F32)).astype(BF16)
    w = w_ref[...]
    w_hi = w.astype(BF16)
    w_lo = (w - w_hi.astype(F32)).astype(BF16)
    acc = jnp.dot(s_hi, w_hi, preferred_element_type=F32)
    acc = acc + jnp.dot(s_lo, w_hi, preferred_element_type=F32)
    acc = acc + jnp.dot(s_hi, w_lo, preferred_element_type=F32)
    o_ref[...] = acc + b_ref[...]


def _modulation(cond, mod_w, mod_b):
    depth = mod_w.shape[0]
    n_mod = 6
    cols = 2 * D_MODEL
    n_step = n_mod * D_MODEL // cols
    out = pl.pallas_call(
        _mod_kernel,
        grid=(depth, n_step),
        in_specs=[
            pl.BlockSpec((N_COND, D_MODEL), lambda l, j: (0, 0)),
            pl.BlockSpec((None, D_MODEL, cols), lambda l, j: (l, 0, j)),
            pl.BlockSpec((None, 1, cols), lambda l, j: (l, 0, j)),
        ],
        out_specs=pl.BlockSpec((None, N_COND, cols), lambda l, j: (l, 0, j)),
        out_shape=jax.ShapeDtypeStruct((depth, N_COND, n_mod * D_MODEL), F32),
        compiler_params=_params("arbitrary", "arbitrary"),
        name="modulation",
    )(cond, mod_w, mod_b.reshape(depth, 1, n_mod * D_MODEL))
    return out.reshape(depth, N_COND, n_mod, D_MODEL)


def _modulated_norm(x, g, mod, shift_row, scale_row):
    ms = jnp.mean(x * x, axis=-1, keepdims=True)
    y = x * lax.rsqrt(ms + EPS) * g
    return y * (1.0 + mod[scale_row:scale_row + 1, :]) + mod[shift_row:shift_row + 1, :]


def _trunk_specs(block_rows, width):
    n_prompt_blocks = N_PROMPT_TOK // block_rows
    return (pl.BlockSpec((block_rows, width), lambda i: (jnp.minimum(i, n_prompt_blocks - 1), 0)),
            pl.BlockSpec((block_rows, width), lambda i: (jnp.maximum(i - n_prompt_blocks, 0), 0)))


def _select_trunk(p_ref, l_ref, rows=slice(None)):
    block_rows = p_ref.shape[0]
    return jnp.where(pl.program_id(0) < N_PROMPT_TOK // block_rows, p_ref[rows, :], l_ref[rows, :])


def _cast_once(w_ref, wb_ref):
    @pl.when(pl.program_id(0) == 0)
    def _():
        wb_ref[...] = w_ref[...].astype(BF16)


def _resident(shape):
    return pl.BlockSpec(shape, lambda i: tuple(0 for _ in shape), pipeline_mode=pl.Buffered(1))


def _mod_spec(block_rows):
    return pl.BlockSpec((None, 6, D_MODEL), lambda i: (_cond_of_token_block(i, block_rows), 0, 0))


def _in_proj0_kernel(xp_ref, xl_ref, g_ref, mod_ref, w_ref, o_ref, wb_ref, hb_ref):
    _cast_once(w_ref, wb_ref)
    n = IN_PROJ_SUB_ROWS
    n_sub = xp_ref.shape[0] // n

    def prepare(r):
        x = _select_trunk(xp_ref, xl_ref, slice(r * n, (r + 1) * n))
        hb_ref[r] = _modulated_norm(x, g_ref[...], mod_ref[...], 0, 1).astype(BF16)

    def project(r):
        u = jnp.dot(hb_ref[r], wb_ref[...], preferred_element_type=F32)
        o_ref[r * n:(r + 1) * n, :] = u.astype(o_ref.dtype)

    prepare(0)
    for r in range(1, n_sub):
        prepare(r)
        project(r - 1)
    project(n_sub - 1)


def _in_proj0(x_prompt, x_latent, g, mod_l, w, block_rows=512):
    n = w.shape[1]
    return pl.pallas_call(
        _in_proj0_kernel,
        grid=(N_TOK // block_rows,),
        in_specs=[*_trunk_specs(block_rows, D_MODEL), _resident((1, D_MODEL)), _mod_spec(block_rows),
                  _resident((D_MODEL, n))],
        out_specs=pl.BlockSpec((block_rows, n), lambda i: (i, 0)),
        out_shape=jax.ShapeDtypeStruct((N_TOK, n), BF16),
        scratch_shapes=[pltpu.VMEM((D_MODEL, n), BF16),
                        pltpu.VMEM((block_rows // IN_PROJ_SUB_ROWS, IN_PROJ_SUB_ROWS, D_MODEL), BF16)],
        compiler_params=_params("arbitrary"),
        name="in_proj0",
    )(x_prompt, x_latent, g.reshape(1, D_MODEL), mod_l, w)


def _moe_mix(x_ref, ya_ref, yb_ref, wt_ref, mod_ref, rows=slice(None)):
    wt = wt_ref[rows, :]
    mix = wt[:, 0:1] * _unpack_rows(ya_ref[rows, :]) + wt[:, 1:2] * _unpack_rows(yb_ref[rows, :])
    return x_ref[rows, :] + mod_ref[5:6, :] * mix


def _in_proj1_kernel(x_ref, ya_ref, yb_ref, wt_ref, modp_ref, g_ref, mod_ref, w_ref, b_ref, xo_ref, o_ref,
                     wb_ref, hb_ref):
    _cast_once(w_ref, wb_ref)
    n = IN_PROJ_SUB_ROWS
    n_sub = x_ref.shape[0] // n

    def prepare(r):
        rows = slice(r * n, (r + 1) * n)
        x = _moe_mix(x_ref, ya_ref, yb_ref, wt_ref, modp_ref, rows)
        xo_ref[rows, :] = x
        hb_ref[r] = _modulated_norm(x, g_ref[...], mod_ref[...], 0, 1).astype(BF16)

    def project(r):
        rows = slice(r * n, (r + 1) * n)
        u = jnp.dot(hb_ref[r], wb_ref[...], preferred_element_type=F32) + b_ref[...]
        o_ref[rows, :] = u.astype(o_ref.dtype)

    prepare(0)
    for r in range(1, n_sub):
        prepare(r)
        project(r - 1)
    project(n_sub - 1)


def _in_proj1(x, moe_out, mod_prev, g, mod_l, w, bias, block_rows=512):
    ya, yb, w_tok = moe_out
    n = w.shape[1]
    tok = pl.BlockSpec((block_rows, D_MODEL), lambda i: (i, 0))
    packed = pl.BlockSpec((block_rows, ROW_WORDS), lambda i: (i, 0))
    return pl.pallas_call(
        _in_proj1_kernel,
        grid=(N_TOK // block_rows,),
        in_specs=[tok, packed, packed, pl.BlockSpec((block_rows, TOP_K), lambda i: (i, 0)), _mod_spec(block_rows),
                  _resident((1, D_MODEL)), _mod_spec(block_rows), _resident((D_MODEL, n)), _resident((1, n))],
        out_specs=(tok, pl.BlockSpec((block_rows, n), lambda i: (i, 0))),
        out_shape=(jax.ShapeDtypeStruct((N_TOK, D_MODEL), F32), jax.ShapeDtypeStruct((N_TOK, n), BF16)),
        scratch_shapes=[pltpu.VMEM((D_MODEL, n), BF16),
                        pltpu.VMEM((block_rows // IN_PROJ_SUB_ROWS, IN_PROJ_SUB_ROWS, D_MODEL), BF16)],
        compiler_params=_params("arbitrary"),
        name="in_proj1",
    )(x, ya, yb, w_tok, mod_prev, g.reshape(1, D_MODEL), mod_l, w, bias.reshape(1, n))


def _hgrn_kernel(*refs, seq_len, with_state):
    if with_state:
        (q_ref, zf_ref, zb_ref, i_ref, ga_ref, lb_ref, og_ref, s0_ref, o_ref, of_ref, ob_ref) = refs
    else:
        (q_ref, zf_ref, zb_ref, i_ref, ga_ref, lb_ref, og_ref, o_ref, s_ref, of_ref, ob_ref) = refs
    n_blocks = seq_len // HGRN_BLOCK
    chunks_per_block = HGRN_BLOCK // CHUNK

    lbr = lb_ref[...]
    mx = jnp.maximum(lbr[0], lbr[1])
    e0 = jnp.exp(lbr[0] - mx)
    e1 = jnp.exp(lbr[1] - mx)
    lb = e0 / (e0 + e1)

    row = lax.broadcasted_iota(jnp.int32, (HGRN_BLOCK, HGRN_BLOCK), 0)
    col = lax.broadcasted_iota(jnp.int32, (HGRN_BLOCK, HGRN_BLOCK), 1)
    same_chunk = (row // CHUNK) == (col // CHUNK)
    nt = (((1,), (1,)), ((), ()))
    tn = (((0,), (0,)), ((), ()))

    def per_chunk_row(x, idx):
        return jnp.concatenate(
            [jnp.broadcast_to(x[n * CHUNK + idx:n * CHUNK + idx + 1, :], (CHUNK, x.shape[1]))
             for n in range(chunks_per_block)], axis=0)

    def in_chunk_cumsum(tri, x):
        hi = x.astype(BF16)
        lo = (x - hi.astype(F32)).astype(BF16)
        return jnp.dot(tri, hi, preferred_element_type=F32) + jnp.dot(tri, lo, preferred_element_type=F32)

    def prepare(blk, cols, z_ref, lbd, forward):
        rows = slice(blk * HGRN_BLOCK, (blk + 1) * HGRN_BLOCK)
        keep = (same_chunk & (col <= row)) if forward else (same_chunk & (col >= row))
        tri = jnp.where(keep, 1.0, 0.0).astype(BF16)
        mid = CHUNK // 2 if forward else CHUNK - 1 - CHUNK // 2
        last = CHUNK - 1 if forward else 0
        f = lbd + (1.0 - lbd) * jax.nn.sigmoid(z_ref[rows, cols].astype(F32))
        lf = jnp.log(f)
        k = 1.0 - f
        q = q_ref[rows, cols].astype(F32)
        b = in_chunk_cumsum(tri, lf)
        bm = per_chunk_row(b, mid)
        bl = per_chunk_row(b, last)
        return dict(
            rows=rows, cols=cols, keep=keep, forward=forward,
            vb=i_ref[rows, cols].astype(BF16),
            qe=(q * jnp.exp(b - bm)).astype(BF16), ke=(k * jnp.exp(bm - b)).astype(BF16),
            qb=(q * jnp.exp(b)).astype(BF16), ks=(k * jnp.exp(bl - b)).astype(BF16), decay=jnp.exp(bl))

    def within_chunks(u):
        att = lax.dot_general(u["qe"], u["ke"], nt, preferred_element_type=F32)
        att = jnp.where(u["keep"], att, 0.0)
        u["o_intra"] = jnp.dot(att.astype(BF16), u["vb"], preferred_element_type=F32)
        u["upd"] = [lax.dot_general(u["vb"][n * CHUNK:(n + 1) * CHUNK], u["ks"][n * CHUNK:(n + 1) * CHUNK], tn,
                                    preferred_element_type=F32) for n in range(chunks_per_block)]

    def across_chunks(u, st, out_ref):
        order = range(chunks_per_block) if u["forward"] else range(chunks_per_block - 1, -1, -1)
        o_inter = [None] * chunks_per_block
        for n in order:
            cr = slice(n * CHUNK, (n + 1) * CHUNK)
            o_inter[n] = lax.dot_general(u["qb"][cr], st.astype(BF16), nt, preferred_element_type=F32)
            st = st * u["decay"][n * CHUNK:n * CHUNK + 1, :] + u["upd"][n]
        out_ref[u["rows"], u["cols"]] = u["o_intra"] + jnp.concatenate(o_inter, axis=0)
        return st

    n_heads = q_ref.shape[1] // A_DK
    head_cols = [slice(hd * A_DK, (hd + 1) * A_DK) for hd in range(n_heads)]
    if with_state:
        states = {(hd, d): s0_ref[d, hd].T for hd in range(n_heads) for d in range(2)}
    else:
        states = {(hd, d): jnp.zeros((A_DK, A_DK), F32) for hd in range(n_heads) for d in range(2)}
    for step in range(n_blocks):
        units = {}
        for hd, cols in enumerate(head_cols):
            units[hd, 0] = prepare(step, cols, zf_ref, lb[0:1, cols], True)
            units[hd, 1] = prepare(n_blocks - 1 - step, cols, zb_ref, lb[1:2, cols], False)
        for u in units.values():
            within_chunks(u)
        for key, u in units.items():
            states[key] = across_chunks(u, states[key], of_ref if key[1] == 0 else ob_ref)
    for hd, cols in enumerate(head_cols):
        if not with_state:
            s_ref[0, hd] = states[hd, 0].T
            s_ref[1, hd] = states[hd, 1].T
        o = of_ref[:, cols] + ob_ref[:, cols]
        o = o * lax.rsqrt(jnp.mean(o * o, axis=-1, keepdims=True) + EPS) * og_ref[:, cols]
        ga = ga_ref[:, cols].astype(F32)
        o_ref[:, cols] = (o * (ga * jax.nn.sigmoid(ga))).astype(o_ref.dtype)


def _hgrn(z, hgrn_lb, onorm_g, state, *, latent):
    seq_len = LATENT_LEN if latent else PROMPT_LEN
    n_seq = N_LATENT_SEQ if latent else N_PROMPT_SEQ
    row0 = (N_PROMPT_TOK // seq_len) if latent else 0

    hw = HGRN_HEADS_PER_STEP * A_DK
    n_hg = A_HEADS // HGRN_HEADS_PER_STEP

    def zspec(part):
        return pl.BlockSpec((seq_len, hw), lambda s, h: (row0 + s, part * n_hg + h))

    in_specs = [zspec(0), zspec(1), zspec(2), zspec(3), zspec(4),
                pl.BlockSpec((2, 2, hw), lambda s, h: (0, 0, h)),
                pl.BlockSpec((1, hw), lambda s, h: (0, h))]
    args = [z, z, z, z, z, hgrn_lb, onorm_g.reshape(1, A_WIDTH)]
    state_spec = pl.BlockSpec((None, None, 2, HGRN_HEADS_PER_STEP, A_DK, A_DK), lambda s, h: (s, 0, 0, h, 0, 0))
    o_shape = jax.ShapeDtypeStruct((n_seq * seq_len, A_WIDTH), BF16)
    o_spec = pl.BlockSpec((seq_len, hw), lambda s, h: (s, h))
    if latent:
        in_specs.append(state_spec)
        args.append(state)
        out_shape, out_specs = o_shape, o_spec
    else:
        out_shape = (o_shape, jax.ShapeDtypeStruct((n_seq, 1, 2, A_HEADS, A_DK, A_DK), F32))
        out_specs = (o_spec, state_spec)
    return pl.pallas_call(
        functools.partial(_hgrn_kernel, seq_len=seq_len, with_state=latent),
        grid=(n_seq, n_hg),
        in_specs=in_specs,
        out_specs=out_specs,
        out_shape=out_shape,
        scratch_shapes=[pltpu.VMEM((seq_len, hw), F32), pltpu.VMEM((seq_len, hw), F32)],
        compiler_params=_params("arbitrary", "arbitrary"),
        name="hgrn_latent" if latent else "hgrn_prompt",
    )(*args)


def _rope_tables():
    pos = np.arange(LATENT_LEN)
    row, colp = pos // GRID_W, pos % GRID_W
    inv = ROPE_THETA ** (-np.arange(ROPE_PAIRS, dtype=np.float32) / ROPE_PAIRS)
    inv = inv.astype(np.float32)
    ang_r = (row.astype(np.float32)[:, None] * inv).astype(np.float32)
    ang_c = (colp.astype(np.float32)[:, None] * inv).astype(np.float32)
    cos = np.concatenate([np.cos(ang_r), np.cos(ang_r), np.cos(ang_c), np.cos(ang_c)], axis=1)
    sin = np.concatenate([-np.sin(ang_r), np.sin(ang_r), -np.sin(ang_c), np.sin(ang_c)], axis=1)
    return cos.astype(np.float32), sin.astype(np.float32)


def _head_mean_matrix(width):
    idx = np.arange(width) // HEAD_DIM
    return jnp.asarray((idx[:, None] == idx[None, :]).astype(np.float32) / HEAD_DIM).astype(BF16)


def _attn_kernel(*refs, latent):
    if latent:
        (q_ref, k_ref, v_ref, qg_ref, kg_ref, gq_ref, gk_ref, cosq_ref, sinq_ref, cosk_ref, sink_ref,
         ck_ref, cv_ref, o_ref) = refs
    else:
        (q_ref, k_ref, v_ref, qg_ref, kg_ref, gq_ref, gk_ref, o_ref, kout_ref, vout_ref) = refs
    pair_w = 2 * HEAD_DIM

    def head_norm(x, mean_ref, gain):
        sq = x * x
        hi = sq.astype(BF16)
        lo = (sq - hi.astype(F32)).astype(BF16)
        ms = jnp.dot(hi, mean_ref[...], preferred_element_type=F32)
        ms = ms + jnp.dot(lo, mean_ref[...], preferred_element_type=F32)
        return x * lax.rsqrt(ms + EPS) * gain

    def rope(x, cos, sin):
        n = x.shape[1]
        lane = lax.broadcasted_iota(jnp.int32, x.shape, 1)
        first_of_pair = (lane // ROPE_PAIRS) % 2 == 0
        swapped = jnp.where(first_of_pair, pltpu.roll(x, n - ROPE_PAIRS, axis=1), pltpu.roll(x, ROPE_PAIRS, axis=1))
        return x * cos + swapped * sin

    nt = (((1,), (1,)), ((), ()))

    def prepare(rows, seq_idx):
        q = head_norm(q_ref[rows, :].astype(F32), gq_ref, qg_ref[...])
        k = head_norm(k_ref[rows, :].astype(F32), gk_ref, kg_ref[...])
        if latent:
            q = rope(q, cosq_ref[...], sinq_ref[...])
            k = rope(k, cosk_ref[...], sink_ref[...])
        q = q * (HEAD_DIM ** -0.5)
        v = v_ref[rows, :].astype(F32)
        n_q = q.shape[0]
        low_kv = lax.broadcasted_iota(jnp.int32, k.shape, 1) < HEAD_DIM
        low_q = lax.broadcasted_iota(jnp.int32, (n_q, pair_w), 1) < HEAD_DIM
        k_swapped = pltpu.roll(k, HEAD_DIM, axis=1)
        v_swapped = pltpu.roll(v, HEAD_DIM, axis=1)
        if not latent:
            kout_ref[seq_idx] = k.T
            vout_ref[seq_idx] = v.T
        units = []
        for j in range(KV_HEADS):
            kd = (jnp.where(low_kv, k, k_swapped) if j == 0 else jnp.where(low_kv, k_swapped, k)).astype(BF16)
            vd = (jnp.where(low_kv, v, v_swapped) if j == 0 else jnp.where(low_kv, v_swapped, v)).astype(BF16)
            vd = jnp.concatenate([vd, jnp.ones_like(vd)], axis=1)
            tiles = range(j * Q_PER_KV // 2, (j + 1) * Q_PER_KV // 2)
            parts = []
            for t in tiles:
                qt = q[:, t * pair_w:(t + 1) * pair_w]
                parts += [jnp.where(low_q, qt, 0.0), jnp.where(low_q, 0.0, qt)]
            units.append(dict(j=j, rows=rows, tiles=tiles, n_q=n_q, low_q=low_q, kd=kd, vd=vd,
                              qs=jnp.concatenate(parts, axis=0).astype(BF16)))
        return units

    def scores(u):
        u["s_new"] = lax.dot_general(u["qs"], u["kd"], nt, preferred_element_type=F32)
        if latent:
            j = u["j"]
            cvd = jnp.concatenate([cv_ref[j], cv_ref[j]], axis=1).astype(BF16)
            u["cvd"] = jnp.concatenate([cvd, jnp.ones_like(cvd)], axis=1)
            ckd = jnp.concatenate([ck_ref[j], ck_ref[j]], axis=1).astype(BF16)
            u["s_old"] = lax.dot_general(u["qs"], ckd, nt, preferred_element_type=F32)

    def softmax(u):
        m = jnp.max(u["s_new"], axis=-1, keepdims=True)
        if latent:
            m = jnp.maximum(m, jnp.max(u["s_old"], axis=-1, keepdims=True))
        u["p_new"] = jnp.exp(u.pop("s_new") - m).astype(BF16)
        if latent:
            u["p_old"] = jnp.exp(u.pop("s_old") - m).astype(BF16)

    def weighted_values(u):
        acc = jnp.dot(u["p_new"], u["vd"], preferred_element_type=F32)
        if latent:
            acc = acc + jnp.dot(u["p_old"], u["cvd"], preferred_element_type=F32)
        out = acc[:, :pair_w] / acc[:, pair_w:]
        n_q = u["n_q"]
        for i, t in enumerate(u["tiles"]):
            lo_head = out[(2 * i) * n_q:(2 * i + 1) * n_q, :]
            hi_head = out[(2 * i + 1) * n_q:(2 * i + 2) * n_q, :]
            o_ref[u["rows"], t * pair_w:(t + 1) * pair_w] = jnp.where(u["low_q"], lo_head, hi_head).astype(o_ref.dtype)

    if latent:
        units = prepare(slice(None), None)
    else:
        seq = PROMPT_LEN
        units = [u for s in range(q_ref.shape[0] // seq) for u in prepare(slice(s * seq, (s + 1) * seq), s)]
    for phase in (scores, softmax, weighted_values):
        for u in units:
            phase(u)


def _attn_common_args(qn_g, kn_g):
    q_w, kv_w = Q_HEADS * HEAD_DIM, KV_HEADS * HEAD_DIM
    return (jnp.tile(qn_g, Q_HEADS).reshape(1, q_w), jnp.tile(kn_g, KV_HEADS).reshape(1, kv_w),
            _head_mean_matrix(q_w), _head_mean_matrix(kv_w))


def _attention_prompt(z, qn_g, kn_g):
    seqs = 8
    L = seqs * PROMPT_LEN
    cache_shape = jax.ShapeDtypeStruct((N_PROMPT_SEQ, KV_HEADS * HEAD_DIM, PROMPT_LEN), F32)
    cache_spec = pl.BlockSpec((seqs, KV_HEADS * HEAD_DIM, PROMPT_LEN), lambda s: (s, 0, 0))
    q_w, kv_w = Q_HEADS * HEAD_DIM, KV_HEADS * HEAD_DIM
    q_col = (5 * A_WIDTH) // q_w
    k_col = (5 * A_WIDTH + q_w) // kv_w
    const = lambda r, c: pl.BlockSpec((r, c), lambda s: (0, 0))
    return pl.pallas_call(
        functools.partial(_attn_kernel, latent=False),
        grid=(N_PROMPT_TOK // L,),
        in_specs=[
            pl.BlockSpec((L, q_w), lambda s: (s, q_col)),
            pl.BlockSpec((L, kv_w), lambda s: (s, k_col)),
            pl.BlockSpec((L, kv_w), lambda s: (s, k_col + 1)),
            const(1, q_w), const(1, kv_w), const(q_w, q_w), const(kv_w, kv_w),
        ],
        out_specs=(pl.BlockSpec((L, q_w), lambda s: (s, 0)), cache_spec, cache_spec),
        out_shape=(jax.ShapeDtypeStruct((N_PROMPT_TOK, q_w), BF16), cache_shape, cache_shape),
        compiler_params=_params("arbitrary"),
        name="attn_prompt",
    )(z, z, z, *_attn_common_args(qn_g, kn_g))


def _attention_latent(z, qn_g, kn_g, cache_k, cache_v):
    L = LATENT_LEN
    nqb = L // Q_BLOCK
    q_w, kv_w = Q_HEADS * HEAD_DIM, KV_HEADS * HEAD_DIM
    q_col = (5 * A_WIDTH) // q_w
    k_col = (5 * A_WIDTH + q_w) // kv_w
    qrow0 = N_PROMPT_TOK // Q_BLOCK
    krow0 = N_PROMPT_TOK // L
    cos, sin = _rope_tables()
    cos_q, sin_q = jnp.asarray(np.tile(cos, (1, Q_HEADS))), jnp.asarray(np.tile(sin, (1, Q_HEADS)))
    cos_k, sin_k = jnp.asarray(np.tile(cos, (1, KV_HEADS))), jnp.asarray(np.tile(sin, (1, KV_HEADS)))
    const = lambda r, c: pl.BlockSpec((r, c), lambda s, b: (0, 0))
    cache_spec = pl.BlockSpec((None, None, KV_HEADS, PAST_LEN, HEAD_DIM), lambda s, b: (s, 0, 0, 0, 0))
    return pl.pallas_call(
        functools.partial(_attn_kernel, latent=True),
        grid=(N_LATENT_SEQ, nqb),
        in_specs=[
            pl.BlockSpec((Q_BLOCK, q_w), lambda s, b: (qrow0 + s * nqb + b, q_col)),
            pl.BlockSpec((L, kv_w), lambda s, b: (krow0 + s, k_col)),
            pl.BlockSpec((L, kv_w), lambda s, b: (krow0 + s, k_col + 1)),
            const(1, q_w), const(1, kv_w), const(q_w, q_w), const(kv_w, kv_w),
            pl.BlockSpec((Q_BLOCK, q_w), lambda s, b: (b, 0)),
            pl.BlockSpec((Q_BLOCK, q_w), lambda s, b: (b, 0)),
            const(L, kv_w), const(L, kv_w),
            cache_spec, cache_spec,
        ],
        out_specs=pl.BlockSpec((Q_BLOCK, q_w), lambda s, b: (s * nqb + b, 0)),
        out_shape=jax.ShapeDtypeStruct((N_LATENT_TOK, q_w), BF16),
        compiler_params=_params("arbitrary", "arbitrary"),
        name="attn_latent",
    )(z, z, z, *_attn_common_args(qn_g, kn_g), cos_q, sin_q, cos_k, sin_k, cache_k, cache_v)


def _out_proj_kernel(*refs, n_act, n_x):
    a_refs = refs[:2 * n_act]
    x_refs = refs[2 * n_act:2 * n_act + n_x]
    g_ref, mod_ref, rw_ref, w_ref, xo_ref, h_ref, lg_ref, wb_ref, rws_ref, acc_ref = refs[2 * n_act + n_x:]
    _cast_once(w_ref, wb_ref)

    @pl.when(pl.program_id(0) == 0)
    def _():
        rw = rw_ref[...]
        hi = rw.astype(BF16).astype(F32)
        lo = (rw - hi).astype(BF16).astype(F32)
        rws_ref[...] = (hi + pltpu.roll(lo, N_EXPERTS, axis=1)).astype(BF16)

    mod = mod_ref[...]
    n = OUT_PROJ_SUB_ROWS

    n_sub = xo_ref.shape[0] // n

    def sub_rows(r):
        if isinstance(r, int):
            return slice(r * n, (r + 1) * n)
        return pl.ds(pl.multiple_of(r * n, n), n)

    def project(r):
        rows = sub_rows(r)
        acc = None
        k0 = 0
        for ap_ref, al_ref in zip(a_refs[0::2], a_refs[1::2]):
            k1 = k0 + ap_ref.shape[1]
            part = jnp.dot(_select_trunk(ap_ref, al_ref, rows), wb_ref[k0:k1, :], preferred_element_type=F32)
            acc = part if acc is None else acc + part
            k0 = k1
        acc_ref[r % 2] = acc

    def finish(r):
        rows = sub_rows(r)
        x_in = x_refs[0][rows, :] if n_x == 1 else _select_trunk(*x_refs, rows)
        x = x_in + mod[2:3, :] * acc_ref[r % 2]
        xo_ref[rows, :] = x
        h = _modulated_norm(x, g_ref[...], mod, 3, 4)
        h_ref[rows, :] = _pack_rows(h)
        h_hi = h.astype(BF16)
        h_lo = (h - h_hi.astype(F32)).astype(BF16)
        both = jnp.dot(jnp.concatenate([h_hi, h_lo], axis=0), rws_ref[...], preferred_element_type=F32)
        from_hi, from_lo = both[:n], both[n:]
        lg = from_hi + pltpu.roll(from_hi, ROUTER_LANES - N_EXPERTS, axis=1) + from_lo
        lg_ref[:, rows] = lg.T[:N_EXPERTS, :]

    project(0)
    for r in range(n_sub - 1):
        project(r + 1)
        finish(r)
    finish(n_sub - 1)


def _out_proj(acts, w, xs, g, mod_l, router_wp, block_rows=1024):
    tok = lambda width: pl.BlockSpec((block_rows, width), lambda i: (i, 0))
    in_specs = [spec for ap, _ in acts for spec in _trunk_specs(block_rows, ap.shape[1])]
    in_specs += [tok(D_MODEL)] if len(xs) == 1 else list(_trunk_specs(block_rows, D_MODEL))
    in_specs += [_resident((1, D_MODEL)), _mod_spec(block_rows), _resident((D_MODEL, ROUTER_LANES)),
                 _resident(w.shape)]
    return pl.pallas_call(
        functools.partial(_out_proj_kernel, n_act=len(acts), n_x=len(xs)),
        grid=(N_TOK // block_rows,),
        in_specs=in_specs,
        out_specs=(tok(D_MODEL), tok(ROW_WORDS), pl.BlockSpec((N_EXPERTS, block_rows), lambda i: (0, i))),
        out_shape=(jax.ShapeDtypeStruct((N_TOK, D_MODEL), F32),
                   jax.ShapeDtypeStruct((N_TOK, ROW_WORDS), jnp.int32),
                   jax.ShapeDtypeStruct((N_EXPERTS, N_TOK), F32)),
        scratch_shapes=[pltpu.VMEM(w.shape, BF16), pltpu.VMEM((D_MODEL, ROUTER_LANES), BF16),
                        pltpu.VMEM((2, OUT_PROJ_SUB_ROWS, D_MODEL), F32)],
        compiler_params=_params("arbitrary"),
        name="out_proj",
    )(*[a for pair in acts for a in pair], *xs, g.reshape(1, D_MODEL), mod_l, router_wp, w)


def _router_kernel(lg_ref, rb_ref, pos_ref, w_ref, plan_ref, rank_ref):
    lg = lg_ref[...]
    ex = jnp.exp(lg - jnp.max(lg, axis=0, keepdims=True))
    scores = ex / jnp.sum(ex, axis=0, keepdims=True)
    biased = scores + rb_ref[...]
    expert = lax.broadcasted_iota(jnp.int32, biased.shape, 0)
    in_pos = expert % EXPERTS_PER_GROUP
    rank = jnp.zeros_like(biased)
    for d in range(1, EXPERTS_PER_GROUP):
        wraps = in_pos + d >= EXPERTS_PER_GROUP
        partner = jnp.where(wraps, pltpu.roll(biased, EXPERTS_PER_GROUP - d, axis=0),
                            pltpu.roll(biased, N_EXPERTS - d, axis=0))
        rank = rank + jnp.where(wraps, jnp.where(partner >= biased, 1.0, 0.0), jnp.where(partner > biased, 1.0, 0.0))
    selected = rank < 1.5
    contrib = jnp.where(selected, biased, 0.0)
    group_score = []
    for gi in range(N_GROUPS):
        r = [contrib[gi * EXPERTS_PER_GROUP + i:gi * EXPERTS_PER_GROUP + i + 1, :] for i in range(EXPERTS_PER_GROUP)]
        group_score.append(((r[0] + r[1]) + r[2]) + r[3])
    best = group_score[0]
    best_group = jnp.zeros_like(best)
    for gi in range(1, N_GROUPS):
        better = group_score[gi] > best
        best_group = jnp.where(better, float(gi), best_group)
        best = jnp.where(better, group_score[gi], best)
    in_group = (expert // EXPERTS_PER_GROUP).astype(F32) == best_group
    chosen = jnp.where(selected, jnp.where(in_group, 1.0, 0.0), 0.0)
    picked = chosen * scores
    gates = picked / jnp.sum(picked, axis=0, keepdims=True)
    lanes = 128
    n_blk = N_TOK // lanes
    li = lax.broadcasted_iota(jnp.int32, (lanes, lanes), 0)
    lj = lax.broadcasted_iota(jnp.int32, (lanes, lanes), 1)
    prefix = jnp.where(li <= lj, 1.0, 0.0).astype(BF16)
    stacked = jnp.concatenate([chosen[:, blk * lanes:(blk + 1) * lanes] for blk in range(n_blk)], axis=0)
    incl_all = jnp.dot(stacked.astype(BF16), prefix, preferred_element_type=F32)
    carry = jnp.zeros((N_EXPERTS, 1), F32)
    for blk in range(n_blk):
        cols = slice(blk * lanes, (blk + 1) * lanes)
        incl = incl_all[blk * N_EXPERTS:(blk + 1) * N_EXPERTS, :]
        rank_ref[:, cols] = incl - chosen[:, cols] + carry
        carry = carry + incl[:, lanes - 1:lanes]
    count = carry
    padded = jnp.floor((count + float(MOE_TILE - 1)) * (1.0 / MOE_TILE)) * float(MOE_TILE)
    erow = lax.broadcasted_iota(jnp.int32, (N_EXPERTS, 1), 0)
    offset = jnp.zeros((N_EXPERTS, 1), F32)
    for e in range(N_EXPERTS - 1):
        offset = offset + jnp.where(erow > e, padded[e:e + 1, :], 0.0)
    position = rank_ref[...] + offset
    ei = lax.broadcasted_iota(jnp.int32, (N_EXPERTS, N_EXPERTS), 0)
    ej = lax.broadcasted_iota(jnp.int32, (N_EXPERTS, N_EXPERTS), 1)
    lower = jnp.where(ej <= ei, 1.0, 0.0).astype(BF16)
    seen = jnp.dot(lower, chosen.astype(BF16), preferred_element_type=F32)
    first = (chosen > 0.5) & (seen < 1.5)
    second = (chosen > 0.5) & (seen > 1.5)
    pick = lambda flag, x: jnp.sum(jnp.where(flag, x, 0.0), axis=0, keepdims=True)
    pos_ref[0:1, :] = pick(first, position).astype(jnp.int32)
    pos_ref[1:2, :] = pick(second, position).astype(jnp.int32)
    w_rows = jnp.concatenate([pick(first, gates), pick(second, gates), jnp.zeros((6, N_TOK), F32)], axis=0)
    ti = lax.broadcasted_iota(jnp.int32, (8, lanes), 0)
    tj = lax.broadcasted_iota(jnp.int32, (8, lanes), 1)
    eye = jnp.where(ti == tj, 1.0, 0.0).astype(BF16)
    tn = (((0,), (0,)), ((), ()))
    hi = w_rows.astype(BF16)
    r1 = w_rows - hi.astype(F32)
    mid = r1.astype(BF16)
    lo = (r1 - mid.astype(F32)).astype(BF16)
    w_cols = lax.dot_general(hi, eye, tn, preferred_element_type=F32)
    w_cols = w_cols + lax.dot_general(mid, eye, tn, preferred_element_type=F32)
    w_cols = w_cols + lax.dot_general(lo, eye, tn, preferred_element_type=F32)
    w_ref[...] = w_cols[:, :TOP_K]
    start = (lax.broadcasted_iota(jnp.int32, (N_EXPERTS, lanes), 1) * MOE_TILE).astype(F32)
    end = offset + padded
    tile_expert = jnp.sum(jnp.where(end <= start, 1.0, 0.0), axis=0, keepdims=True)
    inside = (offset <= start) & (start < end)
    real = jnp.clip(count - (start - offset), 0.0, float(MOE_TILE))
    tile_rows = jnp.sum(jnp.where(inside, real, 0.0), axis=0, keepdims=True)
    plan_ref[0:1, :] = jnp.minimum(tile_expert, float(N_EXPERTS - 1)).astype(jnp.int32)
    plan_ref[1:2, :] = tile_rows.astype(jnp.int32)


def _router(logits_t, router_b):
    whole = lambda shape: pl.BlockSpec(shape, lambda i: (0, 0))
    return pl.pallas_call(
        _router_kernel,
        grid=(1,),
        in_specs=[whole((N_EXPERTS, N_TOK)), whole((N_EXPERTS, 1))],
        out_specs=(whole((2, N_TOK)), whole((N_TOK, TOP_K)), whole((2, 128))),
        out_shape=(jax.ShapeDtypeStruct((2, N_TOK), jnp.int32),
                   jax.ShapeDtypeStruct((N_TOK, TOP_K), F32),
                   jax.ShapeDtypeStruct((2, 128), jnp.int32)),
        scratch_shapes=[pltpu.VMEM((N_EXPERTS, N_TOK), F32)],
        compiler_params=_params("arbitrary"),
        name="router",
    )(logits_t, router_b.reshape(N_EXPERTS, 1))


def _sc_mesh():
    return plsc.VectorSubcoreMesh(core_axis_name="c", subcore_axis_name="s")


def _sc_worker_base():
    return (lax.axis_index("s") * SC_CORES + lax.axis_index("c")) * SC_TOKENS_PER_WORKER


def _moe_dispatch(h, pos_a, pos_b):
    n_chunks = SC_TOKENS_PER_WORKER // SC_CHUNK
    idx = pltpu.VMEM((SC_CHUNK,), jnp.int32)

    @functools.partial(
        pl.kernel, mesh=_sc_mesh(),
        out_type=jax.ShapeDtypeStruct((MOE_ROWS, ROW_WORDS), jnp.int32),
        scratch_types=[idx, idx, idx, idx, pltpu.VMEM((2, SC_CHUNK, ROW_WORDS), jnp.int32),
                       pltpu.SemaphoreType.DMA((6,)), pltpu.SemaphoreType.DMA((4,))],
        name="moe_dispatch",
    )
    def run(h_hbm, pa_hbm, pb_hbm, xs_hbm, ia0, ib0, ia1, ib1, rows_v, sem_in, sem_out):
        base = _sc_worker_base()
        ia, ib = (ia0, ia1), (ib0, ib1)

        def start_loads(c):
            slot = c % 2
            tok = pl.ds(pl.multiple_of(base + c * SC_CHUNK, 8), SC_CHUNK)
            return (pltpu.async_copy(pa_hbm.at[tok], ia[slot], sem_in.at[3 * slot]),
                    pltpu.async_copy(pb_hbm.at[tok], ib[slot], sem_in.at[3 * slot + 1]),
                    pltpu.async_copy(h_hbm.at[tok], rows_v.at[slot], sem_in.at[3 * slot + 2]))

        loads = start_loads(0)
        scatters = [(), ()]
        for c in range(n_chunks):
            slot = c % 2
            for cp in loads:
                cp.wait()
            if c + 1 < n_chunks:
                for cp in scatters[1 - slot]:
                    cp.wait()
                scatters[1 - slot] = ()
                loads = start_loads(c + 1)
            scatters[slot] = (pltpu.async_copy(rows_v.at[slot], xs_hbm.at[ia[slot]], sem_out.at[2 * slot]),
                              pltpu.async_copy(rows_v.at[slot], xs_hbm.at[ib[slot]], sem_out.at[2 * slot + 1]))
        for pending in scatters:
            for cp in pending:
                cp.wait()

    return run(h, pos_a, pos_b)


def _moe_collect(ys, pos_a, pos_b, tok0=0, n_tok=N_TOK):
    per_worker = n_tok // SC_WORKERS
    chunk = SC_CHUNK if per_worker % SC_CHUNK == 0 else 32
    n_chunks = per_worker // chunk
    out = jax.ShapeDtypeStruct((n_tok, ROW_WORDS), jnp.int32)
    idx = pltpu.VMEM((per_worker,), jnp.int32)
    rows = pltpu.VMEM((2, chunk, ROW_WORDS), jnp.int32)

    @functools.partial(
        pl.kernel, mesh=_sc_mesh(), out_type=(out, out),
        scratch_types=[idx, idx, rows, rows, pltpu.SemaphoreType.DMA((4,)), pltpu.SemaphoreType.DMA((4,))],
        name="moe_collect",
    )
    def run(ys_hbm, pa_hbm, pb_hbm, ya_hbm, yb_hbm, ia_v, ib_v, ra_v, rb_v, sem_g, sem_w):
        base = (lax.axis_index("s") * SC_CORES + lax.axis_index("c")) * per_worker
        mine = pl.ds(pl.multiple_of(tok0 + base, 8), per_worker)
        pltpu.sync_copy(pa_hbm.at[mine], ia_v)
        pltpu.sync_copy(pb_hbm.at[mine], ib_v)
        writes = [(), ()]
        for c in range(n_chunks):
            slot = c % 2
            for cp in writes[slot]:
                cp.wait()
            part = pl.ds(c * chunk, chunk)
            tok = pl.ds(pl.multiple_of(base + c * chunk, 8), chunk)
            ga = pltpu.async_copy(ys_hbm.at[ia_v.at[part]], ra_v.at[slot], sem_g.at[slot])
            gb = pltpu.async_copy(ys_hbm.at[ib_v.at[part]], rb_v.at[slot], sem_g.at[2 + slot])
            ga.wait()
            wa = pltpu.async_copy(ra_v.at[slot], ya_hbm.at[tok], sem_w.at[slot])
            gb.wait()
            wb = pltpu.async_copy(rb_v.at[slot], yb_hbm.at[tok], sem_w.at[2 + slot])
            writes[slot] = (wa, wb)
        for pending in writes:
            for cp in pending:
                cp.wait()

    return run(ys, pos_a, pos_b)


def _experts_kernel(plan_ref, xs_ref, wg_hbm, wu_hbm, wd_hbm, y_ref,
                    sg_ref, su_ref, sd_ref, wgb_ref, wub_ref, wdb_ref, hid_ref, sems, seg_ref, *, layer):
    n_tiles = pl.num_programs(0) * EXPERT_TILES_PER_STEP

    def weight_copies(e, slot):
        return (pltpu.make_async_copy(wg_hbm.at[layer, e], sg_ref.at[slot], sems.at[slot, 0]),
                pltpu.make_async_copy(wu_hbm.at[layer, e], su_ref.at[slot], sems.at[slot, 1]),
                pltpu.make_async_copy(wd_hbm.at[layer, e], sd_ref.at[slot], sems.at[slot, 2]))

    def tile(t, row0):
        expert = plan_ref[t]
        n_real = plan_ref[PLAN_LANES + t]
        fresh = jnp.logical_or(t == 0, expert != plan_ref[jnp.maximum(t - 1, 0)])

        @pl.when(t == 0)
        def _():
            seg_ref[0] = 0

            @pl.when(n_real > 0)
            def _():
                for cp in weight_copies(expert, 0):
                    cp.start()

        @pl.when(jnp.logical_and(n_real > 0, fresh))
        def _():
            slot = seg_ref[0] % 2
            for cp in weight_copies(expert, slot):
                cp.wait()
            wgb_ref[...] = sg_ref[slot].astype(BF16)
            wub_ref[...] = su_ref[slot].astype(BF16)
            wdb_ref[...] = sd_ref[slot].astype(BF16)
            nxt = lax.while_loop(
                lambda u: jnp.logical_and(u < n_tiles, plan_ref[jnp.minimum(u, n_tiles - 1)] == expert),
                lambda u: u + 1, t + 1)
            nxt_c = jnp.minimum(nxt, n_tiles - 1)

            @pl.when(jnp.logical_and(nxt < n_tiles, plan_ref[PLAN_LANES + nxt_c] > 0))
            def _():
                for cp in weight_copies(plan_ref[nxt_c], 1 - slot):
                    cp.start()

            seg_ref[0] = seg_ref[0] + 1

        @pl.when(n_real > 0)
        def _():
            n = EXPERT_SUB_ROWS
            n_sub = MOE_TILE // n
            row = lax.broadcasted_iota(jnp.int32, (n, xs_ref.shape[1]), 0)

            def up(r):
                rows = slice(row0 + r * n, row0 + (r + 1) * n)
                words = jnp.where(row < n_real - r * n, xs_ref[rows, :], 0)
                x = _unpack_rows(words).astype(BF16)
                a = jnp.dot(x, wgb_ref[...], preferred_element_type=F32)
                b = jnp.dot(x, wub_ref[...], preferred_element_type=F32)
                hid_ref[r] = ((a * jax.nn.sigmoid(a)) * b).astype(BF16)

            def down(r):
                rows = slice(row0 + r * n, row0 + (r + 1) * n)
                y_ref[rows, :] = _pack_rows(jnp.dot(hid_ref[r], wdb_ref[...], preferred_element_type=F32))

            up(0)
            for r in range(1, n_sub):
                up(r)
                down(r - 1)
            down(n_sub - 1)

    for q in range(EXPERT_TILES_PER_STEP):
        tile(pl.program_id(0) * EXPERT_TILES_PER_STEP + q, q * MOE_TILE)


def _experts(plan, xs, w_gate, w_up, w_down, layer):
    hbm = pl.BlockSpec(memory_space=pl.ANY)
    step_rows = MOE_TILE * EXPERT_TILES_PER_STEP
    return pl.pallas_call(
        functools.partial(_experts_kernel, layer=layer),
        grid_spec=pltpu.PrefetchScalarGridSpec(
            num_scalar_prefetch=1,
            grid=(MOE_ROWS // step_rows,),
            in_specs=[pl.BlockSpec((step_rows, ROW_WORDS), lambda j, plan: (j, 0)), hbm, hbm, hbm],
            out_specs=pl.BlockSpec((step_rows, ROW_WORDS), lambda j, plan: (j, 0)),
            scratch_shapes=[pltpu.VMEM((2, D_MODEL, D_EXPERT), F32), pltpu.VMEM((2, D_MODEL, D_EXPERT), F32),
                            pltpu.VMEM((2, D_EXPERT, D_MODEL), F32),
                            pltpu.VMEM((D_MODEL, D_EXPERT), BF16), pltpu.VMEM((D_MODEL, D_EXPERT), BF16),
                            pltpu.VMEM((D_EXPERT, D_MODEL), BF16),
                            pltpu.VMEM((MOE_TILE // EXPERT_SUB_ROWS, EXPERT_SUB_ROWS, D_EXPERT), BF16),
                            pltpu.SemaphoreType.DMA((2, 3)), pltpu.SMEM((1,), jnp.int32)],
        ),
        out_shape=jax.ShapeDtypeStruct((MOE_ROWS, ROW_WORDS), jnp.int32),
        compiler_params=_params("arbitrary"),
        name="experts",
    )(plan, xs, w_gate, w_up, w_down)


def _combine_kernel(x_ref, ya_ref, yb_ref, wt_ref, mod_ref, o_ref):
    o_ref[...] = _moe_mix(x_ref, ya_ref, yb_ref, wt_ref, mod_ref)


def _combine(x, moe_out, mod_l, tok0, n_tok, block_rows=1024):
    ya, yb, w_tok = moe_out
    b0 = tok0 // block_rows
    rows = lambda width: pl.BlockSpec((block_rows, width), lambda i: (b0 + i, 0))
    local = pl.BlockSpec((block_rows, ROW_WORDS), lambda i: (i, 0))
    return pl.pallas_call(
        _combine_kernel,
        grid=(n_tok // block_rows,),
        in_specs=[rows(D_MODEL), local, local, rows(TOP_K),
                  pl.BlockSpec((None, 6, D_MODEL), lambda i: (_cond_of_token_block(b0 + i, block_rows), 0, 0))],
        out_specs=pl.BlockSpec((block_rows, D_MODEL), lambda i: (i, 0)),
        out_shape=jax.ShapeDtypeStruct((n_tok, D_MODEL), F32),
        compiler_params=_params("arbitrary"),
        name="combine",
    )(x, ya, yb, w_tok, mod_l)


def _moe(h, logits_t, router_b, w_gate, w_up, w_down, layer, ranges=((0, N_TOK),)):
    pos, w, plan = _router(logits_t, router_b)
    xs = _moe_dispatch(h, pos[0], pos[1])
    ys = _experts(plan.reshape(-1), xs, w_gate, w_up, w_down, layer)
    return [(*_moe_collect(ys, pos[0], pos[1], tok0, n_tok), w) for tok0, n_tok in ranges]


def _dft_tables(L):
    k = np.arange(L)[:, None]
    m = np.arange(L)[None, :]
    r = (k * m) % (2 * L)
    ang = np.pi * r.astype(np.float64) / L
    fc = np.cos(ang)
    fs = np.sin(ang)
    fs[0, :] = np.where(np.arange(L) % 2 == 0, 1.0, -1.0)
    wk = np.full((L, 1), 1.0 / L)
    wk[0, 0] = 0.5 / L
    gc = (fc * wk).T
    gs = (fs * wk).T
    return [jnp.asarray(t.astype(np.float32)).astype(BF16) for t in (fc, fs, gc, gs)]


def _filter_consts(L):
    t = np.linspace(0.0, 1.0, L, dtype=np.float32)[:, None]
    w = (np.float32(2.0 * np.pi) * np.arange(L, dtype=np.float32)[:, None] / np.float32(L)).astype(np.float32)
    fb = np.linspace(1e-4, HY_BANDS - 1, HY_BANDS, dtype=np.float32)[None, :]
    emb = np.concatenate([t, np.cos(fb * w), -np.sin(fb * w)], axis=-1).astype(np.float32)
    lo = math.log(HY_DECAY_TARGET) / HY_SLOW_PCT
    hi = math.log(HY_DECAY_TARGET) / HY_FAST_PCT
    deltas = np.abs(np.linspace(lo, hi, D_MODEL, dtype=np.float32))
    decay = np.exp(-t * deltas).astype(np.float32)
    return jnp.asarray(emb), jnp.asarray(decay)


def _filter_kernel(emb_ref, w1_ref, b1_ref, w2_ref, b2_ref, fr_ref, w3f_ref, w3b_ref, dec_ref,
                   fc_ref, fs_ref, kr_ref, q_ref, krn_ref, hd_ref):
    @pl.when(pl.program_id(0) == 0)
    def _():
        fr = fr_ref[...]
        h1 = jnp.sin(fr * (jnp.dot(emb_ref[...], w1_ref[...], precision=HIGHEST,
                                   preferred_element_type=F32) + b1_ref[...]))
        hd_ref[...] = jnp.sin(fr * (jnp.dot(h1, w2_ref[...], precision=HIGHEST,
                                            preferred_element_type=F32) + b2_ref[...]))

    hd = hd_ref[...]
    dec = dec_ref[...]
    f = jnp.dot(hd, w3f_ref[...], precision=HIGHEST, preferred_element_type=F32) * dec
    g = jnp.dot(hd, w3b_ref[...], precision=HIGHEST, preferred_element_type=F32) * dec
    row = lax.broadcasted_iota(jnp.int32, f.shape, 0)
    g = jnp.where(row == 0, 0.0, g)
    s = f + g
    d = f - g
    kr = jnp.dot(fc_ref[...], s.astype(BF16), preferred_element_type=F32)
    qq = jnp.dot(fs_ref[...], d.astype(BF16), preferred_element_type=F32)
    alt = jnp.where(row % 2 == 0, 1.0, -1.0)
    nyq = jnp.sum(alt * s, axis=0, keepdims=True)
    kr_ref[...] = kr
    q_ref[...] = jnp.where(row == 0, 0.0, qq)
    krn_ref[...] = jnp.where(row == 0, nyq, kr)


def _hyena_filter_spectrum(L, w1, b1, w2, b2, w3, freq, fc, fs, cblk=256):
    emb, decay = _filter_consts(L)
    ncb = D_MODEL // cblk
    n_emb = 128
    emb = jnp.pad(emb, ((0, 0), (0, n_emb - emb.shape[1])))
    w1 = jnp.pad(w1, ((0, n_emb - w1.shape[0]), (0, 0)))
    full = lambda shape: pl.BlockSpec(shape, lambda j: tuple(0 for _ in shape))
    out_sds = jax.ShapeDtypeStruct((L, D_MODEL), F32)
    out_spec = pl.BlockSpec((L, cblk), lambda j: (0, j))
    return pl.pallas_call(
        _filter_kernel,
        grid=(ncb,),
        in_specs=[
            full((L, n_emb)), full((n_emb, HY_FFN)), full((1, HY_FFN)), full((HY_FFN, HY_FFN)),
            full((1, HY_FFN)), full((1, HY_FFN)),
            pl.BlockSpec((HY_FFN, cblk), lambda j: (0, j)),
            pl.BlockSpec((HY_FFN, cblk), lambda j: (0, ncb + j)),
            pl.BlockSpec((L, cblk), lambda j: (0, j)),
            full((L, L)), full((L, L)),
        ],
        out_specs=(out_spec, out_spec, out_spec),
        out_shape=(out_sds, out_sds, out_sds),
        scratch_shapes=[pltpu.VMEM((L, HY_FFN), F32)],
        compiler_params=_params("arbitrary"),
        name=f"hyena_filter_{L}",
    )(emb, w1, b1.reshape(1, HY_FFN), w2, b2.reshape(1, HY_FFN), freq.reshape(1, HY_FFN), w3, w3, decay, fc, fs)


def _hyena_conv_kernel(x0_ref, x1_ref, v_ref, cw0_ref, cw1_ref, cwv_ref, cb0_ref, cb1_ref, cbv_ref,
                       kr_ref, q_ref, krn_ref, ds_ref, fc_ref, fs_ref, gc_ref, gs_ref, o_ref,
                       zz_ref, gate_ref, skip_ref, yr_ref, yw_ref):
    L = fc_ref.shape[0]
    unit_w = zz_ref.shape[2]
    units = [(slice(s * L, (s + 1) * L), slice(c * unit_w, (c + 1) * unit_w))
             for s in range(x0_ref.shape[0] // L) for c in range(x0_ref.shape[1] // unit_w)]
    row = lax.broadcasted_iota(jnp.int32, (L, unit_w), 0)

    def gating(i):
        rows, cols = units[i]

        def short_conv(u_ref, w_ref, b_ref):
            u = u_ref[rows, cols].astype(F32)
            w = w_ref[:, cols]
            prev = jnp.where(row == 0, 0.0, pltpu.roll(u, 1, axis=0))
            nxt = jnp.where(row == L - 1, 0.0, pltpu.roll(u, L - 1, axis=0))
            return prev * w[0:1, :] + u * w[1:2, :] + nxt * w[2:3, :] + b_ref[:, cols]

        x0 = short_conv(x0_ref, cw0_ref, cb0_ref)
        zz = short_conv(v_ref, cwv_ref, cbv_ref) * short_conv(x1_ref, cw1_ref, cb1_ref)
        zz_ref[i] = zz.astype(BF16)
        gate_ref[i] = x0
        skip_ref[i] = x0 * zz * ds_ref[:, cols]

    def spectrum(i):
        cols = units[i][1]
        ur = jnp.dot(fc_ref[...], zz_ref[i], preferred_element_type=F32)
        p = jnp.dot(fs_ref[...], zz_ref[i], preferred_element_type=F32)
        qq = q_ref[:, cols]
        yr_ref[i] = (ur * kr_ref[:, cols] - p * qq).astype(BF16)
        yw_ref[i] = (ur * qq + p * krn_ref[:, cols]).astype(BF16)

    def synthesis(i):
        rows, cols = units[i]
        y = jnp.dot(gc_ref[...], yr_ref[i], preferred_element_type=F32)
        y = y + jnp.dot(gs_ref[...], yw_ref[i], preferred_element_type=F32)
        o_ref[rows, cols] = (gate_ref[i] * y + skip_ref[i]).astype(o_ref.dtype)

    for t in range(len(units) + 2):
        if t < len(units):
            gating(t)
        if 0 <= t - 1 < len(units):
            spectrum(t - 1)
        if 0 <= t - 2 < len(units):
            synthesis(t - 2)


def _hyena_conv(u, conv_w, conv_b, dskip, spectrum, tables, *, latent):
    L = LATENT_LEN if latent else PROMPT_LEN
    n_seq = N_LATENT_SEQ if latent else N_PROMPT_SEQ
    cblk = 512
    unit_w = 256 if latent else 512
    ncb = D_MODEL // cblk
    seqs = 1 if latent else 8
    unit = (seqs * cblk // unit_w, L, unit_w)
    row0 = (N_PROMPT_TOK // L) if latent else 0
    kr, qq, krn = spectrum
    fc, fs, gc, gs = tables

    def part(p, rows):
        if rows != L:
            return pl.BlockSpec((rows, cblk), lambda j, s: (0, p * ncb + j))
        return pl.BlockSpec((seqs * L, cblk), lambda j, s: (row0 // seqs + s, p * ncb + j))

    def const_cols(rows):
        return pl.BlockSpec((rows, cblk), lambda j, s: (0, j))

    mat = pl.BlockSpec((L, L), lambda j, s: (0, 0))
    conv_b2 = conv_b.reshape(1, 3 * D_MODEL)
    in_specs = [part(0, L), part(1, L), part(2, L),
                part(0, 3), part(1, 3), part(2, 3),
                part(0, 1), part(1, 1), part(2, 1),
                const_cols(L), const_cols(L), const_cols(L), const_cols(1),
                mat, mat, mat, mat]
    args = [u, u, u, conv_w, conv_w, conv_w, conv_b2, conv_b2, conv_b2,
            kr, qq, krn, dskip.reshape(1, D_MODEL), fc, fs, gc, gs]
    return pl.pallas_call(
        _hyena_conv_kernel,
        grid=(ncb, n_seq // seqs),
        in_specs=in_specs,
        out_specs=pl.BlockSpec((seqs * L, cblk), lambda j, s: (s, j)),
        out_shape=jax.ShapeDtypeStruct((n_seq * L, D_MODEL), BF16),
        scratch_shapes=[pltpu.VMEM(unit, BF16), pltpu.VMEM(unit, F32), pltpu.VMEM(unit, F32),
                        pltpu.VMEM(unit, BF16), pltpu.VMEM(unit, BF16)],
        compiler_params=_params("arbitrary", "arbitrary"),
        name="hyena_conv_latent" if latent else "hyena_conv_prompt",
    )(*args)


def kernel(x_prompt, x_sample, cache_k, cache_v, state_hgrn, c, c_ctx, norm_g, mod_w, mod_b, ab_in_w, hgrn_lb, hgrn_onorm_g, attn_qnorm_g, attn_knorm_g, ab_out_w, hy_in_w, hy_in_b, hy_conv_w, hy_conv_b, hy_f_w1, hy_f_b1, hy_f_w2, hy_f_b2, hy_f_w3, hy_f_freq, hy_dskip, hy_out_w, router_w, router_b, moe_w_gate, moe_w_up, moe_w_down):
    xp = x_prompt.reshape(N_PROMPT_TOK, D_MODEL)
    xl = x_sample.reshape(N_LATENT_TOK, D_MODEL)
    cond = jnp.concatenate([c_ctx[None, :], c, jnp.zeros((N_COND - 1 - N_LATENT_SEQ, D_MODEL), F32)], axis=0)
    mod = _modulation(cond, mod_w, mod_b)
    router_wp = jnp.pad(router_w, ((0, 0), (0, ROUTER_LANES - N_EXPERTS)))

    z = _in_proj0(xp, xl, norm_g[0, 0], mod[0], ab_in_w[0])
    oa_p, new_state = _hgrn(z, hgrn_lb, hgrn_onorm_g[0], None, latent=False)
    oa_l = _hgrn(z, hgrn_lb, hgrn_onorm_g[0], state_hgrn, latent=True)
    ob_p, k_fm, v_fm = _attention_prompt(z, attn_qnorm_g[0], attn_knorm_g[0])
    fm_shape = (N_PROMPT_SEQ, 1, KV_HEADS, HEAD_DIM, PROMPT_LEN)
    new_k = jnp.swapaxes(k_fm.reshape(fm_shape), -1, -2)
    new_v = jnp.swapaxes(v_fm.reshape(fm_shape), -1, -2)
    ob_l = _attention_latent(z, attn_qnorm_g[0], attn_knorm_g[0], cache_k, cache_v)
    x, h, logits_t = _out_proj([(oa_p, oa_l), (ob_p, ob_l)], ab_out_w[0], (xp, xl), norm_g[0, 1], mod[0],
                               router_wp)
    (moe_out,) = _moe(h, logits_t, router_b, moe_w_gate, moe_w_up, moe_w_down, 0)

    x, u = _in_proj1(x, moe_out, mod[0], norm_g[1, 0], mod[1], hy_in_w[0], hy_in_b[0])
    pre = []
    for latent in (False, True):
        L = LATENT_LEN if latent else PROMPT_LEN
        tables = _dft_tables(L)
        spectrum = _hyena_filter_spectrum(L, hy_f_w1[0], hy_f_b1[0], hy_f_w2[0], hy_f_b2[0], hy_f_w3[0],
                                          hy_f_freq[0], tables[0], tables[1])
        pre.append(_hyena_conv(u, hy_conv_w[0], hy_conv_b[0], hy_dskip[0], spectrum, tables, latent=latent))
    x, h, logits_t = _out_proj([tuple(pre)], hy_out_w[0], (x,), norm_g[1, 1], mod[1], router_wp)
    trunks = ((0, N_PROMPT_TOK), (N_PROMPT_TOK, N_LATENT_TOK))
    out_p, out_l = _moe(h, logits_t, router_b, moe_w_gate, moe_w_up, moe_w_down, 1, ranges=trunks)

    y_prompt = _combine(x, out_p, mod[1], *trunks[0]).reshape(N_PROMPT_SEQ, PROMPT_LEN, D_MODEL)
    y_sample = _combine(x, out_l, mod[1], *trunks[1]).reshape(N_LATENT_SEQ, LATENT_LEN, D_MODEL)
    return (y_prompt, y_sample, new_k, new_v, new_state)
```

```python
import functools
import math

import numpy as np
import jax
import jax.numpy as jnp
from jax import lax
from jax.experimental import pallas as pl
from jax.experimental.pallas import tpu as pltpu
from jax.experimental.pallas import tpu_sc as plsc

F32 = jnp.float32
BF16 = jnp.bfloat16
HIGHEST = lax.Precision.HIGHEST

D_MODEL = 1024
N_PROMPT_SEQ = 32
PROMPT_LEN = 256
N_LATENT_SEQ = 2
LATENT_LEN = 1024
PAST_LEN = 512
GRID_W = 64
N_PROMPT_TOK = N_PROMPT_SEQ * PROMPT_LEN
N_LATENT_TOK = N_LATENT_SEQ * LATENT_LEN
N_TOK = N_PROMPT_TOK + N_LATENT_TOK
N_COND = 8
EPS = 1e-6

A_WIDTH = 512
A_HEADS = 4
A_DK = 128
CHUNK = 64
HGRN_BLOCK = 128
HGRN_HEADS_PER_STEP = 4
HEAD_DIM = 64
Q_HEADS = 8
KV_HEADS = 2
Q_PER_KV = Q_HEADS // KV_HEADS
Q_BLOCK = 256
ROPE_THETA = 10000.0
ROPE_PAIRS = HEAD_DIM // 4

HY_BANDS = 16
HY_FFN = 64
HY_DECAY_TARGET = 1e-2
HY_FAST_PCT = 0.3
HY_SLOW_PCT = 1.5

N_EXPERTS = 16
N_GROUPS = 4
EXPERTS_PER_GROUP = 4
TOP_K = 2
D_EXPERT = 512
ROUTER_LANES = 128
OUT_PROJ_SUB_ROWS = 256
EXPERT_SUB_ROWS = 256
EXPERT_TILES_PER_STEP = 2
IN_PROJ_SUB_ROWS = 256
MOE_TILE = 512
MOE_ROWS = N_TOK * TOP_K + N_EXPERTS * MOE_TILE
PLAN_LANES = 128

SC_CORES = 2
SC_WORKERS = 32
SC_TOKENS_PER_WORKER = N_TOK // SC_WORKERS
SC_CHUNK = 40
ROW_WORDS = D_MODEL // 2

VMEM_LIMIT = 56 * 1024 * 1024


def _params(*sem):
    return pltpu.CompilerParams(dimension_semantics=sem, vmem_limit_bytes=VMEM_LIMIT)


def _pack_rows(x):
    n = x.shape[1] // 2
    bits = pltpu.bitcast(x.astype(BF16).astype(F32), jnp.uint32)
    return pltpu.bitcast(bits[:, :n] | (bits[:, n:] >> 16), jnp.int32)


def _unpack_rows(p):
    bits = pltpu.bitcast(p, jnp.uint32)
    hi = pltpu.bitcast(bits & jnp.uint32(0xFFFF0000), F32)
    lo = pltpu.bitcast(bits << 16, F32)
    return jnp.concatenate([hi, lo], axis=1)


def _cond_of_token_block(i, block_rows):
    start = i * block_rows
    return jnp.where(start < N_PROMPT_TOK, 0, 1 + (start - N_PROMPT_TOK) // LATENT_LEN)


def _mod_kernel(cond_ref, w_ref, b_ref, o_ref):
    cnd = cond_ref[...]
    s = cnd * jax.nn.sigmoid(cnd)
    s_hi = s.astype(BF16)
    s_lo = (s - s_hi.astype(F32)).astype(BF16)
    w = w_ref[...]
    w_hi = w.astype(BF16)
    w_lo = (w - w_hi.astype(F32)).astype(BF16)
    acc = jnp.dot(s_hi, w_hi, preferred_element_type=F32)
    acc = acc + jnp.dot(s_lo, w_hi, preferred_element_type=F32)
    acc = acc + jnp.dot(s_hi, w_lo, preferred_element_type=F32)
    o_ref[...] = acc + b_ref[...]


def _modulation(cond, mod_w, mod_b):
    depth = mod_w.shape[0]
    n_mod = 6
    cols = 2 * D_MODEL
    n_step = n_mod * D_MODEL // cols
    out = pl.pallas_call(
        _mod_kernel,
        grid=(depth, n_step),
        in_specs=[
            pl.BlockSpec((N_COND, D_MODEL), lambda l, j: (0, 0)),
            pl.BlockSpec((None, D_MODEL, cols), lambda l, j: (l, 0, j)),
            pl.BlockSpec((None, 1, cols), lambda l, j: (l, 0, j)),
        ],
        out_specs=pl.BlockSpec((None, N_COND, cols), lambda l, j: (l, 0, j)),
        out_shape=jax.ShapeDtypeStruct((depth, N_COND, n_mod * D_MODEL), F32),
        compiler_params=_params("arbitrary", "arbitrary"),
        name="modulation",
    )(cond, mod_w, mod_b.reshape(depth, 1, n_mod * D_MODEL))
    return out.reshape(depth, N_COND, n_mod, D_MODEL)


def _modulated_norm(x, g, mod, shift_row, scale_row):
    ms = jnp.mean(x * x, axis=-1, keepdims=True)
    y = x * lax.rsqrt(ms + EPS) * g
    return y * (1.0 + mod[scale_row:scale_row + 1, :]) + mod[shift_row:shift_row + 1, :]


def _trunk_specs(block_rows, width):
    n_prompt_blocks = N_PROMPT_TOK // block_rows
    return (pl.BlockSpec((block_rows, width), lambda i: (jnp.minimum(i, n_prompt_blocks - 1), 0)),
            pl.BlockSpec((block_rows, width), lambda i: (jnp.maximum(i - n_prompt_blocks, 0), 0)))


def _select_trunk(p_ref, l_ref, rows=slice(None)):
    block_rows = p_ref.shape[0]
    return jnp.where(pl.program_id(0) < N_PROMPT_TOK // block_rows, p_ref[rows, :], l_ref[rows, :])


def _cast_once(w_ref, wb_ref):
    @pl.when(pl.program_id(0) == 0)
    def _():
        wb_ref[...] = w_ref[...].astype(BF16)


def _resident(shape):
    return pl.BlockSpec(shape, lambda i: tuple(0 for _ in shape), pipeline_mode=pl.Buffered(1))


def _mod_spec(block_rows):
    return pl.BlockSpec((None, 6, D_MODEL), lambda i: (_cond_of_token_block(i, block_rows), 0, 0))


def _in_proj0_kernel(xp_ref, xl_ref, g_ref, mod_ref, w_ref, o_ref, wb_ref, hb_ref):
    _cast_once(w_ref, wb_ref)
    n = IN_PROJ_SUB_ROWS
    n_sub = xp_ref.shape[0] // n

    def prepare(r):
        x = _select_trunk(xp_ref, xl_ref, slice(r * n, (r + 1) * n))
        hb_ref[r] = _modulated_norm(x, g_ref[...], mod_ref[...], 0, 1).astype(BF16)

    def project(r):
        u = jnp.dot(hb_ref[r], wb_ref[...], preferred_element_type=F32)
        o_ref[r * n:(r + 1) * n, :] = u.astype(o_ref.dtype)

    prepare(0)
    for r in range(1, n_sub):
        prepare(r)
        project(r - 1)
    project(n_sub - 1)


def _in_proj0(x_prompt, x_latent, g, mod_l, w, block_rows=512):
    n = w.shape[1]
    return pl.pallas_call(
        _in_proj0_kernel,
        grid=(N_TOK // block_rows,),
        in_specs=[*_trunk_specs(block_rows, D_MODEL), _resident((1, D_MODEL)), _mod_spec(block_rows),
                  _resident((D_MODEL, n))],
        out_specs=pl.BlockSpec((block_rows, n), lambda i: (i, 0)),
        out_shape=jax.ShapeDtypeStruct((N_TOK, n), BF16),
        scratch_shapes=[pltpu.VMEM((D_MODEL, n), BF16),
                        pltpu.VMEM((block_rows // IN_PROJ_SUB_ROWS, IN_PROJ_SUB_ROWS, D_MODEL), BF16)],
        compiler_params=_params("arbitrary"),
        name="in_proj0",
    )(x_prompt, x_latent, g.reshape(1, D_MODEL), mod_l, w)


def _moe_mix(x_ref, ya_ref, yb_ref, wt_ref, mod_ref, rows=slice(None)):
    wt = wt_ref[rows, :]
    mix = wt[:, 0:1] * _unpack_rows(ya_ref[rows, :]) + wt[:, 1:2] * _unpack_rows(yb_ref[rows, :])
    return x_ref[rows, :] + mod_ref[5:6, :] * mix


def _in_proj1_kernel(x_ref, ya_ref, yb_ref, wt_ref, modp_ref, g_ref, mod_ref, w_ref, b_ref, xo_ref, o_ref,
                     wb_ref, hb_ref):
    _cast_once(w_ref, wb_ref)
    n = IN_PROJ_SUB_ROWS
    n_sub = x_ref.shape[0] // n

    def prepare(r):
        rows = slice(r * n, (r + 1) * n)
        x = _moe_mix(x_ref, ya_ref, yb_ref, wt_ref, modp_ref, rows)
        xo_ref[rows, :] = x
        hb_ref[r] = _modulated_norm(x, g_ref[...], mod_ref[...], 0, 1).astype(BF16)

    def project(r):
        rows = slice(r * n, (r + 1) * n)
        u = jnp.dot(hb_ref[r], wb_ref[...], preferred_element_type=F32) + b_ref[...]
        o_ref[rows, :] = u.astype(o_ref.dtype)

    prepare(0)
    for r in range(1, n_sub):
        prepare(r)
        project(r - 1)
    project(n_sub - 1)


def _in_proj1(x, moe_out, mod_prev, g, mod_l, w, bias, block_rows=512):
    ya, yb, w_tok = moe_out
    n = w.shape[1]
    tok = pl.BlockSpec((block_rows, D_MODEL), lambda i: (i, 0))
    packed = pl.BlockSpec((block_rows, ROW_WORDS), lambda i: (i, 0))
    return pl.pallas_call(
        _in_proj1_kernel,
        grid=(N_TOK // block_rows,),
        in_specs=[tok, packed, packed, pl.BlockSpec((block_rows, TOP_K), lambda i: (i, 0)), _mod_spec(block_rows),
                  _resident((1, D_MODEL)), _mod_spec(block_rows), _resident((D_MODEL, n)), _resident((1, n))],
        out_specs=(tok, pl.BlockSpec((block_rows, n), lambda i: (i, 0))),
        out_shape=(jax.ShapeDtypeStruct((N_TOK, D_MODEL), F32), jax.ShapeDtypeStruct((N_TOK, n), BF16)),
        scratch_shapes=[pltpu.VMEM((D_MODEL, n), BF16),
                        pltpu.VMEM((block_rows // IN_PROJ_SUB_ROWS, IN_PROJ_SUB_ROWS, D_MODEL), BF16)],
        compiler_params=_params("arbitrary"),
        name="in_proj1",
    )(x, ya, yb, w_tok, mod_prev, g.reshape(1, D_MODEL), mod_l, w, bias.reshape(1, n))


def _hgrn_kernel(*refs, seq_len, with_state):
    if with_state:
        (q_ref, zf_ref, zb_ref, i_ref, ga_ref, lb_ref, og_ref, s0_ref, o_ref, of_ref, ob_ref) = refs
    else:
        (q_ref, zf_ref, zb_ref, i_ref, ga_ref, lb_ref, og_ref, o_ref, s_ref, of_ref, ob_ref) = refs
    n_blocks = seq_len // HGRN_BLOCK
    chunks_per_block = HGRN_BLOCK // CHUNK

    lbr = lb_ref[...]
    mx = jnp.maximum(lbr[0], lbr[1])
    e0 = jnp.exp(lbr[0] - mx)
    e1 = jnp.exp(lbr[1] - mx)
    lb = e0 / (e0 + e1)

    row = lax.broadcasted_iota(jnp.int32, (HGRN_BLOCK, HGRN_BLOCK), 0)
    col = lax.broadcasted_iota(jnp.int32, (HGRN_BLOCK, HGRN_BLOCK), 1)
    same_chunk = (row // CHUNK) == (col // CHUNK)
    nt = (((1,), (1,)), ((), ()))
    tn = (((0,), (0,)), ((), ()))

    def per_chunk_row(x, idx):
        return jnp.concatenate(
            [jnp.broadcast_to(x[n * CHUNK + idx:n * CHUNK + idx + 1, :], (CHUNK, x.shape[1]))
             for n in range(chunks_per_block)], axis=0)

    def in_chunk_cumsum(tri, x):
        hi = x.astype(BF16)
        lo = (x - hi.astype(F32)).astype(BF16)
        return jnp.dot(tri, hi, preferred_element_type=F32) + jnp.dot(tri, lo, preferred_element_type=F32)

    def prepare(blk, cols, z_ref, lbd, forward):
        rows = slice(blk * HGRN_BLOCK, (blk + 1) * HGRN_BLOCK)
        keep = (same_chunk & (col <= row)) if forward else (same_chunk & (col >= row))
        tri = jnp.where(keep, 1.0, 0.0).astype(BF16)
        mid = CHUNK // 2 if forward else CHUNK - 1 - CHUNK // 2
        last = CHUNK - 1 if forward else 0
        f = lbd + (1.0 - lbd) * jax.nn.sigmoid(z_ref[rows, cols].astype(F32))
        lf = jnp.log(f)
        k = 1.0 - f
        q = q_ref[rows, cols].astype(F32)
        b = in_chunk_cumsum(tri, lf)
        bm = per_chunk_row(b, mid)
        bl = per_chunk_row(b, last)
        return dict(
            rows=rows, cols=cols, keep=keep, forward=forward,
            vb=i_ref[rows, cols].astype(BF16),
            qe=(q * jnp.exp(b - bm)).astype(BF16), ke=(k * jnp.exp(bm - b)).astype(BF16),
            qb=(q * jnp.exp(b)).astype(BF16), ks=(k * jnp.exp(bl - b)).astype(BF16), decay=jnp.exp(bl))

    def within_chunks(u):
        att = lax.dot_general(u["qe"], u["ke"], nt, preferred_element_type=F32)
        att = jnp.where(u["keep"], att, 0.0)
        u["o_intra"] = jnp.dot(att.astype(BF16), u["vb"], preferred_element_type=F32)
        u["upd"] = [lax.dot_general(u["vb"][n * CHUNK:(n + 1) * CHUNK], u["ks"][n * CHUNK:(n + 1) * CHUNK], tn,
                                    preferred_element_type=F32) for n in range(chunks_per_block)]

    def across_chunks(u, st, out_ref):
        order = range(chunks_per_block) if u["forward"] else range(chunks_per_block - 1, -1, -1)
        o_inter = [None] * chunks_per_block
        for n in order:
            cr = slice(n * CHUNK, (n + 1) * CHUNK)
            o_inter[n] = lax.dot_general(u["qb"][cr], st.astype(BF16), nt, preferred_element_type=F32)
            st = st * u["decay"][n * CHUNK:n * CHUNK + 1, :] + u["upd"][n]
        out_ref[u["rows"], u["cols"]] = u["o_intra"] + jnp.concatenate(o_inter, axis=0)
        return st

    n_heads = q_ref.shape[1] // A_DK
    head_cols = [slice(hd * A_DK, (hd + 1) * A_DK) for hd in range(n_heads)]
    if with_state:
        states = {(hd, d): s0_ref[d, hd].T for hd in range(n_heads) for d in range(2)}
    else:
        states = {(hd, d): jnp.zeros((A_DK, A_DK), F32) for hd in range(n_heads) for d in range(2)}
    for step in range(n_blocks):
        units = {}
        for hd, cols in enumerate(head_cols):
            units[hd, 0] = prepare(step, cols, zf_ref, lb[0:1, cols], True)
            units[hd, 1] = prepare(n_blocks - 1 - step, cols, zb_ref, lb[1:2, cols], False)
        for u in units.values():
            within_chunks(u)
        for key, u in units.items():
            states[key] = across_chunks(u, states[key], of_ref if key[1] == 0 else ob_ref)
    for hd, cols in enumerate(head_cols):
        if not with_state:
            s_ref[0, hd] = states[hd, 0].T
            s_ref[1, hd] = states[hd, 1].T
        o = of_ref[:, cols] + ob_ref[:, cols]
        o = o * lax.rsqrt(jnp.mean(o * o, axis=-1, keepdims=True) + EPS) * og_ref[:, cols]
        ga = ga_ref[:, cols].astype(F32)
        o_ref[:, cols] = (o * (ga * jax.nn.sigmoid(ga))).astype(o_ref.dtype)


def _hgrn(z, hgrn_lb, onorm_g, state, *, latent):
    seq_len = LATENT_LEN if latent else PROMPT_LEN
    n_seq = N_LATENT_SEQ if latent else N_PROMPT_SEQ
    row0 = (N_PROMPT_TOK // seq_len) if latent else 0

    hw = HGRN_HEADS_PER_STEP * A_DK
    n_hg = A_HEADS // HGRN_HEADS_PER_STEP

    def zspec(part):
        return pl.BlockSpec((seq_len, hw), lambda s, h: (row0 + s, part * n_hg + h))

    in_specs = [zspec(0), zspec(1), zspec(2), zspec(3), zspec(4),
                pl.BlockSpec((2, 2, hw), lambda s, h: (0, 0, h)),
                pl.BlockSpec((1, hw), lambda s, h: (0, h))]
    args = [z, z, z, z, z, hgrn_lb, onorm_g.reshape(1, A_WIDTH)]
    state_spec = pl.BlockSpec((None, None, 2, HGRN_HEADS_PER_STEP, A_DK, A_DK), lambda s, h: (s, 0, 0, h, 0, 0))
    o_shape = jax.ShapeDtypeStruct((n_seq * seq_len, A_WIDTH), BF16)
    o_spec = pl.BlockSpec((seq_len, hw), lambda s, h: (s, h))
    if latent:
        in_specs.append(state_spec)
        args.append(state)
        out_shape, out_specs = o_shape, o_spec
    else:
        out_shape = (o_shape, jax.ShapeDtypeStruct((n_seq, 1, 2, A_HEADS, A_DK, A_DK), F32))
        out_specs = (o_spec, state_spec)
    return pl.pallas_call(
        functools.partial(_hgrn_kernel, seq_len=seq_len, with_state=latent),
        grid=(n_seq, n_hg),
        in_specs=in_specs,
        out_specs=out_specs,
        out_shape=out_shape,
        scratch_shapes=[pltpu.VMEM((seq_len, hw), F32), pltpu.VMEM((seq_len, hw), F32)],
        compiler_params=_params("arbitrary", "arbitrary"),
        name="hgrn_latent" if latent else "hgrn_prompt",
    )(*args)


def _rope_tables():
    pos = np.arange(LATENT_LEN)
    row, colp = pos // GRID_W, pos % GRID_W
    inv = ROPE_THETA ** (-np.arange(ROPE_PAIRS, dtype=np.float32) / ROPE_PAIRS)
    inv = inv.astype(np.float32)
    ang_r = (row.astype(np.float32)[:, None] * inv).astype(np.float32)
    ang_c = (colp.astype(np.float32)[:, None] * inv).astype(np.float32)
    cos = np.concatenate([np.cos(ang_r), np.cos(ang_r), np.cos(ang_c), np.cos(ang_c)], axis=1)
    sin = np.concatenate([-np.sin(ang_r), np.sin(ang_r), -np.sin(ang_c), np.sin(ang_c)], axis=1)
    return cos.astype(np.float32), sin.astype(np.float32)


def _head_mean_matrix(width):
    idx = np.arange(width) // HEAD_DIM
    return jnp.asarray((idx[:, None] == idx[None, :]).astype(np.float32) / HEAD_DIM).astype(BF16)


def _attn_kernel(*refs, latent):
    if latent:
        (q_ref, k_ref, v_ref, qg_ref, kg_ref, gq_ref, gk_ref, cosq_ref, sinq_ref, cosk_ref, sink_ref,
         ck_ref, cv_ref, o_ref) = refs
    else:
        (q_ref, k_ref, v_ref, qg_ref, kg_ref, gq_ref, gk_ref, o_ref, kout_ref, vout_ref) = refs
    pair_w = 2 * HEAD_DIM

    def head_norm(x, mean_ref, gain):
        sq = x * x
        hi = sq.astype(BF16)
        lo = (sq - hi.astype(F32)).astype(BF16)
        ms = jnp.dot(hi, mean_ref[...], preferred_element_type=F32)
        ms = ms + jnp.dot(lo, mean_ref[...], preferred_element_type=F32)
        return x * lax.rsqrt(ms + EPS) * gain

    def rope(x, cos, sin):
        n = x.shape[1]
        lane = lax.broadcasted_iota(jnp.int32, x.shape, 1)
        first_of_pair = (lane // ROPE_PAIRS) % 2 == 0
        swapped = jnp.where(first_of_pair, pltpu.roll(x, n - ROPE_PAIRS, axis=1), pltpu.roll(x, ROPE_PAIRS, axis=1))
        return x * cos + swapped * sin

    nt = (((1,), (1,)), ((), ()))

    def prepare(rows, seq_idx):
        q = head_norm(q_ref[rows, :].astype(F32), gq_ref, qg_ref[...])
        k = head_norm(k_ref[rows, :].astype(F32), gk_ref, kg_ref[...])
        if latent:
            q = rope(q, cosq_ref[...], sinq_ref[...])
            k = rope(k, cosk_ref[...], sink_ref[...])
        q = q * (HEAD_DIM ** -0.5)
        v = v_ref[rows, :].astype(F32)
        n_q = q.shape[0]
        low_kv = lax.broadcasted_iota(jnp.int32, k.shape, 1) < HEAD_DIM
        low_q = lax.broadcasted_iota(jnp.int32, (n_q, pair_w), 1) < HEAD_DIM
        k_swapped = pltpu.roll(k, HEAD_DIM, axis=1)
        v_swapped = pltpu.roll(v, HEAD_DIM, axis=1)
        if not latent:
            kout_ref[seq_idx] = k.T
            vout_ref[seq_idx] = v.T
        units = []
        for j in range(KV_HEADS):
            kd = (jnp.where(low_kv, k, k_swapped) if j == 0 else jnp.where(low_kv, k_swapped, k)).astype(BF16)
            vd = (jnp.where(low_kv, v, v_swapped) if j == 0 else jnp.where(low_kv, v_swapped, v)).astype(BF16)
            vd = jnp.concatenate([vd, jnp.ones_like(vd)], axis=1)
            tiles = range(j * Q_PER_KV // 2, (j + 1) * Q_PER_KV // 2)
            parts = []
            for t in tiles:
                qt = q[:, t * pair_w:(t + 1) * pair_w]
                parts += [jnp.where(low_q, qt, 0.0), jnp.where(low_q, 0.0, qt)]
            units.append(dict(j=j, rows=rows, tiles=tiles, n_q=n_q, low_q=low_q, kd=kd, vd=vd,
                              qs=jnp.concatenate(parts, axis=0).astype(BF16)))
        return units

    def scores(u):
        u["s_new"] = lax.dot_general(u["qs"], u["kd"], nt, preferred_element_type=F32)
        if latent:
            j = u["j"]
            cvd = jnp.concatenate([cv_ref[j], cv_ref[j]], axis=1).astype(BF16)
            u["cvd"] = jnp.concatenate([cvd, jnp.ones_like(cvd)], axis=1)
            ckd = jnp.concatenate([ck_ref[j], ck_ref[j]], axis=1).astype(BF16)
            u["s_old"] = lax.dot_general(u["qs"], ckd, nt, preferred_element_type=F32)

    def softmax(u):
        m = jnp.max(u["s_new"], axis=-1, keepdims=True)
        if latent:
            m = jnp.maximum(m, jnp.max(u["s_old"], axis=-1, keepdims=True))
        u["p_new"] = jnp.exp(u.pop("s_new") - m).astype(BF16)
        if latent:
            u["p_old"] = jnp.exp(u.pop("s_old") - m).astype(BF16)

    def weighted_values(u):
        acc = jnp.dot(u["p_new"], u["vd"], preferred_element_type=F32)
        if latent:
            acc = acc + jnp.dot(u["p_old"], u["cvd"], preferred_element_type=F32)
        out = acc[:, :pair_w] / acc[:, pair_w:]
        n_q = u["n_q"]
        for i, t in enumerate(u["tiles"]):
            lo_head = out[(2 * i) * n_q:(2 * i + 1) * n_q, :]
            hi_head = out[(2 * i + 1) * n_q:(2 * i + 2) * n_q, :]
            o_ref[u["rows"], t * pair_w:(t + 1) * pair_w] = jnp.where(u["low_q"], lo_head, hi_head).astype(o_ref.dtype)

    if latent:
        units = prepare(slice(None), None)
    else:
        seq = PROMPT_LEN
        units = [u for s in range(q_ref.shape[0] // seq) for u in prepare(slice(s * seq, (s + 1) * seq), s)]
    for phase in (scores, softmax, weighted_values):
        for u in units:
            phase(u)


def _attn_common_args(qn_g, kn_g):
    q_w, kv_w = Q_HEADS * HEAD_DIM, KV_HEADS * HEAD_DIM
    return (jnp.tile(qn_g, Q_HEADS).reshape(1, q_w), jnp.tile(kn_g, KV_HEADS).reshape(1, kv_w),
            _head_mean_matrix(q_w), _head_mean_matrix(kv_w))


def _attention_prompt(z, qn_g, kn_g):
    seqs = 8
    L = seqs * PROMPT_LEN
    cache_shape = jax.ShapeDtypeStruct((N_PROMPT_SEQ, KV_HEADS * HEAD_DIM, PROMPT_LEN), F32)
    cache_spec = pl.BlockSpec((seqs, KV_HEADS * HEAD_DIM, PROMPT_LEN), lambda s: (s, 0, 0))
    q_w, kv_w = Q_HEADS * HEAD_DIM, KV_HEADS * HEAD_DIM
    q_col = (5 * A_WIDTH) // q_w
    k_col = (5 * A_WIDTH + q_w) // kv_w
    const = lambda r, c: pl.BlockSpec((r, c), lambda s: (0, 0))
    return pl.pallas_call(
        functools.partial(_attn_kernel, latent=False),
        grid=(N_PROMPT_TOK // L,),
        in_specs=[
            pl.BlockSpec((L, q_w), lambda s: (s, q_col)),
            pl.BlockSpec((L, kv_w), lambda s: (s, k_col)),
            pl.BlockSpec((L, kv_w), lambda s: (s, k_col + 1)),
            const(1, q_w), const(1, kv_w), const(q_w, q_w), const(kv_w, kv_w),
        ],
        out_specs=(pl.BlockSpec((L, q_w), lambda s: (s, 0)), cache_spec, cache_spec),
        out_shape=(jax.ShapeDtypeStruct((N_PROMPT_TOK, q_w), BF16), cache_shape, cache_shape),
        compiler_params=_params("arbitrary"),
        name="attn_prompt",
    )(z, z, z, *_attn_common_args(qn_g, kn_g))


def _attention_latent(z, qn_g, kn_g, cache_k, cache_v):
    L = LATENT_LEN
    nqb = L // Q_BLOCK
    q_w, kv_w = Q_HEADS * HEAD_DIM, KV_HEADS * HEAD_DIM
    q_col = (5 * A_WIDTH) // q_w
    k_col = (5 * A_WIDTH + q_w) // kv_w
    qrow0 = N_PROMPT_TOK // Q_BLOCK
    krow0 = N_PROMPT_TOK // L
    cos, sin = _rope_tables()
    cos_q, sin_q = jnp.asarray(np.tile(cos, (1, Q_HEADS))), jnp.asarray(np.tile(sin, (1, Q_HEADS)))
    cos_k, sin_k = jnp.asarray(np.tile(cos, (1, KV_HEADS))), jnp.asarray(np.tile(sin, (1, KV_HEADS)))
    const = lambda r, c: pl.BlockSpec((r, c), lambda s, b: (0, 0))
    cache_spec = pl.BlockSpec((None, None, KV_HEADS, PAST_LEN, HEAD_DIM), lambda s, b: (s, 0, 0, 0, 0))
    return pl.pallas_call(
        functools.partial(_attn_kernel, latent=True),
        grid=(N_LATENT_SEQ, nqb),
        in_specs=[
            pl.BlockSpec((Q_BLOCK, q_w), lambda s, b: (qrow0 + s * nqb + b, q_col)),
            pl.BlockSpec((L, kv_w), lambda s, b: (krow0 + s, k_col)),
            pl.BlockSpec((L, kv_w), lambda s, b: (krow0 + s, k_col + 1)),
            const(1, q_w), const(1, kv_w), const(q_w, q_w), const(kv_w, kv_w),
            pl.BlockSpec((Q_BLOCK, q_w), lambda s, b: (b, 0)),
            pl.BlockSpec((Q_BLOCK, q_w), lambda s, b: (b, 0)),
            const(L, kv_w), const(L, kv_w),
            cache_spec, cache_spec,
        ],
        out_specs=pl.BlockSpec((Q_BLOCK, q_w), lambda s, b: (s * nqb + b, 0)),
        out_shape=jax.ShapeDtypeStruct((N_LATENT_TOK, q_w), BF16),
        compiler_params=_params("arbitrary", "arbitrary"),
        name="attn_latent",
    )(z, z, z, *_attn_common_args(qn_g, kn_g), cos_q, sin_q, cos_k, sin_k, cache_k, cache_v)


def _out_proj_kernel(*refs, n_act, n_x):
    a_refs = refs[:2 * n_act]
    x_refs = refs[2 * n_act:2 * n_act + n_x]
    g_ref, mod_ref, rw_ref, w_ref, xo_ref, h_ref, lg_ref, wb_ref, rws_ref, acc_ref = refs[2 * n_act + n_x:]
    _cast_once(w_ref, wb_ref)

    @pl.when(pl.program_id(0) == 0)
    def _():
        rw = rw_ref[...]
        hi = rw.astype(BF16).astype(F32)
        lo = (rw - hi).astype(BF16).astype(F32)
        rws_ref[...] = (hi + pltpu.roll(lo, N_EXPERTS, axis=1)).astype(BF16)

    mod = mod_ref[...]
    n = OUT_PROJ_SUB_ROWS

    n_sub = xo_ref.shape[0] // n

    def sub_rows(r):
        if isinstance(r, int):
            return slice(r * n, (r + 1) * n)
        return pl.ds(pl.multiple_of(r * n, n), n)

    def project(r):
        rows = sub_rows(r)
        acc = None
        k0 = 0
        for ap_ref, al_ref in zip(a_refs[0::2], a_refs[1::2]):
            k1 = k0 + ap_ref.shape[1]
            part = jnp.dot(_select_trunk(ap_ref, al_ref, rows), wb_ref[k0:k1, :], preferred_element_type=F32)
            acc = part if acc is None else acc + part
            k0 = k1
        acc_ref[r % 2] = acc

    def finish(r):
        rows = sub_rows(r)
        x_in = x_refs[0][rows, :] if n_x == 1 else _select_trunk(*x_refs, rows)
        x = x_in + mod[2:3, :] * acc_ref[r % 2]
        xo_ref[rows, :] = x
        h = _modulated_norm(x, g_ref[...], mod, 3, 4)
        h_ref[rows, :] = _pack_rows(h)
        h_hi = h.astype(BF16)
        h_lo = (h - h_hi.astype(F32)).astype(BF16)
        both = jnp.dot(jnp.concatenate([h_hi, h_lo], axis=0), rws_ref[...], preferred_element_type=F32)
        from_hi, from_lo = both[:n], both[n:]
        lg = from_hi + pltpu.roll(from_hi, ROUTER_LANES - N_EXPERTS, axis=1) + from_lo
        lg_ref[:, rows] = lg.T[:N_EXPERTS, :]

    project(0)
    for r in range(n_sub - 1):
        project(r + 1)
        finish(r)
    finish(n_sub - 1)


def _out_proj(acts, w, xs, g, mod_l, router_wp, block_rows=1024):
    tok = lambda width: pl.BlockSpec((block_rows, width), lambda i: (i, 0))
    in_specs = [spec for ap, _ in acts for spec in _trunk_specs(block_rows, ap.shape[1])]
    in_specs += [tok(D_MODEL)] if len(xs) == 1 else list(_trunk_specs(block_rows, D_MODEL))
    in_specs += [_resident((1, D_MODEL)), _mod_spec(block_rows), _resident((D_MODEL, ROUTER_LANES)),
                 _resident(w.shape)]
    return pl.pallas_call(
        functools.partial(_out_proj_kernel, n_act=len(acts), n_x=len(xs)),
        grid=(N_TOK // block_rows,),
        in_specs=in_specs,
        out_specs=(tok(D_MODEL), tok(ROW_WORDS), pl.BlockSpec((N_EXPERTS, block_rows), lambda i: (0, i))),
        out_shape=(jax.ShapeDtypeStruct((N_TOK, D_MODEL), F32),
                   jax.ShapeDtypeStruct((N_TOK, ROW_WORDS), jnp.int32),
                   jax.ShapeDtypeStruct((N_EXPERTS, N_TOK), F32)),
        scratch_shapes=[pltpu.VMEM(w.shape, BF16), pltpu.VMEM((D_MODEL, ROUTER_LANES), BF16),
                        pltpu.VMEM((2, OUT_PROJ_SUB_ROWS, D_MODEL), F32)],
        compiler_params=_params("arbitrary"),
        name="out_proj",
    )(*[a for pair in acts for a in pair], *xs, g.reshape(1, D_MODEL), mod_l, router_wp, w)


def _router_kernel(lg_ref, rb_ref, pos_ref, w_ref, plan_ref, rank_ref):
    lg = lg_ref[...]
    ex = jnp.exp(lg - jnp.max(lg, axis=0, keepdims=True))
    scores = ex / jnp.sum(ex, axis=0, keepdims=True)
    biased = scores + rb_ref[...]
    expert = lax.broadcasted_iota(jnp.int32, biased.shape, 0)
    in_pos = expert % EXPERTS_PER_GROUP
    rank = jnp.zeros_like(biased)
    for d in range(1, EXPERTS_PER_GROUP):
        wraps = in_pos + d >= EXPERTS_PER_GROUP
        partner = jnp.where(wraps, pltpu.roll(biased, EXPERTS_PER_GROUP - d, axis=0),
                            pltpu.roll(biased, N_EXPERTS - d, axis=0))
        rank = rank + jnp.where(wraps, jnp.where(partner >= biased, 1.0, 0.0), jnp.where(partner > biased, 1.0, 0.0))
    selected = rank < 1.5
    contrib = jnp.where(selected, biased, 0.0)
    group_score = []
    for gi in range(N_GROUPS):
        r = [contrib[gi * EXPERTS_PER_GROUP + i:gi * EXPERTS_PER_GROUP + i + 1, :] for i in range(EXPERTS_PER_GROUP)]
        group_score.append(((r[0] + r[1]) + r[2]) + r[3])
    best = group_score[0]
    best_group = jnp.zeros_like(best)
    for gi in range(1, N_GROUPS):
        better = group_score[gi] > best
        best_group = jnp.where(better, float(gi), best_group)
        best = jnp.where(better, group_score[gi], best)
    in_group = (expert // EXPERTS_PER_GROUP).astype(F32) == best_group
    chosen = jnp.where(selected, jnp.where(in_group, 1.0, 0.0), 0.0)
    picked = chosen * scores
    gates = picked / jnp.sum(picked, axis=0, keepdims=True)
    lanes = 128
    n_blk = N_TOK // lanes
    li = lax.broadcasted_iota(jnp.int32, (lanes, lanes), 0)
    lj = lax.broadcasted_iota(jnp.int32, (lanes, lanes), 1)
    prefix = jnp.where(li <= lj, 1.0, 0.0).astype(BF16)
    stacked = jnp.concatenate([chosen[:, blk * lanes:(blk + 1) * lanes] for blk in range(n_blk)], axis=0)
    incl_all = jnp.dot(stacked.astype(BF16), prefix, preferred_element_type=F32)
    carry = jnp.zeros((N_EXPERTS, 1), F32)
    for blk in range(n_blk):
        cols = slice(blk * lanes, (blk + 1) * lanes)
        incl = incl_all[blk * N_EXPERTS:(blk + 1) * N_EXPERTS, :]
        rank_ref[:, cols] = incl - chosen[:, cols] + carry
        carry = carry + incl[:, lanes - 1:lanes]
    count = carry
    padded = jnp.floor((count + float(MOE_TILE - 1)) * (1.0 / MOE_TILE)) * float(MOE_TILE)
    erow = lax.broadcasted_iota(jnp.int32, (N_EXPERTS, 1), 0)
    offset = jnp.zeros((N_EXPERTS, 1), F32)
    for e in range(N_EXPERTS - 1):
        offset = offset + jnp.where(erow > e, padded[e:e + 1, :], 0.0)
    position = rank_ref[...] + offset
    ei = lax.broadcasted_iota(jnp.int32, (N_EXPERTS, N_EXPERTS), 0)
    ej = lax.broadcasted_iota(jnp.int32, (N_EXPERTS, N_EXPERTS), 1)
    lower = jnp.where(ej <= ei, 1.0, 0.0).astype(BF16)
    seen = jnp.dot(lower, chosen.astype(BF16), preferred_element_type=F32)
    first = (chosen > 0.5) & (seen < 1.5)
    second = (chosen > 0.5) & (seen > 1.5)
    pick = lambda flag, x: jnp.sum(jnp.where(flag, x, 0.0), axis=0, keepdims=True)
    pos_ref[0:1, :] = pick(first, position).astype(jnp.int32)
    pos_ref[1:2, :] = pick(second, position).astype(jnp.int32)
    w_rows = jnp.concatenate([pick(first, gates), pick(second, gates), jnp.zeros((6, N_TOK), F32)], axis=0)
    ti = lax.broadcasted_iota(jnp.int32, (8, lanes), 0)
    tj = lax.broadcasted_iota(jnp.int32, (8, lanes), 1)
    eye = jnp.where(ti == tj, 1.0, 0.0).astype(BF16)
    tn = (((0,), (0,)), ((), ()))
    hi = w_rows.astype(BF16)
    r1 = w_rows - hi.astype(F32)
    mid = r1.astype(BF16)
    lo = (r1 - mid.astype(F32)).astype(BF16)
    w_cols = lax.dot_general(hi, eye, tn, preferred_element_type=F32)
    w_cols = w_cols + lax.dot_general(mid, eye, tn, preferred_element_type=F32)
    w_cols = w_cols + lax.dot_general(lo, eye, tn, preferred_element_type=F32)
    w_ref[...] = w_cols[:, :TOP_K]
    start = (lax.broadcasted_iota(jnp.int32, (N_EXPERTS, lanes), 1) * MOE_TILE).astype(F32)
    end = offset + padded
    tile_expert = jnp.sum(jnp.where(end <= start, 1.0, 0.0), axis=0, keepdims=True)
    inside = (offset <= start) & (start < end)
    real = jnp.clip(count - (start - offset), 0.0, float(MOE_TILE))
    tile_rows = jnp.sum(jnp.where(inside, real, 0.0), axis=0, keepdims=True)
    plan_ref[0:1, :] = jnp.minimum(tile_expert, float(N_EXPERTS - 1)).astype(jnp.int32)
    plan_ref[1:2, :] = tile_rows.astype(jnp.int32)


def _router(logits_t, router_b):
    whole = lambda shape: pl.BlockSpec(shape, lambda i: (0, 0))
    return pl.pallas_call(
        _router_kernel,
        grid=(1,),
        in_specs=[whole((N_EXPERTS, N_TOK)), whole((N_EXPERTS, 1))],
        out_specs=(whole((2, N_TOK)), whole((N_TOK, TOP_K)), whole((2, 128))),
        out_shape=(jax.ShapeDtypeStruct((2, N_TOK), jnp.int32),
                   jax.ShapeDtypeStruct((N_TOK, TOP_K), F32),
                   jax.ShapeDtypeStruct((2, 128), jnp.int32)),
        scratch_shapes=[pltpu.VMEM((N_EXPERTS, N_TOK), F32)],
        compiler_params=_params("arbitrary"),
        name="router",
    )(logits_t, router_b.reshape(N_EXPERTS, 1))


def _sc_mesh():
    return plsc.VectorSubcoreMesh(core_axis_name="c", subcore_axis_name="s")


def _sc_worker_base():
    return (lax.axis_index("s") * SC_CORES + lax.axis_index("c")) * SC_TOKENS_PER_WORKER


def _moe_dispatch(h, pos_a, pos_b):
    n_chunks = SC_TOKENS_PER_WORKER // SC_CHUNK
    idx = pltpu.VMEM((SC_CHUNK,), jnp.int32)

    @functools.partial(
        pl.kernel, mesh=_sc_mesh(),
        out_type=jax.ShapeDtypeStruct((MOE_ROWS, ROW_WORDS), jnp.int32),
        scratch_types=[idx, idx, idx, idx, pltpu.VMEM((2, SC_CHUNK, ROW_WORDS), jnp.int32),
                       pltpu.SemaphoreType.DMA((6,)), pltpu.SemaphoreType.DMA((4,))],
        name="moe_dispatch",
    )
    def run(h_hbm, pa_hbm, pb_hbm, xs_hbm, ia0, ib0, ia1, ib1, rows_v, sem_in, sem_out):
        base = _sc_worker_base()
        ia, ib = (ia0, ia1), (ib0, ib1)

        def start_loads(c):
            slot = c % 2
            tok = pl.ds(pl.multiple_of(base + c * SC_CHUNK, 8), SC_CHUNK)
            return (pltpu.async_copy(pa_hbm.at[tok], ia[slot], sem_in.at[3 * slot]),
                    pltpu.async_copy(pb_hbm.at[tok], ib[slot], sem_in.at[3 * slot + 1]),
                    pltpu.async_copy(h_hbm.at[tok], rows_v.at[slot], sem_in.at[3 * slot + 2]))

        loads = start_loads(0)
        scatters = [(), ()]
        for c in range(n_chunks):
            slot = c % 2
            for cp in loads:
                cp.wait()
            if c + 1 < n_chunks:
                for cp in scatters[1 - slot]:
                    cp.wait()
                scatters[1 - slot] = ()
                loads = start_loads(c + 1)
            scatters[slot] = (pltpu.async_copy(rows_v.at[slot], xs_hbm.at[ia[slot]], sem_out.at[2 * slot]),
                              pltpu.async_copy(rows_v.at[slot], xs_hbm.at[ib[slot]], sem_out.at[2 * slot + 1]))
        for pending in scatters:
            for cp in pending:
                cp.wait()

    return run(h, pos_a, pos_b)


def _moe_collect(ys, pos_a, pos_b, tok0=0, n_tok=N_TOK):
    per_worker = n_tok // SC_WORKERS
    chunk = SC_CHUNK if per_worker % SC_CHUNK == 0 else 32
    n_chunks = per_worker // chunk
    out = jax.ShapeDtypeStruct((n_tok, ROW_WORDS), jnp.int32)
    idx = pltpu.VMEM((per_worker,), jnp.int32)
    rows = pltpu.VMEM((2, chunk, ROW_WORDS), jnp.int32)

    @functools.partial(
        pl.kernel, mesh=_sc_mesh(), out_type=(out, out),
        scratch_types=[idx, idx, rows, rows, pltpu.SemaphoreType.DMA((4,)), pltpu.SemaphoreType.DMA((4,))],
        name="moe_collect",
    )
    def run(ys_hbm, pa_hbm, pb_hbm, ya_hbm, yb_hbm, ia_v, ib_v, ra_v, rb_v, sem_g, sem_w):
        base = (lax.axis_index("s") * SC_CORES + lax.axis_index("c")) * per_worker
        mine = pl.ds(pl.multiple_of(tok0 + base, 8), per_worker)
        pltpu.sync_copy(pa_hbm.at[mine], ia_v)
        pltpu.sync_copy(pb_hbm.at[mine], ib_v)
        writes = [(), ()]
        for c in range(n_chunks):
            slot = c % 2
            for cp in writes[slot]:
                cp.wait()
            part = pl.ds(c * chunk, chunk)
            tok = pl.ds(pl.multiple_of(base + c * chunk, 8), chunk)
            ga = pltpu.async_copy(ys_hbm.at[ia_v.at[part]], ra_v.at[slot], sem_g.at[slot])
            gb = pltpu.async_copy(ys_hbm.at[ib_v.at[part]], rb_v.at[slot], sem_g.at[2 + slot])
            ga.wait()
            wa = pltpu.async_copy(ra_v.at[slot], ya_hbm.at[tok], sem_w.at[slot])
            gb.wait()
            wb = pltpu.async_copy(rb_v.at[slot], yb_hbm.at[tok], sem_w.at[2 + slot])
            writes[slot] = (wa, wb)
        for pending in writes:
            for cp in pending:
                cp.wait()

    return run(ys, pos_a, pos_b)


def _experts_kernel(plan_ref, xs_ref, wg_hbm, wu_hbm, wd_hbm, y_ref,
                    sg_ref, su_ref, sd_ref, wgb_ref, wub_ref, wdb_ref, hid_ref, sems, seg_ref, *, layer):
    n_tiles = pl.num_programs(0) * EXPERT_TILES_PER_STEP

    def weight_copies(e, slot):
        return (pltpu.make_async_copy(wg_hbm.at[layer, e], sg_ref.at[slot], sems.at[slot, 0]),
                pltpu.make_async_copy(wu_hbm.at[layer, e], su_ref.at[slot], sems.at[slot, 1]),
                pltpu.make_async_copy(wd_hbm.at[layer, e], sd_ref.at[slot], sems.at[slot, 2]))

    def tile(t, row0):
        expert = plan_ref[t]
        n_real = plan_ref[PLAN_LANES + t]
        fresh = jnp.logical_or(t == 0, expert != plan_ref[jnp.maximum(t - 1, 0)])

        @pl.when(t == 0)
        def _():
            seg_ref[0] = 0

            @pl.when(n_real > 0)
            def _():
                for cp in weight_copies(expert, 0):
                    cp.start()

        @pl.when(jnp.logical_and(n_real > 0, fresh))
        def _():
            slot = seg_ref[0] % 2
            for cp in weight_copies(expert, slot):
                cp.wait()
            wgb_ref[...] = sg_ref[slot].astype(BF16)
            wub_ref[...] = su_ref[slot].astype(BF16)
            wdb_ref[...] = sd_ref[slot].astype(BF16)
            nxt = lax.while_loop(
                lambda u: jnp.logical_and(u < n_tiles, plan_ref[jnp.minimum(u, n_tiles - 1)] == expert),
                lambda u: u + 1, t + 1)
            nxt_c = jnp.minimum(nxt, n_tiles - 1)

            @pl.when(jnp.logical_and(nxt < n_tiles, plan_ref[PLAN_LANES + nxt_c] > 0))
            def _():
                for cp in weight_copies(plan_ref[nxt_c], 1 - slot):
                    cp.start()

            seg_ref[0] = seg_ref[0] + 1

        @pl.when(n_real > 0)
        def _():
            n = EXPERT_SUB_ROWS
            n_sub = MOE_TILE // n
            row = lax.broadcasted_iota(jnp.int32, (n, xs_ref.shape[1]), 0)

            def up(r):
                rows = slice(row0 + r * n, row0 + (r + 1) * n)
                words = jnp.where(row < n_real - r * n, xs_ref[rows, :], 0)
                x = _unpack_rows(words).astype(BF16)
                a = jnp.dot(x, wgb_ref[...], preferred_element_type=F32)
                b = jnp.dot(x, wub_ref[...], preferred_element_type=F32)
                hid_ref[r] = ((a * jax.nn.sigmoid(a)) * b).astype(BF16)

            def down(r):
                rows = slice(row0 + r * n, row0 + (r + 1) * n)
                y_ref[rows, :] = _pack_rows(jnp.dot(hid_ref[r], wdb_ref[...], preferred_element_type=F32))

            up(0)
            for r in range(1, n_sub):
                up(r)
                down(r - 1)
            down(n_sub - 1)

    for q in range(EXPERT_TILES_PER_STEP):
        tile(pl.program_id(0) * EXPERT_TILES_PER_STEP + q, q * MOE_TILE)


def _experts(plan, xs, w_gate, w_up, w_down, layer):
    hbm = pl.BlockSpec(memory_space=pl.ANY)
    step_rows = MOE_TILE * EXPERT_TILES_PER_STEP
    return pl.pallas_call(
        functools.partial(_experts_kernel, layer=layer),
        grid_spec=pltpu.PrefetchScalarGridSpec(
            num_scalar_prefetch=1,
            grid=(MOE_ROWS // step_rows,),
            in_specs=[pl.BlockSpec((step_rows, ROW_WORDS), lambda j, plan: (j, 0)), hbm, hbm, hbm],
            out_specs=pl.BlockSpec((step_rows, ROW_WORDS), lambda j, plan: (j, 0)),
            scratch_shapes=[pltpu.VMEM((2, D_MODEL, D_EXPERT), F32), pltpu.VMEM((2, D_MODEL, D_EXPERT), F32),
                            pltpu.VMEM((2, D_EXPERT, D_MODEL), F32),
                            pltpu.VMEM((D_MODEL, D_EXPERT), BF16), pltpu.VMEM((D_MODEL, D_EXPERT), BF16),
                            pltpu.VMEM((D_EXPERT, D_MODEL), BF16),
                            pltpu.VMEM((MOE_TILE // EXPERT_SUB_ROWS, EXPERT_SUB_ROWS, D_EXPERT), BF16),
                            pltpu.SemaphoreType.DMA((2, 3)), pltpu.SMEM((1,), jnp.int32)],
        ),
        out_shape=jax.ShapeDtypeStruct((MOE_ROWS, ROW_WORDS), jnp.int32),
        compiler_params=_params("arbitrary"),
        name="experts",
    )(plan, xs, w_gate, w_up, w_down)


def _combine_kernel(x_ref, ya_ref, yb_ref, wt_ref, mod_ref, o_ref):
    o_ref[...] = _moe_mix(x_ref, ya_ref, yb_ref, wt_ref, mod_ref)


def _combine(x, moe_out, mod_l, tok0, n_tok, block_rows=1024):
    ya, yb, w_tok = moe_out
    b0 = tok0 // block_rows
    rows = lambda width: pl.BlockSpec((block_rows, width), lambda i: (b0 + i, 0))
    local = pl.BlockSpec((block_rows, ROW_WORDS), lambda i: (i, 0))
    return pl.pallas_call(
        _combine_kernel,
        grid=(n_tok // block_rows,),
        in_specs=[rows(D_MODEL), local, local, rows(TOP_K),
                  pl.BlockSpec((None, 6, D_MODEL), lambda i: (_cond_of_token_block(b0 + i, block_rows), 0, 0))],
        out_specs=pl.BlockSpec((block_rows, D_MODEL), lambda i: (i, 0)),
        out_shape=jax.ShapeDtypeStruct((n_tok, D_MODEL), F32),
        compiler_params=_params("arbitrary"),
        name="combine",
    )(x, ya, yb, w_tok, mod_l)


def _moe(h, logits_t, router_b, w_gate, w_up, w_down, layer, ranges=((0, N_TOK),)):
    pos, w, plan = _router(logits_t, router_b)
    xs = _moe_dispatch(h, pos[0], pos[1])
    ys = _experts(plan.reshape(-1), xs, w_gate, w_up, w_down, layer)
    return [(*_moe_collect(ys, pos[0], pos[1], tok0, n_tok), w) for tok0, n_tok in ranges]


def _dft_tables(L):
    k = np.arange(L)[:, None]
    m = np.arange(L)[None, :]
    r = (k * m) % (2 * L)
    ang = np.pi * r.astype(np.float64) / L
    fc = np.cos(ang)
    fs = np.sin(ang)
    fs[0, :] = np.where(np.arange(L) % 2 == 0, 1.0, -1.0)
    wk = np.full((L, 1), 1.0 / L)
    wk[0, 0] = 0.5 / L
    gc = (fc * wk).T
    gs = (fs * wk).T
    return [jnp.asarray(t.astype(np.float32)).astype(BF16) for t in (fc, fs, gc, gs)]


def _filter_consts(L):
    t = np.linspace(0.0, 1.0, L, dtype=np.float32)[:, None]
    w = (np.float32(2.0 * np.pi) * np.arange(L, dtype=np.float32)[:, None] / np.float32(L)).astype(np.float32)
    fb = np.linspace(1e-4, HY_BANDS - 1, HY_BANDS, dtype=np.float32)[None, :]
    emb = np.concatenate([t, np.cos(fb * w), -np.sin(fb * w)], axis=-1).astype(np.float32)
    lo = math.log(HY_DECAY_TARGET) / HY_SLOW_PCT
    hi = math.log(HY_DECAY_TARGET) / HY_FAST_PCT
    deltas = np.abs(np.linspace(lo, hi, D_MODEL, dtype=np.float32))
    decay = np.exp(-t * deltas).astype(np.float32)
    return jnp.asarray(emb), jnp.asarray(decay)


def _filter_kernel(emb_ref, w1_ref, b1_ref, w2_ref, b2_ref, fr_ref, w3f_ref, w3b_ref, dec_ref,
                   fc_ref, fs_ref, kr_ref, q_ref, krn_ref, hd_ref):
    @pl.when(pl.program_id(0) == 0)
    def _():
        fr = fr_ref[...]
        h1 = jnp.sin(fr * (jnp.dot(emb_ref[...], w1_ref[...], precision=HIGHEST,
                                   preferred_element_type=F32) + b1_ref[...]))
        hd_ref[...] = jnp.sin(fr * (jnp.dot(h1, w2_ref[...], precision=HIGHEST,
                                            preferred_element_type=F32) + b2_ref[...]))

    hd = hd_ref[...]
    dec = dec_ref[...]
    f = jnp.dot(hd, w3f_ref[...], precision=HIGHEST, preferred_element_type=F32) * dec
    g = jnp.dot(hd, w3b_ref[...], precision=HIGHEST, preferred_element_type=F32) * dec
    row = lax.broadcasted_iota(jnp.int32, f.shape, 0)
    g = jnp.where(row == 0, 0.0, g)
    s = f + g
    d = f - g
    kr = jnp.dot(fc_ref[...], s.astype(BF16), preferred_element_type=F32)
    qq = jnp.dot(fs_ref[...], d.astype(BF16), preferred_element_type=F32)
    alt = jnp.where(row % 2 == 0, 1.0, -1.0)
    nyq = jnp.sum(alt * s, axis=0, keepdims=True)
    kr_ref[...] = kr
    q_ref[...] = jnp.where(row == 0, 0.0, qq)
    krn_ref[...] = jnp.where(row == 0, nyq, kr)


def _hyena_filter_spectrum(L, w1, b1, w2, b2, w3, freq, fc, fs, cblk=256):
    emb, decay = _filter_consts(L)
    ncb = D_MODEL // cblk
    n_emb = 128
    emb = jnp.pad(emb, ((0, 0), (0, n_emb - emb.shape[1])))
    w1 = jnp.pad(w1, ((0, n_emb - w1.shape[0]), (0, 0)))
    full = lambda shape: pl.BlockSpec(shape, lambda j: tuple(0 for _ in shape))
    out_sds = jax.ShapeDtypeStruct((L, D_MODEL), F32)
    out_spec = pl.BlockSpec((L, cblk), lambda j: (0, j))
    return pl.pallas_call(
        _filter_kernel,
        grid=(ncb,),
        in_specs=[
            full((L, n_emb)), full((n_emb, HY_FFN)), full((1, HY_FFN)), full((HY_FFN, HY_FFN)),
            full((1, HY_FFN)), full((1, HY_FFN)),
            pl.BlockSpec((HY_FFN, cblk), lambda j: (0, j)),
            pl.BlockSpec((HY_FFN, cblk), lambda j: (0, ncb + j)),
            pl.BlockSpec((L, cblk), lambda j: (0, j)),
            full((L, L)), full((L, L)),
        ],
        out_specs=(out_spec, out_spec, out_spec),
        out_shape=(out_sds, out_sds, out_sds),
        scratch_shapes=[pltpu.VMEM((L, HY_FFN), F32)],
        compiler_params=_params("arbitrary"),
        name=f"hyena_filter_{L}",
    )(emb, w1, b1.reshape(1, HY_FFN), w2, b2.reshape(1, HY_FFN), freq.reshape(1, HY_FFN), w3, w3, decay, fc, fs)


def _hyena_conv_kernel(x0_ref, x1_ref, v_ref, cw0_ref, cw1_ref, cwv_ref, cb0_ref, cb1_ref, cbv_ref,
                       kr_ref, q_ref, krn_ref, ds_ref, fc_ref, fs_ref, gc_ref, gs_ref, o_ref,
                       zz_ref, gate_ref, skip_ref, yr_ref, yw_ref):
    L = fc_ref.shape[0]
    unit_w = zz_ref.shape[2]
    units = [(slice(s * L, (s + 1) * L), slice(c * unit_w, (c + 1) * unit_w))
             for s in range(x0_ref.shape[0] // L) for c in range(x0_ref.shape[1] // unit_w)]
    row = lax.broadcasted_iota(jnp.int32, (L, unit_w), 0)

    def gating(i):
        rows, cols = units[i]

        def short_conv(u_ref, w_ref, b_ref):
            u = u_ref[rows, cols].astype(F32)
            w = w_ref[:, cols]
            prev = jnp.where(row == 0, 0.0, pltpu.roll(u, 1, axis=0))
            nxt = jnp.where(row == L - 1, 0.0, pltpu.roll(u, L - 1, axis=0))
            return prev * w[0:1, :] + u * w[1:2, :] + nxt * w[2:3, :] + b_ref[:, cols]

        x0 = short_conv(x0_ref, cw0_ref, cb0_ref)
        zz = short_conv(v_ref, cwv_ref, cbv_ref) * short_conv(x1_ref, cw1_ref, cb1_ref)
        zz_ref[i] = zz.astype(BF16)
        gate_ref[i] = x0
        skip_ref[i] = x0 * zz * ds_ref[:, cols]

    def spectrum(i):
        cols = units[i][1]
        ur = jnp.dot(fc_ref[...], zz_ref[i], preferred_element_type=F32)
        p = jnp.dot(fs_ref[...], zz_ref[i], preferred_element_type=F32)
        qq = q_ref[:, cols]
        yr_ref[i] = (ur * kr_ref[:, cols] - p * qq).astype(BF16)
        yw_ref[i] = (ur * qq + p * krn_ref[:, cols]).astype(BF16)

    def synthesis(i):
        rows, cols = units[i]
        y = jnp.dot(gc_ref[...], yr_ref[i], preferred_element_type=F32)
        y = y + jnp.dot(gs_ref[...], yw_ref[i], preferred_element_type=F32)
        o_ref[rows, cols] = (gate_ref[i] * y + skip_ref[i]).astype(o_ref.dtype)

    for t in range(len(units) + 2):
        if t < len(units):
            gating(t)
        if 0 <= t - 1 < len(units):
            spectrum(t - 1)
        if 0 <= t - 2 < len(units):
            synthesis(t - 2)


def _hyena_conv(u, conv_w, conv_b, dskip, spectrum, tables, *, latent):
    L = LATENT_LEN if latent else PROMPT_LEN
    n_seq = N_LATENT_SEQ if latent else N_PROMPT_SEQ
    cblk = 512
    unit_w = 256 if latent else 512
    ncb = D_MODEL // cblk
    seqs = 1 if latent else 8
    unit = (seqs * cblk // unit_w, L, unit_w)
    row0 = (N_PROMPT_TOK // L) if latent else 0
    kr, qq, krn = spectrum
    fc, fs, gc, gs = tables

    def part(p, rows):
        if rows != L:
            return pl.BlockSpec((rows, cblk), lambda j, s: (0, p * ncb + j))
        return pl.BlockSpec((seqs * L, cblk), lambda j, s: (row0 // seqs + s, p * ncb + j))

    def const_cols(rows):
        return pl.BlockSpec((rows, cblk), lambda j, s: (0, j))

    mat = pl.BlockSpec((L, L), lambda j, s: (0, 0))
    conv_b2 = conv_b.reshape(1, 3 * D_MODEL)
    in_specs = [part(0, L), part(1, L), part(2, L),
                part(0, 3), part(1, 3), part(2, 3),
                part(0, 1), part(1, 1), part(2, 1),
                const_cols(L), const_cols(L), const_cols(L), const_cols(1),
                mat, mat, mat, mat]
    args = [u, u, u, conv_w, conv_w, conv_w, conv_b2, conv_b2, conv_b2,
            kr, qq, krn, dskip.reshape(1, D_MODEL), fc, fs, gc, gs]
    return pl.pallas_call(
        _hyena_conv_kernel,
        grid=(ncb, n_seq // seqs),
        in_specs=in_specs,
        out_specs=pl.BlockSpec((seqs * L, cblk), lambda j, s: (s, j)),
        out_shape=jax.ShapeDtypeStruct((n_seq * L, D_MODEL), BF16),
        scratch_shapes=[pltpu.VMEM(unit, BF16), pltpu.VMEM(unit, F32), pltpu.VMEM(unit, F32),
                        pltpu.VMEM(unit, BF16), pltpu.VMEM(unit, BF16)],
        compiler_params=_params("arbitrary", "arbitrary"),
        name="hyena_conv_latent" if latent else "hyena_conv_prompt",
    )(*args)


def kernel(x_prompt, x_sample, cache_k, cache_v, state_hgrn, c, c_ctx, norm_g, mod_w, mod_b, ab_in_w, hgrn_lb, hgrn_onorm_g, attn_qnorm_g, attn_knorm_g, ab_out_w, hy_in_w, hy_in_b, hy_conv_w, hy_conv_b, hy_f_w1, hy_f_b1, hy_f_w2, hy_f_b2, hy_f_w3, hy_f_freq, hy_dskip, hy_out_w, router_w, router_b, moe_w_gate, moe_w_up, moe_w_down):
    xp = x_prompt.reshape(N_PROMPT_TOK, D_MODEL)
    xl = x_sample.reshape(N_LATENT_TOK, D_MODEL)
    cond = jnp.concatenate([c_ctx[None, :], c, jnp.zeros((N_COND - 1 - N_LATENT_SEQ, D_MODEL), F32)], axis=0)
    mod = _modulation(cond, mod_w, mod_b)
    router_wp = jnp.pad(router_w, ((0, 0), (0, ROUTER_LANES - N_EXPERTS)))

    z = _in_proj0(xp, xl, norm_g[0, 0], mod[0], ab_in_w[0])
    oa_p, new_state = _hgrn(z, hgrn_lb, hgrn_onorm_g[0], None, latent=False)
    oa_l = _hgrn(z, hgrn_lb, hgrn_onorm_g[0], state_hgrn, latent=True)
    ob_p, k_fm, v_fm = _attention_prompt(z, attn_qnorm_g[0], attn_knorm_g[0])
    fm_shape = (N_PROMPT_SEQ, 1, KV_HEADS, HEAD_DIM, PROMPT_LEN)
    new_k = jnp.swapaxes(k_fm.reshape(fm_shape), -1, -2)
    new_v = jnp.swapaxes(v_fm.reshape(fm_shape), -1, -2)
    ob_l = _attention_latent(z, attn_qnorm_g[0], attn_knorm_g[0], cache_k, cache_v)
    x, h, logits_t = _out_proj([(oa_p, oa_l), (ob_p, ob_l)], ab_out_w[0], (xp, xl), norm_g[0, 1], mod[0],
                               router_wp)
    (moe_out,) = _moe(h, logits_t, router_b, moe_w_gate, moe_w_up, moe_w_down, 0)

    x, u = _in_proj1(x, moe_out, mod[0], norm_g[1, 0], mod[1], hy_in_w[0], hy_in_b[0])
    pre = []
    for latent in (False, True):
        L = LATENT_LEN if latent else PROMPT_LEN
        tables = _dft_tables(L)
        spectrum = _hyena_filter_spectrum(L, hy_f_w1[0], hy_f_b1[0], hy_f_w2[0], hy_f_b2[0], hy_f_w3[0],
                                          hy_f_freq[0], tables[0], tables[1])
        pre.append(_hyena_conv(u, hy_conv_w[0], hy_conv_b[0], hy_dskip[0], spectrum, tables, latent=latent))
    x, h, logits_t = _out_proj([tuple(pre)], hy_out_w[0], (x,), norm_g[1, 1], mod[1], router_wp)
    trunks = ((0, N_PROMPT_TOK), (N_PROMPT_TOK, N_LATENT_TOK))
    out_p, out_l = _moe(h, logits_t, router_b, moe_w_gate, moe_w_up, moe_w_down, 1, ranges=trunks)

    y_prompt = _combine(x, out_p, mod[1], *trunks[0]).reshape(N_PROMPT_SEQ, PROMPT_LEN, D_MODEL)
    y_sample = _combine(x, out_l, mod[1], *trunks[1]).reshape(N_LATENT_SEQ, LATENT_LEN, D_MODEL)
    return (y_prompt, y_sample, new_k, new_v, new_state)
```

```python
import functools
import math

import numpy as np
import jax
import jax.numpy as jnp
from jax import lax
from jax.experimental import pallas as pl
from jax.experimental.pallas import tpu as pltpu
from jax.experimental.pallas import tpu_sc as plsc

F32 = jnp.float32
BF16 = jnp.bfloat16
HIGHEST = lax.Precision.HIGHEST

D_MODEL = 1024
N_PROMPT_SEQ = 32
PROMPT_LEN = 256
N_LATENT_SEQ = 2
LATENT_LEN = 1024
PAST_LEN = 512
GRID_W = 64
N_PROMPT_TOK = N_PROMPT_SEQ * PROMPT_LEN
N_LATENT_TOK = N_LATENT_SEQ * LATENT_LEN
N_TOK = N_PROMPT_TOK + N_LATENT_TOK
N_COND = 8
EPS = 1e-6

A_WIDTH = 512
A_HEADS = 4
A_DK = 128
CHUNK = 64
HGRN_BLOCK = 128
HGRN_HEADS_PER_STEP = 4
HEAD_DIM = 64
Q_HEADS = 8
KV_HEADS = 2
Q_PER_KV = Q_HEADS // KV_HEADS
Q_BLOCK = 256
ROPE_THETA = 10000.0
ROPE_PAIRS = HEAD_DIM // 4

HY_BANDS = 16
HY_FFN = 64
HY_DECAY_TARGET = 1e-2
HY_FAST_PCT = 0.3
HY_SLOW_PCT = 1.5

N_EXPERTS = 16
N_GROUPS = 4
EXPERTS_PER_GROUP = 4
TOP_K = 2
D_EXPERT = 512
ROUTER_LANES = 128
OUT_PROJ_SUB_ROWS = 256
EXPERT_SUB_ROWS = 256
EXPERT_TILES_PER_STEP = 2
IN_PROJ_SUB_ROWS = 256
MOE_TILE = 512
MOE_ROWS = N_TOK * TOP_K + N_EXPERTS * MOE_TILE
PLAN_LANES = 128

SC_CORES = 2
SC_WORKERS = 32
SC_TOKENS_PER_WORKER = N_TOK // SC_WORKERS
SC_CHUNK = 40
ROW_WORDS = D_MODEL // 2

VMEM_LIMIT = 56 * 1024 * 1024


def _params(*sem):
    return pltpu.CompilerParams(dimension_semantics=sem, vmem_limit_bytes=VMEM_LIMIT)


def _pack_rows(x):
    n = x.shape[1] // 2
    bits = pltpu.bitcast(x.astype(BF16).astype(F32), jnp.uint32)
    return pltpu.bitcast(bits[:, :n] | (bits[:, n:] >> 16), jnp.int32)


def _unpack_rows(p):
    bits = pltpu.bitcast(p, jnp.uint32)
    hi = pltpu.bitcast(bits & jnp.uint32(0xFFFF0000), F32)
    lo = pltpu.bitcast(bits << 16, F32)
    return jnp.concatenate([hi, lo], axis=1)


def _cond_of_token_block(i, block_rows):
    start = i * block_rows
    return jnp.where(start < N_PROMPT_TOK, 0, 1 + (start - N_PROMPT_TOK) // LATENT_LEN)


def _mod_kernel(cond_ref, w_ref, b_ref, o_ref):
    cnd = cond_ref[...]
    s = cnd * jax.nn.sigmoid(cnd)
    s_hi = s.astype(BF16)
    s_lo = (s - s_hi.astype(F32)).astype(BF16)
    w = w_ref[...]
    w_hi = w.astype(BF16)
    w_lo = (w - w_hi.astype(F32)).astype(BF16)
    acc = jnp.dot(s_hi, w_hi, preferred_element_type=F32)
    acc = acc + jnp.dot(s_lo, w_hi, preferred_element_type=F32)
    acc = acc + jnp.dot(s_hi, w_lo, preferred_element_type=F32)
    o_ref[...] = acc + b_ref[...]


def _modulation(cond, mod_w, mod_b):
    depth = mod_w.shape[0]
    n_mod = 6
    cols = 2 * D_MODEL
    n_step = n_mod * D_MODEL // cols
    out = pl.pallas_call(
        _mod_kernel,
        grid=(depth, n_step),
        in_specs=[
            pl.BlockSpec((N_COND, D_MODEL), lambda l, j: (0, 0)),
            pl.BlockSpec((None, D_MODEL, cols), lambda l, j: (l, 0, j)),
            pl.BlockSpec((None, 1, cols), lambda l, j: (l, 0, j)),
        ],
        out_specs=pl.BlockSpec((None, N_COND, cols), lambda l, j: (l, 0, j)),
        out_shape=jax.ShapeDtypeStruct((depth, N_COND, n_mod * D_MODEL), F32),
        compiler_params=_params("arbitrary", "arbitrary"),
        name="modulation",
    )(cond, mod_w, mod_b.reshape(depth, 1, n_mod * D_MODEL))
    return out.reshape(depth, N_COND, n_mod, D_MODEL)


def _modulated_norm(x, g, mod, shift_row, scale_row):
    ms = jnp.mean(x * x, axis=-1, keepdims=True)
    y = x * lax.rsqrt(ms + EPS) * g
    return y * (1.0 + mod[scale_row:scale_row + 1, :]) + mod[shift_row:shift_row + 1, :]


def _trunk_specs(block_rows, width):
    n_prompt_blocks = N_PROMPT_TOK // block_rows
    return (pl.BlockSpec((block_rows, width), lambda i: (jnp.minimum(i, n_prompt_blocks - 1), 0)),
            pl.BlockSpec((block_rows, width), lambda i: (jnp.maximum(i - n_prompt_blocks, 0), 0)))


def _select_trunk(p_ref, l_ref, rows=slice(None)):
    block_rows = p_ref.shape[0]
    return jnp.where(pl.program_id(0) < N_PROMPT_TOK // block_rows, p_ref[rows, :], l_ref[rows, :])


def _cast_once(w_ref, wb_ref):
    @pl.when(pl.program_id(0) == 0)
    def _():
        wb_ref[...] = w_ref[...].astype(BF16)


def _resident(shape):
    return pl.BlockSpec(shape, lambda i: tuple(0 for _ in shape), pipeline_mode=pl.Buffered(1))


def _mod_spec(block_rows):
    return pl.BlockSpec((None, 6, D_MODEL), lambda i: (_cond_of_token_block(i, block_rows), 0, 0))


def _in_proj0_kernel(xp_ref, xl_ref, g_ref, mod_ref, w_ref, o_ref, wb_ref, hb_ref):
    _cast_once(w_ref, wb_ref)
    n = IN_PROJ_SUB_ROWS
    n_sub = xp_ref.shape[0] // n

    def prepare(r):
        x = _select_trunk(xp_ref, xl_ref, slice(r * n, (r + 1) * n))
        hb_ref[r] = _modulated_norm(x, g_ref[...], mod_ref[...], 0, 1).astype(BF16)

    def project(r):
        u = jnp.dot(hb_ref[r], wb_ref[...], preferred_element_type=F32)
        o_ref[r * n:(r + 1) * n, :] = u.astype(o_ref.dtype)

    prepare(0)
    for r in range(1, n_sub):
        prepare(r)
        project(r - 1)
    project(n_sub - 1)


def _in_proj0(x_prompt, x_latent, g, mod_l, w, block_rows=512):
    n = w.shape[1]
    return pl.pallas_call(
        _in_proj0_kernel,
        grid=(N_TOK // block_rows,),
        in_specs=[*_trunk_specs(block_rows, D_MODEL), _resident((1, D_MODEL)), _mod_spec(block_rows),
                  _resident((D_MODEL, n))],
        out_specs=pl.BlockSpec((block_rows, n), lambda i: (i, 0)),
        out_shape=jax.ShapeDtypeStruct((N_TOK, n), BF16),
        scratch_shapes=[pltpu.VMEM((D_MODEL, n), BF16),
                        pltpu.VMEM((block_rows // IN_PROJ_SUB_ROWS, IN_PROJ_SUB_ROWS, D_MODEL), BF16)],
        compiler_params=_params("arbitrary"),
        name="in_proj0",
    )(x_prompt, x_latent, g.reshape(1, D_MODEL), mod_l, w)


def _moe_mix(x_ref, ya_ref, yb_ref, wt_ref, mod_ref, rows=slice(None)):
    wt = wt_ref[rows, :]
    mix = wt[:, 0:1] * _unpack_rows(ya_ref[rows, :]) + wt[:, 1:2] * _unpack_rows(yb_ref[rows, :])
    return x_ref[rows, :] + mod_ref[5:6, :] * mix


def _in_proj1_kernel(x_ref, ya_ref, yb_ref, wt_ref, modp_ref, g_ref, mod_ref, w_ref, b_ref, xo_ref, o_ref,
                     wb_ref, hb_ref):
    _cast_once(w_ref, wb_ref)
    n = IN_PROJ_SUB_ROWS
    n_sub = x_ref.shape[0] // n

    def prepare(r):
        rows = slice(r * n, (r + 1) * n)
        x = _moe_mix(x_ref, ya_ref, yb_ref, wt_ref, modp_ref, rows)
        xo_ref[rows, :] = x
        hb_ref[r] = _modulated_norm(x, g_ref[...], mod_ref[...], 0, 1).astype(BF16)

    def project(r):
        rows = slice(r * n, (r + 1) * n)
        u = jnp.dot(hb_ref[r], wb_ref[...], preferred_element_type=F32) + b_ref[...]
        o_ref[rows, :] = u.astype(o_ref.dtype)

    prepare(0)
    for r in range(1, n_sub):
        prepare(r)
        project(r - 1)
    project(n_sub - 1)


def _in_proj1(x, moe_out, mod_prev, g, mod_l, w, bias, block_rows=512):
    ya, yb, w_tok = moe_out
    n = w.shape[1]
    tok = pl.BlockSpec((block_rows, D_MODEL), lambda i: (i, 0))
    packed = pl.BlockSpec((block_rows, ROW_WORDS), lambda i: (i, 0))
    return pl.pallas_call(
        _in_proj1_kernel,
        grid=(N_TOK // block_rows,),
        in_specs=[tok, packed, packed, pl.BlockSpec((block_rows, TOP_K), lambda i: (i, 0)), _mod_spec(block_rows),
                  _resident((1, D_MODEL)), _mod_spec(block_rows), _resident((D_MODEL, n)), _resident((1, n))],
        out_specs=(tok, pl.BlockSpec((block_rows, n), lambda i: (i, 0))),
        out_shape=(jax.ShapeDtypeStruct((N_TOK, D_MODEL), F32), jax.ShapeDtypeStruct((N_TOK, n), BF16)),
        scratch_shapes=[pltpu.VMEM((D_MODEL, n), BF16),
                        pltpu.VMEM((block_rows // IN_PROJ_SUB_ROWS, IN_PROJ_SUB_ROWS, D_MODEL), BF16)],
        compiler_params=_params("arbitrary"),
        name="in_proj1",
    )(x, ya, yb, w_tok, mod_prev, g.reshape(1, D_MODEL), mod_l, w, bias.reshape(1, n))


def _hgrn_kernel(*refs, seq_len, with_state):
    if with_state:
        (q_ref, zf_ref, zb_ref, i_ref, ga_ref, lb_ref, og_ref, s0_ref, o_ref, of_ref, ob_ref) = refs
    else:
        (q_ref, zf_ref, zb_ref, i_ref, ga_ref, lb_ref, og_ref, o_ref, s_ref, of_ref, ob_ref) = refs
    n_blocks = seq_len // HGRN_BLOCK
    chunks_per_block = HGRN_BLOCK // CHUNK

    lbr = lb_ref[...]
    mx = jnp.maximum(lbr[0], lbr[1])
    e0 = jnp.exp(lbr[0] - mx)
    e1 = jnp.exp(lbr[1] - mx)
    lb = e0 / (e0 + e1)

    row = lax.broadcasted_iota(jnp.int32, (HGRN_BLOCK, HGRN_BLOCK), 0)
    col = lax.broadcasted_iota(jnp.int32, (HGRN_BLOCK, HGRN_BLOCK), 1)
    same_chunk = (row // CHUNK) == (col // CHUNK)
    nt = (((1,), (1,)), ((), ()))
    tn = (((0,), (0,)), ((), ()))

    def per_chunk_row(x, idx):
        return jnp.concatenate(
            [jnp.broadcast_to(x[n * CHUNK + idx:n * CHUNK + idx + 1, :], (CHUNK, x.shape[1]))
             for n in range(chunks_per_block)], axis=0)

    def in_chunk_cumsum(tri, x):
        hi = x.astype(BF16)
        lo = (x - hi.astype(F32)).astype(BF16)
        return jnp.dot(tri, hi, preferred_element_type=F32) + jnp.dot(tri, lo, preferred_element_type=F32)

    def prepare(blk, cols, z_ref, lbd, forward):
        rows = slice(blk * HGRN_BLOCK, (blk + 1) * HGRN_BLOCK)
        keep = (same_chunk & (col <= row)) if forward else (same_chunk & (col >= row))
        tri = jnp.where(keep, 1.0, 0.0).astype(BF16)
        mid = CHUNK // 2 if forward else CHUNK - 1 - CHUNK // 2
        last = CHUNK - 1 if forward else 0
        f = lbd + (1.0 - lbd) * jax.nn.sigmoid(z_ref[rows, cols].astype(F32))
        lf = jnp.log(f)
        k = 1.0 - f
        q = q_ref[rows, cols].astype(F32)
        b = in_chunk_cumsum(tri, lf)
        bm = per_chunk_row(b, mid)
        bl = per_chunk_row(b, last)
        return dict(
            rows=rows, cols=cols, keep=keep, forward=forward,
            vb=i_ref[rows, cols].astype(BF16),
            qe=(q * jnp.exp(b - bm)).astype(BF16), ke=(k * jnp.exp(bm - b)).astype(BF16),
            qb=(q * jnp.exp(b)).astype(BF16), ks=(k * jnp.exp(bl - b)).astype(BF16), decay=jnp.exp(bl))

    def within_chunks(u):
        att = lax.dot_general(u["qe"], u["ke"], nt, preferred_element_type=F32)
        att = jnp.where(u["keep"], att, 0.0)
        u["o_intra"] = jnp.dot(att.astype(BF16), u["vb"], preferred_element_type=F32)
        u["upd"] = [lax.dot_general(u["vb"][n * CHUNK:(n + 1) * CHUNK], u["ks"][n * CHUNK:(n + 1) * CHUNK], tn,
                                    preferred_element_type=F32) for n in range(chunks_per_block)]

    def across_chunks(u, st, out_ref):
        order = range(chunks_per_block) if u["forward"] else range(chunks_per_block - 1, -1, -1)
        o_inter = [None] * chunks_per_block
        for n in order:
            cr = slice(n * CHUNK, (n + 1) * CHUNK)
            o_inter[n] = lax.dot_general(u["qb"][cr], st.astype(BF16), nt, preferred_element_type=F32)
            st = st * u["decay"][n * CHUNK:n * CHUNK + 1, :] + u["upd"][n]
        out_ref[u["rows"], u["cols"]] = u["o_intra"] + jnp.concatenate(o_inter, axis=0)
        return st

    n_heads = q_ref.shape[1] // A_DK
    head_cols = [slice(hd * A_DK, (hd + 1) * A_DK) for hd in range(n_heads)]
    if with_state:
        states = {(hd, d): s0_ref[d, hd].T for hd in range(n_heads) for d in range(2)}
    else:
        states = {(hd, d): jnp.zeros((A_DK, A_DK), F32) for hd in range(n_heads) for d in range(2)}
    for step in range(n_blocks):
        units = {}
        for hd, cols in enumerate(head_cols):
            units[hd, 0] = prepare(step, cols, zf_ref, lb[0:1, cols], True)
            units[hd, 1] = prepare(n_blocks - 1 - step, cols, zb_ref, lb[1:2, cols], False)
        for u in units.values():
            within_chunks(u)
        for key, u in units.items():
            states[key] = across_chunks(u, states[key], of_ref if key[1] == 0 else ob_ref)
    for hd, cols in enumerate(head_cols):
        if not with_state:
            s_ref[0, hd] = states[hd, 0].T
            s_ref[1, hd] = states[hd, 1].T
        o = of_ref[:, cols] + ob_ref[:, cols]
        o = o * lax.rsqrt(jnp.mean(o * o, axis=-1, keepdims=True) + EPS) * og_ref[:, cols]
        ga = ga_ref[:, cols].astype(F32)
        o_ref[:, cols] = (o * (ga * jax.nn.sigmoid(ga))).astype(o_ref.dtype)


def _hgrn(z, hgrn_lb, onorm_g, state, *, latent):
    seq_len = LATENT_LEN if latent else PROMPT_LEN
    n_seq = N_LATENT_SEQ if latent else N_PROMPT_SEQ
    row0 = (N_PROMPT_TOK // seq_len) if latent else 0

    hw = HGRN_HEADS_PER_STEP * A_DK
    n_hg = A_HEADS // HGRN_HEADS_PER_STEP

    def zspec(part):
        return pl.BlockSpec((seq_len, hw), lambda s, h: (row0 + s, part * n_hg + h))

    in_specs = [zspec(0), zspec(1), zspec(2), zspec(3), zspec(4),
                pl.BlockSpec((2, 2, hw), lambda s, h: (0, 0, h)),
                pl.BlockSpec((1, hw), lambda s, h: (0, h))]
    args = [z, z, z, z, z, hgrn_lb, onorm_g.reshape(1, A_WIDTH)]
    state_spec = pl.BlockSpec((None, None, 2, HGRN_HEADS_PER_STEP, A_DK, A_DK), lambda s, h: (s, 0, 0, h, 0, 0))
    o_shape = jax.ShapeDtypeStruct((n_seq * seq_len, A_WIDTH), BF16)
    o_spec = pl.BlockSpec((seq_len, hw), lambda s, h: (s, h))
    if latent:
        in_specs.append(state_spec)
        args.append(state)
        out_shape, out_specs = o_shape, o_spec
    else:
        out_shape = (o_shape, jax.ShapeDtypeStruct((n_seq, 1, 2, A_HEADS, A_DK, A_DK), F32))
        out_specs = (o_spec, state_spec)
    return pl.pallas_call(
        functools.partial(_hgrn_kernel, seq_len=seq_len, with_state=latent),
        grid=(n_seq, n_hg),
        in_specs=in_specs,
        out_specs=out_specs,
        out_shape=out_shape,
        scratch_shapes=[pltpu.VMEM((seq_len, hw), F32), pltpu.VMEM((seq_len, hw), F32)],
        compiler_params=_params("arbitrary", "arbitrary"),
        name="hgrn_latent" if latent else "hgrn_prompt",
    )(*args)


def _rope_tables():
    pos = np.arange(LATENT_LEN)
    row, colp = pos // GRID_W, pos % GRID_W
    inv = ROPE_THETA ** (-np.arange(ROPE_PAIRS, dtype=np.float32) / ROPE_PAIRS)
    inv = inv.astype(np.float32)
    ang_r = (row.astype(np.float32)[:, None] * inv).astype(np.float32)
    ang_c = (colp.astype(np.float32)[:, None] * inv).astype(np.float32)
    cos = np.concatenate([np.cos(ang_r), np.cos(ang_r), np.cos(ang_c), np.cos(ang_c)], axis=1)
    sin = np.concatenate([-np.sin(ang_r), np.sin(ang_r), -np.sin(ang_c), np.sin(ang_c)], axis=1)
    return cos.astype(np.float32), sin.astype(np.float32)


def _head_mean_matrix(width):
    idx = np.arange(width) // HEAD_DIM
    return jnp.asarray((idx[:, None] == idx[None, :]).astype(np.float32) / HEAD_DIM).astype(BF16)


def _attn_kernel(*refs, latent):
    if latent:
        (q_ref, k_ref, v_ref, qg_ref, kg_ref, gq_ref, gk_ref, cosq_ref, sinq_ref, cosk_ref, sink_ref,
         ck_ref, cv_ref, o_ref, kd_ref, vd_ref, ckd_ref, cvd_ref) = refs
    else:
        (q_ref, k_ref, v_ref, qg_ref, kg_ref, gq_ref, gk_ref, o_ref, kout_ref, vout_ref) = refs
    pair_w = 2 * HEAD_DIM

    def head_norm(x, mean_ref, gain):
        sq = x * x
        hi = sq.astype(BF16)
        lo = (sq - hi.astype(F32)).astype(BF16)
        ms = jnp.dot(hi, mean_ref[...], preferred_element_type=F32)
        ms = ms + jnp.dot(lo, mean_ref[...], preferred_element_type=F32)
        return x * lax.rsqrt(ms + EPS) * gain

    def rope(x, cos, sin):
        n = x.shape[1]
        lane = lax.broadcasted_iota(jnp.int32, x.shape, 1)
        first_of_pair = (lane // ROPE_PAIRS) % 2 == 0
        swapped = jnp.where(first_of_pair, pltpu.roll(x, n - ROPE_PAIRS, axis=1), pltpu.roll(x, ROPE_PAIRS, axis=1))
        return x * cos + swapped * sin

    nt = (((1,), (1,)), ((), ()))

    def key_value_tiles(rows, seq_idx):
        k = head_norm(k_ref[rows, :].astype(F32), gk_ref, kg_ref[...])
        if latent:
            k = rope(k, cosk_ref[...], sink_ref[...])
        v = v_ref[rows, :].astype(F32)
        low_kv = lax.broadcasted_iota(jnp.int32, k.shape, 1) < HEAD_DIM
        k_swapped = pltpu.roll(k, HEAD_DIM, axis=1)
        v_swapped = pltpu.roll(v, HEAD_DIM, axis=1)
        if not latent:
            kout_ref[seq_idx] = k.T
            vout_ref[seq_idx] = v.T
        kd, vd = [], []
        for j in range(KV_HEADS):
            kd.append((jnp.where(low_kv, k, k_swapped) if j == 0 else jnp.where(low_kv, k_swapped, k)).astype(BF16))
            vj = (jnp.where(low_kv, v, v_swapped) if j == 0 else jnp.where(low_kv, v_swapped, v)).astype(BF16)
            vd.append(jnp.concatenate([vj, jnp.ones_like(vj)], axis=1))
        return kd, vd

    def query_units(rows, kd, vd):
        q = head_norm(q_ref[rows, :].astype(F32), gq_ref, qg_ref[...])
        if latent:
            q = rope(q, cosq_ref[...], sinq_ref[...])
        q = q * (HEAD_DIM ** -0.5)
        n_q = q.shape[0]
        low_q = lax.broadcasted_iota(jnp.int32, (n_q, pair_w), 1) < HEAD_DIM
        units = []
        for j in range(KV_HEADS):
            tiles = range(j * Q_PER_KV // 2, (j + 1) * Q_PER_KV // 2)
            parts = []
            for t in tiles:
                qt = q[:, t * pair_w:(t + 1) * pair_w]
                parts += [jnp.where(low_q, qt, 0.0), jnp.where(low_q, 0.0, qt)]
            units.append(dict(j=j, rows=rows, tiles=tiles, n_q=n_q, low_q=low_q, kd=kd[j], vd=vd[j],
                              qs=jnp.concatenate(parts, axis=0).astype(BF16)))
        return units

    def scores(u):
        u["s_new"] = lax.dot_general(u["qs"], u["kd"], nt, preferred_element_type=F32)
        if latent:
            u["s_old"] = lax.dot_general(u["qs"], ckd_ref[u["j"]], nt, preferred_element_type=F32)

    def softmax(u):
        m = jnp.max(u["s_new"], axis=-1, keepdims=True)
        if latent:
            m = jnp.maximum(m, jnp.max(u["s_old"], axis=-1, keepdims=True))
        u["p_new"] = jnp.exp(u.pop("s_new") - m).astype(BF16)
        if latent:
            u["p_old"] = jnp.exp(u.pop("s_old") - m).astype(BF16)

    def weighted_values(u):
        acc = jnp.dot(u["p_new"], u["vd"], preferred_element_type=F32)
        if latent:
            acc = acc + jnp.dot(u["p_old"], cvd_ref[u["j"]], preferred_element_type=F32)
        out = acc[:, :pair_w] / acc[:, pair_w:]
        n_q = u["n_q"]
        for i, t in enumerate(u["tiles"]):
            lo_head = out[(2 * i) * n_q:(2 * i + 1) * n_q, :]
            hi_head = out[(2 * i + 1) * n_q:(2 * i + 2) * n_q, :]
            o_ref[u["rows"], t * pair_w:(t + 1) * pair_w] = jnp.where(u["low_q"], lo_head, hi_head).astype(o_ref.dtype)

    if latent:
        @pl.when(pl.program_id(1) == 0)
        def _():
            kd, vd = key_value_tiles(slice(None), None)
            for j in range(KV_HEADS):
                kd_ref[j] = kd[j]
                vd_ref[j] = vd[j]
                ckd_ref[j] = jnp.concatenate([ck_ref[j], ck_ref[j]], axis=1).astype(BF16)
                cvd = jnp.concatenate([cv_ref[j], cv_ref[j]], axis=1).astype(BF16)
                cvd_ref[j] = jnp.concatenate([cvd, jnp.ones_like(cvd)], axis=1)

        units = query_units(slice(None), [kd_ref[j] for j in range(KV_HEADS)], [vd_ref[j] for j in range(KV_HEADS)])
    else:
        seq = PROMPT_LEN
        units = []
        for s in range(q_ref.shape[0] // seq):
            rows = slice(s * seq, (s + 1) * seq)
            units += query_units(rows, *key_value_tiles(rows, s))
    for phase in (scores, softmax, weighted_values):
        for u in units:
            phase(u)


def _attn_common_args(qn_g, kn_g):
    q_w, kv_w = Q_HEADS * HEAD_DIM, KV_HEADS * HEAD_DIM
    return (jnp.tile(qn_g, Q_HEADS).reshape(1, q_w), jnp.tile(kn_g, KV_HEADS).reshape(1, kv_w),
            _head_mean_matrix(q_w), _head_mean_matrix(kv_w))


def _attention_prompt(z, qn_g, kn_g):
    seqs = 8
    L = seqs * PROMPT_LEN
    cache_shape = jax.ShapeDtypeStruct((N_PROMPT_SEQ, KV_HEADS * HEAD_DIM, PROMPT_LEN), F32)
    cache_spec = pl.BlockSpec((seqs, KV_HEADS * HEAD_DIM, PROMPT_LEN), lambda s: (s, 0, 0))
    q_w, kv_w = Q_HEADS * HEAD_DIM, KV_HEADS * HEAD_DIM
    q_col = (5 * A_WIDTH) // q_w
    k_col = (5 * A_WIDTH + q_w) // kv_w
    const = lambda r, c: pl.BlockSpec((r, c), lambda s: (0, 0))
    return pl.pallas_call(
        functools.partial(_attn_kernel, latent=False),
        grid=(N_PROMPT_TOK // L,),
        in_specs=[
            pl.BlockSpec((L, q_w), lambda s: (s, q_col)),
            pl.BlockSpec((L, kv_w), lambda s: (s, k_col)),
            pl.BlockSpec((L, kv_w), lambda s: (s, k_col + 1)),
            const(1, q_w), const(1, kv_w), const(q_w, q_w), const(kv_w, kv_w),
        ],
        out_specs=(pl.BlockSpec((L, q_w), lambda s: (s, 0)), cache_spec, cache_spec),
        out_shape=(jax.ShapeDtypeStruct((N_PROMPT_TOK, q_w), BF16), cache_shape, cache_shape),
        compiler_params=_params("arbitrary"),
        name="attn_prompt",
    )(z, z, z, *_attn_common_args(qn_g, kn_g))


def _attention_latent(z, qn_g, kn_g, cache_k, cache_v):
    L = LATENT_LEN
    nqb = L // Q_BLOCK
    q_w, kv_w = Q_HEADS * HEAD_DIM, KV_HEADS * HEAD_DIM
    q_col = (5 * A_WIDTH) // q_w
    k_col = (5 * A_WIDTH + q_w) // kv_w
    qrow0 = N_PROMPT_TOK // Q_BLOCK
    krow0 = N_PROMPT_TOK // L
    cos, sin = _rope_tables()
    cos_q, sin_q = jnp.asarray(np.tile(cos, (1, Q_HEADS))), jnp.asarray(np.tile(sin, (1, Q_HEADS)))
    cos_k, sin_k = jnp.asarray(np.tile(cos, (1, KV_HEADS))), jnp.asarray(np.tile(sin, (1, KV_HEADS)))
    const = lambda r, c: pl.BlockSpec((r, c), lambda s, b: (0, 0))
    cache_spec = pl.BlockSpec((None, None, KV_HEADS, PAST_LEN, HEAD_DIM), lambda s, b: (s, 0, 0, 0, 0))
    return pl.pallas_call(
        functools.partial(_attn_kernel, latent=True),
        grid=(N_LATENT_SEQ, nqb),
        in_specs=[
            pl.BlockSpec((Q_BLOCK, q_w), lambda s, b: (qrow0 + s * nqb + b, q_col)),
            pl.BlockSpec((L, kv_w), lambda s, b: (krow0 + s, k_col)),
            pl.BlockSpec((L, kv_w), lambda s, b: (krow0 + s, k_col + 1)),
            const(1, q_w), const(1, kv_w), const(q_w, q_w), const(kv_w, kv_w),
            pl.BlockSpec((Q_BLOCK, q_w), lambda s, b: (b, 0)),
            pl.BlockSpec((Q_BLOCK, q_w), lambda s, b: (b, 0)),
            const(L, kv_w), const(L, kv_w),
            cache_spec, cache_spec,
        ],
        out_specs=pl.BlockSpec((Q_BLOCK, q_w), lambda s, b: (s * nqb + b, 0)),
        out_shape=jax.ShapeDtypeStruct((N_LATENT_TOK, q_w), BF16),
        scratch_shapes=[pltpu.VMEM((KV_HEADS, L, kv_w), BF16), pltpu.VMEM((KV_HEADS, L, 2 * kv_w), BF16),
                        pltpu.VMEM((KV_HEADS, PAST_LEN, kv_w), BF16), pltpu.VMEM((KV_HEADS, PAST_LEN, 2 * kv_w), BF16)],
        compiler_params=_params("arbitrary", "arbitrary"),
        name="attn_latent",
    )(z, z, z, *_attn_common_args(qn_g, kn_g), cos_q, sin_q, cos_k, sin_k, cache_k, cache_v)


def _out_proj_kernel(*refs, n_act, n_x):
    a_refs = refs[:2 * n_act]
    x_refs = refs[2 * n_act:2 * n_act + n_x]
    g_ref, mod_ref, rw_ref, w_ref, xo_ref, h_ref, lg_ref, wb_ref, rws_ref, acc_ref = refs[2 * n_act + n_x:]
    _cast_once(w_ref, wb_ref)

    @pl.when(pl.program_id(0) == 0)
    def _():
        rw = rw_ref[...]
        hi = rw.astype(BF16).astype(F32)
        lo = (rw - hi).astype(BF16).astype(F32)
        rws_ref[...] = (hi + pltpu.roll(lo, N_EXPERTS, axis=1)).astype(BF16)

    mod = mod_ref[...]
    n = OUT_PROJ_SUB_ROWS

    n_sub = xo_ref.shape[0] // n

    def sub_rows(r):
        if isinstance(r, int):
            return slice(r * n, (r + 1) * n)
        return pl.ds(pl.multiple_of(r * n, n), n)

    def project(r):
        rows = sub_rows(r)
        acc = None
        k0 = 0
        for ap_ref, al_ref in zip(a_refs[0::2], a_refs[1::2]):
            k1 = k0 + ap_ref.shape[1]
            part = jnp.dot(_select_trunk(ap_ref, al_ref, rows), wb_ref[k0:k1, :], preferred_element_type=F32)
            acc = part if acc is None else acc + part
            k0 = k1
        acc_ref[r % 2] = acc

    def finish(r):
        rows = sub_rows(r)
        x_in = x_refs[0][rows, :] if n_x == 1 else _select_trunk(*x_refs, rows)
        x = x_in + mod[2:3, :] * acc_ref[r % 2]
        xo_ref[rows, :] = x
        h = _modulated_norm(x, g_ref[...], mod, 3, 4)
        h_ref[rows, :] = _pack_rows(h)
        h_hi = h.astype(BF16)
        h_lo = (h - h_hi.astype(F32)).astype(BF16)
        both = jnp.dot(jnp.concatenate([h_hi, h_lo], axis=0), rws_ref[...], preferred_element_type=F32)
        from_hi, from_lo = both[:n], both[n:]
        lg = from_hi + pltpu.roll(from_hi, ROUTER_LANES - N_EXPERTS, axis=1) + from_lo
        lg_ref[:, rows] = lg.T[:N_EXPERTS, :]

    project(0)
    for r in range(n_sub - 1):
        project(r + 1)
        finish(r)
    finish(n_sub - 1)


def _out_proj(acts, w, xs, g, mod_l, router_wp, block_rows=1024):
    tok = lambda width: pl.BlockSpec((block_rows, width), lambda i: (i, 0))
    in_specs = [spec for ap, _ in acts for spec in _trunk_specs(block_rows, ap.shape[1])]
    in_specs += [tok(D_MODEL)] if len(xs) == 1 else list(_trunk_specs(block_rows, D_MODEL))
    in_specs += [_resident((1, D_MODEL)), _mod_spec(block_rows), _resident((D_MODEL, ROUTER_LANES)),
                 _resident(w.shape)]
    return pl.pallas_call(
        functools.partial(_out_proj_kernel, n_act=len(acts), n_x=len(xs)),
        grid=(N_TOK // block_rows,),
        in_specs=in_specs,
        out_specs=(tok(D_MODEL), tok(ROW_WORDS), pl.BlockSpec((N_EXPERTS, block_rows), lambda i: (0, i))),
        out_shape=(jax.ShapeDtypeStruct((N_TOK, D_MODEL), F32),
                   jax.ShapeDtypeStruct((N_TOK, ROW_WORDS), jnp.int32),
                   jax.ShapeDtypeStruct((N_EXPERTS, N_TOK), F32)),
        scratch_shapes=[pltpu.VMEM(w.shape, BF16), pltpu.VMEM((D_MODEL, ROUTER_LANES), BF16),
                        pltpu.VMEM((2, OUT_PROJ_SUB_ROWS, D_MODEL), F32)],
        compiler_params=_params("arbitrary"),
        name="out_proj",
    )(*[a for pair in acts for a in pair], *xs, g.reshape(1, D_MODEL), mod_l, router_wp, w)


def _router_kernel(lg_ref, rb_ref, pos_ref, w_ref, plan_ref, rank_ref):
    lg = lg_ref[...]
    ex = jnp.exp(lg - jnp.max(lg, axis=0, keepdims=True))
    scores = ex / jnp.sum(ex, axis=0, keepdims=True)
    biased = scores + rb_ref[...]
    expert = lax.broadcasted_iota(jnp.int32, biased.shape, 0)
    in_pos = expert % EXPERTS_PER_GROUP
    rank = jnp.zeros_like(biased)
    for d in range(1, EXPERTS_PER_GROUP):
        wraps = in_pos + d >= EXPERTS_PER_GROUP
        partner = jnp.where(wraps, pltpu.roll(biased, EXPERTS_PER_GROUP - d, axis=0),
                            pltpu.roll(biased, N_EXPERTS - d, axis=0))
        rank = rank + jnp.where(wraps, jnp.where(partner >= biased, 1.0, 0.0), jnp.where(partner > biased, 1.0, 0.0))
    selected = rank < 1.5
    contrib = jnp.where(selected, biased, 0.0)
    group_score = []
    for gi in range(N_GROUPS):
        r = [contrib[gi * EXPERTS_PER_GROUP + i:gi * EXPERTS_PER_GROUP + i + 1, :] for i in range(EXPERTS_PER_GROUP)]
        group_score.append(((r[0] + r[1]) + r[2]) + r[3])
    best = group_score[0]
    best_group = jnp.zeros_like(best)
    for gi in range(1, N_GROUPS):
        better = group_score[gi] > best
        best_group = jnp.where(better, float(gi), best_group)
        best = jnp.where(better, group_score[gi], best)
    in_group = (expert // EXPERTS_PER_GROUP).astype(F32) == best_group
    chosen = jnp.where(selected, jnp.where(in_group, 1.0, 0.0), 0.0)
    picked = chosen * scores
    gates = picked / jnp.sum(picked, axis=0, keepdims=True)
    lanes = 128
    n_blk = N_TOK // lanes
    li = lax.broadcasted_iota(jnp.int32, (lanes, lanes), 0)
    lj = lax.broadcasted_iota(jnp.int32, (lanes, lanes), 1)
    prefix = jnp.where(li <= lj, 1.0, 0.0).astype(BF16)
    stacked = jnp.concatenate([chosen[:, blk * lanes:(blk + 1) * lanes] for blk in range(n_blk)], axis=0)
    incl_all = jnp.dot(stacked.astype(BF16), prefix, preferred_element_type=F32)
    carry = jnp.zeros((N_EXPERTS, 1), F32)
    for blk in range(n_blk):
        cols = slice(blk * lanes, (blk + 1) * lanes)
        incl = incl_all[blk * N_EXPERTS:(blk + 1) * N_EXPERTS, :]
        rank_ref[:, cols] = incl - chosen[:, cols] + carry
        carry = carry + incl[:, lanes - 1:lanes]
    count = carry
    padded = jnp.floor((count + float(MOE_TILE - 1)) * (1.0 / MOE_TILE)) * float(MOE_TILE)
    erow = lax.broadcasted_iota(jnp.int32, (N_EXPERTS, 1), 0)
    offset = jnp.zeros((N_EXPERTS, 1), F32)
    for e in range(N_EXPERTS - 1):
        offset = offset + jnp.where(erow > e, padded[e:e + 1, :], 0.0)
    position = rank_ref[...] + offset
    ei = lax.broadcasted_iota(jnp.int32, (N_EXPERTS, N_EXPERTS), 0)
    ej = lax.broadcasted_iota(jnp.int32, (N_EXPERTS, N_EXPERTS), 1)
    lower = jnp.where(ej <= ei, 1.0, 0.0).astype(BF16)
    seen = jnp.dot(lower, chosen.astype(BF16), preferred_element_type=F32)
    first = (chosen > 0.5) & (seen < 1.5)
    second = (chosen > 0.5) & (seen > 1.5)
    pick = lambda flag, x: jnp.sum(jnp.where(flag, x, 0.0), axis=0, keepdims=True)
    pos_ref[0:1, :] = pick(first, position).astype(jnp.int32)
    pos_ref[1:2, :] = pick(second, position).astype(jnp.int32)
    w_rows = jnp.concatenate([pick(first, gates), pick(second, gates), jnp.zeros((6, N_TOK), F32)], axis=0)
    ti = lax.broadcasted_iota(jnp.int32, (8, lanes), 0)
    tj = lax.broadcasted_iota(jnp.int32, (8, lanes), 1)
    eye = jnp.where(ti == tj, 1.0, 0.0).astype(BF16)
    tn = (((0,), (0,)), ((), ()))
    hi = w_rows.astype(BF16)
    r1 = w_rows - hi.astype(F32)
    mid = r1.astype(BF16)
    lo = (r1 - mid.astype(F32)).astype(BF16)
    w_cols = lax.dot_general(hi, eye, tn, preferred_element_type=F32)
    w_cols = w_cols + lax.dot_general(mid, eye, tn, preferred_element_type=F32)
    w_cols = w_cols + lax.dot_general(lo, eye, tn, preferred_element_type=F32)
    w_ref[...] = w_cols[:, :TOP_K]
    start = (lax.broadcasted_iota(jnp.int32, (N_EXPERTS, lanes), 1) * MOE_TILE).astype(F32)
    end = offset + padded
    tile_expert = jnp.sum(jnp.where(end <= start, 1.0, 0.0), axis=0, keepdims=True)
    inside = (offset <= start) & (start < end)
    real = jnp.clip(count - (start - offset), 0.0, float(MOE_TILE))
    tile_rows = jnp.sum(jnp.where(inside, real, 0.0), axis=0, keepdims=True)
    plan_ref[0:1, :] = jnp.minimum(tile_expert, float(N_EXPERTS - 1)).astype(jnp.int32)
    plan_ref[1:2, :] = tile_rows.astype(jnp.int32)


def _router(logits_t, router_b):
    whole = lambda shape: pl.BlockSpec(shape, lambda i: (0, 0))
    return pl.pallas_call(
        _router_kernel,
        grid=(1,),
        in_specs=[whole((N_EXPERTS, N_TOK)), whole((N_EXPERTS, 1))],
        out_specs=(whole((2, N_TOK)), whole((N_TOK, TOP_K)), whole((2, 128))),
        out_shape=(jax.ShapeDtypeStruct((2, N_TOK), jnp.int32),
                   jax.ShapeDtypeStruct((N_TOK, TOP_K), F32),
                   jax.ShapeDtypeStruct((2, 128), jnp.int32)),
        scratch_shapes=[pltpu.VMEM((N_EXPERTS, N_TOK), F32)],
        compiler_params=_params("arbitrary"),
        name="router",
    )(logits_t, router_b.reshape(N_EXPERTS, 1))


def _sc_mesh():
    return plsc.VectorSubcoreMesh(core_axis_name="c", subcore_axis_name="s")


def _sc_worker_base():
    return (lax.axis_index("s") * SC_CORES + lax.axis_index("c")) * SC_TOKENS_PER_WORKER


def _moe_dispatch(h, pos_a, pos_b):
    n_chunks = SC_TOKENS_PER_WORKER // SC_CHUNK
    idx = pltpu.VMEM((SC_CHUNK,), jnp.int32)

    @functools.partial(
        pl.kernel, mesh=_sc_mesh(),
        out_type=jax.ShapeDtypeStruct((MOE_ROWS, ROW_WORDS), jnp.int32),
        scratch_types=[idx, idx, idx, idx, pltpu.VMEM((2, SC_CHUNK, ROW_WORDS), jnp.int32),
                       pltpu.SemaphoreType.DMA((6,)), pltpu.SemaphoreType.DMA((4,))],
        name="moe_dispatch",
    )
    def run(h_hbm, pa_hbm, pb_hbm, xs_hbm, ia0, ib0, ia1, ib1, rows_v, sem_in, sem_out):
        base = _sc_worker_base()
        ia, ib = (ia0, ia1), (ib0, ib1)

        def start_loads(c):
            slot = c % 2
            tok = pl.ds(pl.multiple_of(base + c * SC_CHUNK, 8), SC_CHUNK)
            return (pltpu.async_copy(pa_hbm.at[tok], ia[slot], sem_in.at[3 * slot]),
                    pltpu.async_copy(pb_hbm.at[tok], ib[slot], sem_in.at[3 * slot + 1]),
                    pltpu.async_copy(h_hbm.at[tok], rows_v.at[slot], sem_in.at[3 * slot + 2]))

        loads = start_loads(0)
        scatters = [(), ()]
        for c in range(n_chunks):
            slot = c % 2
            for cp in loads:
                cp.wait()
            if c + 1 < n_chunks:
                for cp in scatters[1 - slot]:
                    cp.wait()
                scatters[1 - slot] = ()
                loads = start_loads(c + 1)
            scatters[slot] = (pltpu.async_copy(rows_v.at[slot], xs_hbm.at[ia[slot]], sem_out.at[2 * slot]),
                              pltpu.async_copy(rows_v.at[slot], xs_hbm.at[ib[slot]], sem_out.at[2 * slot + 1]))
        for pending in scatters:
            for cp in pending:
                cp.wait()

    return run(h, pos_a, pos_b)


def _moe_collect(ys, pos_a, pos_b, tok0=0, n_tok=N_TOK):
    per_worker = n_tok // SC_WORKERS
    chunk = SC_CHUNK if per_worker % SC_CHUNK == 0 else 32
    n_chunks = per_worker // chunk
    out = jax.ShapeDtypeStruct((n_tok, ROW_WORDS), jnp.int32)
    idx = pltpu.VMEM((per_worker,), jnp.int32)
    rows = pltpu.VMEM((2, chunk, ROW_WORDS), jnp.int32)

    @functools.partial(
        pl.kernel, mesh=_sc_mesh(), out_type=(out, out),
        scratch_types=[idx, idx, rows, rows, pltpu.SemaphoreType.DMA((4,)), pltpu.SemaphoreType.DMA((4,))],
        name="moe_collect",
    )
    def run(ys_hbm, pa_hbm, pb_hbm, ya_hbm, yb_hbm, ia_v, ib_v, ra_v, rb_v, sem_g, sem_w):
        base = (lax.axis_index("s") * SC_CORES + lax.axis_index("c")) * per_worker
        mine = pl.ds(pl.multiple_of(tok0 + base, 8), per_worker)
        pltpu.sync_copy(pa_hbm.at[mine], ia_v)
        pltpu.sync_copy(pb_hbm.at[mine], ib_v)
        writes = [(), ()]
        for c in range(n_chunks):
            slot = c % 2
            for cp in writes[slot]:
                cp.wait()
            part = pl.ds(c * chunk, chunk)
            tok = pl.ds(pl.multiple_of(base + c * chunk, 8), chunk)
            ga = pltpu.async_copy(ys_hbm.at[ia_v.at[part]], ra_v.at[slot], sem_g.at[slot])
            gb = pltpu.async_copy(ys_hbm.at[ib_v.at[part]], rb_v.at[slot], sem_g.at[2 + slot])
            ga.wait()
            wa = pltpu.async_copy(ra_v.at[slot], ya_hbm.at[tok], sem_w.at[slot])
            gb.wait()
            wb = pltpu.async_copy(rb_v.at[slot], yb_hbm.at[tok], sem_w.at[2 + slot])
            writes[slot] = (wa, wb)
        for pending in writes:
            for cp in pending:
                cp.wait()

    return run(ys, pos_a, pos_b)


def _experts_kernel(plan_ref, xs_ref, wg_hbm, wu_hbm, wd_hbm, y_ref,
                    sg_ref, su_ref, sd_ref, wgb_ref, wub_ref, wdb_ref, hid_ref, sems, seg_ref, *, layer):
    n_tiles = pl.num_programs(0) * EXPERT_TILES_PER_STEP

    def weight_copies(e, slot):
        return (pltpu.make_async_copy(wg_hbm.at[layer, e], sg_ref.at[slot], sems.at[slot, 0]),
                pltpu.make_async_copy(wu_hbm.at[layer, e], su_ref.at[slot], sems.at[slot, 1]),
                pltpu.make_async_copy(wd_hbm.at[layer, e], sd_ref.at[slot], sems.at[slot, 2]))

    def tile(t, row0):
        expert = plan_ref[t]
        n_real = plan_ref[PLAN_LANES + t]
        fresh = jnp.logical_or(t == 0, expert != plan_ref[jnp.maximum(t - 1, 0)])

        @pl.when(t == 0)
        def _():
            seg_ref[0] = 0

            @pl.when(n_real > 0)
            def _():
                for cp in weight_copies(expert, 0):
                    cp.start()

        @pl.when(jnp.logical_and(n_real > 0, fresh))
        def _():
            slot = seg_ref[0] % 2
            for cp in weight_copies(expert, slot):
                cp.wait()
            wgb_ref[...] = sg_ref[slot].astype(BF16)
            wub_ref[...] = su_ref[slot].astype(BF16)
            wdb_ref[...] = sd_ref[slot].astype(BF16)
            nxt = lax.while_loop(
                lambda u: jnp.logical_and(u < n_tiles, plan_ref[jnp.minimum(u, n_tiles - 1)] == expert),
                lambda u: u + 1, t + 1)
            nxt_c = jnp.minimum(nxt, n_tiles - 1)

            @pl.when(jnp.logical_and(nxt < n_tiles, plan_ref[PLAN_LANES + nxt_c] > 0))
            def _():
                for cp in weight_copies(plan_ref[nxt_c], 1 - slot):
                    cp.start()

            seg_ref[0] = seg_ref[0] + 1

        @pl.when(n_real > 0)
        def _():
            n = EXPERT_SUB_ROWS
            n_sub = MOE_TILE // n
            row = lax.broadcasted_iota(jnp.int32, (n, xs_ref.shape[1]), 0)

            def up(r):
                rows = slice(row0 + r * n, row0 + (r + 1) * n)
                words = jnp.where(row < n_real - r * n, xs_ref[rows, :], 0)
                x = _unpack_rows(words).astype(BF16)
                a = jnp.dot(x, wgb_ref[...], preferred_element_type=F32)
                b = jnp.dot(x, wub_ref[...], preferred_element_type=F32)
                hid_ref[r] = ((a * jax.nn.sigmoid(a)) * b).astype(BF16)

            def down(r):
                rows = slice(row0 + r * n, row0 + (r + 1) * n)
                y_ref[rows, :] = _pack_rows(jnp.dot(hid_ref[r], wdb_ref[...], preferred_element_type=F32))

            up(0)
            for r in range(1, n_sub):
                up(r)
                down(r - 1)
            down(n_sub - 1)

    for q in range(EXPERT_TILES_PER_STEP):
        tile(pl.program_id(0) * EXPERT_TILES_PER_STEP + q, q * MOE_TILE)


def _experts(plan, xs, w_gate, w_up, w_down, layer):
    hbm = pl.BlockSpec(memory_space=pl.ANY)
    step_rows = MOE_TILE * EXPERT_TILES_PER_STEP
    return pl.pallas_call(
        functools.partial(_experts_kernel, layer=layer),
        grid_spec=pltpu.PrefetchScalarGridSpec(
            num_scalar_prefetch=1,
            grid=(MOE_ROWS // step_rows,),
            in_specs=[pl.BlockSpec((step_rows, ROW_WORDS), lambda j, plan: (j, 0)), hbm, hbm, hbm],
            out_specs=pl.BlockSpec((step_rows, ROW_WORDS), lambda j, plan: (j, 0)),
            scratch_shapes=[pltpu.VMEM((2, D_MODEL, D_EXPERT), F32), pltpu.VMEM((2, D_MODEL, D_EXPERT), F32),
                            pltpu.VMEM((2, D_EXPERT, D_MODEL), F32),
                            pltpu.VMEM((D_MODEL, D_EXPERT), BF16), pltpu.VMEM((D_MODEL, D_EXPERT), BF16),
                            pltpu.VMEM((D_EXPERT, D_MODEL), BF16),
                            pltpu.VMEM((MOE_TILE // EXPERT_SUB_ROWS, EXPERT_SUB_ROWS, D_EXPERT), BF16),
                            pltpu.SemaphoreType.DMA((2, 3)), pltpu.SMEM((1,), jnp.int32)],
        ),
        out_shape=jax.ShapeDtypeStruct((MOE_ROWS, ROW_WORDS), jnp.int32),
        compiler_params=_params("arbitrary"),
        name="experts",
    )(plan, xs, w_gate, w_up, w_down)


def _combine_kernel(x_ref, ya_ref, yb_ref, wt_ref, mod_ref, o_ref):
    o_ref[...] = _moe_mix(x_ref, ya_ref, yb_ref, wt_ref, mod_ref)


def _combine(x, moe_out, mod_l, tok0, n_tok, block_rows=1024):
    ya, yb, w_tok = moe_out
    b0 = tok0 // block_rows
    rows = lambda width: pl.BlockSpec((block_rows, width), lambda i: (b0 + i, 0))
    local = pl.BlockSpec((block_rows, ROW_WORDS), lambda i: (i, 0))
    return pl.pallas_call(
        _combine_kernel,
        grid=(n_tok // block_rows,),
        in_specs=[rows(D_MODEL), local, local, rows(TOP_K),
                  pl.BlockSpec((None, 6, D_MODEL), lambda i: (_cond_of_token_block(b0 + i, block_rows), 0, 0))],
        out_specs=pl.BlockSpec((block_rows, D_MODEL), lambda i: (i, 0)),
        out_shape=jax.ShapeDtypeStruct((n_tok, D_MODEL), F32),
        compiler_params=_params("arbitrary"),
        name="combine",
    )(x, ya, yb, w_tok, mod_l)


def _moe(h, logits_t, router_b, w_gate, w_up, w_down, layer, ranges=((0, N_TOK),)):
    pos, w, plan = _router(logits_t, router_b)
    xs = _moe_dispatch(h, pos[0], pos[1])
    ys = _experts(plan.reshape(-1), xs, w_gate, w_up, w_down, layer)
    return [(*_moe_collect(ys, pos[0], pos[1], tok0, n_tok), w) for tok0, n_tok in ranges]


def _dft_tables(L):
    k = np.arange(L)[:, None]
    m = np.arange(L)[None, :]
    r = (k * m) % (2 * L)
    ang = np.pi * r.astype(np.float64) / L
    fc = np.cos(ang)
    fs = np.sin(ang)
    fs[0, :] = np.where(np.arange(L) % 2 == 0, 1.0, -1.0)
    wk = np.full((L, 1), 1.0 / L)
    wk[0, 0] = 0.5 / L
    gc = (fc * wk).T
    gs = (fs * wk).T
    return [jnp.asarray(t.astype(np.float32)).astype(BF16) for t in (fc, fs, gc, gs)]


def _filter_consts(L):
    t = np.linspace(0.0, 1.0, L, dtype=np.float32)[:, None]
    w = (np.float32(2.0 * np.pi) * np.arange(L, dtype=np.float32)[:, None] / np.float32(L)).astype(np.float32)
    fb = np.linspace(1e-4, HY_BANDS - 1, HY_BANDS, dtype=np.float32)[None, :]
    emb = np.concatenate([t, np.cos(fb * w), -np.sin(fb * w)], axis=-1).astype(np.float32)
    lo = math.log(HY_DECAY_TARGET) / HY_SLOW_PCT
    hi = math.log(HY_DECAY_TARGET) / HY_FAST_PCT
    deltas = np.abs(np.linspace(lo, hi, D_MODEL, dtype=np.float32))
    decay = np.exp(-t * deltas).astype(np.float32)
    return jnp.asarray(emb), jnp.asarray(decay)


def _filter_kernel(emb_ref, w1_ref, b1_ref, w2_ref, b2_ref, fr_ref, w3f_ref, w3b_ref, dec_ref,
                   fc_ref, fs_ref, kr_ref, q_ref, krn_ref, hd_ref):
    @pl.when(pl.program_id(0) == 0)
    def _():
        fr = fr_ref[...]
        h1 = jnp.sin(fr * (jnp.dot(emb_ref[...], w1_ref[...], precision=HIGHEST,
                                   preferred_element_type=F32) + b1_ref[...]))
        hd_ref[...] = jnp.sin(fr * (jnp.dot(h1, w2_ref[...], precision=HIGHEST,
                                            preferred_element_type=F32) + b2_ref[...]))

    hd = hd_ref[...]
    dec = dec_ref[...]
    f = jnp.dot(hd, w3f_ref[...], precision=HIGHEST, preferred_element_type=F32) * dec
    g = jnp.dot(hd, w3b_ref[...], precision=HIGHEST, preferred_element_type=F32) * dec
    row = lax.broadcasted_iota(jnp.int32, f.shape, 0)
    g = jnp.where(row == 0, 0.0, g)
    s = f + g
    d = f - g
    kr = jnp.dot(fc_ref[...], s.astype(BF16), preferred_element_type=F32)
    qq = jnp.dot(fs_ref[...], d.astype(BF16), preferred_element_type=F32)
    alt = jnp.where(row % 2 == 0, 1.0, -1.0)
    nyq = jnp.sum(alt * s, axis=0, keepdims=True)
    kr_ref[...] = kr
    q_ref[...] = jnp.where(row == 0, 0.0, qq)
    krn_ref[...] = jnp.where(row == 0, nyq, kr)


def _hyena_filter_spectrum(L, w1, b1, w2, b2, w3, freq, fc, fs, cblk=256):
    emb, decay = _filter_consts(L)
    ncb = D_MODEL // cblk
    n_emb = 128
    emb = jnp.pad(emb, ((0, 0), (0, n_emb - emb.shape[1])))
    w1 = jnp.pad(w1, ((0, n_emb - w1.shape[0]), (0, 0)))
    full = lambda shape: pl.BlockSpec(shape, lambda j: tuple(0 for _ in shape))
    out_sds = jax.ShapeDtypeStruct((L, D_MODEL), F32)
    out_spec = pl.BlockSpec((L, cblk), lambda j: (0, j))
    return pl.pallas_call(
        _filter_kernel,
        grid=(ncb,),
        in_specs=[
            full((L, n_emb)), full((n_emb, HY_FFN)), full((1, HY_FFN)), full((HY_FFN, HY_FFN)),
            full((1, HY_FFN)), full((1, HY_FFN)),
            pl.BlockSpec((HY_FFN, cblk), lambda j: (0, j)),
            pl.BlockSpec((HY_FFN, cblk), lambda j: (0, ncb + j)),
            pl.BlockSpec((L, cblk), lambda j: (0, j)),
            full((L, L)), full((L, L)),
        ],
        out_specs=(out_spec, out_spec, out_spec),
        out_shape=(out_sds, out_sds, out_sds),
        scratch_shapes=[pltpu.VMEM((L, HY_FFN), F32)],
        compiler_params=_params("arbitrary"),
        name=f"hyena_filter_{L}",
    )(emb, w1, b1.reshape(1, HY_FFN), w2, b2.reshape(1, HY_FFN), freq.reshape(1, HY_FFN), w3, w3, decay, fc, fs)


def _hyena_conv_kernel(x0_ref, x1_ref, v_ref, cw0_ref, cw1_ref, cwv_ref, cb0_ref, cb1_ref, cbv_ref,
                       kr_ref, q_ref, krn_ref, ds_ref, fc_ref, fs_ref, gc_ref, gs_ref, o_ref,
                       zz_ref, gate_ref, skip_ref, yr_ref, yw_ref):
    L = fc_ref.shape[0]
    unit_w = zz_ref.shape[2]
    units = [(slice(s * L, (s + 1) * L), slice(c * unit_w, (c + 1) * unit_w))
             for s in range(x0_ref.shape[0] // L) for c in range(x0_ref.shape[1] // unit_w)]
    row = lax.broadcasted_iota(jnp.int32, (L, unit_w), 0)

    def gating(i):
        rows, cols = units[i]

        def short_conv(u_ref, w_ref, b_ref):
            u = u_ref[rows, cols].astype(F32)
            w = w_ref[:, cols]
            prev = jnp.where(row == 0, 0.0, pltpu.roll(u, 1, axis=0))
            nxt = jnp.where(row == L - 1, 0.0, pltpu.roll(u, L - 1, axis=0))
            return prev * w[0:1, :] + u * w[1:2, :] + nxt * w[2:3, :] + b_ref[:, cols]

        x0 = short_conv(x0_ref, cw0_ref, cb0_ref)
        zz = short_conv(v_ref, cwv_ref, cbv_ref) * short_conv(x1_ref, cw1_ref, cb1_ref)
        zz_ref[i] = zz.astype(BF16)
        gate_ref[i] = x0
        skip_ref[i] = x0 * zz * ds_ref[:, cols]

    def spectrum(i):
        cols = units[i][1]
        ur = jnp.dot(fc_ref[...], zz_ref[i], preferred_element_type=F32)
        p = jnp.dot(fs_ref[...], zz_ref[i], preferred_element_type=F32)
        qq = q_ref[:, cols]
        yr_ref[i] = (ur * kr_ref[:, cols] - p * qq).astype(BF16)
        yw_ref[i] = (ur * qq + p * krn_ref[:, cols]).astype(BF16)

    def synthesis(i):
        rows, cols = units[i]
        y = jnp.dot(gc_ref[...], yr_ref[i], preferred_element_type=F32)
        y = y + jnp.dot(gs_ref[...], yw_ref[i], preferred_element_type=F32)
        o_ref[rows, cols] = (gate_ref[i] * y + skip_ref[i]).astype(o_ref.dtype)

    for t in range(len(units) + 2):
        if t < len(units):
            gating(t)
        if 0 <= t - 1 < len(units):
            spectrum(t - 1)
        if 0 <= t - 2 < len(units):
            synthesis(t - 2)


def _hyena_conv(u, conv_w, conv_b, dskip, spectrum, tables, *, latent):
    L = LATENT_LEN if latent else PROMPT_LEN
    n_seq = N_LATENT_SEQ if latent else N_PROMPT_SEQ
    cblk = 512
    unit_w = 256 if latent else 512
    ncb = D_MODEL // cblk
    seqs = 1 if latent else 8
    unit = (seqs * cblk // unit_w, L, unit_w)
    row0 = (N_PROMPT_TOK // L) if latent else 0
    kr, qq, krn = spectrum
    fc, fs, gc, gs = tables

    def part(p, rows):
        if rows != L:
            return pl.BlockSpec((rows, cblk), lambda j, s: (0, p * ncb + j))
        return pl.BlockSpec((seqs * L, cblk), lambda j, s: (row0 // seqs + s, p * ncb + j))

    def const_cols(rows):
        return pl.BlockSpec((rows, cblk), lambda j, s: (0, j))

    mat = pl.BlockSpec((L, L), lambda j, s: (0, 0))
    conv_b2 = conv_b.reshape(1, 3 * D_MODEL)
    in_specs = [part(0, L), part(1, L), part(2, L),
                part(0, 3), part(1, 3), part(2, 3),
                part(0, 1), part(1, 1), part(2, 1),
                const_cols(L), const_cols(L), const_cols(L), const_cols(1),
                mat, mat, mat, mat]
    args = [u, u, u, conv_w, conv_w, conv_w, conv_b2, conv_b2, conv_b2,
            kr, qq, krn, dskip.reshape(1, D_MODEL), fc, fs, gc, gs]
    return pl.pallas_call(
        _hyena_conv_kernel,
        grid=(ncb, n_seq // seqs),
        in_specs=in_specs,
        out_specs=pl.BlockSpec((seqs * L, cblk), lambda j, s: (s, j)),
        out_shape=jax.ShapeDtypeStruct((n_seq * L, D_MODEL), BF16),
        scratch_shapes=[pltpu.VMEM(unit, BF16), pltpu.VMEM(unit, F32), pltpu.VMEM(unit, F32),
                        pltpu.VMEM(unit, BF16), pltpu.VMEM(unit, BF16)],
        compiler_params=_params("arbitrary", "arbitrary"),
        name="hyena_conv_latent" if latent else "hyena_conv_prompt",
    )(*args)


def kernel(x_prompt, x_sample, cache_k, cache_v, state_hgrn, c, c_ctx, norm_g, mod_w, mod_b, ab_in_w, hgrn_lb, hgrn_onorm_g, attn_qnorm_g, attn_knorm_g, ab_out_w, hy_in_w, hy_in_b, hy_conv_w, hy_conv_b, hy_f_w1, hy_f_b1, hy_f_w2, hy_f_b2, hy_f_w3, hy_f_freq, hy_dskip, hy_out_w, router_w, router_b, moe_w_gate, moe_w_up, moe_w_down):
    xp = x_prompt.reshape(N_PROMPT_TOK, D_MODEL)
    xl = x_sample.reshape(N_LATENT_TOK, D_MODEL)
    cond = jnp.concatenate([c_ctx[None, :], c, jnp.zeros((N_COND - 1 - N_LATENT_SEQ, D_MODEL), F32)], axis=0)
    mod = _modulation(cond, mod_w, mod_b)
    router_wp = jnp.pad(router_w, ((0, 0), (0, ROUTER_LANES - N_EXPERTS)))

    z = _in_proj0(xp, xl, norm_g[0, 0], mod[0], ab_in_w[0])
    oa_p, new_state = _hgrn(z, hgrn_lb, hgrn_onorm_g[0], None, latent=False)
    oa_l = _hgrn(z, hgrn_lb, hgrn_onorm_g[0], state_hgrn, latent=True)
    ob_p, k_fm, v_fm = _attention_prompt(z, attn_qnorm_g[0], attn_knorm_g[0])
    fm_shape = (N_PROMPT_SEQ, 1, KV_HEADS, HEAD_DIM, PROMPT_LEN)
    new_k = jnp.swapaxes(k_fm.reshape(fm_shape), -1, -2)
    new_v = jnp.swapaxes(v_fm.reshape(fm_shape), -1, -2)
    ob_l = _attention_latent(z, attn_qnorm_g[0], attn_knorm_g[0], cache_k, cache_v)
    x, h, logits_t = _out_proj([(oa_p, oa_l), (ob_p, ob_l)], ab_out_w[0], (xp, xl), norm_g[0, 1], mod[0],
                               router_wp)
    (moe_out,) = _moe(h, logits_t, router_b, moe_w_gate, moe_w_up, moe_w_down, 0)

    x, u = _in_proj1(x, moe_out, mod[0], norm_g[1, 0], mod[1], hy_in_w[0], hy_in_b[0])
    pre = []
    for latent in (False, True):
        L = LATENT_LEN if latent else PROMPT_LEN
        tables = _dft_tables(L)
        spectrum = _hyena_filter_spectrum(L, hy_f_w1[0], hy_f_b1[0], hy_f_w2[0], hy_f_b2[0], hy_f_w3[0],
                                          hy_f_freq[0], tables[0], tables[1])
        pre.append(_hyena_conv(u, hy_conv_w[0], hy_conv_b[0], hy_dskip[0], spectrum, tables, latent=latent))
    x, h, logits_t = _out_proj([tuple(pre)], hy_out_w[0], (x,), norm_g[1, 1], mod[1], router_wp)
    trunks = ((0, N_PROMPT_TOK), (N_PROMPT_TOK, N_LATENT_TOK))
    out_p, out_l = _moe(h, logits_t, router_b, moe_w_gate, moe_w_up, moe_w_down, 1, ranges=trunks)

    y_prompt = _combine(x, out_p, mod[1], *trunks[0]).reshape(N_PROMPT_SEQ, PROMPT_LEN, D_MODEL)
    y_sample = _combine(x, out_l, mod[1], *trunks[1]).reshape(N_LATENT_SEQ, LATENT_LEN, D_MODEL)
    return (y_prompt, y_sample, new_k, new_v, new_state)
```

```python
import functools
import math

import numpy as np
import jax
import jax.numpy as jnp
from jax import lax
from jax.experimental import pallas as pl
from jax.experimental.pallas import tpu as pltpu
from jax.experimental.pallas import tpu_sc as plsc

F32 = jnp.float32
BF16 = jnp.bfloat16
HIGHEST = lax.Precision.HIGHEST

D_MODEL = 1024
N_PROMPT_SEQ = 32
PROMPT_LEN = 256
N_LATENT_SEQ = 2
LATENT_LEN = 1024
PAST_LEN = 512
GRID_W = 64
N_PROMPT_TOK = N_PROMPT_SEQ * PROMPT_LEN
N_LATENT_TOK = N_LATENT_SEQ * LATENT_LEN
N_TOK = N_PROMPT_TOK + N_LATENT_TOK
N_COND = 8
EPS = 1e-6

A_WIDTH = 512
A_HEADS = 4
A_DK = 128
CHUNK = 64
HGRN_BLOCK = 128
HGRN_HEADS_PER_STEP = 4
HEAD_DIM = 64
Q_HEADS = 8
KV_HEADS = 2
Q_PER_KV = Q_HEADS // KV_HEADS
Q_BLOCK = 256
ROPE_THETA = 10000.0
ROPE_PAIRS = HEAD_DIM // 4

HY_BANDS = 16
HY_FFN = 64
HY_DECAY_TARGET = 1e-2
HY_FAST_PCT = 0.3
HY_SLOW_PCT = 1.5

N_EXPERTS = 16
N_GROUPS = 4
EXPERTS_PER_GROUP = 4
TOP_K = 2
D_EXPERT = 512
ROUTER_LANES = 128
OUT_PROJ_SUB_ROWS = 256
EXPERT_SUB_ROWS = 256
EXPERT_TILES_PER_STEP = 2
IN_PROJ_SUB_ROWS = 256
MOE_TILE = 512
MOE_ROWS = N_TOK * TOP_K + N_EXPERTS * MOE_TILE
PLAN_LANES = 128

SC_CORES = 2
SC_WORKERS = 32
SC_TOKENS_PER_WORKER = N_TOK // SC_WORKERS
SC_CHUNK = 40
ROW_WORDS = D_MODEL // 2

VMEM_LIMIT = 56 * 1024 * 1024


def _params(*sem):
    return pltpu.CompilerParams(dimension_semantics=sem, vmem_limit_bytes=VMEM_LIMIT)


def _pack_rows(x):
    n = x.shape[1] // 2
    bits = pltpu.bitcast(x.astype(BF16).astype(F32), jnp.uint32)
    return pltpu.bitcast(bits[:, :n] | (bits[:, n:] >> 16), jnp.int32)


def _unpack_rows(p):
    bits = pltpu.bitcast(p, jnp.uint32)
    hi = pltpu.bitcast(bits & jnp.uint32(0xFFFF0000), F32)
    lo = pltpu.bitcast(bits << 16, F32)
    return jnp.concatenate([hi, lo], axis=1)


def _cond_of_token_block(i, block_rows):
    start = i * block_rows
    return jnp.where(start < N_PROMPT_TOK, 0, 1 + (start - N_PROMPT_TOK) // LATENT_LEN)


def _mod_kernel(cond_ref, w_ref, b_ref, o_ref):
    cnd = cond_ref[...]
    s = cnd * jax.nn.sigmoid(cnd)
    s_hi = s.astype(BF16)
    s_lo = (s - s_hi.astype(F32)).astype(BF16)
    w = w_ref[...]
    w_hi = w.astype(BF16)
    w_lo = (w - w_hi.astype(F32)).astype(BF16)
    acc = jnp.dot(s_hi, w_hi, preferred_element_type=F32)
    acc = acc + jnp.dot(s_lo, w_hi, preferred_element_type=F32)
    acc = acc + jnp.dot(s_hi, w_lo, preferred_element_type=F32)
    o_ref[...] = acc + b_ref[...]


def _modulation(cond, mod_w, mod_b):
    depth = mod_w.shape[0]
    n_mod = 6
    cols = 2 * D_MODEL
    n_step = n_mod * D_MODEL // cols
    out = pl.pallas_call(
        _mod_kernel,
        grid=(depth, n_step),
        in_specs=[
            pl.BlockSpec((N_COND, D_MODEL), lambda l, j: (0, 0)),
            pl.BlockSpec((None, D_MODEL, cols), lambda l, j: (l, 0, j)),
            pl.BlockSpec((None, 1, cols), lambda l, j: (l, 0, j)),
        ],
        out_specs=pl.BlockSpec((None, N_COND, cols), lambda l, j: (l, 0, j)),
        out_shape=jax.ShapeDtypeStruct((depth, N_COND, n_mod * D_MODEL), F32),
        compiler_params=_params("arbitrary", "arbitrary"),
        name="modulation",
    )(cond, mod_w, mod_b.reshape(depth, 1, n_mod * D_MODEL))
    return out.reshape(depth, N_COND, n_mod, D_MODEL)


def _modulated_norm(x, g, mod, shift_row, scale_row):
    ms = jnp.mean(x * x, axis=-1, keepdims=True)
    y = x * lax.rsqrt(ms + EPS) * g
    return y * (1.0 + mod[scale_row:scale_row + 1, :]) + mod[shift_row:shift_row + 1, :]


def _trunk_specs(block_rows, width):
    n_prompt_blocks = N_PROMPT_TOK // block_rows
    return (pl.BlockSpec((block_rows, width), lambda i: (jnp.minimum(i, n_prompt_blocks - 1), 0)),
            pl.BlockSpec((block_rows, width), lambda i: (jnp.maximum(i - n_prompt_blocks, 0), 0)))


def _select_trunk(p_ref, l_ref, rows=slice(None)):
    block_rows = p_ref.shape[0]
    return jnp.where(pl.program_id(0) < N_PROMPT_TOK // block_rows, p_ref[rows, :], l_ref[rows, :])


def _cast_once(w_ref, wb_ref):
    @pl.when(pl.program_id(0) == 0)
    def _():
        wb_ref[...] = w_ref[...].astype(BF16)


def _resident(shape):
    return pl.BlockSpec(shape, lambda i: tuple(0 for _ in shape), pipeline_mode=pl.Buffered(1))


def _mod_spec(block_rows):
    return pl.BlockSpec((None, 6, D_MODEL), lambda i: (_cond_of_token_block(i, block_rows), 0, 0))


def _in_proj0_kernel(xp_ref, xl_ref, g_ref, mod_ref, w_ref, o_ref, wb_ref, hb_ref):
    _cast_once(w_ref, wb_ref)
    n = IN_PROJ_SUB_ROWS
    n_sub = xp_ref.shape[0] // n

    def prepare(r):
        x = _select_trunk(xp_ref, xl_ref, slice(r * n, (r + 1) * n))
        hb_ref[r] = _modulated_norm(x, g_ref[...], mod_ref[...], 0, 1).astype(BF16)

    def project(r):
        u = jnp.dot(hb_ref[r], wb_ref[...], preferred_element_type=F32)
        o_ref[r * n:(r + 1) * n, :] = u.astype(o_ref.dtype)

    prepare(0)
    for r in range(1, n_sub):
        prepare(r)
        project(r - 1)
    project(n_sub - 1)


def _in_proj0(x_prompt, x_latent, g, mod_l, w, block_rows=512):
    n = w.shape[1]
    return pl.pallas_call(
        _in_proj0_kernel,
        grid=(N_TOK // block_rows,),
        in_specs=[*_trunk_specs(block_rows, D_MODEL), _resident((1, D_MODEL)), _mod_spec(block_rows),
                  _resident((D_MODEL, n))],
        out_specs=pl.BlockSpec((block_rows, n), lambda i: (i, 0)),
        out_shape=jax.ShapeDtypeStruct((N_TOK, n), BF16),
        scratch_shapes=[pltpu.VMEM((D_MODEL, n), BF16),
                        pltpu.VMEM((block_rows // IN_PROJ_SUB_ROWS, IN_PROJ_SUB_ROWS, D_MODEL), BF16)],
        compiler_params=_params("arbitrary"),
        name="in_proj0",
    )(x_prompt, x_latent, g.reshape(1, D_MODEL), mod_l, w)


def _moe_mix(x_ref, ya_ref, yb_ref, wt_ref, mod_ref, rows=slice(None)):
    wt = wt_ref[rows, :]
    mix = wt[:, 0:1] * _unpack_rows(ya_ref[rows, :]) + wt[:, 1:2] * _unpack_rows(yb_ref[rows, :])
    return x_ref[rows, :] + mod_ref[5:6, :] * mix


def _in_proj1_kernel(x_ref, ya_ref, yb_ref, wt_ref, modp_ref, g_ref, mod_ref, w_ref, b_ref, xo_ref, o_ref,
                     wb_ref, hb_ref):
    _cast_once(w_ref, wb_ref)
    n = IN_PROJ_SUB_ROWS
    n_sub = x_ref.shape[0] // n

    def prepare(r):
        rows = slice(r * n, (r + 1) * n)
        x = _moe_mix(x_ref, ya_ref, yb_ref, wt_ref, modp_ref, rows)
        xo_ref[rows, :] = x
        hb_ref[r] = _modulated_norm(x, g_ref[...], mod_ref[...], 0, 1).astype(BF16)

    def project(r):
        rows = slice(r * n, (r + 1) * n)
        u = jnp.dot(hb_ref[r], wb_ref[...], preferred_element_type=F32) + b_ref[...]
        o_ref[rows, :] = u.astype(o_ref.dtype)

    prepare(0)
    for r in range(1, n_sub):
        prepare(r)
        project(r - 1)
    project(n_sub - 1)


def _in_proj1(x, moe_out, mod_prev, g, mod_l, w, bias, block_rows=512):
    ya, yb, w_tok = moe_out
    n = w.shape[1]
    tok = pl.BlockSpec((block_rows, D_MODEL), lambda i: (i, 0))
    packed = pl.BlockSpec((block_rows, ROW_WORDS), lambda i: (i, 0))
    return pl.pallas_call(
        _in_proj1_kernel,
        grid=(N_TOK // block_rows,),
        in_specs=[tok, packed, packed, pl.BlockSpec((block_rows, TOP_K), lambda i: (i, 0)), _mod_spec(block_rows),
                  _resident((1, D_MODEL)), _mod_spec(block_rows), _resident((D_MODEL, n)), _resident((1, n))],
        out_specs=(tok, pl.BlockSpec((block_rows, n), lambda i: (i, 0))),
        out_shape=(jax.ShapeDtypeStruct((N_TOK, D_MODEL), F32), jax.ShapeDtypeStruct((N_TOK, n), BF16)),
        scratch_shapes=[pltpu.VMEM((D_MODEL, n), BF16),
                        pltpu.VMEM((block_rows // IN_PROJ_SUB_ROWS, IN_PROJ_SUB_ROWS, D_MODEL), BF16)],
        compiler_params=_params("arbitrary"),
        name="in_proj1",
    )(x, ya, yb, w_tok, mod_prev, g.reshape(1, D_MODEL), mod_l, w, bias.reshape(1, n))


def _hgrn_kernel(*refs, seq_len, with_state):
    if with_state:
        (q_ref, zf_ref, zb_ref, i_ref, ga_ref, lb_ref, og_ref, s0_ref, o_ref, of_ref, ob_ref) = refs
    else:
        (q_ref, zf_ref, zb_ref, i_ref, ga_ref, lb_ref, og_ref, o_ref, s_ref, of_ref, ob_ref) = refs
    n_blocks = seq_len // HGRN_BLOCK
    chunks_per_block = HGRN_BLOCK // CHUNK

    lbr = lb_ref[...]
    mx = jnp.maximum(lbr[0], lbr[1])
    e0 = jnp.exp(lbr[0] - mx)
    e1 = jnp.exp(lbr[1] - mx)
    lb = e0 / (e0 + e1)

    row = lax.broadcasted_iota(jnp.int32, (HGRN_BLOCK, HGRN_BLOCK), 0)
    col = lax.broadcasted_iota(jnp.int32, (HGRN_BLOCK, HGRN_BLOCK), 1)
    same_chunk = (row // CHUNK) == (col // CHUNK)
    nt = (((1,), (1,)), ((), ()))
    tn = (((0,), (0,)), ((), ()))

    def per_chunk_row(x, idx):
        return jnp.concatenate(
            [jnp.broadcast_to(x[n * CHUNK + idx:n * CHUNK + idx + 1, :], (CHUNK, x.shape[1]))
             for n in range(chunks_per_block)], axis=0)

    def in_chunk_cumsum(tri, x):
        hi = x.astype(BF16)
        lo = (x - hi.astype(F32)).astype(BF16)
        return jnp.dot(tri, hi, preferred_element_type=F32) + jnp.dot(tri, lo, preferred_element_type=F32)

    def prepare(blk, cols, z_ref, lbd, forward):
        rows = slice(blk * HGRN_BLOCK, (blk + 1) * HGRN_BLOCK)
        keep = (same_chunk & (col <= row)) if forward else (same_chunk & (col >= row))
        tri = jnp.where(keep, 1.0, 0.0).astype(BF16)
        mid = CHUNK // 2 if forward else CHUNK - 1 - CHUNK // 2
        last = CHUNK - 1 if forward else 0
        f = lbd + (1.0 - lbd) * jax.nn.sigmoid(z_ref[rows, cols].astype(F32))
        lf = jnp.log(f)
        k = 1.0 - f
        q = q_ref[rows, cols].astype(F32)
        b = in_chunk_cumsum(tri, lf)
        bm = per_chunk_row(b, mid)
        bl = per_chunk_row(b, last)
        return dict(
            rows=rows, cols=cols, keep=keep, forward=forward,
            vb=i_ref[rows, cols].astype(BF16),
            qe=(q * jnp.exp(b - bm)).astype(BF16), ke=(k * jnp.exp(bm - b)).astype(BF16),
            qb=(q * jnp.exp(b)).astype(BF16), ks=(k * jnp.exp(bl - b)).astype(BF16), decay=jnp.exp(bl))

    def within_chunks(u):
        att = lax.dot_general(u["qe"], u["ke"], nt, preferred_element_type=F32)
        att = jnp.where(u["keep"], att, 0.0)
        u["o_intra"] = jnp.dot(att.astype(BF16), u["vb"], preferred_element_type=F32)
        u["upd"] = [lax.dot_general(u["vb"][n * CHUNK:(n + 1) * CHUNK], u["ks"][n * CHUNK:(n + 1) * CHUNK], tn,
                                    preferred_element_type=F32) for n in range(chunks_per_block)]

    def across_chunks(u, st, out_ref):
        order = range(chunks_per_block) if u["forward"] else range(chunks_per_block - 1, -1, -1)
        o_inter = [None] * chunks_per_block
        for n in order:
            cr = slice(n * CHUNK, (n + 1) * CHUNK)
            o_inter[n] = lax.dot_general(u["qb"][cr], st.astype(BF16), nt, preferred_element_type=F32)
            st = st * u["decay"][n * CHUNK:n * CHUNK + 1, :] + u["upd"][n]
        out_ref[u["rows"], u["cols"]] = u["o_intra"] + jnp.concatenate(o_inter, axis=0)
        return st

    n_heads = q_ref.shape[1] // A_DK
    head_cols = [slice(hd * A_DK, (hd + 1) * A_DK) for hd in range(n_heads)]
    if with_state:
        states = {(hd, d): s0_ref[d, hd].T for hd in range(n_heads) for d in range(2)}
    else:
        states = {(hd, d): jnp.zeros((A_DK, A_DK), F32) for hd in range(n_heads) for d in range(2)}
    for step in range(n_blocks):
        units = {}
        for hd, cols in enumerate(head_cols):
            units[hd, 0] = prepare(step, cols, zf_ref, lb[0:1, cols], True)
            units[hd, 1] = prepare(n_blocks - 1 - step, cols, zb_ref, lb[1:2, cols], False)
        for u in units.values():
            within_chunks(u)
        for key, u in units.items():
            states[key] = across_chunks(u, states[key], of_ref if key[1] == 0 else ob_ref)
    for hd, cols in enumerate(head_cols):
        if not with_state:
            s_ref[0, hd] = states[hd, 0].T
            s_ref[1, hd] = states[hd, 1].T
        o = of_ref[:, cols] + ob_ref[:, cols]
        o = o * lax.rsqrt(jnp.mean(o * o, axis=-1, keepdims=True) + EPS) * og_ref[:, cols]
        ga = ga_ref[:, cols].astype(F32)
        o_ref[:, cols] = (o * (ga * jax.nn.sigmoid(ga))).astype(o_ref.dtype)


def _hgrn(z, hgrn_lb, onorm_g, state, *, latent):
    seq_len = LATENT_LEN if latent else PROMPT_LEN
    n_seq = N_LATENT_SEQ if latent else N_PROMPT_SEQ
    row0 = (N_PROMPT_TOK // seq_len) if latent else 0

    hw = HGRN_HEADS_PER_STEP * A_DK
    n_hg = A_HEADS // HGRN_HEADS_PER_STEP

    def zspec(part):
        return pl.BlockSpec((seq_len, hw), lambda s, h: (row0 + s, part * n_hg + h))

    in_specs = [zspec(0), zspec(1), zspec(2), zspec(3), zspec(4),
                pl.BlockSpec((2, 2, hw), lambda s, h: (0, 0, h)),
                pl.BlockSpec((1, hw), lambda s, h: (0, h))]
    args = [z, z, z, z, z, hgrn_lb, onorm_g.reshape(1, A_WIDTH)]
    state_spec = pl.BlockSpec((None, None, 2, HGRN_HEADS_PER_STEP, A_DK, A_DK), lambda s, h: (s, 0, 0, h, 0, 0))
    o_shape = jax.ShapeDtypeStruct((n_seq * seq_len, A_WIDTH), BF16)
    o_spec = pl.BlockSpec((seq_len, hw), lambda s, h: (s, h))
    if latent:
        in_specs.append(state_spec)
        args.append(state)
        out_shape, out_specs = o_shape, o_spec
    else:
        out_shape = (o_shape, jax.ShapeDtypeStruct((n_seq, 1, 2, A_HEADS, A_DK, A_DK), F32))
        out_specs = (o_spec, state_spec)
    return pl.pallas_call(
        functools.partial(_hgrn_kernel, seq_len=seq_len, with_state=latent),
        grid=(n_seq, n_hg),
        in_specs=in_specs,
        out_specs=out_specs,
        out_shape=out_shape,
        scratch_shapes=[pltpu.VMEM((seq_len, hw), F32), pltpu.VMEM((seq_len, hw), F32)],
        compiler_params=_params("arbitrary", "arbitrary"),
        name="hgrn_latent" if latent else "hgrn_prompt",
    )(*args)


def _rope_tables():
    pos = np.arange(LATENT_LEN)
    row, colp = pos // GRID_W, pos % GRID_W
    inv = ROPE_THETA ** (-np.arange(ROPE_PAIRS, dtype=np.float32) / ROPE_PAIRS)
    inv = inv.astype(np.float32)
    ang_r = (row.astype(np.float32)[:, None] * inv).astype(np.float32)
    ang_c = (colp.astype(np.float32)[:, None] * inv).astype(np.float32)
    cos = np.concatenate([np.cos(ang_r), np.cos(ang_r), np.cos(ang_c), np.cos(ang_c)], axis=1)
    sin = np.concatenate([-np.sin(ang_r), np.sin(ang_r), -np.sin(ang_c), np.sin(ang_c)], axis=1)
    return cos.astype(np.float32), sin.astype(np.float32)


def _head_mean_matrix(width):
    idx = np.arange(width) // HEAD_DIM
    return jnp.asarray((idx[:, None] == idx[None, :]).astype(np.float32) / HEAD_DIM).astype(BF16)


def _attn_kernel(*refs, latent):
    if latent:
        (q_ref, k_ref, v_ref, qg_ref, kg_ref, gq_ref, gk_ref, cosq_ref, sinq_ref, cosk_ref, sink_ref,
         ck_ref, cv_ref, o_ref, kd_ref, vd_ref, ckd_ref, cvd_ref) = refs
    else:
        (q_ref, k_ref, v_ref, qg_ref, kg_ref, gq_ref, gk_ref, o_ref, kout_ref, vout_ref) = refs
    pair_w = 2 * HEAD_DIM

    def head_norm(x, mean_ref, gain):
        sq = x * x
        hi = sq.astype(BF16)
        lo = (sq - hi.astype(F32)).astype(BF16)
        ms = jnp.dot(hi, mean_ref[...], preferred_element_type=F32)
        ms = ms + jnp.dot(lo, mean_ref[...], preferred_element_type=F32)
        return x * lax.rsqrt(ms + EPS) * gain

    def rope(x, cos, sin):
        n = x.shape[1]
        lane = lax.broadcasted_iota(jnp.int32, x.shape, 1)
        first_of_pair = (lane // ROPE_PAIRS) % 2 == 0
        swapped = jnp.where(first_of_pair, pltpu.roll(x, n - ROPE_PAIRS, axis=1), pltpu.roll(x, ROPE_PAIRS, axis=1))
        return x * cos + swapped * sin

    nt = (((1,), (1,)), ((), ()))

    def key_value_tiles(rows, seq_idx):
        k = head_norm(k_ref[rows, :].astype(F32), gk_ref, kg_ref[...])
        if latent:
            k = rope(k, cosk_ref[...], sink_ref[...])
        v = v_ref[rows, :].astype(F32)
        low_kv = lax.broadcasted_iota(jnp.int32, k.shape, 1) < HEAD_DIM
        k_swapped = pltpu.roll(k, HEAD_DIM, axis=1)
        v_swapped = pltpu.roll(v, HEAD_DIM, axis=1)
        if not latent:
            kout_ref[seq_idx] = k.T
            vout_ref[seq_idx] = v.T
        kd, vd = [], []
        for j in range(KV_HEADS):
            kd.append((jnp.where(low_kv, k, k_swapped) if j == 0 else jnp.where(low_kv, k_swapped, k)).astype(BF16))
            vj = (jnp.where(low_kv, v, v_swapped) if j == 0 else jnp.where(low_kv, v_swapped, v)).astype(BF16)
            vd.append(jnp.concatenate([vj, jnp.ones_like(vj)], axis=1))
        return kd, vd

    def query_units(rows, kd, vd):
        q = head_norm(q_ref[rows, :].astype(F32), gq_ref, qg_ref[...])
        if latent:
            q = rope(q, cosq_ref[...], sinq_ref[...])
        q = q * (HEAD_DIM ** -0.5)
        n_q = q.shape[0]
        low_q = lax.broadcasted_iota(jnp.int32, (n_q, pair_w), 1) < HEAD_DIM
        units = []
        for j in range(KV_HEADS):
            tiles = range(j * Q_PER_KV // 2, (j + 1) * Q_PER_KV // 2)
            parts = []
            for t in tiles:
                qt = q[:, t * pair_w:(t + 1) * pair_w]
                parts += [jnp.where(low_q, qt, 0.0), jnp.where(low_q, 0.0, qt)]
            units.append(dict(j=j, rows=rows, tiles=tiles, n_q=n_q, low_q=low_q, kd=kd[j], vd=vd[j],
                              qs=jnp.concatenate(parts, axis=0).astype(BF16)))
        return units

    def scores(u):
        u["s_new"] = lax.dot_general(u["qs"], u["kd"], nt, preferred_element_type=F32)
        if latent:
            u["s_old"] = lax.dot_general(u["qs"], ckd_ref[u["j"]], nt, preferred_element_type=F32)

    def softmax(u):
        m = jnp.max(u["s_new"], axis=-1, keepdims=True)
        if latent:
            m = jnp.maximum(m, jnp.max(u["s_old"], axis=-1, keepdims=True))
        u["p_new"] = jnp.exp(u.pop("s_new") - m).astype(BF16)
        if latent:
            u["p_old"] = jnp.exp(u.pop("s_old") - m).astype(BF16)

    def weighted_values(u):
        acc = jnp.dot(u["p_new"], u["vd"], preferred_element_type=F32)
        if latent:
            acc = acc + jnp.dot(u["p_old"], cvd_ref[u["j"]], preferred_element_type=F32)
        out = acc[:, :pair_w] / acc[:, pair_w:]
        n_q = u["n_q"]
        for i, t in enumerate(u["tiles"]):
            lo_head = out[(2 * i) * n_q:(2 * i + 1) * n_q, :]
            hi_head = out[(2 * i + 1) * n_q:(2 * i + 2) * n_q, :]
            o_ref[u["rows"], t * pair_w:(t + 1) * pair_w] = jnp.where(u["low_q"], lo_head, hi_head).astype(o_ref.dtype)

    if latent:
        @pl.when(pl.program_id(1) == 0)
        def _():
            kd, vd = key_value_tiles(slice(None), None)
            for j in range(KV_HEADS):
                kd_ref[j] = kd[j]
                vd_ref[j] = vd[j]
                ckd_ref[j] = jnp.concatenate([ck_ref[j], ck_ref[j]], axis=1).astype(BF16)
                cvd = jnp.concatenate([cv_ref[j], cv_ref[j]], axis=1).astype(BF16)
                cvd_ref[j] = jnp.concatenate([cvd, jnp.ones_like(cvd)], axis=1)

        units = query_units(slice(None), [kd_ref[j] for j in range(KV_HEADS)], [vd_ref[j] for j in range(KV_HEADS)])
    else:
        seq = PROMPT_LEN
        units = []
        for s in range(q_ref.shape[0] // seq):
            rows = slice(s * seq, (s + 1) * seq)
            units += query_units(rows, *key_value_tiles(rows, s))
    for phase in (scores, softmax, weighted_values):
        for u in units:
            phase(u)


def _attn_common_args(qn_g, kn_g):
    q_w, kv_w = Q_HEADS * HEAD_DIM, KV_HEADS * HEAD_DIM
    return (jnp.tile(qn_g, Q_HEADS).reshape(1, q_w), jnp.tile(kn_g, KV_HEADS).reshape(1, kv_w),
            _head_mean_matrix(q_w), _head_mean_matrix(kv_w))


def _attention_prompt(z, qn_g, kn_g):
    seqs = 8
    L = seqs * PROMPT_LEN
    cache_shape = jax.ShapeDtypeStruct((N_PROMPT_SEQ, KV_HEADS * HEAD_DIM, PROMPT_LEN), F32)
    cache_spec = pl.BlockSpec((seqs, KV_HEADS * HEAD_DIM, PROMPT_LEN), lambda s: (s, 0, 0))
    q_w, kv_w = Q_HEADS * HEAD_DIM, KV_HEADS * HEAD_DIM
    q_col = (5 * A_WIDTH) // q_w
    k_col = (5 * A_WIDTH + q_w) // kv_w
    const = lambda r, c: pl.BlockSpec((r, c), lambda s: (0, 0))
    return pl.pallas_call(
        functools.partial(_attn_kernel, latent=False),
        grid=(N_PROMPT_TOK // L,),
        in_specs=[
            pl.BlockSpec((L, q_w), lambda s: (s, q_col)),
            pl.BlockSpec((L, kv_w), lambda s: (s, k_col)),
            pl.BlockSpec((L, kv_w), lambda s: (s, k_col + 1)),
            const(1, q_w), const(1, kv_w), const(q_w, q_w), const(kv_w, kv_w),
        ],
        out_specs=(pl.BlockSpec((L, q_w), lambda s: (s, 0)), cache_spec, cache_spec),
        out_shape=(jax.ShapeDtypeStruct((N_PROMPT_TOK, q_w), BF16), cache_shape, cache_shape),
        compiler_params=_params("arbitrary"),
        name="attn_prompt",
    )(z, z, z, *_attn_common_args(qn_g, kn_g))


def _attention_latent(z, qn_g, kn_g, cache_k, cache_v):
    L = LATENT_LEN
    nqb = L // Q_BLOCK
    q_w, kv_w = Q_HEADS * HEAD_DIM, KV_HEADS * HEAD_DIM
    q_col = (5 * A_WIDTH) // q_w
    k_col = (5 * A_WIDTH + q_w) // kv_w
    qrow0 = N_PROMPT_TOK // Q_BLOCK
    krow0 = N_PROMPT_TOK // L
    cos, sin = _rope_tables()
    cos_q, sin_q = jnp.asarray(np.tile(cos, (1, Q_HEADS))), jnp.asarray(np.tile(sin, (1, Q_HEADS)))
    cos_k, sin_k = jnp.asarray(np.tile(cos, (1, KV_HEADS))), jnp.asarray(np.tile(sin, (1, KV_HEADS)))
    const = lambda r, c: pl.BlockSpec((r, c), lambda s, b: (0, 0))
    cache_spec = pl.BlockSpec((None, None, KV_HEADS, PAST_LEN, HEAD_DIM), lambda s, b: (s, 0, 0, 0, 0))
    return pl.pallas_call(
        functools.partial(_attn_kernel, latent=True),
        grid=(N_LATENT_SEQ, nqb),
        in_specs=[
            pl.BlockSpec((Q_BLOCK, q_w), lambda s, b: (qrow0 + s * nqb + b, q_col)),
            pl.BlockSpec((L, kv_w), lambda s, b: (krow0 + s, k_col)),
            pl.BlockSpec((L, kv_w), lambda s, b: (krow0 + s, k_col + 1)),
            const(1, q_w), const(1, kv_w), const(q_w, q_w), const(kv_w, kv_w),
            pl.BlockSpec((Q_BLOCK, q_w), lambda s, b: (b, 0)),
            pl.BlockSpec((Q_BLOCK, q_w), lambda s, b: (b, 0)),
            const(L, kv_w), const(L, kv_w),
            cache_spec, cache_spec,
        ],
        out_specs=pl.BlockSpec((Q_BLOCK, q_w), lambda s, b: (s * nqb + b, 0)),
        out_shape=jax.ShapeDtypeStruct((N_LATENT_TOK, q_w), BF16),
        scratch_shapes=[pltpu.VMEM((KV_HEADS, L, kv_w), BF16), pltpu.VMEM((KV_HEADS, L, 2 * kv_w), BF16),
                        pltpu.VMEM((KV_HEADS, PAST_LEN, kv_w), BF16), pltpu.VMEM((KV_HEADS, PAST_LEN, 2 * kv_w), BF16)],
        compiler_params=_params("arbitrary", "arbitrary"),
        name="attn_latent",
    )(z, z, z, *_attn_common_args(qn_g, kn_g), cos_q, sin_q, cos_k, sin_k, cache_k, cache_v)


def _out_proj_kernel(*refs, n_act, n_x):
    a_refs = refs[:2 * n_act]
    x_refs = refs[2 * n_act:2 * n_act + n_x]
    g_ref, mod_ref, rw_ref, w_ref, xo_ref, h_ref, lg_ref, wb_ref, rws_ref, acc_ref = refs[2 * n_act + n_x:]
    _cast_once(w_ref, wb_ref)

    @pl.when(pl.program_id(0) == 0)
    def _():
        rw = rw_ref[...]
        hi = rw.astype(BF16).astype(F32)
        lo = (rw - hi).astype(BF16).astype(F32)
        rws_ref[...] = (hi + pltpu.roll(lo, N_EXPERTS, axis=1)).astype(BF16)

    mod = mod_ref[...]
    n = OUT_PROJ_SUB_ROWS

    n_sub = xo_ref.shape[0] // n

    def sub_rows(r):
        if isinstance(r, int):
            return slice(r * n, (r + 1) * n)
        return pl.ds(pl.multiple_of(r * n, n), n)

    def project(r):
        rows = sub_rows(r)
        acc = None
        k0 = 0
        for ap_ref, al_ref in zip(a_refs[0::2], a_refs[1::2]):
            k1 = k0 + ap_ref.shape[1]
            part = jnp.dot(_select_trunk(ap_ref, al_ref, rows), wb_ref[k0:k1, :], preferred_element_type=F32)
            acc = part if acc is None else acc + part
            k0 = k1
        acc_ref[r % 2] = acc

    def finish(r):
        rows = sub_rows(r)
        x_in = x_refs[0][rows, :] if n_x == 1 else _select_trunk(*x_refs, rows)
        x = x_in + mod[2:3, :] * acc_ref[r % 2]
        xo_ref[rows, :] = x
        h = _modulated_norm(x, g_ref[...], mod, 3, 4)
        h_ref[rows, :] = _pack_rows(h)
        h_hi = h.astype(BF16)
        h_lo = (h - h_hi.astype(F32)).astype(BF16)
        both = jnp.dot(jnp.concatenate([h_hi, h_lo], axis=0), rws_ref[...], preferred_element_type=F32)
        from_hi, from_lo = both[:n], both[n:]
        lg = from_hi + pltpu.roll(from_hi, ROUTER_LANES - N_EXPERTS, axis=1) + from_lo
        lg_ref[:, rows] = lg.T[:N_EXPERTS, :]

    project(0)
    for r in range(n_sub - 1):
        project(r + 1)
        finish(r)
    finish(n_sub - 1)


def _out_proj(acts, w, xs, g, mod_l, router_wp, block_rows=1024):
    tok = lambda width: pl.BlockSpec((block_rows, width), lambda i: (i, 0))
    in_specs = [spec for ap, _ in acts for spec in _trunk_specs(block_rows, ap.shape[1])]
    in_specs += [tok(D_MODEL)] if len(xs) == 1 else list(_trunk_specs(block_rows, D_MODEL))
    in_specs += [_resident((1, D_MODEL)), _mod_spec(block_rows), _resident((D_MODEL, ROUTER_LANES)),
                 _resident(w.shape)]
    return pl.pallas_call(
        functools.partial(_out_proj_kernel, n_act=len(acts), n_x=len(xs)),
        grid=(N_TOK // block_rows,),
        in_specs=in_specs,
        out_specs=(tok(D_MODEL), tok(ROW_WORDS), pl.BlockSpec((N_EXPERTS, block_rows), lambda i: (0, i))),
        out_shape=(jax.ShapeDtypeStruct((N_TOK, D_MODEL), F32),
                   jax.ShapeDtypeStruct((N_TOK, ROW_WORDS), jnp.int32),
                   jax.ShapeDtypeStruct((N_EXPERTS, N_TOK), F32)),
        scratch_shapes=[pltpu.VMEM(w.shape, BF16), pltpu.VMEM((D_MODEL, ROUTER_LANES), BF16),
                        pltpu.VMEM((2, OUT_PROJ_SUB_ROWS, D_MODEL), F32)],
        compiler_params=_params("arbitrary"),
        name="out_proj",
    )(*[a for pair in acts for a in pair], *xs, g.reshape(1, D_MODEL), mod_l, router_wp, w)


def _router_kernel(lg_ref, rb_ref, pos_ref, w_ref, plan_ref, rank_ref):
    lg = lg_ref[...]
    ex = jnp.exp(lg - jnp.max(lg, axis=0, keepdims=True))
    scores = ex / jnp.sum(ex, axis=0, keepdims=True)
    biased = scores + rb_ref[...]
    expert = lax.broadcasted_iota(jnp.int32, biased.shape, 0)
    in_pos = expert % EXPERTS_PER_GROUP
    rank = jnp.zeros_like(biased)
    for d in range(1, EXPERTS_PER_GROUP):
        wraps = in_pos + d >= EXPERTS_PER_GROUP
        partner = jnp.where(wraps, pltpu.roll(biased, EXPERTS_PER_GROUP - d, axis=0),
                            pltpu.roll(biased, N_EXPERTS - d, axis=0))
        rank = rank + jnp.where(wraps, jnp.where(partner >= biased, 1.0, 0.0), jnp.where(partner > biased, 1.0, 0.0))
    selected = rank < 1.5
    contrib = jnp.where(selected, biased, 0.0)
    group_score = []
    for gi in range(N_GROUPS):
        r = [contrib[gi * EXPERTS_PER_GROUP + i:gi * EXPERTS_PER_GROUP + i + 1, :] for i in range(EXPERTS_PER_GROUP)]
        group_score.append(((r[0] + r[1]) + r[2]) + r[3])
    best = group_score[0]
    best_group = jnp.zeros_like(best)
    for gi in range(1, N_GROUPS):
        better = group_score[gi] > best
        best_group = jnp.where(better, float(gi), best_group)
        best = jnp.where(better, group_score[gi], best)
    in_group = (expert // EXPERTS_PER_GROUP).astype(F32) == best_group
    chosen = jnp.where(selected, jnp.where(in_group, 1.0, 0.0), 0.0)
    picked = chosen * scores
    gates = picked / jnp.sum(picked, axis=0, keepdims=True)
    lanes = 128
    n_blk = N_TOK // lanes
    li = lax.broadcasted_iota(jnp.int32, (lanes, lanes), 0)
    lj = lax.broadcasted_iota(jnp.int32, (lanes, lanes), 1)
    prefix = jnp.where(li <= lj, 1.0, 0.0).astype(BF16)
    stacked = jnp.concatenate([chosen[:, blk * lanes:(blk + 1) * lanes] for blk in range(n_blk)], axis=0)
    incl_all = jnp.dot(stacked.astype(BF16), prefix, preferred_element_type=F32)
    carry = jnp.zeros((N_EXPERTS, 1), F32)
    for blk in range(n_blk):
        cols = slice(blk * lanes, (blk + 1) * lanes)
        incl = incl_all[blk * N_EXPERTS:(blk + 1) * N_EXPERTS, :]
        rank_ref[:, cols] = incl - chosen[:, cols] + carry
        carry = carry + incl[:, lanes - 1:lanes]
    count = carry
    padded = jnp.floor((count + float(MOE_TILE - 1)) * (1.0 / MOE_TILE)) * float(MOE_TILE)
    erow = lax.broadcasted_iota(jnp.int32, (N_EXPERTS, 1), 0)
    offset = jnp.zeros((N_EXPERTS, 1), F32)
    for e in range(N_EXPERTS - 1):
        offset = offset + jnp.where(erow > e, padded[e:e + 1, :], 0.0)
    position = rank_ref[...] + offset
    ei = lax.broadcasted_iota(jnp.int32, (N_EXPERTS, N_EXPERTS), 0)
    ej = lax.broadcasted_iota(jnp.int32, (N_EXPERTS, N_EXPERTS), 1)
    lower = jnp.where(ej <= ei, 1.0, 0.0).astype(BF16)
    seen = jnp.dot(lower, chosen.astype(BF16), preferred_element_type=F32)
    first = (chosen > 0.5) & (seen < 1.5)
    second = (chosen > 0.5) & (seen > 1.5)
    pick = lambda flag, x: jnp.sum(jnp.where(flag, x, 0.0), axis=0, keepdims=True)
    pos_ref[0:1, :] = pick(first, position).astype(jnp.int32)
    pos_ref[1:2, :] = pick(second, position).astype(jnp.int32)
    w_rows = jnp.concatenate([pick(first, gates), pick(second, gates), jnp.zeros((6, N_TOK), F32)], axis=0)
    ti = lax.broadcasted_iota(jnp.int32, (8, lanes), 0)
    tj = lax.broadcasted_iota(jnp.int32, (8, lanes), 1)
    eye = jnp.where(ti == tj, 1.0, 0.0).astype(BF16)
    tn = (((0,), (0,)), ((), ()))
    hi = w_rows.astype(BF16)
    r1 = w_rows - hi.astype(F32)
    mid = r1.astype(BF16)
    lo = (r1 - mid.astype(F32)).astype(BF16)
    w_cols = lax.dot_general(hi, eye, tn, preferred_element_type=F32)
    w_cols = w_cols + lax.dot_general(mid, eye, tn, preferred_element_type=F32)
    w_cols = w_cols + lax.dot_general(lo, eye, tn, preferred_element_type=F32)
    w_ref[...] = w_cols[:, :TOP_K]
    start = (lax.broadcasted_iota(jnp.int32, (N_EXPERTS, lanes), 1) * MOE_TILE).astype(F32)
    end = offset + padded
    tile_expert = jnp.sum(jnp.where(end <= start, 1.0, 0.0), axis=0, keepdims=True)
    inside = (offset <= start) & (start < end)
    real = jnp.clip(count - (start - offset), 0.0, float(MOE_TILE))
    tile_rows = jnp.sum(jnp.where(inside, real, 0.0), axis=0, keepdims=True)
    plan_ref[0:1, :] = jnp.minimum(tile_expert, float(N_EXPERTS - 1)).astype(jnp.int32)
    plan_ref[1:2, :] = tile_rows.astype(jnp.int32)


def _router(logits_t, router_b):
    whole = lambda shape: pl.BlockSpec(shape, lambda i: (0, 0))
    return pl.pallas_call(
        _router_kernel,
        grid=(1,),
        in_specs=[whole((N_EXPERTS, N_TOK)), whole((N_EXPERTS, 1))],
        out_specs=(whole((2, N_TOK)), whole((N_TOK, TOP_K)), whole((2, 128))),
        out_shape=(jax.ShapeDtypeStruct((2, N_TOK), jnp.int32),
                   jax.ShapeDtypeStruct((N_TOK, TOP_K), F32),
                   jax.ShapeDtypeStruct((2, 128), jnp.int32)),
        scratch_shapes=[pltpu.VMEM((N_EXPERTS, N_TOK), F32)],
        compiler_params=_params("arbitrary"),
        name="router",
    )(logits_t, router_b.reshape(N_EXPERTS, 1))


def _sc_mesh():
    return plsc.VectorSubcoreMesh(core_axis_name="c", subcore_axis_name="s")


def _sc_worker_base():
    return (lax.axis_index("s") * SC_CORES + lax.axis_index("c")) * SC_TOKENS_PER_WORKER


def _moe_dispatch(h, pos_a, pos_b):
    n_chunks = SC_TOKENS_PER_WORKER // SC_CHUNK
    idx = pltpu.VMEM((SC_CHUNK,), jnp.int32)

    @functools.partial(
        pl.kernel, mesh=_sc_mesh(),
        out_type=jax.ShapeDtypeStruct((MOE_ROWS, ROW_WORDS), jnp.int32),
        scratch_types=[idx, idx, idx, idx, pltpu.VMEM((2, SC_CHUNK, ROW_WORDS), jnp.int32),
                       pltpu.SemaphoreType.DMA((6,)), pltpu.SemaphoreType.DMA((4,))],
        name="moe_dispatch",
    )
    def run(h_hbm, pa_hbm, pb_hbm, xs_hbm, ia0, ib0, ia1, ib1, rows_v, sem_in, sem_out):
        base = _sc_worker_base()
        ia, ib = (ia0, ia1), (ib0, ib1)

        def start_loads(c):
            slot = c % 2
            tok = pl.ds(pl.multiple_of(base + c * SC_CHUNK, 8), SC_CHUNK)
            return (pltpu.async_copy(pa_hbm.at[tok], ia[slot], sem_in.at[3 * slot]),
                    pltpu.async_copy(pb_hbm.at[tok], ib[slot], sem_in.at[3 * slot + 1]),
                    pltpu.async_copy(h_hbm.at[tok], rows_v.at[slot], sem_in.at[3 * slot + 2]))

        loads = start_loads(0)
        scatters = [(), ()]
        for c in range(n_chunks):
            slot = c % 2
            for cp in loads:
                cp.wait()
            if c + 1 < n_chunks:
                for cp in scatters[1 - slot]:
                    cp.wait()
                scatters[1 - slot] = ()
                loads = start_loads(c + 1)
            scatters[slot] = (pltpu.async_copy(rows_v.at[slot], xs_hbm.at[ia[slot]], sem_out.at[2 * slot]),
                              pltpu.async_copy(rows_v.at[slot], xs_hbm.at[ib[slot]], sem_out.at[2 * slot + 1]))
        for pending in scatters:
            for cp in pending:
                cp.wait()

    return run(h, pos_a, pos_b)


def _moe_collect(ys, pos_a, pos_b, tok0=0, n_tok=N_TOK):
    per_worker = n_tok // SC_WORKERS
    chunk = SC_CHUNK if per_worker % SC_CHUNK == 0 else 32
    n_chunks = per_worker // chunk
    out = jax.ShapeDtypeStruct((n_tok, ROW_WORDS), jnp.int32)
    idx = pltpu.VMEM((per_worker,), jnp.int32)
    rows = pltpu.VMEM((2, chunk, ROW_WORDS), jnp.int32)

    @functools.partial(
        pl.kernel, mesh=_sc_mesh(), out_type=(out, out),
        scratch_types=[idx, idx, rows, rows, pltpu.SemaphoreType.DMA((4,)), pltpu.SemaphoreType.DMA((4,))],
        name="moe_collect",
    )
    def run(ys_hbm, pa_hbm, pb_hbm, ya_hbm, yb_hbm, ia_v, ib_v, ra_v, rb_v, sem_g, sem_w):
        base = (lax.axis_index("s") * SC_CORES + lax.axis_index("c")) * per_worker
        mine = pl.ds(pl.multiple_of(tok0 + base, 8), per_worker)
        pltpu.sync_copy(pa_hbm.at[mine], ia_v)
        pltpu.sync_copy(pb_hbm.at[mine], ib_v)
        writes = [(), ()]
        for c in range(n_chunks):
            slot = c % 2
            for cp in writes[slot]:
                cp.wait()
            part = pl.ds(c * chunk, chunk)
            tok = pl.ds(pl.multiple_of(base + c * chunk, 8), chunk)
            ga = pltpu.async_copy(ys_hbm.at[ia_v.at[part]], ra_v.at[slot], sem_g.at[slot])
            gb = pltpu.async_copy(ys_hbm.at[ib_v.at[part]], rb_v.at[slot], sem_g.at[2 + slot])
            ga.wait()
            wa = pltpu.async_copy(ra_v.at[slot], ya_hbm.at[tok], sem_w.at[slot])
            gb.wait()
            wb = pltpu.async_copy(rb_v.at[slot], yb_hbm.at[tok], sem_w.at[2 + slot])
            writes[slot] = (wa, wb)
        for pending in writes:
            for cp in pending:
                cp.wait()

    return run(ys, pos_a, pos_b)


def _experts_kernel(plan_ref, xs_ref, wg_hbm, wu_hbm, wd_hbm, y_ref,
                    sg_ref, su_ref, sd_ref, wgb_ref, wub_ref, wdb_ref, hid_ref, sems, seg_ref, *, layer):
    n_tiles = pl.num_programs(0) * EXPERT_TILES_PER_STEP

    def weight_copies(e, slot):
        return (pltpu.make_async_copy(wg_hbm.at[layer, e], sg_ref.at[slot], sems.at[slot, 0]),
                pltpu.make_async_copy(wu_hbm.at[layer, e], su_ref.at[slot], sems.at[slot, 1]),
                pltpu.make_async_copy(wd_hbm.at[layer, e], sd_ref.at[slot], sems.at[slot, 2]))

    def tile(t, row0):
        expert = plan_ref[t]
        n_real = plan_ref[PLAN_LANES + t]
        fresh = jnp.logical_or(t == 0, expert != plan_ref[jnp.maximum(t - 1, 0)])

        @pl.when(t == 0)
        def _():
            seg_ref[0] = 0

            @pl.when(n_real > 0)
            def _():
                for cp in weight_copies(expert, 0):
                    cp.start()

        @pl.when(jnp.logical_and(n_real > 0, fresh))
        def _():
            slot = seg_ref[0] % 2
            for cp in weight_copies(expert, slot):
                cp.wait()
            wgb_ref[...] = sg_ref[slot].astype(BF16)
            wub_ref[...] = su_ref[slot].astype(BF16)
            wdb_ref[...] = sd_ref[slot].astype(BF16)
            nxt = lax.while_loop(
                lambda u: jnp.logical_and(u < n_tiles, plan_ref[jnp.minimum(u, n_tiles - 1)] == expert),
                lambda u: u + 1, t + 1)
            nxt_c = jnp.minimum(nxt, n_tiles - 1)

            @pl.when(jnp.logical_and(nxt < n_tiles, plan_ref[PLAN_LANES + nxt_c] > 0))
            def _():
                for cp in weight_copies(plan_ref[nxt_c], 1 - slot):
                    cp.start()

            seg_ref[0] = seg_ref[0] + 1

        @pl.when(n_real > 0)
        def _():
            n = EXPERT_SUB_ROWS
            n_sub = MOE_TILE // n
            row = lax.broadcasted_iota(jnp.int32, (n, xs_ref.shape[1]), 0)

            def up(r):
                rows = slice(row0 + r * n, row0 + (r + 1) * n)
                words = jnp.where(row < n_real - r * n, xs_ref[rows, :], 0)
                x = _unpack_rows(words).astype(BF16)
                a = jnp.dot(x, wgb_ref[...], preferred_element_type=F32)
                b = jnp.dot(x, wub_ref[...], preferred_element_type=F32)
                hid_ref[r] = ((a * jax.nn.sigmoid(a)) * b).astype(BF16)

            def down(r):
                rows = slice(row0 + r * n, row0 + (r + 1) * n)
                y_ref[rows, :] = _pack_rows(jnp.dot(hid_ref[r], wdb_ref[...], preferred_element_type=F32))

            up(0)
            for r in range(1, n_sub):
                up(r)
                down(r - 1)
            down(n_sub - 1)

    for q in range(EXPERT_TILES_PER_STEP):
        tile(pl.program_id(0) * EXPERT_TILES_PER_STEP + q, q * MOE_TILE)


def _experts(plan, xs, w_gate, w_up, w_down, layer):
    hbm = pl.BlockSpec(memory_space=pl.ANY)
    step_rows = MOE_TILE * EXPERT_TILES_PER_STEP
    return pl.pallas_call(
        functools.partial(_experts_kernel, layer=layer),
        grid_spec=pltpu.PrefetchScalarGridSpec(
            num_scalar_prefetch=1,
            grid=(MOE_ROWS // step_rows,),
            in_specs=[pl.BlockSpec((step_rows, ROW_WORDS), lambda j, plan: (j, 0)), hbm, hbm, hbm],
            out_specs=pl.BlockSpec((step_rows, ROW_WORDS), lambda j, plan: (j, 0)),
            scratch_shapes=[pltpu.VMEM((2, D_MODEL, D_EXPERT), F32), pltpu.VMEM((2, D_MODEL, D_EXPERT), F32),
                            pltpu.VMEM((2, D_EXPERT, D_MODEL), F32),
                            pltpu.VMEM((D_MODEL, D_EXPERT), BF16), pltpu.VMEM((D_MODEL, D_EXPERT), BF16),
                            pltpu.VMEM((D_EXPERT, D_MODEL), BF16),
                            pltpu.VMEM((MOE_TILE // EXPERT_SUB_ROWS, EXPERT_SUB_ROWS, D_EXPERT), BF16),
                            pltpu.SemaphoreType.DMA((2, 3)), pltpu.SMEM((1,), jnp.int32)],
        ),
        out_shape=jax.ShapeDtypeStruct((MOE_ROWS, ROW_WORDS), jnp.int32),
        compiler_params=_params("arbitrary"),
        name="experts",
    )(plan, xs, w_gate, w_up, w_down)


def _combine_kernel(x_ref, ya_ref, yb_ref, wt_ref, mod_ref, o_ref):
    o_ref[...] = _moe_mix(x_ref, ya_ref, yb_ref, wt_ref, mod_ref)


def _combine(x, moe_out, mod_l, tok0, n_tok, block_rows=1024):
    ya, yb, w_tok = moe_out
    b0 = tok0 // block_rows
    rows = lambda width: pl.BlockSpec((block_rows, width), lambda i: (b0 + i, 0))
    local = pl.BlockSpec((block_rows, ROW_WORDS), lambda i: (i, 0))
    return pl.pallas_call(
        _combine_kernel,
        grid=(n_tok // block_rows,),
        in_specs=[rows(D_MODEL), local, local, rows(TOP_K),
                  pl.BlockSpec((None, 6, D_MODEL), lambda i: (_cond_of_token_block(b0 + i, block_rows), 0, 0))],
        out_specs=pl.BlockSpec((block_rows, D_MODEL), lambda i: (i, 0)),
        out_shape=jax.ShapeDtypeStruct((n_tok, D_MODEL), F32),
        compiler_params=_params("arbitrary"),
        name="combine",
    )(x, ya, yb, w_tok, mod_l)


def _moe(h, logits_t, router_b, w_gate, w_up, w_down, layer, ranges=((0, N_TOK),)):
    pos, w, plan = _router(logits_t, router_b)
    xs = _moe_dispatch(h, pos[0], pos[1])
    ys = _experts(plan.reshape(-1), xs, w_gate, w_up, w_down, layer)
    return [(*_moe_collect(ys, pos[0], pos[1], tok0, n_tok), w) for tok0, n_tok in ranges]


def _dft_tables(L):
    k = np.arange(L)[:, None]
    m = np.arange(L)[None, :]
    r = (k * m) % (2 * L)
    ang = np.pi * r.astype(np.float64) / L
    fc = np.cos(ang)
    fs = np.sin(ang)
    fs[0, :] = np.where(np.arange(L) % 2 == 0, 1.0, -1.0)
    wk = np.full((L, 1), 1.0 / L)
    wk[0, 0] = 0.5 / L
    gc = (fc * wk).T
    gs = (fs * wk).T
    return [jnp.asarray(t.astype(np.float32)).astype(BF16) for t in (fc, fs, gc, gs)]


def _filter_consts(L):
    t = np.linspace(0.0, 1.0, L, dtype=np.float32)[:, None]
    w = (np.float32(2.0 * np.pi) * np.arange(L, dtype=np.float32)[:, None] / np.float32(L)).astype(np.float32)
    fb = np.linspace(1e-4, HY_BANDS - 1, HY_BANDS, dtype=np.float32)[None, :]
    emb = np.concatenate([t, np.cos(fb * w), -np.sin(fb * w)], axis=-1).astype(np.float32)
    lo = math.log(HY_DECAY_TARGET) / HY_SLOW_PCT
    hi = math.log(HY_DECAY_TARGET) / HY_FAST_PCT
    deltas = np.abs(np.linspace(lo, hi, D_MODEL, dtype=np.float32))
    decay = np.exp(-t * deltas).astype(np.float32)
    return jnp.asarray(emb), jnp.asarray(decay)


def _filter_kernel(emb_ref, w1_ref, b1_ref, w2_ref, b2_ref, fr_ref, w3f_ref, w3b_ref, dec_ref,
                   fc_ref, fs_ref, kr_ref, q_ref, krn_ref, hd_ref):
    @pl.when(pl.program_id(0) == 0)
    def _():
        fr = fr_ref[...]
        h1 = jnp.sin(fr * (jnp.dot(emb_ref[...], w1_ref[...], precision=HIGHEST,
                                   preferred_element_type=F32) + b1_ref[...]))
        hd_ref[...] = jnp.sin(fr * (jnp.dot(h1, w2_ref[...], precision=HIGHEST,
                                            preferred_element_type=F32) + b2_ref[...]))

    hd = hd_ref[...]
    dec = dec_ref[...]
    f = jnp.dot(hd, w3f_ref[...], precision=HIGHEST, preferred_element_type=F32) * dec
    g = jnp.dot(hd, w3b_ref[...], precision=HIGHEST, preferred_element_type=F32) * dec
    row = lax.broadcasted_iota(jnp.int32, f.shape, 0)
    g = jnp.where(row == 0, 0.0, g)
    s = f + g
    d = f - g
    kr = jnp.dot(fc_ref[...], s.astype(BF16), preferred_element_type=F32)
    qq = jnp.dot(fs_ref[...], d.astype(BF16), preferred_element_type=F32)
    alt = jnp.where(row % 2 == 0, 1.0, -1.0)
    nyq = jnp.sum(alt * s, axis=0, keepdims=True)
    kr_ref[...] = kr
    q_ref[...] = jnp.where(row == 0, 0.0, qq)
    krn_ref[...] = jnp.where(row == 0, nyq, kr)


def _hyena_filter_spectrum(L, w1, b1, w2, b2, w3, freq, fc, fs, cblk=256):
    emb, decay = _filter_consts(L)
    ncb = D_MODEL // cblk
    n_emb = 128
    emb = jnp.pad(emb, ((0, 0), (0, n_emb - emb.shape[1])))
    w1 = jnp.pad(w1, ((0, n_emb - w1.shape[0]), (0, 0)))
    full = lambda shape: pl.BlockSpec(shape, lambda j: tuple(0 for _ in shape))
    out_sds = jax.ShapeDtypeStruct((L, D_MODEL), F32)
    out_spec = pl.BlockSpec((L, cblk), lambda j: (0, j))
    return pl.pallas_call(
        _filter_kernel,
        grid=(ncb,),
        in_specs=[
            full((L, n_emb)), full((n_emb, HY_FFN)), full((1, HY_FFN)), full((HY_FFN, HY_FFN)),
            full((1, HY_FFN)), full((1, HY_FFN)),
            pl.BlockSpec((HY_FFN, cblk), lambda j: (0, j)),
            pl.BlockSpec((HY_FFN, cblk), lambda j: (0, ncb + j)),
            pl.BlockSpec((L, cblk), lambda j: (0, j)),
            full((L, L)), full((L, L)),
        ],
        out_specs=(out_spec, out_spec, out_spec),
        out_shape=(out_sds, out_sds, out_sds),
        scratch_shapes=[pltpu.VMEM((L, HY_FFN), F32)],
        compiler_params=_params("arbitrary"),
        name=f"hyena_filter_{L}",
    )(emb, w1, b1.reshape(1, HY_FFN), w2, b2.reshape(1, HY_FFN), freq.reshape(1, HY_FFN), w3, w3, decay, fc, fs)


def _hyena_conv_kernel(x0_ref, x1_ref, v_ref, cw0_ref, cw1_ref, cwv_ref, cb0_ref, cb1_ref, cbv_ref,
                       kr_ref, q_ref, krn_ref, ds_ref, fc_ref, fs_ref, gc_ref, gs_ref, o_ref,
                       zz_ref, gate_ref, skip_ref, yr_ref, yw_ref):
    L = fc_ref.shape[0]
    unit_w = zz_ref.shape[2]
    units = [(slice(s * L, (s + 1) * L), slice(c * unit_w, (c + 1) * unit_w))
             for s in range(x0_ref.shape[0] // L) for c in range(x0_ref.shape[1] // unit_w)]
    row = lax.broadcasted_iota(jnp.int32, (L, unit_w), 0)

    def gating(i):
        rows, cols = units[i]

        def short_conv(u_ref, w_ref, b_ref):
            u = u_ref[rows, cols].astype(F32)
            w = w_ref[:, cols]
            prev = jnp.where(row == 0, 0.0, pltpu.roll(u, 1, axis=0))
            nxt = jnp.where(row == L - 1, 0.0, pltpu.roll(u, L - 1, axis=0))
            return prev * w[0:1, :] + u * w[1:2, :] + nxt * w[2:3, :] + b_ref[:, cols]

        x0 = short_conv(x0_ref, cw0_ref, cb0_ref)
        zz = short_conv(v_ref, cwv_ref, cbv_ref) * short_conv(x1_ref, cw1_ref, cb1_ref)
        zz_ref[i] = zz.astype(BF16)
        gate_ref[i] = x0
        skip_ref[i] = x0 * zz * ds_ref[:, cols]

    def spectrum(i):
        cols = units[i][1]
        ur = jnp.dot(fc_ref[...], zz_ref[i], preferred_element_type=F32)
        p = jnp.dot(fs_ref[...], zz_ref[i], preferred_element_type=F32)
        qq = q_ref[:, cols]
        yr_ref[i] = (ur * kr_ref[:, cols] - p * qq).astype(BF16)
        yw_ref[i] = (ur * qq + p * krn_ref[:, cols]).astype(BF16)

    def synthesis(i):
        rows, cols = units[i]
        y = jnp.dot(gc_ref[...], yr_ref[i], preferred_element_type=F32)
        y = y + jnp.dot(gs_ref[...], yw_ref[i], preferred_element_type=F32)
        o_ref[rows, cols] = (gate_ref[i] * y + skip_ref[i]).astype(o_ref.dtype)

    for t in range(len(units) + 2):
        if t < len(units):
            gating(t)
        if 0 <= t - 1 < len(units):
            spectrum(t - 1)
        if 0 <= t - 2 < len(units):
            synthesis(t - 2)


def _hyena_conv(u, conv_w, conv_b, dskip, spectrum, tables, *, latent):
    L = LATENT_LEN if latent else PROMPT_LEN
    n_seq = N_LATENT_SEQ if latent else N_PROMPT_SEQ
    cblk = 512
    unit_w = 256 if latent else 512
    ncb = D_MODEL // cblk
    seqs = 1 if latent else 8
    unit = (seqs * cblk // unit_w, L, unit_w)
    row0 = (N_PROMPT_TOK // L) if latent else 0
    kr, qq, krn = spectrum
    fc, fs, gc, gs = tables

    def part(p, rows):
        if rows != L:
            return pl.BlockSpec((rows, cblk), lambda j, s: (0, p * ncb + j))
        return pl.BlockSpec((seqs * L, cblk), lambda j, s: (row0 // seqs + s, p * ncb + j))

    def const_cols(rows):
        return pl.BlockSpec((rows, cblk), lambda j, s: (0, j))

    mat = pl.BlockSpec((L, L), lambda j, s: (0, 0))
    conv_b2 = conv_b.reshape(1, 3 * D_MODEL)
    in_specs = [part(0, L), part(1, L), part(2, L),
                part(0, 3), part(1, 3), part(2, 3),
                part(0, 1), part(1, 1), part(2, 1),
                const_cols(L), const_cols(L), const_cols(L), const_cols(1),
                mat, mat, mat, mat]
    args = [u, u, u, conv_w, conv_w, conv_w, conv_b2, conv_b2, conv_b2,
            kr, qq, krn, dskip.reshape(1, D_MODEL), fc, fs, gc, gs]
    return pl.pallas_call(
        _hyena_conv_kernel,
        grid=(ncb, n_seq // seqs),
        in_specs=in_specs,
        out_specs=pl.BlockSpec((seqs * L, cblk), lambda j, s: (s, j)),
        out_shape=jax.ShapeDtypeStruct((n_seq * L, D_MODEL), BF16),
        scratch_shapes=[pltpu.VMEM(unit, BF16), pltpu.VMEM(unit, F32), pltpu.VMEM(unit, F32),
                        pltpu.VMEM(unit, BF16), pltpu.VMEM(unit, BF16)],
        compiler_params=_params("arbitrary", "arbitrary"),
        name="hyena_conv_latent" if latent else "hyena_conv_prompt",
    )(*args)


def kernel(x_prompt, x_sample, cache_k, cache_v, state_hgrn, c, c_ctx, norm_g, mod_w, mod_b, ab_in_w, hgrn_lb, hgrn_onorm_g, attn_qnorm_g, attn_knorm_g, ab_out_w, hy_in_w, hy_in_b, hy_conv_w, hy_conv_b, hy_f_w1, hy_f_b1, hy_f_w2, hy_f_b2, hy_f_w3, hy_f_freq, hy_dskip, hy_out_w, router_w, router_b, moe_w_gate, moe_w_up, moe_w_down):
    xp = x_prompt.reshape(N_PROMPT_TOK, D_MODEL)
    xl = x_sample.reshape(N_LATENT_TOK, D_MODEL)
    cond = jnp.concatenate([c_ctx[None, :], c, jnp.zeros((N_COND - 1 - N_LATENT_SEQ, D_MODEL), F32)], axis=0)
    mod = _modulation(cond, mod_w, mod_b)
    router_wp = jnp.pad(router_w, ((0, 0), (0, ROUTER_LANES - N_EXPERTS)))

    z = _in_proj0(xp, xl, norm_g[0, 0], mod[0], ab_in_w[0])
    oa_p, new_state = _hgrn(z, hgrn_lb, hgrn_onorm_g[0], None, latent=False)
    oa_l = _hgrn(z, hgrn_lb, hgrn_onorm_g[0], state_hgrn, latent=True)
    ob_p, k_fm, v_fm = _attention_prompt(z, attn_qnorm_g[0], attn_knorm_g[0])
    fm_shape = (N_PROMPT_SEQ, 1, KV_HEADS, HEAD_DIM, PROMPT_LEN)
    new_k = jnp.swapaxes(k_fm.reshape(fm_shape), -1, -2)
    new_v = jnp.swapaxes(v_fm.reshape(fm_shape), -1, -2)
    ob_l = _attention_latent(z, attn_qnorm_g[0], attn_knorm_g[0], cache_k, cache_v)
    x, h, logits_t = _out_proj([(oa_p, oa_l), (ob_p, ob_l)], ab_out_w[0], (xp, xl), norm_g[0, 1], mod[0],
                               router_wp)
    (moe_out,) = _moe(h, logits_t, router_b, moe_w_gate, moe_w_up, moe_w_down, 0)

    x, u = _in_proj1(x, moe_out, mod[0], norm_g[1, 0], mod[1], hy_in_w[0], hy_in_b[0])
    pre = []
    for latent in (False, True):
        L = LATENT_LEN if latent else PROMPT_LEN
        tables = _dft_tables(L)
        spectrum = _hyena_filter_spectrum(L, hy_f_w1[0], hy_f_b1[0], hy_f_w2[0], hy_f_b2[0], hy_f_w3[0],
                                          hy_f_freq[0], tables[0], tables[1])
        pre.append(_hyena_conv(u, hy_conv_w[0], hy_conv_b[0], hy_dskip[0], spectrum, tables, latent=latent))
    x, h, logits_t = _out_proj([tuple(pre)], hy_out_w[0], (x,), norm_g[1, 1], mod[1], router_wp)
    trunks = ((0, N_PROMPT_TOK), (N_PROMPT_TOK, N_LATENT_TOK))
    out_p, out_l = _moe(h, logits_t, router_b, moe_w_gate, moe_w_up, moe_w_down, 1, ranges=trunks)

    y_prompt = _combine(x, out_p, mod[1], *trunks[0]).reshape(N_PROMPT_SEQ, PROMPT_LEN, D_MODEL)
    y_sample = _combine(x, out_l, mod[1], *trunks[1], block_rows=256).reshape(N_LATENT_SEQ, LATENT_LEN, D_MODEL)
    return (y_prompt, y_sample, new_k, new_v, new_state)
```

```python
import functools
import math

import numpy as np
import jax
import jax.numpy as jnp
from jax import lax
from jax.experimental import pallas as pl
from jax.experimental.pallas import tpu as pltpu
from jax.experimental.pallas import tpu_sc as plsc

F32 = jnp.float32
BF16 = jnp.bfloat16
HIGHEST = lax.Precision.HIGHEST

D_MODEL = 1024
N_PROMPT_SEQ = 32
PROMPT_LEN = 256
N_LATENT_SEQ = 2
LATENT_LEN = 1024
PAST_LEN = 512
GRID_W = 64
N_PROMPT_TOK = N_PROMPT_SEQ * PROMPT_LEN
N_LATENT_TOK = N_LATENT_SEQ * LATENT_LEN
N_TOK = N_PROMPT_TOK + N_LATENT_TOK
N_COND = 8
EPS = 1e-6

A_WIDTH = 512
A_HEADS = 4
A_DK = 128
CHUNK = 64
HGRN_BLOCK = 128
HGRN_HEADS_PER_STEP = 4
HEAD_DIM = 64
Q_HEADS = 8
KV_HEADS = 2
Q_PER_KV = Q_HEADS // KV_HEADS
Q_BLOCK = 256
ROPE_THETA = 10000.0
ROPE_PAIRS = HEAD_DIM // 4

HY_BANDS = 16
HY_FFN = 64
HY_DECAY_TARGET = 1e-2
HY_FAST_PCT = 0.3
HY_SLOW_PCT = 1.5

N_EXPERTS = 16
N_GROUPS = 4
EXPERTS_PER_GROUP = 4
TOP_K = 2
D_EXPERT = 512
ROUTER_LANES = 128
OUT_PROJ_SUB_ROWS = 256
EXPERT_SUB_ROWS = 256
EXPERT_TILES_PER_STEP = 2
IN_PROJ_SUB_ROWS = 256
MOE_TILE = 512
MOE_ROWS = N_TOK * TOP_K + N_EXPERTS * MOE_TILE
PLAN_LANES = 128

SC_CORES = 2
SC_WORKERS = 32
SC_TOKENS_PER_WORKER = N_TOK // SC_WORKERS
SC_CHUNK = 40
ROW_WORDS = D_MODEL // 2

VMEM_LIMIT = 56 * 1024 * 1024


def _params(*sem):
    return pltpu.CompilerParams(dimension_semantics=sem, vmem_limit_bytes=VMEM_LIMIT)


def _pack_rows(x):
    n = x.shape[1] // 2
    bits = pltpu.bitcast(x.astype(BF16).astype(F32), jnp.uint32)
    return pltpu.bitcast(bits[:, :n] | (bits[:, n:] >> 16), jnp.int32)


def _unpack_rows(p):
    bits = pltpu.bitcast(p, jnp.uint32)
    hi = pltpu.bitcast(bits & jnp.uint32(0xFFFF0000), F32)
    lo = pltpu.bitcast(bits << 16, F32)
    return jnp.concatenate([hi, lo], axis=1)


def _cond_of_token_block(i, block_rows):
    start = i * block_rows
    return jnp.where(start < N_PROMPT_TOK, 0, 1 + (start - N_PROMPT_TOK) // LATENT_LEN)


def _mod_kernel(cond_ref, w_ref, b_ref, o_ref):
    cnd = cond_ref[...]
    s = cnd * jax.nn.sigmoid(cnd)
    s_hi = s.astype(BF16)
    s_lo = (s - s_hi.astype(F32)).astype(BF16)
    w = w_ref[...]
    w_hi = w.astype(BF16)
    w_lo = (w - w_hi.astype(F32)).astype(BF16)
    acc = jnp.dot(s_hi, w_hi, preferred_element_type=F32)
    acc = acc + jnp.dot(s_lo, w_hi, preferred_element_type=F32)
    acc = acc + jnp.dot(s_hi, w_lo, preferred_element_type=F32)
    o_ref[...] = acc + b_ref[...]


def _modulation(cond, mod_w, mod_b):
    depth = mod_w.shape[0]
    n_mod = 6
    cols = 2 * D_MODEL
    n_step = n_mod * D_MODEL // cols
    out = pl.pallas_call(
        _mod_kernel,
        grid=(depth, n_step),
        in_specs=[
            pl.BlockSpec((N_COND, D_MODEL), lambda l, j: (0, 0)),
            pl.BlockSpec((None, D_MODEL, cols), lambda l, j: (l, 0, j)),
            pl.BlockSpec((None, 1, cols), lambda l, j: (l, 0, j)),
        ],
        out_specs=pl.BlockSpec((None, N_COND, cols), lambda l, j: (l, 0, j)),
        out_shape=jax.ShapeDtypeStruct((depth, N_COND, n_mod * D_MODEL), F32),
        compiler_params=_params("arbitrary", "arbitrary"),
        name="modulation",
    )(cond, mod_w, mod_b.reshape(depth, 1, n_mod * D_MODEL))
    return out.reshape(depth, N_COND, n_mod, D_MODEL)


def _modulated_norm(x, g, mod, shift_row, scale_row):
    ms = jnp.mean(x * x, axis=-1, keepdims=True)
    y = x * lax.rsqrt(ms + EPS) * g
    return y * (1.0 + mod[scale_row:scale_row + 1, :]) + mod[shift_row:shift_row + 1, :]


def _trunk_specs(block_rows, width):
    n_prompt_blocks = N_PROMPT_TOK // block_rows
    return (pl.BlockSpec((block_rows, width), lambda i: (jnp.minimum(i, n_prompt_blocks - 1), 0)),
            pl.BlockSpec((block_rows, width), lambda i: (jnp.maximum(i - n_prompt_blocks, 0), 0)))


def _select_trunk(p_ref, l_ref, rows=slice(None)):
    block_rows = p_ref.shape[0]
    return jnp.where(pl.program_id(0) < N_PROMPT_TOK // block_rows, p_ref[rows, :], l_ref[rows, :])


def _cast_once(w_ref, wb_ref):
    @pl.when(pl.program_id(0) == 0)
    def _():
        wb_ref[...] = w_ref[...].astype(BF16)


def _resident(shape):
    return pl.BlockSpec(shape, lambda i: tuple(0 for _ in shape), pipeline_mode=pl.Buffered(1))


def _mod_spec(block_rows):
    return pl.BlockSpec((None, 6, D_MODEL), lambda i: (_cond_of_token_block(i, block_rows), 0, 0))


def _in_proj0_kernel(xp_ref, xl_ref, g_ref, mod_ref, w_ref, o_ref, wb_ref, hb_ref):
    _cast_once(w_ref, wb_ref)
    n = IN_PROJ_SUB_ROWS
    n_sub = xp_ref.shape[0] // n

    def prepare(r):
        x = _select_trunk(xp_ref, xl_ref, slice(r * n, (r + 1) * n))
        hb_ref[r] = _modulated_norm(x, g_ref[...], mod_ref[...], 0, 1).astype(BF16)

    def project(r):
        u = jnp.dot(hb_ref[r], wb_ref[...], preferred_element_type=F32)
        o_ref[r * n:(r + 1) * n, :] = u.astype(o_ref.dtype)

    prepare(0)
    for r in range(1, n_sub):
        prepare(r)
        project(r - 1)
    project(n_sub - 1)


def _in_proj0(x_prompt, x_latent, g, mod_l, w, block_rows=512):
    n = w.shape[1]
    return pl.pallas_call(
        _in_proj0_kernel,
        grid=(N_TOK // block_rows,),
        in_specs=[*_trunk_specs(block_rows, D_MODEL), _resident((1, D_MODEL)), _mod_spec(block_rows),
                  _resident((D_MODEL, n))],
        out_specs=pl.BlockSpec((block_rows, n), lambda i: (i, 0)),
        out_shape=jax.ShapeDtypeStruct((N_TOK, n), BF16),
        scratch_shapes=[pltpu.VMEM((D_MODEL, n), BF16),
                        pltpu.VMEM((block_rows // IN_PROJ_SUB_ROWS, IN_PROJ_SUB_ROWS, D_MODEL), BF16)],
        compiler_params=_params("arbitrary"),
        name="in_proj0",
    )(x_prompt, x_latent, g.reshape(1, D_MODEL), mod_l, w)


def _moe_mix(x_ref, ya_ref, yb_ref, wt_ref, mod_ref, rows=slice(None)):
    wt = wt_ref[rows, :]
    mix = wt[:, 0:1] * _unpack_rows(ya_ref[rows, :]) + wt[:, 1:2] * _unpack_rows(yb_ref[rows, :])
    return x_ref[rows, :] + mod_ref[5:6, :] * mix


def _in_proj1_kernel(x_ref, ya_ref, yb_ref, wt_ref, modp_ref, g_ref, mod_ref, w_ref, b_ref, xo_ref, o_ref,
                     wb_ref, hb_ref):
    _cast_once(w_ref, wb_ref)
    n = IN_PROJ_SUB_ROWS
    n_sub = x_ref.shape[0] // n

    def prepare(r):
        rows = slice(r * n, (r + 1) * n)
        x = _moe_mix(x_ref, ya_ref, yb_ref, wt_ref, modp_ref, rows)
        xo_ref[rows, :] = x
        hb_ref[r] = _modulated_norm(x, g_ref[...], mod_ref[...], 0, 1).astype(BF16)

    def project(r):
        rows = slice(r * n, (r + 1) * n)
        u = jnp.dot(hb_ref[r], wb_ref[...], preferred_element_type=F32) + b_ref[...]
        o_ref[rows, :] = u.astype(o_ref.dtype)

    prepare(0)
    for r in range(1, n_sub):
        prepare(r)
        project(r - 1)
    project(n_sub - 1)


def _in_proj1(x, moe_out, mod_prev, g, mod_l, w, bias, block_rows=512):
    ya, yb, w_tok = moe_out
    n = w.shape[1]
    tok = pl.BlockSpec((block_rows, D_MODEL), lambda i: (i, 0))
    packed = pl.BlockSpec((block_rows, ROW_WORDS), lambda i: (i, 0))
    return pl.pallas_call(
        _in_proj1_kernel,
        grid=(N_TOK // block_rows,),
        in_specs=[tok, packed, packed, pl.BlockSpec((block_rows, TOP_K), lambda i: (i, 0)), _mod_spec(block_rows),
                  _resident((1, D_MODEL)), _mod_spec(block_rows), _resident((D_MODEL, n)), _resident((1, n))],
        out_specs=(tok, pl.BlockSpec((block_rows, n), lambda i: (i, 0))),
        out_shape=(jax.ShapeDtypeStruct((N_TOK, D_MODEL), F32), jax.ShapeDtypeStruct((N_TOK, n), BF16)),
        scratch_shapes=[pltpu.VMEM((D_MODEL, n), BF16),
                        pltpu.VMEM((block_rows // IN_PROJ_SUB_ROWS, IN_PROJ_SUB_ROWS, D_MODEL), BF16)],
        compiler_params=_params("arbitrary"),
        name="in_proj1",
    )(x, ya, yb, w_tok, mod_prev, g.reshape(1, D_MODEL), mod_l, w, bias.reshape(1, n))


def _hgrn_kernel(*refs, seq_len, with_state):
    if with_state:
        (q_ref, zf_ref, zb_ref, i_ref, ga_ref, lb_ref, og_ref, s0_ref, o_ref, of_ref, ob_ref) = refs
    else:
        (q_ref, zf_ref, zb_ref, i_ref, ga_ref, lb_ref, og_ref, o_ref, s_ref, of_ref, ob_ref) = refs
    n_blocks = seq_len // HGRN_BLOCK
    chunks_per_block = HGRN_BLOCK // CHUNK

    lbr = lb_ref[...]
    mx = jnp.maximum(lbr[0], lbr[1])
    e0 = jnp.exp(lbr[0] - mx)
    e1 = jnp.exp(lbr[1] - mx)
    lb = e0 / (e0 + e1)

    row = lax.broadcasted_iota(jnp.int32, (HGRN_BLOCK, HGRN_BLOCK), 0)
    col = lax.broadcasted_iota(jnp.int32, (HGRN_BLOCK, HGRN_BLOCK), 1)
    same_chunk = (row // CHUNK) == (col // CHUNK)
    nt = (((1,), (1,)), ((), ()))
    tn = (((0,), (0,)), ((), ()))

    def per_chunk_row(x, idx):
        return jnp.concatenate(
            [jnp.broadcast_to(x[n * CHUNK + idx:n * CHUNK + idx + 1, :], (CHUNK, x.shape[1]))
             for n in range(chunks_per_block)], axis=0)

    def in_chunk_cumsum(tri, x):
        hi = x.astype(BF16)
        lo = (x - hi.astype(F32)).astype(BF16)
        return jnp.dot(tri, hi, preferred_element_type=F32) + jnp.dot(tri, lo, preferred_element_type=F32)

    def prepare(blk, cols, z_ref, lbd, forward):
        rows = slice(blk * HGRN_BLOCK, (blk + 1) * HGRN_BLOCK)
        keep = (same_chunk & (col <= row)) if forward else (same_chunk & (col >= row))
        tri = jnp.where(keep, 1.0, 0.0).astype(BF16)
        mid = CHUNK // 2 if forward else CHUNK - 1 - CHUNK // 2
        last = CHUNK - 1 if forward else 0
        f = lbd + (1.0 - lbd) * jax.nn.sigmoid(z_ref[rows, cols].astype(F32))
        lf = jnp.log(f)
        k = 1.0 - f
        q = q_ref[rows, cols].astype(F32)
        b = in_chunk_cumsum(tri, lf)
        bm = per_chunk_row(b, mid)
        bl = per_chunk_row(b, last)
        return dict(
            rows=rows, cols=cols, keep=keep, forward=forward,
            vb=i_ref[rows, cols].astype(BF16),
            qe=(q * jnp.exp(b - bm)).astype(BF16), ke=(k * jnp.exp(bm - b)).astype(BF16),
            qb=(q * jnp.exp(b)).astype(BF16), ks=(k * jnp.exp(bl - b)).astype(BF16), decay=jnp.exp(bl))

    def within_chunks(u):
        att = lax.dot_general(u["qe"], u["ke"], nt, preferred_element_type=F32)
        att = jnp.where(u["keep"], att, 0.0)
        u["o_intra"] = jnp.dot(att.astype(BF16), u["vb"], preferred_element_type=F32)
        u["upd"] = [lax.dot_general(u["vb"][n * CHUNK:(n + 1) * CHUNK], u["ks"][n * CHUNK:(n + 1) * CHUNK], tn,
                                    preferred_element_type=F32) for n in range(chunks_per_block)]

    def across_chunks(u, st, out_ref):
        order = range(chunks_per_block) if u["forward"] else range(chunks_per_block - 1, -1, -1)
        o_inter = [None] * chunks_per_block
        for n in order:
            cr = slice(n * CHUNK, (n + 1) * CHUNK)
            o_inter[n] = lax.dot_general(u["qb"][cr], st.astype(BF16), nt, preferred_element_type=F32)
            st = st * u["decay"][n * CHUNK:n * CHUNK + 1, :] + u["upd"][n]
        out_ref[u["rows"], u["cols"]] = u["o_intra"] + jnp.concatenate(o_inter, axis=0)
        return st

    n_heads = q_ref.shape[1] // A_DK
    head_cols = [slice(hd * A_DK, (hd + 1) * A_DK) for hd in range(n_heads)]
    if with_state:
        states = {(hd, d): s0_ref[d, hd].T for hd in range(n_heads) for d in range(2)}
    else:
        states = {(hd, d): jnp.zeros((A_DK, A_DK), F32) for hd in range(n_heads) for d in range(2)}
    for step in range(n_blocks):
        units = {}
        for hd, cols in enumerate(head_cols):
            units[hd, 0] = prepare(step, cols, zf_ref, lb[0:1, cols], True)
            units[hd, 1] = prepare(n_blocks - 1 - step, cols, zb_ref, lb[1:2, cols], False)
        for u in units.values():
            within_chunks(u)
        for key, u in units.items():
            states[key] = across_chunks(u, states[key], of_ref if key[1] == 0 else ob_ref)
    for hd, cols in enumerate(head_cols):
        if not with_state:
            s_ref[0, hd] = states[hd, 0].T
            s_ref[1, hd] = states[hd, 1].T
        o = of_ref[:, cols] + ob_ref[:, cols]
        o = o * lax.rsqrt(jnp.mean(o * o, axis=-1, keepdims=True) + EPS) * og_ref[:, cols]
        ga = ga_ref[:, cols].astype(F32)
        o_ref[:, cols] = (o * (ga * jax.nn.sigmoid(ga))).astype(o_ref.dtype)


def _hgrn(z, hgrn_lb, onorm_g, state, *, latent):
    seq_len = LATENT_LEN if latent else PROMPT_LEN
    n_seq = N_LATENT_SEQ if latent else N_PROMPT_SEQ
    row0 = (N_PROMPT_TOK // seq_len) if latent else 0

    hw = HGRN_HEADS_PER_STEP * A_DK
    n_hg = A_HEADS // HGRN_HEADS_PER_STEP

    def zspec(part):
        return pl.BlockSpec((seq_len, hw), lambda s, h: (row0 + s, part * n_hg + h))

    in_specs = [zspec(0), zspec(1), zspec(2), zspec(3), zspec(4),
                pl.BlockSpec((2, 2, hw), lambda s, h: (0, 0, h)),
                pl.BlockSpec((1, hw), lambda s, h: (0, h))]
    args = [z, z, z, z, z, hgrn_lb, onorm_g.reshape(1, A_WIDTH)]
    state_spec = pl.BlockSpec((None, None, 2, HGRN_HEADS_PER_STEP, A_DK, A_DK), lambda s, h: (s, 0, 0, h, 0, 0))
    o_shape = jax.ShapeDtypeStruct((n_seq * seq_len, A_WIDTH), BF16)
    o_spec = pl.BlockSpec((seq_len, hw), lambda s, h: (s, h))
    if latent:
        in_specs.append(state_spec)
        args.append(state)
        out_shape, out_specs = o_shape, o_spec
    else:
        out_shape = (o_shape, jax.ShapeDtypeStruct((n_seq, 1, 2, A_HEADS, A_DK, A_DK), F32))
        out_specs = (o_spec, state_spec)
    return pl.pallas_call(
        functools.partial(_hgrn_kernel, seq_len=seq_len, with_state=latent),
        grid=(n_seq, n_hg),
        in_specs=in_specs,
        out_specs=out_specs,
        out_shape=out_shape,
        scratch_shapes=[pltpu.VMEM((seq_len, hw), F32), pltpu.VMEM((seq_len, hw), F32)],
        compiler_params=_params("arbitrary", "arbitrary"),
        name="hgrn_latent" if latent else "hgrn_prompt",
    )(*args)


def _rope_tables():
    pos = np.arange(LATENT_LEN)
    row, colp = pos // GRID_W, pos % GRID_W
    inv = ROPE_THETA ** (-np.arange(ROPE_PAIRS, dtype=np.float32) / ROPE_PAIRS)
    inv = inv.astype(np.float32)
    ang_r = (row.astype(np.float32)[:, None] * inv).astype(np.float32)
    ang_c = (colp.astype(np.float32)[:, None] * inv).astype(np.float32)
    cos = np.concatenate([np.cos(ang_r), np.cos(ang_r), np.cos(ang_c), np.cos(ang_c)], axis=1)
    sin = np.concatenate([-np.sin(ang_r), np.sin(ang_r), -np.sin(ang_c), np.sin(ang_c)], axis=1)
    return cos.astype(np.float32), sin.astype(np.float32)


def _head_mean_matrix(width):
    idx = np.arange(width) // HEAD_DIM
    return jnp.asarray((idx[:, None] == idx[None, :]).astype(np.float32) / HEAD_DIM).astype(BF16)


def _attn_kernel(*refs, latent):
    if latent:
        (q_ref, k_ref, v_ref, qg_ref, kg_ref, gq_ref, gk_ref, cosq_ref, sinq_ref, cosk_ref, sink_ref,
         ck_ref, cv_ref, o_ref, kd_ref, vd_ref, ckd_ref, cvd_ref) = refs
    else:
        (q_ref, k_ref, v_ref, qg_ref, kg_ref, gq_ref, gk_ref, o_ref, kout_ref, vout_ref) = refs
    pair_w = 2 * HEAD_DIM

    def head_norm(x, mean_ref, gain):
        sq = x * x
        hi = sq.astype(BF16)
        lo = (sq - hi.astype(F32)).astype(BF16)
        ms = jnp.dot(hi, mean_ref[...], preferred_element_type=F32)
        ms = ms + jnp.dot(lo, mean_ref[...], preferred_element_type=F32)
        return x * lax.rsqrt(ms + EPS) * gain

    def rope(x, cos, sin):
        n = x.shape[1]
        lane = lax.broadcasted_iota(jnp.int32, x.shape, 1)
        first_of_pair = (lane // ROPE_PAIRS) % 2 == 0
        swapped = jnp.where(first_of_pair, pltpu.roll(x, n - ROPE_PAIRS, axis=1), pltpu.roll(x, ROPE_PAIRS, axis=1))
        return x * cos + swapped * sin

    nt = (((1,), (1,)), ((), ()))

    def key_value_tiles(rows, seq_idx):
        k = head_norm(k_ref[rows, :].astype(F32), gk_ref, kg_ref[...])
        if latent:
            k = rope(k, cosk_ref[...], sink_ref[...])
        v = v_ref[rows, :].astype(F32)
        low_kv = lax.broadcasted_iota(jnp.int32, k.shape, 1) < HEAD_DIM
        k_swapped = pltpu.roll(k, HEAD_DIM, axis=1)
        v_swapped = pltpu.roll(v, HEAD_DIM, axis=1)
        if not latent:
            kout_ref[seq_idx] = k.T
            vout_ref[seq_idx] = v.T
        kd, vd = [], []
        for j in range(KV_HEADS):
            kd.append((jnp.where(low_kv, k, k_swapped) if j == 0 else jnp.where(low_kv, k_swapped, k)).astype(BF16))
            vj = (jnp.where(low_kv, v, v_swapped) if j == 0 else jnp.where(low_kv, v_swapped, v)).astype(BF16)
            vd.append(jnp.concatenate([vj, jnp.ones_like(vj)], axis=1))
        return kd, vd

    def query_units(rows, kd, vd):
        q = head_norm(q_ref[rows, :].astype(F32), gq_ref, qg_ref[...])
        if latent:
            q = rope(q, cosq_ref[...], sinq_ref[...])
        q = q * (HEAD_DIM ** -0.5)
        n_q = q.shape[0]
        low_q = lax.broadcasted_iota(jnp.int32, (n_q, pair_w), 1) < HEAD_DIM
        units = []
        for j in range(KV_HEADS):
            tiles = range(j * Q_PER_KV // 2, (j + 1) * Q_PER_KV // 2)
            parts = []
            for t in tiles:
                qt = q[:, t * pair_w:(t + 1) * pair_w]
                parts += [jnp.where(low_q, qt, 0.0), jnp.where(low_q, 0.0, qt)]
            units.append(dict(j=j, rows=rows, tiles=tiles, n_q=n_q, low_q=low_q, kd=kd[j], vd=vd[j],
                              qs=jnp.concatenate(parts, axis=0).astype(BF16)))
        return units

    def scores(u):
        u["s_new"] = lax.dot_general(u["qs"], u["kd"], nt, preferred_element_type=F32)
        if latent:
            u["s_old"] = lax.dot_general(u["qs"], ckd_ref[u["j"]], nt, preferred_element_type=F32)

    def softmax(u):
        m = jnp.max(u["s_new"], axis=-1, keepdims=True)
        if latent:
            m = jnp.maximum(m, jnp.max(u["s_old"], axis=-1, keepdims=True))
        u["p_new"] = jnp.exp(u.pop("s_new") - m).astype(BF16)
        if latent:
            u["p_old"] = jnp.exp(u.pop("s_old") - m).astype(BF16)

    def weighted_values(u):
        acc = jnp.dot(u["p_new"], u["vd"], preferred_element_type=F32)
        if latent:
            acc = acc + jnp.dot(u["p_old"], cvd_ref[u["j"]], preferred_element_type=F32)
        out = acc[:, :pair_w] / acc[:, pair_w:]
        n_q = u["n_q"]
        for i, t in enumerate(u["tiles"]):
            lo_head = out[(2 * i) * n_q:(2 * i + 1) * n_q, :]
            hi_head = out[(2 * i + 1) * n_q:(2 * i + 2) * n_q, :]
            o_ref[u["rows"], t * pair_w:(t + 1) * pair_w] = jnp.where(u["low_q"], lo_head, hi_head).astype(o_ref.dtype)

    if latent:
        @pl.when(pl.program_id(1) == 0)
        def _():
            kd, vd = key_value_tiles(slice(None), None)
            for j in range(KV_HEADS):
                kd_ref[j] = kd[j]
                vd_ref[j] = vd[j]
                ckd_ref[j] = jnp.concatenate([ck_ref[j], ck_ref[j]], axis=1).astype(BF16)
                cvd = jnp.concatenate([cv_ref[j], cv_ref[j]], axis=1).astype(BF16)
                cvd_ref[j] = jnp.concatenate([cvd, jnp.ones_like(cvd)], axis=1)

        units = query_units(slice(None), [kd_ref[j] for j in range(KV_HEADS)], [vd_ref[j] for j in range(KV_HEADS)])
    else:
        seq = PROMPT_LEN
        units = []
        for s in range(q_ref.shape[0] // seq):
            rows = slice(s * seq, (s + 1) * seq)
            units += query_units(rows, *key_value_tiles(rows, s))
    for phase in (scores, softmax, weighted_values):
        for u in units:
            phase(u)


def _attn_common_args(qn_g, kn_g):
    q_w, kv_w = Q_HEADS * HEAD_DIM, KV_HEADS * HEAD_DIM
    return (jnp.tile(qn_g, Q_HEADS).reshape(1, q_w), jnp.tile(kn_g, KV_HEADS).reshape(1, kv_w),
            _head_mean_matrix(q_w), _head_mean_matrix(kv_w))


def _attention_prompt(z, qn_g, kn_g):
    seqs = 8
    L = seqs * PROMPT_LEN
    cache_shape = jax.ShapeDtypeStruct((N_PROMPT_SEQ, KV_HEADS * HEAD_DIM, PROMPT_LEN), F32)
    cache_spec = pl.BlockSpec((seqs, KV_HEADS * HEAD_DIM, PROMPT_LEN), lambda s: (s, 0, 0))
    q_w, kv_w = Q_HEADS * HEAD_DIM, KV_HEADS * HEAD_DIM
    q_col = (5 * A_WIDTH) // q_w
    k_col = (5 * A_WIDTH + q_w) // kv_w
    const = lambda r, c: pl.BlockSpec((r, c), lambda s: (0, 0))
    return pl.pallas_call(
        functools.partial(_attn_kernel, latent=False),
        grid=(N_PROMPT_TOK // L,),
        in_specs=[
            pl.BlockSpec((L, q_w), lambda s: (s, q_col)),
            pl.BlockSpec((L, kv_w), lambda s: (s, k_col)),
            pl.BlockSpec((L, kv_w), lambda s: (s, k_col + 1)),
            const(1, q_w), const(1, kv_w), const(q_w, q_w), const(kv_w, kv_w),
        ],
        out_specs=(pl.BlockSpec((L, q_w), lambda s: (s, 0)), cache_spec, cache_spec),
        out_shape=(jax.ShapeDtypeStruct((N_PROMPT_TOK, q_w), BF16), cache_shape, cache_shape),
        compiler_params=_params("arbitrary"),
        name="attn_prompt",
    )(z, z, z, *_attn_common_args(qn_g, kn_g))


def _attention_latent(z, qn_g, kn_g, cache_k, cache_v):
    L = LATENT_LEN
    nqb = L // Q_BLOCK
    q_w, kv_w = Q_HEADS * HEAD_DIM, KV_HEADS * HEAD_DIM
    q_col = (5 * A_WIDTH) // q_w
    k_col = (5 * A_WIDTH + q_w) // kv_w
    qrow0 = N_PROMPT_TOK // Q_BLOCK
    krow0 = N_PROMPT_TOK // L
    cos, sin = _rope_tables()
    cos_q, sin_q = jnp.asarray(np.tile(cos, (1, Q_HEADS))), jnp.asarray(np.tile(sin, (1, Q_HEADS)))
    cos_k, sin_k = jnp.asarray(np.tile(cos, (1, KV_HEADS))), jnp.asarray(np.tile(sin, (1, KV_HEADS)))
    const = lambda r, c: pl.BlockSpec((r, c), lambda s, b: (0, 0))
    cache_spec = pl.BlockSpec((None, None, KV_HEADS, PAST_LEN, HEAD_DIM), lambda s, b: (s, 0, 0, 0, 0))
    return pl.pallas_call(
        functools.partial(_attn_kernel, latent=True),
        grid=(N_LATENT_SEQ, nqb),
        in_specs=[
            pl.BlockSpec((Q_BLOCK, q_w), lambda s, b: (qrow0 + s * nqb + b, q_col)),
            pl.BlockSpec((L, kv_w), lambda s, b: (krow0 + s, k_col)),
            pl.BlockSpec((L, kv_w), lambda s, b: (krow0 + s, k_col + 1)),
            const(1, q_w), const(1, kv_w), const(q_w, q_w), const(kv_w, kv_w),
            pl.BlockSpec((Q_BLOCK, q_w), lambda s, b: (b, 0)),
            pl.BlockSpec((Q_BLOCK, q_w), lambda s, b: (b, 0)),
            const(L, kv_w), const(L, kv_w),
            cache_spec, cache_spec,
        ],
        out_specs=pl.BlockSpec((Q_BLOCK, q_w), lambda s, b: (s * nqb + b, 0)),
        out_shape=jax.ShapeDtypeStruct((N_LATENT_TOK, q_w), BF16),
        scratch_shapes=[pltpu.VMEM((KV_HEADS, L, kv_w), BF16), pltpu.VMEM((KV_HEADS, L, 2 * kv_w), BF16),
                        pltpu.VMEM((KV_HEADS, PAST_LEN, kv_w), BF16), pltpu.VMEM((KV_HEADS, PAST_LEN, 2 * kv_w), BF16)],
        compiler_params=_params("arbitrary", "arbitrary"),
        name="attn_latent",
    )(z, z, z, *_attn_common_args(qn_g, kn_g), cos_q, sin_q, cos_k, sin_k, cache_k, cache_v)


def _out_proj_kernel(*refs, n_act, n_x):
    a_refs = refs[:2 * n_act]
    x_refs = refs[2 * n_act:2 * n_act + n_x]
    g_ref, mod_ref, rw_ref, w_ref, xo_ref, h_ref, lg_ref, wb_ref, rws_ref, acc_ref = refs[2 * n_act + n_x:]
    _cast_once(w_ref, wb_ref)

    @pl.when(pl.program_id(0) == 0)
    def _():
        rw = rw_ref[...]
        hi = rw.astype(BF16).astype(F32)
        lo = (rw - hi).astype(BF16).astype(F32)
        rws_ref[...] = (hi + pltpu.roll(lo, N_EXPERTS, axis=1)).astype(BF16)

    mod = mod_ref[...]
    n = OUT_PROJ_SUB_ROWS

    n_sub = xo_ref.shape[0] // n

    def sub_rows(r):
        if isinstance(r, int):
            return slice(r * n, (r + 1) * n)
        return pl.ds(pl.multiple_of(r * n, n), n)

    def project(r):
        rows = sub_rows(r)
        acc = None
        k0 = 0
        for ap_ref, al_ref in zip(a_refs[0::2], a_refs[1::2]):
            k1 = k0 + ap_ref.shape[1]
            part = jnp.dot(_select_trunk(ap_ref, al_ref, rows), wb_ref[k0:k1, :], preferred_element_type=F32)
            acc = part if acc is None else acc + part
            k0 = k1
        acc_ref[r % 2] = acc

    def finish(r):
        rows = sub_rows(r)
        x_in = x_refs[0][rows, :] if n_x == 1 else _select_trunk(*x_refs, rows)
        x = x_in + mod[2:3, :] * acc_ref[r % 2]
        xo_ref[rows, :] = x
        h = _modulated_norm(x, g_ref[...], mod, 3, 4)
        h_ref[rows, :] = _pack_rows(h)
        h_hi = h.astype(BF16)
        h_lo = (h - h_hi.astype(F32)).astype(BF16)
        both = jnp.dot(jnp.concatenate([h_hi, h_lo], axis=0), rws_ref[...], preferred_element_type=F32)
        from_hi, from_lo = both[:n], both[n:]
        lg = from_hi + pltpu.roll(from_hi, ROUTER_LANES - N_EXPERTS, axis=1) + from_lo
        lg_ref[:, rows] = lg.T[:N_EXPERTS, :]

    project(0)
    for r in range(n_sub - 1):
        project(r + 1)
        finish(r)
    finish(n_sub - 1)


def _out_proj(acts, w, xs, g, mod_l, router_wp, block_rows=1024):
    tok = lambda width: pl.BlockSpec((block_rows, width), lambda i: (i, 0))
    in_specs = [spec for ap, _ in acts for spec in _trunk_specs(block_rows, ap.shape[1])]
    in_specs += [tok(D_MODEL)] if len(xs) == 1 else list(_trunk_specs(block_rows, D_MODEL))
    in_specs += [_resident((1, D_MODEL)), _mod_spec(block_rows), _resident((D_MODEL, ROUTER_LANES)),
                 _resident(w.shape)]
    return pl.pallas_call(
        functools.partial(_out_proj_kernel, n_act=len(acts), n_x=len(xs)),
        grid=(N_TOK // block_rows,),
        in_specs=in_specs,
        out_specs=(tok(D_MODEL), tok(ROW_WORDS), pl.BlockSpec((N_EXPERTS, block_rows), lambda i: (0, i))),
        out_shape=(jax.ShapeDtypeStruct((N_TOK, D_MODEL), F32),
                   jax.ShapeDtypeStruct((N_TOK, ROW_WORDS), jnp.int32),
                   jax.ShapeDtypeStruct((N_EXPERTS, N_TOK), F32)),
        scratch_shapes=[pltpu.VMEM(w.shape, BF16), pltpu.VMEM((D_MODEL, ROUTER_LANES), BF16),
                        pltpu.VMEM((2, OUT_PROJ_SUB_ROWS, D_MODEL), F32)],
        compiler_params=_params("arbitrary"),
        name="out_proj",
    )(*[a for pair in acts for a in pair], *xs, g.reshape(1, D_MODEL), mod_l, router_wp, w)


def _router_kernel(lg_ref, rb_ref, pos_ref, w_ref, plan_ref, rank_ref):
    lg = lg_ref[...]
    ex = jnp.exp(lg - jnp.max(lg, axis=0, keepdims=True))
    scores = ex / jnp.sum(ex, axis=0, keepdims=True)
    biased = scores + rb_ref[...]
    expert = lax.broadcasted_iota(jnp.int32, biased.shape, 0)
    in_pos = expert % EXPERTS_PER_GROUP
    rank = jnp.zeros_like(biased)
    for d in range(1, EXPERTS_PER_GROUP):
        wraps = in_pos + d >= EXPERTS_PER_GROUP
        partner = jnp.where(wraps, pltpu.roll(biased, EXPERTS_PER_GROUP - d, axis=0),
                            pltpu.roll(biased, N_EXPERTS - d, axis=0))
        rank = rank + jnp.where(wraps, jnp.where(partner >= biased, 1.0, 0.0), jnp.where(partner > biased, 1.0, 0.0))
    selected = rank < 1.5
    contrib = jnp.where(selected, biased, 0.0)
    group_score = []
    for gi in range(N_GROUPS):
        r = [contrib[gi * EXPERTS_PER_GROUP + i:gi * EXPERTS_PER_GROUP + i + 1, :] for i in range(EXPERTS_PER_GROUP)]
        group_score.append(((r[0] + r[1]) + r[2]) + r[3])
    best = group_score[0]
    best_group = jnp.zeros_like(best)
    for gi in range(1, N_GROUPS):
        better = group_score[gi] > best
        best_group = jnp.where(better, float(gi), best_group)
        best = jnp.where(better, group_score[gi], best)
    in_group = (expert // EXPERTS_PER_GROUP).astype(F32) == best_group
    chosen = jnp.where(selected, jnp.where(in_group, 1.0, 0.0), 0.0)
    picked = chosen * scores
    gates = picked / jnp.sum(picked, axis=0, keepdims=True)
    lanes = 128
    n_blk = N_TOK // lanes
    li = lax.broadcasted_iota(jnp.int32, (lanes, lanes), 0)
    lj = lax.broadcasted_iota(jnp.int32, (lanes, lanes), 1)
    prefix = jnp.where(li <= lj, 1.0, 0.0).astype(BF16)
    stacked = jnp.concatenate([chosen[:, blk * lanes:(blk + 1) * lanes] for blk in range(n_blk)], axis=0)
    incl_all = jnp.dot(stacked.astype(BF16), prefix, preferred_element_type=F32)
    carry = jnp.zeros((N_EXPERTS, 1), F32)
    for blk in range(n_blk):
        cols = slice(blk * lanes, (blk + 1) * lanes)
        incl = incl_all[blk * N_EXPERTS:(blk + 1) * N_EXPERTS, :]
        rank_ref[:, cols] = incl - chosen[:, cols] + carry
        carry = carry + incl[:, lanes - 1:lanes]
    count = carry
    padded = jnp.floor((count + float(MOE_TILE - 1)) * (1.0 / MOE_TILE)) * float(MOE_TILE)
    erow = lax.broadcasted_iota(jnp.int32, (N_EXPERTS, 1), 0)
    offset = jnp.zeros((N_EXPERTS, 1), F32)
    for e in range(N_EXPERTS - 1):
        offset = offset + jnp.where(erow > e, padded[e:e + 1, :], 0.0)
    position = rank_ref[...] + offset
    ei = lax.broadcasted_iota(jnp.int32, (N_EXPERTS, N_EXPERTS), 0)
    ej = lax.broadcasted_iota(jnp.int32, (N_EXPERTS, N_EXPERTS), 1)
    lower = jnp.where(ej <= ei, 1.0, 0.0).astype(BF16)
    seen = jnp.dot(lower, chosen.astype(BF16), preferred_element_type=F32)
    first = (chosen > 0.5) & (seen < 1.5)
    second = (chosen > 0.5) & (seen > 1.5)
    pick = lambda flag, x: jnp.sum(jnp.where(flag, x, 0.0), axis=0, keepdims=True)
    pos_ref[0:1, :] = pick(first, position).astype(jnp.int32)
    pos_ref[1:2, :] = pick(second, position).astype(jnp.int32)
    w_rows = jnp.concatenate([pick(first, gates), pick(second, gates), jnp.zeros((6, N_TOK), F32)], axis=0)
    ti = lax.broadcasted_iota(jnp.int32, (8, lanes), 0)
    tj = lax.broadcasted_iota(jnp.int32, (8, lanes), 1)
    eye = jnp.where(ti == tj, 1.0, 0.0).astype(BF16)
    tn = (((0,), (0,)), ((), ()))
    hi = w_rows.astype(BF16)
    r1 = w_rows - hi.astype(F32)
    mid = r1.astype(BF16)
    lo = (r1 - mid.astype(F32)).astype(BF16)
    w_cols = lax.dot_general(hi, eye, tn, preferred_element_type=F32)
    w_cols = w_cols + lax.dot_general(mid, eye, tn, preferred_element_type=F32)
    w_cols = w_cols + lax.dot_general(lo, eye, tn, preferred_element_type=F32)
    w_ref[...] = w_cols[:, :TOP_K]
    start = (lax.broadcasted_iota(jnp.int32, (N_EXPERTS, lanes), 1) * MOE_TILE).astype(F32)
    end = offset + padded
    tile_expert = jnp.sum(jnp.where(end <= start, 1.0, 0.0), axis=0, keepdims=True)
    inside = (offset <= start) & (start < end)
    real = jnp.clip(count - (start - offset), 0.0, float(MOE_TILE))
    tile_rows = jnp.sum(jnp.where(inside, real, 0.0), axis=0, keepdims=True)
    plan_ref[0:1, :] = jnp.minimum(tile_expert, float(N_EXPERTS - 1)).astype(jnp.int32)
    plan_ref[1:2, :] = tile_rows.astype(jnp.int32)


def _router(logits_t, router_b):
    whole = lambda shape: pl.BlockSpec(shape, lambda i: (0, 0))
    return pl.pallas_call(
        _router_kernel,
        grid=(1,),
        in_specs=[whole((N_EXPERTS, N_TOK)), whole((N_EXPERTS, 1))],
        out_specs=(whole((2, N_TOK)), whole((N_TOK, TOP_K)), whole((2, 128))),
        out_shape=(jax.ShapeDtypeStruct((2, N_TOK), jnp.int32),
                   jax.ShapeDtypeStruct((N_TOK, TOP_K), F32),
                   jax.ShapeDtypeStruct((2, 128), jnp.int32)),
        scratch_shapes=[pltpu.VMEM((N_EXPERTS, N_TOK), F32)],
        compiler_params=_params("arbitrary"),
        name="router",
    )(logits_t, router_b.reshape(N_EXPERTS, 1))


def _sc_mesh():
    return plsc.VectorSubcoreMesh(core_axis_name="c", subcore_axis_name="s")


def _sc_worker_base():
    return (lax.axis_index("s") * SC_CORES + lax.axis_index("c")) * SC_TOKENS_PER_WORKER


def _moe_dispatch(h, pos_a, pos_b):
    n_chunks = SC_TOKENS_PER_WORKER // SC_CHUNK
    idx = pltpu.VMEM((SC_CHUNK,), jnp.int32)

    @functools.partial(
        pl.kernel, mesh=_sc_mesh(),
        out_type=jax.ShapeDtypeStruct((MOE_ROWS, ROW_WORDS), jnp.int32),
        scratch_types=[idx, idx, idx, idx, pltpu.VMEM((2, SC_CHUNK, ROW_WORDS), jnp.int32),
                       pltpu.SemaphoreType.DMA((6,)), pltpu.SemaphoreType.DMA((4,))],
        name="moe_dispatch",
    )
    def run(h_hbm, pa_hbm, pb_hbm, xs_hbm, ia0, ib0, ia1, ib1, rows_v, sem_in, sem_out):
        base = _sc_worker_base()
        ia, ib = (ia0, ia1), (ib0, ib1)

        def start_loads(c):
            slot = c % 2
            tok = pl.ds(pl.multiple_of(base + c * SC_CHUNK, 8), SC_CHUNK)
            return (pltpu.async_copy(pa_hbm.at[tok], ia[slot], sem_in.at[3 * slot]),
                    pltpu.async_copy(pb_hbm.at[tok], ib[slot], sem_in.at[3 * slot + 1]),
                    pltpu.async_copy(h_hbm.at[tok], rows_v.at[slot], sem_in.at[3 * slot + 2]))

        loads = start_loads(0)
        scatters = [(), ()]
        for c in range(n_chunks):
            slot = c % 2
            for cp in loads:
                cp.wait()
            if c + 1 < n_chunks:
                for cp in scatters[1 - slot]:
                    cp.wait()
                scatters[1 - slot] = ()
                loads = start_loads(c + 1)
            scatters[slot] = (pltpu.async_copy(rows_v.at[slot], xs_hbm.at[ia[slot]], sem_out.at[2 * slot]),
                              pltpu.async_copy(rows_v.at[slot], xs_hbm.at[ib[slot]], sem_out.at[2 * slot + 1]))
        for pending in scatters:
            for cp in pending:
                cp.wait()

    return run(h, pos_a, pos_b)


def _moe_collect(ys, pos_a, pos_b, tok0=0, n_tok=N_TOK):
    per_worker = n_tok // SC_WORKERS
    chunk = SC_CHUNK if per_worker % SC_CHUNK == 0 else 32
    n_chunks = per_worker // chunk
    out = jax.ShapeDtypeStruct((n_tok, ROW_WORDS), jnp.int32)
    idx = pltpu.VMEM((per_worker,), jnp.int32)
    rows = pltpu.VMEM((2, chunk, ROW_WORDS), jnp.int32)

    @functools.partial(
        pl.kernel, mesh=_sc_mesh(), out_type=(out, out),
        scratch_types=[idx, idx, rows, rows, pltpu.SemaphoreType.DMA((4,)), pltpu.SemaphoreType.DMA((4,))],
        name="moe_collect",
    )
    def run(ys_hbm, pa_hbm, pb_hbm, ya_hbm, yb_hbm, ia_v, ib_v, ra_v, rb_v, sem_g, sem_w):
        base = (lax.axis_index("s") * SC_CORES + lax.axis_index("c")) * per_worker
        mine = pl.ds(pl.multiple_of(tok0 + base, 8), per_worker)
        pltpu.sync_copy(pa_hbm.at[mine], ia_v)
        pltpu.sync_copy(pb_hbm.at[mine], ib_v)
        writes = [(), ()]
        for c in range(n_chunks):
            slot = c % 2
            for cp in writes[slot]:
                cp.wait()
            part = pl.ds(c * chunk, chunk)
            tok = pl.ds(pl.multiple_of(base + c * chunk, 8), chunk)
            ga = pltpu.async_copy(ys_hbm.at[ia_v.at[part]], ra_v.at[slot], sem_g.at[slot])
            gb = pltpu.async_copy(ys_hbm.at[ib_v.at[part]], rb_v.at[slot], sem_g.at[2 + slot])
            ga.wait()
            wa = pltpu.async_copy(ra_v.at[slot], ya_hbm.at[tok], sem_w.at[slot])
            gb.wait()
            wb = pltpu.async_copy(rb_v.at[slot], yb_hbm.at[tok], sem_w.at[2 + slot])
            writes[slot] = (wa, wb)
        for pending in writes:
            for cp in pending:
                cp.wait()

    return run(ys, pos_a, pos_b)


def _experts_kernel(plan_ref, xs_ref, wg_hbm, wu_hbm, wd_hbm, y_ref,
                    sg_ref, su_ref, sd_ref, wgb_ref, wub_ref, wdb_ref, hid_ref, sems, seg_ref, *, layer):
    n_tiles = pl.num_programs(0) * EXPERT_TILES_PER_STEP

    def weight_copies(e, slot):
        return (pltpu.make_async_copy(wg_hbm.at[layer, e], sg_ref.at[slot], sems.at[slot, 0]),
                pltpu.make_async_copy(wu_hbm.at[layer, e], su_ref.at[slot], sems.at[slot, 1]),
                pltpu.make_async_copy(wd_hbm.at[layer, e], sd_ref.at[slot], sems.at[slot, 2]))

    def tile(t, row0):
        expert = plan_ref[t]
        n_real = plan_ref[PLAN_LANES + t]
        fresh = jnp.logical_or(t == 0, expert != plan_ref[jnp.maximum(t - 1, 0)])

        @pl.when(t == 0)
        def _():
            seg_ref[0] = 0

            @pl.when(n_real > 0)
            def _():
                for cp in weight_copies(expert, 0):
                    cp.start(priority=1)

        @pl.when(jnp.logical_and(n_real > 0, fresh))
        def _():
            slot = seg_ref[0] % 2
            for cp in weight_copies(expert, slot):
                cp.wait()
            wgb_ref[...] = sg_ref[slot].astype(BF16)
            wub_ref[...] = su_ref[slot].astype(BF16)
            wdb_ref[...] = sd_ref[slot].astype(BF16)
            nxt = lax.while_loop(
                lambda u: jnp.logical_and(u < n_tiles, plan_ref[jnp.minimum(u, n_tiles - 1)] == expert),
                lambda u: u + 1, t + 1)
            nxt_c = jnp.minimum(nxt, n_tiles - 1)

            @pl.when(jnp.logical_and(nxt < n_tiles, plan_ref[PLAN_LANES + nxt_c] > 0))
            def _():
                for cp in weight_copies(plan_ref[nxt_c], 1 - slot):
                    cp.start(priority=1)

            seg_ref[0] = seg_ref[0] + 1

        @pl.when(n_real > 0)
        def _():
            n = EXPERT_SUB_ROWS
            n_sub = MOE_TILE // n
            row = lax.broadcasted_iota(jnp.int32, (n, xs_ref.shape[1]), 0)

            def up(r):
                rows = slice(row0 + r * n, row0 + (r + 1) * n)
                words = jnp.where(row < n_real - r * n, xs_ref[rows, :], 0)
                x = _unpack_rows(words).astype(BF16)
                a = jnp.dot(x, wgb_ref[...], preferred_element_type=F32)
                b = jnp.dot(x, wub_ref[...], preferred_element_type=F32)
                hid_ref[r] = ((a * jax.nn.sigmoid(a)) * b).astype(BF16)

            def down(r):
                rows = slice(row0 + r * n, row0 + (r + 1) * n)
                y_ref[rows, :] = _pack_rows(jnp.dot(hid_ref[r], wdb_ref[...], preferred_element_type=F32))

            up(0)
            for r in range(1, n_sub):
                up(r)
                down(r - 1)
            down(n_sub - 1)

    for q in range(EXPERT_TILES_PER_STEP):
        tile(pl.program_id(0) * EXPERT_TILES_PER_STEP + q, q * MOE_TILE)


def _experts(plan, xs, w_gate, w_up, w_down, layer):
    hbm = pl.BlockSpec(memory_space=pl.ANY)
    step_rows = MOE_TILE * EXPERT_TILES_PER_STEP
    return pl.pallas_call(
        functools.partial(_experts_kernel, layer=layer),
        grid_spec=pltpu.PrefetchScalarGridSpec(
            num_scalar_prefetch=1,
            grid=(MOE_ROWS // step_rows,),
            in_specs=[pl.BlockSpec((step_rows, ROW_WORDS), lambda j, plan: (j, 0)), hbm, hbm, hbm],
            out_specs=pl.BlockSpec((step_rows, ROW_WORDS), lambda j, plan: (j, 0)),
            scratch_shapes=[pltpu.VMEM((2, D_MODEL, D_EXPERT), F32), pltpu.VMEM((2, D_MODEL, D_EXPERT), F32),
                            pltpu.VMEM((2, D_EXPERT, D_MODEL), F32),
                            pltpu.VMEM((D_MODEL, D_EXPERT), BF16), pltpu.VMEM((D_MODEL, D_EXPERT), BF16),
                            pltpu.VMEM((D_EXPERT, D_MODEL), BF16),
                            pltpu.VMEM((MOE_TILE // EXPERT_SUB_ROWS, EXPERT_SUB_ROWS, D_EXPERT), BF16),
                            pltpu.SemaphoreType.DMA((2, 3)), pltpu.SMEM((1,), jnp.int32)],
        ),
        out_shape=jax.ShapeDtypeStruct((MOE_ROWS, ROW_WORDS), jnp.int32),
        compiler_params=_params("arbitrary"),
        name="experts",
    )(plan, xs, w_gate, w_up, w_down)


def _combine_kernel(x_ref, ya_ref, yb_ref, wt_ref, mod_ref, o_ref):
    o_ref[...] = _moe_mix(x_ref, ya_ref, yb_ref, wt_ref, mod_ref)


def _combine(x, moe_out, mod_l, tok0, n_tok, block_rows=1024):
    ya, yb, w_tok = moe_out
    b0 = tok0 // block_rows
    rows = lambda width: pl.BlockSpec((block_rows, width), lambda i: (b0 + i, 0))
    local = pl.BlockSpec((block_rows, ROW_WORDS), lambda i: (i, 0))
    return pl.pallas_call(
        _combine_kernel,
        grid=(n_tok // block_rows,),
        in_specs=[rows(D_MODEL), local, local, rows(TOP_K),
                  pl.BlockSpec((None, 6, D_MODEL), lambda i: (_cond_of_token_block(b0 + i, block_rows), 0, 0))],
        out_specs=pl.BlockSpec((block_rows, D_MODEL), lambda i: (i, 0)),
        out_shape=jax.ShapeDtypeStruct((n_tok, D_MODEL), F32),
        compiler_params=_params("arbitrary"),
        name="combine",
    )(x, ya, yb, w_tok, mod_l)


def _moe(h, logits_t, router_b, w_gate, w_up, w_down, layer, ranges=((0, N_TOK),)):
    pos, w, plan = _router(logits_t, router_b)
    xs = _moe_dispatch(h, pos[0], pos[1])
    ys = _experts(plan.reshape(-1), xs, w_gate, w_up, w_down, layer)
    return [(*_moe_collect(ys, pos[0], pos[1], tok0, n_tok), w) for tok0, n_tok in ranges]


def _dft_tables(L):
    k = np.arange(L)[:, None]
    m = np.arange(L)[None, :]
    r = (k * m) % (2 * L)
    ang = np.pi * r.astype(np.float64) / L
    fc = np.cos(ang)
    fs = np.sin(ang)
    fs[0, :] = np.where(np.arange(L) % 2 == 0, 1.0, -1.0)
    wk = np.full((L, 1), 1.0 / L)
    wk[0, 0] = 0.5 / L
    gc = (fc * wk).T
    gs = (fs * wk).T
    return [jnp.asarray(t.astype(np.float32)).astype(BF16) for t in (fc, fs, gc, gs)]


def _filter_consts(L):
    t = np.linspace(0.0, 1.0, L, dtype=np.float32)[:, None]
    w = (np.float32(2.0 * np.pi) * np.arange(L, dtype=np.float32)[:, None] / np.float32(L)).astype(np.float32)
    fb = np.linspace(1e-4, HY_BANDS - 1, HY_BANDS, dtype=np.float32)[None, :]
    emb = np.concatenate([t, np.cos(fb * w), -np.sin(fb * w)], axis=-1).astype(np.float32)
    lo = math.log(HY_DECAY_TARGET) / HY_SLOW_PCT
    hi = math.log(HY_DECAY_TARGET) / HY_FAST_PCT
    deltas = np.abs(np.linspace(lo, hi, D_MODEL, dtype=np.float32))
    decay = np.exp(-t * deltas).astype(np.float32)
    return jnp.asarray(emb), jnp.asarray(decay)


def _filter_kernel(emb_ref, w1_ref, b1_ref, w2_ref, b2_ref, fr_ref, w3f_ref, w3b_ref, dec_ref,
                   fc_ref, fs_ref, kr_ref, q_ref, krn_ref, hd_ref):
    @pl.when(pl.program_id(0) == 0)
    def _():
        fr = fr_ref[...]
        h1 = jnp.sin(fr * (jnp.dot(emb_ref[...], w1_ref[...], precision=HIGHEST,
                                   preferred_element_type=F32) + b1_ref[...]))
        hd_ref[...] = jnp.sin(fr * (jnp.dot(h1, w2_ref[...], precision=HIGHEST,
                                            preferred_element_type=F32) + b2_ref[...]))

    hd = hd_ref[...]
    dec = dec_ref[...]
    f = jnp.dot(hd, w3f_ref[...], precision=HIGHEST, preferred_element_type=F32) * dec
    g = jnp.dot(hd, w3b_ref[...], precision=HIGHEST, preferred_element_type=F32) * dec
    row = lax.broadcasted_iota(jnp.int32, f.shape, 0)
    g = jnp.where(row == 0, 0.0, g)
    s = f + g
    d = f - g
    kr = jnp.dot(fc_ref[...], s.astype(BF16), preferred_element_type=F32)
    qq = jnp.dot(fs_ref[...], d.astype(BF16), preferred_element_type=F32)
    alt = jnp.where(row % 2 == 0, 1.0, -1.0)
    nyq = jnp.sum(alt * s, axis=0, keepdims=True)
    kr_ref[...] = kr
    q_ref[...] = jnp.where(row == 0, 0.0, qq)
    krn_ref[...] = jnp.where(row == 0, nyq, kr)


def _hyena_filter_spectrum(L, w1, b1, w2, b2, w3, freq, fc, fs, cblk=256):
    emb, decay = _filter_consts(L)
    ncb = D_MODEL // cblk
    n_emb = 128
    emb = jnp.pad(emb, ((0, 0), (0, n_emb - emb.shape[1])))
    w1 = jnp.pad(w1, ((0, n_emb - w1.shape[0]), (0, 0)))
    full = lambda shape: pl.BlockSpec(shape, lambda j: tuple(0 for _ in shape))
    out_sds = jax.ShapeDtypeStruct((L, D_MODEL), F32)
    out_spec = pl.BlockSpec((L, cblk), lambda j: (0, j))
    return pl.pallas_call(
        _filter_kernel,
        grid=(ncb,),
        in_specs=[
            full((L, n_emb)), full((n_emb, HY_FFN)), full((1, HY_FFN)), full((HY_FFN, HY_FFN)),
            full((1, HY_FFN)), full((1, HY_FFN)),
            pl.BlockSpec((HY_FFN, cblk), lambda j: (0, j)),
            pl.BlockSpec((HY_FFN, cblk), lambda j: (0, ncb + j)),
            pl.BlockSpec((L, cblk), lambda j: (0, j)),
            full((L, L)), full((L, L)),
        ],
        out_specs=(out_spec, out_spec, out_spec),
        out_shape=(out_sds, out_sds, out_sds),
        scratch_shapes=[pltpu.VMEM((L, HY_FFN), F32)],
        compiler_params=_params("arbitrary"),
        name=f"hyena_filter_{L}",
    )(emb, w1, b1.reshape(1, HY_FFN), w2, b2.reshape(1, HY_FFN), freq.reshape(1, HY_FFN), w3, w3, decay, fc, fs)


def _hyena_conv_kernel(x0_ref, x1_ref, v_ref, cw0_ref, cw1_ref, cwv_ref, cb0_ref, cb1_ref, cbv_ref,
                       kr_ref, q_ref, krn_ref, ds_ref, fc_ref, fs_ref, gc_ref, gs_ref, o_ref,
                       zz_ref, gate_ref, skip_ref, yr_ref, yw_ref):
    L = fc_ref.shape[0]
    unit_w = zz_ref.shape[2]
    units = [(slice(s * L, (s + 1) * L), slice(c * unit_w, (c + 1) * unit_w))
             for s in range(x0_ref.shape[0] // L) for c in range(x0_ref.shape[1] // unit_w)]
    row = lax.broadcasted_iota(jnp.int32, (L, unit_w), 0)

    def gating(i):
        rows, cols = units[i]

        def short_conv(u_ref, w_ref, b_ref):
            u = u_ref[rows, cols].astype(F32)
            w = w_ref[:, cols]
            prev = jnp.where(row == 0, 0.0, pltpu.roll(u, 1, axis=0))
            nxt = jnp.where(row == L - 1, 0.0, pltpu.roll(u, L - 1, axis=0))
            return prev * w[0:1, :] + u * w[1:2, :] + nxt * w[2:3, :] + b_ref[:, cols]

        x0 = short_conv(x0_ref, cw0_ref, cb0_ref)
        zz = short_conv(v_ref, cwv_ref, cbv_ref) * short_conv(x1_ref, cw1_ref, cb1_ref)
        zz_ref[i] = zz.astype(BF16)
        gate_ref[i] = x0
        skip_ref[i] = x0 * zz * ds_ref[:, cols]

    def spectrum(i):
        cols = units[i][1]
        ur = jnp.dot(fc_ref[...], zz_ref[i], preferred_element_type=F32)
        p = jnp.dot(fs_ref[...], zz_ref[i], preferred_element_type=F32)
        qq = q_ref[:, cols]
        yr_ref[i] = (ur * kr_ref[:, cols] - p * qq).astype(BF16)
        yw_ref[i] = (ur * qq + p * krn_ref[:, cols]).astype(BF16)

    def synthesis(i):
        rows, cols = units[i]
        y = jnp.dot(gc_ref[...], yr_ref[i], preferred_element_type=F32)
        y = y + jnp.dot(gs_ref[...], yw_ref[i], preferred_element_type=F32)
        o_ref[rows, cols] = (gate_ref[i] * y + skip_ref[i]).astype(o_ref.dtype)

    for t in range(len(units) + 2):
        if t < len(units):
            gating(t)
        if 0 <= t - 1 < len(units):
            spectrum(t - 1)
        if 0 <= t - 2 < len(units):
            synthesis(t - 2)


def _hyena_conv(u, conv_w, conv_b, dskip, spectrum, tables, *, latent):
    L = LATENT_LEN if latent else PROMPT_LEN
    n_seq = N_LATENT_SEQ if latent else N_PROMPT_SEQ
    cblk = 512
    unit_w = 256 if latent else 512
    ncb = D_MODEL // cblk
    seqs = 1 if latent else 8
    unit = (seqs * cblk // unit_w, L, unit_w)
    row0 = (N_PROMPT_TOK // L) if latent else 0
    kr, qq, krn = spectrum
    fc, fs, gc, gs = tables

    def part(p, rows):
        if rows != L:
            return pl.BlockSpec((rows, cblk), lambda j, s: (0, p * ncb + j))
        return pl.BlockSpec((seqs * L, cblk), lambda j, s: (row0 // seqs + s, p * ncb + j))

    def const_cols(rows):
        return pl.BlockSpec((rows, cblk), lambda j, s: (0, j))

    mat = pl.BlockSpec((L, L), lambda j, s: (0, 0))
    conv_b2 = conv_b.reshape(1, 3 * D_MODEL)
    in_specs = [part(0, L), part(1, L), part(2, L),
                part(0, 3), part(1, 3), part(2, 3),
                part(0, 1), part(1, 1), part(2, 1),
                const_cols(L), const_cols(L), const_cols(L), const_cols(1),
                mat, mat, mat, mat]
    args = [u, u, u, conv_w, conv_w, conv_w, conv_b2, conv_b2, conv_b2,
            kr, qq, krn, dskip.reshape(1, D_MODEL), fc, fs, gc, gs]
    return pl.pallas_call(
        _hyena_conv_kernel,
        grid=(ncb, n_seq // seqs),
        in_specs=in_specs,
        out_specs=pl.BlockSpec((seqs * L, cblk), lambda j, s: (s, j)),
        out_shape=jax.ShapeDtypeStruct((n_seq * L, D_MODEL), BF16),
        scratch_shapes=[pltpu.VMEM(unit, BF16), pltpu.VMEM(unit, F32), pltpu.VMEM(unit, F32),
                        pltpu.VMEM(unit, BF16), pltpu.VMEM(unit, BF16)],
        compiler_params=_params("arbitrary", "arbitrary"),
        name="hyena_conv_latent" if latent else "hyena_conv_prompt",
    )(*args)


def kernel(x_prompt, x_sample, cache_k, cache_v, state_hgrn, c, c_ctx, norm_g, mod_w, mod_b, ab_in_w, hgrn_lb, hgrn_onorm_g, attn_qnorm_g, attn_knorm_g, ab_out_w, hy_in_w, hy_in_b, hy_conv_w, hy_conv_b, hy_f_w1, hy_f_b1, hy_f_w2, hy_f_b2, hy_f_w3, hy_f_freq, hy_dskip, hy_out_w, router_w, router_b, moe_w_gate, moe_w_up, moe_w_down):
    xp = x_prompt.reshape(N_PROMPT_TOK, D_MODEL)
    xl = x_sample.reshape(N_LATENT_TOK, D_MODEL)
    cond = jnp.concatenate([c_ctx[None, :], c, jnp.zeros((N_COND - 1 - N_LATENT_SEQ, D_MODEL), F32)], axis=0)
    mod = _modulation(cond, mod_w, mod_b)
    router_wp = jnp.pad(router_w, ((0, 0), (0, ROUTER_LANES - N_EXPERTS)))

    z = _in_proj0(xp, xl, norm_g[0, 0], mod[0], ab_in_w[0])
    oa_p, new_state = _hgrn(z, hgrn_lb, hgrn_onorm_g[0], None, latent=False)
    oa_l = _hgrn(z, hgrn_lb, hgrn_onorm_g[0], state_hgrn, latent=True)
    ob_p, k_fm, v_fm = _attention_prompt(z, attn_qnorm_g[0], attn_knorm_g[0])
    fm_shape = (N_PROMPT_SEQ, 1, KV_HEADS, HEAD_DIM, PROMPT_LEN)
    new_k = jnp.swapaxes(k_fm.reshape(fm_shape), -1, -2)
    new_v = jnp.swapaxes(v_fm.reshape(fm_shape), -1, -2)
    ob_l = _attention_latent(z, attn_qnorm_g[0], attn_knorm_g[0], cache_k, cache_v)
    x, h, logits_t = _out_proj([(oa_p, oa_l), (ob_p, ob_l)], ab_out_w[0], (xp, xl), norm_g[0, 1], mod[0],
                               router_wp)
    (moe_out,) = _moe(h, logits_t, router_b, moe_w_gate, moe_w_up, moe_w_down, 0)

    x, u = _in_proj1(x, moe_out, mod[0], norm_g[1, 0], mod[1], hy_in_w[0], hy_in_b[0])
    pre = []
    for latent in (False, True):
        L = LATENT_LEN if latent else PROMPT_LEN
        tables = _dft_tables(L)
        spectrum = _hyena_filter_spectrum(L, hy_f_w1[0], hy_f_b1[0], hy_f_w2[0], hy_f_b2[0], hy_f_w3[0],
                                          hy_f_freq[0], tables[0], tables[1])
        pre.append(_hyena_conv(u, hy_conv_w[0], hy_conv_b[0], hy_dskip[0], spectrum, tables, latent=latent))
    x, h, logits_t = _out_proj([tuple(pre)], hy_out_w[0], (x,), norm_g[1, 1], mod[1], router_wp)
    trunks = ((0, N_PROMPT_TOK), (N_PROMPT_TOK, N_LATENT_TOK))
    out_p, out_l = _moe(h, logits_t, router_b, moe_w_gate, moe_w_up, moe_w_down, 1, ranges=trunks)

    y_prompt = _combine(x, out_p, mod[1], *trunks[0]).reshape(N_PROMPT_SEQ, PROMPT_LEN, D_MODEL)
    y_sample = _combine(x, out_l, mod[1], *trunks[1]).reshape(N_LATENT_SEQ, LATENT_LEN, D_MODEL)
    return (y_prompt, y_sample, new_k, new_v, new_state)
```

```python
import functools
import math

import numpy as np
import jax
import jax.numpy as jnp
from jax import lax
from jax.experimental import pallas as pl
from jax.experimental.pallas import tpu as pltpu
from jax.experimental.pallas import tpu_sc as plsc

F32 = jnp.float32
BF16 = jnp.bfloat16
HIGHEST = lax.Precision.HIGHEST

D_MODEL = 1024
N_PROMPT_SEQ = 32
PROMPT_LEN = 256
N_LATENT_SEQ = 2
LATENT_LEN = 1024
PAST_LEN = 512
GRID_W = 64
N_PROMPT_TOK = N_PROMPT_SEQ * PROMPT_LEN
N_LATENT_TOK = N_LATENT_SEQ * LATENT_LEN
N_TOK = N_PROMPT_TOK + N_LATENT_TOK
N_COND = 8
EPS = 1e-6

A_WIDTH = 512
A_HEADS = 4
A_DK = 128
CHUNK = 64
HGRN_BLOCK = 128
HGRN_HEADS_PER_STEP = 4
HEAD_DIM = 64
Q_HEADS = 8
KV_HEADS = 2
Q_PER_KV = Q_HEADS // KV_HEADS
Q_BLOCK = 256
ROPE_THETA = 10000.0
ROPE_PAIRS = HEAD_DIM // 4

HY_BANDS = 16
HY_FFN = 64
HY_DECAY_TARGET = 1e-2
HY_FAST_PCT = 0.3
HY_SLOW_PCT = 1.5

N_EXPERTS = 16
N_GROUPS = 4
EXPERTS_PER_GROUP = 4
TOP_K = 2
D_EXPERT = 512
ROUTER_LANES = 128
OUT_PROJ_SUB_ROWS = 256
EXPERT_SUB_ROWS = 256
EXPERT_TILES_PER_STEP = 2
IN_PROJ_SUB_ROWS = 256
MOE_TILE = 512
MOE_ROWS = N_TOK * TOP_K + N_EXPERTS * MOE_TILE
PLAN_LANES = 128

SC_CORES = 2
SC_WORKERS = 32
SC_TOKENS_PER_WORKER = N_TOK // SC_WORKERS
SC_CHUNK = 40
ROW_WORDS = D_MODEL // 2

VMEM_LIMIT = 56 * 1024 * 1024


def _params(*sem):
    return pltpu.CompilerParams(dimension_semantics=sem, vmem_limit_bytes=VMEM_LIMIT)


def _pack_rows(x):
    n = x.shape[1] // 2
    bits = pltpu.bitcast(x.astype(BF16).astype(F32), jnp.uint32)
    return pltpu.bitcast(bits[:, :n] | (bits[:, n:] >> 16), jnp.int32)


def _unpack_rows(p):
    bits = pltpu.bitcast(p, jnp.uint32)
    hi = pltpu.bitcast(bits & jnp.uint32(0xFFFF0000), F32)
    lo = pltpu.bitcast(bits << 16, F32)
    return jnp.concatenate([hi, lo], axis=1)


def _cond_of_token_block(i, block_rows):
    start = i * block_rows
    return jnp.where(start < N_PROMPT_TOK, 0, 1 + (start - N_PROMPT_TOK) // LATENT_LEN)


def _mod_kernel(cond_ref, w_ref, b_ref, o_ref):
    cnd = cond_ref[...]
    s = cnd * jax.nn.sigmoid(cnd)
    s_hi = s.astype(BF16)
    s_lo = (s - s_hi.astype(F32)).astype(BF16)
    w = w_ref[...]
    w_hi = w.astype(BF16)
    w_lo = (w - w_hi.astype(F32)).astype(BF16)
    acc = jnp.dot(s_hi, w_hi, preferred_element_type=F32)
    acc = acc + jnp.dot(s_lo, w_hi, preferred_element_type=F32)
    acc = acc + jnp.dot(s_hi, w_lo, preferred_element_type=F32)
    o_ref[...] = acc + b_ref[...]


def _modulation(cond, mod_w, mod_b):
    depth = mod_w.shape[0]
    n_mod = 6
    cols = 2 * D_MODEL
    n_step = n_mod * D_MODEL // cols
    out = pl.pallas_call(
        _mod_kernel,
        grid=(depth, n_step),
        in_specs=[
            pl.BlockSpec((N_COND, D_MODEL), lambda l, j: (0, 0)),
            pl.BlockSpec((None, D_MODEL, cols), lambda l, j: (l, 0, j)),
            pl.BlockSpec((None, 1, cols), lambda l, j: (l, 0, j)),
        ],
        out_specs=pl.BlockSpec((None, N_COND, cols), lambda l, j: (l, 0, j)),
        out_shape=jax.ShapeDtypeStruct((depth, N_COND, n_mod * D_MODEL), F32),
        compiler_params=_params("arbitrary", "arbitrary"),
        name="modulation",
    )(cond, mod_w, mod_b.reshape(depth, 1, n_mod * D_MODEL))
    return out.reshape(depth, N_COND, n_mod, D_MODEL)


def _modulated_norm(x, g, mod, shift_row, scale_row):
    ms = jnp.mean(x * x, axis=-1, keepdims=True)
    y = x * lax.rsqrt(ms + EPS) * g
    return y * (1.0 + mod[scale_row:scale_row + 1, :]) + mod[shift_row:shift_row + 1, :]


def _trunk_specs(block_rows, width):
    n_prompt_blocks = N_PROMPT_TOK // block_rows
    return (pl.BlockSpec((block_rows, width), lambda i: (jnp.minimum(i, n_prompt_blocks - 1), 0)),
            pl.BlockSpec((block_rows, width), lambda i: (jnp.maximum(i - n_prompt_blocks, 0), 0)))


def _select_trunk(p_ref, l_ref, rows=slice(None)):
    block_rows = p_ref.shape[0]
    return jnp.where(pl.program_id(0) < N_PROMPT_TOK // block_rows, p_ref[rows, :], l_ref[rows, :])


def _cast_once(w_ref, wb_ref):
    @pl.when(pl.program_id(0) == 0)
    def _():
        wb_ref[...] = w_ref[...].astype(BF16)


def _resident(shape):
    return pl.BlockSpec(shape, lambda i: tuple(0 for _ in shape), pipeline_mode=pl.Buffered(1))


def _mod_spec(block_rows):
    return pl.BlockSpec((None, 6, D_MODEL), lambda i: (_cond_of_token_block(i, block_rows), 0, 0))


def _in_proj0_kernel(xp_ref, xl_ref, g_ref, mod_ref, w_ref, o_ref, wb_ref, hb_ref):
    _cast_once(w_ref, wb_ref)
    n = IN_PROJ_SUB_ROWS
    n_sub = xp_ref.shape[0] // n

    def prepare(r):
        x = _select_trunk(xp_ref, xl_ref, slice(r * n, (r + 1) * n))
        hb_ref[r] = _modulated_norm(x, g_ref[...], mod_ref[...], 0, 1).astype(BF16)

    def project(r):
        u = jnp.dot(hb_ref[r], wb_ref[...], preferred_element_type=F32)
        o_ref[r * n:(r + 1) * n, :] = u.astype(o_ref.dtype)

    prepare(0)
    for r in range(1, n_sub):
        prepare(r)
        project(r - 1)
    project(n_sub - 1)


def _in_proj0(x_prompt, x_latent, g, mod_l, w, block_rows=512):
    n = w.shape[1]
    return pl.pallas_call(
        _in_proj0_kernel,
        grid=(N_TOK // block_rows,),
        in_specs=[*_trunk_specs(block_rows, D_MODEL), _resident((1, D_MODEL)), _mod_spec(block_rows),
                  _resident((D_MODEL, n))],
        out_specs=pl.BlockSpec((block_rows, n), lambda i: (i, 0)),
        out_shape=jax.ShapeDtypeStruct((N_TOK, n), BF16),
        scratch_shapes=[pltpu.VMEM((D_MODEL, n), BF16),
                        pltpu.VMEM((block_rows // IN_PROJ_SUB_ROWS, IN_PROJ_SUB_ROWS, D_MODEL), BF16)],
        compiler_params=_params("arbitrary"),
        name="in_proj0",
    )(x_prompt, x_latent, g.reshape(1, D_MODEL), mod_l, w)


def _moe_mix(x_ref, ya_ref, yb_ref, wt_ref, mod_ref, rows=slice(None)):
    wt = wt_ref[rows, :]
    mix = wt[:, 0:1] * _unpack_rows(ya_ref[rows, :]) + wt[:, 1:2] * _unpack_rows(yb_ref[rows, :])
    return x_ref[rows, :] + mod_ref[5:6, :] * mix


def _in_proj1_kernel(x_ref, ya_ref, yb_ref, wt_ref, modp_ref, g_ref, mod_ref, w_ref, b_ref, xo_ref, o_ref,
                     wb_ref, hb_ref):
    _cast_once(w_ref, wb_ref)
    n = IN_PROJ_SUB_ROWS
    n_sub = x_ref.shape[0] // n

    def prepare(r):
        rows = slice(r * n, (r + 1) * n)
        x = _moe_mix(x_ref, ya_ref, yb_ref, wt_ref, modp_ref, rows)
        xo_ref[rows, :] = x
        hb_ref[r] = _modulated_norm(x, g_ref[...], mod_ref[...], 0, 1).astype(BF16)

    def project(r):
        rows = slice(r * n, (r + 1) * n)
        u = jnp.dot(hb_ref[r], wb_ref[...], preferred_element_type=F32) + b_ref[...]
        o_ref[rows, :] = u.astype(o_ref.dtype)

    prepare(0)
    for r in range(1, n_sub):
        prepare(r)
        project(r - 1)
    project(n_sub - 1)


def _in_proj1(x, moe_out, mod_prev, g, mod_l, w, bias, block_rows=512):
    ya, yb, w_tok = moe_out
    n = w.shape[1]
    tok = pl.BlockSpec((block_rows, D_MODEL), lambda i: (i, 0))
    packed = pl.BlockSpec((block_rows, ROW_WORDS), lambda i: (i, 0))
    return pl.pallas_call(
        _in_proj1_kernel,
        grid=(N_TOK // block_rows,),
        in_specs=[tok, packed, packed, pl.BlockSpec((block_rows, TOP_K), lambda i: (i, 0)), _mod_spec(block_rows),
                  _resident((1, D_MODEL)), _mod_spec(block_rows), _resident((D_MODEL, n)), _resident((1, n))],
        out_specs=(tok, pl.BlockSpec((block_rows, n), lambda i: (i, 0))),
        out_shape=(jax.ShapeDtypeStruct((N_TOK, D_MODEL), F32), jax.ShapeDtypeStruct((N_TOK, n), BF16)),
        scratch_shapes=[pltpu.VMEM((D_MODEL, n), BF16),
                        pltpu.VMEM((block_rows // IN_PROJ_SUB_ROWS, IN_PROJ_SUB_ROWS, D_MODEL), BF16)],
        compiler_params=_params("arbitrary"),
        name="in_proj1",
    )(x, ya, yb, w_tok, mod_prev, g.reshape(1, D_MODEL), mod_l, w, bias.reshape(1, n))


def _hgrn_kernel(*refs, seq_len, with_state):
    if with_state:
        (q_ref, zf_ref, zb_ref, i_ref, ga_ref, lb_ref, og_ref, s0_ref, o_ref, of_ref, ob_ref) = refs
    else:
        (q_ref, zf_ref, zb_ref, i_ref, ga_ref, lb_ref, og_ref, o_ref, s_ref, of_ref, ob_ref) = refs
    n_blocks = seq_len // HGRN_BLOCK
    chunks_per_block = HGRN_BLOCK // CHUNK

    lbr = lb_ref[...]
    mx = jnp.maximum(lbr[0], lbr[1])
    e0 = jnp.exp(lbr[0] - mx)
    e1 = jnp.exp(lbr[1] - mx)
    lb = e0 / (e0 + e1)

    row = lax.broadcasted_iota(jnp.int32, (HGRN_BLOCK, HGRN_BLOCK), 0)
    col = lax.broadcasted_iota(jnp.int32, (HGRN_BLOCK, HGRN_BLOCK), 1)
    same_chunk = (row // CHUNK) == (col // CHUNK)
    nt = (((1,), (1,)), ((), ()))
    tn = (((0,), (0,)), ((), ()))

    def per_chunk_row(x, idx):
        return jnp.concatenate(
            [jnp.broadcast_to(x[n * CHUNK + idx:n * CHUNK + idx + 1, :], (CHUNK, x.shape[1]))
             for n in range(chunks_per_block)], axis=0)

    def in_chunk_cumsum(tri, x):
        hi = x.astype(BF16)
        lo = (x - hi.astype(F32)).astype(BF16)
        return jnp.dot(tri, hi, preferred_element_type=F32) + jnp.dot(tri, lo, preferred_element_type=F32)

    def prepare(blk, cols, z_ref, lbd, forward):
        rows = slice(blk * HGRN_BLOCK, (blk + 1) * HGRN_BLOCK)
        keep = (same_chunk & (col <= row)) if forward else (same_chunk & (col >= row))
        tri = jnp.where(keep, 1.0, 0.0).astype(BF16)
        mid = CHUNK // 2 if forward else CHUNK - 1 - CHUNK // 2
        last = CHUNK - 1 if forward else 0
        f = lbd + (1.0 - lbd) * jax.nn.sigmoid(z_ref[rows, cols].astype(F32))
        lf = jnp.log(f)
        k = 1.0 - f
        q = q_ref[rows, cols].astype(F32)
        b = in_chunk_cumsum(tri, lf)
        bm = per_chunk_row(b, mid)
        bl = per_chunk_row(b, last)
        return dict(
            rows=rows, cols=cols, keep=keep, forward=forward,
            vb=i_ref[rows, cols].astype(BF16),
            qe=(q * jnp.exp(b - bm)).astype(BF16), ke=(k * jnp.exp(bm - b)).astype(BF16),
            qb=(q * jnp.exp(b)).astype(BF16), ks=(k * jnp.exp(bl - b)).astype(BF16), decay=jnp.exp(bl))

    def within_chunks(u):
        att = lax.dot_general(u["qe"], u["ke"], nt, preferred_element_type=F32)
        att = jnp.where(u["keep"], att, 0.0)
        u["o_intra"] = jnp.dot(att.astype(BF16), u["vb"], preferred_element_type=F32)
        u["upd"] = [lax.dot_general(u["vb"][n * CHUNK:(n + 1) * CHUNK], u["ks"][n * CHUNK:(n + 1) * CHUNK], tn,
                                    preferred_element_type=F32) for n in range(chunks_per_block)]

    def across_chunks(u, st, out_ref):
        order = range(chunks_per_block) if u["forward"] else range(chunks_per_block - 1, -1, -1)
        o_inter = [None] * chunks_per_block
        for n in order:
            cr = slice(n * CHUNK, (n + 1) * CHUNK)
            o_inter[n] = lax.dot_general(u["qb"][cr], st.astype(BF16), nt, preferred_element_type=F32)
            st = st * u["decay"][n * CHUNK:n * CHUNK + 1, :] + u["upd"][n]
        out_ref[u["rows"], u["cols"]] = u["o_intra"] + jnp.concatenate(o_inter, axis=0)
        return st

    n_heads = q_ref.shape[1] // A_DK
    head_cols = [slice(hd * A_DK, (hd + 1) * A_DK) for hd in range(n_heads)]
    if with_state:
        states = {(hd, d): s0_ref[d, hd].T for hd in range(n_heads) for d in range(2)}
    else:
        states = {(hd, d): jnp.zeros((A_DK, A_DK), F32) for hd in range(n_heads) for d in range(2)}
    for step in range(n_blocks):
        units = {}
        for hd, cols in enumerate(head_cols):
            units[hd, 0] = prepare(step, cols, zf_ref, lb[0:1, cols], True)
            units[hd, 1] = prepare(n_blocks - 1 - step, cols, zb_ref, lb[1:2, cols], False)
        for u in units.values():
            within_chunks(u)
        for key, u in units.items():
            states[key] = across_chunks(u, states[key], of_ref if key[1] == 0 else ob_ref)
    for hd, cols in enumerate(head_cols):
        if not with_state:
            s_ref[0, hd] = states[hd, 0].T
            s_ref[1, hd] = states[hd, 1].T
        o = of_ref[:, cols] + ob_ref[:, cols]
        o = o * lax.rsqrt(jnp.mean(o * o, axis=-1, keepdims=True) + EPS) * og_ref[:, cols]
        ga = ga_ref[:, cols].astype(F32)
        o_ref[:, cols] = (o * (ga * jax.nn.sigmoid(ga))).astype(o_ref.dtype)


def _hgrn(z, hgrn_lb, onorm_g, state, *, latent):
    seq_len = LATENT_LEN if latent else PROMPT_LEN
    n_seq = N_LATENT_SEQ if latent else N_PROMPT_SEQ
    row0 = (N_PROMPT_TOK // seq_len) if latent else 0

    hw = HGRN_HEADS_PER_STEP * A_DK
    n_hg = A_HEADS // HGRN_HEADS_PER_STEP

    def zspec(part):
        return pl.BlockSpec((seq_len, hw), lambda s, h: (row0 + s, part * n_hg + h))

    in_specs = [zspec(0), zspec(1), zspec(2), zspec(3), zspec(4),
                pl.BlockSpec((2, 2, hw), lambda s, h: (0, 0, h)),
                pl.BlockSpec((1, hw), lambda s, h: (0, h))]
    args = [z, z, z, z, z, hgrn_lb, onorm_g.reshape(1, A_WIDTH)]
    state_spec = pl.BlockSpec((None, None, 2, HGRN_HEADS_PER_STEP, A_DK, A_DK), lambda s, h: (s, 0, 0, h, 0, 0))
    o_shape = jax.ShapeDtypeStruct((n_seq * seq_len, A_WIDTH), BF16)
    o_spec = pl.BlockSpec((seq_len, hw), lambda s, h: (s, h))
    if latent:
        in_specs.append(state_spec)
        args.append(state)
        out_shape, out_specs = o_shape, o_spec
    else:
        out_shape = (o_shape, jax.ShapeDtypeStruct((n_seq, 1, 2, A_HEADS, A_DK, A_DK), F32))
        out_specs = (o_spec, state_spec)
    return pl.pallas_call(
        functools.partial(_hgrn_kernel, seq_len=seq_len, with_state=latent),
        grid=(n_seq, n_hg),
        in_specs=in_specs,
        out_specs=out_specs,
        out_shape=out_shape,
        scratch_shapes=[pltpu.VMEM((seq_len, hw), F32), pltpu.VMEM((seq_len, hw), F32)],
        compiler_params=_params("arbitrary", "arbitrary"),
        name="hgrn_latent" if latent else "hgrn_prompt",
    )(*args)


def _rope_tables():
    pos = np.arange(LATENT_LEN)
    row, colp = pos // GRID_W, pos % GRID_W
    inv = ROPE_THETA ** (-np.arange(ROPE_PAIRS, dtype=np.float32) / ROPE_PAIRS)
    inv = inv.astype(np.float32)
    ang_r = (row.astype(np.float32)[:, None] * inv).astype(np.float32)
    ang_c = (colp.astype(np.float32)[:, None] * inv).astype(np.float32)
    cos = np.concatenate([np.cos(ang_r), np.cos(ang_r), np.cos(ang_c), np.cos(ang_c)], axis=1)
    sin = np.concatenate([-np.sin(ang_r), np.sin(ang_r), -np.sin(ang_c), np.sin(ang_c)], axis=1)
    return cos.astype(np.float32), sin.astype(np.float32)


def _head_mean_matrix(width):
    idx = np.arange(width) // HEAD_DIM
    return jnp.asarray((idx[:, None] == idx[None, :]).astype(np.float32) / HEAD_DIM).astype(BF16)


def _attn_kernel(*refs, latent):
    if latent:
        (q_ref, k_ref, v_ref, qg_ref, kg_ref, gq_ref, gk_ref, cosq_ref, sinq_ref, cosk_ref, sink_ref,
         ck_ref, cv_ref, o_ref, kd_ref, vd_ref, ckd_ref, cvd_ref) = refs
    else:
        (q_ref, k_ref, v_ref, qg_ref, kg_ref, gq_ref, gk_ref, o_ref, kout_ref, vout_ref) = refs
    pair_w = 2 * HEAD_DIM

    def head_norm(x, mean_ref, gain):
        sq = x * x
        hi = sq.astype(BF16)
        lo = (sq - hi.astype(F32)).astype(BF16)
        ms = jnp.dot(hi, mean_ref[...], preferred_element_type=F32)
        ms = ms + jnp.dot(lo, mean_ref[...], preferred_element_type=F32)
        return x * lax.rsqrt(ms + EPS) * gain

    def rope(x, cos, sin):
        n = x.shape[1]
        lane = lax.broadcasted_iota(jnp.int32, x.shape, 1)
        first_of_pair = (lane // ROPE_PAIRS) % 2 == 0
        swapped = jnp.where(first_of_pair, pltpu.roll(x, n - ROPE_PAIRS, axis=1), pltpu.roll(x, ROPE_PAIRS, axis=1))
        return x * cos + swapped * sin

    nt = (((1,), (1,)), ((), ()))

    def key_value_tiles(rows, seq_idx):
        k = head_norm(k_ref[rows, :].astype(F32), gk_ref, kg_ref[...])
        if latent:
            k = rope(k, cosk_ref[...], sink_ref[...])
        v = v_ref[rows, :].astype(F32)
        low_kv = lax.broadcasted_iota(jnp.int32, k.shape, 1) < HEAD_DIM
        k_swapped = pltpu.roll(k, HEAD_DIM, axis=1)
        v_swapped = pltpu.roll(v, HEAD_DIM, axis=1)
        if not latent:
            kout_ref[seq_idx] = k.T
            vout_ref[seq_idx] = v.T
        kd, vd = [], []
        for j in range(KV_HEADS):
            kd.append((jnp.where(low_kv, k, k_swapped) if j == 0 else jnp.where(low_kv, k_swapped, k)).astype(BF16))
            vj = (jnp.where(low_kv, v, v_swapped) if j == 0 else jnp.where(low_kv, v_swapped, v)).astype(BF16)
            vd.append(jnp.concatenate([vj, jnp.ones_like(vj)], axis=1))
        return kd, vd

    def query_units(rows, kd, vd):
        q = head_norm(q_ref[rows, :].astype(F32), gq_ref, qg_ref[...])
        if latent:
            q = rope(q, cosq_ref[...], sinq_ref[...])
        q = q * (HEAD_DIM ** -0.5)
        n_q = q.shape[0]
        low_q = lax.broadcasted_iota(jnp.int32, (n_q, pair_w), 1) < HEAD_DIM
        units = []
        for j in range(KV_HEADS):
            tiles = range(j * Q_PER_KV // 2, (j + 1) * Q_PER_KV // 2)
            parts = []
            for t in tiles:
                qt = q[:, t * pair_w:(t + 1) * pair_w]
                parts += [jnp.where(low_q, qt, 0.0), jnp.where(low_q, 0.0, qt)]
            units.append(dict(j=j, rows=rows, tiles=tiles, n_q=n_q, low_q=low_q, kd=kd[j], vd=vd[j],
                              qs=jnp.concatenate(parts, axis=0).astype(BF16)))
        return units

    def scores(u):
        u["s_new"] = lax.dot_general(u["qs"], u["kd"], nt, preferred_element_type=F32)
        if latent:
            u["s_old"] = lax.dot_general(u["qs"], ckd_ref[u["j"]], nt, preferred_element_type=F32)

    def softmax(u):
        m = jnp.max(u["s_new"], axis=-1, keepdims=True)
        if latent:
            m = jnp.maximum(m, jnp.max(u["s_old"], axis=-1, keepdims=True))
        u["p_new"] = jnp.exp(u.pop("s_new") - m).astype(BF16)
        if latent:
            u["p_old"] = jnp.exp(u.pop("s_old") - m).astype(BF16)

    def weighted_values(u):
        acc = jnp.dot(u["p_new"], u["vd"], preferred_element_type=F32)
        if latent:
            acc = acc + jnp.dot(u["p_old"], cvd_ref[u["j"]], preferred_element_type=F32)
        out = acc[:, :pair_w] / acc[:, pair_w:]
        n_q = u["n_q"]
        for i, t in enumerate(u["tiles"]):
            lo_head = out[(2 * i) * n_q:(2 * i + 1) * n_q, :]
            hi_head = out[(2 * i + 1) * n_q:(2 * i + 2) * n_q, :]
            o_ref[u["rows"], t * pair_w:(t + 1) * pair_w] = jnp.where(u["low_q"], lo_head, hi_head).astype(o_ref.dtype)

    if latent:
        @pl.when(pl.program_id(1) == 0)
        def _():
            kd, vd = key_value_tiles(slice(None), None)
            for j in range(KV_HEADS):
                kd_ref[j] = kd[j]
                vd_ref[j] = vd[j]
                ckd_ref[j] = jnp.concatenate([ck_ref[j], ck_ref[j]], axis=1).astype(BF16)
                cvd = jnp.concatenate([cv_ref[j], cv_ref[j]], axis=1).astype(BF16)
                cvd_ref[j] = jnp.concatenate([cvd, jnp.ones_like(cvd)], axis=1)

        units = query_units(slice(None), [kd_ref[j] for j in range(KV_HEADS)], [vd_ref[j] for j in range(KV_HEADS)])
    else:
        seq = PROMPT_LEN
        units = []
        for s in range(q_ref.shape[0] // seq):
            rows = slice(s * seq, (s + 1) * seq)
            units += query_units(rows, *key_value_tiles(rows, s))
    for phase in (scores, softmax, weighted_values):
        for u in units:
            phase(u)


def _attn_common_args(qn_g, kn_g):
    q_w, kv_w = Q_HEADS * HEAD_DIM, KV_HEADS * HEAD_DIM
    return (jnp.tile(qn_g, Q_HEADS).reshape(1, q_w), jnp.tile(kn_g, KV_HEADS).reshape(1, kv_w),
            _head_mean_matrix(q_w), _head_mean_matrix(kv_w))


def _attention_prompt(z, qn_g, kn_g):
    seqs = 8
    L = seqs * PROMPT_LEN
    cache_shape = jax.ShapeDtypeStruct((N_PROMPT_SEQ, KV_HEADS * HEAD_DIM, PROMPT_LEN), F32)
    cache_spec = pl.BlockSpec((seqs, KV_HEADS * HEAD_DIM, PROMPT_LEN), lambda s: (s, 0, 0))
    q_w, kv_w = Q_HEADS * HEAD_DIM, KV_HEADS * HEAD_DIM
    q_col = (5 * A_WIDTH) // q_w
    k_col = (5 * A_WIDTH + q_w) // kv_w
    const = lambda r, c: pl.BlockSpec((r, c), lambda s: (0, 0))
    return pl.pallas_call(
        functools.partial(_attn_kernel, latent=False),
        grid=(N_PROMPT_TOK // L,),
        in_specs=[
            pl.BlockSpec((L, q_w), lambda s: (s, q_col)),
            pl.BlockSpec((L, kv_w), lambda s: (s, k_col)),
            pl.BlockSpec((L, kv_w), lambda s: (s, k_col + 1)),
            const(1, q_w), const(1, kv_w), const(q_w, q_w), const(kv_w, kv_w),
        ],
        out_specs=(pl.BlockSpec((L, q_w), lambda s: (s, 0)), cache_spec, cache_spec),
        out_shape=(jax.ShapeDtypeStruct((N_PROMPT_TOK, q_w), BF16), cache_shape, cache_shape),
        compiler_params=_params("arbitrary"),
        name="attn_prompt",
    )(z, z, z, *_attn_common_args(qn_g, kn_g))


def _attention_latent(z, qn_g, kn_g, cache_k, cache_v):
    L = LATENT_LEN
    nqb = L // Q_BLOCK
    q_w, kv_w = Q_HEADS * HEAD_DIM, KV_HEADS * HEAD_DIM
    q_col = (5 * A_WIDTH) // q_w
    k_col = (5 * A_WIDTH + q_w) // kv_w
    qrow0 = N_PROMPT_TOK // Q_BLOCK
    krow0 = N_PROMPT_TOK // L
    cos, sin = _rope_tables()
    cos_q, sin_q = jnp.asarray(np.tile(cos, (1, Q_HEADS))), jnp.asarray(np.tile(sin, (1, Q_HEADS)))
    cos_k, sin_k = jnp.asarray(np.tile(cos, (1, KV_HEADS))), jnp.asarray(np.tile(sin, (1, KV_HEADS)))
    const = lambda r, c: pl.BlockSpec((r, c), lambda s, b: (0, 0))
    cache_spec = pl.BlockSpec((None, None, KV_HEADS, PAST_LEN, HEAD_DIM), lambda s, b: (s, 0, 0, 0, 0))
    return pl.pallas_call(
        functools.partial(_attn_kernel, latent=True),
        grid=(N_LATENT_SEQ, nqb),
        in_specs=[
            pl.BlockSpec((Q_BLOCK, q_w), lambda s, b: (qrow0 + s * nqb + b, q_col)),
            pl.BlockSpec((L, kv_w), lambda s, b: (krow0 + s, k_col)),
            pl.BlockSpec((L, kv_w), lambda s, b: (krow0 + s, k_col + 1)),
            const(1, q_w), const(1, kv_w), const(q_w, q_w), const(kv_w, kv_w),
            pl.BlockSpec((Q_BLOCK, q_w), lambda s, b: (b, 0)),
            pl.BlockSpec((Q_BLOCK, q_w), lambda s, b: (b, 0)),
            const(L, kv_w), const(L, kv_w),
            cache_spec, cache_spec,
        ],
        out_specs=pl.BlockSpec((Q_BLOCK, q_w), lambda s, b: (s * nqb + b, 0)),
        out_shape=jax.ShapeDtypeStruct((N_LATENT_TOK, q_w), BF16),
        scratch_shapes=[pltpu.VMEM((KV_HEADS, L, kv_w), BF16), pltpu.VMEM((KV_HEADS, L, 2 * kv_w), BF16),
                        pltpu.VMEM((KV_HEADS, PAST_LEN, kv_w), BF16), pltpu.VMEM((KV_HEADS, PAST_LEN, 2 * kv_w), BF16)],
        compiler_params=_params("arbitrary", "arbitrary"),
        name="attn_latent",
    )(z, z, z, *_attn_common_args(qn_g, kn_g), cos_q, sin_q, cos_k, sin_k, cache_k, cache_v)


def _out_proj_kernel(*refs, n_act, n_x):
    a_refs = refs[:2 * n_act]
    x_refs = refs[2 * n_act:2 * n_act + n_x]
    g_ref, mod_ref, rw_ref, w_ref, xo_ref, h_ref, lg_ref, wb_ref, rws_ref, acc_ref = refs[2 * n_act + n_x:]
    _cast_once(w_ref, wb_ref)

    @pl.when(pl.program_id(0) == 0)
    def _():
        rw = rw_ref[...]
        hi = rw.astype(BF16).astype(F32)
        lo = (rw - hi).astype(BF16).astype(F32)
        rws_ref[...] = (hi + pltpu.roll(lo, N_EXPERTS, axis=1)).astype(BF16)

    mod = mod_ref[...]
    n = OUT_PROJ_SUB_ROWS

    n_sub = xo_ref.shape[0] // n

    def sub_rows(r):
        if isinstance(r, int):
            return slice(r * n, (r + 1) * n)
        return pl.ds(pl.multiple_of(r * n, n), n)

    def project(r):
        rows = sub_rows(r)
        acc = None
        k0 = 0
        for ap_ref, al_ref in zip(a_refs[0::2], a_refs[1::2]):
            k1 = k0 + ap_ref.shape[1]
            part = jnp.dot(_select_trunk(ap_ref, al_ref, rows), wb_ref[k0:k1, :], preferred_element_type=F32)
            acc = part if acc is None else acc + part
            k0 = k1
        acc_ref[r % 2] = acc

    def finish(r):
        rows = sub_rows(r)
        x_in = x_refs[0][rows, :] if n_x == 1 else _select_trunk(*x_refs, rows)
        x = x_in + mod[2:3, :] * acc_ref[r % 2]
        xo_ref[rows, :] = x
        h = _modulated_norm(x, g_ref[...], mod, 3, 4)
        h_ref[rows, :] = _pack_rows(h)
        h_hi = h.astype(BF16)
        h_lo = (h - h_hi.astype(F32)).astype(BF16)
        both = jnp.dot(jnp.concatenate([h_hi, h_lo], axis=0), rws_ref[...], preferred_element_type=F32)
        from_hi, from_lo = both[:n], both[n:]
        lg = from_hi + pltpu.roll(from_hi, ROUTER_LANES - N_EXPERTS, axis=1) + from_lo
        lg_ref[:, rows] = lg.T[:N_EXPERTS, :]

    project(0)
    for r in range(n_sub - 1):
        project(r + 1)
        finish(r)
    finish(n_sub - 1)


def _out_proj(acts, w, xs, g, mod_l, router_wp, block_rows=1024):
    tok = lambda width: pl.BlockSpec((block_rows, width), lambda i: (i, 0))
    in_specs = [spec for ap, _ in acts for spec in _trunk_specs(block_rows, ap.shape[1])]
    in_specs += [tok(D_MODEL)] if len(xs) == 1 else list(_trunk_specs(block_rows, D_MODEL))
    in_specs += [_resident((1, D_MODEL)), _mod_spec(block_rows), _resident((D_MODEL, ROUTER_LANES)),
                 _resident(w.shape)]
    return pl.pallas_call(
        functools.partial(_out_proj_kernel, n_act=len(acts), n_x=len(xs)),
        grid=(N_TOK // block_rows,),
        in_specs=in_specs,
        out_specs=(tok(D_MODEL), tok(ROW_WORDS), pl.BlockSpec((N_EXPERTS, block_rows), lambda i: (0, i))),
        out_shape=(jax.ShapeDtypeStruct((N_TOK, D_MODEL), F32),
                   jax.ShapeDtypeStruct((N_TOK, ROW_WORDS), jnp.int32),
                   jax.ShapeDtypeStruct((N_EXPERTS, N_TOK), F32)),
        scratch_shapes=[pltpu.VMEM(w.shape, BF16), pltpu.VMEM((D_MODEL, ROUTER_LANES), BF16),
                        pltpu.VMEM((2, OUT_PROJ_SUB_ROWS, D_MODEL), F32)],
        compiler_params=_params("arbitrary"),
        name="out_proj",
    )(*[a for pair in acts for a in pair], *xs, g.reshape(1, D_MODEL), mod_l, router_wp, w)


def _router_kernel(lg_ref, rb_ref, pos_ref, w_ref, plan_ref, rank_ref):
    lg = lg_ref[...]
    ex = jnp.exp(lg - jnp.max(lg, axis=0, keepdims=True))
    scores = ex / jnp.sum(ex, axis=0, keepdims=True)
    biased = scores + rb_ref[...]
    expert = lax.broadcasted_iota(jnp.int32, biased.shape, 0)
    in_pos = expert % EXPERTS_PER_GROUP
    rank = jnp.zeros_like(biased)
    for d in range(1, EXPERTS_PER_GROUP):
        wraps = in_pos + d >= EXPERTS_PER_GROUP
        partner = jnp.where(wraps, pltpu.roll(biased, EXPERTS_PER_GROUP - d, axis=0),
                            pltpu.roll(biased, N_EXPERTS - d, axis=0))
        rank = rank + jnp.where(wraps, jnp.where(partner >= biased, 1.0, 0.0), jnp.where(partner > biased, 1.0, 0.0))
    selected = rank < 1.5
    contrib = jnp.where(selected, biased, 0.0)
    group_score = []
    for gi in range(N_GROUPS):
        r = [contrib[gi * EXPERTS_PER_GROUP + i:gi * EXPERTS_PER_GROUP + i + 1, :] for i in range(EXPERTS_PER_GROUP)]
        group_score.append(((r[0] + r[1]) + r[2]) + r[3])
    best = group_score[0]
    best_group = jnp.zeros_like(best)
    for gi in range(1, N_GROUPS):
        better = group_score[gi] > best
        best_group = jnp.where(better, float(gi), best_group)
        best = jnp.where(better, group_score[gi], best)
    in_group = (expert // EXPERTS_PER_GROUP).astype(F32) == best_group
    chosen = jnp.where(selected, jnp.where(in_group, 1.0, 0.0), 0.0)
    picked = chosen * scores
    gates = picked / jnp.sum(picked, axis=0, keepdims=True)
    lanes = 128
    n_blk = N_TOK // lanes
    li = lax.broadcasted_iota(jnp.int32, (lanes, lanes), 0)
    lj = lax.broadcasted_iota(jnp.int32, (lanes, lanes), 1)
    prefix = jnp.where(li <= lj, 1.0, 0.0).astype(BF16)
    stacked = jnp.concatenate([chosen[:, blk * lanes:(blk + 1) * lanes] for blk in range(n_blk)], axis=0)
    incl_all = jnp.dot(stacked.astype(BF16), prefix, preferred_element_type=F32)
    carry = jnp.zeros((N_EXPERTS, 1), F32)
    for blk in range(n_blk):
        cols = slice(blk * lanes, (blk + 1) * lanes)
        incl = incl_all[blk * N_EXPERTS:(blk + 1) * N_EXPERTS, :]
        rank_ref[:, cols] = incl - chosen[:, cols] + carry
        carry = carry + incl[:, lanes - 1:lanes]
    count = carry
    padded = jnp.floor((count + float(MOE_TILE - 1)) * (1.0 / MOE_TILE)) * float(MOE_TILE)
    erow = lax.broadcasted_iota(jnp.int32, (N_EXPERTS, 1), 0)
    offset = jnp.zeros((N_EXPERTS, 1), F32)
    for e in range(N_EXPERTS - 1):
        offset = offset + jnp.where(erow > e, padded[e:e + 1, :], 0.0)
    position = rank_ref[...] + offset
    ei = lax.broadcasted_iota(jnp.int32, (N_EXPERTS, N_EXPERTS), 0)
    ej = lax.broadcasted_iota(jnp.int32, (N_EXPERTS, N_EXPERTS), 1)
    lower = jnp.where(ej <= ei, 1.0, 0.0).astype(BF16)
    seen = jnp.dot(lower, chosen.astype(BF16), preferred_element_type=F32)
    first = (chosen > 0.5) & (seen < 1.5)
    second = (chosen > 0.5) & (seen > 1.5)
    pick = lambda flag, x: jnp.sum(jnp.where(flag, x, 0.0), axis=0, keepdims=True)
    pos_ref[0:1, :] = pick(first, position).astype(jnp.int32)
    pos_ref[1:2, :] = pick(second, position).astype(jnp.int32)
    w_rows = jnp.concatenate([pick(first, gates), pick(second, gates), jnp.zeros((6, N_TOK), F32)], axis=0)
    ti = lax.broadcasted_iota(jnp.int32, (8, lanes), 0)
    tj = lax.broadcasted_iota(jnp.int32, (8, lanes), 1)
    eye = jnp.where(ti == tj, 1.0, 0.0).astype(BF16)
    tn = (((0,), (0,)), ((), ()))
    hi = w_rows.astype(BF16)
    r1 = w_rows - hi.astype(F32)
    mid = r1.astype(BF16)
    lo = (r1 - mid.astype(F32)).astype(BF16)
    w_cols = lax.dot_general(hi, eye, tn, preferred_element_type=F32)
    w_cols = w_cols + lax.dot_general(mid, eye, tn, preferred_element_type=F32)
    w_cols = w_cols + lax.dot_general(lo, eye, tn, preferred_element_type=F32)
    w_ref[...] = w_cols[:, :TOP_K]
    start = (lax.broadcasted_iota(jnp.int32, (N_EXPERTS, lanes), 1) * MOE_TILE).astype(F32)
    end = offset + padded
    tile_expert = jnp.sum(jnp.where(end <= start, 1.0, 0.0), axis=0, keepdims=True)
    inside = (offset <= start) & (start < end)
    real = jnp.clip(count - (start - offset), 0.0, float(MOE_TILE))
    tile_rows = jnp.sum(jnp.where(inside, real, 0.0), axis=0, keepdims=True)
    plan_ref[0:1, :] = jnp.minimum(tile_expert, float(N_EXPERTS - 1)).astype(jnp.int32)
    plan_ref[1:2, :] = tile_rows.astype(jnp.int32)


def _router(logits_t, router_b):
    whole = lambda shape: pl.BlockSpec(shape, lambda i: (0, 0))
    return pl.pallas_call(
        _router_kernel,
        grid=(1,),
        in_specs=[whole((N_EXPERTS, N_TOK)), whole((N_EXPERTS, 1))],
        out_specs=(whole((2, N_TOK)), whole((N_TOK, TOP_K)), whole((2, 128))),
        out_shape=(jax.ShapeDtypeStruct((2, N_TOK), jnp.int32),
                   jax.ShapeDtypeStruct((N_TOK, TOP_K), F32),
                   jax.ShapeDtypeStruct((2, 128), jnp.int32)),
        scratch_shapes=[pltpu.VMEM((N_EXPERTS, N_TOK), F32)],
        compiler_params=_params("arbitrary"),
        name="router",
    )(logits_t, router_b.reshape(N_EXPERTS, 1))


def _sc_mesh():
    return plsc.VectorSubcoreMesh(core_axis_name="c", subcore_axis_name="s")


def _sc_worker_base():
    return (lax.axis_index("s") * SC_CORES + lax.axis_index("c")) * SC_TOKENS_PER_WORKER


def _moe_dispatch(h, pos_a, pos_b):
    chunk = 2 * SC_CHUNK
    n_chunks = SC_TOKENS_PER_WORKER // chunk
    idx = pltpu.VMEM((chunk,), jnp.int32)

    @functools.partial(
        pl.kernel, mesh=_sc_mesh(),
        out_type=jax.ShapeDtypeStruct((MOE_ROWS, ROW_WORDS), jnp.int32),
        scratch_types=[idx, idx, idx, idx, pltpu.VMEM((2, chunk, ROW_WORDS), jnp.int32),
                       pltpu.SemaphoreType.DMA((6,)), pltpu.SemaphoreType.DMA((4,))],
        name="moe_dispatch",
    )
    def run(h_hbm, pa_hbm, pb_hbm, xs_hbm, ia0, ib0, ia1, ib1, rows_v, sem_in, sem_out):
        base = _sc_worker_base()
        ia, ib = (ia0, ia1), (ib0, ib1)

        def start_loads(c):
            slot = c % 2
            tok = pl.ds(pl.multiple_of(base + c * chunk, 8), chunk)
            return (pltpu.async_copy(pa_hbm.at[tok], ia[slot], sem_in.at[3 * slot]),
                    pltpu.async_copy(pb_hbm.at[tok], ib[slot], sem_in.at[3 * slot + 1]),
                    pltpu.async_copy(h_hbm.at[tok], rows_v.at[slot], sem_in.at[3 * slot + 2]))

        loads = start_loads(0)
        scatters = [(), ()]
        for c in range(n_chunks):
            slot = c % 2
            for cp in loads:
                cp.wait()
            if c + 1 < n_chunks:
                for cp in scatters[1 - slot]:
                    cp.wait()
                scatters[1 - slot] = ()
                loads = start_loads(c + 1)
            scatters[slot] = (pltpu.async_copy(rows_v.at[slot], xs_hbm.at[ia[slot]], sem_out.at[2 * slot]),
                              pltpu.async_copy(rows_v.at[slot], xs_hbm.at[ib[slot]], sem_out.at[2 * slot + 1]))
        for pending in scatters:
            for cp in pending:
                cp.wait()

    return run(h, pos_a, pos_b)


def _moe_collect(ys, pos_a, pos_b, tok0=0, n_tok=N_TOK):
    per_worker = n_tok // SC_WORKERS
    chunk = SC_CHUNK if per_worker % SC_CHUNK == 0 else 32
    n_chunks = per_worker // chunk
    out = jax.ShapeDtypeStruct((n_tok, ROW_WORDS), jnp.int32)
    idx = pltpu.VMEM((per_worker,), jnp.int32)
    rows = pltpu.VMEM((2, chunk, ROW_WORDS), jnp.int32)

    @functools.partial(
        pl.kernel, mesh=_sc_mesh(), out_type=(out, out),
        scratch_types=[idx, idx, rows, rows, pltpu.SemaphoreType.DMA((4,)), pltpu.SemaphoreType.DMA((4,))],
        name="moe_collect",
    )
    def run(ys_hbm, pa_hbm, pb_hbm, ya_hbm, yb_hbm, ia_v, ib_v, ra_v, rb_v, sem_g, sem_w):
        base = (lax.axis_index("s") * SC_CORES + lax.axis_index("c")) * per_worker
        mine = pl.ds(pl.multiple_of(tok0 + base, 8), per_worker)
        pltpu.sync_copy(pa_hbm.at[mine], ia_v)
        pltpu.sync_copy(pb_hbm.at[mine], ib_v)
        writes = [(), ()]
        for c in range(n_chunks):
            slot = c % 2
            for cp in writes[slot]:
                cp.wait()
            part = pl.ds(c * chunk, chunk)
            tok = pl.ds(pl.multiple_of(base + c * chunk, 8), chunk)
            ga = pltpu.async_copy(ys_hbm.at[ia_v.at[part]], ra_v.at[slot], sem_g.at[slot])
            gb = pltpu.async_copy(ys_hbm.at[ib_v.at[part]], rb_v.at[slot], sem_g.at[2 + slot])
            ga.wait()
            wa = pltpu.async_copy(ra_v.at[slot], ya_hbm.at[tok], sem_w.at[slot])
            gb.wait()
            wb = pltpu.async_copy(rb_v.at[slot], yb_hbm.at[tok], sem_w.at[2 + slot])
            writes[slot] = (wa, wb)
        for pending in writes:
            for cp in pending:
                cp.wait()

    return run(ys, pos_a, pos_b)


def _experts_kernel(plan_ref, xs_ref, wg_hbm, wu_hbm, wd_hbm, y_ref,
                    sg_ref, su_ref, sd_ref, wgb_ref, wub_ref, wdb_ref, hid_ref, sems, seg_ref, *, layer):
    n_tiles = pl.num_programs(0) * EXPERT_TILES_PER_STEP

    def weight_copies(e, slot):
        return (pltpu.make_async_copy(wg_hbm.at[layer, e], sg_ref.at[slot], sems.at[slot, 0]),
                pltpu.make_async_copy(wu_hbm.at[layer, e], su_ref.at[slot], sems.at[slot, 1]),
                pltpu.make_async_copy(wd_hbm.at[layer, e], sd_ref.at[slot], sems.at[slot, 2]))

    def tile(t, row0):
        expert = plan_ref[t]
        n_real = plan_ref[PLAN_LANES + t]
        fresh = jnp.logical_or(t == 0, expert != plan_ref[jnp.maximum(t - 1, 0)])

        @pl.when(t == 0)
        def _():
            seg_ref[0] = 0

            @pl.when(n_real > 0)
            def _():
                for cp in weight_copies(expert, 0):
                    cp.start()

        @pl.when(jnp.logical_and(n_real > 0, fresh))
        def _():
            slot = seg_ref[0] % 2
            for cp in weight_copies(expert, slot):
                cp.wait()
            wgb_ref[...] = sg_ref[slot].astype(BF16)
            wub_ref[...] = su_ref[slot].astype(BF16)
            wdb_ref[...] = sd_ref[slot].astype(BF16)
            nxt = lax.while_loop(
                lambda u: jnp.logical_and(u < n_tiles, plan_ref[jnp.minimum(u, n_tiles - 1)] == expert),
                lambda u: u + 1, t + 1)
            nxt_c = jnp.minimum(nxt, n_tiles - 1)

            @pl.when(jnp.logical_and(nxt < n_tiles, plan_ref[PLAN_LANES + nxt_c] > 0))
            def _():
                for cp in weight_copies(plan_ref[nxt_c], 1 - slot):
                    cp.start()

            seg_ref[0] = seg_ref[0] + 1

        @pl.when(n_real > 0)
        def _():
            n = EXPERT_SUB_ROWS
            n_sub = MOE_TILE // n
            row = lax.broadcasted_iota(jnp.int32, (n, xs_ref.shape[1]), 0)

            def up(r):
                rows = slice(row0 + r * n, row0 + (r + 1) * n)
                words = jnp.where(row < n_real - r * n, xs_ref[rows, :], 0)
                x = _unpack_rows(words).astype(BF16)
                a = jnp.dot(x, wgb_ref[...], preferred_element_type=F32)
                b = jnp.dot(x, wub_ref[...], preferred_element_type=F32)
                hid_ref[r] = ((a * jax.nn.sigmoid(a)) * b).astype(BF16)

            def down(r):
                rows = slice(row0 + r * n, row0 + (r + 1) * n)
                y_ref[rows, :] = _pack_rows(jnp.dot(hid_ref[r], wdb_ref[...], preferred_element_type=F32))

            up(0)
            for r in range(1, n_sub):
                up(r)
                down(r - 1)
            down(n_sub - 1)

    for q in range(EXPERT_TILES_PER_STEP):
        tile(pl.program_id(0) * EXPERT_TILES_PER_STEP + q, q * MOE_TILE)


def _experts(plan, xs, w_gate, w_up, w_down, layer):
    hbm = pl.BlockSpec(memory_space=pl.ANY)
    step_rows = MOE_TILE * EXPERT_TILES_PER_STEP
    return pl.pallas_call(
        functools.partial(_experts_kernel, layer=layer),
        grid_spec=pltpu.PrefetchScalarGridSpec(
            num_scalar_prefetch=1,
            grid=(MOE_ROWS // step_rows,),
            in_specs=[pl.BlockSpec((step_rows, ROW_WORDS), lambda j, plan: (j, 0)), hbm, hbm, hbm],
            out_specs=pl.BlockSpec((step_rows, ROW_WORDS), lambda j, plan: (j, 0)),
            scratch_shapes=[pltpu.VMEM((2, D_MODEL, D_EXPERT), F32), pltpu.VMEM((2, D_MODEL, D_EXPERT), F32),
                            pltpu.VMEM((2, D_EXPERT, D_MODEL), F32),
                            pltpu.VMEM((D_MODEL, D_EXPERT), BF16), pltpu.VMEM((D_MODEL, D_EXPERT), BF16),
                            pltpu.VMEM((D_EXPERT, D_MODEL), BF16),
                            pltpu.VMEM((MOE_TILE // EXPERT_SUB_ROWS, EXPERT_SUB_ROWS, D_EXPERT), BF16),
                            pltpu.SemaphoreType.DMA((2, 3)), pltpu.SMEM((1,), jnp.int32)],
        ),
        out_shape=jax.ShapeDtypeStruct((MOE_ROWS, ROW_WORDS), jnp.int32),
        compiler_params=_params("arbitrary"),
        name="experts",
    )(plan, xs, w_gate, w_up, w_down)


def _combine_kernel(x_ref, ya_ref, yb_ref, wt_ref, mod_ref, o_ref):
    o_ref[...] = _moe_mix(x_ref, ya_ref, yb_ref, wt_ref, mod_ref)


def _combine(x, moe_out, mod_l, tok0, n_tok, block_rows=1024):
    ya, yb, w_tok = moe_out
    b0 = tok0 // block_rows
    rows = lambda width: pl.BlockSpec((block_rows, width), lambda i: (b0 + i, 0))
    local = pl.BlockSpec((block_rows, ROW_WORDS), lambda i: (i, 0))
    return pl.pallas_call(
        _combine_kernel,
        grid=(n_tok // block_rows,),
        in_specs=[rows(D_MODEL), local, local, rows(TOP_K),
                  pl.BlockSpec((None, 6, D_MODEL), lambda i: (_cond_of_token_block(b0 + i, block_rows), 0, 0))],
        out_specs=pl.BlockSpec((block_rows, D_MODEL), lambda i: (i, 0)),
        out_shape=jax.ShapeDtypeStruct((n_tok, D_MODEL), F32),
        compiler_params=_params("arbitrary"),
        name="combine",
    )(x, ya, yb, w_tok, mod_l)


def _moe(h, logits_t, router_b, w_gate, w_up, w_down, layer, ranges=((0, N_TOK),)):
    pos, w, plan = _router(logits_t, router_b)
    xs = _moe_dispatch(h, pos[0], pos[1])
    ys = _experts(plan.reshape(-1), xs, w_gate, w_up, w_down, layer)
    return [(*_moe_collect(ys, pos[0], pos[1], tok0, n_tok), w) for tok0, n_tok in ranges]


def _dft_tables(L):
    k = np.arange(L)[:, None]
    m = np.arange(L)[None, :]
    r = (k * m) % (2 * L)
    ang = np.pi * r.astype(np.float64) / L
    fc = np.cos(ang)
    fs = np.sin(ang)
    fs[0, :] = np.where(np.arange(L) % 2 == 0, 1.0, -1.0)
    wk = np.full((L, 1), 1.0 / L)
    wk[0, 0] = 0.5 / L
    gc = (fc * wk).T
    gs = (fs * wk).T
    return [jnp.asarray(t.astype(np.float32)).astype(BF16) for t in (fc, fs, gc, gs)]


def _filter_consts(L):
    t = np.linspace(0.0, 1.0, L, dtype=np.float32)[:, None]
    w = (np.float32(2.0 * np.pi) * np.arange(L, dtype=np.float32)[:, None] / np.float32(L)).astype(np.float32)
    fb = np.linspace(1e-4, HY_BANDS - 1, HY_BANDS, dtype=np.float32)[None, :]
    emb = np.concatenate([t, np.cos(fb * w), -np.sin(fb * w)], axis=-1).astype(np.float32)
    lo = math.log(HY_DECAY_TARGET) / HY_SLOW_PCT
    hi = math.log(HY_DECAY_TARGET) / HY_FAST_PCT
    deltas = np.abs(np.linspace(lo, hi, D_MODEL, dtype=np.float32))
    decay = np.exp(-t * deltas).astype(np.float32)
    return jnp.asarray(emb), jnp.asarray(decay)


def _filter_kernel(emb_ref, w1_ref, b1_ref, w2_ref, b2_ref, fr_ref, w3f_ref, w3b_ref, dec_ref,
                   fc_ref, fs_ref, kr_ref, q_ref, krn_ref, hd_ref):
    @pl.when(pl.program_id(0) == 0)
    def _():
        fr = fr_ref[...]
        h1 = jnp.sin(fr * (jnp.dot(emb_ref[...], w1_ref[...], precision=HIGHEST,
                                   preferred_element_type=F32) + b1_ref[...]))
        hd_ref[...] = jnp.sin(fr * (jnp.dot(h1, w2_ref[...], precision=HIGHEST,
                                            preferred_element_type=F32) + b2_ref[...]))

    hd = hd_ref[...]
    dec = dec_ref[...]
    f = jnp.dot(hd, w3f_ref[...], precision=HIGHEST, preferred_element_type=F32) * dec
    g = jnp.dot(hd, w3b_ref[...], precision=HIGHEST, preferred_element_type=F32) * dec
    row = lax.broadcasted_iota(jnp.int32, f.shape, 0)
    g = jnp.where(row == 0, 0.0, g)
    s = f + g
    d = f - g
    kr = jnp.dot(fc_ref[...], s.astype(BF16), preferred_element_type=F32)
    qq = jnp.dot(fs_ref[...], d.astype(BF16), preferred_element_type=F32)
    alt = jnp.where(row % 2 == 0, 1.0, -1.0)
    nyq = jnp.sum(alt * s, axis=0, keepdims=True)
    kr_ref[...] = kr
    q_ref[...] = jnp.where(row == 0, 0.0, qq)
    krn_ref[...] = jnp.where(row == 0, nyq, kr)


def _hyena_filter_spectrum(L, w1, b1, w2, b2, w3, freq, fc, fs, cblk=256):
    emb, decay = _filter_consts(L)
    ncb = D_MODEL // cblk
    n_emb = 128
    emb = jnp.pad(emb, ((0, 0), (0, n_emb - emb.shape[1])))
    w1 = jnp.pad(w1, ((0, n_emb - w1.shape[0]), (0, 0)))
    full = lambda shape: pl.BlockSpec(shape, lambda j: tuple(0 for _ in shape))
    out_sds = jax.ShapeDtypeStruct((L, D_MODEL), F32)
    out_spec = pl.BlockSpec((L, cblk), lambda j: (0, j))
    return pl.pallas_call(
        _filter_kernel,
        grid=(ncb,),
        in_specs=[
            full((L, n_emb)), full((n_emb, HY_FFN)), full((1, HY_FFN)), full((HY_FFN, HY_FFN)),
            full((1, HY_FFN)), full((1, HY_FFN)),
            pl.BlockSpec((HY_FFN, cblk), lambda j: (0, j)),
            pl.BlockSpec((HY_FFN, cblk), lambda j: (0, ncb + j)),
            pl.BlockSpec((L, cblk), lambda j: (0, j)),
            full((L, L)), full((L, L)),
        ],
        out_specs=(out_spec, out_spec, out_spec),
        out_shape=(out_sds, out_sds, out_sds),
        scratch_shapes=[pltpu.VMEM((L, HY_FFN), F32)],
        compiler_params=_params("arbitrary"),
        name=f"hyena_filter_{L}",
    )(emb, w1, b1.reshape(1, HY_FFN), w2, b2.reshape(1, HY_FFN), freq.reshape(1, HY_FFN), w3, w3, decay, fc, fs)


def _hyena_conv_kernel(x0_ref, x1_ref, v_ref, cw0_ref, cw1_ref, cwv_ref, cb0_ref, cb1_ref, cbv_ref,
                       kr_ref, q_ref, krn_ref, ds_ref, fc_ref, fs_ref, gc_ref, gs_ref, o_ref,
                       zz_ref, gate_ref, skip_ref, yr_ref, yw_ref):
    L = fc_ref.shape[0]
    unit_w = zz_ref.shape[2]
    units = [(slice(s * L, (s + 1) * L), slice(c * unit_w, (c + 1) * unit_w))
             for s in range(x0_ref.shape[0] // L) for c in range(x0_ref.shape[1] // unit_w)]
    row = lax.broadcasted_iota(jnp.int32, (L, unit_w), 0)

    def gating(i):
        rows, cols = units[i]

        def short_conv(u_ref, w_ref, b_ref):
            u = u_ref[rows, cols].astype(F32)
            w = w_ref[:, cols]
            prev = jnp.where(row == 0, 0.0, pltpu.roll(u, 1, axis=0))
            nxt = jnp.where(row == L - 1, 0.0, pltpu.roll(u, L - 1, axis=0))
            return prev * w[0:1, :] + u * w[1:2, :] + nxt * w[2:3, :] + b_ref[:, cols]

        x0 = short_conv(x0_ref, cw0_ref, cb0_ref)
        zz = short_conv(v_ref, cwv_ref, cbv_ref) * short_conv(x1_ref, cw1_ref, cb1_ref)
        zz_ref[i] = zz.astype(BF16)
        gate_ref[i] = x0
        skip_ref[i] = x0 * zz * ds_ref[:, cols]

    def spectrum(i):
        cols = units[i][1]
        ur = jnp.dot(fc_ref[...], zz_ref[i], preferred_element_type=F32)
        p = jnp.dot(fs_ref[...], zz_ref[i], preferred_element_type=F32)
        qq = q_ref[:, cols]
        yr_ref[i] = (ur * kr_ref[:, cols] - p * qq).astype(BF16)
        yw_ref[i] = (ur * qq + p * krn_ref[:, cols]).astype(BF16)

    def synthesis(i):
        rows, cols = units[i]
        y = jnp.dot(gc_ref[...], yr_ref[i], preferred_element_type=F32)
        y = y + jnp.dot(gs_ref[...], yw_ref[i], preferred_element_type=F32)
        o_ref[rows, cols] = (gate_ref[i] * y + skip_ref[i]).astype(o_ref.dtype)

    for t in range(len(units) + 2):
        if t < len(units):
            gating(t)
        if 0 <= t - 1 < len(units):
            spectrum(t - 1)
        if 0 <= t - 2 < len(units):
            synthesis(t - 2)


def _hyena_conv(u, conv_w, conv_b, dskip, spectrum, tables, *, latent):
    L = LATENT_LEN if latent else PROMPT_LEN
    n_seq = N_LATENT_SEQ if latent else N_PROMPT_SEQ
    cblk = 512
    unit_w = 256 if latent else 512
    ncb = D_MODEL // cblk
    seqs = 1 if latent else 8
    unit = (seqs * cblk // unit_w, L, unit_w)
    row0 = (N_PROMPT_TOK // L) if latent else 0
    kr, qq, krn = spectrum
    fc, fs, gc, gs = tables

    def part(p, rows):
        if rows != L:
            return pl.BlockSpec((rows, cblk), lambda j, s: (0, p * ncb + j))
        return pl.BlockSpec((seqs * L, cblk), lambda j, s: (row0 // seqs + s, p * ncb + j))

    def const_cols(rows):
        return pl.BlockSpec((rows, cblk), lambda j, s: (0, j))

    mat = pl.BlockSpec((L, L), lambda j, s: (0, 0))
    conv_b2 = conv_b.reshape(1, 3 * D_MODEL)
    in_specs = [part(0, L), part(1, L), part(2, L),
                part(0, 3), part(1, 3), part(2, 3),
                part(0, 1), part(1, 1), part(2, 1),
                const_cols(L), const_cols(L), const_cols(L), const_cols(1),
                mat, mat, mat, mat]
    args = [u, u, u, conv_w, conv_w, conv_w, conv_b2, conv_b2, conv_b2,
            kr, qq, krn, dskip.reshape(1, D_MODEL), fc, fs, gc, gs]
    return pl.pallas_call(
        _hyena_conv_kernel,
        grid=(ncb, n_seq // seqs),
        in_specs=in_specs,
        out_specs=pl.BlockSpec((seqs * L, cblk), lambda j, s: (s, j)),
        out_shape=jax.ShapeDtypeStruct((n_seq * L, D_MODEL), BF16),
        scratch_shapes=[pltpu.VMEM(unit, BF16), pltpu.VMEM(unit, F32), pltpu.VMEM(unit, F32),
                        pltpu.VMEM(unit, BF16), pltpu.VMEM(unit, BF16)],
        compiler_params=_params("arbitrary", "arbitrary"),
        name="hyena_conv_latent" if latent else "hyena_conv_prompt",
    )(*args)


def kernel(x_prompt, x_sample, cache_k, cache_v, state_hgrn, c, c_ctx, norm_g, mod_w, mod_b, ab_in_w, hgrn_lb, hgrn_onorm_g, attn_qnorm_g, attn_knorm_g, ab_out_w, hy_in_w, hy_in_b, hy_conv_w, hy_conv_b, hy_f_w1, hy_f_b1, hy_f_w2, hy_f_b2, hy_f_w3, hy_f_freq, hy_dskip, hy_out_w, router_w, router_b, moe_w_gate, moe_w_up, moe_w_down):
    xp = x_prompt.reshape(N_PROMPT_TOK, D_MODEL)
    xl = x_sample.reshape(N_LATENT_TOK, D_MODEL)
    cond = jnp.concatenate([c_ctx[None, :], c, jnp.zeros((N_COND - 1 - N_LATENT_SEQ, D_MODEL), F32)], axis=0)
    mod = _modulation(cond, mod_w, mod_b)
    router_wp = jnp.pad(router_w, ((0, 0), (0, ROUTER_LANES - N_EXPERTS)))

    z = _in_proj0(xp, xl, norm_g[0, 0], mod[0], ab_in_w[0])
    oa_p, new_state = _hgrn(z, hgrn_lb, hgrn_onorm_g[0], None, latent=False)
    oa_l = _hgrn(z, hgrn_lb, hgrn_onorm_g[0], state_hgrn, latent=True)
    ob_p, k_fm, v_fm = _attention_prompt(z, attn_qnorm_g[0], attn_knorm_g[0])
    fm_shape = (N_PROMPT_SEQ, 1, KV_HEADS, HEAD_DIM, PROMPT_LEN)
    new_k = jnp.swapaxes(k_fm.reshape(fm_shape), -1, -2)
    new_v = jnp.swapaxes(v_fm.reshape(fm_shape), -1, -2)
    ob_l = _attention_latent(z, attn_qnorm_g[0], attn_knorm_g[0], cache_k, cache_v)
    x, h, logits_t = _out_proj([(oa_p, oa_l), (ob_p, ob_l)], ab_out_w[0], (xp, xl), norm_g[0, 1], mod[0],
                               router_wp)
    (moe_out,) = _moe(h, logits_t, router_b, moe_w_gate, moe_w_up, moe_w_down, 0)

    x, u = _in_proj1(x, moe_out, mod[0], norm_g[1, 0], mod[1], hy_in_w[0], hy_in_b[0])
    pre = []
    for latent in (False, True):
        L = LATENT_LEN if latent else PROMPT_LEN
        tables = _dft_tables(L)
        spectrum = _hyena_filter_spectrum(L, hy_f_w1[0], hy_f_b1[0], hy_f_w2[0], hy_f_b2[0], hy_f_w3[0],
                                          hy_f_freq[0], tables[0], tables[1])
        pre.append(_hyena_conv(u, hy_conv_w[0], hy_conv_b[0], hy_dskip[0], spectrum, tables, latent=latent))
    x, h, logits_t = _out_proj([tuple(pre)], hy_out_w[0], (x,), norm_g[1, 1], mod[1], router_wp)
    trunks = ((0, N_PROMPT_TOK), (N_PROMPT_TOK, N_LATENT_TOK))
    out_p, out_l = _moe(h, logits_t, router_b, moe_w_gate, moe_w_up, moe_w_down, 1, ranges=trunks)

    y_prompt = _combine(x, out_p, mod[1], *trunks[0]).reshape(N_PROMPT_SEQ, PROMPT_LEN, D_MODEL)
    y_sample = _combine(x, out_l, mod[1], *trunks[1]).reshape(N_LATENT_SEQ, LATENT_LEN, D_MODEL)
    return (y_prompt, y_sample, new_k, new_v, new_state)
```
